```python
import math
import jax, jax.numpy as jnp
from jax import lax
import numpy as np

D_MODEL = 1024
BATCH = 4
SEQ = 8192
DEPTH = 2

GRID_W = 64
N_BRANCH = 4
BRANCH_W = 256
HEAD_DIM = 64

ATT_HEADS = 4
ATT_KV_HEADS = 2
ATT_BLOCK = 128
ROPE_THETA = 10000.0
QK_EPS = 1e-6

RW_HEADS = 4
RW_DECAY_LORA = 64
RW_A_LORA = 64
RW_G_LORA = 128
RW_GN_EPS = 64e-5
RW_SIZES = (BRANCH_W, BRANCH_W, BRANCH_W, RW_DECAY_LORA, RW_DECAY_LORA, RW_A_LORA, RW_G_LORA)
RW_COLS = sum(RW_SIZES)

S5_GROUP = 16
S5_GROUPS = BRANCH_W // S5_GROUP
S5_STATE = 64

ML_HEADS = 4
ML_CHUNK = 64
ML_CONV_W = 5

IN_SIZES = (ATT_HEADS * HEAD_DIM, ATT_KV_HEADS * HEAD_DIM, ATT_KV_HEADS * HEAD_DIM, RW_COLS, BRANCH_W,
            2 * BRANCH_W, BRANCH_W, 2 * ML_HEADS, 2 * ML_HEADS, BRANCH_W)
IN_COLS = sum(IN_SIZES)

D_FF = 2816
N_EXPERTS = 8
TOP_K = 2
D_FF_EXPERT = 3584
N_DENSE = (DEPTH + 1) // 2
N_MOE = DEPTH // 2

ALPHA = (2 * DEPTH) ** 0.25
BETA = (8 * DEPTH) ** -0.25
LN_EPS = 1e-5

kernel_name = 'hybrid_gated_parallel_encoder'


def _split(t, sizes):
    return jnp.split(t, np.cumsum(sizes)[:-1].tolist(), axis=-1)


def layer_norm(x, g, b):
    xf = x.astype(jnp.float32)
    mu = jnp.mean(xf, -1, keepdims=True)
    var = jnp.mean(jnp.square(xf - mu), -1, keepdims=True)
    return ((xf - mu) * lax.rsqrt(var + LN_EPS) * g + b).astype(x.dtype)


def rms_norm_heads(x, g):
    xf = x.astype(jnp.float32)
    return xf * lax.rsqrt(jnp.mean(jnp.square(xf), -1, keepdims=True) + QK_EPS) * g


def _rotate(x, pos):
    n = x.shape[-1] // 2
    inv_freq = jnp.power(ROPE_THETA, -jnp.arange(n, dtype=jnp.float32) / n)
    ang = pos[:, None] * inv_freq
    cos = jnp.cos(ang)[None, :, None, :]
    sin = jnp.sin(ang)[None, :, None, :]
    x1, x2 = x[..., :n], x[..., n:]
    return jnp.concatenate([x1 * cos - x2 * sin, x1 * sin + x2 * cos], -1)


def rope_2d(x, row, col):
    half = x.shape[-1] // 2
    return jnp.concatenate([_rotate(x[..., :half], row), _rotate(x[..., half:], col)], -1)


def attention_mixer(q, k, v, gq, gk):
    bsz, seq, _ = q.shape
    rows = seq // GRID_W
    r_idx, c_idx = jnp.meshgrid(jnp.arange(rows), jnp.arange(GRID_W), indexing='ij')
    row = r_idx.reshape(-1).astype(jnp.float32)
    col = c_idx.reshape(-1).astype(jnp.float32)
    group = ATT_HEADS // ATT_KV_HEADS
    q = rope_2d(rms_norm_heads(q.reshape(bsz, seq, ATT_HEADS, HEAD_DIM), gq), row, col).astype(v.dtype)
    k = rope_2d(rms_norm_heads(k.reshape(bsz, seq, ATT_KV_HEADS, HEAD_DIM), gk), row, col).astype(v.dtype)
    v = v.reshape(bsz, seq, ATT_KV_HEADS, HEAD_DIM)
    n_blk = seq // ATT_BLOCK
    q_blocks = q.reshape(bsz, n_blk, ATT_BLOCK, ATT_KV_HEADS, group, HEAD_DIM).transpose(1, 0, 3, 4, 2, 5)
    k_t = k.transpose(0, 2, 1, 3)
    v_t = v.transpose(0, 2, 1, 3)
    scale = HEAD_DIM ** -0.5

    def one_block(q_blk):
        s = jnp.einsum('bkgqd,bksd->bkgqs', q_blk, k_t, preferred_element_type=jnp.float32) * scale
        p = jax.nn.softmax(s, axis=-1)
        return jnp.einsum('bkgqs,bksd->bkgqd', p.astype(v_t.dtype), v_t)

    o = lax.map(one_block, q_blocks)
    return o.transpose(1, 0, 4, 2, 3, 5).reshape(bsz, seq, ATT_HEADS * HEAD_DIM)


def centred_token_shift(p, mix):
    prev = jnp.pad(p[:, :-1], ((0, 0), (1, 0), (0, 0)))
    nxt = jnp.pad(p[:, 1:], ((0, 0), (0, 1), (0, 0)))
    return p + mix[0] * (prev - p) + mix[1] * (nxt - p)


def _rwkv7_step(state, inp):
    r_t, w_t, k_t, v_t, kk_t, kka_t = inp
    sa = jnp.einsum('bhvk,bhk->bhv', state, -kk_t)
    state = state * w_t[:, :, None, :] + sa[..., None] * kka_t[:, :, None, :] + v_t[..., None] * k_t[:, :, None, :]
    return state, jnp.einsum('bhvk,bhk->bhv', state, r_t)


def rwkv7_mixer(p, mix, w0, w2, a0, a2, g2, k_k, k_a, r_k, gn_g, gn_b):
    bsz, seq, _ = p.shape
    f32 = jnp.float32
    p = centred_token_shift(p, mix)
    r, k, v, wd_f, wd_b, a_lo, g_lo = _split(p, RW_SIZES)
    heads = lambda t: t.astype(f32).reshape(bsz, seq, RW_HEADS, HEAD_DIM)
    a = heads(jax.nn.sigmoid((a0 + a_lo @ a2).astype(f32)))
    gate = jax.nn.sigmoid(g_lo) @ g2
    r, k, v = heads(r), heads(k), heads(v)
    kk = k * k_k.astype(f32).reshape(RW_HEADS, HEAD_DIM)
    kk = kk / jnp.maximum(jnp.sqrt(jnp.sum(kk * kk, -1, keepdims=True)), 1e-12)
    k = k * (1.0 + (a - 1.0) * k_a.astype(f32).reshape(RW_HEADS, HEAD_DIM))

    def decay(w_lo, w0_d, w2_d):
        w = -jax.nn.softplus(-(w0_d + jnp.tanh(w_lo) @ w2_d).astype(f32)) - 0.5
        return heads(jnp.exp(-jnp.exp(w)))

    time_major = lambda t: jnp.moveaxis(t, 1, 0)
    r_s, k_s, v_s, kk_s, kka_s = (time_major(t) for t in (r, k, v, kk, kk * a))
    state0 = jnp.zeros((bsz, RW_HEADS, HEAD_DIM, HEAD_DIM), f32)
    _, y_f = lax.scan(_rwkv7_step, state0, (r_s, time_major(decay(wd_f, w0[0], w2[0])), k_s, v_s, kk_s, kka_s))
    _, y_b = lax.scan(_rwkv7_step, state0, (r_s, time_major(decay(wd_b, w0[1], w2[1])), k_s, v_s, kk_s, kka_s), reverse=True)
    y = jnp.moveaxis(y_f + y_b, 0, 1)
    mu = jnp.mean(y, -1, keepdims=True)
    var = jnp.mean(jnp.square(y - mu), -1, keepdims=True)
    y = ((y - mu) * lax.rsqrt(var + RW_GN_EPS)).reshape(bsz, seq, BRANCH_W) * gn_g + gn_b
    bonus = jnp.sum(r * k * r_k.astype(f32), -1, keepdims=True) * v
    y = y + bonus.reshape(bsz, seq, BRANCH_W)
    return (y * gate).astype(p.dtype)


def _ssm_combine(left, right):
    a_l, b_l = left
    a_r, b_r = right
    return a_r * a_l, a_r * b_l + b_r


def s5_mixer(u, lam_re, lam_im, log_dt, b_re, b_im, c_re, c_im, d_skip, glu_w, glu_b):
    bsz, seq, _ = u.shape
    f32 = jnp.float32
    uf = u.astype(f32)
    u_c = uf.reshape(bsz, seq, S5_GROUPS, S5_GROUP).astype(jnp.complex64)
    b_c = lax.complex(b_re.astype(f32), b_im.astype(f32))
    y = uf * d_skip.astype(f32)
    for direction in range(2):
        lam = lax.complex(jnp.minimum(lam_re[direction].astype(f32), -1e-4), lam_im[direction].astype(f32))
        dt = jnp.exp(log_dt[direction].astype(f32))[:, None]
        lam_bar = jnp.exp(lam * dt)
        b_bar = ((lam_bar - 1.0) / lam)[..., None] * b_c
        bu = jnp.einsum('bsgc,gpc->bsgp', u_c, b_bar)
        a_el = jnp.broadcast_to(lam_bar, (1, seq, S5_GROUPS, S5_STATE))
        _, states = lax.associative_scan(_ssm_combine, (a_el, bu), reverse=(direction == 1), axis=1)
        y_dir = (jnp.einsum('bsgp,gcp->bsgc', states.real, c_re[direction].astype(f32))
                 - jnp.einsum('bsgp,gcp->bsgc', states.imag, c_im[direction].astype(f32)))
        y = y + y_dir.reshape(bsz, seq, BRANCH_W)
    y = jax.nn.gelu(y)
    y = y * jax.nn.sigmoid(y @ glu_w.astype(f32) + glu_b.astype(f32))
    return y.astype(u.dtype)


def centred_depthwise_conv(x, w, b):
    width = w.shape[0]
    y = lax.conv_general_dilated(x, w.astype(x.dtype)[:, None, :], window_strides=(1,),
                                 padding=[(width // 2, width // 2)],
                                 dimension_numbers=('NWC', 'WIO', 'NWC'),
                                 feature_group_count=x.shape[-1])
    return y + b


def mlstm_chunkwise(q, k, v, log_i, log_f):
    bsz, nh, seq, d = q.shape
    nc, L = seq // ML_CHUNK, ML_CHUNK
    q, k, v = (t.reshape(bsz, nh, nc, L, d) for t in (q, k, v))
    log_i, log_f = log_i.reshape(bsz, nh, nc, L), log_f.reshape(bsz, nh, nc, L)
    b = jnp.cumsum(log_f, axis=-1)
    g = b[..., -1]
    w_end = g[..., None] - b + log_i
    m_loc = jnp.max(w_end, -1)
    e = jnp.exp(w_end - m_loc[..., None])
    u_c = jnp.einsum('bhcl,bhcld,bhcle->bhcde', e, k, v)
    u_n = jnp.einsum('bhcl,bhcld->bhcd', e, k)

    def step(carry, xs):
        c_st, n_st, m = carry
        g_c, ml_c, uc_c, un_c = xs
        m_new = jnp.maximum(g_c + m, ml_c)
        a = jnp.exp(g_c + m - m_new)
        bb = jnp.exp(ml_c - m_new)
        new = (a[..., None, None] * c_st + bb[..., None, None] * uc_c, a[..., None] * n_st + bb[..., None] * un_c, m_new)
        return new, (c_st, n_st, m)

    init = (jnp.zeros((bsz, nh, d, d), jnp.float32), jnp.zeros((bsz, nh, d), jnp.float32), jnp.zeros((bsz, nh), jnp.float32))
    chunk_major = lambda t: jnp.moveaxis(t, 2, 0)
    _, (c_prev, n_prev, m_prev) = lax.scan(step, init, (chunk_major(g), chunk_major(m_loc), chunk_major(u_c), chunk_major(u_n)))
    c_prev, n_prev, m_prev = jnp.moveaxis(c_prev, 0, 2), jnp.moveaxis(n_prev, 0, 2), jnp.moveaxis(m_prev, 0, 2)
    log_inter = b + m_prev[..., None]
    lower = jnp.tril(jnp.ones((L, L), dtype=bool))
    log_intra = jnp.where(lower, b[..., :, None] - b[..., None, :] + log_i[..., None, :], -jnp.inf)
    m_row = jnp.maximum(log_inter, jnp.max(log_intra, -1))
    s = jnp.einsum('bhcjd,bhcsd->bhcjs', q, k) * jnp.exp(log_intra - m_row[..., None])
    inter = jnp.exp(log_inter - m_row)
    num = jnp.einsum('bhcjs,bhcse->bhcje', s, v) + inter[..., None] * jnp.einsum('bhcjd,bhcde->bhcje', q, c_prev)
    den = jnp.sum(s, -1) + inter * jnp.einsum('bhcjd,bhcd->bhcj', q, n_prev)
    h = num / jnp.maximum(jnp.abs(den), jnp.exp(-m_row))[..., None]
    return h.reshape(bsz, nh, seq, d)


def mlstm_mixer(qk, v, i_pre, f_pre, o_pre, conv_w, conv_b, i_bias, f_bias):
    bsz, seq, _ = v.shape
    f32 = jnp.float32
    qk = jax.nn.silu(centred_depthwise_conv(qk, conv_w, conv_b))
    q, k = _split(qk, (BRANCH_W, BRANCH_W))
    heads = lambda t: t.astype(f32).reshape(bsz, seq, ML_HEADS, HEAD_DIM).transpose(0, 2, 1, 3)
    q, k, vh = heads(q), heads(k) * HEAD_DIM ** -0.5, heads(v)
    gate_shape = (bsz, seq, 2, ML_HEADS)
    log_i = (i_pre.astype(f32).reshape(gate_shape) + i_bias.astype(f32)).transpose(2, 0, 3, 1)
    log_f = jax.nn.log_sigmoid(f_pre.astype(f32).reshape(gate_shape) + f_bias.astype(f32)).transpose(2, 0, 3, 1)
    h_fwd = mlstm_chunkwise(q, k, vh, log_i[0], log_f[0])
    flip = lambda t: jnp.flip(t, axis=2)
    h_bwd = flip(mlstm_chunkwise(flip(q), flip(k), flip(vh), flip(log_i[1]), flip(log_f[1])))
    h = (h_fwd + h_bwd).transpose(0, 2, 1, 3).reshape(bsz, seq, BRANCH_W)
    return (jax.nn.sigmoid(o_pre.astype(f32)) * h).astype(v.dtype)


def swiglu(t, w1, w3, w2):
    return (jax.nn.silu(t @ w1) * (t @ w3)) @ w2


def moe_swiglu(x, router, w1, w3, w2):
    bsz, seq, d = x.shape
    t = x.reshape(bsz * seq, d)
    logits = (t @ router).astype(jnp.float32)
    top_val, top_idx = lax.top_k(logits, TOP_K)
    probs = jax.nn.softmax(top_val, axis=-1)
    combine = jnp.sum(jax.nn.one_hot(top_idx, N_EXPERTS, dtype=jnp.float32) * probs[..., None], axis=1)
    out = jnp.zeros((bsz * seq, d), jnp.float32)
    for e in range(N_EXPERTS):
        out = out + combine[:, e:e + 1] * swiglu(t, w1[e], w3[e], w2[e])
    return out.astype(x.dtype).reshape(bsz, seq, d)


def setup_inputs(seed: int = 0) -> dict:
    key = jax.random.key(seed)
    keys = iter(jax.random.split(key, 64))
    f32 = jnp.float32

    def nrm(shape, scale):
        return jax.random.normal(next(keys), shape, f32) * scale

    def unif(shape, lo, hi):
        return jax.random.uniform(next(keys), shape, f32, lo, hi)

    L, D, W, G, P = DEPTH, D_MODEL, BRANCH_W, S5_GROUPS, S5_STATE
    return {
        'x': nrm((BATCH, SEQ, D), 1.0),
        'w_in': nrm((L, D, IN_COLS), D ** -0.5),
        'b_in': nrm((L, IN_COLS), 0.02),
        'att_gq': 1.0 + nrm((L, HEAD_DIM), 0.02),
        'att_gk': 1.0 + nrm((L, HEAD_DIM), 0.02),
        'rw_mix': unif((L, 2, RW_COLS), 0.0, 0.5),
        'rw_w0': jnp.linspace(-6.5, -1.5, W, dtype=f32) + nrm((L, 2, W), 0.1),
        'rw_w2': nrm((L, 2, RW_DECAY_LORA, W), 0.1 * RW_DECAY_LORA ** -0.5),
        'rw_a0': nrm((L, W), 0.1),
        'rw_a2': nrm((L, RW_A_LORA, W), RW_A_LORA ** -0.5),
        'rw_g2': nrm((L, RW_G_LORA, W), RW_G_LORA ** -0.5),
        'rw_kk': 0.85 + nrm((L, W), 0.02),
        'rw_ka': 1.0 + nrm((L, W), 0.02),
        'rw_rk': nrm((L, RW_HEADS, HEAD_DIM), 0.1),
        'rw_ln_g': 1.0 + nrm((L, W), 0.02),
        'rw_ln_b': nrm((L, W), 0.02),
        's5_lam_re': -0.5 + nrm((L, 2, G, P), 0.01),
        's5_lam_im': math.pi * jnp.arange(P, dtype=f32) + nrm((L, 2, G, P), 0.01),
        's5_log_dt': unif((L, 2, G), math.log(1e-3), math.log(1e-1)),
        's5_b_re': nrm((L, G, P, S5_GROUP), (2 * S5_GROUP) ** -0.5),
        's5_b_im': nrm((L, G, P, S5_GROUP), (2 * S5_GROUP) ** -0.5),
        's5_c_re': nrm((L, 2, G, S5_GROUP, P), (2 * P) ** -0.5),
        's5_c_im': nrm((L, 2, G, S5_GROUP, P), (2 * P) ** -0.5),
        's5_d': nrm((L, W), 1.0),
        's5_glu_w': nrm((L, W, W), W ** -0.5),
        's5_glu_b': nrm((L, W), 0.02),
        'ml_conv_w': nrm((L, ML_CONV_W, 2 * W), ML_CONV_W ** -0.5),
        'ml_conv_b': nrm((L, 2 * W), 0.02),
        'ml_ib': nrm((L, 2, ML_HEADS), 0.1),
        'ml_fb': jnp.linspace(3.0, 6.0, ML_HEADS, dtype=f32) + nrm((L, 2, ML_HEADS), 0.1),
        'w_gate': nrm((L, N_BRANCH, D, D), D ** -0.5),
        'b_gate': nrm((L, N_BRANCH, D), 0.02),
        'w_branch': nrm((L, N_BRANCH, W, D), W ** -0.5),
        'w_out': nrm((L, D, D), BETA * D ** -0.5),
        'ln1_g': 1.0 + nrm((L, D), 0.02),
        'ln1_b': nrm((L, D), 0.02),
        'ffn_w1': nrm((N_DENSE, D, D_FF), D ** -0.5),
        'ffn_w3': nrm((N_DENSE, D, D_FF), D ** -0.5),
        'ffn_w2': nrm((N_DENSE, D_FF, D), BETA * D_FF ** -0.5),
        'moe_router': nrm((N_MOE, D, N_EXPERTS), D ** -0.5),
        'moe_w1': nrm((N_MOE, N_EXPERTS, D, D_FF_EXPERT), D ** -0.5),
        'moe_w3': nrm((N_MOE, N_EXPERTS, D, D_FF_EXPERT), D ** -0.5),
        'moe_w2': nrm((N_MOE, N_EXPERTS, D_FF_EXPERT, D), BETA * D_FF_EXPERT ** -0.5),
        'ln2_g': 1.0 + nrm((L, D), 0.02),
        'ln2_b': nrm((L, D), 0.02),
    }


def reference(x, w_in, b_in, att_gq, att_gk, rw_mix, rw_w0, rw_w2, rw_a0, rw_a2, rw_g2, rw_kk, rw_ka, rw_rk,
              rw_ln_g, rw_ln_b, s5_lam_re, s5_lam_im, s5_log_dt, s5_b_re, s5_b_im, s5_c_re, s5_c_im, s5_d,
              s5_glu_w, s5_glu_b, ml_conv_w, ml_conv_b, ml_ib, ml_fb, w_gate, b_gate, w_branch, w_out,
              ln1_g, ln1_b, ffn_w1, ffn_w3, ffn_w2, moe_router, moe_w1, moe_w3, moe_w2, ln2_g, ln2_b):
    for l in range(DEPTH):
        p = x @ w_in[l] + b_in[l]
        att_q, att_k, att_v, p_rw, p_s5, ml_qk, ml_v, ml_i, ml_f, ml_o = _split(p, IN_SIZES)
        branches = (
            attention_mixer(att_q, att_k, att_v, att_gq[l], att_gk[l]),
            rwkv7_mixer(p_rw, rw_mix[l], rw_w0[l], rw_w2[l], rw_a0[l], rw_a2[l], rw_g2[l], rw_kk[l], rw_ka[l],
                        rw_rk[l], rw_ln_g[l], rw_ln_b[l]),
            s5_mixer(p_s5, s5_lam_re[l], s5_lam_im[l], s5_log_dt[l], s5_b_re[l], s5_b_im[l], s5_c_re[l],
                     s5_c_im[l], s5_d[l], s5_glu_w[l], s5_glu_b[l]),
            mlstm_mixer(ml_qk, ml_v, ml_i, ml_f, ml_o, ml_conv_w[l], ml_conv_b[l], ml_ib[l], ml_fb[l]),
        )
        merged = jnp.zeros_like(x)
        for n in range(N_BRANCH):
            merged = merged + jax.nn.sigmoid(x @ w_gate[l, n] + b_gate[l, n]) * (branches[n] @ w_branch[l, n])
        x = layer_norm(ALPHA * x + merged @ w_out[l], ln1_g[l], ln1_b[l])
        if l % 2 == 0:
            ff = swiglu(x, ffn_w1[l // 2], ffn_w3[l // 2], ffn_w2[l // 2])
        else:
            ff = moe_swiglu(x, moe_router[l // 2], moe_w1[l // 2], moe_w3[l // 2], moe_w2[l // 2])
        x = layer_norm(ALPHA * x + ff, ln2_g[l], ln2_b[l])
    return x
```

```python
import functools
import math

import jax
import jax.numpy as jnp
import numpy as np
from jax import lax
from jax.experimental import pallas as pl
from jax.experimental.pallas import tpu as pltpu

f32 = jnp.float32
bf16 = jnp.bfloat16
HI = lax.Precision.HIGHEST

D_MODEL = 1024
DEPTH = 2
GRID_W = 64
BRANCH_W = 256
HEAD_DIM = 64
ATT_HEADS = 4
ATT_KV_HEADS = 2
ROPE_THETA = 10000.0
QK_EPS = 1e-6
RW_GN_EPS = 64e-5
RW_COLS = 1088
S5_GROUP = 16
S5_GROUPS = 16
S5_STATE = 64
ML_HEADS = 4
N_EXPERTS = 8
ALPHA = (2 * DEPTH) ** 0.25
LN_EPS = 1e-5

LANES = 128
CHUNK = 64
NEG = -1e30
VMEM_LIMIT = 56 * 1024 * 1024

PROJ_SPLITS = (768, 256, 512, 256, 256, RW_COLS)


def _cparams(sem):
    return pltpu.CompilerParams(dimension_semantics=sem, vmem_limit_bytes=VMEM_LIMIT)


def _sigmoid(x):
    return 1.0 / (1.0 + jnp.exp(-x))


def _softplus(x):
    return jnp.maximum(x, 0.0) + jnp.log(1.0 + jnp.exp(-jnp.abs(x)))


def _bdot(a, b):
    return jnp.dot(a.astype(bf16), b.astype(bf16), preferred_element_type=f32)


def _bdot_nt(a, b):
    return lax.dot_general(a.astype(bf16), b.astype(bf16), (((1,), (1,)), ((), ())), preferred_element_type=f32)


def _bdot_tn(a, b):
    return lax.dot_general(a.astype(bf16), b.astype(bf16), (((0,), (0,)), ((), ())), preferred_element_type=f32)


def _hdot(a, b):
    return jnp.dot(a, b, precision=HI, preferred_element_type=f32)


def _seg_matrix(n, seg=HEAD_DIM):
    r = lax.broadcasted_iota(jnp.int32, (n, n), 0) // seg
    c = lax.broadcasted_iota(jnp.int32, (n, n), 1) // seg
    return (r == c).astype(f32)


def _layer_norm(y, g, b):
    mu = jnp.mean(y, axis=-1, keepdims=True)
    d = y - mu
    var = jnp.mean(d * d, axis=-1, keepdims=True)
    return d * lax.rsqrt(var + LN_EPS) * g + b


def _row_to_col(row, eye):
    n = eye.shape[0]
    return jnp.sum(jnp.where(eye, jnp.broadcast_to(row, (n, n)), 0.0), axis=1, keepdims=True)


def _stack_heads(x):
    h0 = lax.broadcasted_iota(jnp.int32, x.shape, 1) < HEAD_DIM
    return jnp.concatenate([jnp.where(h0, x, 0.0), jnp.where(h0, 0.0, x)], axis=0)


def _proj_body(x_ref, w_ref, b_ref, wg_ref, bg_ref, att_ref, s5_ref, mqk_ref, mv_ref, mo_ref, rw_ref, g_ref):
    xb = x_ref[...].astype(bf16)
    off = 0
    for o_ref, n in zip((att_ref, s5_ref, mqk_ref, mv_ref, mo_ref, rw_ref), PROJ_SPLITS):
        o_ref[...] = jnp.dot(xb, w_ref[:, off:off + n], preferred_element_type=f32) + b_ref[:, off:off + n]
        off += n
    g_ref[...] = lax.dot_general(wg_ref[...], xb, (((1,), (1,)), ((), ())), preferred_element_type=f32) + bg_ref[...]


def _proj(xt, w, b, wg, bg, tm=512):
    T = xt.shape[0]
    n_tot = sum(PROJ_SPLITS)
    outs = [jax.ShapeDtypeStruct((T, n), f32) for n in PROJ_SPLITS] + [jax.ShapeDtypeStruct((16, T), f32)]
    return pl.pallas_call(
        _proj_body, grid=(T // tm,),
        in_specs=[pl.BlockSpec((tm, D_MODEL), lambda i: (i, 0)),
                  pl.BlockSpec((D_MODEL, n_tot), lambda i: (0, 0)),
                  pl.BlockSpec((1, n_tot), lambda i: (0, 0)),
                  pl.BlockSpec((16, D_MODEL), lambda i: (0, 0)),
                  pl.BlockSpec((16, 1), lambda i: (0, 0))],
        out_specs=[pl.BlockSpec((tm, n), lambda i: (i, 0)) for n in PROJ_SPLITS]
        + [pl.BlockSpec((16, tm), lambda i: (0, i))],
        out_shape=outs, compiler_params=_cparams(("parallel",)), name="proj")(xt, w, b, wg, bg)


def _proj_params(w_in, b_in, ml_ib, ml_fb):
    o = np.cumsum((0, 256, 128, 128, RW_COLS, 256, 512, 256, 8, 8, 256))
    sl = lambda i: (w_in[:, o[i]:o[i + 1]], b_in[o[i]:o[i + 1]])
    (wq, bq), (wk, bk), (wv, bv), (wrw, brw), (ws5, bs5), (wqk, bqk), (wmv, bmv), (wi, bi), (wf, bf), (wo, bo) = (
        sl(i) for i in range(10))
    zw, zb = jnp.zeros((D_MODEL, HEAD_DIM), f32), jnp.zeros((HEAD_DIM,), f32)
    wq_e, bq_e = [], []
    for h in range(ATT_HEADS):
        wh, bh = wq[:, 64 * h:64 * h + 64], bq[64 * h:64 * h + 64]
        wq_e += [wh, zw] if h // 2 == 0 else [zw, wh]
        bq_e += [bh, zb] if h // 2 == 0 else [zb, bh]
    w = jnp.concatenate(wq_e + [wk, wv, ws5, wqk, wmv, wo, wrw], axis=1)
    b = jnp.concatenate(bq_e + [bk, bv, bs5, bqk, bmv, bo, brw])
    wg = jnp.concatenate([wi, wf], axis=1).T
    bg = jnp.concatenate([bi + ml_ib.reshape(-1), bf + ml_fb.reshape(-1)])
    return w.astype(bf16), b[None, :], wg.astype(bf16), bg[:, None]


def _att_prep_body(a_ref, cos_ref, sin_ref, gain_ref, q_ref, k_ref, v_ref):
    x = a_ref[:, 0:640]
    ms = _hdot(x * x, _seg_matrix(640)) * (1.0 / HEAD_DIM)
    xn = x * lax.rsqrt(ms + QK_EPS) * gain_ref[...]
    lane = lax.broadcasted_iota(jnp.int32, xn.shape, 1)
    partner = jnp.where((lane % 32) < 16, pltpu.roll(xn, 640 - 16, 1), pltpu.roll(xn, 16, 1))
    cos = jnp.concatenate([cos_ref[...]] * 5, axis=1)
    sin = jnp.concatenate([sin_ref[...]] * 5, axis=1)
    rot = xn * cos + partner * sin
    q_ref[...] = rot[:, 0:512].astype(bf16)
    k_ref[...] = rot[:, 512:640].astype(bf16)
    v_ref[...] = a_ref[:, 640:768].astype(bf16)


def _att_prep(att, cos, sin, gain, B, S, tq=512):
    T = B * S
    nb = S // tq
    return pl.pallas_call(
        _att_prep_body, grid=(B, nb),
        in_specs=[pl.BlockSpec((tq, 768), lambda b, i: (b * nb + i, 0)),
                  pl.BlockSpec((tq, LANES), lambda b, i: (i, 0)),
                  pl.BlockSpec((tq, LANES), lambda b, i: (i, 0)),
                  pl.BlockSpec((1, 640), lambda b, i: (0, 0))],
        out_specs=[pl.BlockSpec((tq, 512), lambda b, i: (b * nb + i, 0)),
                   pl.BlockSpec((tq, LANES), lambda b, i: (b * nb + i, 0)),
                   pl.BlockSpec((tq, LANES), lambda b, i: (b * nb + i, 0))],
        out_shape=[jax.ShapeDtypeStruct((T, 512), bf16), jax.ShapeDtypeStruct((T, LANES), bf16),
                   jax.ShapeDtypeStruct((T, LANES), bf16)],
        compiler_params=_cparams(("parallel", "parallel")), name="att_prep")(att, cos, sin, gain)


def _rope_tables(S):
    t = np.arange(S)
    row = (t // GRID_W).astype(np.float32)
    col = (t % GRID_W).astype(np.float32)
    n = 16
    inv = np.power(np.float32(ROPE_THETA), -np.arange(n, dtype=np.float32) / n).astype(np.float32)
    ar = jnp.asarray(row)[:, None] * jnp.asarray(inv)
    ac = jnp.asarray(col)[:, None] * jnp.asarray(inv)
    cos = jnp.concatenate([jnp.cos(ar), jnp.cos(ar), jnp.cos(ac), jnp.cos(ac)], axis=1)
    sin = jnp.concatenate([-jnp.sin(ar), jnp.sin(ar), -jnp.sin(ac), jnp.sin(ac)], axis=1)
    return jnp.concatenate([cos, cos], axis=1), jnp.concatenate([sin, sin], axis=1)


def _flash_body(q_ref, k_ref, v_ref, o_ref, m_ref, l_ref, acc_ref, *, tk):
    tq = q_ref.shape[0]
    nk = k_ref.shape[0] // tk
    q2 = jnp.concatenate([q_ref[:, 0:LANES], q_ref[:, LANES:2 * LANES]], axis=0)
    m_ref[...] = jnp.full(m_ref.shape, NEG, f32)
    l_ref[...] = jnp.zeros(l_ref.shape, f32)
    acc_ref[...] = jnp.zeros(acc_ref.shape, f32)

    def step(j, carry):
        r0 = pl.multiple_of(j * tk, tk)
        kj = k_ref[pl.ds(r0, tk), :]
        vj = v_ref[pl.ds(r0, tk), :]
        s = lax.dot_general(q2, kj, (((1,), (1,)), ((), ())), preferred_element_type=f32)
        m_old = m_ref[...]
        m_new = jnp.maximum(m_old, jnp.max(s, axis=1, keepdims=True))
        alpha = jnp.exp(m_old - m_new)
        p = jnp.exp(s - m_new)
        l_ref[...] = alpha * l_ref[...] + jnp.sum(p, axis=1, keepdims=True)
        acc_ref[...] = alpha * acc_ref[...] + jnp.dot(p.astype(bf16), vj, preferred_element_type=f32)
        m_ref[...] = m_new
        return carry

    lax.fori_loop(0, nk, step, 0)
    o = acc_ref[...] / l_ref[...]
    o_ref[...] = jnp.concatenate([o[0:tq], o[tq:2 * tq]], axis=1)


def _flash(q, k, v, B, S, tq=256, tk=512):
    T = B * S
    nb = S // tq
    tk = min(tk, S)
    return pl.pallas_call(
        functools.partial(_flash_body, tk=tk), grid=(B, ATT_KV_HEADS, nb),
        in_specs=[pl.BlockSpec((tq, 2 * LANES), lambda b, g, i: (b * nb + i, g)),
                  pl.BlockSpec((S, LANES), lambda b, g, i: (b, 0)),
                  pl.BlockSpec((S, LANES), lambda b, g, i: (b, 0))],
        out_specs=pl.BlockSpec((tq, 2 * LANES), lambda b, g, i: (b * nb + i, g)),
        out_shape=jax.ShapeDtypeStruct((T, 512), f32),
        scratch_shapes=[pltpu.VMEM((2 * tq, 1), f32), pltpu.VMEM((2 * tq, 1), f32), pltpu.VMEM((2 * tq, LANES), f32)],
        compiler_params=_cparams(("parallel", "parallel", "parallel")), name="flash")(q, k, v)


def _halo_specs(width, tm, B, S):
    nb = S // tm
    r8 = tm // 8
    last8 = B * S // 8 - 1

    def main(b, i):
        return (b * nb + i, 0)

    def prev(b, i):
        return (jnp.maximum(b * (S // 8) + i * r8 - 1, 0), 0)

    def nxt(b, i):
        return (jnp.minimum(b * (S // 8) + (i + 1) * r8, last8), 0)

    return [pl.BlockSpec((tm, width), main), pl.BlockSpec((8, width), prev), pl.BlockSpec((8, width), nxt)]


def _fill_halo(buf_ref, x_ref, p_ref, n_ref):
    tm = x_ref.shape[0]
    i = pl.program_id(1)
    last = pl.num_programs(1) - 1
    buf_ref[pl.ds(8, tm), :] = x_ref[...]
    buf_ref[pl.ds(0, 8), :] = jnp.where(i > 0, p_ref[...], 0.0)
    buf_ref[pl.ds(8 + tm, 8), :] = jnp.where(i < last, n_ref[...], 0.0)


def _rw_prep_body(x_ref, p_ref, n_ref, mix_ref, w2_ref, a2_ref, g2_ref, vec_ref,
                  r_ref, k_ref, v_ref, an_ref, bn_ref, lw_ref, gate_ref, bonus_ref, buf_ref):
    tm = x_ref.shape[0]
    _fill_halo(buf_ref, x_ref, p_ref, n_ref)
    x = x_ref[...]
    p = x + mix_ref[0:1, :] * (buf_ref[pl.ds(7, tm), :] - x) + mix_ref[1:2, :] * (buf_ref[pl.ds(9, tm), :] - x)
    r, k, v = p[:, 0:256], p[:, 256:512], p[:, 512:768]
    w0f, w0b, a0, k_k, k_a, r_k = (vec_ref[j:j + 1, :] for j in range(6))
    dec = _bdot(jnp.tanh(p[:, 768:896]), w2_ref[...])
    z = p[:, 896:1088]
    a = _sigmoid(a0 + _bdot(z, a2_ref[...]))
    gate_ref[...] = _bdot(_sigmoid(z), g2_ref[...])
    seg = _seg_matrix(BRANCH_W)
    kk = k * k_k
    kk = kk / jnp.maximum(jnp.sqrt(_hdot(kk * kk, seg)), 1e-12)
    k2 = k * (1.0 + (a - 1.0) * k_a)
    bonus_ref[...] = _hdot(r * k2 * r_k, seg) * v
    r_ref[...] = r
    k_ref[...] = k2
    v_ref[...] = v
    an_ref[...] = -kk
    bn_ref[...] = kk * a
    lw_ref[0] = -jnp.exp(-_softplus(-(w0f + dec[:, 0:256])) - 0.5)
    lw_ref[1] = -jnp.exp(-_softplus(-(w0b + dec[:, 256:512])) - 0.5)


def _rw_prep(rw, mix, w2, a2, g2, vec, B, S, tm=512):
    T = B * S
    nb = S // tm
    o256 = pl.BlockSpec((tm, BRANCH_W), lambda b, i: (b * nb + i, 0))
    full = lambda shape: pl.BlockSpec(shape, lambda b, i: (0,) * len(shape))
    return pl.pallas_call(
        _rw_prep_body, grid=(B, nb),
        in_specs=_halo_specs(RW_COLS, tm, B, S) + [full((2, RW_COLS)), full((128, 512)), full((192, 256)),
                                                   full((192, 256)), full((8, 256))],
        out_specs=[o256] * 5 + [pl.BlockSpec((2, tm, BRANCH_W), lambda b, i: (0, b * nb + i, 0)), o256, o256],
        out_shape=[jax.ShapeDtypeStruct((T, BRANCH_W), f32)] * 5 + [jax.ShapeDtypeStruct((2, T, BRANCH_W), f32)]
        + [jax.ShapeDtypeStruct((T, BRANCH_W), f32)] * 2,
        scratch_shapes=[pltpu.VMEM((tm + 16, RW_COLS), f32)],
        compiler_params=_cparams(("parallel", "parallel")), name="rw_prep")(rw, rw, rw, mix, w2, a2, g2, vec)


def _pair_masks(rev):
    n = 2 * CHUNK
    r = lax.broadcasted_iota(jnp.int32, (n, n), 0)
    c = lax.broadcasted_iota(jnp.int32, (n, n), 1)
    same = (r // CHUNK) == (c // CHUNK)
    if rev:
        return r, c, same & (c > r), same & (c >= r)
    return r, c, same & (c < r), same & (c <= r)


def _rw_chunk(st, r, k, v, an, bn, lw, rev):
    L = CHUNK
    ri = lax.broadcasted_iota(jnp.int32, (L, L), 0)
    ci = lax.broadcasted_iota(jnp.int32, (L, L), 1)
    tri = ((ci >= ri) if rev else (ci <= ri)).astype(f32)
    cs = _hdot(tri, lw)
    tot = jnp.sum(lw, axis=0, keepdims=True)
    e_neg = jnp.exp(-cs)
    a2 = _stack_heads(an * jnp.exp(cs - lw))
    r2 = _stack_heads(r * jnp.exp(cs))
    b2 = _stack_heads(bn * e_neg)
    k2 = _stack_heads(k * e_neg)
    v2 = _stack_heads(v)
    rr, cc, strict, incl = _pair_masks(rev)
    mab = jnp.where(strict, _bdot_nt(a2, b2), 0.0)
    mak = jnp.where(strict, _bdot_nt(a2, k2), 0.0)
    pb = jnp.where(incl, _bdot_nt(r2, b2), 0.0)
    pk = jnp.where(incl, _bdot_nt(r2, k2), 0.0)
    eye = rr == cc
    m8 = jnp.where((rr // 8) == (cc // 8), mab, 0.0)
    x = eye.astype(f32) + m8
    p = _bdot(m8, m8)
    x = x + _bdot(x, p)
    p = _bdot(p, p)
    x = x + _bdot(x, p)
    n = 8
    while n < L:
        e = jnp.where(((rr // (2 * n)) == (cc // (2 * n))) & ((rr // n) != (cc // n)), mab, 0.0)
        x = x + _bdot(_bdot(x, e), x)
        n *= 2
    w = _bdot(x, a2)
    u0 = _bdot(x, _bdot(mak, v2))
    rh = r2 + _bdot(pb, w)
    y2 = _bdot(pb, u0) + _bdot(pk, v2) + _bdot(rh, st)
    y = y2[0:L] + y2[L:2 * L]
    gam = _row_to_col(jnp.exp(tot), eye)
    st = gam * (st + _bdot(_bdot_tn(b2, w), st) + _bdot_tn(b2, u0) + _bdot_tn(k2, v2))
    return y, st


def _rw_scan_body(r_ref, k_ref, v_ref, an_ref, bn_ref, lw_ref, y_ref, st_ref, *, rev):
    nch = r_ref.shape[0] // CHUNK

    @pl.when(pl.program_id(2) == 0)
    def _():
        st_ref[...] = jnp.zeros(st_ref.shape, f32)

    def step(cc, carry):
        c = (nch - 1 - cc) if rev else cc
        rows = pl.ds(pl.multiple_of(c * CHUNK, CHUNK), CHUNK)
        y, st = _rw_chunk(st_ref[...], r_ref[rows, :], k_ref[rows, :], v_ref[rows, :], an_ref[rows, :],
                          bn_ref[rows, :], lw_ref[0, rows, :], rev)
        y_ref[rows, :] = y
        st_ref[...] = st
        return carry

    lax.fori_loop(0, nch, step, 0)


def _rw_scan(r, k, v, an, bn, lw, B, S, rev, ts=512):
    T = B * S
    nb = S // ts
    d = 1 if rev else 0

    def blk(p, b, i):
        return b * nb + ((nb - 1 - i) if rev else i)

    io = pl.BlockSpec((ts, LANES), lambda p, b, i: (blk(p, b, i), p))
    return pl.pallas_call(
        functools.partial(_rw_scan_body, rev=rev), grid=(2, B, nb),
        in_specs=[io] * 5 + [pl.BlockSpec((1, ts, LANES), lambda p, b, i: (d, blk(p, b, i), p))],
        out_specs=io, out_shape=jax.ShapeDtypeStruct((T, BRANCH_W), f32),
        scratch_shapes=[pltpu.VMEM((LANES, LANES), f32)],
        compiler_params=_cparams(("parallel", "parallel", "arbitrary")),
        name="rw_scan_bwd" if rev else "rw_scan_fwd")(r, k, v, an, bn, lw)


def _rw_finish_body(yf_ref, yb_ref, gate_ref, bonus_ref, gn_ref, o_ref):
    y = yf_ref[...] + yb_ref[...]
    seg = _seg_matrix(BRANCH_W)
    mu = _hdot(y, seg) * (1.0 / HEAD_DIM)
    d = y - mu
    var = _hdot(d * d, seg) * (1.0 / HEAD_DIM)
    yn = d * lax.rsqrt(var + RW_GN_EPS) * gn_ref[0:1, :] + gn_ref[1:2, :]
    o_ref[...] = (yn + bonus_ref[...]) * gate_ref[...]


def _rw_finish(yf, yb, gate, bonus, gn, tm=1024):
    T = yf.shape[0]
    io = pl.BlockSpec((tm, BRANCH_W), lambda i: (i, 0))
    return pl.pallas_call(
        _rw_finish_body, grid=(T // tm,), in_specs=[io] * 4 + [pl.BlockSpec((2, BRANCH_W), lambda i: (0, 0))],
        out_specs=io, out_shape=jax.ShapeDtypeStruct((T, BRANCH_W), f32),
        compiler_params=_cparams(("parallel",)), name="rw_finish")(yf, yb, gate, bonus, gn)


def _rwkv(rw, prm, B, S):
    mix, w2, a2, g2, vec, gn = prm
    r, k, v, an, bn, lw, gate, bonus = _rw_prep(rw, mix, w2, a2, g2, vec, B, S)
    yf = _rw_scan(r, k, v, an, bn, lw, B, S, rev=False)
    yb = _rw_scan(r, k, v, an, bn, lw, B, S, rev=True)
    return _rw_finish(yf, yb, gate, bonus, gn)


def _rw_params(mix, w0, w2, a0, a2, g2, k_k, k_a, r_k, gn_g, gn_b):
    z = jnp.zeros((64, 256), f32)
    w2c = jnp.concatenate([jnp.concatenate([w2[0], z], axis=1), jnp.concatenate([z, w2[1]], axis=1)], axis=0)
    a2p = jnp.concatenate([a2, jnp.zeros((128, 256), f32)], axis=0)
    g2p = jnp.concatenate([jnp.zeros((64, 256), f32), g2], axis=0)
    vec = jnp.stack([w0[0], w0[1], a0, k_k, k_a, r_k.reshape(-1), jnp.zeros_like(a0), jnp.zeros_like(a0)])
    return mix, w2c.astype(bf16), a2p.astype(bf16), g2p.astype(bf16), vec, jnp.stack([gn_g, gn_b])


def _s5_scan_body(u_ref, bcat_ref, cre_ref, cim_ref, lre_ref, lim_ref, y_ref, sre_ref, sim_ref, bre_ref, bim_ref):
    ts = u_ref.shape[0]
    n = S5_GROUPS * S5_STATE

    @pl.when(pl.program_id(0) == 0)
    def _():
        sre_ref[...] = jnp.zeros(sre_ref.shape, f32)
        sim_ref[...] = jnp.zeros(sim_ref.shape, f32)

    u = u_ref[...].reshape(ts * 8, BRANCH_W).astype(bf16)
    fwd = lax.broadcasted_iota(jnp.int32, (ts, 8, n), 1) < 4

    def bu(part):
        return jnp.dot(u, bcat_ref[:, part * n:(part + 1) * n], preferred_element_type=f32).reshape(ts, 8, n)

    bre_ref[...] = jnp.where(fwd, bu(0), bu(2))
    bim_ref[...] = jnp.where(fwd, bu(1), bu(3))
    lre = lre_ref[...]
    lim = lim_ref[...]

    def step(t, carry):
        sre, sim = carry
        nre = lre * sre - lim * sim + bre_ref[t]
        nim = lre * sim + lim * sre + bim_ref[t]
        bre_ref[t] = nre
        bim_ref[t] = nim
        return nre, nim

    sre, sim = lax.fori_loop(0, ts, step, (sre_ref[...], sim_ref[...]), unroll=4)
    sre_ref[...] = sre
    sim_ref[...] = sim
    y = (_bdot(bre_ref[...].reshape(ts * 8, n), cre_ref[...])
         - _bdot(bim_ref[...].reshape(ts * 8, n), cim_ref[...])).reshape(ts, 8, 2 * BRANCH_W)
    fwd_o = lax.broadcasted_iota(jnp.int32, (ts, 8, BRANCH_W), 1) < 4
    y_ref[...] = jnp.where(fwd_o, y[:, :, 0:BRANCH_W], y[:, :, BRANCH_W:2 * BRANCH_W])


def _s5_scan(u2, bcat, cre, cim, lre, lim, ts=128):
    S = u2.shape[0]
    n = S5_GROUPS * S5_STATE
    full = lambda shape: pl.BlockSpec(shape, lambda i: (0,) * len(shape))
    return pl.pallas_call(
        _s5_scan_body, grid=(S // ts,),
        in_specs=[pl.BlockSpec((ts, 8, BRANCH_W), lambda i: (i, 0, 0)), full((BRANCH_W, 4 * n)),
                  full((n, 2 * BRANCH_W)), full((n, 2 * BRANCH_W)), full((8, n)), full((8, n))],
        out_specs=pl.BlockSpec((ts, 8, BRANCH_W), lambda i: (i, 0, 0)),
        out_shape=jax.ShapeDtypeStruct((S, 8, BRANCH_W), f32),
        scratch_shapes=[pltpu.VMEM((8, n), f32), pltpu.VMEM((8, n), f32),
                        pltpu.VMEM((ts, 8, n), f32), pltpu.VMEM((ts, 8, n), f32)],
        compiler_params=_cparams(("arbitrary",)), name="s5_scan")(u2, bcat, cre, cim, lre, lim)


def _s5_finish_body(y_ref, u_ref, d_ref, w_ref, b_ref, o_ref):
    y = y_ref[...] + u_ref[...] * d_ref[...]
    y = 0.5 * y * (1.0 + jnp.tanh(math.sqrt(2.0 / math.pi) * (y + 0.044715 * (y * y * y))))
    o_ref[...] = y * _sigmoid(_bdot(y, w_ref[...]) + b_ref[...])


def _s5_finish(y, u, d, w, b, tm=1024):
    T = y.shape[0]
    io = pl.BlockSpec((tm, BRANCH_W), lambda i: (i, 0))
    vec = pl.BlockSpec((1, BRANCH_W), lambda i: (0, 0))
    return pl.pallas_call(
        _s5_finish_body, grid=(T // tm,),
        in_specs=[io, io, vec, pl.BlockSpec((BRANCH_W, BRANCH_W), lambda i: (0, 0)), vec],
        out_specs=io, out_shape=jax.ShapeDtypeStruct((T, BRANCH_W), f32),
        compiler_params=_cparams(("parallel",)), name="s5_finish")(y, u, d, w, b)


def _s5_params(lam_re, lam_im, log_dt, b_re, b_im, c_re, c_im):
    G, P, C = S5_GROUPS, S5_STATE, S5_GROUP
    eye = jnp.eye(G, dtype=f32)
    b_c = lax.complex(b_re, b_im)
    bcat, cre, cim, lre, lim = [], [], [], [], []
    for d in range(2):
        lam = lax.complex(jnp.minimum(lam_re[d], -1e-4), lam_im[d])
        lam_bar = jnp.exp(lam * jnp.exp(log_dt[d])[:, None])
        b_bar = ((lam_bar - 1.0) / lam)[..., None] * b_c
        for part in (jnp.real(b_bar), jnp.imag(b_bar)):
            bcat.append(jnp.einsum('gh,gpc->gchp', eye, part).reshape(G * C, G * P))
        cre.append(jnp.einsum('gh,gcp->gphc', eye, c_re[d]).reshape(G * P, G * C))
        cim.append(jnp.einsum('gh,gcp->gphc', eye, c_im[d]).reshape(G * P, G * C))
        lre.append(jnp.broadcast_to(jnp.real(lam_bar).reshape(1, G * P), (4, G * P)))
        lim.append(jnp.broadcast_to(jnp.imag(lam_bar).reshape(1, G * P), (4, G * P)))
    return (jnp.concatenate(bcat, axis=1).astype(bf16), jnp.concatenate(cre, axis=1).astype(bf16),
            jnp.concatenate(cim, axis=1).astype(bf16), jnp.concatenate(lre, axis=0), jnp.concatenate(lim, axis=0))


def _s5(u, prm, fin, B, S):
    u3 = u.reshape(B, S, BRANCH_W).transpose(1, 0, 2)
    u2 = jnp.concatenate([u3, u3[::-1]], axis=1)
    y2 = _s5_scan(u2, *prm)
    y = (y2[:, 0:B] + y2[::-1, B:2 * B]).transpose(1, 0, 2).reshape(B * S, BRANCH_W)
    return _s5_finish(y, u, *fin)


def _ml_prep_body(x_ref, p_ref, n_ref, w_ref, b_ref, q_ref, k_ref, buf_ref):
    tm = x_ref.shape[0]
    _fill_halo(buf_ref, x_ref, p_ref, n_ref)
    y = b_ref[...] + w_ref[2:3, :] * x_ref[...]
    for j in (0, 1, 3, 4):
        y = y + w_ref[j:j + 1, :] * buf_ref[pl.ds(6 + j, tm), :]
    y = y * _sigmoid(y)
    q_ref[...] = y[:, 0:BRANCH_W]
    k_ref[...] = y[:, BRANCH_W:2 * BRANCH_W] * (HEAD_DIM ** -0.5)


def _ml_prep(mqk, w, b, B, S, tm=512):
    T = B * S
    nb = S // tm
    o = pl.BlockSpec((tm, BRANCH_W), lambda b_, i: (b_ * nb + i, 0))
    return pl.pallas_call(
        _ml_prep_body, grid=(B, nb),
        in_specs=_halo_specs(512, tm, B, S) + [pl.BlockSpec((5, 512), lambda b_, i: (0, 0)),
                                               pl.BlockSpec((1, 512), lambda b_, i: (0, 0))],
        out_specs=[o, o], out_shape=[jax.ShapeDtypeStruct((T, BRANCH_W), f32)] * 2,
        scratch_shapes=[pltpu.VMEM((tm + 16, 512), f32)],
        compiler_params=_cparams(("parallel", "parallel")), name="ml_prep")(mqk, mqk, mqk, w, b)


def _ml_chunk(state, q, k, v, li, lfp, rev):
    c2, n_row, m_row = state
    L = CHUNK
    rr, cc, _, incl = _pair_masks(rev)
    same = (rr // L) == (cc // L)
    eye = rr == cc
    lane = lax.broadcasted_iota(jnp.int32, (1, 2 * L), 1)
    lf = jnp.minimum(lfp, 0.0) - jnp.log(1.0 + jnp.exp(-jnp.abs(lfp)))
    lf8 = jnp.broadcast_to(lf, (8, 2 * L))
    cum = (same & ((rr >= cc) if rev else (rr <= cc))).astype(f32)
    b_row = _hdot(lf8, cum)[0:1]
    g_row = _hdot(lf8, same.astype(f32))[0:1]
    w_end = g_row - b_row + li
    m0 = jnp.max(jnp.where(lane < L, w_end, NEG), axis=1, keepdims=True)
    m1 = jnp.max(jnp.where(lane < L, NEG, w_end), axis=1, keepdims=True)
    m_loc = jnp.where(lane < L, m0, m1)
    e_col = _row_to_col(jnp.exp(w_end - m_loc), eye)
    b_col = _row_to_col(b_row, eye)
    q2, k2, v2 = _stack_heads(q), _stack_heads(k), _stack_heads(v)
    log_inter = b_col + _row_to_col(m_row, eye)
    log_intra = jnp.where(incl, b_col - b_row + li, NEG)
    m_r = jnp.maximum(log_inter, jnp.max(log_intra, axis=1, keepdims=True))
    s = _bdot_nt(q2, k2) * jnp.exp(log_intra - m_r)
    inter = jnp.exp(log_inter - m_r)
    num = _bdot(s, v2) + inter * _bdot(q2, c2)
    den = jnp.sum(s, axis=1, keepdims=True) + inter * jnp.sum(q2 * n_row, axis=1, keepdims=True)
    h2 = num / jnp.maximum(jnp.abs(den), jnp.exp(-m_r))
    h = h2[0:L] + h2[L:2 * L]
    m_new = jnp.maximum(g_row + m_row, m_loc)
    a = jnp.exp(g_row + m_row - m_new)
    bb = jnp.exp(m_loc - m_new)
    ek = e_col * k2
    c2 = a * c2 + bb * _bdot_tn(ek, v2)
    n_row = a * n_row + bb * jnp.sum(ek, axis=0, keepdims=True)
    return h, (c2, n_row, m_new)


def _ml_scan_body(q_ref, k_ref, v_ref, g_ref, h_ref, c_ref, n_ref, m_ref, *, rev):
    nch = q_ref.shape[0] // CHUNK

    @pl.when(pl.program_id(2) == 0)
    def _():
        c_ref[...] = jnp.zeros(c_ref.shape, f32)
        n_ref[...] = jnp.zeros(n_ref.shape, f32)
        m_ref[...] = jnp.zeros(m_ref.shape, f32)

    def step(cc, carry):
        c = (nch - 1 - cc) if rev else cc
        rows = pl.ds(pl.multiple_of(c * CHUNK, CHUNK), CHUNK)
        h, (c2, n_row, m_row) = _ml_chunk((c_ref[...], n_ref[...], m_ref[...]), q_ref[rows, :], k_ref[rows, :],
                                          v_ref[rows, :], g_ref[0, 0, 0, pl.ds(c, 1), :], g_ref[1, 0, 0, pl.ds(c, 1), :], rev)
        h_ref[rows, :] = h
        c_ref[...] = c2
        n_ref[...] = n_row
        m_ref[...] = m_row
        return carry

    lax.fori_loop(0, nch, step, 0)


def _ml_scan(q, k, v, g, B, S, rev, ts=512):
    T = B * S
    nb = S // ts
    nch = ts // CHUNK
    d = 1 if rev else 0

    def blk(b, i):
        return b * nb + ((nb - 1 - i) if rev else i)

    io = pl.BlockSpec((ts, LANES), lambda p, b, i: (blk(b, i), p))
    return pl.pallas_call(
        functools.partial(_ml_scan_body, rev=rev), grid=(2, B, nb),
        in_specs=[io, io, io, pl.BlockSpec((2, 1, 1, nch, LANES), lambda p, b, i: (0, d, p, blk(b, i), 0))],
        out_specs=io, out_shape=jax.ShapeDtypeStruct((T, BRANCH_W), f32),
        scratch_shapes=[pltpu.VMEM((LANES, LANES), f32), pltpu.VMEM((1, LANES), f32), pltpu.VMEM((1, LANES), f32)],
        compiler_params=_cparams(("parallel", "parallel", "arbitrary")),
        name="ml_scan_bwd" if rev else "ml_scan_fwd")(q, k, v, g)


def _mlstm(mqk, mv, gt, conv_w, conv_b, B, S):
    q, k = _ml_prep(mqk, conv_w, conv_b, B, S)
    g = gt.reshape(2, 2, 2, 2, B * S // CHUNK, CHUNK).transpose(0, 1, 2, 4, 3, 5).reshape(2, 2, 2, B * S // CHUNK, LANES)
    return _ml_scan(q, k, mv, g, B, S, rev=False), _ml_scan(q, k, mv, g, B, S, rev=True)


def _merge_body(x_ref, att_ref, rw_ref, s5_ref, hf_ref, hb_ref, mo_ref, wg_ref, bg_ref, wba_ref, wb_ref, wo_ref,
                ln_ref, o_ref):
    x = x_ref[...]
    xb = x.astype(bf16)
    ml = _sigmoid(mo_ref[...]) * (hf_ref[...] + hb_ref[...])
    branches = (att_ref[...], rw_ref[...], s5_ref[...], ml)
    merged = None
    for n in range(4):
        gate = _sigmoid(jnp.dot(xb, wg_ref[n], preferred_element_type=f32) + bg_ref[n:n + 1, :])
        wide = _bdot(branches[n], wba_ref[...] if n == 0 else wb_ref[n - 1])
        merged = gate * wide if merged is None else merged + gate * wide
    y = ALPHA * x + _bdot(merged, wo_ref[...])
    o_ref[...] = _layer_norm(y, ln_ref[0:1, :], ln_ref[1:2, :])


def _merge(xt, att, rw, s5, hf, hb, mo, wg, bg, wba, wb, wo, ln, tm=256):
    T = xt.shape[0]
    row = lambda n: pl.BlockSpec((tm, n), lambda i: (i, 0))
    const = lambda shape: pl.BlockSpec(shape, lambda i: (0,) * len(shape), pipeline_mode=pl.Buffered(1))
    return pl.pallas_call(
        _merge_body, grid=(T // tm,),
        in_specs=[row(D_MODEL), row(512)] + [row(BRANCH_W)] * 5
        + [const((4, D_MODEL, D_MODEL)), const((4, D_MODEL)), const((512, D_MODEL)), const((3, BRANCH_W, D_MODEL)),
           const((D_MODEL, D_MODEL)), const((2, D_MODEL))],
        out_specs=row(D_MODEL), out_shape=jax.ShapeDtypeStruct((T, D_MODEL), f32),
        compiler_params=_cparams(("parallel",)), name="merge")(xt, att, rw, s5, hf, hb, mo, wg, bg, wba, wb, wo, ln)


def _att_branch_weight(wb):
    z = jnp.zeros((HEAD_DIM, D_MODEL), f32)
    parts = []
    for h in range(ATT_HEADS):
        wh = wb[64 * h:64 * h + 64]
        parts += [wh, z] if h // 2 == 0 else [z, wh]
    return jnp.concatenate(parts, axis=0)


def _ffn_body(x_ref, w1_ref, w3_ref, w2_ref, ln_ref, o_ref, acc_ref):
    j = pl.program_id(1)
    xb = x_ref[...].astype(bf16)
    h1 = jnp.dot(xb, w1_ref[...], preferred_element_type=f32)
    h3 = jnp.dot(xb, w3_ref[...], preferred_element_type=f32)
    part = _bdot(h1 * _sigmoid(h1) * h3, w2_ref[...])

    @pl.when(j == 0)
    def _():
        acc_ref[...] = part

    @pl.when(j > 0)
    def _():
        acc_ref[...] += part

    @pl.when(j == pl.num_programs(1) - 1)
    def _():
        o_ref[...] = _layer_norm(ALPHA * x_ref[...] + acc_ref[...], ln_ref[0:1, :], ln_ref[1:2, :])


def _ffn(xt, w1, w3, w2, ln, tm=512, tf=1408):
    T = xt.shape[0]
    dff = w1.shape[1]
    return pl.pallas_call(
        _ffn_body, grid=(T // tm, dff // tf),
        in_specs=[pl.BlockSpec((tm, D_MODEL), lambda i, j: (i, 0)), pl.BlockSpec((D_MODEL, tf), lambda i, j: (0, j)),
                  pl.BlockSpec((D_MODEL, tf), lambda i, j: (0, j)), pl.BlockSpec((tf, D_MODEL), lambda i, j: (j, 0)),
                  pl.BlockSpec((2, D_MODEL), lambda i, j: (0, 0))],
        out_specs=pl.BlockSpec((tm, D_MODEL), lambda i, j: (i, 0)),
        out_shape=jax.ShapeDtypeStruct((T, D_MODEL), f32), scratch_shapes=[pltpu.VMEM((tm, D_MODEL), f32)],
        compiler_params=_cparams(("parallel", "arbitrary")), name="ffn")(xt, w1, w3, w2, ln)


def _moe_body(x_ref, rt_ref, w1_ref, w3_ref, w2_ref, ln_ref, o_ref, acc_ref, comb_ref):
    e = pl.program_id(1)
    j = pl.program_id(2)
    first = (e == 0) & (j == 0)

    @pl.when(first)
    def _():
        logits = _hdot(x_ref[...], rt_ref[...])
        lane = lax.broadcasted_iota(jnp.int32, logits.shape, 1)
        lg = jnp.where(lane < N_EXPERTS, logits, NEG)
        v1 = jnp.max(lg, axis=1, keepdims=True)
        i1 = jnp.min(jnp.where(lg == v1, lane, LANES), axis=1, keepdims=True)
        lg2 = jnp.where(lane == i1, NEG, lg)
        v2 = jnp.max(lg2, axis=1, keepdims=True)
        i2 = jnp.min(jnp.where(lg2 == v2, lane, LANES), axis=1, keepdims=True)
        e2 = jnp.exp(v2 - v1)
        comb_ref[...] = jnp.where(lane == i1, 1.0 / (1.0 + e2), 0.0) + jnp.where(lane == i2, e2 / (1.0 + e2), 0.0)
        acc_ref[...] = jnp.zeros(acc_ref.shape, f32)

    xb = x_ref[...].astype(bf16)
    h1 = jnp.dot(xb, w1_ref[0], preferred_element_type=f32)
    h3 = jnp.dot(xb, w3_ref[0], preferred_element_type=f32)
    lane = lax.broadcasted_iota(jnp.int32, comb_ref.shape, 1)
    ce = jnp.sum(jnp.where(lane == e, comb_ref[...], 0.0), axis=1, keepdims=True)
    acc_ref[...] += ce * _bdot(h1 * _sigmoid(h1) * h3, w2_ref[0])

    @pl.when((e == pl.num_programs(1) - 1) & (j == pl.num_programs(2) - 1))
    def _():
        o_ref[...] = _layer_norm(ALPHA * x_ref[...] + acc_ref[...], ln_ref[0:1, :], ln_ref[1:2, :])


def _moe(xt, router, w1, w3, w2, ln, tm=1024, tf=512):
    T = xt.shape[0]
    tm = min(tm, T)
    dff = w1.shape[2]
    return pl.pallas_call(
        _moe_body, grid=(T // tm, N_EXPERTS, dff // tf),
        in_specs=[pl.BlockSpec((tm, D_MODEL), lambda i, e, j: (i, 0)),
                  pl.BlockSpec((D_MODEL, LANES), lambda i, e, j: (0, 0)),
                  pl.BlockSpec((1, D_MODEL, tf), lambda i, e, j: (e, 0, j)),
                  pl.BlockSpec((1, D_MODEL, tf), lambda i, e, j: (e, 0, j)),
                  pl.BlockSpec((1, tf, D_MODEL), lambda i, e, j: (e, j, 0)),
                  pl.BlockSpec((2, D_MODEL), lambda i, e, j: (0, 0))],
        out_specs=pl.BlockSpec((tm, D_MODEL), lambda i, e, j: (i, 0)),
        out_shape=jax.ShapeDtypeStruct((T, D_MODEL), f32),
        scratch_shapes=[pltpu.VMEM((tm, D_MODEL), f32), pltpu.VMEM((tm, LANES), f32)],
        compiler_params=_cparams(("parallel", "arbitrary", "arbitrary")), name="moe")(xt, router, w1, w3, w2, ln)


def kernel(x, w_in, b_in, att_gq, att_gk, rw_mix, rw_w0, rw_w2, rw_a0, rw_a2, rw_g2, rw_kk, rw_ka, rw_rk, rw_ln_g, rw_ln_b, s5_lam_re, s5_lam_im, s5_log_dt, s5_b_re, s5_b_im, s5_c_re, s5_c_im, s5_d, s5_glu_w, s5_glu_b, ml_conv_w, ml_conv_b, ml_ib, ml_fb, w_gate, b_gate, w_branch, w_out, ln1_g, ln1_b, ffn_w1, ffn_w3, ffn_w2, moe_router, moe_w1, moe_w3, moe_w2, ln2_g, ln2_b):
    B, S, D = x.shape
    xt = x.reshape(B * S, D)
    cos, sin = _rope_tables(S)
    for l in range(DEPTH):
        att, s5u, mqk, mv, mo, rw, gt = _proj(xt, *_proj_params(w_in[l], b_in[l], ml_ib[l], ml_fb[l]))
        gain = jnp.concatenate([jnp.tile(att_gq[l], 8) * (HEAD_DIM ** -0.5), jnp.tile(att_gk[l], 2)])[None, :]
        q, k, v = _att_prep(att, cos, sin, gain, B, S)
        o_att = _flash(q, k, v, B, S)
        o_rw = _rwkv(rw, _rw_params(rw_mix[l], rw_w0[l], rw_w2[l], rw_a0[l], rw_a2[l], rw_g2[l], rw_kk[l], rw_ka[l],
                                    rw_rk[l], rw_ln_g[l], rw_ln_b[l]), B, S)
        o_s5 = _s5(s5u, _s5_params(s5_lam_re[l], s5_lam_im[l], s5_log_dt[l], s5_b_re[l], s5_b_im[l], s5_c_re[l],
                                   s5_c_im[l]),
                   (s5_d[l][None, :], s5_glu_w[l].astype(bf16), s5_glu_b[l][None, :]), B, S)
        hf, hb = _mlstm(mqk, mv, gt, ml_conv_w[l], ml_conv_b[l][None, :], B, S)
        xt = _merge(xt, o_att, o_rw, o_s5, hf, hb, mo, w_gate[l].astype(bf16), b_gate[l],
                    _att_branch_weight(w_branch[l, 0]).astype(bf16), w_branch[l, 1:].astype(bf16),
                    w_out[l].astype(bf16), jnp.stack([ln1_g[l], ln1_b[l]]))
        ln2 = jnp.stack([ln2_g[l], ln2_b[l]])
        if l % 2 == 0:
            xt = _ffn(xt, ffn_w1[l // 2].astype(bf16), ffn_w3[l // 2].astype(bf16), ffn_w2[l // 2].astype(bf16), ln2)
        else:
            router = jnp.pad(moe_router[l // 2], ((0, 0), (0, LANES - N_EXPERTS)))
            xt = _moe(xt, router, moe_w1[l // 2].astype(bf16), moe_w3[l // 2].astype(bf16),
                      moe_w2[l // 2].astype(bf16), ln2)
    return xt.reshape(B, S, D)
```

```python
import functools
import math

import jax
import jax.numpy as jnp
import numpy as np
from jax import lax
from jax.experimental import pallas as pl
from jax.experimental.pallas import tpu as pltpu

f32 = jnp.float32
bf16 = jnp.bfloat16
HI = lax.Precision.HIGHEST

D_MODEL = 1024
DEPTH = 2
GRID_W = 64
BRANCH_W = 256
HEAD_DIM = 64
ATT_HEADS = 4
ATT_KV_HEADS = 2
ROPE_THETA = 10000.0
QK_EPS = 1e-6
RW_GN_EPS = 64e-5
RW_COLS = 1088
S5_GROUP = 16
S5_GROUPS = 16
S5_STATE = 64
ML_HEADS = 4
N_EXPERTS = 8
ALPHA = (2 * DEPTH) ** 0.25
LN_EPS = 1e-5

LANES = 128
CHUNK = 64
NEG = -1e30
VMEM_LIMIT = 56 * 1024 * 1024

PROJ_SPLITS = (768, 256, 512, 256, 256, RW_COLS)


def _cparams(sem):
    return pltpu.CompilerParams(dimension_semantics=sem, vmem_limit_bytes=VMEM_LIMIT)


def _sigmoid(x):
    return 1.0 / (1.0 + jnp.exp(-x))


def _softplus(x):
    return jnp.maximum(x, 0.0) + jnp.log(1.0 + jnp.exp(-jnp.abs(x)))


def _dims(a, lhs_c, rhs_c):
    lead = a.ndim - 2
    batch = tuple(range(lead))
    return (((lhs_c + lead,), (rhs_c + lead,)), (batch, batch))


def _bdot(a, b):
    return lax.dot_general(a.astype(bf16), b.astype(bf16), _dims(a, 1, 0), preferred_element_type=f32)


def _bdot_nt(a, b):
    return lax.dot_general(a.astype(bf16), b.astype(bf16), _dims(a, 1, 1), preferred_element_type=f32)


def _bdot_tn(a, b):
    return lax.dot_general(a.astype(bf16), b.astype(bf16), _dims(a, 0, 0), preferred_element_type=f32)


def _hdot(a, b):
    return lax.dot_general(a, b, _dims(a, 1, 0), precision=HI, preferred_element_type=f32)


def _seg_matrix(n, seg=HEAD_DIM):
    r = lax.broadcasted_iota(jnp.int32, (n, n), 0) // seg
    c = lax.broadcasted_iota(jnp.int32, (n, n), 1) // seg
    return (r == c).astype(f32)


def _layer_norm(y, g, b):
    mu = jnp.mean(y, axis=-1, keepdims=True)
    d = y - mu
    var = jnp.mean(d * d, axis=-1, keepdims=True)
    return d * lax.rsqrt(var + LN_EPS) * g + b


def _row_to_col(row, eye):
    return jnp.sum(jnp.where(eye, jnp.broadcast_to(row, eye.shape), 0.0), axis=2, keepdims=True)


def _stack_heads(x):
    h0 = lax.broadcasted_iota(jnp.int32, x.shape, 2) < HEAD_DIM
    return jnp.concatenate([jnp.where(h0, x, 0.0), jnp.where(h0, 0.0, x)], axis=1)


def _proj_body(x_ref, w_ref, b_ref, wg_ref, bg_ref, att_ref, s5_ref, mqk_ref, mv_ref, mo_ref, rw_ref, g_ref):
    xb = x_ref[...].astype(bf16)
    off = 0
    for o_ref, n in zip((att_ref, s5_ref, mqk_ref, mv_ref, mo_ref, rw_ref), PROJ_SPLITS):
        o_ref[...] = jnp.dot(xb, w_ref[:, off:off + n], preferred_element_type=f32) + b_ref[:, off:off + n]
        off += n
    g_ref[...] = lax.dot_general(wg_ref[...], xb, (((1,), (1,)), ((), ())), preferred_element_type=f32) + bg_ref[...]


def _proj(xt, w, b, wg, bg, tm=512):
    T = xt.shape[0]
    n_tot = sum(PROJ_SPLITS)
    outs = [jax.ShapeDtypeStruct((T, n), f32) for n in PROJ_SPLITS] + [jax.ShapeDtypeStruct((16, T), f32)]
    return pl.pallas_call(
        _proj_body, grid=(T // tm,),
        in_specs=[pl.BlockSpec((tm, D_MODEL), lambda i: (i, 0)),
                  pl.BlockSpec((D_MODEL, n_tot), lambda i: (0, 0)),
                  pl.BlockSpec((1, n_tot), lambda i: (0, 0)),
                  pl.BlockSpec((16, D_MODEL), lambda i: (0, 0)),
                  pl.BlockSpec((16, 1), lambda i: (0, 0))],
        out_specs=[pl.BlockSpec((tm, n), lambda i: (i, 0)) for n in PROJ_SPLITS]
        + [pl.BlockSpec((16, tm), lambda i: (0, i))],
        out_shape=outs, compiler_params=_cparams(("parallel",)), name="proj")(xt, w, b, wg, bg)


def _proj_params(w_in, b_in, ml_ib, ml_fb):
    o = np.cumsum((0, 256, 128, 128, RW_COLS, 256, 512, 256, 8, 8, 256))
    sl = lambda i: (w_in[:, o[i]:o[i + 1]], b_in[o[i]:o[i + 1]])
    (wq, bq), (wk, bk), (wv, bv), (wrw, brw), (ws5, bs5), (wqk, bqk), (wmv, bmv), (wi, bi), (wf, bf), (wo, bo) = (
        sl(i) for i in range(10))
    zw, zb = jnp.zeros((D_MODEL, HEAD_DIM), f32), jnp.zeros((HEAD_DIM,), f32)
    wq_e, bq_e = [], []
    for h in range(ATT_HEADS):
        wh, bh = wq[:, 64 * h:64 * h + 64], bq[64 * h:64 * h + 64]
        wq_e += [wh, zw] if h // 2 == 0 else [zw, wh]
        bq_e += [bh, zb] if h // 2 == 0 else [zb, bh]
    w = jnp.concatenate(wq_e + [wk, wv, ws5, wqk, wmv, wo, wrw], axis=1)
    b = jnp.concatenate(bq_e + [bk, bv, bs5, bqk, bmv, bo, brw])
    wg = jnp.concatenate([wi, wf], axis=1).T
    bg = jnp.concatenate([bi + ml_ib.reshape(-1), bf + ml_fb.reshape(-1)])
    return w.astype(bf16), b[None, :], wg.astype(bf16), bg[:, None]


def _att_prep_body(a_ref, cos_ref, sin_ref, gain_ref, q_ref, k_ref, v_ref):
    x = a_ref[:, 0:640]
    ms = _hdot(x * x, _seg_matrix(640)) * (1.0 / HEAD_DIM)
    xn = x * lax.rsqrt(ms + QK_EPS) * gain_ref[...]
    lane = lax.broadcasted_iota(jnp.int32, xn.shape, 1)
    partner = jnp.where((lane % 32) < 16, pltpu.roll(xn, 640 - 16, 1), pltpu.roll(xn, 16, 1))
    cos = jnp.concatenate([cos_ref[...]] * 5, axis=1)
    sin = jnp.concatenate([sin_ref[...]] * 5, axis=1)
    rot = xn * cos + partner * sin
    q_ref[...] = rot[:, 0:512].astype(bf16)
    k_ref[...] = rot[:, 512:640].astype(bf16)
    v_ref[:, 0:LANES] = a_ref[:, 640:768].astype(bf16)
    v_ref[:, LANES:2 * LANES] = jnp.ones((x.shape[0], LANES), bf16)


def _att_prep(att, cos, sin, gain, B, S, tq=512):
    T = B * S
    nb = S // tq
    return pl.pallas_call(
        _att_prep_body, grid=(B, nb),
        in_specs=[pl.BlockSpec((tq, 768), lambda b, i: (b * nb + i, 0)),
                  pl.BlockSpec((tq, LANES), lambda b, i: (i, 0)),
                  pl.BlockSpec((tq, LANES), lambda b, i: (i, 0)),
                  pl.BlockSpec((1, 640), lambda b, i: (0, 0))],
        out_specs=[pl.BlockSpec((tq, 512), lambda b, i: (b * nb + i, 0)),
                   pl.BlockSpec((tq, LANES), lambda b, i: (b * nb + i, 0)),
                   pl.BlockSpec((tq, 2 * LANES), lambda b, i: (b * nb + i, 0))],
        out_shape=[jax.ShapeDtypeStruct((T, 512), bf16), jax.ShapeDtypeStruct((T, LANES), bf16),
                   jax.ShapeDtypeStruct((T, 2 * LANES), bf16)],
        compiler_params=_cparams(("parallel", "parallel")), name="att_prep")(att, cos, sin, gain)


def _rope_tables(S):
    t = np.arange(S)
    row = (t // GRID_W).astype(np.float32)
    col = (t % GRID_W).astype(np.float32)
    n = 16
    inv = np.power(np.float32(ROPE_THETA), -np.arange(n, dtype=np.float32) / n).astype(np.float32)
    ar = jnp.asarray(row)[:, None] * jnp.asarray(inv)
    ac = jnp.asarray(col)[:, None] * jnp.asarray(inv)
    cos = jnp.concatenate([jnp.cos(ar), jnp.cos(ar), jnp.cos(ac), jnp.cos(ac)], axis=1)
    sin = jnp.concatenate([-jnp.sin(ar), jnp.sin(ar), -jnp.sin(ac), jnp.sin(ac)], axis=1)
    return jnp.concatenate([cos, cos], axis=1), jnp.concatenate([sin, sin], axis=1)


def _flash_body(q_ref, k_ref, v_ref, o_ref, acc_ref, *m_scratch, tk, track_max):
    tq = q_ref.shape[0]
    nk = k_ref.shape[0] // tk
    q2 = jnp.concatenate([q_ref[:, 0:LANES], q_ref[:, LANES:2 * LANES]], axis=0)
    acc_ref[...] = jnp.zeros(acc_ref.shape, f32)
    if track_max:
        m_ref, = m_scratch
        m_ref[...] = jnp.full(m_ref.shape, NEG, f32)

    def step(j, carry):
        rows = pl.ds(pl.multiple_of(j * tk, tk), tk)
        s = lax.dot_general(q2, k_ref[rows, :], (((1,), (1,)), ((), ())), preferred_element_type=f32)
        if track_max:
            m_old = m_ref[...]
            m_new = jnp.maximum(m_old, jnp.max(s, axis=1, keepdims=True))
            p = jnp.exp(s - m_new).astype(bf16)
            acc_ref[...] = jnp.exp(m_old - m_new) * acc_ref[...] + jnp.dot(p, v_ref[rows, :], preferred_element_type=f32)
            m_ref[...] = m_new
        else:
            acc_ref[...] += jnp.dot(jnp.exp(s).astype(bf16), v_ref[rows, :], preferred_element_type=f32)
        return carry

    lax.fori_loop(0, nk, step, 0)
    o = acc_ref[:, 0:LANES] / acc_ref[:, LANES:2 * LANES]
    o_ref[...] = jnp.concatenate([o[0:tq], o[tq:2 * tq]], axis=1)


SCORE_BOUND_MAX = 60.0


def _flash(q, k, v, score_bound, B, S, tq=256, tk=1024):
    T = B * S
    nb = S // tq
    tk = min(tk, S)

    def call(track_max):
        scratch = [pltpu.VMEM((2 * tq, 2 * LANES), f32)] + ([pltpu.VMEM((2 * tq, 1), f32)] if track_max else [])
        return pl.pallas_call(
            functools.partial(_flash_body, tk=tk, track_max=track_max), grid=(B, ATT_KV_HEADS, nb),
            in_specs=[pl.BlockSpec((tq, 2 * LANES), lambda b, g, i: (b * nb + i, g)),
                      pl.BlockSpec((S, LANES), lambda b, g, i: (b, 0)),
                      pl.BlockSpec((S, 2 * LANES), lambda b, g, i: (b, 0))],
            out_specs=pl.BlockSpec((tq, 2 * LANES), lambda b, g, i: (b * nb + i, g)),
            out_shape=jax.ShapeDtypeStruct((T, 512), f32), scratch_shapes=scratch,
            compiler_params=_cparams(("parallel", "parallel", "parallel")),
            name="flash_safe" if track_max else "flash")(q, k, v)

    return lax.cond(score_bound <= SCORE_BOUND_MAX, lambda: call(False), lambda: call(True))


def _halo_specs(width, tm, B, S):
    nb = S // tm
    r8 = tm // 8
    last8 = B * S // 8 - 1

    def main(b, i):
        return (b * nb + i, 0)

    def prev(b, i):
        return (jnp.maximum(b * (S // 8) + i * r8 - 1, 0), 0)

    def nxt(b, i):
        return (jnp.minimum(b * (S // 8) + (i + 1) * r8, last8), 0)

    return [pl.BlockSpec((tm, width), main), pl.BlockSpec((8, width), prev), pl.BlockSpec((8, width), nxt)]


def _fill_halo(buf_ref, x_ref, p_ref, n_ref):
    tm = x_ref.shape[0]
    i = pl.program_id(1)
    last = pl.num_programs(1) - 1
    buf_ref[pl.ds(8, tm), :] = x_ref[...]
    buf_ref[pl.ds(0, 8), :] = jnp.where(i > 0, p_ref[...], 0.0)
    buf_ref[pl.ds(8 + tm, 8), :] = jnp.where(i < last, n_ref[...], 0.0)


def _rw_prep_body(x_ref, p_ref, n_ref, mix_ref, w2_ref, a2_ref, g2_ref, vec_ref,
                  r_ref, k_ref, v_ref, an_ref, bn_ref, lw_ref, gate_ref, bonus_ref, buf_ref):
    tm = x_ref.shape[0]
    _fill_halo(buf_ref, x_ref, p_ref, n_ref)
    x = x_ref[...]
    p = x + mix_ref[0:1, :] * (buf_ref[pl.ds(7, tm), :] - x) + mix_ref[1:2, :] * (buf_ref[pl.ds(9, tm), :] - x)
    r, k, v = p[:, 0:256], p[:, 256:512], p[:, 512:768]
    w0f, w0b, a0, k_k, k_a, r_k = (vec_ref[j:j + 1, :] for j in range(6))
    dec = _bdot(jnp.tanh(p[:, 768:896]), w2_ref[...])
    z = p[:, 896:1088]
    a = _sigmoid(a0 + _bdot(z, a2_ref[...]))
    gate_ref[...] = _bdot(_sigmoid(z), g2_ref[...])
    seg = _seg_matrix(BRANCH_W)
    kk = k * k_k
    kk = kk / jnp.maximum(jnp.sqrt(_hdot(kk * kk, seg)), 1e-12)
    k2 = k * (1.0 + (a - 1.0) * k_a)
    bonus_ref[...] = _hdot(r * k2 * r_k, seg) * v
    r_ref[...] = r
    k_ref[...] = k2
    v_ref[...] = v
    an_ref[...] = -kk
    bn_ref[...] = kk * a
    lw_ref[0] = -jnp.exp(-_softplus(-(w0f + dec[:, 0:256])) - 0.5)
    lw_ref[1] = -jnp.exp(-_softplus(-(w0b + dec[:, 256:512])) - 0.5)


def _rw_prep(rw, mix, w2, a2, g2, vec, B, S, tm=512):
    T = B * S
    nb = S // tm
    o256 = pl.BlockSpec((tm, BRANCH_W), lambda b, i: (b * nb + i, 0))
    full = lambda shape: pl.BlockSpec(shape, lambda b, i: (0,) * len(shape))
    return pl.pallas_call(
        _rw_prep_body, grid=(B, nb),
        in_specs=_halo_specs(RW_COLS, tm, B, S) + [full((2, RW_COLS)), full((128, 512)), full((192, 256)),
                                                   full((192, 256)), full((8, 256))],
        out_specs=[o256] * 5 + [pl.BlockSpec((2, tm, BRANCH_W), lambda b, i: (0, b * nb + i, 0)), o256, o256],
        out_shape=[jax.ShapeDtypeStruct((T, BRANCH_W), f32)] * 5 + [jax.ShapeDtypeStruct((2, T, BRANCH_W), f32)]
        + [jax.ShapeDtypeStruct((T, BRANCH_W), f32)] * 2,
        scratch_shapes=[pltpu.VMEM((tm + 16, RW_COLS), f32)],
        compiler_params=_cparams(("parallel", "parallel")), name="rw_prep")(rw, rw, rw, mix, w2, a2, g2, vec)


def _pair_masks(rev, nbatch):
    n = 2 * CHUNK
    r = lax.broadcasted_iota(jnp.int32, (nbatch, n, n), 1)
    c = lax.broadcasted_iota(jnp.int32, (nbatch, n, n), 2)
    same = (r // CHUNK) == (c // CHUNK)
    if rev:
        return r, c, same & (c > r), same & (c >= r)
    return r, c, same & (c < r), same & (c <= r)


def _rw_chunk(st, r, k, v, an, bn, lw, rev):
    L = CHUNK
    N = r.shape[0]
    ri = lax.broadcasted_iota(jnp.int32, (N, L, L), 1)
    ci = lax.broadcasted_iota(jnp.int32, (N, L, L), 2)
    tri = ((ci >= ri) if rev else (ci <= ri)).astype(f32)
    cs = _hdot(tri, lw)
    tot = jnp.sum(lw, axis=1, keepdims=True)
    e_neg = jnp.exp(-cs)
    a2 = _stack_heads(an * jnp.exp(cs - lw))
    r2 = _stack_heads(r * jnp.exp(cs))
    b2 = _stack_heads(bn * e_neg)
    k2 = _stack_heads(k * e_neg)
    v2 = _stack_heads(v)
    rr, cc, strict, incl = _pair_masks(rev, N)
    mab = jnp.where(strict, _bdot_nt(a2, b2), 0.0)
    mak = jnp.where(strict, _bdot_nt(a2, k2), 0.0)
    pb = jnp.where(incl, _bdot_nt(r2, b2), 0.0)
    pk = jnp.where(incl, _bdot_nt(r2, k2), 0.0)
    eye = rr == cc
    m8 = jnp.where((rr // 8) == (cc // 8), mab, 0.0)
    x = eye.astype(f32) + m8
    p = _bdot(m8, m8)
    x = x + _bdot(x, p)
    p = _bdot(p, p)
    x = x + _bdot(x, p)
    n = 8
    while n < L:
        e = jnp.where(((rr // (2 * n)) == (cc // (2 * n))) & ((rr // n) != (cc // n)), mab, 0.0)
        x = x + _bdot(_bdot(x, e), x)
        n *= 2
    w = _bdot(x, a2)
    u0 = _bdot(x, _bdot(mak, v2))
    rh = r2 + _bdot(pb, w)
    y2 = _bdot(pb, u0) + _bdot(pk, v2) + _bdot(rh, st)
    y = y2[:, 0:L] + y2[:, L:2 * L]
    gam = _row_to_col(jnp.exp(tot), eye)
    st = gam * (st + _bdot(_bdot_tn(b2, w), st) + _bdot_tn(b2, u0) + _bdot_tn(k2, v2))
    return y, st


def _load_pairs(ref, rows):
    return jnp.concatenate([ref[:, rows, 0:LANES], ref[:, rows, LANES:2 * LANES]], axis=0)


def _store_pairs(ref, rows, y):
    nb = ref.shape[0]
    ref[:, rows, 0:LANES] = y[0:nb]
    ref[:, rows, LANES:2 * LANES] = y[nb:2 * nb]


def _rw_scan_body(r_ref, k_ref, v_ref, an_ref, bn_ref, lw_ref, y_ref, st_ref, *, rev):
    nch = r_ref.shape[1] // CHUNK

    @pl.when(pl.program_id(0) == 0)
    def _():
        st_ref[...] = jnp.zeros(st_ref.shape, f32)

    def step(cc, carry):
        c = (nch - 1 - cc) if rev else cc
        rows = pl.ds(pl.multiple_of(c * CHUNK, CHUNK), CHUNK)
        y, st = _rw_chunk(st_ref[...], *(_load_pairs(ref, rows) for ref in (r_ref, k_ref, v_ref, an_ref, bn_ref)),
                          _load_pairs(lw_ref.at[0], rows), rev)
        _store_pairs(y_ref, rows, y)
        st_ref[...] = st
        return carry

    lax.fori_loop(0, nch, step, 0)


def _rw_scan(r, k, v, an, bn, lw, B, S, rev, ts=256):
    nb = S // ts
    d = 1 if rev else 0
    blk = (lambda i: nb - 1 - i) if rev else (lambda i: i)
    io = pl.BlockSpec((B, ts, BRANCH_W), lambda i: (0, blk(i), 0))
    return pl.pallas_call(
        functools.partial(_rw_scan_body, rev=rev), grid=(nb,),
        in_specs=[io] * 5 + [pl.BlockSpec((1, B, ts, BRANCH_W), lambda i: (d, 0, blk(i), 0))],
        out_specs=io, out_shape=jax.ShapeDtypeStruct((B, S, BRANCH_W), f32),
        scratch_shapes=[pltpu.VMEM((B * BRANCH_W // LANES, LANES, LANES), f32)],
        compiler_params=_cparams(("arbitrary",)),
        name="rw_scan_bwd" if rev else "rw_scan_fwd")(r, k, v, an, bn, lw)


def _rw_finish_body(yf_ref, yb_ref, gate_ref, bonus_ref, gn_ref, o_ref):
    y = yf_ref[...] + yb_ref[...]
    seg = _seg_matrix(BRANCH_W)
    mu = _hdot(y, seg) * (1.0 / HEAD_DIM)
    d = y - mu
    var = _hdot(d * d, seg) * (1.0 / HEAD_DIM)
    yn = d * lax.rsqrt(var + RW_GN_EPS) * gn_ref[0:1, :] + gn_ref[1:2, :]
    o_ref[...] = (yn + bonus_ref[...]) * gate_ref[...]


def _rw_finish(yf, yb, gate, bonus, gn, tm=1024):
    T = yf.shape[0]
    io = pl.BlockSpec((tm, BRANCH_W), lambda i: (i, 0))
    return pl.pallas_call(
        _rw_finish_body, grid=(T // tm,), in_specs=[io] * 4 + [pl.BlockSpec((2, BRANCH_W), lambda i: (0, 0))],
        out_specs=io, out_shape=jax.ShapeDtypeStruct((T, BRANCH_W), f32),
        compiler_params=_cparams(("parallel",)), name="rw_finish")(yf, yb, gate, bonus, gn)


def _rwkv(rw, prm, B, S):
    mix, w2, a2, g2, vec, gn = prm
    r, k, v, an, bn, lw, gate, bonus = _rw_prep(rw, mix, w2, a2, g2, vec, B, S)
    seq = [t.reshape(B, S, BRANCH_W) for t in (r, k, v, an, bn)] + [lw.reshape(2, B, S, BRANCH_W)]
    yf = _rw_scan(*seq, B, S, rev=False).reshape(B * S, BRANCH_W)
    yb = _rw_scan(*seq, B, S, rev=True).reshape(B * S, BRANCH_W)
    return _rw_finish(yf, yb, gate, bonus, gn)


def _rw_params(mix, w0, w2, a0, a2, g2, k_k, k_a, r_k, gn_g, gn_b):
    z = jnp.zeros((64, 256), f32)
    w2c = jnp.concatenate([jnp.concatenate([w2[0], z], axis=1), jnp.concatenate([z, w2[1]], axis=1)], axis=0)
    a2p = jnp.concatenate([a2, jnp.zeros((128, 256), f32)], axis=0)
    g2p = jnp.concatenate([jnp.zeros((64, 256), f32), g2], axis=0)
    vec = jnp.stack([w0[0], w0[1], a0, k_k, k_a, r_k.reshape(-1), jnp.zeros_like(a0), jnp.zeros_like(a0)])
    return mix, w2c.astype(bf16), a2p.astype(bf16), g2p.astype(bf16), vec, jnp.stack([gn_g, gn_b])


def _s5_scan_body(u_ref, bcat_ref, cre_ref, cim_ref, lre_ref, lim_ref, y_ref, sre_ref, sim_ref, bre_ref, bim_ref):
    ts = u_ref.shape[0]
    n = S5_GROUPS * S5_STATE

    @pl.when(pl.program_id(0) == 0)
    def _():
        sre_ref[...] = jnp.zeros(sre_ref.shape, f32)
        sim_ref[...] = jnp.zeros(sim_ref.shape, f32)

    u = u_ref[...].reshape(ts * 8, BRANCH_W).astype(bf16)
    fwd = lax.broadcasted_iota(jnp.int32, (ts, 8, n), 1) < 4

    def bu(part):
        return jnp.dot(u, bcat_ref[:, part * n:(part + 1) * n], preferred_element_type=f32).reshape(ts, 8, n)

    bre_ref[...] = jnp.where(fwd, bu(0), bu(2))
    bim_ref[...] = jnp.where(fwd, bu(1), bu(3))
    lre = lre_ref[...]
    lim = lim_ref[...]

    def step(t, carry):
        sre, sim = carry
        nre = lre * sre - lim * sim + bre_ref[t]
        nim = lre * sim + lim * sre + bim_ref[t]
        bre_ref[t] = nre
        bim_ref[t] = nim
        return nre, nim

    sre, sim = lax.fori_loop(0, ts, step, (sre_ref[...], sim_ref[...]), unroll=4)
    sre_ref[...] = sre
    sim_ref[...] = sim
    y = (_bdot(bre_ref[...].reshape(ts * 8, n), cre_ref[...])
         - _bdot(bim_ref[...].reshape(ts * 8, n), cim_ref[...])).reshape(ts, 8, 2 * BRANCH_W)
    fwd_o = lax.broadcasted_iota(jnp.int32, (ts, 8, BRANCH_W), 1) < 4
    y_ref[...] = jnp.where(fwd_o, y[:, :, 0:BRANCH_W], y[:, :, BRANCH_W:2 * BRANCH_W])


def _s5_scan(u2, bcat, cre, cim, lre, lim, ts=128):
    S = u2.shape[0]
    n = S5_GROUPS * S5_STATE
    full = lambda shape: pl.BlockSpec(shape, lambda i: (0,) * len(shape))
    return pl.pallas_call(
        _s5_scan_body, grid=(S // ts,),
        in_specs=[pl.BlockSpec((ts, 8, BRANCH_W), lambda i: (i, 0, 0)), full((BRANCH_W, 4 * n)),
                  full((n, 2 * BRANCH_W)), full((n, 2 * BRANCH_W)), full((8, n)), full((8, n))],
        out_specs=pl.BlockSpec((ts, 8, BRANCH_W), lambda i: (i, 0, 0)),
        out_shape=jax.ShapeDtypeStruct((S, 8, BRANCH_W), f32),
        scratch_shapes=[pltpu.VMEM((8, n), f32), pltpu.VMEM((8, n), f32),
                        pltpu.VMEM((ts, 8, n), f32), pltpu.VMEM((ts, 8, n), f32)],
        compiler_params=_cparams(("arbitrary",)), name="s5_scan")(u2, bcat, cre, cim, lre, lim)


def _s5_finish_body(y_ref, u_ref, d_ref, w_ref, b_ref, o_ref):
    y = y_ref[...] + u_ref[...] * d_ref[...]
    y = 0.5 * y * (1.0 + jnp.tanh(math.sqrt(2.0 / math.pi) * (y + 0.044715 * (y * y * y))))
    o_ref[...] = y * _sigmoid(_bdot(y, w_ref[...]) + b_ref[...])


def _s5_finish(y, u, d, w, b, tm=1024):
    T = y.shape[0]
    io = pl.BlockSpec((tm, BRANCH_W), lambda i: (i, 0))
    vec = pl.BlockSpec((1, BRANCH_W), lambda i: (0, 0))
    return pl.pallas_call(
        _s5_finish_body, grid=(T // tm,),
        in_specs=[io, io, vec, pl.BlockSpec((BRANCH_W, BRANCH_W), lambda i: (0, 0)), vec],
        out_specs=io, out_shape=jax.ShapeDtypeStruct((T, BRANCH_W), f32),
        compiler_params=_cparams(("parallel",)), name="s5_finish")(y, u, d, w, b)


def _s5_params(lam_re, lam_im, log_dt, b_re, b_im, c_re, c_im):
    G, P, C = S5_GROUPS, S5_STATE, S5_GROUP
    eye = jnp.eye(G, dtype=f32)
    b_c = lax.complex(b_re, b_im)
    bcat, cre, cim, lre, lim = [], [], [], [], []
    for d in range(2):
        lam = lax.complex(jnp.minimum(lam_re[d], -1e-4), lam_im[d])
        lam_bar = jnp.exp(lam * jnp.exp(log_dt[d])[:, None])
        b_bar = ((lam_bar - 1.0) / lam)[..., None] * b_c
        for part in (jnp.real(b_bar), jnp.imag(b_bar)):
            bcat.append(jnp.einsum('gh,gpc->gchp', eye, part).reshape(G * C, G * P))
        cre.append(jnp.einsum('gh,gcp->gphc', eye, c_re[d]).reshape(G * P, G * C))
        cim.append(jnp.einsum('gh,gcp->gphc', eye, c_im[d]).reshape(G * P, G * C))
        lre.append(jnp.broadcast_to(jnp.real(lam_bar).reshape(1, G * P), (4, G * P)))
        lim.append(jnp.broadcast_to(jnp.imag(lam_bar).reshape(1, G * P), (4, G * P)))
    return (jnp.concatenate(bcat, axis=1).astype(bf16), jnp.concatenate(cre, axis=1).astype(bf16),
            jnp.concatenate(cim, axis=1).astype(bf16), jnp.concatenate(lre, axis=0), jnp.concatenate(lim, axis=0))


def _s5(u, prm, fin, B, S):
    u3 = u.reshape(B, S, BRANCH_W).transpose(1, 0, 2)
    u2 = jnp.concatenate([u3, u3[::-1]], axis=1)
    y2 = _s5_scan(u2, *prm)
    y = (y2[:, 0:B] + y2[::-1, B:2 * B]).transpose(1, 0, 2).reshape(B * S, BRANCH_W)
    return _s5_finish(y, u, *fin)


def _ml_prep_body(x_ref, p_ref, n_ref, w_ref, b_ref, q_ref, k_ref, buf_ref):
    tm = x_ref.shape[0]
    _fill_halo(buf_ref, x_ref, p_ref, n_ref)
    y = b_ref[...] + w_ref[2:3, :] * x_ref[...]
    for j in (0, 1, 3, 4):
        y = y + w_ref[j:j + 1, :] * buf_ref[pl.ds(6 + j, tm), :]
    y = y * _sigmoid(y)
    q_ref[...] = y[:, 0:BRANCH_W]
    k_ref[...] = y[:, BRANCH_W:2 * BRANCH_W] * (HEAD_DIM ** -0.5)


def _ml_prep(mqk, w, b, B, S, tm=512):
    T = B * S
    nb = S // tm
    o = pl.BlockSpec((tm, BRANCH_W), lambda b_, i: (b_ * nb + i, 0))
    return pl.pallas_call(
        _ml_prep_body, grid=(B, nb),
        in_specs=_halo_specs(512, tm, B, S) + [pl.BlockSpec((5, 512), lambda b_, i: (0, 0)),
                                               pl.BlockSpec((1, 512), lambda b_, i: (0, 0))],
        out_specs=[o, o], out_shape=[jax.ShapeDtypeStruct((T, BRANCH_W), f32)] * 2,
        scratch_shapes=[pltpu.VMEM((tm + 16, 512), f32)],
        compiler_params=_cparams(("parallel", "parallel")), name="ml_prep")(mqk, mqk, mqk, w, b)


def _ml_chunk(state, q, k, v, li, lfp, rev):
    c2, n_row, m_row = state
    L = CHUNK
    N = q.shape[0]
    rr, cc, _, incl = _pair_masks(rev, N)
    same = (rr // L) == (cc // L)
    eye = rr == cc
    lane = lax.broadcasted_iota(jnp.int32, (N, 1, 2 * L), 2)
    lf = jnp.minimum(lfp, 0.0) - jnp.log(1.0 + jnp.exp(-jnp.abs(lfp)))
    lf8 = jnp.broadcast_to(lf, (N, 8, 2 * L))
    cum = (same & ((rr >= cc) if rev else (rr <= cc))).astype(f32)
    b_row = _hdot(lf8, cum)[:, 0:1]
    g_row = _hdot(lf8, same.astype(f32))[:, 0:1]
    w_end = g_row - b_row + li
    m0 = jnp.max(jnp.where(lane < L, w_end, NEG), axis=2, keepdims=True)
    m1 = jnp.max(jnp.where(lane < L, NEG, w_end), axis=2, keepdims=True)
    m_loc = jnp.where(lane < L, m0, m1)
    e_col = _row_to_col(jnp.exp(w_end - m_loc), eye)
    b_col = _row_to_col(b_row, eye)
    q2, k2, v2 = _stack_heads(q), _stack_heads(k), _stack_heads(v)
    log_inter = b_col + _row_to_col(m_row, eye)
    log_intra = jnp.where(incl, b_col - b_row + li, NEG)
    m_r = jnp.maximum(log_inter, jnp.max(log_intra, axis=2, keepdims=True))
    s = _bdot_nt(q2, k2) * jnp.exp(log_intra - m_r)
    inter = jnp.exp(log_inter - m_r)
    num = _bdot(s, v2) + inter * _bdot(q2, c2)
    den = jnp.sum(s, axis=2, keepdims=True) + inter * jnp.sum(q2 * n_row, axis=2, keepdims=True)
    h2 = num / jnp.maximum(jnp.abs(den), jnp.exp(-m_r))
    h = h2[:, 0:L] + h2[:, L:2 * L]
    m_new = jnp.maximum(g_row + m_row, m_loc)
    a = jnp.exp(g_row + m_row - m_new)
    bb = jnp.exp(m_loc - m_new)
    ek = e_col * k2
    c2 = a * c2 + bb * _bdot_tn(ek, v2)
    n_row = a * n_row + bb * jnp.sum(ek, axis=1, keepdims=True)
    return h, (c2, n_row, m_new)


def _ml_scan_body(q_ref, k_ref, v_ref, g_ref, h_ref, c_ref, n_ref, m_ref, *, rev):
    nch = q_ref.shape[1] // CHUNK

    @pl.when(pl.program_id(0) == 0)
    def _():
        c_ref[...] = jnp.zeros(c_ref.shape, f32)
        n_ref[...] = jnp.zeros(n_ref.shape, f32)
        m_ref[...] = jnp.zeros(m_ref.shape, f32)

    def step(cc, carry):
        c = (nch - 1 - cc) if rev else cc
        rows = pl.ds(pl.multiple_of(c * CHUNK, CHUNK), CHUNK)
        li, lfp = (jnp.concatenate([g_ref[t, 0, 0, :, pl.ds(c, 1), :], g_ref[t, 0, 1, :, pl.ds(c, 1), :]], axis=0)
                   for t in range(2))
        h, (c2, n_row, m_row) = _ml_chunk((c_ref[...], n_ref[...], m_ref[...]), _load_pairs(q_ref, rows),
                                          _load_pairs(k_ref, rows), _load_pairs(v_ref, rows), li, lfp, rev)
        _store_pairs(h_ref, rows, h)
        c_ref[...] = c2
        n_ref[...] = n_row
        m_ref[...] = m_row
        return carry

    lax.fori_loop(0, nch, step, 0)


def _ml_scan(q, k, v, g, B, S, rev, ts=512):
    nb = S // ts
    nch = ts // CHUNK
    d = 1 if rev else 0
    nchain = B * BRANCH_W // LANES
    blk = (lambda i: nb - 1 - i) if rev else (lambda i: i)
    io = pl.BlockSpec((B, ts, BRANCH_W), lambda i: (0, blk(i), 0))
    return pl.pallas_call(
        functools.partial(_ml_scan_body, rev=rev), grid=(nb,),
        in_specs=[io, io, io, pl.BlockSpec((2, 1, 2, B, nch, LANES), lambda i: (0, d, 0, 0, blk(i), 0))],
        out_specs=io, out_shape=jax.ShapeDtypeStruct((B, S, BRANCH_W), f32),
        scratch_shapes=[pltpu.VMEM((nchain, LANES, LANES), f32), pltpu.VMEM((nchain, 1, LANES), f32),
                        pltpu.VMEM((nchain, 1, LANES), f32)],
        compiler_params=_cparams(("arbitrary",)),
        name="ml_scan_bwd" if rev else "ml_scan_fwd")(q, k, v, g)


def _mlstm(mqk, mv, gt, conv_w, conv_b, B, S):
    q, k = _ml_prep(mqk, conv_w, conv_b, B, S)
    g = gt.reshape(2, 2, 2, 2, B, S // CHUNK, CHUNK).transpose(0, 1, 2, 4, 5, 3, 6).reshape(2, 2, 2, B, S // CHUNK, LANES)
    seq = [t.reshape(B, S, BRANCH_W) for t in (q, k, mv)]
    return (_ml_scan(*seq, g, B, S, rev=False).reshape(B * S, BRANCH_W),
            _ml_scan(*seq, g, B, S, rev=True).reshape(B * S, BRANCH_W))


def _merge_body(x_ref, att_ref, rw_ref, s5_ref, hf_ref, hb_ref, mo_ref, wg_ref, bg_ref, wba_ref, wb_ref, wo_ref,
                ln_ref, o_ref):
    x = x_ref[...]
    xb = x.astype(bf16)
    ml = _sigmoid(mo_ref[...]) * (hf_ref[...] + hb_ref[...])
    branches = (att_ref[...], rw_ref[...], s5_ref[...], ml)
    merged = None
    for n in range(4):
        gate = _sigmoid(jnp.dot(xb, wg_ref[n], preferred_element_type=f32) + bg_ref[n:n + 1, :])
        wide = _bdot(branches[n], wba_ref[...] if n == 0 else wb_ref[n - 1])
        merged = gate * wide if merged is None else merged + gate * wide
    y = ALPHA * x + _bdot(merged, wo_ref[...])
    o_ref[...] = _layer_norm(y, ln_ref[0:1, :], ln_ref[1:2, :])


def _merge(xt, att, rw, s5, hf, hb, mo, wg, bg, wba, wb, wo, ln, tm=256):
    T = xt.shape[0]
    row = lambda n: pl.BlockSpec((tm, n), lambda i: (i, 0))
    const = lambda shape: pl.BlockSpec(shape, lambda i: (0,) * len(shape), pipeline_mode=pl.Buffered(1))
    return pl.pallas_call(
        _merge_body, grid=(T // tm,),
        in_specs=[row(D_MODEL), row(512)] + [row(BRANCH_W)] * 5
        + [const((4, D_MODEL, D_MODEL)), const((4, D_MODEL)), const((512, D_MODEL)), const((3, BRANCH_W, D_MODEL)),
           const((D_MODEL, D_MODEL)), const((2, D_MODEL))],
        out_specs=row(D_MODEL), out_shape=jax.ShapeDtypeStruct((T, D_MODEL), f32),
        compiler_params=_cparams(("parallel",)), name="merge")(xt, att, rw, s5, hf, hb, mo, wg, bg, wba, wb, wo, ln)


def _att_branch_weight(wb):
    z = jnp.zeros((HEAD_DIM, D_MODEL), f32)
    parts = []
    for h in range(ATT_HEADS):
        wh = wb[64 * h:64 * h + 64]
        parts += [wh, z] if h // 2 == 0 else [z, wh]
    return jnp.concatenate(parts, axis=0)


def _ffn_body(x_ref, w1_ref, w3_ref, w2_ref, ln_ref, o_ref, acc_ref):
    j = pl.program_id(1)
    xb = x_ref[...].astype(bf16)
    h1 = jnp.dot(xb, w1_ref[...], preferred_element_type=f32)
    h3 = jnp.dot(xb, w3_ref[...], preferred_element_type=f32)
    part = _bdot(h1 * _sigmoid(h1) * h3, w2_ref[...])

    @pl.when(j == 0)
    def _():
        acc_ref[...] = part

    @pl.when(j > 0)
    def _():
        acc_ref[...] += part

    @pl.when(j == pl.num_programs(1) - 1)
    def _():
        o_ref[...] = _layer_norm(ALPHA * x_ref[...] + acc_ref[...], ln_ref[0:1, :], ln_ref[1:2, :])


def _ffn(xt, w1, w3, w2, ln, tm=512, tf=1408):
    T = xt.shape[0]
    dff = w1.shape[1]
    return pl.pallas_call(
        _ffn_body, grid=(T // tm, dff // tf),
        in_specs=[pl.BlockSpec((tm, D_MODEL), lambda i, j: (i, 0)), pl.BlockSpec((D_MODEL, tf), lambda i, j: (0, j)),
                  pl.BlockSpec((D_MODEL, tf), lambda i, j: (0, j)), pl.BlockSpec((tf, D_MODEL), lambda i, j: (j, 0)),
                  pl.BlockSpec((2, D_MODEL), lambda i, j: (0, 0))],
        out_specs=pl.BlockSpec((tm, D_MODEL), lambda i, j: (i, 0)),
        out_shape=jax.ShapeDtypeStruct((T, D_MODEL), f32), scratch_shapes=[pltpu.VMEM((tm, D_MODEL), f32)],
        compiler_params=_cparams(("parallel", "arbitrary")), name="ffn")(xt, w1, w3, w2, ln)


def _moe_body(x_ref, rt_ref, w1_ref, w3_ref, w2_ref, ln_ref, o_ref, acc_ref, comb_ref):
    e = pl.program_id(1)
    j = pl.program_id(2)
    first = (e == 0) & (j == 0)

    @pl.when(first)
    def _():
        logits = _hdot(x_ref[...], rt_ref[...])
        lane = lax.broadcasted_iota(jnp.int32, logits.shape, 1)
        lg = jnp.where(lane < N_EXPERTS, logits, NEG)
        v1 = jnp.max(lg, axis=1, keepdims=True)
        i1 = jnp.min(jnp.where(lg == v1, lane, LANES), axis=1, keepdims=True)
        lg2 = jnp.where(lane == i1, NEG, lg)
        v2 = jnp.max(lg2, axis=1, keepdims=True)
        i2 = jnp.min(jnp.where(lg2 == v2, lane, LANES), axis=1, keepdims=True)
        e2 = jnp.exp(v2 - v1)
        comb_ref[...] = jnp.where(lane == i1, 1.0 / (1.0 + e2), 0.0) + jnp.where(lane == i2, e2 / (1.0 + e2), 0.0)
        acc_ref[...] = jnp.zeros(acc_ref.shape, f32)

    xb = x_ref[...].astype(bf16)
    h1 = jnp.dot(xb, w1_ref[0], preferred_element_type=f32)
    h3 = jnp.dot(xb, w3_ref[0], preferred_element_type=f32)
    lane = lax.broadcasted_iota(jnp.int32, comb_ref.shape, 1)
    ce = jnp.sum(jnp.where(lane == e, comb_ref[...], 0.0), axis=1, keepdims=True)
    acc_ref[...] += ce * _bdot(h1 * _sigmoid(h1) * h3, w2_ref[0])

    @pl.when((e == pl.num_programs(1) - 1) & (j == pl.num_programs(2) - 1))
    def _():
        o_ref[...] = _layer_norm(ALPHA * x_ref[...] + acc_ref[...], ln_ref[0:1, :], ln_ref[1:2, :])


def _moe(xt, router, w1, w3, w2, ln, tm=1024, tf=512):
    T = xt.shape[0]
    tm = min(tm, T)
    dff = w1.shape[2]
    return pl.pallas_call(
        _moe_body, grid=(T // tm, N_EXPERTS, dff // tf),
        in_specs=[pl.BlockSpec((tm, D_MODEL), lambda i, e, j: (i, 0)),
                  pl.BlockSpec((D_MODEL, LANES), lambda i, e, j: (0, 0)),
                  pl.BlockSpec((1, D_MODEL, tf), lambda i, e, j: (e, 0, j)),
                  pl.BlockSpec((1, D_MODEL, tf), lambda i, e, j: (e, 0, j)),
                  pl.BlockSpec((1, tf, D_MODEL), lambda i, e, j: (e, j, 0)),
                  pl.BlockSpec((2, D_MODEL), lambda i, e, j: (0, 0))],
        out_specs=pl.BlockSpec((tm, D_MODEL), lambda i, e, j: (i, 0)),
        out_shape=jax.ShapeDtypeStruct((T, D_MODEL), f32),
        scratch_shapes=[pltpu.VMEM((tm, D_MODEL), f32), pltpu.VMEM((tm, LANES), f32)],
        compiler_params=_cparams(("parallel", "arbitrary", "arbitrary")), name="moe")(xt, router, w1, w3, w2, ln)


def kernel(x, w_in, b_in, att_gq, att_gk, rw_mix, rw_w0, rw_w2, rw_a0, rw_a2, rw_g2, rw_kk, rw_ka, rw_rk, rw_ln_g, rw_ln_b, s5_lam_re, s5_lam_im, s5_log_dt, s5_b_re, s5_b_im, s5_c_re, s5_c_im, s5_d, s5_glu_w, s5_glu_b, ml_conv_w, ml_conv_b, ml_ib, ml_fb, w_gate, b_gate, w_branch, w_out, ln1_g, ln1_b, ffn_w1, ffn_w3, ffn_w2, moe_router, moe_w1, moe_w3, moe_w2, ln2_g, ln2_b):
    B, S, D = x.shape
    xt = x.reshape(B * S, D)
    cos, sin = _rope_tables(S)
    for l in range(DEPTH):
        att, s5u, mqk, mv, mo, rw, gt = _proj(xt, *_proj_params(w_in[l], b_in[l], ml_ib[l], ml_fb[l]))
        gain = jnp.concatenate([jnp.tile(att_gq[l], 8) * (HEAD_DIM ** -0.5), jnp.tile(att_gk[l], 2)])[None, :]
        q, k, v = _att_prep(att, cos, sin, gain, B, S)
        score_bound = 8.1 * jnp.max(jnp.abs(att_gq[l])) * jnp.max(jnp.abs(att_gk[l]))
        o_att = _flash(q, k, v, score_bound, B, S)
        o_rw = _rwkv(rw, _rw_params(rw_mix[l], rw_w0[l], rw_w2[l], rw_a0[l], rw_a2[l], rw_g2[l], rw_kk[l], rw_ka[l],
                                    rw_rk[l], rw_ln_g[l], rw_ln_b[l]), B, S)
        o_s5 = _s5(s5u, _s5_params(s5_lam_re[l], s5_lam_im[l], s5_log_dt[l], s5_b_re[l], s5_b_im[l], s5_c_re[l],
                                   s5_c_im[l]),
                   (s5_d[l][None, :], s5_glu_w[l].astype(bf16), s5_glu_b[l][None, :]), B, S)
        hf, hb = _mlstm(mqk, mv, gt, ml_conv_w[l], ml_conv_b[l][None, :], B, S)
        xt = _merge(xt, o_att, o_rw, o_s5, hf, hb, mo, w_gate[l].astype(bf16), b_gate[l],
                    _att_branch_weight(w_branch[l, 0]).astype(bf16), w_branch[l, 1:].astype(bf16),
                    w_out[l].astype(bf16), jnp.stack([ln1_g[l], ln1_b[l]]))
        ln2 = jnp.stack([ln2_g[l], ln2_b[l]])
        if l % 2 == 0:
            xt = _ffn(xt, ffn_w1[l // 2].astype(bf16), ffn_w3[l // 2].astype(bf16), ffn_w2[l // 2].astype(bf16), ln2)
        else:
            router = jnp.pad(moe_router[l // 2], ((0, 0), (0, LANES - N_EXPERTS)))
            xt = _moe(xt, router, moe_w1[l // 2].astype(bf16), moe_w3[l // 2].astype(bf16),
                      moe_w2[l // 2].astype(bf16), ln2)
    return xt.reshape(B, S, D)
```

```python
import functools
import math

import jax
import jax.numpy as jnp
import numpy as np
from jax import lax
from jax.experimental import pallas as pl
from jax.experimental.pallas import tpu as pltpu
from jax.experimental.pallas import tpu_sc as plsc

f32 = jnp.float32
bf16 = jnp.bfloat16
HI = lax.Precision.HIGHEST

D_MODEL = 1024
DEPTH = 2
GRID_W = 64
BRANCH_W = 256
HEAD_DIM = 64
ATT_HEADS = 4
ATT_KV_HEADS = 2
ROPE_THETA = 10000.0
QK_EPS = 1e-6
RW_GN_EPS = 64e-5
RW_COLS = 1088
S5_GROUP = 16
S5_GROUPS = 16
S5_STATE = 64
ML_HEADS = 4
N_EXPERTS = 8
ALPHA = (2 * DEPTH) ** 0.25
LN_EPS = 1e-5

LANES = 128
CHUNK = 64
NEG = -1e30
VMEM_LIMIT = 56 * 1024 * 1024

PROJ_SPLITS = (768, 256, 512, 256, 256, RW_COLS)


def _cparams(sem):
    return pltpu.CompilerParams(dimension_semantics=sem, vmem_limit_bytes=VMEM_LIMIT)


def _sigmoid(x):
    return 1.0 / (1.0 + jnp.exp(-x))


def _softplus(x):
    return jnp.maximum(x, 0.0) + jnp.log(1.0 + jnp.exp(-jnp.abs(x)))


def _dims(a, lhs_c, rhs_c):
    lead = a.ndim - 2
    batch = tuple(range(lead))
    return (((lhs_c + lead,), (rhs_c + lead,)), (batch, batch))


def _bdot(a, b):
    return lax.dot_general(a.astype(bf16), b.astype(bf16), _dims(a, 1, 0), preferred_element_type=f32)


def _bdot_nt(a, b):
    return lax.dot_general(a.astype(bf16), b.astype(bf16), _dims(a, 1, 1), preferred_element_type=f32)


def _bdot_tn(a, b):
    return lax.dot_general(a.astype(bf16), b.astype(bf16), _dims(a, 0, 0), preferred_element_type=f32)


def _hdot(a, b):
    return lax.dot_general(a, b, _dims(a, 1, 0), precision=HI, preferred_element_type=f32)


def _seg_matrix(n, seg=HEAD_DIM):
    r = lax.broadcasted_iota(jnp.int32, (n, n), 0) // seg
    c = lax.broadcasted_iota(jnp.int32, (n, n), 1) // seg
    return (r == c).astype(f32)


def _layer_norm(y, g, b):
    mu = jnp.mean(y, axis=-1, keepdims=True)
    d = y - mu
    var = jnp.mean(d * d, axis=-1, keepdims=True)
    return d * lax.rsqrt(var + LN_EPS) * g + b


def _row_to_col(row, eye):
    return jnp.sum(jnp.where(eye, jnp.broadcast_to(row, eye.shape), 0.0), axis=2, keepdims=True)


def _stack_heads(x):
    h0 = lax.broadcasted_iota(jnp.int32, x.shape, 2) < HEAD_DIM
    return jnp.concatenate([jnp.where(h0, x, 0.0), jnp.where(h0, 0.0, x)], axis=1)


def _proj_body(x_ref, w_ref, b_ref, wg_ref, bg_ref, att_ref, s5_ref, mqk_ref, mv_ref, mo_ref, rw_ref, g_ref):
    xb = x_ref[...].astype(bf16)
    off = 0
    for o_ref, n in zip((att_ref, s5_ref, mqk_ref, mv_ref, mo_ref, rw_ref), PROJ_SPLITS):
        o_ref[...] = jnp.dot(xb, w_ref[:, off:off + n], preferred_element_type=f32) + b_ref[:, off:off + n]
        off += n
    g_ref[...] = lax.dot_general(wg_ref[...], xb, (((1,), (1,)), ((), ())), preferred_element_type=f32) + bg_ref[...]


def _proj(xt, w, b, wg, bg, tm=512):
    T = xt.shape[0]
    n_tot = sum(PROJ_SPLITS)
    outs = [jax.ShapeDtypeStruct((T, n), f32) for n in PROJ_SPLITS] + [jax.ShapeDtypeStruct((16, T), f32)]
    return pl.pallas_call(
        _proj_body, grid=(T // tm,),
        in_specs=[pl.BlockSpec((tm, D_MODEL), lambda i: (i, 0)),
                  pl.BlockSpec((D_MODEL, n_tot), lambda i: (0, 0)),
                  pl.BlockSpec((1, n_tot), lambda i: (0, 0)),
                  pl.BlockSpec((16, D_MODEL), lambda i: (0, 0)),
                  pl.BlockSpec((16, 1), lambda i: (0, 0))],
        out_specs=[pl.BlockSpec((tm, n), lambda i: (i, 0)) for n in PROJ_SPLITS]
        + [pl.BlockSpec((16, tm), lambda i: (0, i))],
        out_shape=outs, compiler_params=_cparams(("parallel",)), name="proj")(xt, w, b, wg, bg)


def _proj_params(w_in, b_in, ml_ib, ml_fb):
    o = np.cumsum((0, 256, 128, 128, RW_COLS, 256, 512, 256, 8, 8, 256))
    sl = lambda i: (w_in[:, o[i]:o[i + 1]], b_in[o[i]:o[i + 1]])
    (wq, bq), (wk, bk), (wv, bv), (wrw, brw), (ws5, bs5), (wqk, bqk), (wmv, bmv), (wi, bi), (wf, bf), (wo, bo) = (
        sl(i) for i in range(10))
    zw, zb = jnp.zeros((D_MODEL, HEAD_DIM), f32), jnp.zeros((HEAD_DIM,), f32)
    wq_e, bq_e = [], []
    for h in range(ATT_HEADS):
        wh, bh = wq[:, 64 * h:64 * h + 64], bq[64 * h:64 * h + 64]
        wq_e += [wh, zw] if h // 2 == 0 else [zw, wh]
        bq_e += [bh, zb] if h // 2 == 0 else [zb, bh]
    w = jnp.concatenate(wq_e + [wk, wv, ws5, wqk, wmv, wo, wrw], axis=1)
    b = jnp.concatenate(bq_e + [bk, bv, bs5, bqk, bmv, bo, brw])
    wg = jnp.concatenate([wi, wf], axis=1).T
    bg = jnp.concatenate([bi + ml_ib.reshape(-1), bf + ml_fb.reshape(-1)])
    return w.astype(bf16), b[None, :], wg.astype(bf16), bg[:, None]


def _att_prep_body(a_ref, cos_ref, sin_ref, gain_ref, q_ref, k_ref, v_ref):
    x = a_ref[:, 0:640]
    ms = _hdot(x * x, _seg_matrix(640)) * (1.0 / HEAD_DIM)
    xn = x * lax.rsqrt(ms + QK_EPS) * gain_ref[...]
    lane = lax.broadcasted_iota(jnp.int32, xn.shape, 1)
    partner = jnp.where((lane % 32) < 16, pltpu.roll(xn, 640 - 16, 1), pltpu.roll(xn, 16, 1))
    cos = jnp.concatenate([cos_ref[...]] * 5, axis=1)
    sin = jnp.concatenate([sin_ref[...]] * 5, axis=1)
    rot = xn * cos + partner * sin
    q_ref[...] = rot[:, 0:512].astype(bf16)
    k_ref[...] = rot[:, 512:640].astype(bf16)
    v_ref[:, 0:LANES] = a_ref[:, 640:768].astype(bf16)
    v_ref[:, LANES:2 * LANES] = jnp.ones((x.shape[0], LANES), bf16)


def _att_prep(att, cos, sin, gain, B, S, tq=512):
    T = B * S
    nb = S // tq
    return pl.pallas_call(
        _att_prep_body, grid=(B, nb),
        in_specs=[pl.BlockSpec((tq, 768), lambda b, i: (b * nb + i, 0)),
                  pl.BlockSpec((tq, LANES), lambda b, i: (i, 0)),
                  pl.BlockSpec((tq, LANES), lambda b, i: (i, 0)),
                  pl.BlockSpec((1, 640), lambda b, i: (0, 0))],
        out_specs=[pl.BlockSpec((tq, 512), lambda b, i: (b * nb + i, 0)),
                   pl.BlockSpec((tq, LANES), lambda b, i: (b * nb + i, 0)),
                   pl.BlockSpec((tq, 2 * LANES), lambda b, i: (b * nb + i, 0))],
        out_shape=[jax.ShapeDtypeStruct((T, 512), bf16), jax.ShapeDtypeStruct((T, LANES), bf16),
                   jax.ShapeDtypeStruct((T, 2 * LANES), bf16)],
        compiler_params=_cparams(("parallel", "parallel")), name="att_prep")(att, cos, sin, gain)


def _rope_tables(S):
    t = np.arange(S)
    row = (t // GRID_W).astype(np.float32)
    col = (t % GRID_W).astype(np.float32)
    n = 16
    inv = np.power(np.float32(ROPE_THETA), -np.arange(n, dtype=np.float32) / n).astype(np.float32)
    ar = jnp.asarray(row)[:, None] * jnp.asarray(inv)
    ac = jnp.asarray(col)[:, None] * jnp.asarray(inv)
    cos = jnp.concatenate([jnp.cos(ar), jnp.cos(ar), jnp.cos(ac), jnp.cos(ac)], axis=1)
    sin = jnp.concatenate([-jnp.sin(ar), jnp.sin(ar), -jnp.sin(ac), jnp.sin(ac)], axis=1)
    return jnp.concatenate([cos, cos], axis=1), jnp.concatenate([sin, sin], axis=1)


def _flash_body(q_ref, k_ref, v_ref, o_ref, acc_ref, *m_scratch, tk, track_max):
    tq = q_ref.shape[0]
    nk = k_ref.shape[0] // tk
    q2 = jnp.concatenate([q_ref[:, 0:LANES], q_ref[:, LANES:2 * LANES]], axis=0)
    acc_ref[...] = jnp.zeros(acc_ref.shape, f32)
    if track_max:
        m_ref, = m_scratch
        m_ref[...] = jnp.full(m_ref.shape, NEG, f32)

    def step(j, carry):
        rows = pl.ds(pl.multiple_of(j * tk, tk), tk)
        s = lax.dot_general(q2, k_ref[rows, :], (((1,), (1,)), ((), ())), preferred_element_type=f32)
        if track_max:
            m_old = m_ref[...]
            m_new = jnp.maximum(m_old, jnp.max(s, axis=1, keepdims=True))
            p = jnp.exp(s - m_new).astype(bf16)
            acc_ref[...] = jnp.exp(m_old - m_new) * acc_ref[...] + jnp.dot(p, v_ref[rows, :], preferred_element_type=f32)
            m_ref[...] = m_new
        else:
            acc_ref[...] += jnp.dot(jnp.exp(s).astype(bf16), v_ref[rows, :], preferred_element_type=f32)
        return carry

    lax.fori_loop(0, nk, step, 0)
    o = acc_ref[:, 0:LANES] / acc_ref[:, LANES:2 * LANES]
    o_ref[...] = jnp.concatenate([o[0:tq], o[tq:2 * tq]], axis=1)


SCORE_BOUND_MAX = 60.0


def _flash(q, k, v, score_bound, B, S, tq=256, tk=1024):
    T = B * S
    nb = S // tq
    tk = min(tk, S)

    def call(track_max):
        scratch = [pltpu.VMEM((2 * tq, 2 * LANES), f32)] + ([pltpu.VMEM((2 * tq, 1), f32)] if track_max else [])
        return pl.pallas_call(
            functools.partial(_flash_body, tk=tk, track_max=track_max), grid=(B, ATT_KV_HEADS, nb),
            in_specs=[pl.BlockSpec((tq, 2 * LANES), lambda b, g, i: (b * nb + i, g)),
                      pl.BlockSpec((S, LANES), lambda b, g, i: (b, 0)),
                      pl.BlockSpec((S, 2 * LANES), lambda b, g, i: (b, 0))],
            out_specs=pl.BlockSpec((tq, 2 * LANES), lambda b, g, i: (b * nb + i, g)),
            out_shape=jax.ShapeDtypeStruct((T, 512), f32), scratch_shapes=scratch,
            compiler_params=_cparams(("parallel", "parallel", "parallel")),
            name="flash_safe" if track_max else "flash")(q, k, v)

    return lax.cond(score_bound <= SCORE_BOUND_MAX, lambda: call(False), lambda: call(True))


def _halo_specs(width, tm, B, S):
    nb = S // tm
    r8 = tm // 8
    last8 = B * S // 8 - 1

    def main(b, i):
        return (b * nb + i, 0)

    def prev(b, i):
        return (jnp.maximum(b * (S // 8) + i * r8 - 1, 0), 0)

    def nxt(b, i):
        return (jnp.minimum(b * (S // 8) + (i + 1) * r8, last8), 0)

    return [pl.BlockSpec((tm, width), main), pl.BlockSpec((8, width), prev), pl.BlockSpec((8, width), nxt)]


def _fill_halo(buf_ref, x_ref, p_ref, n_ref):
    tm = x_ref.shape[0]
    i = pl.program_id(1)
    last = pl.num_programs(1) - 1
    buf_ref[pl.ds(8, tm), :] = x_ref[...]
    buf_ref[pl.ds(0, 8), :] = jnp.where(i > 0, p_ref[...], 0.0)
    buf_ref[pl.ds(8 + tm, 8), :] = jnp.where(i < last, n_ref[...], 0.0)


def _rw_prep_body(x_ref, p_ref, n_ref, mix_ref, w2_ref, a2_ref, g2_ref, vec_ref,
                  r_ref, k_ref, v_ref, an_ref, bn_ref, lw_ref, gate_ref, bonus_ref, buf_ref):
    tm = x_ref.shape[0]
    _fill_halo(buf_ref, x_ref, p_ref, n_ref)
    x = x_ref[...]
    p = x + mix_ref[0:1, :] * (buf_ref[pl.ds(7, tm), :] - x) + mix_ref[1:2, :] * (buf_ref[pl.ds(9, tm), :] - x)
    r, k, v = p[:, 0:256], p[:, 256:512], p[:, 512:768]
    w0f, w0b, a0, k_k, k_a, r_k = (vec_ref[j:j + 1, :] for j in range(6))
    dec = _bdot(jnp.tanh(p[:, 768:896]), w2_ref[...])
    z = p[:, 896:1088]
    a = _sigmoid(a0 + _bdot(z, a2_ref[...]))
    gate_ref[...] = _bdot(_sigmoid(z), g2_ref[...])
    seg = _seg_matrix(BRANCH_W)
    kk = k * k_k
    kk = kk / jnp.maximum(jnp.sqrt(_hdot(kk * kk, seg)), 1e-12)
    k2 = k * (1.0 + (a - 1.0) * k_a)
    bonus_ref[...] = _hdot(r * k2 * r_k, seg) * v
    r_ref[...] = r
    k_ref[...] = k2
    v_ref[...] = v
    an_ref[...] = -kk
    bn_ref[...] = kk * a
    lw_ref[0] = -jnp.exp(-_softplus(-(w0f + dec[:, 0:256])) - 0.5)
    lw_ref[1] = -jnp.exp(-_softplus(-(w0b + dec[:, 256:512])) - 0.5)


def _rw_prep(rw, mix, w2, a2, g2, vec, B, S, tm=512):
    T = B * S
    nb = S // tm
    o256 = pl.BlockSpec((tm, BRANCH_W), lambda b, i: (b * nb + i, 0))
    full = lambda shape: pl.BlockSpec(shape, lambda b, i: (0,) * len(shape))
    return pl.pallas_call(
        _rw_prep_body, grid=(B, nb),
        in_specs=_halo_specs(RW_COLS, tm, B, S) + [full((2, RW_COLS)), full((128, 512)), full((192, 256)),
                                                   full((192, 256)), full((8, 256))],
        out_specs=[o256] * 5 + [pl.BlockSpec((2, tm, BRANCH_W), lambda b, i: (0, b * nb + i, 0)), o256, o256],
        out_shape=[jax.ShapeDtypeStruct((T, BRANCH_W), f32)] * 5 + [jax.ShapeDtypeStruct((2, T, BRANCH_W), f32)]
        + [jax.ShapeDtypeStruct((T, BRANCH_W), f32)] * 2,
        scratch_shapes=[pltpu.VMEM((tm + 16, RW_COLS), f32)],
        compiler_params=_cparams(("parallel", "parallel")), name="rw_prep")(rw, rw, rw, mix, w2, a2, g2, vec)


def _pair_masks(rev, nbatch):
    n = 2 * CHUNK
    r = lax.broadcasted_iota(jnp.int32, (nbatch, n, n), 1)
    c = lax.broadcasted_iota(jnp.int32, (nbatch, n, n), 2)
    same = (r // CHUNK) == (c // CHUNK)
    if rev:
        return r, c, same & (c > r), same & (c >= r)
    return r, c, same & (c < r), same & (c <= r)


def _rw_chunk(st, r, k, v, an, bn, lw, rev):
    L = CHUNK
    N = r.shape[0]
    ri = lax.broadcasted_iota(jnp.int32, (N, L, L), 1)
    ci = lax.broadcasted_iota(jnp.int32, (N, L, L), 2)
    tri = ((ci >= ri) if rev else (ci <= ri)).astype(f32)
    cs = _hdot(tri, lw)
    tot = jnp.sum(lw, axis=1, keepdims=True)
    e_neg = jnp.exp(-cs)
    a2 = _stack_heads(an * jnp.exp(cs - lw))
    r2 = _stack_heads(r * jnp.exp(cs))
    b2 = _stack_heads(bn * e_neg)
    k2 = _stack_heads(k * e_neg)
    v2 = _stack_heads(v)
    rr, cc, strict, incl = _pair_masks(rev, N)
    mab = jnp.where(strict, _bdot_nt(a2, b2), 0.0)
    mak = jnp.where(strict, _bdot_nt(a2, k2), 0.0)
    pb = jnp.where(incl, _bdot_nt(r2, b2), 0.0)
    pk = jnp.where(incl, _bdot_nt(r2, k2), 0.0)
    eye = rr == cc
    m8 = jnp.where((rr // 8) == (cc // 8), mab, 0.0)
    x = eye.astype(f32) + m8
    p = _bdot(m8, m8)
    x = x + _bdot(x, p)
    p = _bdot(p, p)
    x = x + _bdot(x, p)
    n = 8
    while n < L:
        e = jnp.where(((rr // (2 * n)) == (cc // (2 * n))) & ((rr // n) != (cc // n)), mab, 0.0)
        x = x + _bdot(_bdot(x, e), x)
        n *= 2
    w = _bdot(x, a2)
    u0 = _bdot(x, _bdot(mak, v2))
    rh = r2 + _bdot(pb, w)
    y2 = _bdot(pb, u0) + _bdot(pk, v2) + _bdot(rh, st)
    y = y2[:, 0:L] + y2[:, L:2 * L]
    gam = _row_to_col(jnp.exp(tot), eye)
    st = gam * (st + _bdot(_bdot_tn(b2, w), st) + _bdot_tn(b2, u0) + _bdot_tn(k2, v2))
    return y, st


def _load_pairs(ref, rows):
    return jnp.concatenate([ref[:, rows, 0:LANES], ref[:, rows, LANES:2 * LANES]], axis=0)


def _store_pairs(ref, rows, y):
    nb = ref.shape[0]
    ref[:, rows, 0:LANES] = y[0:nb]
    ref[:, rows, LANES:2 * LANES] = y[nb:2 * nb]


def _rw_scan_body(r_ref, k_ref, v_ref, an_ref, bn_ref, lw_ref, y_ref, st_ref, *, rev):
    nch = r_ref.shape[1] // CHUNK

    @pl.when(pl.program_id(0) == 0)
    def _():
        st_ref[...] = jnp.zeros(st_ref.shape, f32)

    def step(cc, carry):
        c = (nch - 1 - cc) if rev else cc
        rows = pl.ds(pl.multiple_of(c * CHUNK, CHUNK), CHUNK)
        y, st = _rw_chunk(st_ref[...], *(_load_pairs(ref, rows) for ref in (r_ref, k_ref, v_ref, an_ref, bn_ref)),
                          _load_pairs(lw_ref.at[0], rows), rev)
        _store_pairs(y_ref, rows, y)
        st_ref[...] = st
        return carry

    lax.fori_loop(0, nch, step, 0)


def _rw_scan(r, k, v, an, bn, lw, B, S, rev, ts=256):
    nb = S // ts
    d = 1 if rev else 0
    blk = (lambda i: nb - 1 - i) if rev else (lambda i: i)
    io = pl.BlockSpec((B, ts, BRANCH_W), lambda i: (0, blk(i), 0))
    return pl.pallas_call(
        functools.partial(_rw_scan_body, rev=rev), grid=(nb,),
        in_specs=[io] * 5 + [pl.BlockSpec((1, B, ts, BRANCH_W), lambda i: (d, 0, blk(i), 0))],
        out_specs=io, out_shape=jax.ShapeDtypeStruct((B, S, BRANCH_W), f32),
        scratch_shapes=[pltpu.VMEM((B * BRANCH_W // LANES, LANES, LANES), f32)],
        compiler_params=_cparams(("arbitrary",)),
        name="rw_scan_bwd" if rev else "rw_scan_fwd")(r, k, v, an, bn, lw)


def _rw_finish_body(yf_ref, yb_ref, gate_ref, bonus_ref, gn_ref, o_ref):
    y = yf_ref[...] + yb_ref[...]
    seg = _seg_matrix(BRANCH_W)
    mu = _hdot(y, seg) * (1.0 / HEAD_DIM)
    d = y - mu
    var = _hdot(d * d, seg) * (1.0 / HEAD_DIM)
    yn = d * lax.rsqrt(var + RW_GN_EPS) * gn_ref[0:1, :] + gn_ref[1:2, :]
    o_ref[...] = (yn + bonus_ref[...]) * gate_ref[...]


def _rw_finish(yf, yb, gate, bonus, gn, tm=1024):
    T = yf.shape[0]
    io = pl.BlockSpec((tm, BRANCH_W), lambda i: (i, 0))
    return pl.pallas_call(
        _rw_finish_body, grid=(T // tm,), in_specs=[io] * 4 + [pl.BlockSpec((2, BRANCH_W), lambda i: (0, 0))],
        out_specs=io, out_shape=jax.ShapeDtypeStruct((T, BRANCH_W), f32),
        compiler_params=_cparams(("parallel",)), name="rw_finish")(yf, yb, gate, bonus, gn)


def _rwkv(rw, prm, B, S):
    mix, w2, a2, g2, vec, gn = prm
    r, k, v, an, bn, lw, gate, bonus = _rw_prep(rw, mix, w2, a2, g2, vec, B, S)
    seq = [t.reshape(B, S, BRANCH_W) for t in (r, k, v, an, bn)] + [lw.reshape(2, B, S, BRANCH_W)]
    yf = _rw_scan(*seq, B, S, rev=False).reshape(B * S, BRANCH_W)
    yb = _rw_scan(*seq, B, S, rev=True).reshape(B * S, BRANCH_W)
    return _rw_finish(yf, yb, gate, bonus, gn)


def _rw_params(mix, w0, w2, a0, a2, g2, k_k, k_a, r_k, gn_g, gn_b):
    z = jnp.zeros((64, 256), f32)
    w2c = jnp.concatenate([jnp.concatenate([w2[0], z], axis=1), jnp.concatenate([z, w2[1]], axis=1)], axis=0)
    a2p = jnp.concatenate([a2, jnp.zeros((128, 256), f32)], axis=0)
    g2p = jnp.concatenate([jnp.zeros((64, 256), f32), g2], axis=0)
    vec = jnp.stack([w0[0], w0[1], a0, k_k, k_a, r_k.reshape(-1), jnp.zeros_like(a0), jnp.zeros_like(a0)])
    return mix, w2c.astype(bf16), a2p.astype(bf16), g2p.astype(bf16), vec, jnp.stack([gn_g, gn_b])


def _s5_scan_body(u_ref, bcat_ref, cre_ref, cim_ref, lre_ref, lim_ref, y_ref, sre_ref, sim_ref, bre_ref, bim_ref):
    ts = u_ref.shape[0]
    n = S5_GROUPS * S5_STATE

    @pl.when(pl.program_id(0) == 0)
    def _():
        sre_ref[...] = jnp.zeros(sre_ref.shape, f32)
        sim_ref[...] = jnp.zeros(sim_ref.shape, f32)

    u = u_ref[...].reshape(ts * 8, BRANCH_W).astype(bf16)
    fwd = lax.broadcasted_iota(jnp.int32, (ts, 8, n), 1) < 4

    def bu(part):
        return jnp.dot(u, bcat_ref[:, part * n:(part + 1) * n], preferred_element_type=f32).reshape(ts, 8, n)

    bre_ref[...] = jnp.where(fwd, bu(0), bu(2))
    bim_ref[...] = jnp.where(fwd, bu(1), bu(3))
    lre = lre_ref[...]
    lim = lim_ref[...]

    def step(t, carry):
        sre, sim = carry
        nre = lre * sre - lim * sim + bre_ref[t]
        nim = lre * sim + lim * sre + bim_ref[t]
        bre_ref[t] = nre
        bim_ref[t] = nim
        return nre, nim

    sre, sim = lax.fori_loop(0, ts, step, (sre_ref[...], sim_ref[...]), unroll=4)
    sre_ref[...] = sre
    sim_ref[...] = sim
    y = (_bdot(bre_ref[...].reshape(ts * 8, n), cre_ref[...])
         - _bdot(bim_ref[...].reshape(ts * 8, n), cim_ref[...])).reshape(ts, 8, 2 * BRANCH_W)
    fwd_o = lax.broadcasted_iota(jnp.int32, (ts, 8, BRANCH_W), 1) < 4
    y_ref[...] = jnp.where(fwd_o, y[:, :, 0:BRANCH_W], y[:, :, BRANCH_W:2 * BRANCH_W])


def _s5_scan(u2, bcat, cre, cim, lre, lim, ts=128):
    S = u2.shape[0]
    n = S5_GROUPS * S5_STATE
    full = lambda shape: pl.BlockSpec(shape, lambda i: (0,) * len(shape))
    return pl.pallas_call(
        _s5_scan_body, grid=(S // ts,),
        in_specs=[pl.BlockSpec((ts, 8, BRANCH_W), lambda i: (i, 0, 0)), full((BRANCH_W, 4 * n)),
                  full((n, 2 * BRANCH_W)), full((n, 2 * BRANCH_W)), full((8, n)), full((8, n))],
        out_specs=pl.BlockSpec((ts, 8, BRANCH_W), lambda i: (i, 0, 0)),
        out_shape=jax.ShapeDtypeStruct((S, 8, BRANCH_W), f32),
        scratch_shapes=[pltpu.VMEM((8, n), f32), pltpu.VMEM((8, n), f32),
                        pltpu.VMEM((ts, 8, n), f32), pltpu.VMEM((ts, 8, n), f32)],
        compiler_params=_cparams(("arbitrary",)), name="s5_scan")(u2, bcat, cre, cim, lre, lim)


def _s5_finish_body(y_ref, u_ref, d_ref, w_ref, b_ref, o_ref):
    y = y_ref[...] + u_ref[...] * d_ref[...]
    y = 0.5 * y * (1.0 + jnp.tanh(math.sqrt(2.0 / math.pi) * (y + 0.044715 * (y * y * y))))
    o_ref[...] = y * _sigmoid(_bdot(y, w_ref[...]) + b_ref[...])


def _s5_finish(y, u, d, w, b, tm=1024):
    T = y.shape[0]
    io = pl.BlockSpec((tm, BRANCH_W), lambda i: (i, 0))
    vec = pl.BlockSpec((1, BRANCH_W), lambda i: (0, 0))
    return pl.pallas_call(
        _s5_finish_body, grid=(T // tm,),
        in_specs=[io, io, vec, pl.BlockSpec((BRANCH_W, BRANCH_W), lambda i: (0, 0)), vec],
        out_specs=io, out_shape=jax.ShapeDtypeStruct((T, BRANCH_W), f32),
        compiler_params=_cparams(("parallel",)), name="s5_finish")(y, u, d, w, b)


def _s5_params(lam_re, lam_im, log_dt, b_re, b_im, c_re, c_im):
    G, P, C = S5_GROUPS, S5_STATE, S5_GROUP
    eye = jnp.eye(G, dtype=f32)
    b_c = lax.complex(b_re, b_im)
    bcat, cre, cim, lre, lim = [], [], [], [], []
    for d in range(2):
        lam = lax.complex(jnp.minimum(lam_re[d], -1e-4), lam_im[d])
        lam_bar = jnp.exp(lam * jnp.exp(log_dt[d])[:, None])
        b_bar = ((lam_bar - 1.0) / lam)[..., None] * b_c
        for part in (jnp.real(b_bar), jnp.imag(b_bar)):
            bcat.append(jnp.einsum('gh,gpc->gchp', eye, part).reshape(G * C, G * P))
        cre.append(jnp.einsum('gh,gcp->gphc', eye, c_re[d]).reshape(G * P, G * C))
        cim.append(jnp.einsum('gh,gcp->gphc', eye, c_im[d]).reshape(G * P, G * C))
        lre.append(jnp.broadcast_to(jnp.real(lam_bar).reshape(1, G * P), (4, G * P)))
        lim.append(jnp.broadcast_to(jnp.imag(lam_bar).reshape(1, G * P), (4, G * P)))
    return (jnp.concatenate(bcat, axis=1).astype(bf16), jnp.concatenate(cre, axis=1).astype(bf16),
            jnp.concatenate(cim, axis=1).astype(bf16), jnp.concatenate(lre, axis=0), jnp.concatenate(lim, axis=0))


def _s5(u, prm, fin, B, S):
    u3 = u.reshape(B, S, BRANCH_W).transpose(1, 0, 2)
    u2 = jnp.concatenate([u3, u3[::-1]], axis=1)
    y2 = _s5_scan(u2, *prm)
    y = (y2[:, 0:B] + y2[::-1, B:2 * B]).transpose(1, 0, 2).reshape(B * S, BRANCH_W)
    return _s5_finish(y, u, *fin)


def _ml_prep_body(x_ref, p_ref, n_ref, w_ref, b_ref, q_ref, k_ref, buf_ref):
    tm = x_ref.shape[0]
    _fill_halo(buf_ref, x_ref, p_ref, n_ref)
    y = b_ref[...] + w_ref[2:3, :] * x_ref[...]
    for j in (0, 1, 3, 4):
        y = y + w_ref[j:j + 1, :] * buf_ref[pl.ds(6 + j, tm), :]
    y = y * _sigmoid(y)
    q_ref[...] = y[:, 0:BRANCH_W]
    k_ref[...] = y[:, BRANCH_W:2 * BRANCH_W] * (HEAD_DIM ** -0.5)


def _ml_prep(mqk, w, b, B, S, tm=512):
    T = B * S
    nb = S // tm
    o = pl.BlockSpec((tm, BRANCH_W), lambda b_, i: (b_ * nb + i, 0))
    return pl.pallas_call(
        _ml_prep_body, grid=(B, nb),
        in_specs=_halo_specs(512, tm, B, S) + [pl.BlockSpec((5, 512), lambda b_, i: (0, 0)),
                                               pl.BlockSpec((1, 512), lambda b_, i: (0, 0))],
        out_specs=[o, o], out_shape=[jax.ShapeDtypeStruct((T, BRANCH_W), f32)] * 2,
        scratch_shapes=[pltpu.VMEM((tm + 16, 512), f32)],
        compiler_params=_cparams(("parallel", "parallel")), name="ml_prep")(mqk, mqk, mqk, w, b)


def _ml_chunk(state, q, k, v, li, lfp, rev):
    c2, n_row, m_row = state
    L = CHUNK
    N = q.shape[0]
    rr, cc, _, incl = _pair_masks(rev, N)
    same = (rr // L) == (cc // L)
    eye = rr == cc
    lane = lax.broadcasted_iota(jnp.int32, (N, 1, 2 * L), 2)
    lf = jnp.minimum(lfp, 0.0) - jnp.log(1.0 + jnp.exp(-jnp.abs(lfp)))
    lf8 = jnp.broadcast_to(lf, (N, 8, 2 * L))
    cum = (same & ((rr >= cc) if rev else (rr <= cc))).astype(f32)
    b_row = _hdot(lf8, cum)[:, 0:1]
    g_row = _hdot(lf8, same.astype(f32))[:, 0:1]
    w_end = g_row - b_row + li
    m0 = jnp.max(jnp.where(lane < L, w_end, NEG), axis=2, keepdims=True)
    m1 = jnp.max(jnp.where(lane < L, NEG, w_end), axis=2, keepdims=True)
    m_loc = jnp.where(lane < L, m0, m1)
    e_col = _row_to_col(jnp.exp(w_end - m_loc), eye)
    b_col = _row_to_col(b_row, eye)
    q2, k2, v2 = _stack_heads(q), _stack_heads(k), _stack_heads(v)
    log_inter = b_col + _row_to_col(m_row, eye)
    log_intra = jnp.where(incl, b_col - b_row + li, NEG)
    m_r = jnp.maximum(log_inter, jnp.max(log_intra, axis=2, keepdims=True))
    s = _bdot_nt(q2, k2) * jnp.exp(log_intra - m_r)
    inter = jnp.exp(log_inter - m_r)
    num = _bdot(s, v2) + inter * _bdot(q2, c2)
    den = jnp.sum(s, axis=2, keepdims=True) + inter * jnp.sum(q2 * n_row, axis=2, keepdims=True)
    h2 = num / jnp.maximum(jnp.abs(den), jnp.exp(-m_r))
    h = h2[:, 0:L] + h2[:, L:2 * L]
    m_new = jnp.maximum(g_row + m_row, m_loc)
    a = jnp.exp(g_row + m_row - m_new)
    bb = jnp.exp(m_loc - m_new)
    ek = e_col * k2
    c2 = a * c2 + bb * _bdot_tn(ek, v2)
    n_row = a * n_row + bb * jnp.sum(ek, axis=1, keepdims=True)
    return h, (c2, n_row, m_new)


def _ml_scan_body(q_ref, k_ref, v_ref, g_ref, h_ref, c_ref, n_ref, m_ref, *, rev):
    nch = q_ref.shape[1] // CHUNK

    @pl.when(pl.program_id(0) == 0)
    def _():
        c_ref[...] = jnp.zeros(c_ref.shape, f32)
        n_ref[...] = jnp.zeros(n_ref.shape, f32)
        m_ref[...] = jnp.zeros(m_ref.shape, f32)

    def step(cc, carry):
        c = (nch - 1 - cc) if rev else cc
        rows = pl.ds(pl.multiple_of(c * CHUNK, CHUNK), CHUNK)
        li, lfp = (jnp.concatenate([g_ref[t, 0, 0, :, pl.ds(c, 1), :], g_ref[t, 0, 1, :, pl.ds(c, 1), :]], axis=0)
                   for t in range(2))
        h, (c2, n_row, m_row) = _ml_chunk((c_ref[...], n_ref[...], m_ref[...]), _load_pairs(q_ref, rows),
                                          _load_pairs(k_ref, rows), _load_pairs(v_ref, rows), li, lfp, rev)
        _store_pairs(h_ref, rows, h)
        c_ref[...] = c2
        n_ref[...] = n_row
        m_ref[...] = m_row
        return carry

    lax.fori_loop(0, nch, step, 0)


def _ml_scan(q, k, v, g, B, S, rev, ts=512):
    nb = S // ts
    nch = ts // CHUNK
    d = 1 if rev else 0
    nchain = B * BRANCH_W // LANES
    blk = (lambda i: nb - 1 - i) if rev else (lambda i: i)
    io = pl.BlockSpec((B, ts, BRANCH_W), lambda i: (0, blk(i), 0))
    return pl.pallas_call(
        functools.partial(_ml_scan_body, rev=rev), grid=(nb,),
        in_specs=[io, io, io, pl.BlockSpec((2, 1, 2, B, nch, LANES), lambda i: (0, d, 0, 0, blk(i), 0))],
        out_specs=io, out_shape=jax.ShapeDtypeStruct((B, S, BRANCH_W), f32),
        scratch_shapes=[pltpu.VMEM((nchain, LANES, LANES), f32), pltpu.VMEM((nchain, 1, LANES), f32),
                        pltpu.VMEM((nchain, 1, LANES), f32)],
        compiler_params=_cparams(("arbitrary",)),
        name="ml_scan_bwd" if rev else "ml_scan_fwd")(q, k, v, g)


def _mlstm(mqk, mv, gt, conv_w, conv_b, B, S):
    q, k = _ml_prep(mqk, conv_w, conv_b, B, S)
    g = gt.reshape(2, 2, 2, 2, B, S // CHUNK, CHUNK).transpose(0, 1, 2, 4, 5, 3, 6).reshape(2, 2, 2, B, S // CHUNK, LANES)
    seq = [t.reshape(B, S, BRANCH_W) for t in (q, k, mv)]
    return (_ml_scan(*seq, g, B, S, rev=False).reshape(B * S, BRANCH_W),
            _ml_scan(*seq, g, B, S, rev=True).reshape(B * S, BRANCH_W))


def _merge_body(x_ref, att_ref, rw_ref, s5_ref, hf_ref, hb_ref, mo_ref, wg_ref, bg_ref, wba_ref, wb_ref, wo_ref,
                ln_ref, o_ref):
    x = x_ref[...]
    xb = x.astype(bf16)
    ml = _sigmoid(mo_ref[...]) * (hf_ref[...] + hb_ref[...])
    branches = (att_ref[...], rw_ref[...], s5_ref[...], ml)
    merged = None
    for n in range(4):
        gate = _sigmoid(jnp.dot(xb, wg_ref[n], preferred_element_type=f32) + bg_ref[n:n + 1, :])
        wide = _bdot(branches[n], wba_ref[...] if n == 0 else wb_ref[n - 1])
        merged = gate * wide if merged is None else merged + gate * wide
    y = ALPHA * x + _bdot(merged, wo_ref[...])
    o_ref[...] = _layer_norm(y, ln_ref[0:1, :], ln_ref[1:2, :])


def _merge(xt, att, rw, s5, hf, hb, mo, wg, bg, wba, wb, wo, ln, tm=256):
    T = xt.shape[0]
    row = lambda n: pl.BlockSpec((tm, n), lambda i: (i, 0))
    const = lambda shape: pl.BlockSpec(shape, lambda i: (0,) * len(shape), pipeline_mode=pl.Buffered(1))
    return pl.pallas_call(
        _merge_body, grid=(T // tm,),
        in_specs=[row(D_MODEL), row(512)] + [row(BRANCH_W)] * 5
        + [const((4, D_MODEL, D_MODEL)), const((4, D_MODEL)), const((512, D_MODEL)), const((3, BRANCH_W, D_MODEL)),
           const((D_MODEL, D_MODEL)), const((2, D_MODEL))],
        out_specs=row(D_MODEL), out_shape=jax.ShapeDtypeStruct((T, D_MODEL), f32),
        compiler_params=_cparams(("parallel",)), name="merge")(xt, att, rw, s5, hf, hb, mo, wg, bg, wba, wb, wo, ln)


def _att_branch_weight(wb):
    z = jnp.zeros((HEAD_DIM, D_MODEL), f32)
    parts = []
    for h in range(ATT_HEADS):
        wh = wb[64 * h:64 * h + 64]
        parts += [wh, z] if h // 2 == 0 else [z, wh]
    return jnp.concatenate(parts, axis=0)


def _ffn_body(x_ref, w1_ref, w3_ref, w2_ref, ln_ref, o_ref, acc_ref):
    j = pl.program_id(1)
    xb = x_ref[...].astype(bf16)
    h1 = jnp.dot(xb, w1_ref[...], preferred_element_type=f32)
    h3 = jnp.dot(xb, w3_ref[...], preferred_element_type=f32)
    part = _bdot(h1 * _sigmoid(h1) * h3, w2_ref[...])

    @pl.when(j == 0)
    def _():
        acc_ref[...] = part

    @pl.when(j > 0)
    def _():
        acc_ref[...] += part

    @pl.when(j == pl.num_programs(1) - 1)
    def _():
        o_ref[...] = _layer_norm(ALPHA * x_ref[...] + acc_ref[...], ln_ref[0:1, :], ln_ref[1:2, :])


def _ffn(xt, w1, w3, w2, ln, tm=512, tf=1408):
    T = xt.shape[0]
    dff = w1.shape[1]
    return pl.pallas_call(
        _ffn_body, grid=(T // tm, dff // tf),
        in_specs=[pl.BlockSpec((tm, D_MODEL), lambda i, j: (i, 0)), pl.BlockSpec((D_MODEL, tf), lambda i, j: (0, j)),
                  pl.BlockSpec((D_MODEL, tf), lambda i, j: (0, j)), pl.BlockSpec((tf, D_MODEL), lambda i, j: (j, 0)),
                  pl.BlockSpec((2, D_MODEL), lambda i, j: (0, 0))],
        out_specs=pl.BlockSpec((tm, D_MODEL), lambda i, j: (i, 0)),
        out_shape=jax.ShapeDtypeStruct((T, D_MODEL), f32), scratch_shapes=[pltpu.VMEM((tm, D_MODEL), f32)],
        compiler_params=_cparams(("parallel", "arbitrary")), name="ffn")(xt, w1, w3, w2, ln)


MOE_TILE = 512
SC_WINDOW = 128
SC_WORDS = 256


def _router_body(x_ref, rt_ref, xb_ref, meta_ref, cnt_ref, run_ref):
    tb = x_ref.shape[0]

    @pl.when(pl.program_id(0) == 0)
    def _():
        run_ref[...] = jnp.zeros(run_ref.shape, f32)

    x = x_ref[...]
    xb_ref[...] = x.astype(bf16)
    logits = _hdot(x, rt_ref[...])
    lane = lax.broadcasted_iota(jnp.int32, logits.shape, 1)
    lg = jnp.where(lane < N_EXPERTS, logits, NEG)
    v1 = jnp.max(lg, axis=1, keepdims=True)
    i1 = jnp.min(jnp.where(lg == v1, lane, LANES), axis=1, keepdims=True)
    lg2 = jnp.where(lane == i1, NEG, lg)
    v2 = jnp.max(lg2, axis=1, keepdims=True)
    i2 = jnp.min(jnp.where(lg2 == v2, lane, LANES), axis=1, keepdims=True)
    e2 = jnp.exp(v2 - v1)
    sel1, sel2 = lane == i1, lane == i2
    mask = (sel1 | sel2).astype(f32)
    r = lax.broadcasted_iota(jnp.int32, (tb, tb), 0)
    c = lax.broadcasted_iota(jnp.int32, (tb, tb), 1)
    rank = _bdot((c < r).astype(f32), mask) + run_ref[0:1, :]
    run_ref[...] = run_ref[...] + jnp.sum(mask, axis=0, keepdims=True)
    rank1 = jnp.sum(jnp.where(sel1, rank, 0.0), axis=1, keepdims=True)
    rank2 = jnp.sum(jnp.where(sel2, rank, 0.0), axis=1, keepdims=True)
    cols = (i1.astype(f32), i2.astype(f32), rank1, rank2, 1.0 / (1.0 + e2), e2 / (1.0 + e2))
    meta = jnp.zeros(logits.shape, f32)
    for n, col in enumerate(cols):
        meta = jnp.where(lane == n, col, meta)
    meta_ref[...] = meta
    cnt_ref[...] = run_ref[...]


def _router(xt, router, tb=1024):
    T = xt.shape[0]
    tb = min(tb, T)
    return pl.pallas_call(
        _router_body, grid=(T // tb,),
        in_specs=[pl.BlockSpec((tb, D_MODEL), lambda i: (i, 0)), pl.BlockSpec((D_MODEL, LANES), lambda i: (0, 0))],
        out_specs=[pl.BlockSpec((tb, D_MODEL), lambda i: (i, 0)), pl.BlockSpec((tb, LANES), lambda i: (i, 0)),
                   pl.BlockSpec((8, LANES), lambda i: (0, 0))],
        out_shape=[jax.ShapeDtypeStruct((T, D_MODEL), bf16), jax.ShapeDtypeStruct((T, LANES), f32),
                   jax.ShapeDtypeStruct((8, LANES), f32)],
        scratch_shapes=[pltpu.VMEM((8, LANES), f32)],
        compiler_params=_cparams(("arbitrary",)), name="moe_router")(xt, router)


def _sc_gather(table, idx):
    n = idx.shape[0]
    mesh = plsc.VectorSubcoreMesh(core_axis_name="c", subcore_axis_name="s")

    @functools.partial(pl.kernel, out_type=jax.ShapeDtypeStruct((n, SC_WORDS), table.dtype), mesh=mesh)
    def gather(x_hbm, i_hbm, o_hbm):
        def body(i_vmem, o_vmem):
            pltpu.sync_copy(x_hbm.at[i_vmem.at[0]], o_vmem)

        pltpu.emit_pipeline(
            body, grid=(n // SC_WINDOW,),
            in_specs=[pl.BlockSpec((1, SC_WINDOW), index_map=lambda i: (0, i))],
            out_specs=[pl.BlockSpec((SC_WINDOW, SC_WORDS), index_map=lambda i: (i, 0))],
            core_axis_name="s", dimension_semantics=(pltpu.PARALLEL,))(i_hbm, o_hbm)

    return gather(table, idx.reshape(1, n))


def _gather_rows(rows_bf16, idx):
    R = rows_bf16.shape[0]
    words = lax.bitcast_convert_type(rows_bf16.reshape(R, D_MODEL // 2, 2), jnp.int32).reshape(2 * R, SC_WORDS)
    idx2 = (2 * idx[:, None] + jnp.arange(2, dtype=jnp.int32)[None, :]).reshape(-1)
    out = _sc_gather(words, idx2).reshape(idx.shape[0], D_MODEL // 2)
    return lax.bitcast_convert_type(out, bf16).reshape(idx.shape[0], D_MODEL)


def _experts_body(te_ref, nt_ref, x_ref, w1_ref, w3_ref, w2_ref, o_ref, acc_ref):
    i = pl.program_id(0)
    j = pl.program_id(1)

    @pl.when(i < nt_ref[0])
    def _():
        x = x_ref[...]
        h1 = jnp.dot(x, w1_ref[0], preferred_element_type=f32)
        h3 = jnp.dot(x, w3_ref[0], preferred_element_type=f32)
        part = _bdot(h1 * _sigmoid(h1) * h3, w2_ref[0])

        @pl.when(j == 0)
        def _():
            acc_ref[...] = part

        @pl.when(j > 0)
        def _():
            acc_ref[...] += part

    @pl.when(j == pl.num_programs(1) - 1)
    def _():
        o_ref[...] = jnp.where(i < nt_ref[0], acc_ref[...], 0.0).astype(bf16)


def _experts(xs, tile_expert, n_tiles, w1, w3, w2, tf=512):
    P = xs.shape[0]
    dff = w1.shape[2]
    grid_spec = pltpu.PrefetchScalarGridSpec(
        num_scalar_prefetch=2, grid=(P // MOE_TILE, dff // tf),
        in_specs=[pl.BlockSpec((MOE_TILE, D_MODEL), lambda i, j, te, nt: (i, 0)),
                  pl.BlockSpec((1, D_MODEL, tf), lambda i, j, te, nt: (te[i], 0, j)),
                  pl.BlockSpec((1, D_MODEL, tf), lambda i, j, te, nt: (te[i], 0, j)),
                  pl.BlockSpec((1, tf, D_MODEL), lambda i, j, te, nt: (te[i], j, 0))],
        out_specs=pl.BlockSpec((MOE_TILE, D_MODEL), lambda i, j, te, nt: (i, 0)),
        scratch_shapes=[pltpu.VMEM((MOE_TILE, D_MODEL), f32)])
    return pl.pallas_call(
        _experts_body, grid_spec=grid_spec, out_shape=jax.ShapeDtypeStruct((P, D_MODEL), bf16),
        compiler_params=_cparams(("parallel", "arbitrary")), name="moe_experts")(tile_expert, n_tiles, xs, w1, w3, w2)


def _combine_body(x_ref, y0_ref, y1_ref, meta_ref, ln_ref, o_ref):
    meta = meta_ref[...]
    ff = meta[:, 4:5] * y0_ref[0].astype(f32) + meta[:, 5:6] * y1_ref[0].astype(f32)
    o_ref[...] = _layer_norm(ALPHA * x_ref[...] + ff, ln_ref[0:1, :], ln_ref[1:2, :])


def _combine(xt, yg, meta, ln, tm=1024):
    T = xt.shape[0]
    tm = min(tm, T)
    row = pl.BlockSpec((tm, D_MODEL), lambda i: (i, 0))
    return pl.pallas_call(
        _combine_body, grid=(T // tm,),
        in_specs=[row, pl.BlockSpec((1, tm, D_MODEL), lambda i: (0, i, 0)),
                  pl.BlockSpec((1, tm, D_MODEL), lambda i: (1, i, 0)), pl.BlockSpec((tm, LANES), lambda i: (i, 0)),
                  pl.BlockSpec((2, D_MODEL), lambda i: (0, 0))],
        out_specs=row, out_shape=jax.ShapeDtypeStruct((T, D_MODEL), f32),
        compiler_params=_cparams(("parallel",)), name="moe_combine")(xt, yg, yg, meta, ln)


def _moe(xt, router, w1, w3, w2, ln):
    T = xt.shape[0]
    xb, meta, cnt = _router(xt, router)
    counts = cnt[0, :N_EXPERTS].astype(jnp.int32)
    tiles = (counts + MOE_TILE - 1) // MOE_TILE
    tile_end = jnp.cumsum(tiles)
    offset = (tile_end - tiles) * MOE_TILE
    expert = meta[:, 0:2].astype(jnp.int32)
    pos = offset[expert] + meta[:, 2:4].astype(jnp.int32)
    P = 2 * T + N_EXPERTS * MOE_TILE
    token = jnp.broadcast_to(jnp.arange(T, dtype=jnp.int32)[:, None], (T, 2))
    tok_sorted = jnp.zeros((P,), jnp.int32).at[pos.reshape(-1)].set(token.reshape(-1))
    tile_id = jnp.arange(P // MOE_TILE, dtype=jnp.int32)
    tile_expert = jnp.minimum(jnp.sum((tile_id[:, None] >= tile_end[None, :]).astype(jnp.int32), axis=1), N_EXPERTS - 1)
    xs = _gather_rows(xb, tok_sorted)
    ys = _experts(xs, tile_expert, tile_end[-1:], w1, w3, w2)
    yg = _gather_rows(ys, pos.T.reshape(-1)).reshape(2, T, D_MODEL)
    return _combine(xt, yg, meta, ln)


def kernel(x, w_in, b_in, att_gq, att_gk, rw_mix, rw_w0, rw_w2, rw_a0, rw_a2, rw_g2, rw_kk, rw_ka, rw_rk, rw_ln_g, rw_ln_b, s5_lam_re, s5_lam_im, s5_log_dt, s5_b_re, s5_b_im, s5_c_re, s5_c_im, s5_d, s5_glu_w, s5_glu_b, ml_conv_w, ml_conv_b, ml_ib, ml_fb, w_gate, b_gate, w_branch, w_out, ln1_g, ln1_b, ffn_w1, ffn_w3, ffn_w2, moe_router, moe_w1, moe_w3, moe_w2, ln2_g, ln2_b):
    B, S, D = x.shape
    xt = x.reshape(B * S, D)
    cos, sin = _rope_tables(S)
    for l in range(DEPTH):
        att, s5u, mqk, mv, mo, rw, gt = _proj(xt, *_proj_params(w_in[l], b_in[l], ml_ib[l], ml_fb[l]))
        gain = jnp.concatenate([jnp.tile(att_gq[l], 8) * (HEAD_DIM ** -0.5), jnp.tile(att_gk[l], 2)])[None, :]
        q, k, v = _att_prep(att, cos, sin, gain, B, S)
        score_bound = 8.1 * jnp.max(jnp.abs(att_gq[l])) * jnp.max(jnp.abs(att_gk[l]))
        o_att = _flash(q, k, v, score_bound, B, S)
        o_rw = _rwkv(rw, _rw_params(rw_mix[l], rw_w0[l], rw_w2[l], rw_a0[l], rw_a2[l], rw_g2[l], rw_kk[l], rw_ka[l],
                                    rw_rk[l], rw_ln_g[l], rw_ln_b[l]), B, S)
        o_s5 = _s5(s5u, _s5_params(s5_lam_re[l], s5_lam_im[l], s5_log_dt[l], s5_b_re[l], s5_b_im[l], s5_c_re[l],
                                   s5_c_im[l]),
                   (s5_d[l][None, :], s5_glu_w[l].astype(bf16), s5_glu_b[l][None, :]), B, S)
        hf, hb = _mlstm(mqk, mv, gt, ml_conv_w[l], ml_conv_b[l][None, :], B, S)
        xt = _merge(xt, o_att, o_rw, o_s5, hf, hb, mo, w_gate[l].astype(bf16), b_gate[l],
                    _att_branch_weight(w_branch[l, 0]).astype(bf16), w_branch[l, 1:].astype(bf16),
                    w_out[l].astype(bf16), jnp.stack([ln1_g[l], ln1_b[l]]))
        ln2 = jnp.stack([ln2_g[l], ln2_b[l]])
        if l % 2 == 0:
            xt = _ffn(xt, ffn_w1[l // 2].astype(bf16), ffn_w3[l // 2].astype(bf16), ffn_w2[l // 2].astype(bf16), ln2)
        else:
            router = jnp.pad(moe_router[l // 2], ((0, 0), (0, LANES - N_EXPERTS)))
            xt = _moe(xt, router, moe_w1[l // 2].astype(bf16), moe_w3[l // 2].astype(bf16),
                      moe_w2[l // 2].astype(bf16), ln2)
    return xt.reshape(B, S, D)
```

```python
import functools
import math

import jax
import jax.numpy as jnp
import numpy as np
from jax import lax
from jax.experimental import pallas as pl
from jax.experimental.pallas import tpu as pltpu
from jax.experimental.pallas import tpu_sc as plsc

f32 = jnp.float32
bf16 = jnp.bfloat16
HI = lax.Precision.HIGHEST

D_MODEL = 1024
DEPTH = 2
GRID_W = 64
BRANCH_W = 256
HEAD_DIM = 64
ATT_HEADS = 4
ATT_KV_HEADS = 2
ROPE_THETA = 10000.0
QK_EPS = 1e-6
RW_GN_EPS = 64e-5
RW_COLS = 1088
S5_GROUP = 16
S5_GROUPS = 16
S5_STATE = 64
ML_HEADS = 4
N_EXPERTS = 8
ALPHA = (2 * DEPTH) ** 0.25
LN_EPS = 1e-5

LANES = 128
CHUNK = 64
NEG = -1e30
VMEM_LIMIT = 56 * 1024 * 1024

PROJ_SPLITS = (768, 256, 512, 256, 256, RW_COLS)


def _cparams(sem):
    return pltpu.CompilerParams(dimension_semantics=sem, vmem_limit_bytes=VMEM_LIMIT)


def _sigmoid(x):
    return 1.0 / (1.0 + jnp.exp(-x))


def _softplus(x):
    return jnp.maximum(x, 0.0) + jnp.log(1.0 + jnp.exp(-jnp.abs(x)))


def _dims(a, lhs_c, rhs_c):
    lead = a.ndim - 2
    batch = tuple(range(lead))
    return (((lhs_c + lead,), (rhs_c + lead,)), (batch, batch))


def _bdot(a, b):
    return lax.dot_general(a.astype(bf16), b.astype(bf16), _dims(a, 1, 0), preferred_element_type=f32)


def _bdot_nt(a, b):
    return lax.dot_general(a.astype(bf16), b.astype(bf16), _dims(a, 1, 1), preferred_element_type=f32)


def _bdot_tn(a, b):
    return lax.dot_general(a.astype(bf16), b.astype(bf16), _dims(a, 0, 0), preferred_element_type=f32)


def _hdot(a, b):
    return lax.dot_general(a, b, _dims(a, 1, 0), precision=HI, preferred_element_type=f32)


def _seg_matrix(n, seg=HEAD_DIM):
    r = lax.broadcasted_iota(jnp.int32, (n, n), 0) // seg
    c = lax.broadcasted_iota(jnp.int32, (n, n), 1) // seg
    return (r == c).astype(f32)


def _layer_norm(y, g, b):
    mu = jnp.mean(y, axis=-1, keepdims=True)
    d = y - mu
    var = jnp.mean(d * d, axis=-1, keepdims=True)
    return d * lax.rsqrt(var + LN_EPS) * g + b


def _row_to_col(row, eye):
    return jnp.sum(jnp.where(eye, jnp.broadcast_to(row, eye.shape), 0.0), axis=2, keepdims=True)


def _stack_heads(x):
    h0 = lax.broadcasted_iota(jnp.int32, x.shape, 2) < HEAD_DIM
    return jnp.concatenate([jnp.where(h0, x, 0.0), jnp.where(h0, 0.0, x)], axis=1)


def _proj_body(x_ref, w_ref, b_ref, wg_ref, bg_ref, att_ref, s5_ref, mqk_ref, mv_ref, mo_ref, rw_ref, g_ref):
    xb = x_ref[...].astype(bf16)
    off = 0
    for o_ref, n in zip((att_ref, s5_ref, mqk_ref, mv_ref, mo_ref, rw_ref), PROJ_SPLITS):
        o_ref[...] = jnp.dot(xb, w_ref[:, off:off + n], preferred_element_type=f32) + b_ref[:, off:off + n]
        off += n
    g_ref[...] = lax.dot_general(wg_ref[...], xb, (((1,), (1,)), ((), ())), preferred_element_type=f32) + bg_ref[...]


def _proj(xt, w, b, wg, bg, tm=512):
    T = xt.shape[0]
    n_tot = sum(PROJ_SPLITS)
    outs = [jax.ShapeDtypeStruct((T, n), f32) for n in PROJ_SPLITS] + [jax.ShapeDtypeStruct((16, T), f32)]
    return pl.pallas_call(
        _proj_body, grid=(T // tm,),
        in_specs=[pl.BlockSpec((tm, D_MODEL), lambda i: (i, 0)),
                  pl.BlockSpec((D_MODEL, n_tot), lambda i: (0, 0)),
                  pl.BlockSpec((1, n_tot), lambda i: (0, 0)),
                  pl.BlockSpec((16, D_MODEL), lambda i: (0, 0)),
                  pl.BlockSpec((16, 1), lambda i: (0, 0))],
        out_specs=[pl.BlockSpec((tm, n), lambda i: (i, 0)) for n in PROJ_SPLITS]
        + [pl.BlockSpec((16, tm), lambda i: (0, i))],
        out_shape=outs, compiler_params=_cparams(("parallel",)), name="proj")(xt, w, b, wg, bg)


def _proj_params(w_in, b_in, ml_ib, ml_fb):
    o = np.cumsum((0, 256, 128, 128, RW_COLS, 256, 512, 256, 8, 8, 256))
    sl = lambda i: (w_in[:, o[i]:o[i + 1]], b_in[o[i]:o[i + 1]])
    (wq, bq), (wk, bk), (wv, bv), (wrw, brw), (ws5, bs5), (wqk, bqk), (wmv, bmv), (wi, bi), (wf, bf), (wo, bo) = (
        sl(i) for i in range(10))
    zw, zb = jnp.zeros((D_MODEL, HEAD_DIM), f32), jnp.zeros((HEAD_DIM,), f32)
    wq_e, bq_e = [], []
    for h in range(ATT_HEADS):
        wh, bh = wq[:, 64 * h:64 * h + 64], bq[64 * h:64 * h + 64]
        wq_e += [wh, zw] if h // 2 == 0 else [zw, wh]
        bq_e += [bh, zb] if h // 2 == 0 else [zb, bh]
    w = jnp.concatenate(wq_e + [wk, wv, ws5, wqk, wmv, wo, wrw], axis=1)
    b = jnp.concatenate(bq_e + [bk, bv, bs5, bqk, bmv, bo, brw])
    wg = jnp.concatenate([wi, wf], axis=1).T
    bg = jnp.concatenate([bi + ml_ib.reshape(-1), bf + ml_fb.reshape(-1)])
    return w.astype(bf16), b[None, :], wg.astype(bf16), bg[:, None]


def _att_prep_body(a_ref, cos_ref, sin_ref, gain_ref, q_ref, k_ref, v_ref):
    x = a_ref[:, 0:640]
    ms = _hdot(x * x, _seg_matrix(640)) * (1.0 / HEAD_DIM)
    xn = x * lax.rsqrt(ms + QK_EPS) * gain_ref[...]
    lane = lax.broadcasted_iota(jnp.int32, xn.shape, 1)
    partner = jnp.where((lane % 32) < 16, pltpu.roll(xn, 640 - 16, 1), pltpu.roll(xn, 16, 1))
    cos = jnp.concatenate([cos_ref[...]] * 5, axis=1)
    sin = jnp.concatenate([sin_ref[...]] * 5, axis=1)
    rot = xn * cos + partner * sin
    q_ref[...] = rot[:, 0:512].astype(bf16)
    k_ref[...] = rot[:, 512:640].astype(bf16)
    v_ref[:, 0:LANES] = a_ref[:, 640:768].astype(bf16)
    v_ref[:, LANES:2 * LANES] = jnp.ones((x.shape[0], LANES), bf16)


def _att_prep(att, cos, sin, gain, B, S, tq=512):
    T = B * S
    nb = S // tq
    return pl.pallas_call(
        _att_prep_body, grid=(B, nb),
        in_specs=[pl.BlockSpec((tq, 768), lambda b, i: (b * nb + i, 0)),
                  pl.BlockSpec((tq, LANES), lambda b, i: (i, 0)),
                  pl.BlockSpec((tq, LANES), lambda b, i: (i, 0)),
                  pl.BlockSpec((1, 640), lambda b, i: (0, 0))],
        out_specs=[pl.BlockSpec((tq, 512), lambda b, i: (b * nb + i, 0)),
                   pl.BlockSpec((tq, LANES), lambda b, i: (b * nb + i, 0)),
                   pl.BlockSpec((tq, 2 * LANES), lambda b, i: (b * nb + i, 0))],
        out_shape=[jax.ShapeDtypeStruct((T, 512), bf16), jax.ShapeDtypeStruct((T, LANES), bf16),
                   jax.ShapeDtypeStruct((T, 2 * LANES), bf16)],
        compiler_params=_cparams(("parallel", "parallel")), name="att_prep")(att, cos, sin, gain)


def _rope_tables(S):
    t = np.arange(S)
    row = (t // GRID_W).astype(np.float32)
    col = (t % GRID_W).astype(np.float32)
    n = 16
    inv = np.power(np.float32(ROPE_THETA), -np.arange(n, dtype=np.float32) / n).astype(np.float32)
    ar = jnp.asarray(row)[:, None] * jnp.asarray(inv)
    ac = jnp.asarray(col)[:, None] * jnp.asarray(inv)
    cos = jnp.concatenate([jnp.cos(ar), jnp.cos(ar), jnp.cos(ac), jnp.cos(ac)], axis=1)
    sin = jnp.concatenate([-jnp.sin(ar), jnp.sin(ar), -jnp.sin(ac), jnp.sin(ac)], axis=1)
    return jnp.concatenate([cos, cos], axis=1), jnp.concatenate([sin, sin], axis=1)


def _flash_body(q_ref, k_ref, v_ref, o_ref, acc_ref, *m_scratch, tk, track_max):
    tq = q_ref.shape[0]
    nk = k_ref.shape[0] // tk
    q2 = jnp.concatenate([q_ref[:, 0:LANES], q_ref[:, LANES:2 * LANES]], axis=0)
    acc_ref[...] = jnp.zeros(acc_ref.shape, f32)
    if track_max:
        m_ref, = m_scratch
        m_ref[...] = jnp.full(m_ref.shape, NEG, f32)

    def step(j, carry):
        rows = pl.ds(pl.multiple_of(j * tk, tk), tk)
        s = lax.dot_general(q2, k_ref[rows, :], (((1,), (1,)), ((), ())), preferred_element_type=f32)
        if track_max:
            m_old = m_ref[...]
            m_new = jnp.maximum(m_old, jnp.max(s, axis=1, keepdims=True))
            p = jnp.exp(s - m_new).astype(bf16)
            acc_ref[...] = jnp.exp(m_old - m_new) * acc_ref[...] + jnp.dot(p, v_ref[rows, :], preferred_element_type=f32)
            m_ref[...] = m_new
        else:
            acc_ref[...] += jnp.dot(jnp.exp(s).astype(bf16), v_ref[rows, :], preferred_element_type=f32)
        return carry

    lax.fori_loop(0, nk, step, 0)
    o = acc_ref[:, 0:LANES] / acc_ref[:, LANES:2 * LANES]
    o_ref[...] = jnp.concatenate([o[0:tq], o[tq:2 * tq]], axis=1)


SCORE_BOUND_MAX = 60.0


def _flash(q, k, v, score_bound, B, S, tq=256, tk=1024):
    T = B * S
    nb = S // tq
    tk = min(tk, S)

    def call(track_max):
        scratch = [pltpu.VMEM((2 * tq, 2 * LANES), f32)] + ([pltpu.VMEM((2 * tq, 1), f32)] if track_max else [])
        return pl.pallas_call(
            functools.partial(_flash_body, tk=tk, track_max=track_max), grid=(B, ATT_KV_HEADS, nb),
            in_specs=[pl.BlockSpec((tq, 2 * LANES), lambda b, g, i: (b * nb + i, g)),
                      pl.BlockSpec((S, LANES), lambda b, g, i: (b, 0)),
                      pl.BlockSpec((S, 2 * LANES), lambda b, g, i: (b, 0))],
            out_specs=pl.BlockSpec((tq, 2 * LANES), lambda b, g, i: (b * nb + i, g)),
            out_shape=jax.ShapeDtypeStruct((T, 512), f32), scratch_shapes=scratch,
            compiler_params=_cparams(("parallel", "parallel", "parallel")),
            name="flash_safe" if track_max else "flash")(q, k, v)

    return lax.cond(score_bound <= SCORE_BOUND_MAX, lambda: call(False), lambda: call(True))


def _halo_specs(width, tm, B, S):
    nb = S // tm
    r8 = tm // 8
    last8 = B * S // 8 - 1

    def main(b, i):
        return (b * nb + i, 0)

    def prev(b, i):
        return (jnp.maximum(b * (S // 8) + i * r8 - 1, 0), 0)

    def nxt(b, i):
        return (jnp.minimum(b * (S // 8) + (i + 1) * r8, last8), 0)

    return [pl.BlockSpec((tm, width), main), pl.BlockSpec((8, width), prev), pl.BlockSpec((8, width), nxt)]


def _fill_halo(buf_ref, x_ref, p_ref, n_ref):
    tm = x_ref.shape[0]
    i = pl.program_id(1)
    last = pl.num_programs(1) - 1
    buf_ref[pl.ds(8, tm), :] = x_ref[...]
    buf_ref[pl.ds(0, 8), :] = jnp.where(i > 0, p_ref[...], 0.0)
    buf_ref[pl.ds(8 + tm, 8), :] = jnp.where(i < last, n_ref[...], 0.0)


def _rw_prep_body(x_ref, p_ref, n_ref, mix_ref, w2_ref, a2_ref, g2_ref, vec_ref,
                  r_ref, k_ref, v_ref, an_ref, bn_ref, lw_ref, gate_ref, bonus_ref, buf_ref):
    tm = x_ref.shape[0]
    _fill_halo(buf_ref, x_ref, p_ref, n_ref)
    x = x_ref[...]
    p = x + mix_ref[0:1, :] * (buf_ref[pl.ds(7, tm), :] - x) + mix_ref[1:2, :] * (buf_ref[pl.ds(9, tm), :] - x)
    r, k, v = p[:, 0:256], p[:, 256:512], p[:, 512:768]
    w0f, w0b, a0, k_k, k_a, r_k = (vec_ref[j:j + 1, :] for j in range(6))
    dec = _bdot(jnp.tanh(p[:, 768:896]), w2_ref[...])
    z = p[:, 896:1088]
    a = _sigmoid(a0 + _bdot(z, a2_ref[...]))
    gate_ref[...] = _bdot(_sigmoid(z), g2_ref[...])
    seg = _seg_matrix(BRANCH_W)
    kk = k * k_k
    kk = kk / jnp.maximum(jnp.sqrt(_hdot(kk * kk, seg)), 1e-12)
    k2 = k * (1.0 + (a - 1.0) * k_a)
    bonus_ref[...] = _hdot(r * k2 * r_k, seg) * v
    r_ref[...] = r
    k_ref[...] = k2
    v_ref[...] = v
    an_ref[...] = -kk
    bn_ref[...] = kk * a
    lw_ref[0] = -jnp.exp(-_softplus(-(w0f + dec[:, 0:256])) - 0.5)
    lw_ref[1] = -jnp.exp(-_softplus(-(w0b + dec[:, 256:512])) - 0.5)


def _rw_prep(rw, mix, w2, a2, g2, vec, B, S, tm=512):
    T = B * S
    nb = S // tm
    o256 = pl.BlockSpec((tm, BRANCH_W), lambda b, i: (b * nb + i, 0))
    full = lambda shape: pl.BlockSpec(shape, lambda b, i: (0,) * len(shape))
    return pl.pallas_call(
        _rw_prep_body, grid=(B, nb),
        in_specs=_halo_specs(RW_COLS, tm, B, S) + [full((2, RW_COLS)), full((128, 512)), full((192, 256)),
                                                   full((192, 256)), full((8, 256))],
        out_specs=[o256] * 5 + [pl.BlockSpec((2, tm, BRANCH_W), lambda b, i: (0, b * nb + i, 0)), o256, o256],
        out_shape=[jax.ShapeDtypeStruct((T, BRANCH_W), f32)] * 5 + [jax.ShapeDtypeStruct((2, T, BRANCH_W), f32)]
        + [jax.ShapeDtypeStruct((T, BRANCH_W), f32)] * 2,
        scratch_shapes=[pltpu.VMEM((tm + 16, RW_COLS), f32)],
        compiler_params=_cparams(("parallel", "parallel")), name="rw_prep")(rw, rw, rw, mix, w2, a2, g2, vec)


def _pair_masks(rev, nbatch):
    n = 2 * CHUNK
    r = lax.broadcasted_iota(jnp.int32, (nbatch, n, n), 1)
    c = lax.broadcasted_iota(jnp.int32, (nbatch, n, n), 2)
    same = (r // CHUNK) == (c // CHUNK)
    if rev:
        return r, c, same & (c > r), same & (c >= r)
    return r, c, same & (c < r), same & (c <= r)


def _rw_chunk(st, r, k, v, an, bn, lw, rev):
    L = CHUNK
    N = r.shape[0]
    ri = lax.broadcasted_iota(jnp.int32, (N, L, L), 1)
    ci = lax.broadcasted_iota(jnp.int32, (N, L, L), 2)
    tri = ((ci >= ri) if rev else (ci <= ri)).astype(f32)
    cs = _hdot(tri, lw)
    tot = jnp.sum(lw, axis=1, keepdims=True)
    e_neg = jnp.exp(-cs)
    a2 = _stack_heads(an * jnp.exp(cs - lw))
    r2 = _stack_heads(r * jnp.exp(cs))
    b2 = _stack_heads(bn * e_neg)
    k2 = _stack_heads(k * e_neg)
    v2 = _stack_heads(v)
    rr, cc, strict, incl = _pair_masks(rev, N)
    mab = jnp.where(strict, _bdot_nt(a2, b2), 0.0)
    mak = jnp.where(strict, _bdot_nt(a2, k2), 0.0)
    pb = jnp.where(incl, _bdot_nt(r2, b2), 0.0)
    pk = jnp.where(incl, _bdot_nt(r2, k2), 0.0)
    eye = rr == cc
    m8 = jnp.where((rr // 8) == (cc // 8), mab, 0.0)
    x = eye.astype(f32) + m8
    p = _bdot(m8, m8)
    x = x + _bdot(x, p)
    p = _bdot(p, p)
    x = x + _bdot(x, p)
    n = 8
    while n < L:
        e = jnp.where(((rr // (2 * n)) == (cc // (2 * n))) & ((rr // n) != (cc // n)), mab, 0.0)
        x = x + _bdot(_bdot(x, e), x)
        n *= 2
    w = _bdot(x, a2)
    u0 = _bdot(x, _bdot(mak, v2))
    rh = r2 + _bdot(pb, w)
    y2 = _bdot(pb, u0) + _bdot(pk, v2) + _bdot(rh, st)
    y = y2[:, 0:L] + y2[:, L:2 * L]
    gam = _row_to_col(jnp.exp(tot), eye)
    st = gam * (st + _bdot(_bdot_tn(b2, w), st) + _bdot_tn(b2, u0) + _bdot_tn(k2, v2))
    return y, st


def _load_pairs(ref, rows):
    return jnp.concatenate([ref[:, rows, 0:LANES], ref[:, rows, LANES:2 * LANES]], axis=0)


def _store_pairs(ref, rows, y):
    nb = ref.shape[0]
    ref[:, rows, 0:LANES] = y[0:nb]
    ref[:, rows, LANES:2 * LANES] = y[nb:2 * nb]


def _rw_scan_body(r_ref, k_ref, v_ref, an_ref, bn_ref, lw_ref, y_ref, st_ref, *, rev):
    nch = r_ref.shape[1] // CHUNK

    @pl.when(pl.program_id(0) == 0)
    def _():
        st_ref[...] = jnp.zeros(st_ref.shape, f32)

    def step(cc, carry):
        c = (nch - 1 - cc) if rev else cc
        rows = pl.ds(pl.multiple_of(c * CHUNK, CHUNK), CHUNK)
        y, st = _rw_chunk(st_ref[...], *(_load_pairs(ref, rows) for ref in (r_ref, k_ref, v_ref, an_ref, bn_ref)),
                          _load_pairs(lw_ref.at[0], rows), rev)
        _store_pairs(y_ref, rows, y)
        st_ref[...] = st
        return carry

    lax.fori_loop(0, nch, step, 0)


def _rw_scan(r, k, v, an, bn, lw, B, S, rev, ts=256):
    nb = S // ts
    d = 1 if rev else 0
    blk = (lambda i: nb - 1 - i) if rev else (lambda i: i)
    io = pl.BlockSpec((B, ts, BRANCH_W), lambda i: (0, blk(i), 0))
    return pl.pallas_call(
        functools.partial(_rw_scan_body, rev=rev), grid=(nb,),
        in_specs=[io] * 5 + [pl.BlockSpec((1, B, ts, BRANCH_W), lambda i: (d, 0, blk(i), 0))],
        out_specs=io, out_shape=jax.ShapeDtypeStruct((B, S, BRANCH_W), f32),
        scratch_shapes=[pltpu.VMEM((B * BRANCH_W // LANES, LANES, LANES), f32)],
        compiler_params=_cparams(("arbitrary",)),
        name="rw_scan_bwd" if rev else "rw_scan_fwd")(r, k, v, an, bn, lw)


def _rw_finish_body(yf_ref, yb_ref, gate_ref, bonus_ref, gn_ref, o_ref):
    y = yf_ref[...] + yb_ref[...]
    seg = _seg_matrix(BRANCH_W)
    mu = _hdot(y, seg) * (1.0 / HEAD_DIM)
    d = y - mu
    var = _hdot(d * d, seg) * (1.0 / HEAD_DIM)
    yn = d * lax.rsqrt(var + RW_GN_EPS) * gn_ref[0:1, :] + gn_ref[1:2, :]
    o_ref[...] = (yn + bonus_ref[...]) * gate_ref[...]


def _rw_finish(yf, yb, gate, bonus, gn, tm=1024):
    T = yf.shape[0]
    io = pl.BlockSpec((tm, BRANCH_W), lambda i: (i, 0))
    return pl.pallas_call(
        _rw_finish_body, grid=(T // tm,), in_specs=[io] * 4 + [pl.BlockSpec((2, BRANCH_W), lambda i: (0, 0))],
        out_specs=io, out_shape=jax.ShapeDtypeStruct((T, BRANCH_W), f32),
        compiler_params=_cparams(("parallel",)), name="rw_finish")(yf, yb, gate, bonus, gn)


def _rwkv(rw, prm, B, S):
    mix, w2, a2, g2, vec, gn = prm
    r, k, v, an, bn, lw, gate, bonus = _rw_prep(rw, mix, w2, a2, g2, vec, B, S)
    seq = [t.reshape(B, S, BRANCH_W) for t in (r, k, v, an, bn)] + [lw.reshape(2, B, S, BRANCH_W)]
    yf = _rw_scan(*seq, B, S, rev=False).reshape(B * S, BRANCH_W)
    yb = _rw_scan(*seq, B, S, rev=True).reshape(B * S, BRANCH_W)
    return _rw_finish(yf, yb, gate, bonus, gn)


def _rw_params(mix, w0, w2, a0, a2, g2, k_k, k_a, r_k, gn_g, gn_b):
    z = jnp.zeros((64, 256), f32)
    w2c = jnp.concatenate([jnp.concatenate([w2[0], z], axis=1), jnp.concatenate([z, w2[1]], axis=1)], axis=0)
    a2p = jnp.concatenate([a2, jnp.zeros((128, 256), f32)], axis=0)
    g2p = jnp.concatenate([jnp.zeros((64, 256), f32), g2], axis=0)
    vec = jnp.stack([w0[0], w0[1], a0, k_k, k_a, r_k.reshape(-1), jnp.zeros_like(a0), jnp.zeros_like(a0)])
    return mix, w2c.astype(bf16), a2p.astype(bf16), g2p.astype(bf16), vec, jnp.stack([gn_g, gn_b])


def _s5_scan_body(u_ref, bcat_ref, cre_ref, cim_ref, lre_ref, lim_ref, y_ref, sre_ref, sim_ref, bre_ref, bim_ref):
    ts = u_ref.shape[0]
    n = S5_GROUPS * S5_STATE

    @pl.when(pl.program_id(0) == 0)
    def _():
        sre_ref[...] = jnp.zeros(sre_ref.shape, f32)
        sim_ref[...] = jnp.zeros(sim_ref.shape, f32)

    u = u_ref[...].reshape(ts * 8, BRANCH_W).astype(bf16)
    fwd = lax.broadcasted_iota(jnp.int32, (ts, 8, n), 1) < 4

    def bu(part):
        return jnp.dot(u, bcat_ref[:, part * n:(part + 1) * n], preferred_element_type=f32).reshape(ts, 8, n)

    bre_ref[...] = jnp.where(fwd, bu(0), bu(2))
    bim_ref[...] = jnp.where(fwd, bu(1), bu(3))
    lre = lre_ref[...]
    lim = lim_ref[...]

    def step(t, carry):
        sre, sim = carry
        nre = lre * sre - lim * sim + bre_ref[t]
        nim = lre * sim + lim * sre + bim_ref[t]
        bre_ref[t] = nre
        bim_ref[t] = nim
        return nre, nim

    sre, sim = lax.fori_loop(0, ts, step, (sre_ref[...], sim_ref[...]), unroll=4)
    sre_ref[...] = sre
    sim_ref[...] = sim
    y = (_bdot(bre_ref[...].reshape(ts * 8, n), cre_ref[...])
         - _bdot(bim_ref[...].reshape(ts * 8, n), cim_ref[...])).reshape(ts, 8, 2 * BRANCH_W)
    fwd_o = lax.broadcasted_iota(jnp.int32, (ts, 8, BRANCH_W), 1) < 4
    y_ref[...] = jnp.where(fwd_o, y[:, :, 0:BRANCH_W], y[:, :, BRANCH_W:2 * BRANCH_W])


def _s5_scan(u2, bcat, cre, cim, lre, lim, ts=128):
    S = u2.shape[0]
    n = S5_GROUPS * S5_STATE
    full = lambda shape: pl.BlockSpec(shape, lambda i: (0,) * len(shape))
    return pl.pallas_call(
        _s5_scan_body, grid=(S // ts,),
        in_specs=[pl.BlockSpec((ts, 8, BRANCH_W), lambda i: (i, 0, 0)), full((BRANCH_W, 4 * n)),
                  full((n, 2 * BRANCH_W)), full((n, 2 * BRANCH_W)), full((8, n)), full((8, n))],
        out_specs=pl.BlockSpec((ts, 8, BRANCH_W), lambda i: (i, 0, 0)),
        out_shape=jax.ShapeDtypeStruct((S, 8, BRANCH_W), f32),
        scratch_shapes=[pltpu.VMEM((8, n), f32), pltpu.VMEM((8, n), f32),
                        pltpu.VMEM((ts, 8, n), f32), pltpu.VMEM((ts, 8, n), f32)],
        compiler_params=_cparams(("arbitrary",)), name="s5_scan")(u2, bcat, cre, cim, lre, lim)


def _s5_finish_body(y_ref, u_ref, d_ref, w_ref, b_ref, o_ref):
    y = y_ref[...] + u_ref[...] * d_ref[...]
    y = 0.5 * y * (1.0 + jnp.tanh(math.sqrt(2.0 / math.pi) * (y + 0.044715 * (y * y * y))))
    o_ref[...] = y * _sigmoid(_bdot(y, w_ref[...]) + b_ref[...])


def _s5_finish(y, u, d, w, b, tm=1024):
    T = y.shape[0]
    io = pl.BlockSpec((tm, BRANCH_W), lambda i: (i, 0))
    vec = pl.BlockSpec((1, BRANCH_W), lambda i: (0, 0))
    return pl.pallas_call(
        _s5_finish_body, grid=(T // tm,),
        in_specs=[io, io, vec, pl.BlockSpec((BRANCH_W, BRANCH_W), lambda i: (0, 0)), vec],
        out_specs=io, out_shape=jax.ShapeDtypeStruct((T, BRANCH_W), f32),
        compiler_params=_cparams(("parallel",)), name="s5_finish")(y, u, d, w, b)


def _s5_params(lam_re, lam_im, log_dt, b_re, b_im, c_re, c_im):
    G, P, C = S5_GROUPS, S5_STATE, S5_GROUP
    eye = jnp.eye(G, dtype=f32)
    b_c = lax.complex(b_re, b_im)
    bcat, cre, cim, lre, lim = [], [], [], [], []
    for d in range(2):
        lam = lax.complex(jnp.minimum(lam_re[d], -1e-4), lam_im[d])
        lam_bar = jnp.exp(lam * jnp.exp(log_dt[d])[:, None])
        b_bar = ((lam_bar - 1.0) / lam)[..., None] * b_c
        for part in (jnp.real(b_bar), jnp.imag(b_bar)):
            bcat.append(jnp.einsum('gh,gpc->gchp', eye, part).reshape(G * C, G * P))
        cre.append(jnp.einsum('gh,gcp->gphc', eye, c_re[d]).reshape(G * P, G * C))
        cim.append(jnp.einsum('gh,gcp->gphc', eye, c_im[d]).reshape(G * P, G * C))
        lre.append(jnp.broadcast_to(jnp.real(lam_bar).reshape(1, G * P), (4, G * P)))
        lim.append(jnp.broadcast_to(jnp.imag(lam_bar).reshape(1, G * P), (4, G * P)))
    return (jnp.concatenate(bcat, axis=1).astype(bf16), jnp.concatenate(cre, axis=1).astype(bf16),
            jnp.concatenate(cim, axis=1).astype(bf16), jnp.concatenate(lre, axis=0), jnp.concatenate(lim, axis=0))


def _s5(u, prm, fin, B, S):
    u3 = u.reshape(B, S, BRANCH_W).transpose(1, 0, 2)
    u2 = jnp.concatenate([u3, u3[::-1]], axis=1)
    y2 = _s5_scan(u2, *prm)
    y = (y2[:, 0:B] + y2[::-1, B:2 * B]).transpose(1, 0, 2).reshape(B * S, BRANCH_W)
    return _s5_finish(y, u, *fin)


def _ml_prep_body(x_ref, p_ref, n_ref, w_ref, b_ref, q_ref, k_ref, buf_ref):
    tm = x_ref.shape[0]
    _fill_halo(buf_ref, x_ref, p_ref, n_ref)
    y = b_ref[...] + w_ref[2:3, :] * x_ref[...]
    for j in (0, 1, 3, 4):
        y = y + w_ref[j:j + 1, :] * buf_ref[pl.ds(6 + j, tm), :]
    y = y * _sigmoid(y)
    q_ref[...] = y[:, 0:BRANCH_W]
    k_ref[...] = y[:, BRANCH_W:2 * BRANCH_W] * (HEAD_DIM ** -0.5)


def _ml_prep(mqk, w, b, B, S, tm=512):
    T = B * S
    nb = S // tm
    o = pl.BlockSpec((tm, BRANCH_W), lambda b_, i: (b_ * nb + i, 0))
    return pl.pallas_call(
        _ml_prep_body, grid=(B, nb),
        in_specs=_halo_specs(512, tm, B, S) + [pl.BlockSpec((5, 512), lambda b_, i: (0, 0)),
                                               pl.BlockSpec((1, 512), lambda b_, i: (0, 0))],
        out_specs=[o, o], out_shape=[jax.ShapeDtypeStruct((T, BRANCH_W), f32)] * 2,
        scratch_shapes=[pltpu.VMEM((tm + 16, 512), f32)],
        compiler_params=_cparams(("parallel", "parallel")), name="ml_prep")(mqk, mqk, mqk, w, b)


def _ml_chunk(state, q, k, v, li, lfp, rev):
    c2, n_row, m_row = state
    L = CHUNK
    N = q.shape[0]
    rr, cc, _, incl = _pair_masks(rev, N)
    same = (rr // L) == (cc // L)
    eye = rr == cc
    lane = lax.broadcasted_iota(jnp.int32, (N, 1, 2 * L), 2)
    lf = jnp.minimum(lfp, 0.0) - jnp.log(1.0 + jnp.exp(-jnp.abs(lfp)))
    lf8 = jnp.broadcast_to(lf, (N, 8, 2 * L))
    cum = (same & ((rr >= cc) if rev else (rr <= cc))).astype(f32)
    b_row = _hdot(lf8, cum)[:, 0:1]
    g_row = _hdot(lf8, same.astype(f32))[:, 0:1]
    w_end = g_row - b_row + li
    m0 = jnp.max(jnp.where(lane < L, w_end, NEG), axis=2, keepdims=True)
    m1 = jnp.max(jnp.where(lane < L, NEG, w_end), axis=2, keepdims=True)
    m_loc = jnp.where(lane < L, m0, m1)
    e_col = _row_to_col(jnp.exp(w_end - m_loc), eye)
    b_col = _row_to_col(b_row, eye)
    q2, k2, v2 = _stack_heads(q), _stack_heads(k), _stack_heads(v)
    log_inter = b_col + _row_to_col(m_row, eye)
    log_intra = jnp.where(incl, b_col - b_row + li, NEG)
    m_r = jnp.maximum(log_inter, jnp.max(log_intra, axis=2, keepdims=True))
    s = _bdot_nt(q2, k2) * jnp.exp(log_intra - m_r)
    inter = jnp.exp(log_inter - m_r)
    num = _bdot(s, v2) + inter * _bdot(q2, c2)
    den = jnp.sum(s, axis=2, keepdims=True) + inter * jnp.sum(q2 * n_row, axis=2, keepdims=True)
    h2 = num / jnp.maximum(jnp.abs(den), jnp.exp(-m_r))
    h = h2[:, 0:L] + h2[:, L:2 * L]
    m_new = jnp.maximum(g_row + m_row, m_loc)
    a = jnp.exp(g_row + m_row - m_new)
    bb = jnp.exp(m_loc - m_new)
    ek = e_col * k2
    c2 = a * c2 + bb * _bdot_tn(ek, v2)
    n_row = a * n_row + bb * jnp.sum(ek, axis=1, keepdims=True)
    return h, (c2, n_row, m_new)


def _ml_scan_body(q_ref, k_ref, v_ref, g_ref, h_ref, c_ref, n_ref, m_ref, *, rev):
    nch = q_ref.shape[1] // CHUNK

    @pl.when(pl.program_id(0) == 0)
    def _():
        c_ref[...] = jnp.zeros(c_ref.shape, f32)
        n_ref[...] = jnp.zeros(n_ref.shape, f32)
        m_ref[...] = jnp.zeros(m_ref.shape, f32)

    def step(cc, carry):
        c = (nch - 1 - cc) if rev else cc
        rows = pl.ds(pl.multiple_of(c * CHUNK, CHUNK), CHUNK)
        li, lfp = (jnp.concatenate([g_ref[t, 0, 0, :, pl.ds(c, 1), :], g_ref[t, 0, 1, :, pl.ds(c, 1), :]], axis=0)
                   for t in range(2))
        h, (c2, n_row, m_row) = _ml_chunk((c_ref[...], n_ref[...], m_ref[...]), _load_pairs(q_ref, rows),
                                          _load_pairs(k_ref, rows), _load_pairs(v_ref, rows), li, lfp, rev)
        _store_pairs(h_ref, rows, h)
        c_ref[...] = c2
        n_ref[...] = n_row
        m_ref[...] = m_row
        return carry

    lax.fori_loop(0, nch, step, 0)


def _ml_scan(q, k, v, g, B, S, rev, ts=512):
    nb = S // ts
    nch = ts // CHUNK
    d = 1 if rev else 0
    nchain = B * BRANCH_W // LANES
    blk = (lambda i: nb - 1 - i) if rev else (lambda i: i)
    io = pl.BlockSpec((B, ts, BRANCH_W), lambda i: (0, blk(i), 0))
    return pl.pallas_call(
        functools.partial(_ml_scan_body, rev=rev), grid=(nb,),
        in_specs=[io, io, io, pl.BlockSpec((2, 1, 2, B, nch, LANES), lambda i: (0, d, 0, 0, blk(i), 0))],
        out_specs=io, out_shape=jax.ShapeDtypeStruct((B, S, BRANCH_W), f32),
        scratch_shapes=[pltpu.VMEM((nchain, LANES, LANES), f32), pltpu.VMEM((nchain, 1, LANES), f32),
                        pltpu.VMEM((nchain, 1, LANES), f32)],
        compiler_params=_cparams(("arbitrary",)),
        name="ml_scan_bwd" if rev else "ml_scan_fwd")(q, k, v, g)


def _mlstm(mqk, mv, gt, conv_w, conv_b, B, S):
    q, k = _ml_prep(mqk, conv_w, conv_b, B, S)
    g = gt.reshape(2, 2, 2, 2, B, S // CHUNK, CHUNK).transpose(0, 1, 2, 4, 5, 3, 6).reshape(2, 2, 2, B, S // CHUNK, LANES)
    seq = [t.reshape(B, S, BRANCH_W) for t in (q, k, mv)]
    return (_ml_scan(*seq, g, B, S, rev=False).reshape(B * S, BRANCH_W),
            _ml_scan(*seq, g, B, S, rev=True).reshape(B * S, BRANCH_W))


def _merge_body(x_ref, att_ref, rw_ref, s5_ref, hf_ref, hb_ref, mo_ref, wg_ref, bg_ref, wba_ref, wb_ref, wo_ref,
                ln_ref, o_ref):
    x = x_ref[...]
    xb = x.astype(bf16)
    ml = _sigmoid(mo_ref[...]) * (hf_ref[...] + hb_ref[...])
    branches = (att_ref[...], rw_ref[...], s5_ref[...], ml)
    merged = None
    for n in range(4):
        gate = _sigmoid(jnp.dot(xb, wg_ref[n], preferred_element_type=f32) + bg_ref[n:n + 1, :])
        wide = _bdot(branches[n], wba_ref[...] if n == 0 else wb_ref[n - 1])
        merged = gate * wide if merged is None else merged + gate * wide
    y = ALPHA * x + _bdot(merged, wo_ref[...])
    o_ref[...] = _layer_norm(y, ln_ref[0:1, :], ln_ref[1:2, :])


def _merge(xt, att, rw, s5, hf, hb, mo, wg, bg, wba, wb, wo, ln, tm=256):
    T = xt.shape[0]
    row = lambda n: pl.BlockSpec((tm, n), lambda i: (i, 0))
    const = lambda shape: pl.BlockSpec(shape, lambda i: (0,) * len(shape), pipeline_mode=pl.Buffered(1))
    return pl.pallas_call(
        _merge_body, grid=(T // tm,),
        in_specs=[row(D_MODEL), row(512)] + [row(BRANCH_W)] * 5
        + [const((4, D_MODEL, D_MODEL)), const((4, D_MODEL)), const((512, D_MODEL)), const((3, BRANCH_W, D_MODEL)),
           const((D_MODEL, D_MODEL)), const((2, D_MODEL))],
        out_specs=row(D_MODEL), out_shape=jax.ShapeDtypeStruct((T, D_MODEL), f32),
        compiler_params=_cparams(("parallel",)), name="merge")(xt, att, rw, s5, hf, hb, mo, wg, bg, wba, wb, wo, ln)


def _att_branch_weight(wb):
    z = jnp.zeros((HEAD_DIM, D_MODEL), f32)
    parts = []
    for h in range(ATT_HEADS):
        wh = wb[64 * h:64 * h + 64]
        parts += [wh, z] if h // 2 == 0 else [z, wh]
    return jnp.concatenate(parts, axis=0)


def _ffn_body(x_ref, w1_ref, w3_ref, w2_ref, ln_ref, o_ref, acc_ref):
    j = pl.program_id(1)
    xb = x_ref[...].astype(bf16)
    h1 = jnp.dot(xb, w1_ref[...], preferred_element_type=f32)
    h3 = jnp.dot(xb, w3_ref[...], preferred_element_type=f32)
    part = _bdot(h1 * _sigmoid(h1) * h3, w2_ref[...])

    @pl.when(j == 0)
    def _():
        acc_ref[...] = part

    @pl.when(j > 0)
    def _():
        acc_ref[...] += part

    @pl.when(j == pl.num_programs(1) - 1)
    def _():
        o_ref[...] = _layer_norm(ALPHA * x_ref[...] + acc_ref[...], ln_ref[0:1, :], ln_ref[1:2, :])


def _ffn(xt, w1, w3, w2, ln, tm=512, tf=1408):
    T = xt.shape[0]
    dff = w1.shape[1]
    return pl.pallas_call(
        _ffn_body, grid=(T // tm, dff // tf),
        in_specs=[pl.BlockSpec((tm, D_MODEL), lambda i, j: (i, 0)), pl.BlockSpec((D_MODEL, tf), lambda i, j: (0, j)),
                  pl.BlockSpec((D_MODEL, tf), lambda i, j: (0, j)), pl.BlockSpec((tf, D_MODEL), lambda i, j: (j, 0)),
                  pl.BlockSpec((2, D_MODEL), lambda i, j: (0, 0))],
        out_specs=pl.BlockSpec((tm, D_MODEL), lambda i, j: (i, 0)),
        out_shape=jax.ShapeDtypeStruct((T, D_MODEL), f32), scratch_shapes=[pltpu.VMEM((tm, D_MODEL), f32)],
        compiler_params=_cparams(("parallel", "arbitrary")), name="ffn")(xt, w1, w3, w2, ln)


MOE_TILE = 512
SC_WINDOW = 128
SC_WORDS = 256


def _pack_words(x):
    bits = lax.bitcast_convert_type(x.astype(bf16).astype(f32), jnp.int32)
    half = D_MODEL // 2
    w = lax.shift_right_logical(bits[:, 0:half], 16) | bits[:, half:D_MODEL]
    return w[:, 0:SC_WORDS], w[:, SC_WORDS:2 * SC_WORDS]


def _unpack_words(wa, wb):
    w = jnp.concatenate([wa, wb], axis=1)
    lo = lax.bitcast_convert_type(lax.shift_left(w, 16), f32)
    hi = lax.bitcast_convert_type(w & jnp.int32(-65536), f32)
    return jnp.concatenate([lo, hi], axis=1)


def _router_body(x_ref, rt_ref, xa_ref, xb_ref, meta_ref, cnt_ref, run_ref):
    tb = x_ref.shape[0]

    @pl.when(pl.program_id(0) == 0)
    def _():
        run_ref[...] = jnp.zeros(run_ref.shape, f32)

    x = x_ref[...]
    xa_ref[...], xb_ref[...] = _pack_words(x)
    logits = _hdot(x, rt_ref[...])
    lane = lax.broadcasted_iota(jnp.int32, logits.shape, 1)
    lg = jnp.where(lane < N_EXPERTS, logits, NEG)
    v1 = jnp.max(lg, axis=1, keepdims=True)
    i1 = jnp.min(jnp.where(lg == v1, lane, LANES), axis=1, keepdims=True)
    lg2 = jnp.where(lane == i1, NEG, lg)
    v2 = jnp.max(lg2, axis=1, keepdims=True)
    i2 = jnp.min(jnp.where(lg2 == v2, lane, LANES), axis=1, keepdims=True)
    e2 = jnp.exp(v2 - v1)
    sel1, sel2 = lane == i1, lane == i2
    mask = (sel1 | sel2).astype(f32)
    r = lax.broadcasted_iota(jnp.int32, (tb, tb), 0)
    c = lax.broadcasted_iota(jnp.int32, (tb, tb), 1)
    rank = _bdot((c < r).astype(f32), mask) + run_ref[0:1, :]
    run_ref[...] = run_ref[...] + jnp.sum(mask, axis=0, keepdims=True)
    rank1 = jnp.sum(jnp.where(sel1, rank, 0.0), axis=1, keepdims=True)
    rank2 = jnp.sum(jnp.where(sel2, rank, 0.0), axis=1, keepdims=True)
    cols = (i1.astype(f32), i2.astype(f32), rank1, rank2, 1.0 / (1.0 + e2), e2 / (1.0 + e2))
    meta = jnp.zeros(logits.shape, f32)
    for n, col in enumerate(cols):
        meta = jnp.where(lane == n, col, meta)
    meta_ref[...] = meta
    cnt_ref[...] = run_ref[...]


def _router(xt, router, tb=1024):
    T = xt.shape[0]
    tb = min(tb, T)
    return pl.pallas_call(
        _router_body, grid=(T // tb,),
        in_specs=[pl.BlockSpec((tb, D_MODEL), lambda i: (i, 0)), pl.BlockSpec((D_MODEL, LANES), lambda i: (0, 0))],
        out_specs=[pl.BlockSpec((tb, SC_WORDS), lambda i: (i, 0)), pl.BlockSpec((tb, SC_WORDS), lambda i: (i, 0)),
                   pl.BlockSpec((tb, LANES), lambda i: (i, 0)), pl.BlockSpec((8, LANES), lambda i: (0, 0))],
        out_shape=[jax.ShapeDtypeStruct((T, SC_WORDS), jnp.int32), jax.ShapeDtypeStruct((T, SC_WORDS), jnp.int32),
                   jax.ShapeDtypeStruct((T, LANES), f32), jax.ShapeDtypeStruct((8, LANES), f32)],
        scratch_shapes=[pltpu.VMEM((8, LANES), f32)],
        compiler_params=_cparams(("arbitrary",)), name="moe_router")(xt, router)


def _sc_gather(table, idx):
    n = idx.shape[0]
    mesh = plsc.VectorSubcoreMesh(core_axis_name="c", subcore_axis_name="s")

    @functools.partial(pl.kernel, out_type=jax.ShapeDtypeStruct((n, SC_WORDS), table.dtype), mesh=mesh)
    def gather(x_hbm, i_hbm, o_hbm):
        def body(i_vmem, o_vmem):
            pltpu.sync_copy(x_hbm.at[i_vmem.at[0]], o_vmem)

        pltpu.emit_pipeline(
            body, grid=(n // SC_WINDOW,),
            in_specs=[pl.BlockSpec((1, SC_WINDOW), index_map=lambda i: (0, i))],
            out_specs=[pl.BlockSpec((SC_WINDOW, SC_WORDS), index_map=lambda i: (i, 0))],
            core_axis_name="s", dimension_semantics=(pltpu.PARALLEL,))(i_hbm, o_hbm)

    return gather(table, idx.reshape(1, n))


def _sc_scatter(rows, idx, n_out):
    R = rows.shape[0]
    n = idx.shape[0]
    nblk = R // SC_WINDOW
    mesh = plsc.VectorSubcoreMesh(core_axis_name="c", subcore_axis_name="s")

    @functools.partial(pl.kernel, out_type=jax.ShapeDtypeStruct((n_out, SC_WORDS), rows.dtype), mesh=mesh,
                       scratch_types=[])
    def scatter(x_hbm, i_hbm, o_hbm):
        def body(x_vmem, i_vmem):
            pltpu.sync_copy(x_vmem, o_hbm.at[i_vmem.at[0]])

        pltpu.emit_pipeline(
            body, grid=(n // SC_WINDOW,),
            in_specs=[pl.BlockSpec((SC_WINDOW, SC_WORDS), index_map=lambda i: (i % nblk, 0)),
                      pl.BlockSpec((1, SC_WINDOW), index_map=lambda i: (0, i))],
            out_specs=[], core_axis_name="s", dimension_semantics=(pltpu.PARALLEL,))(x_hbm, i_hbm)

    return scatter(rows, idx.reshape(1, n))


def _experts_body(te_ref, nt_ref, xa_ref, xb_ref, w1_ref, w3_ref, w2_ref, oa_ref, ob_ref, acc_ref):
    i = pl.program_id(0)
    j = pl.program_id(1)

    @pl.when(i < nt_ref[0])
    def _():
        x = _unpack_words(xa_ref[...], xb_ref[...]).astype(bf16)
        h1 = jnp.dot(x, w1_ref[0], preferred_element_type=f32)
        h3 = jnp.dot(x, w3_ref[0], preferred_element_type=f32)
        part = _bdot(h1 * _sigmoid(h1) * h3, w2_ref[0])

        @pl.when(j == 0)
        def _():
            acc_ref[...] = part

        @pl.when(j > 0)
        def _():
            acc_ref[...] += part

    @pl.when(j == pl.num_programs(1) - 1)
    def _():
        oa_ref[...], ob_ref[...] = _pack_words(jnp.where(i < nt_ref[0], acc_ref[...], 0.0))


def _experts(xa, xb, tile_expert, n_tiles, w1, w3, w2, tf=512):
    P = xa.shape[0]
    dff = w1.shape[2]
    words = pl.BlockSpec((MOE_TILE, SC_WORDS), lambda i, j, te, nt: (i, 0))
    grid_spec = pltpu.PrefetchScalarGridSpec(
        num_scalar_prefetch=2, grid=(P // MOE_TILE, dff // tf),
        in_specs=[words, words,
                  pl.BlockSpec((1, D_MODEL, tf), lambda i, j, te, nt: (te[i], 0, j)),
                  pl.BlockSpec((1, D_MODEL, tf), lambda i, j, te, nt: (te[i], 0, j)),
                  pl.BlockSpec((1, tf, D_MODEL), lambda i, j, te, nt: (te[i], j, 0))],
        out_specs=[words, words],
        scratch_shapes=[pltpu.VMEM((MOE_TILE, D_MODEL), f32)])
    return pl.pallas_call(
        _experts_body, grid_spec=grid_spec, out_shape=[jax.ShapeDtypeStruct((P, SC_WORDS), jnp.int32)] * 2,
        compiler_params=_cparams(("parallel", "arbitrary")), name="moe_experts")(tile_expert, n_tiles, xa, xb, w1, w3, w2)


def _combine_body(x_ref, y0a_ref, y0b_ref, y1a_ref, y1b_ref, meta_ref, ln_ref, o_ref):
    meta = meta_ref[...]
    ff = (meta[:, 4:5] * _unpack_words(y0a_ref[...], y0b_ref[...])
          + meta[:, 5:6] * _unpack_words(y1a_ref[...], y1b_ref[...]))
    o_ref[...] = _layer_norm(ALPHA * x_ref[...] + ff, ln_ref[0:1, :], ln_ref[1:2, :])


def _combine(xt, yga, ygb, meta, ln, tm=1024):
    T = xt.shape[0]
    tm = min(tm, T)
    nb = T // tm
    row = pl.BlockSpec((tm, D_MODEL), lambda i: (i, 0))
    first = pl.BlockSpec((tm, SC_WORDS), lambda i: (i, 0))
    second = pl.BlockSpec((tm, SC_WORDS), lambda i: (nb + i, 0))
    return pl.pallas_call(
        _combine_body, grid=(nb,),
        in_specs=[row, first, first, second, second, pl.BlockSpec((tm, LANES), lambda i: (i, 0)),
                  pl.BlockSpec((2, D_MODEL), lambda i: (0, 0))],
        out_specs=row, out_shape=jax.ShapeDtypeStruct((T, D_MODEL), f32),
        compiler_params=_cparams(("parallel",)), name="moe_combine")(xt, yga, ygb, yga, ygb, meta, ln)


def _moe(xt, router, w1, w3, w2, ln):
    T = xt.shape[0]
    xa, xb, meta, cnt = _router(xt, router)
    counts = cnt[0, :N_EXPERTS].astype(jnp.int32)
    tiles = (counts + MOE_TILE - 1) // MOE_TILE
    tile_end = jnp.cumsum(tiles)
    offset = (tile_end - tiles) * MOE_TILE
    expert = meta[:, 0:2].astype(jnp.int32)
    onehot = expert[:, :, None] == jnp.arange(N_EXPERTS, dtype=jnp.int32)[None, None, :]
    pos = jnp.sum(jnp.where(onehot, offset[None, None, :], 0), axis=2) + meta[:, 2:4].astype(jnp.int32)
    pos = pos.T.reshape(-1)
    P = 2 * T + N_EXPERTS * MOE_TILE
    tile_id = jnp.arange(P // MOE_TILE, dtype=jnp.int32)
    tile_expert = jnp.minimum(jnp.sum((tile_id[:, None] >= tile_end[None, :]).astype(jnp.int32), axis=1), N_EXPERTS - 1)
    ya, yb = _experts(_sc_scatter(xa, pos, P), _sc_scatter(xb, pos, P), tile_expert, tile_end[-1:], w1, w3, w2)
    return _combine(xt, _sc_gather(ya, pos), _sc_gather(yb, pos), meta, ln)


def kernel(x, w_in, b_in, att_gq, att_gk, rw_mix, rw_w0, rw_w2, rw_a0, rw_a2, rw_g2, rw_kk, rw_ka, rw_rk, rw_ln_g, rw_ln_b, s5_lam_re, s5_lam_im, s5_log_dt, s5_b_re, s5_b_im, s5_c_re, s5_c_im, s5_d, s5_glu_w, s5_glu_b, ml_conv_w, ml_conv_b, ml_ib, ml_fb, w_gate, b_gate, w_branch, w_out, ln1_g, ln1_b, ffn_w1, ffn_w3, ffn_w2, moe_router, moe_w1, moe_w3, moe_w2, ln2_g, ln2_b):
    B, S, D = x.shape
    xt = x.reshape(B * S, D)
    cos, sin = _rope_tables(S)
    for l in range(DEPTH):
        att, s5u, mqk, mv, mo, rw, gt = _proj(xt, *_proj_params(w_in[l], b_in[l], ml_ib[l], ml_fb[l]))
        gain = jnp.concatenate([jnp.tile(att_gq[l], 8) * (HEAD_DIM ** -0.5), jnp.tile(att_gk[l], 2)])[None, :]
        q, k, v = _att_prep(att, cos, sin, gain, B, S)
        score_bound = 8.1 * jnp.max(jnp.abs(att_gq[l])) * jnp.max(jnp.abs(att_gk[l]))
        o_att = _flash(q, k, v, score_bound, B, S)
        o_rw = _rwkv(rw, _rw_params(rw_mix[l], rw_w0[l], rw_w2[l], rw_a0[l], rw_a2[l], rw_g2[l], rw_kk[l], rw_ka[l],
                                    rw_rk[l], rw_ln_g[l], rw_ln_b[l]), B, S)
        o_s5 = _s5(s5u, _s5_params(s5_lam_re[l], s5_lam_im[l], s5_log_dt[l], s5_b_re[l], s5_b_im[l], s5_c_re[l],
                                   s5_c_im[l]),
                   (s5_d[l][None, :], s5_glu_w[l].astype(bf16), s5_glu_b[l][None, :]), B, S)
        hf, hb = _mlstm(mqk, mv, gt, ml_conv_w[l], ml_conv_b[l][None, :], B, S)
        xt = _merge(xt, o_att, o_rw, o_s5, hf, hb, mo, w_gate[l].astype(bf16), b_gate[l],
                    _att_branch_weight(w_branch[l, 0]).astype(bf16), w_branch[l, 1:].astype(bf16),
                    w_out[l].astype(bf16), jnp.stack([ln1_g[l], ln1_b[l]]))
        ln2 = jnp.stack([ln2_g[l], ln2_b[l]])
        if l % 2 == 0:
            xt = _ffn(xt, ffn_w1[l // 2].astype(bf16), ffn_w3[l // 2].astype(bf16), ffn_w2[l // 2].astype(bf16), ln2)
        else:
            router = jnp.pad(moe_router[l // 2], ((0, 0), (0, LANES - N_EXPERTS)))
            xt = _moe(xt, router, moe_w1[l // 2].astype(bf16), moe_w3[l // 2].astype(bf16),
                      moe_w2[l // 2].astype(bf16), ln2)
    return xt.reshape(B, S, D)
```

```python
import functools
import math

import jax
import jax.numpy as jnp
import numpy as np
from jax import lax
from jax.experimental import pallas as pl
from jax.experimental.pallas import tpu as pltpu
from jax.experimental.pallas import tpu_sc as plsc

f32 = jnp.float32
bf16 = jnp.bfloat16

D_MODEL = 1024
DEPTH = 2
GRID_W = 64
BRANCH_W = 256
HEAD_DIM = 64
ATT_HEADS = 4
ATT_KV_HEADS = 2
ROPE_THETA = 10000.0
QK_EPS = 1e-6
RW_GN_EPS = 64e-5
RW_COLS = 1088
S5_GROUP = 16
S5_GROUPS = 16
S5_STATE = 64
ML_HEADS = 4
N_EXPERTS = 8
ALPHA = (2 * DEPTH) ** 0.25
LN_EPS = 1e-5

LANES = 128
CHUNK = 64
NEG = -1e30
VMEM_LIMIT = 56 * 1024 * 1024

PROJ_SPLITS = (768, 256, 512, 256, 256, RW_COLS)


def _cparams(sem):
    return pltpu.CompilerParams(dimension_semantics=sem, vmem_limit_bytes=VMEM_LIMIT)


def _sigmoid(x):
    return 1.0 / (1.0 + jnp.exp(-x))


def _softplus(x):
    return jnp.maximum(x, 0.0) + jnp.log(1.0 + jnp.exp(-jnp.abs(x)))


def _dims(a, lhs_c, rhs_c):
    lead = a.ndim - 2
    batch = tuple(range(lead))
    return (((lhs_c + lead,), (rhs_c + lead,)), (batch, batch))


def _bdot(a, b):
    return lax.dot_general(a.astype(bf16), b.astype(bf16), _dims(a, 1, 0), preferred_element_type=f32)


def _bdot_nt(a, b):
    return lax.dot_general(a.astype(bf16), b.astype(bf16), _dims(a, 1, 1), preferred_element_type=f32)


def _bdot_tn(a, b):
    return lax.dot_general(a.astype(bf16), b.astype(bf16), _dims(a, 0, 0), preferred_element_type=f32)


def _split(x):
    hi = x.astype(bf16)
    return hi, (x - hi.astype(f32)).astype(bf16)


def _sdot(a, b, exact):
    dims = _dims(a, 1, 0)
    if exact == "rhs":
        hi, lo = _split(a)
        bb = b.astype(bf16)
        return (lax.dot_general(hi, bb, dims, preferred_element_type=f32)
                + lax.dot_general(lo, bb, dims, preferred_element_type=f32))
    hi, lo = _split(b)
    ab = a.astype(bf16)
    return (lax.dot_general(ab, hi, dims, preferred_element_type=f32)
            + lax.dot_general(ab, lo, dims, preferred_element_type=f32))


def _sdot3(a, b):
    dims = _dims(a, 1, 0)
    ah, al = _split(a)
    bh, bl = _split(b)
    return (lax.dot_general(ah, bh, dims, preferred_element_type=f32)
            + lax.dot_general(ah, bl, dims, preferred_element_type=f32)
            + lax.dot_general(al, bh, dims, preferred_element_type=f32))


def _seg_matrix(n, seg=HEAD_DIM):
    r = lax.broadcasted_iota(jnp.int32, (n, n), 0) // seg
    c = lax.broadcasted_iota(jnp.int32, (n, n), 1) // seg
    return (r == c).astype(f32)


def _layer_norm(y, g, b):
    mu = jnp.mean(y, axis=-1, keepdims=True)
    d = y - mu
    var = jnp.mean(d * d, axis=-1, keepdims=True)
    return d * lax.rsqrt(var + LN_EPS) * g + b


def _row_to_col(row, eye):
    return jnp.sum(jnp.where(eye, jnp.broadcast_to(row, eye.shape), 0.0), axis=2, keepdims=True)


def _stack_heads(x):
    h0 = lax.broadcasted_iota(jnp.int32, x.shape, 2) < HEAD_DIM
    return jnp.concatenate([jnp.where(h0, x, 0.0), jnp.where(h0, 0.0, x)], axis=1)


def _proj_body(x_ref, w_ref, b_ref, wg_ref, bg_ref, att_ref, s5_ref, mqk_ref, mv_ref, mo_ref, rw_ref, g_ref):
    xb = x_ref[...].astype(bf16)
    off = 0
    for o_ref, n in zip((att_ref, s5_ref, mqk_ref, mv_ref, mo_ref, rw_ref), PROJ_SPLITS):
        o_ref[...] = jnp.dot(xb, w_ref[:, off:off + n], preferred_element_type=f32) + b_ref[:, off:off + n]
        off += n
    g_ref[...] = lax.dot_general(wg_ref[...], xb, (((1,), (1,)), ((), ())), preferred_element_type=f32) + bg_ref[...]


def _proj(xt, w, b, wg, bg, tm=512):
    T = xt.shape[0]
    n_tot = sum(PROJ_SPLITS)
    outs = [jax.ShapeDtypeStruct((T, n), f32) for n in PROJ_SPLITS] + [jax.ShapeDtypeStruct((16, T), f32)]
    return pl.pallas_call(
        _proj_body, grid=(T // tm,),
        in_specs=[pl.BlockSpec((tm, D_MODEL), lambda i: (i, 0)),
                  pl.BlockSpec((D_MODEL, n_tot), lambda i: (0, 0)),
                  pl.BlockSpec((1, n_tot), lambda i: (0, 0)),
                  pl.BlockSpec((16, D_MODEL), lambda i: (0, 0)),
                  pl.BlockSpec((16, 1), lambda i: (0, 0))],
        out_specs=[pl.BlockSpec((tm, n), lambda i: (i, 0)) for n in PROJ_SPLITS]
        + [pl.BlockSpec((16, tm), lambda i: (0, i))],
        out_shape=outs, compiler_params=_cparams(("parallel",)), name="proj")(xt, w, b, wg, bg)


def _proj_params(w_in, b_in, ml_ib, ml_fb):
    o = np.cumsum((0, 256, 128, 128, RW_COLS, 256, 512, 256, 8, 8, 256))
    sl = lambda i: (w_in[:, o[i]:o[i + 1]], b_in[o[i]:o[i + 1]])
    (wq, bq), (wk, bk), (wv, bv), (wrw, brw), (ws5, bs5), (wqk, bqk), (wmv, bmv), (wi, bi), (wf, bf), (wo, bo) = (
        sl(i) for i in range(10))
    zw, zb = jnp.zeros((D_MODEL, HEAD_DIM), f32), jnp.zeros((HEAD_DIM,), f32)
    wq_e, bq_e = [], []
    for h in range(ATT_HEADS):
        wh, bh = wq[:, 64 * h:64 * h + 64], bq[64 * h:64 * h + 64]
        wq_e += [wh, zw] if h // 2 == 0 else [zw, wh]
        bq_e += [bh, zb] if h // 2 == 0 else [zb, bh]
    w = jnp.concatenate(wq_e + [wk, wv, ws5, wqk, wmv, wo, wrw], axis=1)
    b = jnp.concatenate(bq_e + [bk, bv, bs5, bqk, bmv, bo, brw])
    wg = jnp.concatenate([wi, wf], axis=1).T
    bg = jnp.concatenate([bi + ml_ib.reshape(-1), bf + ml_fb.reshape(-1)])
    return w.astype(bf16), b[None, :], wg.astype(bf16), bg[:, None]


def _att_prep_body(a_ref, cos_ref, sin_ref, gain_ref, q_ref, k_ref, v_ref):
    x = a_ref[:, 0:640]
    ms = _sdot(x * x, _seg_matrix(640), "rhs") * (1.0 / HEAD_DIM)
    xn = x * lax.rsqrt(ms + QK_EPS) * gain_ref[...]
    lane = lax.broadcasted_iota(jnp.int32, xn.shape, 1)
    partner = jnp.where((lane % 32) < 16, pltpu.roll(xn, 640 - 16, 1), pltpu.roll(xn, 16, 1))
    cos = jnp.concatenate([cos_ref[...]] * 5, axis=1)
    sin = jnp.concatenate([sin_ref[...]] * 5, axis=1)
    rot = xn * cos + partner * sin
    q_ref[...] = rot[:, 0:512].astype(bf16)
    k_ref[...] = rot[:, 512:640].astype(bf16)
    v_ref[:, 0:LANES] = a_ref[:, 640:768].astype(bf16)
    v_ref[:, LANES:2 * LANES] = jnp.ones((x.shape[0], LANES), bf16)


def _att_prep(att, cos, sin, gain, B, S, tq=512):
    T = B * S
    nb = S // tq
    return pl.pallas_call(
        _att_prep_body, grid=(B, nb),
        in_specs=[pl.BlockSpec((tq, 768), lambda b, i: (b * nb + i, 0)),
                  pl.BlockSpec((tq, LANES), lambda b, i: (i, 0)),
                  pl.BlockSpec((tq, LANES), lambda b, i: (i, 0)),
                  pl.BlockSpec((1, 640), lambda b, i: (0, 0))],
        out_specs=[pl.BlockSpec((tq, 512), lambda b, i: (b * nb + i, 0)),
                   pl.BlockSpec((tq, LANES), lambda b, i: (b * nb + i, 0)),
                   pl.BlockSpec((tq, 2 * LANES), lambda b, i: (b * nb + i, 0))],
        out_shape=[jax.ShapeDtypeStruct((T, 512), bf16), jax.ShapeDtypeStruct((T, LANES), bf16),
                   jax.ShapeDtypeStruct((T, 2 * LANES), bf16)],
        compiler_params=_cparams(("parallel", "parallel")), name="att_prep")(att, cos, sin, gain)


def _rope_tables(S):
    t = np.arange(S)
    row = (t // GRID_W).astype(np.float32)
    col = (t % GRID_W).astype(np.float32)
    n = 16
    inv = np.power(np.float32(ROPE_THETA), -np.arange(n, dtype=np.float32) / n).astype(np.float32)
    ar = jnp.asarray(row)[:, None] * jnp.asarray(inv)
    ac = jnp.asarray(col)[:, None] * jnp.asarray(inv)
    cos = jnp.concatenate([jnp.cos(ar), jnp.cos(ar), jnp.cos(ac), jnp.cos(ac)], axis=1)
    sin = jnp.concatenate([-jnp.sin(ar), jnp.sin(ar), -jnp.sin(ac), jnp.sin(ac)], axis=1)
    return jnp.concatenate([cos, cos], axis=1), jnp.concatenate([sin, sin], axis=1)


def _flash_body(q_ref, k_ref, v_ref, o_ref, acc_ref, *m_scratch, tk, track_max):
    tq = q_ref.shape[0]
    nk = k_ref.shape[0] // tk
    q2 = jnp.concatenate([q_ref[:, 0:LANES], q_ref[:, LANES:2 * LANES]], axis=0)
    acc_ref[...] = jnp.zeros(acc_ref.shape, f32)
    if track_max:
        m_ref, = m_scratch
        m_ref[...] = jnp.full(m_ref.shape, NEG, f32)

    def step(j, carry):
        rows = pl.ds(pl.multiple_of(j * tk, tk), tk)
        s = lax.dot_general(q2, k_ref[rows, :], (((1,), (1,)), ((), ())), preferred_element_type=f32)
        if track_max:
            m_old = m_ref[...]
            m_new = jnp.maximum(m_old, jnp.max(s, axis=1, keepdims=True))
            p = jnp.exp(s - m_new).astype(bf16)
            acc_ref[...] = jnp.exp(m_old - m_new) * acc_ref[...] + jnp.dot(p, v_ref[rows, :], preferred_element_type=f32)
            m_ref[...] = m_new
        else:
            acc_ref[...] += jnp.dot(jnp.exp(s).astype(bf16), v_ref[rows, :], preferred_element_type=f32)
        return carry

    lax.fori_loop(0, nk, step, 0)
    o = acc_ref[:, 0:LANES] / acc_ref[:, LANES:2 * LANES]
    o_ref[...] = jnp.concatenate([o[0:tq], o[tq:2 * tq]], axis=1)


SCORE_BOUND_MAX = 60.0


def _flash(q, k, v, score_bound, B, S, tq=256, tk=8192):
    T = B * S
    nb = S // tq
    tk = min(tk, S)

    def call(track_max):
        scratch = [pltpu.VMEM((2 * tq, 2 * LANES), f32)] + ([pltpu.VMEM((2 * tq, 1), f32)] if track_max else [])
        return pl.pallas_call(
            functools.partial(_flash_body, tk=tk, track_max=track_max), grid=(B, ATT_KV_HEADS, nb),
            in_specs=[pl.BlockSpec((tq, 2 * LANES), lambda b, g, i: (b * nb + i, g)),
                      pl.BlockSpec((S, LANES), lambda b, g, i: (b, 0)),
                      pl.BlockSpec((S, 2 * LANES), lambda b, g, i: (b, 0))],
            out_specs=pl.BlockSpec((tq, 2 * LANES), lambda b, g, i: (b * nb + i, g)),
            out_shape=jax.ShapeDtypeStruct((T, 512), f32), scratch_shapes=scratch,
            compiler_params=_cparams(("parallel", "parallel", "parallel")),
            name="flash_safe" if track_max else "flash")(q, k, v)

    return lax.cond(score_bound <= SCORE_BOUND_MAX, lambda: call(False), lambda: call(True))


def _halo_specs(width, tm, B, S):
    nb = S // tm
    r8 = tm // 8
    last8 = B * S // 8 - 1

    def main(b, i):
        return (b * nb + i, 0)

    def prev(b, i):
        return (jnp.maximum(b * (S // 8) + i * r8 - 1, 0), 0)

    def nxt(b, i):
        return (jnp.minimum(b * (S // 8) + (i + 1) * r8, last8), 0)

    return [pl.BlockSpec((tm, width), main), pl.BlockSpec((8, width), prev), pl.BlockSpec((8, width), nxt)]


def _fill_halo(buf_ref, x_ref, p_ref, n_ref):
    tm = x_ref.shape[0]
    i = pl.program_id(1)
    last = pl.num_programs(1) - 1
    buf_ref[pl.ds(8, tm), :] = x_ref[...]
    buf_ref[pl.ds(0, 8), :] = jnp.where(i > 0, p_ref[...], 0.0)
    buf_ref[pl.ds(8 + tm, 8), :] = jnp.where(i < last, n_ref[...], 0.0)


def _rw_prep_body(x_ref, p_ref, n_ref, mix_ref, w2_ref, a2_ref, g2_ref, vec_ref,
                  r_ref, k_ref, v_ref, an_ref, bn_ref, lw_ref, gate_ref, bonus_ref, buf_ref):
    tm = x_ref.shape[0]
    _fill_halo(buf_ref, x_ref, p_ref, n_ref)
    x = x_ref[...]
    p = x + mix_ref[0:1, :] * (buf_ref[pl.ds(7, tm), :] - x) + mix_ref[1:2, :] * (buf_ref[pl.ds(9, tm), :] - x)
    r, k, v = p[:, 0:256], p[:, 256:512], p[:, 512:768]
    w0f, w0b, a0, k_k, k_a, r_k = (vec_ref[j:j + 1, :] for j in range(6))
    dec = _bdot(jnp.tanh(p[:, 768:896]), w2_ref[...])
    z = p[:, 896:1088]
    a = _sigmoid(a0 + _bdot(z, a2_ref[...]))
    gate_ref[...] = _bdot(_sigmoid(z), g2_ref[...])
    seg = _seg_matrix(BRANCH_W)
    kk = k * k_k
    kk = kk / jnp.maximum(jnp.sqrt(_sdot(kk * kk, seg, "rhs")), 1e-12)
    k2 = k * (1.0 + (a - 1.0) * k_a)
    bonus_ref[...] = _sdot(r * k2 * r_k, seg, "rhs") * v
    r_ref[...] = r
    k_ref[...] = k2
    v_ref[...] = v
    an_ref[...] = -kk
    bn_ref[...] = kk * a
    lw_ref[0] = -jnp.exp(-_softplus(-(w0f + dec[:, 0:256])) - 0.5)
    lw_ref[1] = -jnp.exp(-_softplus(-(w0b + dec[:, 256:512])) - 0.5)


def _rw_prep(rw, mix, w2, a2, g2, vec, B, S, tm=512):
    T = B * S
    nb = S // tm
    o256 = pl.BlockSpec((tm, BRANCH_W), lambda b, i: (b * nb + i, 0))
    full = lambda shape: pl.BlockSpec(shape, lambda b, i: (0,) * len(shape))
    return pl.pallas_call(
        _rw_prep_body, grid=(B, nb),
        in_specs=_halo_specs(RW_COLS, tm, B, S) + [full((2, RW_COLS)), full((128, 512)), full((192, 256)),
                                                   full((192, 256)), full((8, 256))],
        out_specs=[o256] * 5 + [pl.BlockSpec((2, tm, BRANCH_W), lambda b, i: (0, b * nb + i, 0)), o256, o256],
        out_shape=[jax.ShapeDtypeStruct((T, BRANCH_W), f32)] * 5 + [jax.ShapeDtypeStruct((2, T, BRANCH_W), f32)]
        + [jax.ShapeDtypeStruct((T, BRANCH_W), f32)] * 2,
        scratch_shapes=[pltpu.VMEM((tm + 16, RW_COLS), f32)],
        compiler_params=_cparams(("parallel", "parallel")), name="rw_prep")(rw, rw, rw, mix, w2, a2, g2, vec)


def _pair_masks(rev, nbatch):
    n = 2 * CHUNK
    r = lax.broadcasted_iota(jnp.int32, (nbatch, n, n), 1)
    c = lax.broadcasted_iota(jnp.int32, (nbatch, n, n), 2)
    same = (r // CHUNK) == (c // CHUNK)
    if rev:
        return r, c, same & (c > r), same & (c >= r)
    return r, c, same & (c < r), same & (c <= r)


def _rw_chunk(st, r, k, v, an, bn, lw, rev):
    L = CHUNK
    N = r.shape[0]
    ri = lax.broadcasted_iota(jnp.int32, (N, L, L), 1)
    ci = lax.broadcasted_iota(jnp.int32, (N, L, L), 2)
    tri = ((ci >= ri) if rev else (ci <= ri)).astype(f32)
    cs = _sdot(tri, lw, "lhs")
    tot = jnp.sum(lw, axis=1, keepdims=True)
    e_neg = jnp.exp(-cs)
    a2 = _stack_heads(an * jnp.exp(cs - lw))
    r2 = _stack_heads(r * jnp.exp(cs))
    b2 = _stack_heads(bn * e_neg)
    k2 = _stack_heads(k * e_neg)
    v2 = _stack_heads(v)
    rr, cc, strict, incl = _pair_masks(rev, N)
    n2 = 2 * L
    g = _bdot_nt(jnp.concatenate([a2, r2], axis=1), jnp.concatenate([b2, k2], axis=1))
    mab = jnp.where(strict, g[:, 0:n2, 0:n2], 0.0)
    mak = jnp.where(strict, g[:, 0:n2, n2:2 * n2], 0.0)
    pb = jnp.where(incl, g[:, n2:2 * n2, 0:n2], 0.0)
    pk = jnp.where(incl, g[:, n2:2 * n2, n2:2 * n2], 0.0)
    eye = rr == cc
    m8 = jnp.where((rr // 8) == (cc // 8), mab, 0.0)
    x = eye.astype(f32) + m8
    p = _bdot(m8, m8)
    x = x + _bdot(x, p)
    p = _bdot(p, p)
    x = x + _bdot(x, p)
    n = 8
    while n < L:
        e = jnp.where(((rr // (2 * n)) == (cc // (2 * n))) & ((rr // n) != (cc // n)), mab, 0.0)
        x = x + _bdot(_bdot(x, e), x)
        n *= 2
    wu = _bdot(x, jnp.concatenate([a2, _bdot(mak, v2)], axis=2))
    pwu = _bdot(pb, wu)
    rh = r2 + pwu[:, :, 0:LANES]
    y2 = pwu[:, :, LANES:2 * LANES] + _bdot(jnp.concatenate([pk, rh], axis=2), jnp.concatenate([v2, st], axis=1))
    y = y2[:, 0:L] + y2[:, L:2 * L]
    gam = _row_to_col(jnp.exp(tot), eye)
    bwu = _bdot_tn(b2, wu)
    st = gam * (st + _bdot(bwu[:, :, 0:LANES], st) + bwu[:, :, LANES:2 * LANES] + _bdot_tn(k2, v2))
    return y, st


def _load_pairs(ref, rows):
    return jnp.concatenate([ref[:, rows, 0:LANES], ref[:, rows, LANES:2 * LANES]], axis=0)


def _store_pairs(ref, rows, y):
    nb = ref.shape[0]
    ref[:, rows, 0:LANES] = y[0:nb]
    ref[:, rows, LANES:2 * LANES] = y[nb:2 * nb]


def _rw_scan_body(r_ref, k_ref, v_ref, an_ref, bn_ref, lw_ref, y_ref, st_ref, *, rev):
    nch = r_ref.shape[1] // CHUNK

    @pl.when(pl.program_id(0) == 0)
    def _():
        st_ref[...] = jnp.zeros(st_ref.shape, f32)

    def step(cc, carry):
        c = (nch - 1 - cc) if rev else cc
        rows = pl.ds(pl.multiple_of(c * CHUNK, CHUNK), CHUNK)
        y, st = _rw_chunk(st_ref[...], *(_load_pairs(ref, rows) for ref in (r_ref, k_ref, v_ref, an_ref, bn_ref)),
                          _load_pairs(lw_ref.at[0], rows), rev)
        _store_pairs(y_ref, rows, y)
        st_ref[...] = st
        return carry

    lax.fori_loop(0, nch, step, 0)


def _rw_scan(r, k, v, an, bn, lw, B, S, rev, ts=256):
    nb = S // ts
    d = 1 if rev else 0
    blk = (lambda i: nb - 1 - i) if rev else (lambda i: i)
    io = pl.BlockSpec((B, ts, BRANCH_W), lambda i: (0, blk(i), 0))
    return pl.pallas_call(
        functools.partial(_rw_scan_body, rev=rev), grid=(nb,),
        in_specs=[io] * 5 + [pl.BlockSpec((1, B, ts, BRANCH_W), lambda i: (d, 0, blk(i), 0))],
        out_specs=io, out_shape=jax.ShapeDtypeStruct((B, S, BRANCH_W), f32),
        scratch_shapes=[pltpu.VMEM((B * BRANCH_W // LANES, LANES, LANES), f32)],
        compiler_params=_cparams(("arbitrary",)),
        name="rw_scan_bwd" if rev else "rw_scan_fwd")(r, k, v, an, bn, lw)


def _rw_finish_body(yf_ref, yb_ref, gate_ref, bonus_ref, gn_ref, o_ref):
    y = yf_ref[...] + yb_ref[...]
    seg = _seg_matrix(BRANCH_W)
    mu = _sdot(y, seg, "rhs") * (1.0 / HEAD_DIM)
    d = y - mu
    var = _sdot(d * d, seg, "rhs") * (1.0 / HEAD_DIM)
    yn = d * lax.rsqrt(var + RW_GN_EPS) * gn_ref[0:1, :] + gn_ref[1:2, :]
    o_ref[...] = (yn + bonus_ref[...]) * gate_ref[...]


def _rw_finish(yf, yb, gate, bonus, gn, tm=1024):
    T = yf.shape[0]
    io = pl.BlockSpec((tm, BRANCH_W), lambda i: (i, 0))
    return pl.pallas_call(
        _rw_finish_body, grid=(T // tm,), in_specs=[io] * 4 + [pl.BlockSpec((2, BRANCH_W), lambda i: (0, 0))],
        out_specs=io, out_shape=jax.ShapeDtypeStruct((T, BRANCH_W), f32),
        compiler_params=_cparams(("parallel",)), name="rw_finish")(yf, yb, gate, bonus, gn)


def _rwkv(rw, prm, B, S):
    mix, w2, a2, g2, vec, gn = prm
    r, k, v, an, bn, lw, gate, bonus = _rw_prep(rw, mix, w2, a2, g2, vec, B, S)
    seq = [t.reshape(B, S, BRANCH_W) for t in (r, k, v, an, bn)] + [lw.reshape(2, B, S, BRANCH_W)]
    yf = _rw_scan(*seq, B, S, rev=False).reshape(B * S, BRANCH_W)
    yb = _rw_scan(*seq, B, S, rev=True).reshape(B * S, BRANCH_W)
    return _rw_finish(yf, yb, gate, bonus, gn)


def _rw_params(mix, w0, w2, a0, a2, g2, k_k, k_a, r_k, gn_g, gn_b):
    z = jnp.zeros((64, 256), f32)
    w2c = jnp.concatenate([jnp.concatenate([w2[0], z], axis=1), jnp.concatenate([z, w2[1]], axis=1)], axis=0)
    a2p = jnp.concatenate([a2, jnp.zeros((128, 256), f32)], axis=0)
    g2p = jnp.concatenate([jnp.zeros((64, 256), f32), g2], axis=0)
    vec = jnp.stack([w0[0], w0[1], a0, k_k, k_a, r_k.reshape(-1), jnp.zeros_like(a0), jnp.zeros_like(a0)])
    return mix, w2c.astype(bf16), a2p.astype(bf16), g2p.astype(bf16), vec, jnp.stack([gn_g, gn_b])


def _s5_scan_body(u_ref, bcat_ref, cre_ref, cim_ref, lre_ref, lim_ref, y_ref, sre_ref, sim_ref, bre_ref, bim_ref):
    ts = u_ref.shape[0]
    n = S5_GROUPS * S5_STATE

    @pl.when(pl.program_id(0) == 0)
    def _():
        sre_ref[...] = jnp.zeros(sre_ref.shape, f32)
        sim_ref[...] = jnp.zeros(sim_ref.shape, f32)

    u = u_ref[...].reshape(ts * 8, BRANCH_W).astype(bf16)
    fwd = lax.broadcasted_iota(jnp.int32, (ts, 8, n), 1) < 4

    def bu(part):
        return jnp.dot(u, bcat_ref[:, part * n:(part + 1) * n], preferred_element_type=f32).reshape(ts, 8, n)

    bre_ref[...] = jnp.where(fwd, bu(0), bu(2))
    bim_ref[...] = jnp.where(fwd, bu(1), bu(3))
    lre = lre_ref[...]
    lim = lim_ref[...]

    def step(t, carry):
        sre, sim = carry
        nre = lre * sre - lim * sim + bre_ref[t]
        nim = lre * sim + lim * sre + bim_ref[t]
        bre_ref[t] = nre
        bim_ref[t] = nim
        return nre, nim

    sre, sim = lax.fori_loop(0, ts, step, (sre_ref[...], sim_ref[...]), unroll=4)
    sre_ref[...] = sre
    sim_ref[...] = sim
    y = (_bdot(bre_ref[...].reshape(ts * 8, n), cre_ref[...])
         - _bdot(bim_ref[...].reshape(ts * 8, n), cim_ref[...])).reshape(ts, 8, 2 * BRANCH_W)
    fwd_o = lax.broadcasted_iota(jnp.int32, (ts, 8, BRANCH_W), 1) < 4
    y_ref[...] = jnp.where(fwd_o, y[:, :, 0:BRANCH_W], y[:, :, BRANCH_W:2 * BRANCH_W])


def _s5_scan(u2, bcat, cre, cim, lre, lim, ts=128):
    S = u2.shape[0]
    n = S5_GROUPS * S5_STATE
    full = lambda shape: pl.BlockSpec(shape, lambda i: (0,) * len(shape))
    return pl.pallas_call(
        _s5_scan_body, grid=(S // ts,),
        in_specs=[pl.BlockSpec((ts, 8, BRANCH_W), lambda i: (i, 0, 0)), full((BRANCH_W, 4 * n)),
                  full((n, 2 * BRANCH_W)), full((n, 2 * BRANCH_W)), full((8, n)), full((8, n))],
        out_specs=pl.BlockSpec((ts, 8, BRANCH_W), lambda i: (i, 0, 0)),
        out_shape=jax.ShapeDtypeStruct((S, 8, BRANCH_W), f32),
        scratch_shapes=[pltpu.VMEM((8, n), f32), pltpu.VMEM((8, n), f32),
                        pltpu.VMEM((ts, 8, n), f32), pltpu.VMEM((ts, 8, n), f32)],
        compiler_params=_cparams(("arbitrary",)), name="s5_scan")(u2, bcat, cre, cim, lre, lim)


def _s5_finish_body(y_ref, u_ref, d_ref, w_ref, b_ref, o_ref):
    y = y_ref[...] + u_ref[...] * d_ref[...]
    y = 0.5 * y * (1.0 + jnp.tanh(math.sqrt(2.0 / math.pi) * (y + 0.044715 * (y * y * y))))
    o_ref[...] = y * _sigmoid(_bdot(y, w_ref[...]) + b_ref[...])


def _s5_finish(y, u, d, w, b, tm=1024):
    T = y.shape[0]
    io = pl.BlockSpec((tm, BRANCH_W), lambda i: (i, 0))
    vec = pl.BlockSpec((1, BRANCH_W), lambda i: (0, 0))
    return pl.pallas_call(
        _s5_finish_body, grid=(T // tm,),
        in_specs=[io, io, vec, pl.BlockSpec((BRANCH_W, BRANCH_W), lambda i: (0, 0)), vec],
        out_specs=io, out_shape=jax.ShapeDtypeStruct((T, BRANCH_W), f32),
        compiler_params=_cparams(("parallel",)), name="s5_finish")(y, u, d, w, b)


def _s5_params(lam_re, lam_im, log_dt, b_re, b_im, c_re, c_im):
    G, P, C = S5_GROUPS, S5_STATE, S5_GROUP
    eye = jnp.eye(G, dtype=f32)
    b_c = lax.complex(b_re, b_im)
    bcat, cre, cim, lre, lim = [], [], [], [], []
    for d in range(2):
        lam = lax.complex(jnp.minimum(lam_re[d], -1e-4), lam_im[d])
        lam_bar = jnp.exp(lam * jnp.exp(log_dt[d])[:, None])
        b_bar = ((lam_bar - 1.0) / lam)[..., None] * b_c
        for part in (jnp.real(b_bar), jnp.imag(b_bar)):
            bcat.append(jnp.einsum('gh,gpc->gchp', eye, part).reshape(G * C, G * P))
        cre.append(jnp.einsum('gh,gcp->gphc', eye, c_re[d]).reshape(G * P, G * C))
        cim.append(jnp.einsum('gh,gcp->gphc', eye, c_im[d]).reshape(G * P, G * C))
        lre.append(jnp.broadcast_to(jnp.real(lam_bar).reshape(1, G * P), (4, G * P)))
        lim.append(jnp.broadcast_to(jnp.imag(lam_bar).reshape(1, G * P), (4, G * P)))
    return (jnp.concatenate(bcat, axis=1).astype(bf16), jnp.concatenate(cre, axis=1).astype(bf16),
            jnp.concatenate(cim, axis=1).astype(bf16), jnp.concatenate(lre, axis=0), jnp.concatenate(lim, axis=0))


def _s5(u, prm, fin, B, S):
    u3 = u.reshape(B, S, BRANCH_W).transpose(1, 0, 2)
    u2 = jnp.concatenate([u3, u3[::-1]], axis=1)
    y2 = _s5_scan(u2, *prm)
    y = (y2[:, 0:B] + y2[::-1, B:2 * B]).transpose(1, 0, 2).reshape(B * S, BRANCH_W)
    return _s5_finish(y, u, *fin)


def _ml_prep_body(x_ref, p_ref, n_ref, w_ref, b_ref, q_ref, k_ref, buf_ref):
    tm = x_ref.shape[0]
    _fill_halo(buf_ref, x_ref, p_ref, n_ref)
    y = b_ref[...] + w_ref[2:3, :] * x_ref[...]
    for j in (0, 1, 3, 4):
        y = y + w_ref[j:j + 1, :] * buf_ref[pl.ds(6 + j, tm), :]
    y = y * _sigmoid(y)
    q_ref[...] = y[:, 0:BRANCH_W]
    k_ref[...] = y[:, BRANCH_W:2 * BRANCH_W] * (HEAD_DIM ** -0.5)


def _ml_prep(mqk, w, b, B, S, tm=512):
    T = B * S
    nb = S // tm
    o = pl.BlockSpec((tm, BRANCH_W), lambda b_, i: (b_ * nb + i, 0))
    return pl.pallas_call(
        _ml_prep_body, grid=(B, nb),
        in_specs=_halo_specs(512, tm, B, S) + [pl.BlockSpec((5, 512), lambda b_, i: (0, 0)),
                                               pl.BlockSpec((1, 512), lambda b_, i: (0, 0))],
        out_specs=[o, o], out_shape=[jax.ShapeDtypeStruct((T, BRANCH_W), f32)] * 2,
        scratch_shapes=[pltpu.VMEM((tm + 16, 512), f32)],
        compiler_params=_cparams(("parallel", "parallel")), name="ml_prep")(mqk, mqk, mqk, w, b)


def _ml_chunk(state, q, k, v, li, lfp, rev):
    cn, m_row = state
    L = CHUNK
    N = q.shape[0]
    rr, cc, _, incl = _pair_masks(rev, N)
    same = (rr // L) == (cc // L)
    eye = rr == cc
    lane = lax.broadcasted_iota(jnp.int32, (N, 1, 2 * L), 2)
    lf = jnp.minimum(lfp, 0.0) - jnp.log(1.0 + jnp.exp(-jnp.abs(lfp)))
    lf8 = jnp.broadcast_to(lf, (N, 8, 2 * L))
    cum = (same & ((rr >= cc) if rev else (rr <= cc))).astype(f32)
    b_row = _sdot(lf8, cum, "rhs")[:, 0:1]
    g_row = _sdot(lf8, same.astype(f32), "rhs")[:, 0:1]
    w_end = g_row - b_row + li
    m0 = jnp.max(jnp.where(lane < L, w_end, NEG), axis=2, keepdims=True)
    m1 = jnp.max(jnp.where(lane < L, NEG, w_end), axis=2, keepdims=True)
    m_loc = jnp.where(lane < L, m0, m1)
    e_col = _row_to_col(jnp.exp(w_end - m_loc), eye)
    b_col = _row_to_col(b_row, eye)
    q2, k2, v2 = _stack_heads(q), _stack_heads(k), _stack_heads(v)
    v1 = jnp.concatenate([v2, _stack_heads(jnp.ones_like(v))], axis=2)
    log_inter = b_col + _row_to_col(m_row, eye)
    log_intra = jnp.where(incl, b_col - b_row + li, NEG)
    m_r = jnp.maximum(log_inter, jnp.max(log_intra, axis=2, keepdims=True))
    s = _bdot_nt(q2, k2) * jnp.exp(log_intra - m_r)
    inter = jnp.exp(log_inter - m_r)
    nd = _bdot(s, v1) + inter * _bdot(q2, cn)
    h2 = nd[:, :, 0:LANES] / jnp.maximum(jnp.abs(nd[:, :, LANES:2 * LANES]), jnp.exp(-m_r))
    h = h2[:, 0:L] + h2[:, L:2 * L]
    m_new = jnp.maximum(g_row + m_row, m_loc)
    a = jnp.exp(g_row + m_row - m_new)
    bb = jnp.exp(m_loc - m_new)
    cn = jnp.concatenate([a, a], axis=2) * cn + jnp.concatenate([bb, bb], axis=2) * _bdot_tn(e_col * k2, v1)
    return h, (cn, m_new)


def _ml_scan_body(q_ref, k_ref, v_ref, g_ref, h_ref, c_ref, m_ref, *, rev):
    nch = q_ref.shape[1] // CHUNK

    @pl.when(pl.program_id(0) == 0)
    def _():
        c_ref[...] = jnp.zeros(c_ref.shape, f32)
        m_ref[...] = jnp.zeros(m_ref.shape, f32)

    def step(cc, carry):
        c = (nch - 1 - cc) if rev else cc
        rows = pl.ds(pl.multiple_of(c * CHUNK, CHUNK), CHUNK)
        li, lfp = (jnp.concatenate([g_ref[t, 0, 0, :, pl.ds(c, 1), :], g_ref[t, 0, 1, :, pl.ds(c, 1), :]], axis=0)
                   for t in range(2))
        h, (cn, m_row) = _ml_chunk((c_ref[...], m_ref[...]), _load_pairs(q_ref, rows), _load_pairs(k_ref, rows),
                                   _load_pairs(v_ref, rows), li, lfp, rev)
        _store_pairs(h_ref, rows, h)
        c_ref[...] = cn
        m_ref[...] = m_row
        return carry

    lax.fori_loop(0, nch, step, 0)


def _ml_scan(q, k, v, g, B, S, rev, ts=512):
    nb = S // ts
    nch = ts // CHUNK
    d = 1 if rev else 0
    nchain = B * BRANCH_W // LANES
    blk = (lambda i: nb - 1 - i) if rev else (lambda i: i)
    io = pl.BlockSpec((B, ts, BRANCH_W), lambda i: (0, blk(i), 0))
    return pl.pallas_call(
        functools.partial(_ml_scan_body, rev=rev), grid=(nb,),
        in_specs=[io, io, io, pl.BlockSpec((2, 1, 2, B, nch, LANES), lambda i: (0, d, 0, 0, blk(i), 0))],
        out_specs=io, out_shape=jax.ShapeDtypeStruct((B, S, BRANCH_W), f32),
        scratch_shapes=[pltpu.VMEM((nchain, LANES, 2 * LANES), f32), pltpu.VMEM((nchain, 1, LANES), f32)],
        compiler_params=_cparams(("arbitrary",)),
        name="ml_scan_bwd" if rev else "ml_scan_fwd")(q, k, v, g)


def _mlstm(mqk, mv, gt, conv_w, conv_b, B, S):
    q, k = _ml_prep(mqk, conv_w, conv_b, B, S)
    g = gt.reshape(2, 2, 2, 2, B, S // CHUNK, CHUNK).transpose(0, 1, 2, 4, 5, 3, 6).reshape(2, 2, 2, B, S // CHUNK, LANES)
    seq = [t.reshape(B, S, BRANCH_W) for t in (q, k, mv)]
    return (_ml_scan(*seq, g, B, S, rev=False).reshape(B * S, BRANCH_W),
            _ml_scan(*seq, g, B, S, rev=True).reshape(B * S, BRANCH_W))


def _merge_body(x_ref, att_ref, rw_ref, s5_ref, hf_ref, hb_ref, mo_ref, wg_ref, bg_ref, wba_ref, wb_ref, wo_ref,
                ln_ref, o_ref):
    x = x_ref[...]
    xb = x.astype(bf16)
    ml = _sigmoid(mo_ref[...]) * (hf_ref[...] + hb_ref[...])
    branches = (att_ref[...], rw_ref[...], s5_ref[...], ml)
    merged = None
    for n in range(4):
        gate = _sigmoid(jnp.dot(xb, wg_ref[n], preferred_element_type=f32) + bg_ref[n:n + 1, :])
        wide = _bdot(branches[n], wba_ref[...] if n == 0 else wb_ref[n - 1])
        merged = gate * wide if merged is None else merged + gate * wide
    y = ALPHA * x + _bdot(merged, wo_ref[...])
    o_ref[...] = _layer_norm(y, ln_ref[0:1, :], ln_ref[1:2, :])


def _merge(xt, att, rw, s5, hf, hb, mo, wg, bg, wba, wb, wo, ln, tm=256):
    T = xt.shape[0]
    row = lambda n: pl.BlockSpec((tm, n), lambda i: (i, 0))
    const = lambda shape: pl.BlockSpec(shape, lambda i: (0,) * len(shape), pipeline_mode=pl.Buffered(1))
    return pl.pallas_call(
        _merge_body, grid=(T // tm,),
        in_specs=[row(D_MODEL), row(512)] + [row(BRANCH_W)] * 5
        + [const((4, D_MODEL, D_MODEL)), const((4, D_MODEL)), const((512, D_MODEL)), const((3, BRANCH_W, D_MODEL)),
           const((D_MODEL, D_MODEL)), const((2, D_MODEL))],
        out_specs=row(D_MODEL), out_shape=jax.ShapeDtypeStruct((T, D_MODEL), f32),
        compiler_params=_cparams(("parallel",)), name="merge")(xt, att, rw, s5, hf, hb, mo, wg, bg, wba, wb, wo, ln)


def _att_branch_weight(wb):
    z = jnp.zeros((HEAD_DIM, D_MODEL), f32)
    parts = []
    for h in range(ATT_HEADS):
        wh = wb[64 * h:64 * h + 64]
        parts += [wh, z] if h // 2 == 0 else [z, wh]
    return jnp.concatenate(parts, axis=0)


def _ffn_body(x_ref, w1_ref, w3_ref, w2_ref, ln_ref, o_ref, acc_ref):
    j = pl.program_id(1)
    xb = x_ref[...].astype(bf16)
    h1 = jnp.dot(xb, w1_ref[...], preferred_element_type=f32)
    h3 = jnp.dot(xb, w3_ref[...], preferred_element_type=f32)
    part = _bdot(h1 * _sigmoid(h1) * h3, w2_ref[...])

    @pl.when(j == 0)
    def _():
        acc_ref[...] = part

    @pl.when(j > 0)
    def _():
        acc_ref[...] += part

    @pl.when(j == pl.num_programs(1) - 1)
    def _():
        o_ref[...] = _layer_norm(ALPHA * x_ref[...] + acc_ref[...], ln_ref[0:1, :], ln_ref[1:2, :])


def _ffn(xt, w1, w3, w2, ln, tm=512, tf=1408):
    T = xt.shape[0]
    dff = w1.shape[1]
    return pl.pallas_call(
        _ffn_body, grid=(T // tm, dff // tf),
        in_specs=[pl.BlockSpec((tm, D_MODEL), lambda i, j: (i, 0)), pl.BlockSpec((D_MODEL, tf), lambda i, j: (0, j)),
                  pl.BlockSpec((D_MODEL, tf), lambda i, j: (0, j)), pl.BlockSpec((tf, D_MODEL), lambda i, j: (j, 0)),
                  pl.BlockSpec((2, D_MODEL), lambda i, j: (0, 0))],
        out_specs=pl.BlockSpec((tm, D_MODEL), lambda i, j: (i, 0)),
        out_shape=jax.ShapeDtypeStruct((T, D_MODEL), f32), scratch_shapes=[pltpu.VMEM((tm, D_MODEL), f32)],
        compiler_params=_cparams(("parallel", "arbitrary")), name="ffn")(xt, w1, w3, w2, ln)


MOE_TILE = 512
SC_WINDOW = 128
SC_WORDS = 256


def _pack_words(x):
    bits = lax.bitcast_convert_type(x.astype(bf16).astype(f32), jnp.int32)
    half = D_MODEL // 2
    w = lax.shift_right_logical(bits[:, 0:half], 16) | bits[:, half:D_MODEL]
    return w[:, 0:SC_WORDS], w[:, SC_WORDS:2 * SC_WORDS]


def _unpack_words(wa, wb):
    w = jnp.concatenate([wa, wb], axis=1)
    lo = lax.bitcast_convert_type(lax.shift_left(w, 16), f32)
    hi = lax.bitcast_convert_type(w & jnp.int32(-65536), f32)
    return jnp.concatenate([lo, hi], axis=1)


def _router_body(x_ref, rt_ref, xa_ref, xb_ref, meta_ref, cnt_ref, run_ref):
    tb = x_ref.shape[0]

    @pl.when(pl.program_id(0) == 0)
    def _():
        run_ref[...] = jnp.zeros(run_ref.shape, f32)

    x = x_ref[...]
    xa_ref[...], xb_ref[...] = _pack_words(x)
    logits = _sdot3(x, rt_ref[...])
    lane = lax.broadcasted_iota(jnp.int32, logits.shape, 1)
    lg = jnp.where(lane < N_EXPERTS, logits, NEG)
    v1 = jnp.max(lg, axis=1, keepdims=True)
    i1 = jnp.min(jnp.where(lg == v1, lane, LANES), axis=1, keepdims=True)
    lg2 = jnp.where(lane == i1, NEG, lg)
    v2 = jnp.max(lg2, axis=1, keepdims=True)
    i2 = jnp.min(jnp.where(lg2 == v2, lane, LANES), axis=1, keepdims=True)
    e2 = jnp.exp(v2 - v1)
    sel1, sel2 = lane == i1, lane == i2
    mask = (sel1 | sel2).astype(f32)
    r = lax.broadcasted_iota(jnp.int32, (tb, tb), 0)
    c = lax.broadcasted_iota(jnp.int32, (tb, tb), 1)
    rank = _bdot((c < r).astype(f32), mask) + run_ref[0:1, :]
    run_ref[...] = run_ref[...] + jnp.sum(mask, axis=0, keepdims=True)
    rank1 = jnp.sum(jnp.where(sel1, rank, 0.0), axis=1, keepdims=True)
    rank2 = jnp.sum(jnp.where(sel2, rank, 0.0), axis=1, keepdims=True)
    cols = (i1.astype(f32), i2.astype(f32), rank1, rank2, 1.0 / (1.0 + e2), e2 / (1.0 + e2))
    meta = jnp.zeros(logits.shape, f32)
    for n, col in enumerate(cols):
        meta = jnp.where(lane == n, col, meta)
    meta_ref[...] = meta
    cnt_ref[...] = run_ref[...]


def _router(xt, router, tb=1024):
    T = xt.shape[0]
    tb = min(tb, T)
    return pl.pallas_call(
        _router_body, grid=(T // tb,),
        in_specs=[pl.BlockSpec((tb, D_MODEL), lambda i: (i, 0)), pl.BlockSpec((D_MODEL, LANES), lambda i: (0, 0))],
        out_specs=[pl.BlockSpec((tb, SC_WORDS), lambda i: (i, 0)), pl.BlockSpec((tb, SC_WORDS), lambda i: (i, 0)),
                   pl.BlockSpec((tb, LANES), lambda i: (i, 0)), pl.BlockSpec((8, LANES), lambda i: (0, 0))],
        out_shape=[jax.ShapeDtypeStruct((T, SC_WORDS), jnp.int32), jax.ShapeDtypeStruct((T, SC_WORDS), jnp.int32),
                   jax.ShapeDtypeStruct((T, LANES), f32), jax.ShapeDtypeStruct((8, LANES), f32)],
        scratch_shapes=[pltpu.VMEM((8, LANES), f32)],
        compiler_params=_cparams(("arbitrary",)), name="moe_router")(xt, router)


def _sc_gather(table, idx):
    n = idx.shape[0]
    mesh = plsc.VectorSubcoreMesh(core_axis_name="c", subcore_axis_name="s")

    @functools.partial(pl.kernel, out_type=jax.ShapeDtypeStruct((n, SC_WORDS), table.dtype), mesh=mesh)
    def gather(x_hbm, i_hbm, o_hbm):
        def body(i_vmem, o_vmem):
            pltpu.sync_copy(x_hbm.at[i_vmem.at[0]], o_vmem)

        pltpu.emit_pipeline(
            body, grid=(n // SC_WINDOW,),
            in_specs=[pl.BlockSpec((1, SC_WINDOW), index_map=lambda i: (0, i))],
            out_specs=[pl.BlockSpec((SC_WINDOW, SC_WORDS), index_map=lambda i: (i, 0))],
            core_axis_name="s", dimension_semantics=(pltpu.PARALLEL,))(i_hbm, o_hbm)

    return gather(table, idx.reshape(1, n))


def _sc_scatter(rows, idx, n_out):
    R = rows.shape[0]
    n = idx.shape[0]
    nblk = R // SC_WINDOW
    mesh = plsc.VectorSubcoreMesh(core_axis_name="c", subcore_axis_name="s")

    @functools.partial(pl.kernel, out_type=jax.ShapeDtypeStruct((n_out, SC_WORDS), rows.dtype), mesh=mesh,
                       scratch_types=[])
    def scatter(x_hbm, i_hbm, o_hbm):
        def body(x_vmem, i_vmem):
            pltpu.sync_copy(x_vmem, o_hbm.at[i_vmem.at[0]])

        pltpu.emit_pipeline(
            body, grid=(n // SC_WINDOW,),
            in_specs=[pl.BlockSpec((SC_WINDOW, SC_WORDS), index_map=lambda i: (i % nblk, 0)),
                      pl.BlockSpec((1, SC_WINDOW), index_map=lambda i: (0, i))],
            out_specs=[], core_axis_name="s", dimension_semantics=(pltpu.PARALLEL,))(x_hbm, i_hbm)

    return scatter(rows, idx.reshape(1, n))


def _experts_body(te_ref, nt_ref, xa_ref, xb_ref, w1_ref, w3_ref, w2_ref, oa_ref, ob_ref, acc_ref):
    i = pl.program_id(0)
    j = pl.program_id(1)

    @pl.when(i < nt_ref[0])
    def _():
        x = _unpack_words(xa_ref[...], xb_ref[...]).astype(bf16)
        h1 = jnp.dot(x, w1_ref[0], preferred_element_type=f32)
        h3 = jnp.dot(x, w3_ref[0], preferred_element_type=f32)
        part = _bdot(h1 * _sigmoid(h1) * h3, w2_ref[0])

        @pl.when(j == 0)
        def _():
            acc_ref[...] = part

        @pl.when(j > 0)
        def _():
            acc_ref[...] += part

    @pl.when(j == pl.num_programs(1) - 1)
    def _():
        oa_ref[...], ob_ref[...] = _pack_words(jnp.where(i < nt_ref[0], acc_ref[...], 0.0))


def _experts(xa, xb, tile_expert, n_tiles, w1, w3, w2, tf=512):
    P = xa.shape[0]
    dff = w1.shape[2]
    words = pl.BlockSpec((MOE_TILE, SC_WORDS), lambda i, j, te, nt: (i, 0))
    grid_spec = pltpu.PrefetchScalarGridSpec(
        num_scalar_prefetch=2, grid=(P // MOE_TILE, dff // tf),
        in_specs=[words, words,
                  pl.BlockSpec((1, D_MODEL, tf), lambda i, j, te, nt: (te[i], 0, j)),
                  pl.BlockSpec((1, D_MODEL, tf), lambda i, j, te, nt: (te[i], 0, j)),
                  pl.BlockSpec((1, tf, D_MODEL), lambda i, j, te, nt: (te[i], j, 0))],
        out_specs=[words, words],
        scratch_shapes=[pltpu.VMEM((MOE_TILE, D_MODEL), f32)])
    return pl.pallas_call(
        _experts_body, grid_spec=grid_spec, out_shape=[jax.ShapeDtypeStruct((P, SC_WORDS), jnp.int32)] * 2,
        compiler_params=_cparams(("parallel", "arbitrary")), name="moe_experts")(tile_expert, n_tiles, xa, xb, w1, w3, w2)


def _combine_body(x_ref, y0a_ref, y0b_ref, y1a_ref, y1b_ref, meta_ref, ln_ref, o_ref):
    meta = meta_ref[...]
    ff = (meta[:, 4:5] * _unpack_words(y0a_ref[...], y0b_ref[...])
          + meta[:, 5:6] * _unpack_words(y1a_ref[...], y1b_ref[...]))
    o_ref[...] = _layer_norm(ALPHA * x_ref[...] + ff, ln_ref[0:1, :], ln_ref[1:2, :])


def _combine(xt, yga, ygb, meta, ln, tm=1024):
    T = xt.shape[0]
    tm = min(tm, T)
    nb = T // tm
    row = pl.BlockSpec((tm, D_MODEL), lambda i: (i, 0))
    first = pl.BlockSpec((tm, SC_WORDS), lambda i: (i, 0))
    second = pl.BlockSpec((tm, SC_WORDS), lambda i: (nb + i, 0))
    return pl.pallas_call(
        _combine_body, grid=(nb,),
        in_specs=[row, first, first, second, second, pl.BlockSpec((tm, LANES), lambda i: (i, 0)),
                  pl.BlockSpec((2, D_MODEL), lambda i: (0, 0))],
        out_specs=row, out_shape=jax.ShapeDtypeStruct((T, D_MODEL), f32),
        compiler_params=_cparams(("parallel",)), name="moe_combine")(xt, yga, ygb, yga, ygb, meta, ln)


def _moe(xt, router, w1, w3, w2, ln):
    T = xt.shape[0]
    xa, xb, meta, cnt = _router(xt, router)
    counts = cnt[0, :N_EXPERTS].astype(jnp.int32)
    tiles = (counts + MOE_TILE - 1) // MOE_TILE
    tile_end = jnp.cumsum(tiles)
    offset = (tile_end - tiles) * MOE_TILE
    expert = meta[:, 0:2].astype(jnp.int32)
    onehot = expert[:, :, None] == jnp.arange(N_EXPERTS, dtype=jnp.int32)[None, None, :]
    pos = jnp.sum(jnp.where(onehot, offset[None, None, :], 0), axis=2) + meta[:, 2:4].astype(jnp.int32)
    pos = pos.T.reshape(-1)
    P = 2 * T + N_EXPERTS * MOE_TILE
    tile_id = jnp.arange(P // MOE_TILE, dtype=jnp.int32)
    tile_expert = jnp.minimum(jnp.sum((tile_id[:, None] >= tile_end[None, :]).astype(jnp.int32), axis=1), N_EXPERTS - 1)
    ya, yb = _experts(_sc_scatter(xa, pos, P), _sc_scatter(xb, pos, P), tile_expert, tile_end[-1:], w1, w3, w2)
    return _combine(xt, _sc_gather(ya, pos), _sc_gather(yb, pos), meta, ln)


def kernel(x, w_in, b_in, att_gq, att_gk, rw_mix, rw_w0, rw_w2, rw_a0, rw_a2, rw_g2, rw_kk, rw_ka, rw_rk, rw_ln_g, rw_ln_b, s5_lam_re, s5_lam_im, s5_log_dt, s5_b_re, s5_b_im, s5_c_re, s5_c_im, s5_d, s5_glu_w, s5_glu_b, ml_conv_w, ml_conv_b, ml_ib, ml_fb, w_gate, b_gate, w_branch, w_out, ln1_g, ln1_b, ffn_w1, ffn_w3, ffn_w2, moe_router, moe_w1, moe_w3, moe_w2, ln2_g, ln2_b):
    B, S, D = x.shape
    xt = x.reshape(B * S, D)
    cos, sin = _rope_tables(S)
    for l in range(DEPTH):
        att, s5u, mqk, mv, mo, rw, gt = _proj(xt, *_proj_params(w_in[l], b_in[l], ml_ib[l], ml_fb[l]))
        gain = jnp.concatenate([jnp.tile(att_gq[l], 8) * (HEAD_DIM ** -0.5), jnp.tile(att_gk[l], 2)])[None, :]
        q, k, v = _att_prep(att, cos, sin, gain, B, S)
        score_bound = 8.1 * jnp.max(jnp.abs(att_gq[l])) * jnp.max(jnp.abs(att_gk[l]))
        o_att = _flash(q, k, v, score_bound, B, S)
        o_rw = _rwkv(rw, _rw_params(rw_mix[l], rw_w0[l], rw_w2[l], rw_a0[l], rw_a2[l], rw_g2[l], rw_kk[l], rw_ka[l],
                                    rw_rk[l], rw_ln_g[l], rw_ln_b[l]), B, S)
        o_s5 = _s5(s5u, _s5_params(s5_lam_re[l], s5_lam_im[l], s5_log_dt[l], s5_b_re[l], s5_b_im[l], s5_c_re[l],
                                   s5_c_im[l]),
                   (s5_d[l][None, :], s5_glu_w[l].astype(bf16), s5_glu_b[l][None, :]), B, S)
        hf, hb = _mlstm(mqk, mv, gt, ml_conv_w[l], ml_conv_b[l][None, :], B, S)
        xt = _merge(xt, o_att, o_rw, o_s5, hf, hb, mo, w_gate[l].astype(bf16), b_gate[l],
                    _att_branch_weight(w_branch[l, 0]).astype(bf16), w_branch[l, 1:].astype(bf16),
                    w_out[l].astype(bf16), jnp.stack([ln1_g[l], ln1_b[l]]))
        ln2 = jnp.stack([ln2_g[l], ln2_b[l]])
        if l % 2 == 0:
            xt = _ffn(xt, ffn_w1[l // 2].astype(bf16), ffn_w3[l // 2].astype(bf16), ffn_w2[l // 2].astype(bf16), ln2)
        else:
            router = jnp.pad(moe_router[l // 2], ((0, 0), (0, LANES - N_EXPERTS)))
            xt = _moe(xt, router, moe_w1[l // 2].astype(bf16), moe_w3[l // 2].astype(bf16),
                      moe_w2[l // 2].astype(bf16), ln2)
    return xt.reshape(B, S, D)
```

```python
import functools
import math

import jax
import jax.numpy as jnp
import numpy as np
from jax import lax
from jax.experimental import pallas as pl
from jax.experimental.pallas import tpu as pltpu
from jax.experimental.pallas import tpu_sc as plsc

f32 = jnp.float32
bf16 = jnp.bfloat16

D_MODEL = 1024
DEPTH = 2
GRID_W = 64
BRANCH_W = 256
HEAD_DIM = 64
ATT_HEADS = 4
ATT_KV_HEADS = 2
ROPE_THETA = 10000.0
QK_EPS = 1e-6
RW_GN_EPS = 64e-5
RW_COLS = 1088
S5_GROUP = 16
S5_GROUPS = 16
S5_STATE = 64
ML_HEADS = 4
N_EXPERTS = 8
ALPHA = (2 * DEPTH) ** 0.25
LN_EPS = 1e-5

LANES = 128
CHUNK = 64
NEG = -1e30
VMEM_LIMIT = 56 * 1024 * 1024

PROJ_SPLITS = (768, 256, 512, 256, 256, RW_COLS)


def _cparams(sem):
    return pltpu.CompilerParams(dimension_semantics=sem, vmem_limit_bytes=VMEM_LIMIT)


def _sigmoid(x):
    return 1.0 / (1.0 + jnp.exp(-x))


def _softplus(x):
    return jnp.maximum(x, 0.0) + jnp.log(1.0 + jnp.exp(-jnp.abs(x)))


def _dims(a, lhs_c, rhs_c):
    lead = a.ndim - 2
    batch = tuple(range(lead))
    return (((lhs_c + lead,), (rhs_c + lead,)), (batch, batch))


def _bdot(a, b):
    return lax.dot_general(a.astype(bf16), b.astype(bf16), _dims(a, 1, 0), preferred_element_type=f32)


def _bdot_nt(a, b):
    return lax.dot_general(a.astype(bf16), b.astype(bf16), _dims(a, 1, 1), preferred_element_type=f32)


def _bdot_tn(a, b):
    return lax.dot_general(a.astype(bf16), b.astype(bf16), _dims(a, 0, 0), preferred_element_type=f32)


def _split(x):
    hi = x.astype(bf16)
    return hi, (x - hi.astype(f32)).astype(bf16)


def _sdot(a, b, exact):
    dims = _dims(a, 1, 0)
    if exact == "rhs":
        hi, lo = _split(a)
        bb = b.astype(bf16)
        return (lax.dot_general(hi, bb, dims, preferred_element_type=f32)
                + lax.dot_general(lo, bb, dims, preferred_element_type=f32))
    hi, lo = _split(b)
    ab = a.astype(bf16)
    return (lax.dot_general(ab, hi, dims, preferred_element_type=f32)
            + lax.dot_general(ab, lo, dims, preferred_element_type=f32))


def _sdot3(a, b):
    dims = _dims(a, 1, 0)
    ah, al = _split(a)
    bh, bl = _split(b)
    return (lax.dot_general(ah, bh, dims, preferred_element_type=f32)
            + lax.dot_general(ah, bl, dims, preferred_element_type=f32)
            + lax.dot_general(al, bh, dims, preferred_element_type=f32))


def _seg_matrix(n, seg=HEAD_DIM):
    r = lax.broadcasted_iota(jnp.int32, (n, n), 0) // seg
    c = lax.broadcasted_iota(jnp.int32, (n, n), 1) // seg
    return (r == c).astype(f32)


def _layer_norm(y, g, b):
    mu = jnp.mean(y, axis=-1, keepdims=True)
    d = y - mu
    var = jnp.mean(d * d, axis=-1, keepdims=True)
    return d * lax.rsqrt(var + LN_EPS) * g + b


def _row_to_col(row, eye):
    return jnp.sum(jnp.where(eye, jnp.broadcast_to(row, eye.shape), 0.0), axis=2, keepdims=True)


def _stack_heads(x):
    h0 = lax.broadcasted_iota(jnp.int32, x.shape, 2) < HEAD_DIM
    return jnp.concatenate([jnp.where(h0, x, 0.0), jnp.where(h0, 0.0, x)], axis=1)


def _proj_body(x_ref, w_ref, b_ref, wg_ref, bg_ref, att_ref, s5_ref, mqk_ref, mv_ref, mo_ref, rw_ref, g_ref):
    xb = x_ref[...].astype(bf16)
    off = 0
    for o_ref, n in zip((att_ref, s5_ref, mqk_ref, mv_ref, mo_ref, rw_ref), PROJ_SPLITS):
        o_ref[...] = jnp.dot(xb, w_ref[:, off:off + n], preferred_element_type=f32) + b_ref[:, off:off + n]
        off += n
    g_ref[...] = lax.dot_general(wg_ref[...], xb, (((1,), (1,)), ((), ())), preferred_element_type=f32) + bg_ref[...]


def _proj(xt, w, b, wg, bg, tm=512):
    T = xt.shape[0]
    n_tot = sum(PROJ_SPLITS)
    outs = [jax.ShapeDtypeStruct((T, n), f32) for n in PROJ_SPLITS] + [jax.ShapeDtypeStruct((16, T), f32)]
    return pl.pallas_call(
        _proj_body, grid=(T // tm,),
        in_specs=[pl.BlockSpec((tm, D_MODEL), lambda i: (i, 0)),
                  pl.BlockSpec((D_MODEL, n_tot), lambda i: (0, 0)),
                  pl.BlockSpec((1, n_tot), lambda i: (0, 0)),
                  pl.BlockSpec((16, D_MODEL), lambda i: (0, 0)),
                  pl.BlockSpec((16, 1), lambda i: (0, 0))],
        out_specs=[pl.BlockSpec((tm, n), lambda i: (i, 0)) for n in PROJ_SPLITS]
        + [pl.BlockSpec((16, tm), lambda i: (0, i))],
        out_shape=outs, compiler_params=_cparams(("parallel",)), name="proj")(xt, w, b, wg, bg)


def _proj_params(w_in, b_in, ml_ib, ml_fb):
    o = np.cumsum((0, 256, 128, 128, RW_COLS, 256, 512, 256, 8, 8, 256))
    sl = lambda i: (w_in[:, o[i]:o[i + 1]], b_in[o[i]:o[i + 1]])
    (wq, bq), (wk, bk), (wv, bv), (wrw, brw), (ws5, bs5), (wqk, bqk), (wmv, bmv), (wi, bi), (wf, bf), (wo, bo) = (
        sl(i) for i in range(10))
    zw, zb = jnp.zeros((D_MODEL, HEAD_DIM), f32), jnp.zeros((HEAD_DIM,), f32)
    wq_e, bq_e = [], []
    for h in range(ATT_HEADS):
        wh, bh = wq[:, 64 * h:64 * h + 64], bq[64 * h:64 * h + 64]
        wq_e += [wh, zw] if h // 2 == 0 else [zw, wh]
        bq_e += [bh, zb] if h // 2 == 0 else [zb, bh]
    w = jnp.concatenate(wq_e + [wk, wv, ws5, wqk, wmv, wo, wrw], axis=1)
    b = jnp.concatenate(bq_e + [bk, bv, bs5, bqk, bmv, bo, brw])
    wg = jnp.concatenate([wi, wf], axis=1).T
    bg = jnp.concatenate([bi + ml_ib.reshape(-1), bf + ml_fb.reshape(-1)])
    return w.astype(bf16), b[None, :], wg.astype(bf16), bg[:, None]


def _att_prep_body(a_ref, cos_ref, sin_ref, gain_ref, q_ref, k_ref, v_ref):
    x = a_ref[:, 0:640]
    ms = _sdot(x * x, _seg_matrix(640), "rhs") * (1.0 / HEAD_DIM)
    xn = x * lax.rsqrt(ms + QK_EPS) * gain_ref[...]
    lane = lax.broadcasted_iota(jnp.int32, xn.shape, 1)
    partner = jnp.where((lane % 32) < 16, pltpu.roll(xn, 640 - 16, 1), pltpu.roll(xn, 16, 1))
    cos = jnp.concatenate([cos_ref[...]] * 5, axis=1)
    sin = jnp.concatenate([sin_ref[...]] * 5, axis=1)
    rot = xn * cos + partner * sin
    q_ref[...] = rot[:, 0:512].astype(bf16)
    k_ref[...] = rot[:, 512:640].astype(bf16)
    v_ref[:, 0:LANES] = a_ref[:, 640:768].astype(bf16)
    v_ref[:, LANES:2 * LANES] = jnp.ones((x.shape[0], LANES), bf16)


def _att_prep(att, cos, sin, gain, B, S, tq=512):
    T = B * S
    nb = S // tq
    return pl.pallas_call(
        _att_prep_body, grid=(B, nb),
        in_specs=[pl.BlockSpec((tq, 768), lambda b, i: (b * nb + i, 0)),
                  pl.BlockSpec((tq, LANES), lambda b, i: (i, 0)),
                  pl.BlockSpec((tq, LANES), lambda b, i: (i, 0)),
                  pl.BlockSpec((1, 640), lambda b, i: (0, 0))],
        out_specs=[pl.BlockSpec((tq, 512), lambda b, i: (b * nb + i, 0)),
                   pl.BlockSpec((tq, LANES), lambda b, i: (b * nb + i, 0)),
                   pl.BlockSpec((tq, 2 * LANES), lambda b, i: (b * nb + i, 0))],
        out_shape=[jax.ShapeDtypeStruct((T, 512), bf16), jax.ShapeDtypeStruct((T, LANES), bf16),
                   jax.ShapeDtypeStruct((T, 2 * LANES), bf16)],
        compiler_params=_cparams(("parallel", "parallel")), name="att_prep")(att, cos, sin, gain)


def _rope_tables(S):
    t = np.arange(S)
    row = (t // GRID_W).astype(np.float32)
    col = (t % GRID_W).astype(np.float32)
    n = 16
    inv = np.power(np.float32(ROPE_THETA), -np.arange(n, dtype=np.float32) / n).astype(np.float32)
    ar = jnp.asarray(row)[:, None] * jnp.asarray(inv)
    ac = jnp.asarray(col)[:, None] * jnp.asarray(inv)
    cos = jnp.concatenate([jnp.cos(ar), jnp.cos(ar), jnp.cos(ac), jnp.cos(ac)], axis=1)
    sin = jnp.concatenate([-jnp.sin(ar), jnp.sin(ar), -jnp.sin(ac), jnp.sin(ac)], axis=1)
    return jnp.concatenate([cos, cos], axis=1), jnp.concatenate([sin, sin], axis=1)


def _flash_body(q_ref, k_ref, v_ref, o_ref, acc_ref, *m_scratch, tk, track_max):
    tq = q_ref.shape[0]
    nk = k_ref.shape[0] // tk
    q2 = jnp.concatenate([q_ref[:, 0:LANES], q_ref[:, LANES:2 * LANES]], axis=0)
    acc_ref[...] = jnp.zeros(acc_ref.shape, f32)
    if track_max:
        m_ref, = m_scratch
        m_ref[...] = jnp.full(m_ref.shape, NEG, f32)

    def step(j, carry):
        rows = pl.ds(pl.multiple_of(j * tk, tk), tk)
        s = lax.dot_general(q2, k_ref[rows, :], (((1,), (1,)), ((), ())), preferred_element_type=f32)
        if track_max:
            m_old = m_ref[...]
            m_new = jnp.maximum(m_old, jnp.max(s, axis=1, keepdims=True))
            p = jnp.exp(s - m_new).astype(bf16)
            acc_ref[...] = jnp.exp(m_old - m_new) * acc_ref[...] + jnp.dot(p, v_ref[rows, :], preferred_element_type=f32)
            m_ref[...] = m_new
        else:
            acc_ref[...] += jnp.dot(jnp.exp(s).astype(bf16), v_ref[rows, :], preferred_element_type=f32)
        return carry

    lax.fori_loop(0, nk, step, 0)
    o = acc_ref[:, 0:LANES] / acc_ref[:, LANES:2 * LANES]
    o_ref[...] = jnp.concatenate([o[0:tq], o[tq:2 * tq]], axis=1)


SCORE_BOUND_MAX = 60.0


def _flash(q, k, v, score_bound, B, S, tq=256, tk=8192):
    T = B * S
    nb = S // tq
    tk = min(tk, S)

    def call(track_max):
        scratch = [pltpu.VMEM((2 * tq, 2 * LANES), f32)] + ([pltpu.VMEM((2 * tq, 1), f32)] if track_max else [])
        return pl.pallas_call(
            functools.partial(_flash_body, tk=tk, track_max=track_max), grid=(B, ATT_KV_HEADS, nb),
            in_specs=[pl.BlockSpec((tq, 2 * LANES), lambda b, g, i: (b * nb + i, g)),
                      pl.BlockSpec((S, LANES), lambda b, g, i: (b, 0)),
                      pl.BlockSpec((S, 2 * LANES), lambda b, g, i: (b, 0))],
            out_specs=pl.BlockSpec((tq, 2 * LANES), lambda b, g, i: (b * nb + i, g)),
            out_shape=jax.ShapeDtypeStruct((T, 512), f32), scratch_shapes=scratch,
            compiler_params=_cparams(("parallel", "parallel", "parallel")),
            name="flash_safe" if track_max else "flash")(q, k, v)

    return lax.cond(score_bound <= SCORE_BOUND_MAX, lambda: call(False), lambda: call(True))


def _halo_specs(width, tm, B, S):
    nb = S // tm
    r8 = tm // 8
    last8 = B * S // 8 - 1

    def main(b, i):
        return (b * nb + i, 0)

    def prev(b, i):
        return (jnp.maximum(b * (S // 8) + i * r8 - 1, 0), 0)

    def nxt(b, i):
        return (jnp.minimum(b * (S // 8) + (i + 1) * r8, last8), 0)

    return [pl.BlockSpec((tm, width), main), pl.BlockSpec((8, width), prev), pl.BlockSpec((8, width), nxt)]


def _fill_halo(buf_ref, x_ref, p_ref, n_ref):
    tm = x_ref.shape[0]
    i = pl.program_id(1)
    last = pl.num_programs(1) - 1
    buf_ref[pl.ds(8, tm), :] = x_ref[...]
    buf_ref[pl.ds(0, 8), :] = jnp.where(i > 0, p_ref[...], 0.0)
    buf_ref[pl.ds(8 + tm, 8), :] = jnp.where(i < last, n_ref[...], 0.0)


def _rw_prep_body(x_ref, p_ref, n_ref, mix_ref, w2_ref, a2_ref, g2_ref, vec_ref,
                  r_ref, k_ref, v_ref, an_ref, bn_ref, lw_ref, gate_ref, bonus_ref, buf_ref):
    tm = x_ref.shape[0]
    _fill_halo(buf_ref, x_ref, p_ref, n_ref)
    x = x_ref[...]
    p = x + mix_ref[0:1, :] * (buf_ref[pl.ds(7, tm), :] - x) + mix_ref[1:2, :] * (buf_ref[pl.ds(9, tm), :] - x)
    r, k, v = p[:, 0:256], p[:, 256:512], p[:, 512:768]
    w0f, w0b, a0, k_k, k_a, r_k = (vec_ref[j:j + 1, :] for j in range(6))
    dec = _bdot(jnp.tanh(p[:, 768:896]), w2_ref[...])
    z = p[:, 896:1088]
    a = _sigmoid(a0 + _bdot(z, a2_ref[...]))
    gate_ref[...] = _bdot(_sigmoid(z), g2_ref[...])
    seg = _seg_matrix(BRANCH_W)
    kk = k * k_k
    kk = kk / jnp.maximum(jnp.sqrt(_sdot(kk * kk, seg, "rhs")), 1e-12)
    k2 = k * (1.0 + (a - 1.0) * k_a)
    bonus_ref[...] = _sdot(r * k2 * r_k, seg, "rhs") * v
    r_ref[...] = r
    k_ref[...] = k2
    v_ref[...] = v
    an_ref[...] = -kk
    bn_ref[...] = kk * a
    lw_ref[0] = -jnp.exp(-_softplus(-(w0f + dec[:, 0:256])) - 0.5)
    lw_ref[1] = -jnp.exp(-_softplus(-(w0b + dec[:, 256:512])) - 0.5)


def _rw_prep(rw, mix, w2, a2, g2, vec, B, S, tm=512):
    T = B * S
    nb = S // tm
    o256 = pl.BlockSpec((tm, BRANCH_W), lambda b, i: (b * nb + i, 0))
    full = lambda shape: pl.BlockSpec(shape, lambda b, i: (0,) * len(shape))
    return pl.pallas_call(
        _rw_prep_body, grid=(B, nb),
        in_specs=_halo_specs(RW_COLS, tm, B, S) + [full((2, RW_COLS)), full((128, 512)), full((192, 256)),
                                                   full((192, 256)), full((8, 256))],
        out_specs=[o256] * 5 + [pl.BlockSpec((2, tm, BRANCH_W), lambda b, i: (0, b * nb + i, 0)), o256, o256],
        out_shape=[jax.ShapeDtypeStruct((T, BRANCH_W), f32)] * 5 + [jax.ShapeDtypeStruct((2, T, BRANCH_W), f32)]
        + [jax.ShapeDtypeStruct((T, BRANCH_W), f32)] * 2,
        scratch_shapes=[pltpu.VMEM((tm + 16, RW_COLS), f32)],
        compiler_params=_cparams(("parallel", "parallel")), name="rw_prep")(rw, rw, rw, mix, w2, a2, g2, vec)


def _pair_masks(rev, nbatch):
    n = 2 * CHUNK
    r = lax.broadcasted_iota(jnp.int32, (nbatch, n, n), 1)
    c = lax.broadcasted_iota(jnp.int32, (nbatch, n, n), 2)
    same = (r // CHUNK) == (c // CHUNK)
    if rev:
        return r, c, same & (c > r), same & (c >= r)
    return r, c, same & (c < r), same & (c <= r)


def _rw_chunk(st, r, k, v, an, bn, lw, rev):
    L = CHUNK
    N = r.shape[0]
    ri = lax.broadcasted_iota(jnp.int32, (N, L, L), 1)
    ci = lax.broadcasted_iota(jnp.int32, (N, L, L), 2)
    tri = ((ci >= ri) if rev else (ci <= ri)).astype(f32)
    cs = _sdot(tri, lw, "lhs")
    tot = jnp.sum(lw, axis=1, keepdims=True)
    e_neg = jnp.exp(-cs)
    a2 = _stack_heads(an * jnp.exp(cs - lw))
    r2 = _stack_heads(r * jnp.exp(cs))
    b2 = _stack_heads(bn * e_neg)
    k2 = _stack_heads(k * e_neg)
    v2 = _stack_heads(v)
    rr, cc, strict, incl = _pair_masks(rev, N)
    n2 = 2 * L
    g = _bdot_nt(jnp.concatenate([a2, r2], axis=1), jnp.concatenate([b2, k2], axis=1))
    mab = jnp.where(strict, g[:, 0:n2, 0:n2], 0.0)
    mak = jnp.where(strict, g[:, 0:n2, n2:2 * n2], 0.0)
    pb = jnp.where(incl, g[:, n2:2 * n2, 0:n2], 0.0)
    pk = jnp.where(incl, g[:, n2:2 * n2, n2:2 * n2], 0.0)
    eye = rr == cc
    m8 = jnp.where((rr // 8) == (cc // 8), mab, 0.0)
    x = eye.astype(f32) + m8
    p = _bdot(m8, m8)
    x = x + _bdot(x, p)
    p = _bdot(p, p)
    x = x + _bdot(x, p)
    n = 8
    while n < L:
        e = jnp.where(((rr // (2 * n)) == (cc // (2 * n))) & ((rr // n) != (cc // n)), mab, 0.0)
        x = x + _bdot(_bdot(x, e), x)
        n *= 2
    wu = _bdot(x, jnp.concatenate([a2, _bdot(mak, v2)], axis=2))
    pwu = _bdot(pb, wu)
    rh = r2 + pwu[:, :, 0:LANES]
    y2 = pwu[:, :, LANES:2 * LANES] + _bdot(jnp.concatenate([pk, rh], axis=2), jnp.concatenate([v2, st], axis=1))
    y = y2[:, 0:L] + y2[:, L:2 * L]
    gam = _row_to_col(jnp.exp(tot), eye)
    bwu = _bdot_tn(b2, wu)
    st = gam * (st + _bdot(bwu[:, :, 0:LANES], st) + bwu[:, :, LANES:2 * LANES] + _bdot_tn(k2, v2))
    return y, st


def _load_pairs(ref, rows):
    return jnp.concatenate([ref[:, rows, 0:LANES], ref[:, rows, LANES:2 * LANES]], axis=0)


def _store_pairs(ref, rows, y):
    nb = ref.shape[0]
    ref[:, rows, 0:LANES] = y[0:nb]
    ref[:, rows, LANES:2 * LANES] = y[nb:2 * nb]


def _rw_scan_body(r_ref, k_ref, v_ref, an_ref, bn_ref, lw_ref, y_ref, st_ref, *, rev):
    nch = r_ref.shape[1] // CHUNK

    @pl.when(pl.program_id(0) == 0)
    def _():
        st_ref[...] = jnp.zeros(st_ref.shape, f32)

    def step(cc, carry):
        c = (nch - 1 - cc) if rev else cc
        rows = pl.ds(pl.multiple_of(c * CHUNK, CHUNK), CHUNK)
        y, st = _rw_chunk(st_ref[...], *(_load_pairs(ref, rows) for ref in (r_ref, k_ref, v_ref, an_ref, bn_ref)),
                          _load_pairs(lw_ref.at[0], rows), rev)
        _store_pairs(y_ref, rows, y)
        st_ref[...] = st
        return carry

    lax.fori_loop(0, nch, step, 0)


def _rw_scan(r, k, v, an, bn, lw, B, S, rev, ts=256):
    nb = S // ts
    d = 1 if rev else 0
    blk = (lambda i: nb - 1 - i) if rev else (lambda i: i)
    io = pl.BlockSpec((B, ts, BRANCH_W), lambda i: (0, blk(i), 0))
    return pl.pallas_call(
        functools.partial(_rw_scan_body, rev=rev), grid=(nb,),
        in_specs=[io] * 5 + [pl.BlockSpec((1, B, ts, BRANCH_W), lambda i: (d, 0, blk(i), 0))],
        out_specs=io, out_shape=jax.ShapeDtypeStruct((B, S, BRANCH_W), f32),
        scratch_shapes=[pltpu.VMEM((B * BRANCH_W // LANES, LANES, LANES), f32)],
        compiler_params=_cparams(("arbitrary",)),
        name="rw_scan_bwd" if rev else "rw_scan_fwd")(r, k, v, an, bn, lw)


def _rw_finish(y, gate, bonus, gn):
    seg = _seg_matrix(BRANCH_W)
    mu = _sdot(y, seg, "rhs") * (1.0 / HEAD_DIM)
    d = y - mu
    var = _sdot(d * d, seg, "rhs") * (1.0 / HEAD_DIM)
    yn = d * lax.rsqrt(var + RW_GN_EPS) * gn[0:1, :] + gn[1:2, :]
    return (yn + bonus) * gate


def _rwkv(rw, prm, B, S):
    mix, w2, a2, g2, vec = prm
    r, k, v, an, bn, lw, gate, bonus = _rw_prep(rw, mix, w2, a2, g2, vec, B, S)
    seq = [t.reshape(B, S, BRANCH_W) for t in (r, k, v, an, bn)] + [lw.reshape(2, B, S, BRANCH_W)]
    yf = _rw_scan(*seq, B, S, rev=False).reshape(B * S, BRANCH_W)
    yb = _rw_scan(*seq, B, S, rev=True).reshape(B * S, BRANCH_W)
    return yf, yb, gate, bonus


def _rw_params(mix, w0, w2, a0, a2, g2, k_k, k_a, r_k):
    z = jnp.zeros((64, 256), f32)
    w2c = jnp.concatenate([jnp.concatenate([w2[0], z], axis=1), jnp.concatenate([z, w2[1]], axis=1)], axis=0)
    a2p = jnp.concatenate([a2, jnp.zeros((128, 256), f32)], axis=0)
    g2p = jnp.concatenate([jnp.zeros((64, 256), f32), g2], axis=0)
    vec = jnp.stack([w0[0], w0[1], a0, k_k, k_a, r_k.reshape(-1), jnp.zeros_like(a0), jnp.zeros_like(a0)])
    return mix, w2c.astype(bf16), a2p.astype(bf16), g2p.astype(bf16), vec


def _s5_scan_body(u_ref, bcat_ref, cre_ref, cim_ref, lre_ref, lim_ref, y_ref, sre_ref, sim_ref, bre_ref, bim_ref):
    ts = u_ref.shape[0]
    n = S5_GROUPS * S5_STATE

    @pl.when(pl.program_id(0) == 0)
    def _():
        sre_ref[...] = jnp.zeros(sre_ref.shape, f32)
        sim_ref[...] = jnp.zeros(sim_ref.shape, f32)

    u = u_ref[...].reshape(ts * 8, BRANCH_W).astype(bf16)
    fwd = lax.broadcasted_iota(jnp.int32, (ts, 8, n), 1) < 4

    def bu(part):
        return jnp.dot(u, bcat_ref[:, part * n:(part + 1) * n], preferred_element_type=f32).reshape(ts, 8, n)

    bre_ref[...] = jnp.where(fwd, bu(0), bu(2))
    bim_ref[...] = jnp.where(fwd, bu(1), bu(3))
    lre = lre_ref[...]
    lim = lim_ref[...]

    def step(t, carry):
        sre, sim = carry
        nre = lre * sre - lim * sim + bre_ref[t]
        nim = lre * sim + lim * sre + bim_ref[t]
        bre_ref[t] = nre
        bim_ref[t] = nim
        return nre, nim

    sre, sim = lax.fori_loop(0, ts, step, (sre_ref[...], sim_ref[...]), unroll=4)
    sre_ref[...] = sre
    sim_ref[...] = sim
    y = (_bdot(bre_ref[...].reshape(ts * 8, n), cre_ref[...])
         - _bdot(bim_ref[...].reshape(ts * 8, n), cim_ref[...])).reshape(ts, 8, 2 * BRANCH_W)
    fwd_o = lax.broadcasted_iota(jnp.int32, (ts, 8, BRANCH_W), 1) < 4
    y_ref[...] = jnp.where(fwd_o, y[:, :, 0:BRANCH_W], y[:, :, BRANCH_W:2 * BRANCH_W])


def _s5_scan(u2, bcat, cre, cim, lre, lim, ts=128):
    S = u2.shape[0]
    n = S5_GROUPS * S5_STATE
    full = lambda shape: pl.BlockSpec(shape, lambda i: (0,) * len(shape))
    return pl.pallas_call(
        _s5_scan_body, grid=(S // ts,),
        in_specs=[pl.BlockSpec((ts, 8, BRANCH_W), lambda i: (i, 0, 0)), full((BRANCH_W, 4 * n)),
                  full((n, 2 * BRANCH_W)), full((n, 2 * BRANCH_W)), full((8, n)), full((8, n))],
        out_specs=pl.BlockSpec((ts, 8, BRANCH_W), lambda i: (i, 0, 0)),
        out_shape=jax.ShapeDtypeStruct((S, 8, BRANCH_W), f32),
        scratch_shapes=[pltpu.VMEM((8, n), f32), pltpu.VMEM((8, n), f32),
                        pltpu.VMEM((ts, 8, n), f32), pltpu.VMEM((ts, 8, n), f32)],
        compiler_params=_cparams(("arbitrary",)), name="s5_scan")(u2, bcat, cre, cim, lre, lim)


def _s5_finish(y, u, vec, w):
    y = y + u * vec[0:1, :]
    y = 0.5 * y * (1.0 + jnp.tanh(math.sqrt(2.0 / math.pi) * (y + 0.044715 * (y * y * y))))
    return y * _sigmoid(_bdot(y, w) + vec[1:2, :])


def _s5_params(lam_re, lam_im, log_dt, b_re, b_im, c_re, c_im):
    G, P, C = S5_GROUPS, S5_STATE, S5_GROUP
    eye = jnp.eye(G, dtype=f32)
    b_c = lax.complex(b_re, b_im)
    bcat, cre, cim, lre, lim = [], [], [], [], []
    for d in range(2):
        lam = lax.complex(jnp.minimum(lam_re[d], -1e-4), lam_im[d])
        lam_bar = jnp.exp(lam * jnp.exp(log_dt[d])[:, None])
        b_bar = ((lam_bar - 1.0) / lam)[..., None] * b_c
        for part in (jnp.real(b_bar), jnp.imag(b_bar)):
            bcat.append(jnp.einsum('gh,gpc->gchp', eye, part).reshape(G * C, G * P))
        cre.append(jnp.einsum('gh,gcp->gphc', eye, c_re[d]).reshape(G * P, G * C))
        cim.append(jnp.einsum('gh,gcp->gphc', eye, c_im[d]).reshape(G * P, G * C))
        lre.append(jnp.broadcast_to(jnp.real(lam_bar).reshape(1, G * P), (4, G * P)))
        lim.append(jnp.broadcast_to(jnp.imag(lam_bar).reshape(1, G * P), (4, G * P)))
    return (jnp.concatenate(bcat, axis=1).astype(bf16), jnp.concatenate(cre, axis=1).astype(bf16),
            jnp.concatenate(cim, axis=1).astype(bf16), jnp.concatenate(lre, axis=0), jnp.concatenate(lim, axis=0))


def _s5(u, prm, B, S):
    u3 = u.reshape(B, S, BRANCH_W).transpose(1, 0, 2)
    u2 = jnp.concatenate([u3, u3[::-1]], axis=1)
    y2 = _s5_scan(u2, *prm)
    return (y2[:, 0:B] + y2[::-1, B:2 * B]).transpose(1, 0, 2).reshape(B * S, BRANCH_W)


def _ml_prep_body(x_ref, p_ref, n_ref, w_ref, b_ref, q_ref, k_ref, buf_ref):
    tm = x_ref.shape[0]
    _fill_halo(buf_ref, x_ref, p_ref, n_ref)
    y = b_ref[...] + w_ref[2:3, :] * x_ref[...]
    for j in (0, 1, 3, 4):
        y = y + w_ref[j:j + 1, :] * buf_ref[pl.ds(6 + j, tm), :]
    y = y * _sigmoid(y)
    q_ref[...] = y[:, 0:BRANCH_W]
    k_ref[...] = y[:, BRANCH_W:2 * BRANCH_W] * (HEAD_DIM ** -0.5)


def _ml_prep(mqk, w, b, B, S, tm=512):
    T = B * S
    nb = S // tm
    o = pl.BlockSpec((tm, BRANCH_W), lambda b_, i: (b_ * nb + i, 0))
    return pl.pallas_call(
        _ml_prep_body, grid=(B, nb),
        in_specs=_halo_specs(512, tm, B, S) + [pl.BlockSpec((5, 512), lambda b_, i: (0, 0)),
                                               pl.BlockSpec((1, 512), lambda b_, i: (0, 0))],
        out_specs=[o, o], out_shape=[jax.ShapeDtypeStruct((T, BRANCH_W), f32)] * 2,
        scratch_shapes=[pltpu.VMEM((tm + 16, 512), f32)],
        compiler_params=_cparams(("parallel", "parallel")), name="ml_prep")(mqk, mqk, mqk, w, b)


def _ml_chunk(state, q, k, v, li, lfp, rev):
    cn, m_row = state
    L = CHUNK
    N = q.shape[0]
    rr, cc, _, incl = _pair_masks(rev, N)
    same = (rr // L) == (cc // L)
    eye = rr == cc
    lane = lax.broadcasted_iota(jnp.int32, (N, 1, 2 * L), 2)
    lf = jnp.minimum(lfp, 0.0) - jnp.log(1.0 + jnp.exp(-jnp.abs(lfp)))
    lf8 = jnp.broadcast_to(lf, (N, 8, 2 * L))
    cum = (same & ((rr >= cc) if rev else (rr <= cc))).astype(f32)
    b_row = _sdot(lf8, cum, "rhs")[:, 0:1]
    g_row = _sdot(lf8, same.astype(f32), "rhs")[:, 0:1]
    w_end = g_row - b_row + li
    m0 = jnp.max(jnp.where(lane < L, w_end, NEG), axis=2, keepdims=True)
    m1 = jnp.max(jnp.where(lane < L, NEG, w_end), axis=2, keepdims=True)
    m_loc = jnp.where(lane < L, m0, m1)
    e_col = _row_to_col(jnp.exp(w_end - m_loc), eye)
    b_col = _row_to_col(b_row, eye)
    q2, k2, v2 = _stack_heads(q), _stack_heads(k), _stack_heads(v)
    v1 = jnp.concatenate([v2, _stack_heads(jnp.ones_like(v))], axis=2)
    log_inter = b_col + _row_to_col(m_row, eye)
    log_intra = jnp.where(incl, b_col - b_row + li, NEG)
    m_r = jnp.maximum(log_inter, jnp.max(log_intra, axis=2, keepdims=True))
    s = _bdot_nt(q2, k2) * jnp.exp(log_intra - m_r)
    inter = jnp.exp(log_inter - m_r)
    nd = _bdot(s, v1) + inter * _bdot(q2, cn)
    h2 = nd[:, :, 0:LANES] / jnp.maximum(jnp.abs(nd[:, :, LANES:2 * LANES]), jnp.exp(-m_r))
    h = h2[:, 0:L] + h2[:, L:2 * L]
    m_new = jnp.maximum(g_row + m_row, m_loc)
    a = jnp.exp(g_row + m_row - m_new)
    bb = jnp.exp(m_loc - m_new)
    cn = jnp.concatenate([a, a], axis=2) * cn + jnp.concatenate([bb, bb], axis=2) * _bdot_tn(e_col * k2, v1)
    return h, (cn, m_new)


def _ml_scan_body(q_ref, k_ref, v_ref, g_ref, h_ref, c_ref, m_ref, *, rev):
    nch = q_ref.shape[1] // CHUNK

    @pl.when(pl.program_id(0) == 0)
    def _():
        c_ref[...] = jnp.zeros(c_ref.shape, f32)
        m_ref[...] = jnp.zeros(m_ref.shape, f32)

    def step(cc, carry):
        c = (nch - 1 - cc) if rev else cc
        rows = pl.ds(pl.multiple_of(c * CHUNK, CHUNK), CHUNK)
        li, lfp = (jnp.concatenate([g_ref[t, 0, 0, :, pl.ds(c, 1), :], g_ref[t, 0, 1, :, pl.ds(c, 1), :]], axis=0)
                   for t in range(2))
        h, (cn, m_row) = _ml_chunk((c_ref[...], m_ref[...]), _load_pairs(q_ref, rows), _load_pairs(k_ref, rows),
                                   _load_pairs(v_ref, rows), li, lfp, rev)
        _store_pairs(h_ref, rows, h)
        c_ref[...] = cn
        m_ref[...] = m_row
        return carry

    lax.fori_loop(0, nch, step, 0)


def _ml_scan(q, k, v, g, B, S, rev, ts=512):
    nb = S // ts
    nch = ts // CHUNK
    d = 1 if rev else 0
    nchain = B * BRANCH_W // LANES
    blk = (lambda i: nb - 1 - i) if rev else (lambda i: i)
    io = pl.BlockSpec((B, ts, BRANCH_W), lambda i: (0, blk(i), 0))
    return pl.pallas_call(
        functools.partial(_ml_scan_body, rev=rev), grid=(nb,),
        in_specs=[io, io, io, pl.BlockSpec((2, 1, 2, B, nch, LANES), lambda i: (0, d, 0, 0, blk(i), 0))],
        out_specs=io, out_shape=jax.ShapeDtypeStruct((B, S, BRANCH_W), f32),
        scratch_shapes=[pltpu.VMEM((nchain, LANES, 2 * LANES), f32), pltpu.VMEM((nchain, 1, LANES), f32)],
        compiler_params=_cparams(("arbitrary",)),
        name="ml_scan_bwd" if rev else "ml_scan_fwd")(q, k, v, g)


def _mlstm(mqk, mv, gt, conv_w, conv_b, B, S):
    q, k = _ml_prep(mqk, conv_w, conv_b, B, S)
    g = gt.reshape(2, 2, 2, 2, B, S // CHUNK, CHUNK).transpose(0, 1, 2, 4, 5, 3, 6).reshape(2, 2, 2, B, S // CHUNK, LANES)
    seq = [t.reshape(B, S, BRANCH_W) for t in (q, k, mv)]
    return (_ml_scan(*seq, g, B, S, rev=False).reshape(B * S, BRANCH_W),
            _ml_scan(*seq, g, B, S, rev=True).reshape(B * S, BRANCH_W))


def _merge_body(x_ref, att_ref, yf_ref, yb_ref, rg_ref, rb_ref, sy_ref, su_ref, hf_ref, hb_ref, mo_ref,
                wg_ref, bg_ref, wba_ref, wb_ref, wo_ref, ln_ref, gn_ref, sv_ref, sw_ref, o_ref):
    x = x_ref[...]
    xb = x.astype(bf16)
    rw = _rw_finish(yf_ref[...] + yb_ref[...], rg_ref[...], rb_ref[...], gn_ref[...])
    s5 = _s5_finish(sy_ref[...], su_ref[...], sv_ref[...], sw_ref[...])
    ml = _sigmoid(mo_ref[...]) * (hf_ref[...] + hb_ref[...])
    branches = (att_ref[...], rw, s5, ml)
    merged = None
    for n in range(4):
        gate = _sigmoid(jnp.dot(xb, wg_ref[n], preferred_element_type=f32) + bg_ref[n:n + 1, :])
        wide = _bdot(branches[n], wba_ref[...] if n == 0 else wb_ref[n - 1])
        merged = gate * wide if merged is None else merged + gate * wide
    y = ALPHA * x + _bdot(merged, wo_ref[...])
    o_ref[...] = _layer_norm(y, ln_ref[0:1, :], ln_ref[1:2, :])


def _merge(xt, att, rw_parts, s5_parts, ml_parts, wg, bg, wba, wb, wo, ln, gn, s5_vec, s5_w, tm=256):
    T = xt.shape[0]
    row = lambda n: pl.BlockSpec((tm, n), lambda i: (i, 0))
    const = lambda shape: pl.BlockSpec(shape, lambda i: (0,) * len(shape), pipeline_mode=pl.Buffered(1))
    return pl.pallas_call(
        _merge_body, grid=(T // tm,),
        in_specs=[row(D_MODEL), row(512)] + [row(BRANCH_W)] * 9
        + [const((4, D_MODEL, D_MODEL)), const((4, D_MODEL)), const((512, D_MODEL)), const((3, BRANCH_W, D_MODEL)),
           const((D_MODEL, D_MODEL)), const((2, D_MODEL)), const((2, BRANCH_W)), const((2, BRANCH_W)),
           const((BRANCH_W, BRANCH_W))],
        out_specs=row(D_MODEL), out_shape=jax.ShapeDtypeStruct((T, D_MODEL), f32),
        compiler_params=_cparams(("parallel",)), name="merge")(
            xt, att, *rw_parts, *s5_parts, *ml_parts, wg, bg, wba, wb, wo, ln, gn, s5_vec, s5_w)


def _att_branch_weight(wb):
    z = jnp.zeros((HEAD_DIM, D_MODEL), f32)
    parts = []
    for h in range(ATT_HEADS):
        wh = wb[64 * h:64 * h + 64]
        parts += [wh, z] if h // 2 == 0 else [z, wh]
    return jnp.concatenate(parts, axis=0)


def _ffn_body(x_ref, w1_ref, w3_ref, w2_ref, ln_ref, o_ref, acc_ref):
    j = pl.program_id(1)
    xb = x_ref[...].astype(bf16)
    h1 = jnp.dot(xb, w1_ref[...], preferred_element_type=f32)
    h3 = jnp.dot(xb, w3_ref[...], preferred_element_type=f32)
    part = _bdot(h1 * _sigmoid(h1) * h3, w2_ref[...])

    @pl.when(j == 0)
    def _():
        acc_ref[...] = part

    @pl.when(j > 0)
    def _():
        acc_ref[...] += part

    @pl.when(j == pl.num_programs(1) - 1)
    def _():
        o_ref[...] = _layer_norm(ALPHA * x_ref[...] + acc_ref[...], ln_ref[0:1, :], ln_ref[1:2, :])


def _ffn(xt, w1, w3, w2, ln, tm=512, tf=1408):
    T = xt.shape[0]
    dff = w1.shape[1]
    return pl.pallas_call(
        _ffn_body, grid=(T // tm, dff // tf),
        in_specs=[pl.BlockSpec((tm, D_MODEL), lambda i, j: (i, 0)), pl.BlockSpec((D_MODEL, tf), lambda i, j: (0, j)),
                  pl.BlockSpec((D_MODEL, tf), lambda i, j: (0, j)), pl.BlockSpec((tf, D_MODEL), lambda i, j: (j, 0)),
                  pl.BlockSpec((2, D_MODEL), lambda i, j: (0, 0))],
        out_specs=pl.BlockSpec((tm, D_MODEL), lambda i, j: (i, 0)),
        out_shape=jax.ShapeDtypeStruct((T, D_MODEL), f32), scratch_shapes=[pltpu.VMEM((tm, D_MODEL), f32)],
        compiler_params=_cparams(("parallel", "arbitrary")), name="ffn")(xt, w1, w3, w2, ln)


MOE_TILE = 1024
SC_WINDOW = 128
SC_WORDS = 256


def _pack_words(x):
    bits = lax.bitcast_convert_type(x.astype(bf16).astype(f32), jnp.int32)
    half = D_MODEL // 2
    w = lax.shift_right_logical(bits[:, 0:half], 16) | bits[:, half:D_MODEL]
    return w[:, 0:SC_WORDS], w[:, SC_WORDS:2 * SC_WORDS]


def _unpack_words(wa, wb):
    w = jnp.concatenate([wa, wb], axis=1)
    lo = lax.bitcast_convert_type(lax.shift_left(w, 16), f32)
    hi = lax.bitcast_convert_type(w & jnp.int32(-65536), f32)
    return jnp.concatenate([lo, hi], axis=1)


def _router_body(x_ref, rt_ref, xa_ref, xb_ref, meta_ref, cnt_ref, run_ref):
    tb = x_ref.shape[0]

    @pl.when(pl.program_id(0) == 0)
    def _():
        run_ref[...] = jnp.zeros(run_ref.shape, f32)

    x = x_ref[...]
    xa_ref[...], xb_ref[...] = _pack_words(x)
    logits = _sdot3(x, rt_ref[...])
    lane = lax.broadcasted_iota(jnp.int32, logits.shape, 1)
    lg = jnp.where(lane < N_EXPERTS, logits, NEG)
    v1 = jnp.max(lg, axis=1, keepdims=True)
    i1 = jnp.min(jnp.where(lg == v1, lane, LANES), axis=1, keepdims=True)
    lg2 = jnp.where(lane == i1, NEG, lg)
    v2 = jnp.max(lg2, axis=1, keepdims=True)
    i2 = jnp.min(jnp.where(lg2 == v2, lane, LANES), axis=1, keepdims=True)
    e2 = jnp.exp(v2 - v1)
    sel1, sel2 = lane == i1, lane == i2
    mask = (sel1 | sel2).astype(f32)
    r = lax.broadcasted_iota(jnp.int32, (tb, tb), 0)
    c = lax.broadcasted_iota(jnp.int32, (tb, tb), 1)
    rank = _bdot((c < r).astype(f32), mask) + run_ref[0:1, :]
    run_ref[...] = run_ref[...] + jnp.sum(mask, axis=0, keepdims=True)
    rank1 = jnp.sum(jnp.where(sel1, rank, 0.0), axis=1, keepdims=True)
    rank2 = jnp.sum(jnp.where(sel2, rank, 0.0), axis=1, keepdims=True)
    cols = (i1.astype(f32), i2.astype(f32), rank1, rank2, 1.0 / (1.0 + e2), e2 / (1.0 + e2))
    meta = jnp.zeros(logits.shape, f32)
    for n, col in enumerate(cols):
        meta = jnp.where(lane == n, col, meta)
    meta_ref[...] = meta
    cnt_ref[...] = run_ref[...]


def _router(xt, router, tb=1024):
    T = xt.shape[0]
    tb = min(tb, T)
    return pl.pallas_call(
        _router_body, grid=(T // tb,),
        in_specs=[pl.BlockSpec((tb, D_MODEL), lambda i: (i, 0)), pl.BlockSpec((D_MODEL, LANES), lambda i: (0, 0))],
        out_specs=[pl.BlockSpec((tb, SC_WORDS), lambda i: (i, 0)), pl.BlockSpec((tb, SC_WORDS), lambda i: (i, 0)),
                   pl.BlockSpec((tb, LANES), lambda i: (i, 0)), pl.BlockSpec((8, LANES), lambda i: (0, 0))],
        out_shape=[jax.ShapeDtypeStruct((T, SC_WORDS), jnp.int32), jax.ShapeDtypeStruct((T, SC_WORDS), jnp.int32),
                   jax.ShapeDtypeStruct((T, LANES), f32), jax.ShapeDtypeStruct((8, LANES), f32)],
        scratch_shapes=[pltpu.VMEM((8, LANES), f32)],
        compiler_params=_cparams(("arbitrary",)), name="moe_router")(xt, router)


def _sc_gather(table, idx):
    n = idx.shape[0]
    mesh = plsc.VectorSubcoreMesh(core_axis_name="c", subcore_axis_name="s")

    @functools.partial(pl.kernel, out_type=jax.ShapeDtypeStruct((n, SC_WORDS), table.dtype), mesh=mesh)
    def gather(x_hbm, i_hbm, o_hbm):
        def body(i_vmem, o_vmem):
            pltpu.sync_copy(x_hbm.at[i_vmem.at[0]], o_vmem)

        pltpu.emit_pipeline(
            body, grid=(n // SC_WINDOW,),
            in_specs=[pl.BlockSpec((1, SC_WINDOW), index_map=lambda i: (0, i))],
            out_specs=[pl.BlockSpec((SC_WINDOW, SC_WORDS), index_map=lambda i: (i, 0))],
            core_axis_name="s", dimension_semantics=(pltpu.PARALLEL,))(i_hbm, o_hbm)

    return gather(table, idx.reshape(1, n))


def _sc_scatter(rows, idx, n_out):
    R = rows.shape[0]
    n = idx.shape[0]
    nblk = R // SC_WINDOW
    mesh = plsc.VectorSubcoreMesh(core_axis_name="c", subcore_axis_name="s")

    @functools.partial(pl.kernel, out_type=jax.ShapeDtypeStruct((n_out, SC_WORDS), rows.dtype), mesh=mesh,
                       scratch_types=[])
    def scatter(x_hbm, i_hbm, o_hbm):
        def body(x_vmem, i_vmem):
            pltpu.sync_copy(x_vmem, o_hbm.at[i_vmem.at[0]])

        pltpu.emit_pipeline(
            body, grid=(n // SC_WINDOW,),
            in_specs=[pl.BlockSpec((SC_WINDOW, SC_WORDS), index_map=lambda i: (i % nblk, 0)),
                      pl.BlockSpec((1, SC_WINDOW), index_map=lambda i: (0, i))],
            out_specs=[], core_axis_name="s", dimension_semantics=(pltpu.PARALLEL,))(x_hbm, i_hbm)

    return scatter(rows, idx.reshape(1, n))


def _experts_body(te_ref, nt_ref, xa_ref, xb_ref, w1_ref, w3_ref, w2_ref, oa_ref, ob_ref, acc_ref, x_ref):
    i = pl.program_id(0)
    j = pl.program_id(1)

    @pl.when((i < nt_ref[0]) & (j == 0))
    def _():
        x_ref[...] = _unpack_words(xa_ref[...], xb_ref[...]).astype(bf16)

    @pl.when(i < nt_ref[0])
    def _():
        x = x_ref[...]
        h1 = jnp.dot(x, w1_ref[0].astype(bf16), preferred_element_type=f32)
        h3 = jnp.dot(x, w3_ref[0].astype(bf16), preferred_element_type=f32)
        part = _bdot(h1 * _sigmoid(h1) * h3, w2_ref[0])

        @pl.when(j == 0)
        def _():
            acc_ref[...] = part

        @pl.when(j > 0)
        def _():
            acc_ref[...] += part

    @pl.when(j == pl.num_programs(1) - 1)
    def _():
        oa_ref[...], ob_ref[...] = _pack_words(jnp.where(i < nt_ref[0], acc_ref[...], 0.0))


def _experts(xa, xb, tile_expert, n_tiles, w1, w3, w2, tf=512):
    P = xa.shape[0]
    dff = w1.shape[2]
    words = pl.BlockSpec((MOE_TILE, SC_WORDS), lambda i, j, te, nt: (i, 0))
    grid_spec = pltpu.PrefetchScalarGridSpec(
        num_scalar_prefetch=2, grid=(P // MOE_TILE, dff // tf),
        in_specs=[words, words,
                  pl.BlockSpec((1, D_MODEL, tf), lambda i, j, te, nt: (te[i], 0, j)),
                  pl.BlockSpec((1, D_MODEL, tf), lambda i, j, te, nt: (te[i], 0, j)),
                  pl.BlockSpec((1, tf, D_MODEL), lambda i, j, te, nt: (te[i], j, 0))],
        out_specs=[words, words],
        scratch_shapes=[pltpu.VMEM((MOE_TILE, D_MODEL), f32), pltpu.VMEM((MOE_TILE, D_MODEL), bf16)])
    return pl.pallas_call(
        _experts_body, grid_spec=grid_spec, out_shape=[jax.ShapeDtypeStruct((P, SC_WORDS), jnp.int32)] * 2,
        compiler_params=_cparams(("parallel", "arbitrary")), name="moe_experts")(tile_expert, n_tiles, xa, xb, w1, w3, w2)


def _combine_body(x_ref, y0a_ref, y0b_ref, y1a_ref, y1b_ref, meta_ref, ln_ref, o_ref):
    meta = meta_ref[...]
    ff = (meta[:, 4:5] * _unpack_words(y0a_ref[...], y0b_ref[...])
          + meta[:, 5:6] * _unpack_words(y1a_ref[...], y1b_ref[...]))
    o_ref[...] = _layer_norm(ALPHA * x_ref[...] + ff, ln_ref[0:1, :], ln_ref[1:2, :])


def _combine(xt, yga, ygb, meta, ln, tm=1024):
    T = xt.shape[0]
    tm = min(tm, T)
    nb = T // tm
    row = pl.BlockSpec((tm, D_MODEL), lambda i: (i, 0))
    first = pl.BlockSpec((tm, SC_WORDS), lambda i: (i, 0))
    second = pl.BlockSpec((tm, SC_WORDS), lambda i: (nb + i, 0))
    return pl.pallas_call(
        _combine_body, grid=(nb,),
        in_specs=[row, first, first, second, second, pl.BlockSpec((tm, LANES), lambda i: (i, 0)),
                  pl.BlockSpec((2, D_MODEL), lambda i: (0, 0))],
        out_specs=row, out_shape=jax.ShapeDtypeStruct((T, D_MODEL), f32),
        compiler_params=_cparams(("parallel",)), name="moe_combine")(xt, yga, ygb, yga, ygb, meta, ln)


def _moe(xt, router, w1, w3, w2, ln):
    T = xt.shape[0]
    xa, xb, meta, cnt = _router(xt, router)
    counts = cnt[0, :N_EXPERTS].astype(jnp.int32)
    tiles = (counts + MOE_TILE - 1) // MOE_TILE
    tile_end = jnp.cumsum(tiles)
    offset = (tile_end - tiles) * MOE_TILE
    expert = meta[:, 0:2].astype(jnp.int32)
    onehot = expert[:, :, None] == jnp.arange(N_EXPERTS, dtype=jnp.int32)[None, None, :]
    pos = jnp.sum(jnp.where(onehot, offset[None, None, :], 0), axis=2) + meta[:, 2:4].astype(jnp.int32)
    pos = pos.T.reshape(-1)
    P = 2 * T + N_EXPERTS * MOE_TILE
    tile_id = jnp.arange(P // MOE_TILE, dtype=jnp.int32)
    tile_expert = jnp.minimum(jnp.sum((tile_id[:, None] >= tile_end[None, :]).astype(jnp.int32), axis=1), N_EXPERTS - 1)
    ya, yb = _experts(_sc_scatter(xa, pos, P), _sc_scatter(xb, pos, P), tile_expert, tile_end[-1:], w1, w3, w2)
    return _combine(xt, _sc_gather(ya, pos), _sc_gather(yb, pos), meta, ln)


def kernel(x, w_in, b_in, att_gq, att_gk, rw_mix, rw_w0, rw_w2, rw_a0, rw_a2, rw_g2, rw_kk, rw_ka, rw_rk, rw_ln_g, rw_ln_b, s5_lam_re, s5_lam_im, s5_log_dt, s5_b_re, s5_b_im, s5_c_re, s5_c_im, s5_d, s5_glu_w, s5_glu_b, ml_conv_w, ml_conv_b, ml_ib, ml_fb, w_gate, b_gate, w_branch, w_out, ln1_g, ln1_b, ffn_w1, ffn_w3, ffn_w2, moe_router, moe_w1, moe_w3, moe_w2, ln2_g, ln2_b):
    B, S, D = x.shape
    xt = x.reshape(B * S, D)
    cos, sin = _rope_tables(S)
    for l in range(DEPTH):
        att, s5u, mqk, mv, mo, rw, gt = _proj(xt, *_proj_params(w_in[l], b_in[l], ml_ib[l], ml_fb[l]))
        gain = jnp.concatenate([jnp.tile(att_gq[l], 8) * (HEAD_DIM ** -0.5), jnp.tile(att_gk[l], 2)])[None, :]
        q, k, v = _att_prep(att, cos, sin, gain, B, S)
        score_bound = 8.1 * jnp.max(jnp.abs(att_gq[l])) * jnp.max(jnp.abs(att_gk[l]))
        o_att = _flash(q, k, v, score_bound, B, S)
        rw_parts = _rwkv(rw, _rw_params(rw_mix[l], rw_w0[l], rw_w2[l], rw_a0[l], rw_a2[l], rw_g2[l], rw_kk[l],
                                        rw_ka[l], rw_rk[l]), B, S)
        y_s5 = _s5(s5u, _s5_params(s5_lam_re[l], s5_lam_im[l], s5_log_dt[l], s5_b_re[l], s5_b_im[l], s5_c_re[l],
                                   s5_c_im[l]), B, S)
        hf, hb = _mlstm(mqk, mv, gt, ml_conv_w[l], ml_conv_b[l][None, :], B, S)
        xt = _merge(xt, o_att, rw_parts, (y_s5, s5u), (hf, hb, mo), w_gate[l].astype(bf16), b_gate[l],
                    _att_branch_weight(w_branch[l, 0]).astype(bf16), w_branch[l, 1:].astype(bf16),
                    w_out[l].astype(bf16), jnp.stack([ln1_g[l], ln1_b[l]]), jnp.stack([rw_ln_g[l], rw_ln_b[l]]),
                    jnp.stack([s5_d[l], s5_glu_b[l]]), s5_glu_w[l].astype(bf16))
        ln2 = jnp.stack([ln2_g[l], ln2_b[l]])
        if l % 2 == 0:
            xt = _ffn(xt, ffn_w1[l // 2].astype(bf16), ffn_w3[l // 2].astype(bf16), ffn_w2[l // 2].astype(bf16), ln2)
        else:
            router = jnp.pad(moe_router[l // 2], ((0, 0), (0, LANES - N_EXPERTS)))
            xt = _moe(xt, router, moe_w1[l // 2], moe_w3[l // 2], moe_w2[l // 2], ln2)
    return xt.reshape(B, S, D)
```

```python
import functools
import math

import jax
import jax.numpy as jnp
import numpy as np
from jax import lax
from jax.experimental import pallas as pl
from jax.experimental.pallas import tpu as pltpu
from jax.experimental.pallas import tpu_sc as plsc

f32 = jnp.float32
bf16 = jnp.bfloat16

D_MODEL = 1024
DEPTH = 2
GRID_W = 64
BRANCH_W = 256
HEAD_DIM = 64
ATT_HEADS = 4
ATT_KV_HEADS = 2
ROPE_THETA = 10000.0
QK_EPS = 1e-6
RW_GN_EPS = 64e-5
RW_COLS = 1088
S5_GROUP = 16
S5_GROUPS = 16
S5_STATE = 64
ML_HEADS = 4
N_EXPERTS = 8
ALPHA = (2 * DEPTH) ** 0.25
LN_EPS = 1e-5

LANES = 128
CHUNK = 64
NEG = -1e30
VMEM_LIMIT = 56 * 1024 * 1024

PROJ_SPLITS = (768, 256, 512, 256, 256, RW_COLS)


def _cparams(sem):
    return pltpu.CompilerParams(dimension_semantics=sem, vmem_limit_bytes=VMEM_LIMIT)


def _sigmoid(x):
    return 1.0 / (1.0 + jnp.exp(-x))


def _softplus(x):
    return jnp.maximum(x, 0.0) + jnp.log(1.0 + jnp.exp(-jnp.abs(x)))


def _dims(a, lhs_c, rhs_c):
    lead = a.ndim - 2
    batch = tuple(range(lead))
    return (((lhs_c + lead,), (rhs_c + lead,)), (batch, batch))


def _bdot(a, b):
    return lax.dot_general(a.astype(bf16), b.astype(bf16), _dims(a, 1, 0), preferred_element_type=f32)


def _bdot_nt(a, b):
    return lax.dot_general(a.astype(bf16), b.astype(bf16), _dims(a, 1, 1), preferred_element_type=f32)


def _bdot_tn(a, b):
    return lax.dot_general(a.astype(bf16), b.astype(bf16), _dims(a, 0, 0), preferred_element_type=f32)


def _split(x):
    hi = x.astype(bf16)
    return hi, (x - hi.astype(f32)).astype(bf16)


def _sdot(a, b, exact):
    dims = _dims(a, 1, 0)
    if exact == "rhs":
        hi, lo = _split(a)
        bb = b.astype(bf16)
        return (lax.dot_general(hi, bb, dims, preferred_element_type=f32)
                + lax.dot_general(lo, bb, dims, preferred_element_type=f32))
    hi, lo = _split(b)
    ab = a.astype(bf16)
    return (lax.dot_general(ab, hi, dims, preferred_element_type=f32)
            + lax.dot_general(ab, lo, dims, preferred_element_type=f32))


def _sdot3(a, b):
    dims = _dims(a, 1, 0)
    ah, al = _split(a)
    bh, bl = _split(b)
    return (lax.dot_general(ah, bh, dims, preferred_element_type=f32)
            + lax.dot_general(ah, bl, dims, preferred_element_type=f32)
            + lax.dot_general(al, bh, dims, preferred_element_type=f32))


def _seg_matrix(n, seg=HEAD_DIM):
    r = lax.broadcasted_iota(jnp.int32, (n, n), 0) // seg
    c = lax.broadcasted_iota(jnp.int32, (n, n), 1) // seg
    return (r == c).astype(f32)


def _layer_norm(y, g, b):
    mu = jnp.mean(y, axis=-1, keepdims=True)
    d = y - mu
    var = jnp.mean(d * d, axis=-1, keepdims=True)
    return d * lax.rsqrt(var + LN_EPS) * g + b


def _row_to_col(row, eye):
    return jnp.sum(jnp.where(eye, jnp.broadcast_to(row, eye.shape), 0.0), axis=2, keepdims=True)


def _stack_heads(x):
    h0 = lax.broadcasted_iota(jnp.int32, x.shape, 2) < HEAD_DIM
    return jnp.concatenate([jnp.where(h0, x, 0.0), jnp.where(h0, 0.0, x)], axis=1)


def _proj_body(x_ref, xp_ref, xn_ref, cos_ref, sin_ref, w_ref, b_ref, wg_ref, bg_ref, gain_ref, cw_ref, cb_ref,
               mix_ref, w2_ref, a2_ref, g2_ref, vec_ref,
               q_ref, k_ref, v_ref, s5_ref, mq_ref, mk_ref, mv_ref, mo_ref, r_ref, rk_ref, rv_ref, an_ref, bn_ref,
               lw_ref, gate_ref, bonus_ref, g_ref, mbuf_ref, rbuf_ref):
    tm = x_ref.shape[0]
    xb, xp, xn = x_ref[...].astype(bf16), xp_ref[...].astype(bf16), xn_ref[...].astype(bf16)
    offs = np.cumsum((0,) + PROJ_SPLITS)

    def cols(rows, n):
        sl = slice(int(offs[n]), int(offs[n + 1]))
        return jnp.dot(rows, w_ref[:, sl], preferred_element_type=f32) + b_ref[:, sl]

    q_ref[...], k_ref[...], v_ref[...] = _att_prep(cols(xb, 0), cos_ref[...], sin_ref[...], gain_ref[...])
    s5_ref[...] = cols(xb, 1)
    _fill_halo(mbuf_ref, cols(xb, 2), cols(xp, 2), cols(xn, 2))
    mq_ref[...], mk_ref[...] = _ml_prep(mbuf_ref, tm, cw_ref[...], cb_ref[...])
    mv_ref[...] = cols(xb, 3)
    mo_ref[...] = cols(xb, 4)
    _fill_halo(rbuf_ref, cols(xb, 5), cols(xp, 5), cols(xn, 5))
    outs = _rw_prep(rbuf_ref, tm, mix_ref[...], w2_ref[...], a2_ref[...], g2_ref[...], vec_ref[...])
    for o_ref, val in zip((r_ref, rk_ref, rv_ref, an_ref, bn_ref), outs[0:5]):
        o_ref[...] = val
    lw_ref[0], lw_ref[1], gate_ref[...], bonus_ref[...] = outs[5:9]
    g_ref[...] = lax.dot_general(wg_ref[...], xb, (((1,), (1,)), ((), ())), preferred_element_type=f32) + bg_ref[...]


def _proj(xt, proj_prm, att_prm, ml_prm, rw_prm, B, S, tm=512):
    T = B * S
    nb = S // tm
    n_tot = sum(PROJ_SPLITS)
    row = lambda n: pl.BlockSpec((tm, n), lambda b, i: (b * nb + i, 0))
    const = lambda a: pl.BlockSpec(a.shape, lambda b, i: (0,) * a.ndim)
    tab = pl.BlockSpec((tm, LANES), lambda b, i: (i, 0))
    f32out = lambda n: jax.ShapeDtypeStruct((T, n), f32)
    consts = list(proj_prm) + [att_prm[2]] + list(ml_prm) + list(rw_prm)
    out_specs = ([row(512), row(LANES), row(2 * LANES)] + [row(BRANCH_W)] * 10
                 + [pl.BlockSpec((2, tm, BRANCH_W), lambda b, i: (0, b * nb + i, 0)), row(BRANCH_W), row(BRANCH_W),
                    pl.BlockSpec((16, tm), lambda b, i: (0, b * nb + i))])
    out_shape = ([jax.ShapeDtypeStruct((T, 512), bf16), jax.ShapeDtypeStruct((T, LANES), bf16),
                  jax.ShapeDtypeStruct((T, 2 * LANES), bf16)] + [f32out(BRANCH_W)] * 10
                 + [jax.ShapeDtypeStruct((2, T, BRANCH_W), f32), f32out(BRANCH_W), f32out(BRANCH_W),
                    jax.ShapeDtypeStruct((16, T), f32)])
    return pl.pallas_call(
        _proj_body, grid=(B, nb),
        in_specs=_halo_specs(D_MODEL, tm, B, S) + [tab, tab] + [const(a) for a in consts],
        out_specs=out_specs, out_shape=out_shape,
        scratch_shapes=[pltpu.VMEM((tm + 16, 512), f32), pltpu.VMEM((tm + 16, RW_COLS), f32)],
        compiler_params=_cparams(("parallel", "parallel")), name="proj")(
            xt, xt, xt, att_prm[0], att_prm[1], *consts)


def _proj_params(w_in, b_in, ml_ib, ml_fb):
    o = np.cumsum((0, 256, 128, 128, RW_COLS, 256, 512, 256, 8, 8, 256))
    sl = lambda i: (w_in[:, o[i]:o[i + 1]], b_in[o[i]:o[i + 1]])
    (wq, bq), (wk, bk), (wv, bv), (wrw, brw), (ws5, bs5), (wqk, bqk), (wmv, bmv), (wi, bi), (wf, bf), (wo, bo) = (
        sl(i) for i in range(10))
    zw, zb = jnp.zeros((D_MODEL, HEAD_DIM), f32), jnp.zeros((HEAD_DIM,), f32)
    wq_e, bq_e = [], []
    for h in range(ATT_HEADS):
        wh, bh = wq[:, 64 * h:64 * h + 64], bq[64 * h:64 * h + 64]
        wq_e += [wh, zw] if h // 2 == 0 else [zw, wh]
        bq_e += [bh, zb] if h // 2 == 0 else [zb, bh]
    w = jnp.concatenate(wq_e + [wk, wv, ws5, wqk, wmv, wo, wrw], axis=1)
    b = jnp.concatenate(bq_e + [bk, bv, bs5, bqk, bmv, bo, brw])
    wg = jnp.concatenate([wi, wf], axis=1).T
    bg = jnp.concatenate([bi + ml_ib.reshape(-1), bf + ml_fb.reshape(-1)])
    return w.astype(bf16), b[None, :], wg.astype(bf16), bg[:, None]


def _att_prep(att, cos, sin, gain):
    x = att[:, 0:640]
    ms = _sdot(x * x, _seg_matrix(640), "rhs") * (1.0 / HEAD_DIM)
    xn = x * lax.rsqrt(ms + QK_EPS) * gain
    lane = lax.broadcasted_iota(jnp.int32, xn.shape, 1)
    partner = jnp.where((lane % 32) < 16, pltpu.roll(xn, 640 - 16, 1), pltpu.roll(xn, 16, 1))
    rot = xn * jnp.concatenate([cos] * 5, axis=1) + partner * jnp.concatenate([sin] * 5, axis=1)
    v = jnp.concatenate([att[:, 640:768].astype(bf16), jnp.ones((x.shape[0], LANES), bf16)], axis=1)
    return rot[:, 0:512].astype(bf16), rot[:, 512:640].astype(bf16), v


def _rope_tables(S):
    t = np.arange(S)
    row = (t // GRID_W).astype(np.float32)
    col = (t % GRID_W).astype(np.float32)
    n = 16
    inv = np.power(np.float32(ROPE_THETA), -np.arange(n, dtype=np.float32) / n).astype(np.float32)
    ar = jnp.asarray(row)[:, None] * jnp.asarray(inv)
    ac = jnp.asarray(col)[:, None] * jnp.asarray(inv)
    cos = jnp.concatenate([jnp.cos(ar), jnp.cos(ar), jnp.cos(ac), jnp.cos(ac)], axis=1)
    sin = jnp.concatenate([-jnp.sin(ar), jnp.sin(ar), -jnp.sin(ac), jnp.sin(ac)], axis=1)
    return jnp.concatenate([cos, cos], axis=1), jnp.concatenate([sin, sin], axis=1)


def _flash_body(q_ref, k_ref, v_ref, o_ref, acc_ref, *m_scratch, tk, track_max):
    tq = q_ref.shape[0]
    nk = k_ref.shape[0] // tk
    q2 = jnp.concatenate([q_ref[:, 0:LANES], q_ref[:, LANES:2 * LANES]], axis=0)
    acc_ref[...] = jnp.zeros(acc_ref.shape, f32)
    if track_max:
        m_ref, = m_scratch
        m_ref[...] = jnp.full(m_ref.shape, NEG, f32)

    def step(j, carry):
        rows = pl.ds(pl.multiple_of(j * tk, tk), tk)
        s = lax.dot_general(q2, k_ref[rows, :], (((1,), (1,)), ((), ())), preferred_element_type=f32)
        if track_max:
            m_old = m_ref[...]
            m_new = jnp.maximum(m_old, jnp.max(s, axis=1, keepdims=True))
            p = jnp.exp(s - m_new).astype(bf16)
            acc_ref[...] = jnp.exp(m_old - m_new) * acc_ref[...] + jnp.dot(p, v_ref[rows, :], preferred_element_type=f32)
            m_ref[...] = m_new
        else:
            acc_ref[...] += jnp.dot(jnp.exp(s).astype(bf16), v_ref[rows, :], preferred_element_type=f32)
        return carry

    lax.fori_loop(0, nk, step, 0)
    o = acc_ref[:, 0:LANES] / acc_ref[:, LANES:2 * LANES]
    o_ref[...] = jnp.concatenate([o[0:tq], o[tq:2 * tq]], axis=1)


SCORE_BOUND_MAX = 60.0


def _flash(q, k, v, score_bound, B, S, tq=256, tk=8192):
    T = B * S
    nb = S // tq
    tk = min(tk, S)

    def call(track_max):
        scratch = [pltpu.VMEM((2 * tq, 2 * LANES), f32)] + ([pltpu.VMEM((2 * tq, 1), f32)] if track_max else [])
        return pl.pallas_call(
            functools.partial(_flash_body, tk=tk, track_max=track_max), grid=(B, ATT_KV_HEADS, nb),
            in_specs=[pl.BlockSpec((tq, 2 * LANES), lambda b, g, i: (b * nb + i, g)),
                      pl.BlockSpec((S, LANES), lambda b, g, i: (b, 0)),
                      pl.BlockSpec((S, 2 * LANES), lambda b, g, i: (b, 0))],
            out_specs=pl.BlockSpec((tq, 2 * LANES), lambda b, g, i: (b * nb + i, g)),
            out_shape=jax.ShapeDtypeStruct((T, 512), f32), scratch_shapes=scratch,
            compiler_params=_cparams(("parallel", "parallel", "parallel")),
            name="flash_safe" if track_max else "flash")(q, k, v)

    return lax.cond(score_bound <= SCORE_BOUND_MAX, lambda: call(False), lambda: call(True))


def _halo_specs(width, tm, B, S):
    nb = S // tm
    r8 = tm // 8
    last8 = B * S // 8 - 1

    def main(b, i):
        return (b * nb + i, 0)

    def prev(b, i):
        return (jnp.maximum(b * (S // 8) + i * r8 - 1, 0), 0)

    def nxt(b, i):
        return (jnp.minimum(b * (S // 8) + (i + 1) * r8, last8), 0)

    return [pl.BlockSpec((tm, width), main), pl.BlockSpec((8, width), prev), pl.BlockSpec((8, width), nxt)]


def _fill_halo(buf_ref, x, prev8, next8):
    tm = x.shape[0]
    i = pl.program_id(1)
    last = pl.num_programs(1) - 1
    buf_ref[pl.ds(8, tm), :] = x
    buf_ref[pl.ds(0, 8), :] = jnp.where(i > 0, prev8, 0.0)
    buf_ref[pl.ds(8 + tm, 8), :] = jnp.where(i < last, next8, 0.0)


def _rw_prep(buf_ref, tm, mix, w2, a2, g2, vec):
    x = buf_ref[pl.ds(8, tm), :]
    p = x + mix[0:1, :] * (buf_ref[pl.ds(7, tm), :] - x) + mix[1:2, :] * (buf_ref[pl.ds(9, tm), :] - x)
    r, k, v = p[:, 0:256], p[:, 256:512], p[:, 512:768]
    w0f, w0b, a0, k_k, k_a, r_k = (vec[j:j + 1, :] for j in range(6))
    dec = _bdot(jnp.tanh(p[:, 768:896]), w2)
    z = p[:, 896:1088]
    a = _sigmoid(a0 + _bdot(z, a2))
    gate = _bdot(_sigmoid(z), g2)
    seg = _seg_matrix(BRANCH_W)
    kk = k * k_k
    kk = kk / jnp.maximum(jnp.sqrt(_sdot(kk * kk, seg, "rhs")), 1e-12)
    k2 = k * (1.0 + (a - 1.0) * k_a)
    bonus = _sdot(r * k2 * r_k, seg, "rhs") * v
    lwf = -jnp.exp(-_softplus(-(w0f + dec[:, 0:256])) - 0.5)
    lwb = -jnp.exp(-_softplus(-(w0b + dec[:, 256:512])) - 0.5)
    return r, k2, v, -kk, kk * a, lwf, lwb, gate, bonus


def _pair_masks(rev, nbatch):
    n = 2 * CHUNK
    r = lax.broadcasted_iota(jnp.int32, (nbatch, n, n), 1)
    c = lax.broadcasted_iota(jnp.int32, (nbatch, n, n), 2)
    same = (r // CHUNK) == (c // CHUNK)
    if rev:
        return r, c, same & (c > r), same & (c >= r)
    return r, c, same & (c < r), same & (c <= r)


def _rw_chunk(st, r, k, v, an, bn, lw, rev):
    L = CHUNK
    N = r.shape[0]
    ri = lax.broadcasted_iota(jnp.int32, (N, L, L), 1)
    ci = lax.broadcasted_iota(jnp.int32, (N, L, L), 2)
    tri = ((ci >= ri) if rev else (ci <= ri)).astype(f32)
    cs = _sdot(tri, lw, "lhs")
    tot = jnp.sum(lw, axis=1, keepdims=True)
    e_neg = jnp.exp(-cs)
    a2 = _stack_heads(an * jnp.exp(cs - lw))
    r2 = _stack_heads(r * jnp.exp(cs))
    b2 = _stack_heads(bn * e_neg)
    k2 = _stack_heads(k * e_neg)
    v2 = _stack_heads(v)
    rr, cc, strict, incl = _pair_masks(rev, N)
    n2 = 2 * L
    g = _bdot_nt(jnp.concatenate([a2, r2], axis=1), jnp.concatenate([b2, k2], axis=1))
    mab = jnp.where(strict, g[:, 0:n2, 0:n2], 0.0)
    mak = jnp.where(strict, g[:, 0:n2, n2:2 * n2], 0.0)
    pb = jnp.where(incl, g[:, n2:2 * n2, 0:n2], 0.0)
    pk = jnp.where(incl, g[:, n2:2 * n2, n2:2 * n2], 0.0)
    eye = rr == cc
    m8 = jnp.where((rr // 8) == (cc // 8), mab, 0.0)
    x = eye.astype(f32) + m8
    p = _bdot(m8, m8)
    x = x + _bdot(x, p)
    p = _bdot(p, p)
    x = x + _bdot(x, p)
    n = 8
    while n < L:
        e = jnp.where(((rr // (2 * n)) == (cc // (2 * n))) & ((rr // n) != (cc // n)), mab, 0.0)
        x = x + _bdot(_bdot(x, e), x)
        n *= 2
    wu = _bdot(x, jnp.concatenate([a2, _bdot(mak, v2)], axis=2))
    pwu = _bdot(pb, wu)
    rh = r2 + pwu[:, :, 0:LANES]
    y2 = pwu[:, :, LANES:2 * LANES] + _bdot(jnp.concatenate([pk, rh], axis=2), jnp.concatenate([v2, st], axis=1))
    y = y2[:, 0:L] + y2[:, L:2 * L]
    gam = _row_to_col(jnp.exp(tot), eye)
    bwu = _bdot_tn(b2, wu)
    st = gam * (st + _bdot(bwu[:, :, 0:LANES], st) + bwu[:, :, LANES:2 * LANES] + _bdot_tn(k2, v2))
    return y, st


def _load_pairs(ref, rows):
    return jnp.concatenate([ref[:, rows, 0:LANES], ref[:, rows, LANES:2 * LANES]], axis=0)


def _store_pairs(ref, rows, y):
    nb = ref.shape[0]
    ref[:, rows, 0:LANES] = y[0:nb]
    ref[:, rows, LANES:2 * LANES] = y[nb:2 * nb]


def _rw_scan_body(r_ref, k_ref, v_ref, an_ref, bn_ref, lw_ref, y_ref, st_ref, *, rev):
    nch = r_ref.shape[1] // CHUNK

    @pl.when(pl.program_id(0) == 0)
    def _():
        st_ref[...] = jnp.zeros(st_ref.shape, f32)

    def step(cc, carry):
        c = (nch - 1 - cc) if rev else cc
        rows = pl.ds(pl.multiple_of(c * CHUNK, CHUNK), CHUNK)
        y, st = _rw_chunk(st_ref[...], *(_load_pairs(ref, rows) for ref in (r_ref, k_ref, v_ref, an_ref, bn_ref)),
                          _load_pairs(lw_ref.at[0], rows), rev)
        _store_pairs(y_ref, rows, y)
        st_ref[...] = st
        return carry

    lax.fori_loop(0, nch, step, 0)


def _rw_scan(r, k, v, an, bn, lw, B, S, rev, ts=256):
    nb = S // ts
    d = 1 if rev else 0
    blk = (lambda i: nb - 1 - i) if rev else (lambda i: i)
    io = pl.BlockSpec((B, ts, BRANCH_W), lambda i: (0, blk(i), 0))
    return pl.pallas_call(
        functools.partial(_rw_scan_body, rev=rev), grid=(nb,),
        in_specs=[io] * 5 + [pl.BlockSpec((1, B, ts, BRANCH_W), lambda i: (d, 0, blk(i), 0))],
        out_specs=io, out_shape=jax.ShapeDtypeStruct((B, S, BRANCH_W), f32),
        scratch_shapes=[pltpu.VMEM((B * BRANCH_W // LANES, LANES, LANES), f32)],
        compiler_params=_cparams(("arbitrary",)),
        name="rw_scan_bwd" if rev else "rw_scan_fwd")(r, k, v, an, bn, lw)


def _rw_finish(y, gate, bonus, gn):
    seg = _seg_matrix(BRANCH_W)
    mu = _sdot(y, seg, "rhs") * (1.0 / HEAD_DIM)
    d = y - mu
    var = _sdot(d * d, seg, "rhs") * (1.0 / HEAD_DIM)
    yn = d * lax.rsqrt(var + RW_GN_EPS) * gn[0:1, :] + gn[1:2, :]
    return (yn + bonus) * gate


def _rwkv(r, k, v, an, bn, lw, B, S):
    seq = [t.reshape(B, S, BRANCH_W) for t in (r, k, v, an, bn)] + [lw.reshape(2, B, S, BRANCH_W)]
    yf = _rw_scan(*seq, B, S, rev=False).reshape(B * S, BRANCH_W)
    yb = _rw_scan(*seq, B, S, rev=True).reshape(B * S, BRANCH_W)
    return yf, yb


def _rw_params(mix, w0, w2, a0, a2, g2, k_k, k_a, r_k):
    z = jnp.zeros((64, 256), f32)
    w2c = jnp.concatenate([jnp.concatenate([w2[0], z], axis=1), jnp.concatenate([z, w2[1]], axis=1)], axis=0)
    a2p = jnp.concatenate([a2, jnp.zeros((128, 256), f32)], axis=0)
    g2p = jnp.concatenate([jnp.zeros((64, 256), f32), g2], axis=0)
    vec = jnp.stack([w0[0], w0[1], a0, k_k, k_a, r_k.reshape(-1), jnp.zeros_like(a0), jnp.zeros_like(a0)])
    return mix, w2c.astype(bf16), a2p.astype(bf16), g2p.astype(bf16), vec


def _s5_scan_body(u_ref, bcat_ref, cre_ref, cim_ref, lre_ref, lim_ref, y_ref, sre_ref, sim_ref, bre_ref, bim_ref):
    ts = u_ref.shape[0]
    n = S5_GROUPS * S5_STATE

    @pl.when(pl.program_id(0) == 0)
    def _():
        sre_ref[...] = jnp.zeros(sre_ref.shape, f32)
        sim_ref[...] = jnp.zeros(sim_ref.shape, f32)

    u = u_ref[...].reshape(ts * 8, BRANCH_W).astype(bf16)
    fwd = lax.broadcasted_iota(jnp.int32, (ts, 8, n), 1) < 4

    def bu(part):
        return jnp.dot(u, bcat_ref[:, part * n:(part + 1) * n], preferred_element_type=f32).reshape(ts, 8, n)

    bre_ref[...] = jnp.where(fwd, bu(0), bu(2))
    bim_ref[...] = jnp.where(fwd, bu(1), bu(3))
    lre = lre_ref[...]
    lim = lim_ref[...]

    def step(t, carry):
        sre, sim = carry
        nre = lre * sre - lim * sim + bre_ref[t]
        nim = lre * sim + lim * sre + bim_ref[t]
        bre_ref[t] = nre
        bim_ref[t] = nim
        return nre, nim

    sre, sim = lax.fori_loop(0, ts, step, (sre_ref[...], sim_ref[...]), unroll=4)
    sre_ref[...] = sre
    sim_ref[...] = sim
    y = (_bdot(bre_ref[...].reshape(ts * 8, n), cre_ref[...])
         - _bdot(bim_ref[...].reshape(ts * 8, n), cim_ref[...])).reshape(ts, 8, 2 * BRANCH_W)
    fwd_o = lax.broadcasted_iota(jnp.int32, (ts, 8, BRANCH_W), 1) < 4
    y_ref[...] = jnp.where(fwd_o, y[:, :, 0:BRANCH_W], y[:, :, BRANCH_W:2 * BRANCH_W])


def _s5_scan(u2, bcat, cre, cim, lre, lim, ts=128):
    S = u2.shape[0]
    n = S5_GROUPS * S5_STATE
    full = lambda shape: pl.BlockSpec(shape, lambda i: (0,) * len(shape))
    return pl.pallas_call(
        _s5_scan_body, grid=(S // ts,),
        in_specs=[pl.BlockSpec((ts, 8, BRANCH_W), lambda i: (i, 0, 0)), full((BRANCH_W, 4 * n)),
                  full((n, 2 * BRANCH_W)), full((n, 2 * BRANCH_W)), full((8, n)), full((8, n))],
        out_specs=pl.BlockSpec((ts, 8, BRANCH_W), lambda i: (i, 0, 0)),
        out_shape=jax.ShapeDtypeStruct((S, 8, BRANCH_W), f32),
        scratch_shapes=[pltpu.VMEM((8, n), f32), pltpu.VMEM((8, n), f32),
                        pltpu.VMEM((ts, 8, n), f32), pltpu.VMEM((ts, 8, n), f32)],
        compiler_params=_cparams(("arbitrary",)), name="s5_scan")(u2, bcat, cre, cim, lre, lim)


def _s5_finish(y, u, vec, w):
    y = y + u * vec[0:1, :]
    y = 0.5 * y * (1.0 + jnp.tanh(math.sqrt(2.0 / math.pi) * (y + 0.044715 * (y * y * y))))
    return y * _sigmoid(_bdot(y, w) + vec[1:2, :])


def _s5_params(lam_re, lam_im, log_dt, b_re, b_im, c_re, c_im):
    G, P, C = S5_GROUPS, S5_STATE, S5_GROUP
    eye = jnp.eye(G, dtype=f32)
    b_c = lax.complex(b_re, b_im)
    bcat, cre, cim, lre, lim = [], [], [], [], []
    for d in range(2):
        lam = lax.complex(jnp.minimum(lam_re[d], -1e-4), lam_im[d])
        lam_bar = jnp.exp(lam * jnp.exp(log_dt[d])[:, None])
        b_bar = ((lam_bar - 1.0) / lam)[..., None] * b_c
        for part in (jnp.real(b_bar), jnp.imag(b_bar)):
            bcat.append(jnp.einsum('gh,gpc->gchp', eye, part).reshape(G * C, G * P))
        cre.append(jnp.einsum('gh,gcp->gphc', eye, c_re[d]).reshape(G * P, G * C))
        cim.append(jnp.einsum('gh,gcp->gphc', eye, c_im[d]).reshape(G * P, G * C))
        lre.append(jnp.broadcast_to(jnp.real(lam_bar).reshape(1, G * P), (4, G * P)))
        lim.append(jnp.broadcast_to(jnp.imag(lam_bar).reshape(1, G * P), (4, G * P)))
    return (jnp.concatenate(bcat, axis=1).astype(bf16), jnp.concatenate(cre, axis=1).astype(bf16),
            jnp.concatenate(cim, axis=1).astype(bf16), jnp.concatenate(lre, axis=0), jnp.concatenate(lim, axis=0))


def _s5(u, prm, B, S):
    u3 = u.reshape(B, S, BRANCH_W).transpose(1, 0, 2)
    u2 = jnp.concatenate([u3, u3[::-1]], axis=1)
    y2 = _s5_scan(u2, *prm)
    return (y2[:, 0:B] + y2[::-1, B:2 * B]).transpose(1, 0, 2).reshape(B * S, BRANCH_W)


def _ml_prep(buf_ref, tm, w, b):
    y = b
    for j in range(5):
        y = y + w[j:j + 1, :] * buf_ref[pl.ds(6 + j, tm), :]
    y = y * _sigmoid(y)
    return y[:, 0:BRANCH_W], y[:, BRANCH_W:2 * BRANCH_W] * (HEAD_DIM ** -0.5)


def _ml_chunk(state, q, k, v, li, lfp, rev):
    cn, m_row = state
    L = CHUNK
    N = q.shape[0]
    rr, cc, _, incl = _pair_masks(rev, N)
    same = (rr // L) == (cc // L)
    eye = rr == cc
    lane = lax.broadcasted_iota(jnp.int32, (N, 1, 2 * L), 2)
    lf = jnp.minimum(lfp, 0.0) - jnp.log(1.0 + jnp.exp(-jnp.abs(lfp)))
    lf8 = jnp.broadcast_to(lf, (N, 8, 2 * L))
    cum = (same & ((rr >= cc) if rev else (rr <= cc))).astype(f32)
    b_row = _sdot(lf8, cum, "rhs")[:, 0:1]
    g_row = _sdot(lf8, same.astype(f32), "rhs")[:, 0:1]
    w_end = g_row - b_row + li
    m0 = jnp.max(jnp.where(lane < L, w_end, NEG), axis=2, keepdims=True)
    m1 = jnp.max(jnp.where(lane < L, NEG, w_end), axis=2, keepdims=True)
    m_loc = jnp.where(lane < L, m0, m1)
    e_col = _row_to_col(jnp.exp(w_end - m_loc), eye)
    b_col = _row_to_col(b_row, eye)
    q2, k2, v2 = _stack_heads(q), _stack_heads(k), _stack_heads(v)
    v1 = jnp.concatenate([v2, _stack_heads(jnp.ones_like(v))], axis=2)
    log_inter = b_col + _row_to_col(m_row, eye)
    log_intra = jnp.where(incl, b_col - b_row + li, NEG)
    m_r = jnp.maximum(log_inter, jnp.max(log_intra, axis=2, keepdims=True))
    s = _bdot_nt(q2, k2) * jnp.exp(log_intra - m_r)
    inter = jnp.exp(log_inter - m_r)
    nd = _bdot(s, v1) + inter * _bdot(q2, cn)
    h2 = nd[:, :, 0:LANES] / jnp.maximum(jnp.abs(nd[:, :, LANES:2 * LANES]), jnp.exp(-m_r))
    h = h2[:, 0:L] + h2[:, L:2 * L]
    m_new = jnp.maximum(g_row + m_row, m_loc)
    a = jnp.exp(g_row + m_row - m_new)
    bb = jnp.exp(m_loc - m_new)
    cn = jnp.concatenate([a, a], axis=2) * cn + jnp.concatenate([bb, bb], axis=2) * _bdot_tn(e_col * k2, v1)
    return h, (cn, m_new)


def _ml_scan_body(q_ref, k_ref, v_ref, g_ref, h_ref, c_ref, m_ref, *, rev):
    nch = q_ref.shape[1] // CHUNK

    @pl.when(pl.program_id(0) == 0)
    def _():
        c_ref[...] = jnp.zeros(c_ref.shape, f32)
        m_ref[...] = jnp.zeros(m_ref.shape, f32)

    def step(cc, carry):
        c = (nch - 1 - cc) if rev else cc
        rows = pl.ds(pl.multiple_of(c * CHUNK, CHUNK), CHUNK)
        li, lfp = (jnp.concatenate([g_ref[t, 0, 0, :, pl.ds(c, 1), :], g_ref[t, 0, 1, :, pl.ds(c, 1), :]], axis=0)
                   for t in range(2))
        h, (cn, m_row) = _ml_chunk((c_ref[...], m_ref[...]), _load_pairs(q_ref, rows), _load_pairs(k_ref, rows),
                                   _load_pairs(v_ref, rows), li, lfp, rev)
        _store_pairs(h_ref, rows, h)
        c_ref[...] = cn
        m_ref[...] = m_row
        return carry

    lax.fori_loop(0, nch, step, 0)


def _ml_scan(q, k, v, g, B, S, rev, ts=512):
    nb = S // ts
    nch = ts // CHUNK
    d = 1 if rev else 0
    nchain = B * BRANCH_W // LANES
    blk = (lambda i: nb - 1 - i) if rev else (lambda i: i)
    io = pl.BlockSpec((B, ts, BRANCH_W), lambda i: (0, blk(i), 0))
    return pl.pallas_call(
        functools.partial(_ml_scan_body, rev=rev), grid=(nb,),
        in_specs=[io, io, io, pl.BlockSpec((2, 1, 2, B, nch, LANES), lambda i: (0, d, 0, 0, blk(i), 0))],
        out_specs=io, out_shape=jax.ShapeDtypeStruct((B, S, BRANCH_W), f32),
        scratch_shapes=[pltpu.VMEM((nchain, LANES, 2 * LANES), f32), pltpu.VMEM((nchain, 1, LANES), f32)],
        compiler_params=_cparams(("arbitrary",)),
        name="ml_scan_bwd" if rev else "ml_scan_fwd")(q, k, v, g)


def _mlstm(q, k, mv, gt, B, S):
    g = gt.reshape(2, 2, 2, 2, B, S // CHUNK, CHUNK).transpose(0, 1, 2, 4, 5, 3, 6).reshape(2, 2, 2, B, S // CHUNK, LANES)
    seq = [t.reshape(B, S, BRANCH_W) for t in (q, k, mv)]
    return (_ml_scan(*seq, g, B, S, rev=False).reshape(B * S, BRANCH_W),
            _ml_scan(*seq, g, B, S, rev=True).reshape(B * S, BRANCH_W))


def _merge_body(x_ref, att_ref, yf_ref, yb_ref, rg_ref, rb_ref, sy_ref, su_ref, hf_ref, hb_ref, mo_ref,
                wg_ref, bg_ref, wba_ref, wb_ref, wo_ref, ln_ref, gn_ref, sv_ref, sw_ref, o_ref):
    x = x_ref[...]
    xb = x.astype(bf16)
    rw = _rw_finish(yf_ref[...] + yb_ref[...], rg_ref[...], rb_ref[...], gn_ref[...])
    s5 = _s5_finish(sy_ref[...], su_ref[...], sv_ref[...], sw_ref[...])
    ml = _sigmoid(mo_ref[...]) * (hf_ref[...] + hb_ref[...])
    branches = (att_ref[...], rw, s5, ml)
    merged = None
    for n in range(4):
        gate = _sigmoid(jnp.dot(xb, wg_ref[n], preferred_element_type=f32) + bg_ref[n:n + 1, :])
        wide = _bdot(branches[n], wba_ref[...] if n == 0 else wb_ref[n - 1])
        merged = gate * wide if merged is None else merged + gate * wide
    y = ALPHA * x + _bdot(merged, wo_ref[...])
    o_ref[...] = _layer_norm(y, ln_ref[0:1, :], ln_ref[1:2, :])


def _merge(xt, att, rw_parts, s5_parts, ml_parts, wg, bg, wba, wb, wo, ln, gn, s5_vec, s5_w, tm=256):
    T = xt.shape[0]
    row = lambda n: pl.BlockSpec((tm, n), lambda i: (i, 0))
    const = lambda shape: pl.BlockSpec(shape, lambda i: (0,) * len(shape), pipeline_mode=pl.Buffered(1))
    return pl.pallas_call(
        _merge_body, grid=(T // tm,),
        in_specs=[row(D_MODEL), row(512)] + [row(BRANCH_W)] * 9
        + [const((4, D_MODEL, D_MODEL)), const((4, D_MODEL)), const((512, D_MODEL)), const((3, BRANCH_W, D_MODEL)),
           const((D_MODEL, D_MODEL)), const((2, D_MODEL)), const((2, BRANCH_W)), const((2, BRANCH_W)),
           const((BRANCH_W, BRANCH_W))],
        out_specs=row(D_MODEL), out_shape=jax.ShapeDtypeStruct((T, D_MODEL), f32),
        compiler_params=_cparams(("parallel",)), name="merge")(
            xt, att, *rw_parts, *s5_parts, *ml_parts, wg, bg, wba, wb, wo, ln, gn, s5_vec, s5_w)


def _att_branch_weight(wb):
    z = jnp.zeros((HEAD_DIM, D_MODEL), f32)
    parts = []
    for h in range(ATT_HEADS):
        wh = wb[64 * h:64 * h + 64]
        parts += [wh, z] if h // 2 == 0 else [z, wh]
    return jnp.concatenate(parts, axis=0)


def _ffn_body(x_ref, w1_ref, w3_ref, w2_ref, ln_ref, o_ref, acc_ref):
    j = pl.program_id(1)
    xb = x_ref[...].astype(bf16)
    h1 = jnp.dot(xb, w1_ref[...], preferred_element_type=f32)
    h3 = jnp.dot(xb, w3_ref[...], preferred_element_type=f32)
    part = _bdot(h1 * _sigmoid(h1) * h3, w2_ref[...])

    @pl.when(j == 0)
    def _():
        acc_ref[...] = part

    @pl.when(j > 0)
    def _():
        acc_ref[...] += part

    @pl.when(j == pl.num_programs(1) - 1)
    def _():
        o_ref[...] = _layer_norm(ALPHA * x_ref[...] + acc_ref[...], ln_ref[0:1, :], ln_ref[1:2, :])


def _ffn(xt, w1, w3, w2, ln, tm=512, tf=1408):
    T = xt.shape[0]
    dff = w1.shape[1]
    return pl.pallas_call(
        _ffn_body, grid=(T // tm, dff // tf),
        in_specs=[pl.BlockSpec((tm, D_MODEL), lambda i, j: (i, 0)), pl.BlockSpec((D_MODEL, tf), lambda i, j: (0, j)),
                  pl.BlockSpec((D_MODEL, tf), lambda i, j: (0, j)), pl.BlockSpec((tf, D_MODEL), lambda i, j: (j, 0)),
                  pl.BlockSpec((2, D_MODEL), lambda i, j: (0, 0))],
        out_specs=pl.BlockSpec((tm, D_MODEL), lambda i, j: (i, 0)),
        out_shape=jax.ShapeDtypeStruct((T, D_MODEL), f32), scratch_shapes=[pltpu.VMEM((tm, D_MODEL), f32)],
        compiler_params=_cparams(("parallel", "arbitrary")), name="ffn")(xt, w1, w3, w2, ln)


MOE_TILE = 1024
SC_WINDOW = 128
SC_WORDS = 256


def _pack_words(x):
    bits = lax.bitcast_convert_type(x.astype(bf16).astype(f32), jnp.int32)
    half = D_MODEL // 2
    w = lax.shift_right_logical(bits[:, 0:half], 16) | bits[:, half:D_MODEL]
    return w[:, 0:SC_WORDS], w[:, SC_WORDS:2 * SC_WORDS]


def _unpack_words(wa, wb):
    w = jnp.concatenate([wa, wb], axis=1)
    lo = lax.bitcast_convert_type(lax.shift_left(w, 16), f32)
    hi = lax.bitcast_convert_type(w & jnp.int32(-65536), f32)
    return jnp.concatenate([lo, hi], axis=1)


def _router_body(x_ref, rt_ref, xa_ref, xb_ref, meta_ref, cnt_ref, run_ref):
    tb = x_ref.shape[0]

    @pl.when(pl.program_id(0) == 0)
    def _():
        run_ref[...] = jnp.zeros(run_ref.shape, f32)

    x = x_ref[...]
    xa_ref[...], xb_ref[...] = _pack_words(x)
    logits = _sdot3(x, rt_ref[...])
    lane = lax.broadcasted_iota(jnp.int32, logits.shape, 1)
    lg = jnp.where(lane < N_EXPERTS, logits, NEG)
    v1 = jnp.max(lg, axis=1, keepdims=True)
    i1 = jnp.min(jnp.where(lg == v1, lane, LANES), axis=1, keepdims=True)
    lg2 = jnp.where(lane == i1, NEG, lg)
    v2 = jnp.max(lg2, axis=1, keepdims=True)
    i2 = jnp.min(jnp.where(lg2 == v2, lane, LANES), axis=1, keepdims=True)
    e2 = jnp.exp(v2 - v1)
    sel1, sel2 = lane == i1, lane == i2
    mask = (sel1 | sel2).astype(f32)
    r = lax.broadcasted_iota(jnp.int32, (tb, tb), 0)
    c = lax.broadcasted_iota(jnp.int32, (tb, tb), 1)
    rank = _bdot((c < r).astype(f32), mask) + run_ref[0:1, :]
    run_ref[...] = run_ref[...] + jnp.sum(mask, axis=0, keepdims=True)
    rank1 = jnp.sum(jnp.where(sel1, rank, 0.0), axis=1, keepdims=True)
    rank2 = jnp.sum(jnp.where(sel2, rank, 0.0), axis=1, keepdims=True)
    cols = (i1.astype(f32), i2.astype(f32), rank1, rank2, 1.0 / (1.0 + e2), e2 / (1.0 + e2))
    meta = jnp.zeros(logits.shape, f32)
    for n, col in enumerate(cols):
        meta = jnp.where(lane == n, col, meta)
    meta_ref[...] = meta
    cnt_ref[...] = run_ref[...]


def _router(xt, router, tb=1024):
    T = xt.shape[0]
    tb = min(tb, T)
    return pl.pallas_call(
        _router_body, grid=(T // tb,),
        in_specs=[pl.BlockSpec((tb, D_MODEL), lambda i: (i, 0)), pl.BlockSpec((D_MODEL, LANES), lambda i: (0, 0))],
        out_specs=[pl.BlockSpec((tb, SC_WORDS), lambda i: (i, 0)), pl.BlockSpec((tb, SC_WORDS), lambda i: (i, 0)),
                   pl.BlockSpec((tb, LANES), lambda i: (i, 0)), pl.BlockSpec((8, LANES), lambda i: (0, 0))],
        out_shape=[jax.ShapeDtypeStruct((T, SC_WORDS), jnp.int32), jax.ShapeDtypeStruct((T, SC_WORDS), jnp.int32),
                   jax.ShapeDtypeStruct((T, LANES), f32), jax.ShapeDtypeStruct((8, LANES), f32)],
        scratch_shapes=[pltpu.VMEM((8, LANES), f32)],
        compiler_params=_cparams(("arbitrary",)), name="moe_router")(xt, router)


def _sc_gather(table, idx):
    n = idx.shape[0]
    mesh = plsc.VectorSubcoreMesh(core_axis_name="c", subcore_axis_name="s")

    @functools.partial(pl.kernel, out_type=jax.ShapeDtypeStruct((n, SC_WORDS), table.dtype), mesh=mesh)
    def gather(x_hbm, i_hbm, o_hbm):
        def body(i_vmem, o_vmem):
            pltpu.sync_copy(x_hbm.at[i_vmem.at[0]], o_vmem)

        pltpu.emit_pipeline(
            body, grid=(n // SC_WINDOW,),
            in_specs=[pl.BlockSpec((1, SC_WINDOW), index_map=lambda i: (0, i))],
            out_specs=[pl.BlockSpec((SC_WINDOW, SC_WORDS), index_map=lambda i: (i, 0))],
            core_axis_name="s", dimension_semantics=(pltpu.PARALLEL,))(i_hbm, o_hbm)

    return gather(table, idx.reshape(1, n))


def _sc_scatter(rows, idx, n_out):
    R = rows.shape[0]
    n = idx.shape[0]
    nblk = R // SC_WINDOW
    mesh = plsc.VectorSubcoreMesh(core_axis_name="c", subcore_axis_name="s")

    @functools.partial(pl.kernel, out_type=jax.ShapeDtypeStruct((n_out, SC_WORDS), rows.dtype), mesh=mesh,
                       scratch_types=[])
    def scatter(x_hbm, i_hbm, o_hbm):
        def body(x_vmem, i_vmem):
            pltpu.sync_copy(x_vmem, o_hbm.at[i_vmem.at[0]])

        pltpu.emit_pipeline(
            body, grid=(n // SC_WINDOW,),
            in_specs=[pl.BlockSpec((SC_WINDOW, SC_WORDS), index_map=lambda i: (i % nblk, 0)),
                      pl.BlockSpec((1, SC_WINDOW), index_map=lambda i: (0, i))],
            out_specs=[], core_axis_name="s", dimension_semantics=(pltpu.PARALLEL,))(x_hbm, i_hbm)

    return scatter(rows, idx.reshape(1, n))


def _experts_body(te_ref, nt_ref, xa_ref, xb_ref, w1_ref, w3_ref, w2_ref, oa_ref, ob_ref, acc_ref, x_ref):
    i = pl.program_id(0)
    j = pl.program_id(1)

    @pl.when((i < nt_ref[0]) & (j == 0))
    def _():
        x_ref[...] = _unpack_words(xa_ref[...], xb_ref[...]).astype(bf16)

    @pl.when(i < nt_ref[0])
    def _():
        x = x_ref[...]
        h1 = jnp.dot(x, w1_ref[0].astype(bf16), preferred_element_type=f32)
        h3 = jnp.dot(x, w3_ref[0].astype(bf16), preferred_element_type=f32)
        part = _bdot(h1 * _sigmoid(h1) * h3, w2_ref[0])

        @pl.when(j == 0)
        def _():
            acc_ref[...] = part

        @pl.when(j > 0)
        def _():
            acc_ref[...] += part

    @pl.when(j == pl.num_programs(1) - 1)
    def _():
        oa_ref[...], ob_ref[...] = _pack_words(jnp.where(i < nt_ref[0], acc_ref[...], 0.0))


def _experts(xa, xb, tile_expert, n_tiles, w1, w3, w2, tf=512):
    P = xa.shape[0]
    dff = w1.shape[2]
    words = pl.BlockSpec((MOE_TILE, SC_WORDS), lambda i, j, te, nt: (i, 0))
    grid_spec = pltpu.PrefetchScalarGridSpec(
        num_scalar_prefetch=2, grid=(P // MOE_TILE, dff // tf),
        in_specs=[words, words,
                  pl.BlockSpec((1, D_MODEL, tf), lambda i, j, te, nt: (te[i], 0, j)),
                  pl.BlockSpec((1, D_MODEL, tf), lambda i, j, te, nt: (te[i], 0, j)),
                  pl.BlockSpec((1, tf, D_MODEL), lambda i, j, te, nt: (te[i], j, 0))],
        out_specs=[words, words],
        scratch_shapes=[pltpu.VMEM((MOE_TILE, D_MODEL), f32), pltpu.VMEM((MOE_TILE, D_MODEL), bf16)])
    return pl.pallas_call(
        _experts_body, grid_spec=grid_spec, out_shape=[jax.ShapeDtypeStruct((P, SC_WORDS), jnp.int32)] * 2,
        compiler_params=_cparams(("parallel", "arbitrary")), name="moe_experts")(tile_expert, n_tiles, xa, xb, w1, w3, w2)


def _combine_body(x_ref, y0a_ref, y0b_ref, y1a_ref, y1b_ref, meta_ref, ln_ref, o_ref):
    meta = meta_ref[...]
    ff = (meta[:, 4:5] * _unpack_words(y0a_ref[...], y0b_ref[...])
          + meta[:, 5:6] * _unpack_words(y1a_ref[...], y1b_ref[...]))
    o_ref[...] = _layer_norm(ALPHA * x_ref[...] + ff, ln_ref[0:1, :], ln_ref[1:2, :])


def _combine(xt, yga, ygb, meta, ln, tm=1024):
    T = xt.shape[0]
    tm = min(tm, T)
    nb = T // tm
    row = pl.BlockSpec((tm, D_MODEL), lambda i: (i, 0))
    first = pl.BlockSpec((tm, SC_WORDS), lambda i: (i, 0))
    second = pl.BlockSpec((tm, SC_WORDS), lambda i: (nb + i, 0))
    return pl.pallas_call(
        _combine_body, grid=(nb,),
        in_specs=[row, first, first, second, second, pl.BlockSpec((tm, LANES), lambda i: (i, 0)),
                  pl.BlockSpec((2, D_MODEL), lambda i: (0, 0))],
        out_specs=row, out_shape=jax.ShapeDtypeStruct((T, D_MODEL), f32),
        compiler_params=_cparams(("parallel",)), name="moe_combine")(xt, yga, ygb, yga, ygb, meta, ln)


def _moe(xt, router, w1, w3, w2, ln):
    T = xt.shape[0]
    xa, xb, meta, cnt = _router(xt, router)
    counts = cnt[0, :N_EXPERTS].astype(jnp.int32)
    tiles = (counts + MOE_TILE - 1) // MOE_TILE
    tile_end = jnp.cumsum(tiles)
    offset = (tile_end - tiles) * MOE_TILE
    expert = meta[:, 0:2].astype(jnp.int32)
    onehot = expert[:, :, None] == jnp.arange(N_EXPERTS, dtype=jnp.int32)[None, None, :]
    pos = jnp.sum(jnp.where(onehot, offset[None, None, :], 0), axis=2) + meta[:, 2:4].astype(jnp.int32)
    pos = pos.T.reshape(-1)
    P = 2 * T + N_EXPERTS * MOE_TILE
    tile_id = jnp.arange(P // MOE_TILE, dtype=jnp.int32)
    tile_expert = jnp.minimum(jnp.sum((tile_id[:, None] >= tile_end[None, :]).astype(jnp.int32), axis=1), N_EXPERTS - 1)
    ya, yb = _experts(_sc_scatter(xa, pos, P), _sc_scatter(xb, pos, P), tile_expert, tile_end[-1:], w1, w3, w2)
    return _combine(xt, _sc_gather(ya, pos), _sc_gather(yb, pos), meta, ln)


def kernel(x, w_in, b_in, att_gq, att_gk, rw_mix, rw_w0, rw_w2, rw_a0, rw_a2, rw_g2, rw_kk, rw_ka, rw_rk, rw_ln_g, rw_ln_b, s5_lam_re, s5_lam_im, s5_log_dt, s5_b_re, s5_b_im, s5_c_re, s5_c_im, s5_d, s5_glu_w, s5_glu_b, ml_conv_w, ml_conv_b, ml_ib, ml_fb, w_gate, b_gate, w_branch, w_out, ln1_g, ln1_b, ffn_w1, ffn_w3, ffn_w2, moe_router, moe_w1, moe_w3, moe_w2, ln2_g, ln2_b):
    B, S, D = x.shape
    xt = x.reshape(B * S, D)
    cos, sin = _rope_tables(S)
    for l in range(DEPTH):
        gain = jnp.concatenate([jnp.tile(att_gq[l], 8) * (HEAD_DIM ** -0.5), jnp.tile(att_gk[l], 2)])[None, :]
        (q, k, v, s5u, mq, mk, mv, mo, r, rk, rv, an, bn, lw, gate, bonus, gt) = _proj(
            xt, _proj_params(w_in[l], b_in[l], ml_ib[l], ml_fb[l]), (cos, sin, gain),
            (ml_conv_w[l], ml_conv_b[l][None, :]),
            _rw_params(rw_mix[l], rw_w0[l], rw_w2[l], rw_a0[l], rw_a2[l], rw_g2[l], rw_kk[l], rw_ka[l], rw_rk[l]), B, S)
        score_bound = 8.1 * jnp.max(jnp.abs(att_gq[l])) * jnp.max(jnp.abs(att_gk[l]))
        o_att = _flash(q, k, v, score_bound, B, S)
        yf, yb = _rwkv(r, rk, rv, an, bn, lw, B, S)
        y_s5 = _s5(s5u, _s5_params(s5_lam_re[l], s5_lam_im[l], s5_log_dt[l], s5_b_re[l], s5_b_im[l], s5_c_re[l],
                                   s5_c_im[l]), B, S)
        hf, hb = _mlstm(mq, mk, mv, gt, B, S)
        xt = _merge(xt, o_att, (yf, yb, gate, bonus), (y_s5, s5u), (hf, hb, mo), w_gate[l].astype(bf16), b_gate[l],
                    _att_branch_weight(w_branch[l, 0]).astype(bf16), w_branch[l, 1:].astype(bf16),
                    w_out[l].astype(bf16), jnp.stack([ln1_g[l], ln1_b[l]]), jnp.stack([rw_ln_g[l], rw_ln_b[l]]),
                    jnp.stack([s5_d[l], s5_glu_b[l]]), s5_glu_w[l].astype(bf16))
        ln2 = jnp.stack([ln2_g[l], ln2_b[l]])
        if l % 2 == 0:
            xt = _ffn(xt, ffn_w1[l // 2].astype(bf16), ffn_w3[l // 2].astype(bf16), ffn_w2[l // 2].astype(bf16), ln2)
        else:
            router = jnp.pad(moe_router[l // 2], ((0, 0), (0, LANES - N_EXPERTS)))
            xt = _moe(xt, router, moe_w1[l // 2], moe_w3[l // 2], moe_w2[l // 2], ln2)
    return xt.reshape(B, S, D)
```

```python
import functools
import math

import jax
import jax.numpy as jnp
import numpy as np
from jax import lax
from jax.experimental import pallas as pl
from jax.experimental.pallas import tpu as pltpu
from jax.experimental.pallas import tpu_sc as plsc

f32 = jnp.float32
bf16 = jnp.bfloat16

D_MODEL = 1024
DEPTH = 2
GRID_W = 64
BRANCH_W = 256
HEAD_DIM = 64
ATT_HEADS = 4
ATT_KV_HEADS = 2
ROPE_THETA = 10000.0
QK_EPS = 1e-6
RW_GN_EPS = 64e-5
RW_COLS = 1088
S5_GROUP = 16
S5_GROUPS = 16
S5_STATE = 64
ML_HEADS = 4
N_EXPERTS = 8
ALPHA = (2 * DEPTH) ** 0.25
LN_EPS = 1e-5

LANES = 128
CHUNK = 64
NEG = -1e30
VMEM_LIMIT = 56 * 1024 * 1024

PROJ_SPLITS = (768, 256, 512, 256, 256, RW_COLS)


def _cparams(sem):
    return pltpu.CompilerParams(dimension_semantics=sem, vmem_limit_bytes=VMEM_LIMIT)


def _sigmoid(x):
    return 1.0 / (1.0 + jnp.exp(-x))


def _softplus(x):
    return jnp.maximum(x, 0.0) + jnp.log(1.0 + jnp.exp(-jnp.abs(x)))


def _dims(a, lhs_c, rhs_c):
    lead = a.ndim - 2
    batch = tuple(range(lead))
    return (((lhs_c + lead,), (rhs_c + lead,)), (batch, batch))


def _bdot(a, b):
    return lax.dot_general(a.astype(bf16), b.astype(bf16), _dims(a, 1, 0), preferred_element_type=f32)


def _bdot_nt(a, b):
    return lax.dot_general(a.astype(bf16), b.astype(bf16), _dims(a, 1, 1), preferred_element_type=f32)


def _bdot_tn(a, b):
    return lax.dot_general(a.astype(bf16), b.astype(bf16), _dims(a, 0, 0), preferred_element_type=f32)


def _split(x):
    hi = x.astype(bf16)
    return hi, (x - hi.astype(f32)).astype(bf16)


def _sdot(a, b, exact):
    dims = _dims(a, 1, 0)
    if exact == "rhs":
        hi, lo = _split(a)
        bb = b.astype(bf16)
        return (lax.dot_general(hi, bb, dims, preferred_element_type=f32)
                + lax.dot_general(lo, bb, dims, preferred_element_type=f32))
    hi, lo = _split(b)
    ab = a.astype(bf16)
    return (lax.dot_general(ab, hi, dims, preferred_element_type=f32)
            + lax.dot_general(ab, lo, dims, preferred_element_type=f32))


def _sdot3(a, b):
    dims = _dims(a, 1, 0)
    ah, al = _split(a)
    bh, bl = _split(b)
    return (lax.dot_general(ah, bh, dims, preferred_element_type=f32)
            + lax.dot_general(ah, bl, dims, preferred_element_type=f32)
            + lax.dot_general(al, bh, dims, preferred_element_type=f32))


def _seg_matrix(n, seg=HEAD_DIM):
    r = lax.broadcasted_iota(jnp.int32, (n, n), 0) // seg
    c = lax.broadcasted_iota(jnp.int32, (n, n), 1) // seg
    return (r == c).astype(f32)


def _layer_norm(y, g, b):
    mu = jnp.mean(y, axis=-1, keepdims=True)
    d = y - mu
    var = jnp.mean(d * d, axis=-1, keepdims=True)
    return d * lax.rsqrt(var + LN_EPS) * g + b


def _row_to_col(row, eye):
    return jnp.sum(jnp.where(eye, jnp.broadcast_to(row, eye.shape), 0.0), axis=2, keepdims=True)


def _stack_heads(x):
    h0 = lax.broadcasted_iota(jnp.int32, x.shape, 2) < HEAD_DIM
    return jnp.concatenate([jnp.where(h0, x, 0.0), jnp.where(h0, 0.0, x)], axis=1)


def _proj_body(x_ref, xp_ref, xn_ref, cos_ref, sin_ref, w_ref, b_ref, wg_ref, bg_ref, gain_ref, cw_ref, cb_ref,
               mix_ref, w2_ref, a2_ref, g2_ref, vec_ref,
               q_ref, k_ref, v_ref, s5_ref, mq_ref, mk_ref, mv_ref, mo_ref, r_ref, rk_ref, rv_ref, an_ref, bn_ref,
               lw_ref, gate_ref, bonus_ref, g_ref, mbuf_ref, rbuf_ref):
    tm = x_ref.shape[0]
    xb, xp, xn = x_ref[...].astype(bf16), xp_ref[...].astype(bf16), xn_ref[...].astype(bf16)
    offs = np.cumsum((0,) + PROJ_SPLITS)

    def cols(rows, n):
        sl = slice(int(offs[n]), int(offs[n + 1]))
        return jnp.dot(rows, w_ref[:, sl], preferred_element_type=f32) + b_ref[:, sl]

    q_ref[...], k_ref[...], v_ref[...] = _att_prep(cols(xb, 0), cos_ref[...], sin_ref[...], gain_ref[...])
    s5_ref[...] = cols(xb, 1)
    _fill_halo(mbuf_ref, cols(xb, 2), cols(xp, 2), cols(xn, 2))
    mq_ref[...], mk_ref[...] = _ml_prep(mbuf_ref, tm, cw_ref[...], cb_ref[...])
    mv_ref[...] = cols(xb, 3)
    mo_ref[...] = cols(xb, 4)
    _fill_halo(rbuf_ref, cols(xb, 5), cols(xp, 5), cols(xn, 5))
    outs = _rw_prep(rbuf_ref, tm, mix_ref[...], w2_ref[...], a2_ref[...], g2_ref[...], vec_ref[...])
    for o_ref, val in zip((r_ref, rk_ref, rv_ref, an_ref, bn_ref), outs[0:5]):
        o_ref[...] = val
    lw_ref[0], lw_ref[1], gate_ref[...], bonus_ref[...] = outs[5:9]
    g_ref[...] = lax.dot_general(wg_ref[...], xb, (((1,), (1,)), ((), ())), preferred_element_type=f32) + bg_ref[...]


def _proj(xt, proj_prm, att_prm, ml_prm, rw_prm, B, S, tm=512):
    T = B * S
    nb = S // tm
    n_tot = sum(PROJ_SPLITS)
    row = lambda n: pl.BlockSpec((tm, n), lambda b, i: (b * nb + i, 0))
    const = lambda a: pl.BlockSpec(a.shape, lambda b, i: (0,) * a.ndim)
    tab = pl.BlockSpec((tm, LANES), lambda b, i: (i, 0))
    f32out = lambda n: jax.ShapeDtypeStruct((T, n), f32)
    consts = list(proj_prm) + [att_prm[2]] + list(ml_prm) + list(rw_prm)
    out_specs = ([row(512), row(LANES), row(2 * LANES)] + [row(BRANCH_W)] * 10
                 + [pl.BlockSpec((2, tm, BRANCH_W), lambda b, i: (0, b * nb + i, 0)), row(BRANCH_W), row(BRANCH_W),
                    pl.BlockSpec((16, tm), lambda b, i: (0, b * nb + i))])
    out_shape = ([jax.ShapeDtypeStruct((T, 512), bf16), jax.ShapeDtypeStruct((T, LANES), bf16),
                  jax.ShapeDtypeStruct((T, 2 * LANES), bf16)] + [f32out(BRANCH_W)] * 10
                 + [jax.ShapeDtypeStruct((2, T, BRANCH_W), f32), f32out(BRANCH_W), f32out(BRANCH_W),
                    jax.ShapeDtypeStruct((16, T), f32)])
    return pl.pallas_call(
        _proj_body, grid=(B, nb),
        in_specs=_halo_specs(D_MODEL, tm, B, S) + [tab, tab] + [const(a) for a in consts],
        out_specs=out_specs, out_shape=out_shape,
        scratch_shapes=[pltpu.VMEM((tm + 16, 512), f32), pltpu.VMEM((tm + 16, RW_COLS), f32)],
        compiler_params=_cparams(("parallel", "parallel")), name="proj")(
            xt, xt, xt, att_prm[0], att_prm[1], *consts)


def _proj_params(w_in, b_in, ml_ib, ml_fb):
    o = np.cumsum((0, 256, 128, 128, RW_COLS, 256, 512, 256, 8, 8, 256))
    sl = lambda i: (w_in[:, o[i]:o[i + 1]], b_in[o[i]:o[i + 1]])
    (wq, bq), (wk, bk), (wv, bv), (wrw, brw), (ws5, bs5), (wqk, bqk), (wmv, bmv), (wi, bi), (wf, bf), (wo, bo) = (
        sl(i) for i in range(10))
    zw, zb = jnp.zeros((D_MODEL, HEAD_DIM), f32), jnp.zeros((HEAD_DIM,), f32)
    wq_e, bq_e = [], []
    for h in range(ATT_HEADS):
        wh, bh = wq[:, 64 * h:64 * h + 64], bq[64 * h:64 * h + 64]
        wq_e += [wh, zw] if h // 2 == 0 else [zw, wh]
        bq_e += [bh, zb] if h // 2 == 0 else [zb, bh]
    w = jnp.concatenate(wq_e + [wk, wv, ws5, wqk, wmv, wo, wrw], axis=1)
    b = jnp.concatenate(bq_e + [bk, bv, bs5, bqk, bmv, bo, brw])
    wg = jnp.concatenate([wi, wf], axis=1).T
    bg = jnp.concatenate([bi + ml_ib.reshape(-1), bf + ml_fb.reshape(-1)])
    return w.astype(bf16), b[None, :], wg.astype(bf16), bg[:, None]


def _att_prep(att, cos, sin, gain):
    x = att[:, 0:640]
    ms = _sdot(x * x, _seg_matrix(640), "rhs") * (1.0 / HEAD_DIM)
    xn = x * lax.rsqrt(ms + QK_EPS) * gain
    lane = lax.broadcasted_iota(jnp.int32, xn.shape, 1)
    partner = jnp.where((lane % 32) < 16, pltpu.roll(xn, 640 - 16, 1), pltpu.roll(xn, 16, 1))
    rot = xn * jnp.concatenate([cos] * 5, axis=1) + partner * jnp.concatenate([sin] * 5, axis=1)
    v = jnp.concatenate([att[:, 640:768].astype(bf16), jnp.ones((x.shape[0], LANES), bf16)], axis=1)
    return rot[:, 0:512].astype(bf16), rot[:, 512:640].astype(bf16), v


def _rope_tables(S):
    t = np.arange(S)
    row = (t // GRID_W).astype(np.float32)
    col = (t % GRID_W).astype(np.float32)
    n = 16
    inv = np.power(np.float32(ROPE_THETA), -np.arange(n, dtype=np.float32) / n).astype(np.float32)
    ar = jnp.asarray(row)[:, None] * jnp.asarray(inv)
    ac = jnp.asarray(col)[:, None] * jnp.asarray(inv)
    cos = jnp.concatenate([jnp.cos(ar), jnp.cos(ar), jnp.cos(ac), jnp.cos(ac)], axis=1)
    sin = jnp.concatenate([-jnp.sin(ar), jnp.sin(ar), -jnp.sin(ac), jnp.sin(ac)], axis=1)
    return jnp.concatenate([cos, cos], axis=1), jnp.concatenate([sin, sin], axis=1)


def _flash_body(q_ref, k_ref, v_ref, o_ref, acc_ref, *m_scratch, tk, track_max):
    tq = q_ref.shape[0]
    nk = k_ref.shape[0] // tk
    q2 = jnp.concatenate([q_ref[:, 0:LANES], q_ref[:, LANES:2 * LANES]], axis=0)
    acc_ref[...] = jnp.zeros(acc_ref.shape, f32)
    if track_max:
        m_ref, = m_scratch
        m_ref[...] = jnp.full(m_ref.shape, NEG, f32)

    def step(j, carry):
        rows = pl.ds(pl.multiple_of(j * tk, tk), tk)
        s = lax.dot_general(q2, k_ref[rows, :], (((1,), (1,)), ((), ())), preferred_element_type=f32)
        if track_max:
            m_old = m_ref[...]
            m_new = jnp.maximum(m_old, jnp.max(s, axis=1, keepdims=True))
            p = jnp.exp(s - m_new).astype(bf16)
            acc_ref[...] = jnp.exp(m_old - m_new) * acc_ref[...] + jnp.dot(p, v_ref[rows, :], preferred_element_type=f32)
            m_ref[...] = m_new
        else:
            acc_ref[...] += jnp.dot(jnp.exp(s).astype(bf16), v_ref[rows, :], preferred_element_type=f32)
        return carry

    lax.fori_loop(0, nk, step, 0)
    o = acc_ref[:, 0:LANES] / acc_ref[:, LANES:2 * LANES]
    o_ref[...] = jnp.concatenate([o[0:tq], o[tq:2 * tq]], axis=1)


SCORE_BOUND_MAX = 60.0


def _flash(q, k, v, score_bound, B, S, tq=256, tk=8192):
    T = B * S
    nb = S // tq
    tk = min(tk, S)

    def call(track_max):
        scratch = [pltpu.VMEM((2 * tq, 2 * LANES), f32)] + ([pltpu.VMEM((2 * tq, 1), f32)] if track_max else [])
        return pl.pallas_call(
            functools.partial(_flash_body, tk=tk, track_max=track_max), grid=(B, ATT_KV_HEADS, nb),
            in_specs=[pl.BlockSpec((tq, 2 * LANES), lambda b, g, i: (b * nb + i, g)),
                      pl.BlockSpec((S, LANES), lambda b, g, i: (b, 0)),
                      pl.BlockSpec((S, 2 * LANES), lambda b, g, i: (b, 0))],
            out_specs=pl.BlockSpec((tq, 2 * LANES), lambda b, g, i: (b * nb + i, g)),
            out_shape=jax.ShapeDtypeStruct((T, 512), f32), scratch_shapes=scratch,
            compiler_params=_cparams(("parallel", "parallel", "parallel")),
            name="flash_safe" if track_max else "flash")(q, k, v)

    return lax.cond(score_bound <= SCORE_BOUND_MAX, lambda: call(False), lambda: call(True))


def _halo_specs(width, tm, B, S):
    nb = S // tm
    r8 = tm // 8
    last8 = B * S // 8 - 1

    def main(b, i):
        return (b * nb + i, 0)

    def prev(b, i):
        return (jnp.maximum(b * (S // 8) + i * r8 - 1, 0), 0)

    def nxt(b, i):
        return (jnp.minimum(b * (S // 8) + (i + 1) * r8, last8), 0)

    return [pl.BlockSpec((tm, width), main), pl.BlockSpec((8, width), prev), pl.BlockSpec((8, width), nxt)]


def _fill_halo(buf_ref, x, prev8, next8):
    tm = x.shape[0]
    i = pl.program_id(1)
    last = pl.num_programs(1) - 1
    buf_ref[pl.ds(8, tm), :] = x
    buf_ref[pl.ds(0, 8), :] = jnp.where(i > 0, prev8, 0.0)
    buf_ref[pl.ds(8 + tm, 8), :] = jnp.where(i < last, next8, 0.0)


def _rw_prep(buf_ref, tm, mix, w2, a2, g2, vec):
    x = buf_ref[pl.ds(8, tm), :]
    p = x + mix[0:1, :] * (buf_ref[pl.ds(7, tm), :] - x) + mix[1:2, :] * (buf_ref[pl.ds(9, tm), :] - x)
    r, k, v = p[:, 0:256], p[:, 256:512], p[:, 512:768]
    w0f, w0b, a0, k_k, k_a, r_k = (vec[j:j + 1, :] for j in range(6))
    dec = _bdot(jnp.tanh(p[:, 768:896]), w2)
    z = p[:, 896:1088]
    a = _sigmoid(a0 + _bdot(z, a2))
    gate = _bdot(_sigmoid(z), g2)
    seg = _seg_matrix(BRANCH_W)
    kk = k * k_k
    kk = kk / jnp.maximum(jnp.sqrt(_sdot(kk * kk, seg, "rhs")), 1e-12)
    k2 = k * (1.0 + (a - 1.0) * k_a)
    bonus = _sdot(r * k2 * r_k, seg, "rhs") * v
    lwf = -jnp.exp(-_softplus(-(w0f + dec[:, 0:256])) - 0.5)
    lwb = -jnp.exp(-_softplus(-(w0b + dec[:, 256:512])) - 0.5)
    return r, k2, v, -kk, kk * a, lwf, lwb, gate, bonus


def _scan_sign(shape):
    return jnp.where(lax.broadcasted_iota(jnp.int32, shape, 0) < shape[0] // 2, 1, -1)


def _pair_masks(nbatch):
    n = 2 * CHUNK
    r = lax.broadcasted_iota(jnp.int32, (nbatch, n, n), 1)
    c = lax.broadcasted_iota(jnp.int32, (nbatch, n, n), 2)
    same = (r // CHUNK) == (c // CHUNK)
    d = (c - r) * _scan_sign(r.shape)
    return r, c, same & (d < 0), same & (d <= 0)


def _rw_chunk(st, r, k, v, an, bn, lw):
    L = CHUNK
    N = r.shape[0]
    ri = lax.broadcasted_iota(jnp.int32, (N, L, L), 1)
    ci = lax.broadcasted_iota(jnp.int32, (N, L, L), 2)
    tri = ((ci - ri) * _scan_sign(ri.shape) <= 0).astype(f32)
    cs = _sdot(tri, lw, "lhs")
    tot = jnp.sum(lw, axis=1, keepdims=True)
    e_neg = jnp.exp(-cs)
    a2 = _stack_heads(an * jnp.exp(cs - lw))
    r2 = _stack_heads(r * jnp.exp(cs))
    b2 = _stack_heads(bn * e_neg)
    k2 = _stack_heads(k * e_neg)
    v2 = _stack_heads(v)
    rr, cc, strict, incl = _pair_masks(N)
    n2 = 2 * L
    g = _bdot_nt(jnp.concatenate([a2, r2], axis=1), jnp.concatenate([b2, k2], axis=1))
    mab = jnp.where(strict, g[:, 0:n2, 0:n2], 0.0)
    mak = jnp.where(strict, g[:, 0:n2, n2:2 * n2], 0.0)
    pb = jnp.where(incl, g[:, n2:2 * n2, 0:n2], 0.0)
    pk = jnp.where(incl, g[:, n2:2 * n2, n2:2 * n2], 0.0)
    eye = rr == cc
    m8 = jnp.where((rr // 8) == (cc // 8), mab, 0.0)
    x = eye.astype(f32) + m8
    p = _bdot(m8, m8)
    x = x + _bdot(x, p)
    p = _bdot(p, p)
    x = x + _bdot(x, p)
    n = 8
    while n < L:
        e = jnp.where(((rr // (2 * n)) == (cc // (2 * n))) & ((rr // n) != (cc // n)), mab, 0.0)
        x = x + _bdot(_bdot(x, e), x)
        n *= 2
    wu = _bdot(x, jnp.concatenate([a2, _bdot(mak, v2)], axis=2))
    pwu = _bdot(pb, wu)
    rh = r2 + pwu[:, :, 0:LANES]
    y2 = pwu[:, :, LANES:2 * LANES] + _bdot(jnp.concatenate([pk, rh], axis=2), jnp.concatenate([v2, st], axis=1))
    y = y2[:, 0:L] + y2[:, L:2 * L]
    gam = _row_to_col(jnp.exp(tot), eye)
    bwu = _bdot_tn(b2, wu)
    st = gam * (st + _bdot(bwu[:, :, 0:LANES], st) + bwu[:, :, LANES:2 * LANES] + _bdot_tn(k2, v2))
    return y, st


def _load_pairs(ref, rows):
    return jnp.concatenate([ref[:, rows, 0:LANES], ref[:, rows, LANES:2 * LANES]], axis=0)


def _store_pairs(ref, rows, y):
    nb = ref.shape[0]
    ref[:, rows, 0:LANES] = y[0:nb]
    ref[:, rows, LANES:2 * LANES] = y[nb:2 * nb]


def _load_both(f_ref, b_ref, rows_f, rows_b):
    return jnp.concatenate([_load_pairs(f_ref, rows_f), _load_pairs(b_ref, rows_b)], axis=0)


def _chunk_rows(cc, nch):
    return (pl.ds(pl.multiple_of(cc * CHUNK, CHUNK), CHUNK),
            pl.ds(pl.multiple_of((nch - 1 - cc) * CHUNK, CHUNK), CHUNK))


def _rw_scan_body(rf_ref, rb_ref, kf_ref, kb_ref, vf_ref, vb_ref, anf_ref, anb_ref, bnf_ref, bnb_ref, lwf_ref, lwb_ref,
                  yf_ref, yb_ref, st_ref):
    nch = rf_ref.shape[1] // CHUNK
    half = st_ref.shape[0] // 2

    @pl.when(pl.program_id(0) == 0)
    def _():
        st_ref[...] = jnp.zeros(st_ref.shape, f32)

    def step(cc, carry):
        rows_f, rows_b = _chunk_rows(cc, nch)
        pairs = ((rf_ref, rb_ref), (kf_ref, kb_ref), (vf_ref, vb_ref), (anf_ref, anb_ref), (bnf_ref, bnb_ref),
                 (lwf_ref.at[0], lwb_ref.at[0]))
        y, st = _rw_chunk(st_ref[...], *(_load_both(f, b, rows_f, rows_b) for f, b in pairs))
        _store_pairs(yf_ref, rows_f, y[0:half])
        _store_pairs(yb_ref, rows_b, y[half:2 * half])
        st_ref[...] = st
        return carry

    lax.fori_loop(0, nch, step, 0)


def _rw_scan(r, k, v, an, bn, lw, B, S, ts=128):
    nb = S // ts
    fwd = pl.BlockSpec((B, ts, BRANCH_W), lambda i: (0, i, 0))
    bwd = pl.BlockSpec((B, ts, BRANCH_W), lambda i: (0, nb - 1 - i, 0))
    out = jax.ShapeDtypeStruct((B, S, BRANCH_W), f32)
    return pl.pallas_call(
        _rw_scan_body, grid=(nb,),
        in_specs=[fwd, bwd] * 5 + [pl.BlockSpec((1, B, ts, BRANCH_W), lambda i: (0, 0, i, 0)),
                                   pl.BlockSpec((1, B, ts, BRANCH_W), lambda i: (1, 0, nb - 1 - i, 0))],
        out_specs=[fwd, bwd], out_shape=[out, out],
        scratch_shapes=[pltpu.VMEM((2 * B * BRANCH_W // LANES, LANES, LANES), f32)],
        compiler_params=_cparams(("arbitrary",)), name="rw_scan")(r, r, k, k, v, v, an, an, bn, bn, lw, lw)


def _rw_finish(y, gate, bonus, gn):
    seg = _seg_matrix(BRANCH_W)
    mu = _sdot(y, seg, "rhs") * (1.0 / HEAD_DIM)
    d = y - mu
    var = _sdot(d * d, seg, "rhs") * (1.0 / HEAD_DIM)
    yn = d * lax.rsqrt(var + RW_GN_EPS) * gn[0:1, :] + gn[1:2, :]
    return (yn + bonus) * gate


def _rwkv(r, k, v, an, bn, lw, B, S):
    seq = [t.reshape(B, S, BRANCH_W) for t in (r, k, v, an, bn)] + [lw.reshape(2, B, S, BRANCH_W)]
    yf, yb = _rw_scan(*seq, B, S)
    return yf.reshape(B * S, BRANCH_W), yb.reshape(B * S, BRANCH_W)


def _rw_params(mix, w0, w2, a0, a2, g2, k_k, k_a, r_k):
    z = jnp.zeros((64, 256), f32)
    w2c = jnp.concatenate([jnp.concatenate([w2[0], z], axis=1), jnp.concatenate([z, w2[1]], axis=1)], axis=0)
    a2p = jnp.concatenate([a2, jnp.zeros((128, 256), f32)], axis=0)
    g2p = jnp.concatenate([jnp.zeros((64, 256), f32), g2], axis=0)
    vec = jnp.stack([w0[0], w0[1], a0, k_k, k_a, r_k.reshape(-1), jnp.zeros_like(a0), jnp.zeros_like(a0)])
    return mix, w2c.astype(bf16), a2p.astype(bf16), g2p.astype(bf16), vec


def _s5_scan_body(u_ref, bcat_ref, cre_ref, cim_ref, lre_ref, lim_ref, y_ref, sre_ref, sim_ref, bre_ref, bim_ref):
    ts = u_ref.shape[0]
    n = S5_GROUPS * S5_STATE

    @pl.when(pl.program_id(0) == 0)
    def _():
        sre_ref[...] = jnp.zeros(sre_ref.shape, f32)
        sim_ref[...] = jnp.zeros(sim_ref.shape, f32)

    u = u_ref[...].reshape(ts * 8, BRANCH_W).astype(bf16)
    fwd = lax.broadcasted_iota(jnp.int32, (ts, 8, n), 1) < 4

    def bu(part):
        return jnp.dot(u, bcat_ref[:, part * n:(part + 1) * n], preferred_element_type=f32).reshape(ts, 8, n)

    bre_ref[...] = jnp.where(fwd, bu(0), bu(2))
    bim_ref[...] = jnp.where(fwd, bu(1), bu(3))
    lre = lre_ref[...]
    lim = lim_ref[...]

    def step(t, carry):
        sre, sim = carry
        nre = lre * sre - lim * sim + bre_ref[t]
        nim = lre * sim + lim * sre + bim_ref[t]
        bre_ref[t] = nre
        bim_ref[t] = nim
        return nre, nim

    sre, sim = lax.fori_loop(0, ts, step, (sre_ref[...], sim_ref[...]), unroll=4)
    sre_ref[...] = sre
    sim_ref[...] = sim
    y = (_bdot(bre_ref[...].reshape(ts * 8, n), cre_ref[...])
         - _bdot(bim_ref[...].reshape(ts * 8, n), cim_ref[...])).reshape(ts, 8, 2 * BRANCH_W)
    fwd_o = lax.broadcasted_iota(jnp.int32, (ts, 8, BRANCH_W), 1) < 4
    y_ref[...] = jnp.where(fwd_o, y[:, :, 0:BRANCH_W], y[:, :, BRANCH_W:2 * BRANCH_W])


def _s5_scan(u2, bcat, cre, cim, lre, lim, ts=128):
    S = u2.shape[0]
    n = S5_GROUPS * S5_STATE
    full = lambda shape: pl.BlockSpec(shape, lambda i: (0,) * len(shape))
    return pl.pallas_call(
        _s5_scan_body, grid=(S // ts,),
        in_specs=[pl.BlockSpec((ts, 8, BRANCH_W), lambda i: (i, 0, 0)), full((BRANCH_W, 4 * n)),
                  full((n, 2 * BRANCH_W)), full((n, 2 * BRANCH_W)), full((8, n)), full((8, n))],
        out_specs=pl.BlockSpec((ts, 8, BRANCH_W), lambda i: (i, 0, 0)),
        out_shape=jax.ShapeDtypeStruct((S, 8, BRANCH_W), f32),
        scratch_shapes=[pltpu.VMEM((8, n), f32), pltpu.VMEM((8, n), f32),
                        pltpu.VMEM((ts, 8, n), f32), pltpu.VMEM((ts, 8, n), f32)],
        compiler_params=_cparams(("arbitrary",)), name="s5_scan")(u2, bcat, cre, cim, lre, lim)


def _s5_finish(y, u, vec, w):
    y = y + u * vec[0:1, :]
    y = 0.5 * y * (1.0 + jnp.tanh(math.sqrt(2.0 / math.pi) * (y + 0.044715 * (y * y * y))))
    return y * _sigmoid(_bdot(y, w) + vec[1:2, :])


def _s5_params(lam_re, lam_im, log_dt, b_re, b_im, c_re, c_im):
    G, P, C = S5_GROUPS, S5_STATE, S5_GROUP
    eye = jnp.eye(G, dtype=f32)
    b_c = lax.complex(b_re, b_im)
    bcat, cre, cim, lre, lim = [], [], [], [], []
    for d in range(2):
        lam = lax.complex(jnp.minimum(lam_re[d], -1e-4), lam_im[d])
        lam_bar = jnp.exp(lam * jnp.exp(log_dt[d])[:, None])
        b_bar = ((lam_bar - 1.0) / lam)[..., None] * b_c
        for part in (jnp.real(b_bar), jnp.imag(b_bar)):
            bcat.append(jnp.einsum('gh,gpc->gchp', eye, part).reshape(G * C, G * P))
        cre.append(jnp.einsum('gh,gcp->gphc', eye, c_re[d]).reshape(G * P, G * C))
        cim.append(jnp.einsum('gh,gcp->gphc', eye, c_im[d]).reshape(G * P, G * C))
        lre.append(jnp.broadcast_to(jnp.real(lam_bar).reshape(1, G * P), (4, G * P)))
        lim.append(jnp.broadcast_to(jnp.imag(lam_bar).reshape(1, G * P), (4, G * P)))
    return (jnp.concatenate(bcat, axis=1).astype(bf16), jnp.concatenate(cre, axis=1).astype(bf16),
            jnp.concatenate(cim, axis=1).astype(bf16), jnp.concatenate(lre, axis=0), jnp.concatenate(lim, axis=0))


def _s5(u, prm, B, S):
    u3 = u.reshape(B, S, BRANCH_W).transpose(1, 0, 2)
    u2 = jnp.concatenate([u3, u3[::-1]], axis=1)
    y2 = _s5_scan(u2, *prm)
    return (y2[:, 0:B] + y2[::-1, B:2 * B]).transpose(1, 0, 2).reshape(B * S, BRANCH_W)


def _ml_prep(buf_ref, tm, w, b):
    y = b
    for j in range(5):
        y = y + w[j:j + 1, :] * buf_ref[pl.ds(6 + j, tm), :]
    y = y * _sigmoid(y)
    return y[:, 0:BRANCH_W], y[:, BRANCH_W:2 * BRANCH_W] * (HEAD_DIM ** -0.5)


def _ml_chunk(state, q, k, v, li, lfp):
    cn, m_row = state
    L = CHUNK
    N = q.shape[0]
    rr, cc, _, incl = _pair_masks(N)
    same = (rr // L) == (cc // L)
    eye = rr == cc
    lane = lax.broadcasted_iota(jnp.int32, (N, 1, 2 * L), 2)
    lf = jnp.minimum(lfp, 0.0) - jnp.log(1.0 + jnp.exp(-jnp.abs(lfp)))
    lf8 = jnp.broadcast_to(lf, (N, 8, 2 * L))
    cum = (same & ((rr - cc) * _scan_sign(rr.shape) <= 0)).astype(f32)
    b_row = _sdot(lf8, cum, "rhs")[:, 0:1]
    g_row = _sdot(lf8, same.astype(f32), "rhs")[:, 0:1]
    w_end = g_row - b_row + li
    m0 = jnp.max(jnp.where(lane < L, w_end, NEG), axis=2, keepdims=True)
    m1 = jnp.max(jnp.where(lane < L, NEG, w_end), axis=2, keepdims=True)
    m_loc = jnp.where(lane < L, m0, m1)
    e_col = _row_to_col(jnp.exp(w_end - m_loc), eye)
    b_col = _row_to_col(b_row, eye)
    q2, k2, v2 = _stack_heads(q), _stack_heads(k), _stack_heads(v)
    v1 = jnp.concatenate([v2, _stack_heads(jnp.ones_like(v))], axis=2)
    log_inter = b_col + _row_to_col(m_row, eye)
    log_intra = jnp.where(incl, b_col - b_row + li, NEG)
    m_r = jnp.maximum(log_inter, jnp.max(log_intra, axis=2, keepdims=True))
    s = _bdot_nt(q2, k2) * jnp.exp(log_intra - m_r)
    inter = jnp.exp(log_inter - m_r)
    nd = _bdot(s, v1) + inter * _bdot(q2, cn)
    h2 = nd[:, :, 0:LANES] / jnp.maximum(jnp.abs(nd[:, :, LANES:2 * LANES]), jnp.exp(-m_r))
    h = h2[:, 0:L] + h2[:, L:2 * L]
    m_new = jnp.maximum(g_row + m_row, m_loc)
    a = jnp.exp(g_row + m_row - m_new)
    bb = jnp.exp(m_loc - m_new)
    cn = jnp.concatenate([a, a], axis=2) * cn + jnp.concatenate([bb, bb], axis=2) * _bdot_tn(e_col * k2, v1)
    return h, (cn, m_new)


GATE_ROWS = 8


def _ml_scan_body(qf_ref, qb_ref, kf_ref, kb_ref, vf_ref, vb_ref, gf_ref, gb_ref, hf_ref, hb_ref, c_ref, m_ref):
    nch = qf_ref.shape[1] // CHUNK
    half = c_ref.shape[0] // 2
    i = pl.program_id(0)
    nb = pl.num_programs(0)
    per = GATE_ROWS // nch
    base_f = (i % per) * nch
    base_b = ((nb - 1 - i) % per) * nch

    @pl.when(i == 0)
    def _():
        c_ref[...] = jnp.zeros(c_ref.shape, f32)
        m_ref[...] = jnp.zeros(m_ref.shape, f32)

    def step(cc, carry):
        rows_f, rows_b = _chunk_rows(cc, nch)
        gate = lambda t: jnp.concatenate(
            [gf_ref[t, 0, 0, :, pl.ds(base_f + cc, 1), :], gf_ref[t, 0, 1, :, pl.ds(base_f + cc, 1), :],
             gb_ref[t, 0, 0, :, pl.ds(base_b + nch - 1 - cc, 1), :], gb_ref[t, 0, 1, :, pl.ds(base_b + nch - 1 - cc, 1), :]],
            axis=0)
        h, (cn, m_row) = _ml_chunk((c_ref[...], m_ref[...]), _load_both(qf_ref, qb_ref, rows_f, rows_b),
                                   _load_both(kf_ref, kb_ref, rows_f, rows_b), _load_both(vf_ref, vb_ref, rows_f, rows_b),
                                   gate(0), gate(1))
        _store_pairs(hf_ref, rows_f, h[0:half])
        _store_pairs(hb_ref, rows_b, h[half:2 * half])
        c_ref[...] = cn
        m_ref[...] = m_row
        return carry

    lax.fori_loop(0, nch, step, 0)


def _ml_scan(q, k, v, g, B, S, ts=256):
    nb = S // ts
    per = GATE_ROWS * CHUNK // ts
    nchain = 2 * B * BRANCH_W // LANES
    fwd = pl.BlockSpec((B, ts, BRANCH_W), lambda i: (0, i, 0))
    bwd = pl.BlockSpec((B, ts, BRANCH_W), lambda i: (0, nb - 1 - i, 0))
    out = jax.ShapeDtypeStruct((B, S, BRANCH_W), f32)
    return pl.pallas_call(
        _ml_scan_body, grid=(nb,),
        in_specs=[fwd, bwd] * 3 + [pl.BlockSpec((2, 1, 2, B, GATE_ROWS, LANES), lambda i: (0, 0, 0, 0, i // per, 0)),
                                   pl.BlockSpec((2, 1, 2, B, GATE_ROWS, LANES),
                                                lambda i: (0, 1, 0, 0, (nb - 1 - i) // per, 0))],
        out_specs=[fwd, bwd], out_shape=[out, out],
        scratch_shapes=[pltpu.VMEM((nchain, LANES, 2 * LANES), f32), pltpu.VMEM((nchain, 1, LANES), f32)],
        compiler_params=_cparams(("arbitrary",)), name="ml_scan")(q, q, k, k, v, v, g, g)


def _mlstm(q, k, mv, gt, B, S):
    g = gt.reshape(2, 2, 2, 2, B, S // CHUNK, CHUNK).transpose(0, 1, 2, 4, 5, 3, 6).reshape(2, 2, 2, B, S // CHUNK, LANES)
    hf, hb = _ml_scan(*(t.reshape(B, S, BRANCH_W) for t in (q, k, mv)), g, B, S)
    return hf.reshape(B * S, BRANCH_W), hb.reshape(B * S, BRANCH_W)


def _merge_body(x_ref, att_ref, yf_ref, yb_ref, rg_ref, rb_ref, sy_ref, su_ref, hf_ref, hb_ref, mo_ref,
                wg_ref, bg_ref, wba_ref, wb_ref, wo_ref, ln_ref, gn_ref, sv_ref, sw_ref, o_ref):
    x = x_ref[...]
    xb = x.astype(bf16)
    rw = _rw_finish(yf_ref[...] + yb_ref[...], rg_ref[...], rb_ref[...], gn_ref[...])
    s5 = _s5_finish(sy_ref[...], su_ref[...], sv_ref[...], sw_ref[...])
    ml = _sigmoid(mo_ref[...]) * (hf_ref[...] + hb_ref[...])
    branches = (att_ref[...], rw, s5, ml)
    merged = None
    for n in range(4):
        gate = _sigmoid(jnp.dot(xb, wg_ref[n], preferred_element_type=f32) + bg_ref[n:n + 1, :])
        wide = _bdot(branches[n], wba_ref[...] if n == 0 else wb_ref[n - 1])
        merged = gate * wide if merged is None else merged + gate * wide
    y = ALPHA * x + _bdot(merged, wo_ref[...])
    o_ref[...] = _layer_norm(y, ln_ref[0:1, :], ln_ref[1:2, :])


def _merge(xt, att, rw_parts, s5_parts, ml_parts, wg, bg, wba, wb, wo, ln, gn, s5_vec, s5_w, tm=256):
    T = xt.shape[0]
    row = lambda n: pl.BlockSpec((tm, n), lambda i: (i, 0))
    const = lambda shape: pl.BlockSpec(shape, lambda i: (0,) * len(shape), pipeline_mode=pl.Buffered(1))
    return pl.pallas_call(
        _merge_body, grid=(T // tm,),
        in_specs=[row(D_MODEL), row(512)] + [row(BRANCH_W)] * 9
        + [const((4, D_MODEL, D_MODEL)), const((4, D_MODEL)), const((512, D_MODEL)), const((3, BRANCH_W, D_MODEL)),
           const((D_MODEL, D_MODEL)), const((2, D_MODEL)), const((2, BRANCH_W)), const((2, BRANCH_W)),
           const((BRANCH_W, BRANCH_W))],
        out_specs=row(D_MODEL), out_shape=jax.ShapeDtypeStruct((T, D_MODEL), f32),
        compiler_params=_cparams(("parallel",)), name="merge")(
            xt, att, *rw_parts, *s5_parts, *ml_parts, wg, bg, wba, wb, wo, ln, gn, s5_vec, s5_w)


def _att_branch_weight(wb):
    z = jnp.zeros((HEAD_DIM, D_MODEL), f32)
    parts = []
    for h in range(ATT_HEADS):
        wh = wb[64 * h:64 * h + 64]
        parts += [wh, z] if h // 2 == 0 else [z, wh]
    return jnp.concatenate(parts, axis=0)


def _ffn_body(x_ref, w1_ref, w3_ref, w2_ref, ln_ref, o_ref, acc_ref):
    j = pl.program_id(1)

    @pl.when(j == 0)
    def _():
        acc_ref[...] = jnp.zeros(acc_ref.shape, f32)

    xb = x_ref[...].astype(bf16)
    h1 = jnp.dot(xb, w1_ref[...], preferred_element_type=f32)
    h3 = jnp.dot(xb, w3_ref[...], preferred_element_type=f32)
    acc_ref[...] += _bdot(h1 * _sigmoid(h1) * h3, w2_ref[...])

    @pl.when(j == pl.num_programs(1) - 1)
    def _():
        o_ref[...] = _layer_norm(ALPHA * x_ref[...] + acc_ref[...], ln_ref[0:1, :], ln_ref[1:2, :])


def _ffn(xt, w1, w3, w2, ln, tm=512, tf=1408):
    T = xt.shape[0]
    dff = w1.shape[1]
    return pl.pallas_call(
        _ffn_body, grid=(T // tm, dff // tf),
        in_specs=[pl.BlockSpec((tm, D_MODEL), lambda i, j: (i, 0)), pl.BlockSpec((D_MODEL, tf), lambda i, j: (0, j)),
                  pl.BlockSpec((D_MODEL, tf), lambda i, j: (0, j)), pl.BlockSpec((tf, D_MODEL), lambda i, j: (j, 0)),
                  pl.BlockSpec((2, D_MODEL), lambda i, j: (0, 0))],
        out_specs=pl.BlockSpec((tm, D_MODEL), lambda i, j: (i, 0)),
        out_shape=jax.ShapeDtypeStruct((T, D_MODEL), f32), scratch_shapes=[pltpu.VMEM((tm, D_MODEL), f32)],
        compiler_params=_cparams(("parallel", "arbitrary")), name="ffn")(xt, w1, w3, w2, ln)


MOE_TILE = 1024
SC_WINDOW = 128
SC_WORDS = 256


def _pack_words(x):
    bits = lax.bitcast_convert_type(x.astype(bf16).astype(f32), jnp.int32)
    half = D_MODEL // 2
    w = lax.shift_right_logical(bits[:, 0:half], 16) | bits[:, half:D_MODEL]
    return w[:, 0:SC_WORDS], w[:, SC_WORDS:2 * SC_WORDS]


def _unpack_words(wa, wb):
    w = jnp.concatenate([wa, wb], axis=1)
    lo = lax.bitcast_convert_type(lax.shift_left(w, 16), f32)
    hi = lax.bitcast_convert_type(w & jnp.int32(-65536), f32)
    return jnp.concatenate([lo, hi], axis=1)


def _router_body(x_ref, rt_ref, xa_ref, xb_ref, meta_ref, cnt_ref, run_ref):
    tb = x_ref.shape[0]

    @pl.when(pl.program_id(0) == 0)
    def _():
        run_ref[...] = jnp.zeros(run_ref.shape, f32)

    x = x_ref[...]
    xa_ref[...], xb_ref[...] = _pack_words(x)
    logits = _sdot3(x, rt_ref[...])
    lane = lax.broadcasted_iota(jnp.int32, logits.shape, 1)
    lg = jnp.where(lane < N_EXPERTS, logits, NEG)
    v1 = jnp.max(lg, axis=1, keepdims=True)
    i1 = jnp.min(jnp.where(lg == v1, lane, LANES), axis=1, keepdims=True)
    lg2 = jnp.where(lane == i1, NEG, lg)
    v2 = jnp.max(lg2, axis=1, keepdims=True)
    i2 = jnp.min(jnp.where(lg2 == v2, lane, LANES), axis=1, keepdims=True)
    e2 = jnp.exp(v2 - v1)
    sel1, sel2 = lane == i1, lane == i2
    mask = (sel1 | sel2).astype(f32)
    r = lax.broadcasted_iota(jnp.int32, (tb, tb), 0)
    c = lax.broadcasted_iota(jnp.int32, (tb, tb), 1)
    rank = _bdot((c < r).astype(f32), mask) + run_ref[0:1, :]
    run_ref[...] = run_ref[...] + jnp.sum(mask, axis=0, keepdims=True)
    rank1 = jnp.sum(jnp.where(sel1, rank, 0.0), axis=1, keepdims=True)
    rank2 = jnp.sum(jnp.where(sel2, rank, 0.0), axis=1, keepdims=True)
    cols = (i1.astype(f32), i2.astype(f32), rank1, rank2, 1.0 / (1.0 + e2), e2 / (1.0 + e2))
    meta = jnp.zeros(logits.shape, f32)
    for n, col in enumerate(cols):
        meta = jnp.where(lane == n, col, meta)
    meta_ref[...] = meta
    cnt_ref[...] = run_ref[...]


def _router(xt, router, tb=1024):
    T = xt.shape[0]
    tb = min(tb, T)
    return pl.pallas_call(
        _router_body, grid=(T // tb,),
        in_specs=[pl.BlockSpec((tb, D_MODEL), lambda i: (i, 0)), pl.BlockSpec((D_MODEL, LANES), lambda i: (0, 0))],
        out_specs=[pl.BlockSpec((tb, SC_WORDS), lambda i: (i, 0)), pl.BlockSpec((tb, SC_WORDS), lambda i: (i, 0)),
                   pl.BlockSpec((tb, LANES), lambda i: (i, 0)), pl.BlockSpec((8, LANES), lambda i: (0, 0))],
        out_shape=[jax.ShapeDtypeStruct((T, SC_WORDS), jnp.int32), jax.ShapeDtypeStruct((T, SC_WORDS), jnp.int32),
                   jax.ShapeDtypeStruct((T, LANES), f32), jax.ShapeDtypeStruct((8, LANES), f32)],
        scratch_shapes=[pltpu.VMEM((8, LANES), f32)],
        compiler_params=_cparams(("arbitrary",)), name="moe_router")(xt, router)


def _sc_gather(table, idx):
    n = idx.shape[0]
    mesh = plsc.VectorSubcoreMesh(core_axis_name="c", subcore_axis_name="s")

    @functools.partial(pl.kernel, out_type=jax.ShapeDtypeStruct((n, SC_WORDS), table.dtype), mesh=mesh)
    def gather(x_hbm, i_hbm, o_hbm):
        def body(i_vmem, o_vmem):
            pltpu.sync_copy(x_hbm.at[i_vmem.at[0]], o_vmem)

        pltpu.emit_pipeline(
            body, grid=(n // SC_WINDOW,),
            in_specs=[pl.BlockSpec((1, SC_WINDOW), index_map=lambda i: (0, i))],
            out_specs=[pl.BlockSpec((SC_WINDOW, SC_WORDS), index_map=lambda i: (i, 0))],
            core_axis_name=("c", "s"), dimension_semantics=(pltpu.PARALLEL,))(i_hbm, o_hbm)

    return gather(table, idx.reshape(1, n))


def _sc_scatter(rows, idx, n_out):
    R = rows.shape[0]
    n = idx.shape[0]
    nblk = R // SC_WINDOW
    mesh = plsc.VectorSubcoreMesh(core_axis_name="c", subcore_axis_name="s")

    @functools.partial(pl.kernel, out_type=jax.ShapeDtypeStruct((n_out, SC_WORDS), rows.dtype), mesh=mesh,
                       scratch_types=[])
    def scatter(x_hbm, i_hbm, o_hbm):
        def body(x_vmem, i_vmem):
            pltpu.sync_copy(x_vmem, o_hbm.at[i_vmem.at[0]])

        pltpu.emit_pipeline(
            body, grid=(n // SC_WINDOW,),
            in_specs=[pl.BlockSpec((SC_WINDOW, SC_WORDS), index_map=lambda i: (i % nblk, 0)),
                      pl.BlockSpec((1, SC_WINDOW), index_map=lambda i: (0, i))],
            out_specs=[], core_axis_name=("c", "s"), dimension_semantics=(pltpu.PARALLEL,))(x_hbm, i_hbm)

    return scatter(rows, idx.reshape(1, n))


def _experts_body(te_ref, nt_ref, xa_ref, xb_ref, w1_ref, w3_ref, w2_ref, oa_ref, ob_ref, acc_ref, x_ref):
    i = pl.program_id(0)
    j = pl.program_id(1)

    @pl.when(j == 0)
    def _():
        x_ref[...] = _unpack_words(xa_ref[...], xb_ref[...]).astype(bf16)
        acc_ref[...] = jnp.zeros(acc_ref.shape, f32)

    @pl.when(i < nt_ref[0])
    def _():
        x = x_ref[...]
        h1 = jnp.dot(x, w1_ref[0].astype(bf16), preferred_element_type=f32)
        h3 = jnp.dot(x, w3_ref[0].astype(bf16), preferred_element_type=f32)
        acc_ref[...] += _bdot(h1 * _sigmoid(h1) * h3, w2_ref[0])

    @pl.when(j == pl.num_programs(1) - 1)
    def _():
        oa_ref[...], ob_ref[...] = _pack_words(acc_ref[...])


def _experts(xa, xb, tile_expert, n_tiles, w1, w3, w2, tf=512):
    P = xa.shape[0]
    dff = w1.shape[2]
    words = pl.BlockSpec((MOE_TILE, SC_WORDS), lambda i, j, te, nt: (i, 0))
    grid_spec = pltpu.PrefetchScalarGridSpec(
        num_scalar_prefetch=2, grid=(P // MOE_TILE, dff // tf),
        in_specs=[words, words,
                  pl.BlockSpec((1, D_MODEL, tf), lambda i, j, te, nt: (te[i], 0, j)),
                  pl.BlockSpec((1, D_MODEL, tf), lambda i, j, te, nt: (te[i], 0, j)),
                  pl.BlockSpec((1, tf, D_MODEL), lambda i, j, te, nt: (te[i], j, 0))],
        out_specs=[words, words],
        scratch_shapes=[pltpu.VMEM((MOE_TILE, D_MODEL), f32), pltpu.VMEM((MOE_TILE, D_MODEL), bf16)])
    return pl.pallas_call(
        _experts_body, grid_spec=grid_spec, out_shape=[jax.ShapeDtypeStruct((P, SC_WORDS), jnp.int32)] * 2,
        compiler_params=_cparams(("parallel", "arbitrary")), name="moe_experts")(tile_expert, n_tiles, xa, xb, w1, w3, w2)


def _combine_body(x_ref, y0a_ref, y0b_ref, y1a_ref, y1b_ref, meta_ref, ln_ref, o_ref):
    meta = meta_ref[...]
    ff = (meta[:, 4:5] * _unpack_words(y0a_ref[...], y0b_ref[...])
          + meta[:, 5:6] * _unpack_words(y1a_ref[...], y1b_ref[...]))
    o_ref[...] = _layer_norm(ALPHA * x_ref[...] + ff, ln_ref[0:1, :], ln_ref[1:2, :])


def _combine(xt, yga, ygb, meta, ln, tm=1024):
    T = xt.shape[0]
    tm = min(tm, T)
    nb = T // tm
    row = pl.BlockSpec((tm, D_MODEL), lambda i: (i, 0))
    first = pl.BlockSpec((tm, SC_WORDS), lambda i: (i, 0))
    second = pl.BlockSpec((tm, SC_WORDS), lambda i: (nb + i, 0))
    return pl.pallas_call(
        _combine_body, grid=(nb,),
        in_specs=[row, first, first, second, second, pl.BlockSpec((tm, LANES), lambda i: (i, 0)),
                  pl.BlockSpec((2, D_MODEL), lambda i: (0, 0))],
        out_specs=row, out_shape=jax.ShapeDtypeStruct((T, D_MODEL), f32),
        compiler_params=_cparams(("parallel",)), name="moe_combine")(xt, yga, ygb, yga, ygb, meta, ln)


def _moe(xt, router, w1, w3, w2, ln):
    T = xt.shape[0]
    xa, xb, meta, cnt = _router(xt, router)
    counts = cnt[0, :N_EXPERTS].astype(jnp.int32)
    tiles = (counts + MOE_TILE - 1) // MOE_TILE
    tile_end = jnp.cumsum(tiles)
    offset = (tile_end - tiles) * MOE_TILE
    expert = meta[:, 0:2].astype(jnp.int32)
    onehot = expert[:, :, None] == jnp.arange(N_EXPERTS, dtype=jnp.int32)[None, None, :]
    pos = jnp.sum(jnp.where(onehot, offset[None, None, :], 0), axis=2) + meta[:, 2:4].astype(jnp.int32)
    pos = pos.T.reshape(-1)
    P = 2 * T + N_EXPERTS * MOE_TILE
    tile_id = jnp.arange(P // MOE_TILE, dtype=jnp.int32)
    tile_expert = jnp.minimum(jnp.sum((tile_id[:, None] >= tile_end[None, :]).astype(jnp.int32), axis=1), N_EXPERTS - 1)
    ya, yb = _experts(_sc_scatter(xa, pos, P), _sc_scatter(xb, pos, P), tile_expert, tile_end[-1:], w1, w3, w2)
    return _combine(xt, _sc_gather(ya, pos), _sc_gather(yb, pos), meta, ln)


def kernel(x, w_in, b_in, att_gq, att_gk, rw_mix, rw_w0, rw_w2, rw_a0, rw_a2, rw_g2, rw_kk, rw_ka, rw_rk, rw_ln_g, rw_ln_b, s5_lam_re, s5_lam_im, s5_log_dt, s5_b_re, s5_b_im, s5_c_re, s5_c_im, s5_d, s5_glu_w, s5_glu_b, ml_conv_w, ml_conv_b, ml_ib, ml_fb, w_gate, b_gate, w_branch, w_out, ln1_g, ln1_b, ffn_w1, ffn_w3, ffn_w2, moe_router, moe_w1, moe_w3, moe_w2, ln2_g, ln2_b):
    B, S, D = x.shape
    xt = x.reshape(B * S, D)
    cos, sin = _rope_tables(S)
    for l in range(DEPTH):
        gain = jnp.concatenate([jnp.tile(att_gq[l], 8) * (HEAD_DIM ** -0.5), jnp.tile(att_gk[l], 2)])[None, :]
        (q, k, v, s5u, mq, mk, mv, mo, r, rk, rv, an, bn, lw, gate, bonus, gt) = _proj(
            xt, _proj_params(w_in[l], b_in[l], ml_ib[l], ml_fb[l]), (cos, sin, gain),
            (ml_conv_w[l], ml_conv_b[l][None, :]),
            _rw_params(rw_mix[l], rw_w0[l], rw_w2[l], rw_a0[l], rw_a2[l], rw_g2[l], rw_kk[l], rw_ka[l], rw_rk[l]), B, S)
        score_bound = 8.1 * jnp.max(jnp.abs(att_gq[l])) * jnp.max(jnp.abs(att_gk[l]))
        o_att = _flash(q, k, v, score_bound, B, S)
        yf, yb = _rwkv(r, rk, rv, an, bn, lw, B, S)
        y_s5 = _s5(s5u, _s5_params(s5_lam_re[l], s5_lam_im[l], s5_log_dt[l], s5_b_re[l], s5_b_im[l], s5_c_re[l],
                                   s5_c_im[l]), B, S)
        hf, hb = _mlstm(mq, mk, mv, gt, B, S)
        xt = _merge(xt, o_att, (yf, yb, gate, bonus), (y_s5, s5u), (hf, hb, mo), w_gate[l].astype(bf16), b_gate[l],
                    _att_branch_weight(w_branch[l, 0]).astype(bf16), w_branch[l, 1:].astype(bf16),
                    w_out[l].astype(bf16), jnp.stack([ln1_g[l], ln1_b[l]]), jnp.stack([rw_ln_g[l], rw_ln_b[l]]),
                    jnp.stack([s5_d[l], s5_glu_b[l]]), s5_glu_w[l].astype(bf16))
        ln2 = jnp.stack([ln2_g[l], ln2_b[l]])
        if l % 2 == 0:
            xt = _ffn(xt, ffn_w1[l // 2].astype(bf16), ffn_w3[l // 2].astype(bf16), ffn_w2[l // 2].astype(bf16), ln2)
        else:
            router = jnp.pad(moe_router[l // 2], ((0, 0), (0, LANES - N_EXPERTS)))
            xt = _moe(xt, router, moe_w1[l // 2], moe_w3[l // 2], moe_w2[l // 2], ln2)
    return xt.reshape(B, S, D)
```

```python
import functools
import math

import jax
import jax.numpy as jnp
import numpy as np
from jax import lax
from jax.experimental import pallas as pl
from jax.experimental.pallas import tpu as pltpu
from jax.experimental.pallas import tpu_sc as plsc

f32 = jnp.float32
bf16 = jnp.bfloat16

D_MODEL = 1024
DEPTH = 2
GRID_W = 64
BRANCH_W = 256
HEAD_DIM = 64
ATT_HEADS = 4
ATT_KV_HEADS = 2
ROPE_THETA = 10000.0
QK_EPS = 1e-6
RW_GN_EPS = 64e-5
RW_COLS = 1088
S5_GROUP = 16
S5_GROUPS = 16
S5_STATE = 64
ML_HEADS = 4
N_EXPERTS = 8
ALPHA = (2 * DEPTH) ** 0.25
LN_EPS = 1e-5

LANES = 128
CHUNK = 64
NEG = -1e30
VMEM_LIMIT = 56 * 1024 * 1024

PROJ_SPLITS = (768, 256, 512, 256, 256, RW_COLS)


def _cparams(sem):
    return pltpu.CompilerParams(dimension_semantics=sem, vmem_limit_bytes=VMEM_LIMIT)


def _sigmoid(x):
    return 1.0 / (1.0 + jnp.exp(-x))


def _softplus(x):
    return jnp.maximum(x, 0.0) + jnp.log(1.0 + jnp.exp(-jnp.abs(x)))


def _dims(a, lhs_c, rhs_c):
    lead = a.ndim - 2
    batch = tuple(range(lead))
    return (((lhs_c + lead,), (rhs_c + lead,)), (batch, batch))


def _bdot(a, b):
    return lax.dot_general(a.astype(bf16), b.astype(bf16), _dims(a, 1, 0), preferred_element_type=f32)


def _bdot_nt(a, b):
    return lax.dot_general(a.astype(bf16), b.astype(bf16), _dims(a, 1, 1), preferred_element_type=f32)


def _bdot_tn(a, b):
    return lax.dot_general(a.astype(bf16), b.astype(bf16), _dims(a, 0, 0), preferred_element_type=f32)


def _split(x):
    hi = x.astype(bf16)
    return hi, (x - hi.astype(f32)).astype(bf16)


def _sdot(a, b, exact):
    dims = _dims(a, 1, 0)
    if exact == "rhs":
        hi, lo = _split(a)
        bb = b.astype(bf16)
        return (lax.dot_general(hi, bb, dims, preferred_element_type=f32)
                + lax.dot_general(lo, bb, dims, preferred_element_type=f32))
    hi, lo = _split(b)
    ab = a.astype(bf16)
    return (lax.dot_general(ab, hi, dims, preferred_element_type=f32)
            + lax.dot_general(ab, lo, dims, preferred_element_type=f32))


def _sdot3(a, b):
    dims = _dims(a, 1, 0)
    ah, al = _split(a)
    bh, bl = _split(b)
    return (lax.dot_general(ah, bh, dims, preferred_element_type=f32)
            + lax.dot_general(ah, bl, dims, preferred_element_type=f32)
            + lax.dot_general(al, bh, dims, preferred_element_type=f32))


def _seg_matrix(n, seg=HEAD_DIM):
    r = lax.broadcasted_iota(jnp.int32, (n, n), 0) // seg
    c = lax.broadcasted_iota(jnp.int32, (n, n), 1) // seg
    return (r == c).astype(f32)


def _layer_norm(y, g, b):
    mu = jnp.mean(y, axis=-1, keepdims=True)
    d = y - mu
    var = jnp.mean(d * d, axis=-1, keepdims=True)
    return d * lax.rsqrt(var + LN_EPS) * g + b


def _row_to_col(row, eye):
    n = eye.shape[-1]
    return jnp.sum(jnp.where(eye, jnp.broadcast_to(row, (row.shape[0], n, n)), 0.0), axis=2, keepdims=True)


def _stack_heads(x):
    h0 = lax.broadcasted_iota(jnp.int32, x.shape, 2) < HEAD_DIM
    return jnp.concatenate([jnp.where(h0, x, 0.0), jnp.where(h0, 0.0, x)], axis=1)


def _proj_body(x_ref, xp_ref, xn_ref, cos_ref, sin_ref, w_ref, b_ref, wg_ref, bg_ref, gain_ref, cw_ref, cb_ref,
               mix_ref, w2_ref, a2_ref, g2_ref, vec_ref,
               q_ref, k_ref, v_ref, s5_ref, mq_ref, mk_ref, mv_ref, mo_ref, r_ref, rk_ref, rv_ref, an_ref, bn_ref,
               lw_ref, gate_ref, bonus_ref, g_ref, mbuf_ref, rbuf_ref):
    tm = x_ref.shape[0]
    xb, xp, xn = x_ref[...].astype(bf16), xp_ref[...].astype(bf16), xn_ref[...].astype(bf16)
    offs = np.cumsum((0,) + PROJ_SPLITS)

    def cols(rows, n):
        sl = slice(int(offs[n]), int(offs[n + 1]))
        return jnp.dot(rows, w_ref[:, sl], preferred_element_type=f32) + b_ref[:, sl]

    q_ref[...], k_ref[...], v_ref[...] = _att_prep(cols(xb, 0), cos_ref[...], sin_ref[...], gain_ref[...])
    s5_ref[...] = cols(xb, 1)
    _fill_halo(mbuf_ref, cols(xb, 2), cols(xp, 2), cols(xn, 2))
    mq_ref[...], mk_ref[...] = _ml_prep(mbuf_ref, tm, cw_ref[...], cb_ref[...])
    mv_ref[...] = cols(xb, 3)
    mo_ref[...] = cols(xb, 4)
    _fill_halo(rbuf_ref, cols(xb, 5), cols(xp, 5), cols(xn, 5))
    outs = _rw_prep(rbuf_ref, tm, mix_ref[...], w2_ref[...], a2_ref[...], g2_ref[...], vec_ref[...])
    for o_ref, val in zip((r_ref, rk_ref, rv_ref, an_ref, bn_ref), outs[0:5]):
        o_ref[...] = val
    lw_ref[0], lw_ref[1], gate_ref[...], bonus_ref[...] = outs[5:9]
    g_ref[...] = lax.dot_general(wg_ref[...], xb, (((1,), (1,)), ((), ())), preferred_element_type=f32) + bg_ref[...]


def _proj(xt, proj_prm, att_prm, ml_prm, rw_prm, B, S, tm=512):
    T = B * S
    nb = S // tm
    n_tot = sum(PROJ_SPLITS)
    row = lambda n: pl.BlockSpec((tm, n), lambda b, i: (b * nb + i, 0))
    const = lambda a: pl.BlockSpec(a.shape, lambda b, i: (0,) * a.ndim)
    tab = pl.BlockSpec((tm, LANES), lambda b, i: (i, 0))
    f32out = lambda n: jax.ShapeDtypeStruct((T, n), f32)
    consts = list(proj_prm) + [att_prm[2]] + list(ml_prm) + list(rw_prm)
    out_specs = ([row(512), row(LANES), row(2 * LANES)] + [row(BRANCH_W)] * 10
                 + [pl.BlockSpec((2, tm, BRANCH_W), lambda b, i: (0, b * nb + i, 0)), row(BRANCH_W), row(BRANCH_W),
                    pl.BlockSpec((16, tm), lambda b, i: (0, b * nb + i))])
    out_shape = ([jax.ShapeDtypeStruct((T, 512), bf16), jax.ShapeDtypeStruct((T, LANES), bf16),
                  jax.ShapeDtypeStruct((T, 2 * LANES), bf16)] + [f32out(BRANCH_W)] * 10
                 + [jax.ShapeDtypeStruct((2, T, BRANCH_W), f32), f32out(BRANCH_W), f32out(BRANCH_W),
                    jax.ShapeDtypeStruct((16, T), f32)])
    return pl.pallas_call(
        _proj_body, grid=(B, nb),
        in_specs=_halo_specs(D_MODEL, tm, B, S) + [tab, tab] + [const(a) for a in consts],
        out_specs=out_specs, out_shape=out_shape,
        scratch_shapes=[pltpu.VMEM((tm + 16, 512), f32), pltpu.VMEM((tm + 16, RW_COLS), f32)],
        compiler_params=_cparams(("parallel", "parallel")), name="proj")(
            xt, xt, xt, att_prm[0], att_prm[1], *consts)


def _proj_params(w_in, b_in, ml_ib, ml_fb):
    o = np.cumsum((0, 256, 128, 128, RW_COLS, 256, 512, 256, 8, 8, 256))
    sl = lambda i: (w_in[:, o[i]:o[i + 1]], b_in[o[i]:o[i + 1]])
    (wq, bq), (wk, bk), (wv, bv), (wrw, brw), (ws5, bs5), (wqk, bqk), (wmv, bmv), (wi, bi), (wf, bf), (wo, bo) = (
        sl(i) for i in range(10))
    zw, zb = jnp.zeros((D_MODEL, HEAD_DIM), f32), jnp.zeros((HEAD_DIM,), f32)
    wq_e, bq_e = [], []
    for h in range(ATT_HEADS):
        wh, bh = wq[:, 64 * h:64 * h + 64], bq[64 * h:64 * h + 64]
        wq_e += [wh, zw] if h // 2 == 0 else [zw, wh]
        bq_e += [bh, zb] if h // 2 == 0 else [zb, bh]
    w = jnp.concatenate(wq_e + [wk, wv, ws5, wqk, wmv, wo, wrw], axis=1)
    b = jnp.concatenate(bq_e + [bk, bv, bs5, bqk, bmv, bo, brw])
    wg = jnp.concatenate([wi, wf], axis=1).T
    bg = jnp.concatenate([bi + ml_ib.reshape(-1), bf + ml_fb.reshape(-1)])
    return w.astype(bf16), b[None, :], wg.astype(bf16), bg[:, None]


def _att_prep(att, cos, sin, gain):
    x = att[:, 0:640]
    ms = _sdot(x * x, _seg_matrix(640), "rhs") * (1.0 / HEAD_DIM)
    xn = x * lax.rsqrt(ms + QK_EPS) * gain
    lane = lax.broadcasted_iota(jnp.int32, xn.shape, 1)
    partner = jnp.where((lane % 32) < 16, pltpu.roll(xn, 640 - 16, 1), pltpu.roll(xn, 16, 1))
    rot = xn * jnp.concatenate([cos] * 5, axis=1) + partner * jnp.concatenate([sin] * 5, axis=1)
    v = jnp.concatenate([att[:, 640:768].astype(bf16), jnp.ones((x.shape[0], LANES), bf16)], axis=1)
    return rot[:, 0:512].astype(bf16), rot[:, 512:640].astype(bf16), v


def _rope_tables(S):
    t = np.arange(S)
    row = (t // GRID_W).astype(np.float32)
    col = (t % GRID_W).astype(np.float32)
    n = 16
    inv = np.power(np.float32(ROPE_THETA), -np.arange(n, dtype=np.float32) / n).astype(np.float32)
    ar = jnp.asarray(row)[:, None] * jnp.asarray(inv)
    ac = jnp.asarray(col)[:, None] * jnp.asarray(inv)
    cos = jnp.concatenate([jnp.cos(ar), jnp.cos(ar), jnp.cos(ac), jnp.cos(ac)], axis=1)
    sin = jnp.concatenate([-jnp.sin(ar), jnp.sin(ar), -jnp.sin(ac), jnp.sin(ac)], axis=1)
    return jnp.concatenate([cos, cos], axis=1), jnp.concatenate([sin, sin], axis=1)


def _flash_body(q_ref, k_ref, v_ref, o_ref, acc_ref, *m_scratch, tk, track_max):
    tq = q_ref.shape[0]
    nk = k_ref.shape[0] // tk
    q2 = jnp.concatenate([q_ref[:, 0:LANES], q_ref[:, LANES:2 * LANES]], axis=0)
    acc_ref[...] = jnp.zeros(acc_ref.shape, f32)
    if track_max:
        m_ref, = m_scratch
        m_ref[...] = jnp.full(m_ref.shape, NEG, f32)

    def step(j, carry):
        rows = pl.ds(pl.multiple_of(j * tk, tk), tk)
        s = lax.dot_general(q2, k_ref[rows, :], (((1,), (1,)), ((), ())), preferred_element_type=f32)
        if track_max:
            m_old = m_ref[...]
            m_new = jnp.maximum(m_old, jnp.max(s, axis=1, keepdims=True))
            p = jnp.exp(s - m_new).astype(bf16)
            acc_ref[...] = jnp.exp(m_old - m_new) * acc_ref[...] + jnp.dot(p, v_ref[rows, :], preferred_element_type=f32)
            m_ref[...] = m_new
        else:
            acc_ref[...] += jnp.dot(jnp.exp(s).astype(bf16), v_ref[rows, :], preferred_element_type=f32)
        return carry

    lax.fori_loop(0, nk, step, 0)
    o = acc_ref[:, 0:LANES] / acc_ref[:, LANES:2 * LANES]
    o_ref[...] = jnp.concatenate([o[0:tq], o[tq:2 * tq]], axis=1)


SCORE_BOUND_MAX = 60.0


def _flash(q, k, v, score_bound, B, S, tq=256, tk=8192):
    T = B * S
    nb = S // tq
    tk = min(tk, S)

    def call(track_max):
        scratch = [pltpu.VMEM((2 * tq, 2 * LANES), f32)] + ([pltpu.VMEM((2 * tq, 1), f32)] if track_max else [])
        return pl.pallas_call(
            functools.partial(_flash_body, tk=tk, track_max=track_max), grid=(B, ATT_KV_HEADS, nb),
            in_specs=[pl.BlockSpec((tq, 2 * LANES), lambda b, g, i: (b * nb + i, g)),
                      pl.BlockSpec((S, LANES), lambda b, g, i: (b, 0)),
                      pl.BlockSpec((S, 2 * LANES), lambda b, g, i: (b, 0))],
            out_specs=pl.BlockSpec((tq, 2 * LANES), lambda b, g, i: (b * nb + i, g)),
            out_shape=jax.ShapeDtypeStruct((T, 512), f32), scratch_shapes=scratch,
            compiler_params=_cparams(("parallel", "parallel", "parallel")),
            name="flash_safe" if track_max else "flash")(q, k, v)

    return lax.cond(score_bound <= SCORE_BOUND_MAX, lambda: call(False), lambda: call(True))


def _halo_specs(width, tm, B, S):
    nb = S // tm
    r8 = tm // 8
    last8 = B * S // 8 - 1

    def main(b, i):
        return (b * nb + i, 0)

    def prev(b, i):
        return (jnp.maximum(b * (S // 8) + i * r8 - 1, 0), 0)

    def nxt(b, i):
        return (jnp.minimum(b * (S // 8) + (i + 1) * r8, last8), 0)

    return [pl.BlockSpec((tm, width), main), pl.BlockSpec((8, width), prev), pl.BlockSpec((8, width), nxt)]


def _fill_halo(buf_ref, x, prev8, next8):
    tm = x.shape[0]
    i = pl.program_id(1)
    last = pl.num_programs(1) - 1
    buf_ref[pl.ds(8, tm), :] = x
    buf_ref[pl.ds(0, 8), :] = jnp.where(i > 0, prev8, 0.0)
    buf_ref[pl.ds(8 + tm, 8), :] = jnp.where(i < last, next8, 0.0)


def _rw_prep(buf_ref, tm, mix, w2, a2, g2, vec):
    x = buf_ref[pl.ds(8, tm), :]
    p = x + mix[0:1, :] * (buf_ref[pl.ds(7, tm), :] - x) + mix[1:2, :] * (buf_ref[pl.ds(9, tm), :] - x)
    r, k, v = p[:, 0:256], p[:, 256:512], p[:, 512:768]
    w0f, w0b, a0, k_k, k_a, r_k = (vec[j:j + 1, :] for j in range(6))
    dec = _bdot(jnp.tanh(p[:, 768:896]), w2)
    z = p[:, 896:1088]
    a = _sigmoid(a0 + _bdot(z, a2))
    gate = _bdot(_sigmoid(z), g2)
    seg = _seg_matrix(BRANCH_W)
    kk = k * k_k
    kk = kk / jnp.maximum(jnp.sqrt(_sdot(kk * kk, seg, "rhs")), 1e-12)
    k2 = k * (1.0 + (a - 1.0) * k_a)
    bonus = _sdot(r * k2 * r_k, seg, "rhs") * v
    lwf = -jnp.exp(-_softplus(-(w0f + dec[:, 0:256])) - 0.5)
    lwb = -jnp.exp(-_softplus(-(w0b + dec[:, 256:512])) - 0.5)
    return r, k2, v, -kk, kk * a, lwf, lwb, gate, bonus


def _by_direction(x, fwd, bwd, fn):
    h = x.shape[0] // 2
    return jnp.concatenate([fn(x[0:h], fwd), fn(x[h:2 * h], bwd)], axis=0)


def _keep(x, masks):
    return _by_direction(x, masks[0], masks[1], lambda t, m: jnp.where(m, t, 0.0))


def _both(masks, nchain):
    h = nchain // 2
    return jnp.concatenate([jnp.broadcast_to(m.astype(f32), (h,) + m.shape[1:]) for m in masks], axis=0)


def _pair_masks():
    n = 2 * CHUNK
    r = lax.broadcasted_iota(jnp.int32, (1, n, n), 1)
    c = lax.broadcasted_iota(jnp.int32, (1, n, n), 2)
    same = (r // CHUNK) == (c // CHUNK)
    return r, c, same, (same & (c < r), same & (c > r)), (same & (c <= r), same & (c >= r))


def _rw_chunk(st, r, k, v, an, bn, lw):
    L = CHUNK
    N = r.shape[0]
    ri = lax.broadcasted_iota(jnp.int32, (1, L, L), 1)
    ci = lax.broadcasted_iota(jnp.int32, (1, L, L), 2)
    cs = _sdot(_both((ci <= ri, ci >= ri), N), lw, "lhs")
    tot = jnp.sum(lw, axis=1, keepdims=True)
    e_neg = jnp.exp(-cs)
    a2 = _stack_heads(an * jnp.exp(cs - lw))
    r2 = _stack_heads(r * jnp.exp(cs))
    b2 = _stack_heads(bn * e_neg)
    k2 = _stack_heads(k * e_neg)
    v2 = _stack_heads(v)
    rr, cc, _, strict, incl = _pair_masks()
    n2 = 2 * L
    g = _bdot_nt(jnp.concatenate([a2, r2], axis=1), jnp.concatenate([b2, k2], axis=1))
    mab = _keep(g[:, 0:n2, 0:n2], strict)
    mak = _keep(g[:, 0:n2, n2:2 * n2], strict)
    pb = _keep(g[:, n2:2 * n2, 0:n2], incl)
    pk = _keep(g[:, n2:2 * n2, n2:2 * n2], incl)
    eye = rr == cc
    m8 = jnp.where((rr // 8) == (cc // 8), mab, 0.0)
    x = eye.astype(f32) + m8
    p = _bdot(m8, m8)
    x = x + _bdot(x, p)
    p = _bdot(p, p)
    x = x + _bdot(x, p)
    n = 8
    while n < L:
        e = jnp.where(((rr // (2 * n)) == (cc // (2 * n))) & ((rr // n) != (cc // n)), mab, 0.0)
        x = x + _bdot(_bdot(x, e), x)
        n *= 2
    wu = _bdot(x, jnp.concatenate([a2, _bdot(mak, v2)], axis=2))
    pwu = _bdot(pb, wu)
    rh = r2 + pwu[:, :, 0:LANES]
    y2 = pwu[:, :, LANES:2 * LANES] + _bdot(jnp.concatenate([pk, rh], axis=2), jnp.concatenate([v2, st], axis=1))
    y = y2[:, 0:L] + y2[:, L:2 * L]
    gam = _row_to_col(jnp.exp(tot), eye)
    bwu = _bdot_tn(b2, wu)
    st = gam * (st + _bdot(bwu[:, :, 0:LANES], st) + bwu[:, :, LANES:2 * LANES] + _bdot_tn(k2, v2))
    return y, st


def _load_pairs(ref, rows):
    return jnp.concatenate([ref[:, rows, 0:LANES], ref[:, rows, LANES:2 * LANES]], axis=0)


def _store_pairs(ref, rows, y):
    nb = ref.shape[0]
    ref[:, rows, 0:LANES] = y[0:nb]
    ref[:, rows, LANES:2 * LANES] = y[nb:2 * nb]


def _load_both(f_ref, b_ref, rows_f, rows_b):
    return jnp.concatenate([_load_pairs(f_ref, rows_f), _load_pairs(b_ref, rows_b)], axis=0)


def _chunk_rows(cc, nch):
    return (pl.ds(pl.multiple_of(cc * CHUNK, CHUNK), CHUNK),
            pl.ds(pl.multiple_of((nch - 1 - cc) * CHUNK, CHUNK), CHUNK))


def _rw_scan_body(rf_ref, rb_ref, kf_ref, kb_ref, vf_ref, vb_ref, anf_ref, anb_ref, bnf_ref, bnb_ref, lwf_ref, lwb_ref,
                  yf_ref, yb_ref, st_ref):
    nch = rf_ref.shape[1] // CHUNK
    half = st_ref.shape[0] // 2

    @pl.when(pl.program_id(0) == 0)
    def _():
        st_ref[...] = jnp.zeros(st_ref.shape, f32)

    def step(cc, carry):
        rows_f, rows_b = _chunk_rows(cc, nch)
        pairs = ((rf_ref, rb_ref), (kf_ref, kb_ref), (vf_ref, vb_ref), (anf_ref, anb_ref), (bnf_ref, bnb_ref),
                 (lwf_ref.at[0], lwb_ref.at[0]))
        y, st = _rw_chunk(st_ref[...], *(_load_both(f, b, rows_f, rows_b) for f, b in pairs))
        _store_pairs(yf_ref, rows_f, y[0:half])
        _store_pairs(yb_ref, rows_b, y[half:2 * half])
        st_ref[...] = st
        return carry

    lax.fori_loop(0, nch, step, 0)


def _rw_scan(r, k, v, an, bn, lw, B, S, ts=128):
    nb = S // ts
    fwd = pl.BlockSpec((B, ts, BRANCH_W), lambda i: (0, i, 0))
    bwd = pl.BlockSpec((B, ts, BRANCH_W), lambda i: (0, nb - 1 - i, 0))
    out = jax.ShapeDtypeStruct((B, S, BRANCH_W), f32)
    return pl.pallas_call(
        _rw_scan_body, grid=(nb,),
        in_specs=[fwd, bwd] * 5 + [pl.BlockSpec((1, B, ts, BRANCH_W), lambda i: (0, 0, i, 0)),
                                   pl.BlockSpec((1, B, ts, BRANCH_W), lambda i: (1, 0, nb - 1 - i, 0))],
        out_specs=[fwd, bwd], out_shape=[out, out],
        scratch_shapes=[pltpu.VMEM((2 * B * BRANCH_W // LANES, LANES, LANES), f32)],
        compiler_params=_cparams(("arbitrary",)), name="rw_scan")(r, r, k, k, v, v, an, an, bn, bn, lw, lw)


def _rw_finish(y, gate, bonus, gn):
    seg = _seg_matrix(BRANCH_W)
    mu = _sdot(y, seg, "rhs") * (1.0 / HEAD_DIM)
    d = y - mu
    var = _sdot(d * d, seg, "rhs") * (1.0 / HEAD_DIM)
    yn = d * lax.rsqrt(var + RW_GN_EPS) * gn[0:1, :] + gn[1:2, :]
    return (yn + bonus) * gate


def _rwkv(r, k, v, an, bn, lw, B, S):
    seq = [t.reshape(B, S, BRANCH_W) for t in (r, k, v, an, bn)] + [lw.reshape(2, B, S, BRANCH_W)]
    yf, yb = _rw_scan(*seq, B, S)
    return yf.reshape(B * S, BRANCH_W), yb.reshape(B * S, BRANCH_W)


def _rw_params(mix, w0, w2, a0, a2, g2, k_k, k_a, r_k):
    z = jnp.zeros((64, 256), f32)
    w2c = jnp.concatenate([jnp.concatenate([w2[0], z], axis=1), jnp.concatenate([z, w2[1]], axis=1)], axis=0)
    a2p = jnp.concatenate([a2, jnp.zeros((128, 256), f32)], axis=0)
    g2p = jnp.concatenate([jnp.zeros((64, 256), f32), g2], axis=0)
    vec = jnp.stack([w0[0], w0[1], a0, k_k, k_a, r_k.reshape(-1), jnp.zeros_like(a0), jnp.zeros_like(a0)])
    return mix, w2c.astype(bf16), a2p.astype(bf16), g2p.astype(bf16), vec


S5_HALF = S5_GROUPS * S5_STATE // 2


def _s5_scan_body(u_ref, bh_ref, cre_ref, cim_ref, lre_ref, lim_ref, y_ref, sre_ref, sim_ref, bre_ref, bim_ref):
    ts = u_ref.shape[0]
    n = S5_HALF

    @pl.when(pl.program_id(0) == 0)
    def _():
        sre_ref[...] = jnp.zeros(sre_ref.shape, f32)
        sim_ref[...] = jnp.zeros(sim_ref.shape, f32)

    u = u_ref[...]
    ur = pltpu.roll(u, 4, 1)
    row_half = lax.broadcasted_iota(jnp.int32, u.shape, 1) // 4
    lane_half = lax.broadcasted_iota(jnp.int32, u.shape, 2) // (BRANCH_W // 2)
    top = row_half == 0
    low = lane_half == 0
    for d in range(2):
        ud = jnp.where(top, u, ur) if d == 0 else jnp.where(top, ur, u)
        lhs = jnp.where(row_half == lane_half, ud, 0.0).reshape(ts * 8, BRANCH_W)
        bu = _bdot(lhs, bh_ref[d]).reshape(ts, 8, 2 * n)
        bre_ref[d] = bu[:, :, 0:n]
        bim_ref[d] = bu[:, :, n:2 * n]
    lre = lre_ref[...]
    lim = lim_ref[...]

    def step(t, carry):
        out = []
        for d in range(2):
            sre, sim = carry[2 * d], carry[2 * d + 1]
            nre = lre[d] * sre - lim[d] * sim + bre_ref[d, t]
            nim = lre[d] * sim + lim[d] * sre + bim_ref[d, t]
            bre_ref[d, t] = nre
            bim_ref[d, t] = nim
            out += [nre, nim]
        return tuple(out)

    s = lax.fori_loop(0, ts, step, (sre_ref[0], sim_ref[0], sre_ref[1], sim_ref[1]), unroll=4)
    sre_ref[0], sim_ref[0], sre_ref[1], sim_ref[1] = s
    top_o = top[:, :, 0:BRANCH_W]
    low_o = low[:, :, 0:BRANCH_W]
    halves = []
    for d in range(2):
        yv = (_bdot(bre_ref[d].reshape(ts * 8, n), cre_ref[d])
              - _bdot(bim_ref[d].reshape(ts * 8, n), cim_ref[d])).reshape(ts, 8, BRANCH_W)
        halves.append(jnp.where(low_o, yv, pltpu.roll(yv, 4, 1)))
    y_ref[...] = jnp.where(top_o, halves[0], pltpu.roll(halves[1], 4, 1))


def _s5_scan(u2, bh, cre, cim, lre, lim, ts=128):
    S = u2.shape[0]
    n = S5_HALF
    full = lambda shape: pl.BlockSpec(shape, lambda i: (0,) * len(shape))
    return pl.pallas_call(
        _s5_scan_body, grid=(S // ts,),
        in_specs=[pl.BlockSpec((ts, 8, BRANCH_W), lambda i: (i, 0, 0)), full((2, BRANCH_W, 2 * n)),
                  full((2, n, BRANCH_W)), full((2, n, BRANCH_W)), full((2, 8, n)), full((2, 8, n))],
        out_specs=pl.BlockSpec((ts, 8, BRANCH_W), lambda i: (i, 0, 0)),
        out_shape=jax.ShapeDtypeStruct((S, 8, BRANCH_W), f32),
        scratch_shapes=[pltpu.VMEM((2, 8, n), f32), pltpu.VMEM((2, 8, n), f32),
                        pltpu.VMEM((2, ts, 8, n), f32), pltpu.VMEM((2, ts, 8, n), f32)],
        compiler_params=_cparams(("arbitrary",)), name="s5_scan")(u2, bh, cre, cim, lre, lim)


def _s5_finish(y, u, vec, w):
    y = y + u * vec[0:1, :]
    y = 0.5 * y * (1.0 + jnp.tanh(math.sqrt(2.0 / math.pi) * (y + 0.044715 * (y * y * y))))
    return y * _sigmoid(_bdot(y, w) + vec[1:2, :])


def _s5_params(lam_re, lam_im, log_dt, b_re, b_im, c_re, c_im):
    G, P, C = S5_GROUPS, S5_STATE, S5_GROUP
    H = G // 2
    eye = jnp.eye(H, dtype=f32)
    b_c = lax.complex(b_re, b_im)

    def b_part(x):
        return jnp.einsum('ab,hapc->hacbp', eye, x.reshape(2, H, P, C)).reshape(G * C, H * P)

    def c_part(x):
        return jnp.einsum('ab,hacp->aphbc', eye, x.reshape(2, H, C, P)).reshape(H * P, G * C)

    def rows(x):
        return jnp.broadcast_to(x.reshape(2, 1, H * P), (2, 4, H * P)).reshape(8, H * P)

    bh, cre, cim, lre, lim = [], [], [], [], []
    for d in range(2):
        lam = lax.complex(jnp.minimum(lam_re[d], -1e-4), lam_im[d])
        lam_bar = jnp.exp(lam * jnp.exp(log_dt[d])[:, None])
        b_bar = ((lam_bar - 1.0) / lam)[..., None] * b_c
        bh.append(jnp.concatenate([b_part(jnp.real(b_bar)), b_part(jnp.imag(b_bar))], axis=1))
        cre.append(c_part(c_re[d]))
        cim.append(c_part(c_im[d]))
        lre.append(rows(jnp.real(lam_bar)))
        lim.append(rows(jnp.imag(lam_bar)))
    return (jnp.stack(bh).astype(bf16), jnp.stack(cre).astype(bf16), jnp.stack(cim).astype(bf16),
            jnp.stack(lre), jnp.stack(lim))


def _s5(u, prm, B, S):
    u3 = u.reshape(B, S, BRANCH_W).transpose(1, 0, 2)
    u2 = jnp.concatenate([u3, u3[::-1]], axis=1)
    y2 = _s5_scan(u2, *prm)
    return (y2[:, 0:B] + y2[::-1, B:2 * B]).transpose(1, 0, 2).reshape(B * S, BRANCH_W)


def _ml_prep(buf_ref, tm, w, b):
    y = b
    for j in range(5):
        y = y + w[j:j + 1, :] * buf_ref[pl.ds(6 + j, tm), :]
    y = y * _sigmoid(y)
    return y[:, 0:BRANCH_W], y[:, BRANCH_W:2 * BRANCH_W] * (HEAD_DIM ** -0.5)


def _ml_chunk(state, q, k, v, li, lfp):
    cn, m_row = state
    L = CHUNK
    N = q.shape[0]
    rr, cc, same, _, incl = _pair_masks()
    eye = rr == cc
    lane = lax.broadcasted_iota(jnp.int32, (1, 1, 2 * L), 2)
    lf = jnp.minimum(lfp, 0.0) - jnp.log(1.0 + jnp.exp(-jnp.abs(lfp)))
    lf8 = jnp.broadcast_to(lf, (N, 8, 2 * L))
    b_row = _sdot(lf8, _both((incl[1], incl[0]), N), "rhs")[:, 0:1]
    g_row = _sdot(lf8, _both((same, same), N), "rhs")[:, 0:1]
    w_end = g_row - b_row + li
    m0 = jnp.max(jnp.where(lane < L, w_end, NEG), axis=2, keepdims=True)
    m1 = jnp.max(jnp.where(lane < L, NEG, w_end), axis=2, keepdims=True)
    m_loc = jnp.where(lane < L, m0, m1)
    e_col = _row_to_col(jnp.exp(w_end - m_loc), eye)
    b_col = _row_to_col(b_row, eye)
    q2, k2, v2 = _stack_heads(q), _stack_heads(k), _stack_heads(v)
    v1 = jnp.concatenate([v2, _stack_heads(jnp.ones_like(v))], axis=2)
    log_inter = b_col + _row_to_col(m_row, eye)
    log_intra = _by_direction(b_col - b_row + li, incl[0], incl[1], lambda t, m: jnp.where(m, t, NEG))
    m_r = jnp.maximum(log_inter, jnp.max(log_intra, axis=2, keepdims=True))
    s = _bdot_nt(q2, k2) * jnp.exp(log_intra - m_r)
    inter = jnp.exp(log_inter - m_r)
    nd = _bdot(s, v1) + inter * _bdot(q2, cn)
    h2 = nd[:, :, 0:LANES] / jnp.maximum(jnp.abs(nd[:, :, LANES:2 * LANES]), jnp.exp(-m_r))
    h = h2[:, 0:L] + h2[:, L:2 * L]
    m_new = jnp.maximum(g_row + m_row, m_loc)
    a = jnp.exp(g_row + m_row - m_new)
    bb = jnp.exp(m_loc - m_new)
    cn = jnp.concatenate([a, a], axis=2) * cn + jnp.concatenate([bb, bb], axis=2) * _bdot_tn(e_col * k2, v1)
    return h, (cn, m_new)


GATE_ROWS = 8


def _ml_scan_body(qf_ref, qb_ref, kf_ref, kb_ref, vf_ref, vb_ref, gf_ref, gb_ref, hf_ref, hb_ref, c_ref, m_ref):
    nch = qf_ref.shape[1] // CHUNK
    half = c_ref.shape[0] // 2
    i = pl.program_id(0)
    nb = pl.num_programs(0)
    per = GATE_ROWS // nch
    base_f = (i % per) * nch
    base_b = ((nb - 1 - i) % per) * nch

    @pl.when(i == 0)
    def _():
        c_ref[...] = jnp.zeros(c_ref.shape, f32)
        m_ref[...] = jnp.zeros(m_ref.shape, f32)

    def step(cc, carry):
        rows_f, rows_b = _chunk_rows(cc, nch)
        gate = lambda t: jnp.concatenate(
            [gf_ref[t, 0, 0, :, pl.ds(base_f + cc, 1), :], gf_ref[t, 0, 1, :, pl.ds(base_f + cc, 1), :],
             gb_ref[t, 0, 0, :, pl.ds(base_b + nch - 1 - cc, 1), :], gb_ref[t, 0, 1, :, pl.ds(base_b + nch - 1 - cc, 1), :]],
            axis=0)
        h, (cn, m_row) = _ml_chunk((c_ref[...], m_ref[...]), _load_both(qf_ref, qb_ref, rows_f, rows_b),
                                   _load_both(kf_ref, kb_ref, rows_f, rows_b), _load_both(vf_ref, vb_ref, rows_f, rows_b),
                                   gate(0), gate(1))
        _store_pairs(hf_ref, rows_f, h[0:half])
        _store_pairs(hb_ref, rows_b, h[half:2 * half])
        c_ref[...] = cn
        m_ref[...] = m_row
        return carry

    lax.fori_loop(0, nch, step, 0)


def _ml_scan(q, k, v, g, B, S, ts=256):
    nb = S // ts
    per = GATE_ROWS * CHUNK // ts
    nchain = 2 * B * BRANCH_W // LANES
    fwd = pl.BlockSpec((B, ts, BRANCH_W), lambda i: (0, i, 0))
    bwd = pl.BlockSpec((B, ts, BRANCH_W), lambda i: (0, nb - 1 - i, 0))
    out = jax.ShapeDtypeStruct((B, S, BRANCH_W), f32)
    return pl.pallas_call(
        _ml_scan_body, grid=(nb,),
        in_specs=[fwd, bwd] * 3 + [pl.BlockSpec((2, 1, 2, B, GATE_ROWS, LANES), lambda i: (0, 0, 0, 0, i // per, 0)),
                                   pl.BlockSpec((2, 1, 2, B, GATE_ROWS, LANES),
                                                lambda i: (0, 1, 0, 0, (nb - 1 - i) // per, 0))],
        out_specs=[fwd, bwd], out_shape=[out, out],
        scratch_shapes=[pltpu.VMEM((nchain, LANES, 2 * LANES), f32), pltpu.VMEM((nchain, 1, LANES), f32)],
        compiler_params=_cparams(("arbitrary",)), name="ml_scan")(q, q, k, k, v, v, g, g)


def _mlstm(q, k, mv, gt, B, S):
    g = gt.reshape(2, 2, 2, 2, B, S // CHUNK, CHUNK).transpose(0, 1, 2, 4, 5, 3, 6).reshape(2, 2, 2, B, S // CHUNK, LANES)
    hf, hb = _ml_scan(*(t.reshape(B, S, BRANCH_W) for t in (q, k, mv)), g, B, S)
    return hf.reshape(B * S, BRANCH_W), hb.reshape(B * S, BRANCH_W)


def _merge_body(x_ref, att_ref, yf_ref, yb_ref, rg_ref, rb_ref, sy_ref, su_ref, hf_ref, hb_ref, mo_ref,
                wg_ref, bg_ref, wba_ref, wb_ref, wo_ref, ln_ref, gn_ref, sv_ref, sw_ref, o_ref):
    x = x_ref[...]
    xb = x.astype(bf16)
    rw = _rw_finish(yf_ref[...] + yb_ref[...], rg_ref[...], rb_ref[...], gn_ref[...])
    s5 = _s5_finish(sy_ref[...], su_ref[...], sv_ref[...], sw_ref[...])
    ml = _sigmoid(mo_ref[...]) * (hf_ref[...] + hb_ref[...])
    branches = (att_ref[...], rw, s5, ml)
    merged = None
    for n in range(4):
        gate = _sigmoid(jnp.dot(xb, wg_ref[n], preferred_element_type=f32) + bg_ref[n:n + 1, :])
        wide = _bdot(branches[n], wba_ref[...] if n == 0 else wb_ref[n - 1])
        merged = gate * wide if merged is None else merged + gate * wide
    y = ALPHA * x + _bdot(merged, wo_ref[...])
    o_ref[...] = _layer_norm(y, ln_ref[0:1, :], ln_ref[1:2, :])


def _merge(xt, att, rw_parts, s5_parts, ml_parts, wg, bg, wba, wb, wo, ln, gn, s5_vec, s5_w, tm=256):
    T = xt.shape[0]
    row = lambda n: pl.BlockSpec((tm, n), lambda i: (i, 0))
    const = lambda shape: pl.BlockSpec(shape, lambda i: (0,) * len(shape), pipeline_mode=pl.Buffered(1))
    return pl.pallas_call(
        _merge_body, grid=(T // tm,),
        in_specs=[row(D_MODEL), row(512)] + [row(BRANCH_W)] * 9
        + [const((4, D_MODEL, D_MODEL)), const((4, D_MODEL)), const((512, D_MODEL)), const((3, BRANCH_W, D_MODEL)),
           const((D_MODEL, D_MODEL)), const((2, D_MODEL)), const((2, BRANCH_W)), const((2, BRANCH_W)),
           const((BRANCH_W, BRANCH_W))],
        out_specs=row(D_MODEL), out_shape=jax.ShapeDtypeStruct((T, D_MODEL), f32),
        compiler_params=_cparams(("parallel",)), name="merge")(
            xt, att, *rw_parts, *s5_parts, *ml_parts, wg, bg, wba, wb, wo, ln, gn, s5_vec, s5_w)


def _att_branch_weight(wb):
    z = jnp.zeros((HEAD_DIM, D_MODEL), f32)
    parts = []
    for h in range(ATT_HEADS):
        wh = wb[64 * h:64 * h + 64]
        parts += [wh, z] if h // 2 == 0 else [z, wh]
    return jnp.concatenate(parts, axis=0)


def _ffn_body(x_ref, w1_ref, w3_ref, w2_ref, ln_ref, o_ref, acc_ref):
    j = pl.program_id(1)

    @pl.when(j == 0)
    def _():
        acc_ref[...] = jnp.zeros(acc_ref.shape, f32)

    xb = x_ref[...].astype(bf16)
    h1 = jnp.dot(xb, w1_ref[...], preferred_element_type=f32)
    h3 = jnp.dot(xb, w3_ref[...], preferred_element_type=f32)
    acc_ref[...] += _bdot(h1 * _sigmoid(h1) * h3, w2_ref[...])

    @pl.when(j == pl.num_programs(1) - 1)
    def _():
        o_ref[...] = _layer_norm(ALPHA * x_ref[...] + acc_ref[...], ln_ref[0:1, :], ln_ref[1:2, :])


def _ffn(xt, w1, w3, w2, ln, tm=512, tf=1408):
    T = xt.shape[0]
    dff = w1.shape[1]
    return pl.pallas_call(
        _ffn_body, grid=(T // tm, dff // tf),
        in_specs=[pl.BlockSpec((tm, D_MODEL), lambda i, j: (i, 0)), pl.BlockSpec((D_MODEL, tf), lambda i, j: (0, j)),
                  pl.BlockSpec((D_MODEL, tf), lambda i, j: (0, j)), pl.BlockSpec((tf, D_MODEL), lambda i, j: (j, 0)),
                  pl.BlockSpec((2, D_MODEL), lambda i, j: (0, 0))],
        out_specs=pl.BlockSpec((tm, D_MODEL), lambda i, j: (i, 0)),
        out_shape=jax.ShapeDtypeStruct((T, D_MODEL), f32), scratch_shapes=[pltpu.VMEM((tm, D_MODEL), f32)],
        compiler_params=_cparams(("parallel", "arbitrary")), name="ffn")(xt, w1, w3, w2, ln)


MOE_TILE = 1024
SC_WINDOW = 128
SC_WORDS = 256


def _pack_words(x):
    bits = lax.bitcast_convert_type(x.astype(bf16).astype(f32), jnp.int32)
    half = D_MODEL // 2
    w = lax.shift_right_logical(bits[:, 0:half], 16) | bits[:, half:D_MODEL]
    return w[:, 0:SC_WORDS], w[:, SC_WORDS:2 * SC_WORDS]


def _unpack_words(wa, wb):
    w = jnp.concatenate([wa, wb], axis=1)
    lo = lax.bitcast_convert_type(lax.shift_left(w, 16), f32)
    hi = lax.bitcast_convert_type(w & jnp.int32(-65536), f32)
    return jnp.concatenate([lo, hi], axis=1)


def _router_body(x_ref, rt_ref, xa_ref, xb_ref, meta_ref, cnt_ref, run_ref):
    tb = x_ref.shape[0]

    @pl.when(pl.program_id(0) == 0)
    def _():
        run_ref[...] = jnp.zeros(run_ref.shape, f32)

    x = x_ref[...]
    xa_ref[...], xb_ref[...] = _pack_words(x)
    logits = _sdot3(x, rt_ref[...])
    lane = lax.broadcasted_iota(jnp.int32, logits.shape, 1)
    lg = jnp.where(lane < N_EXPERTS, logits, NEG)
    v1 = jnp.max(lg, axis=1, keepdims=True)
    i1 = jnp.min(jnp.where(lg == v1, lane, LANES), axis=1, keepdims=True)
    lg2 = jnp.where(lane == i1, NEG, lg)
    v2 = jnp.max(lg2, axis=1, keepdims=True)
    i2 = jnp.min(jnp.where(lg2 == v2, lane, LANES), axis=1, keepdims=True)
    e2 = jnp.exp(v2 - v1)
    sel1, sel2 = lane == i1, lane == i2
    mask = (sel1 | sel2).astype(f32)
    r = lax.broadcasted_iota(jnp.int32, (tb, tb), 0)
    c = lax.broadcasted_iota(jnp.int32, (tb, tb), 1)
    rank = _bdot((c < r).astype(f32), mask) + run_ref[0:1, :]
    run_ref[...] = run_ref[...] + jnp.sum(mask, axis=0, keepdims=True)
    rank1 = jnp.sum(jnp.where(sel1, rank, 0.0), axis=1, keepdims=True)
    rank2 = jnp.sum(jnp.where(sel2, rank, 0.0), axis=1, keepdims=True)
    cols = (i1.astype(f32), i2.astype(f32), rank1, rank2, 1.0 / (1.0 + e2), e2 / (1.0 + e2))
    meta = jnp.zeros(logits.shape, f32)
    for n, col in enumerate(cols):
        meta = jnp.where(lane == n, col, meta)
    meta_ref[...] = meta
    cnt_ref[...] = run_ref[...]


def _router(xt, router, tb=1024):
    T = xt.shape[0]
    tb = min(tb, T)
    return pl.pallas_call(
        _router_body, grid=(T // tb,),
        in_specs=[pl.BlockSpec((tb, D_MODEL), lambda i: (i, 0)), pl.BlockSpec((D_MODEL, LANES), lambda i: (0, 0))],
        out_specs=[pl.BlockSpec((tb, SC_WORDS), lambda i: (i, 0)), pl.BlockSpec((tb, SC_WORDS), lambda i: (i, 0)),
                   pl.BlockSpec((tb, LANES), lambda i: (i, 0)), pl.BlockSpec((8, LANES), lambda i: (0, 0))],
        out_shape=[jax.ShapeDtypeStruct((T, SC_WORDS), jnp.int32), jax.ShapeDtypeStruct((T, SC_WORDS), jnp.int32),
                   jax.ShapeDtypeStruct((T, LANES), f32), jax.ShapeDtypeStruct((8, LANES), f32)],
        scratch_shapes=[pltpu.VMEM((8, LANES), f32)],
        compiler_params=_cparams(("arbitrary",)), name="moe_router")(xt, router)


def _sc_gather(table, idx):
    n = idx.shape[0]
    mesh = plsc.VectorSubcoreMesh(core_axis_name="c", subcore_axis_name="s")

    @functools.partial(pl.kernel, out_type=jax.ShapeDtypeStruct((n, SC_WORDS), table.dtype), mesh=mesh)
    def gather(x_hbm, i_hbm, o_hbm):
        def body(i_vmem, o_vmem):
            pltpu.sync_copy(x_hbm.at[i_vmem.at[0]], o_vmem)

        pltpu.emit_pipeline(
            body, grid=(n // SC_WINDOW,),
            in_specs=[pl.BlockSpec((1, SC_WINDOW), index_map=lambda i: (0, i))],
            out_specs=[pl.BlockSpec((SC_WINDOW, SC_WORDS), index_map=lambda i: (i, 0))],
            core_axis_name=("c", "s"), dimension_semantics=(pltpu.PARALLEL,))(i_hbm, o_hbm)

    return gather(table, idx.reshape(1, n))


def _sc_scatter(rows, idx, n_out):
    R = rows.shape[0]
    n = idx.shape[0]
    nblk = R // SC_WINDOW
    mesh = plsc.VectorSubcoreMesh(core_axis_name="c", subcore_axis_name="s")

    @functools.partial(pl.kernel, out_type=jax.ShapeDtypeStruct((n_out, SC_WORDS), rows.dtype), mesh=mesh,
                       scratch_types=[])
    def scatter(x_hbm, i_hbm, o_hbm):
        def body(x_vmem, i_vmem):
            pltpu.sync_copy(x_vmem, o_hbm.at[i_vmem.at[0]])

        pltpu.emit_pipeline(
            body, grid=(n // SC_WINDOW,),
            in_specs=[pl.BlockSpec((SC_WINDOW, SC_WORDS), index_map=lambda i: (i % nblk, 0)),
                      pl.BlockSpec((1, SC_WINDOW), index_map=lambda i: (0, i))],
            out_specs=[], core_axis_name=("c", "s"), dimension_semantics=(pltpu.PARALLEL,))(x_hbm, i_hbm)

    return scatter(rows, idx.reshape(1, n))


def _experts_body(te_ref, nt_ref, xa_ref, xb_ref, w1_ref, w3_ref, w2_ref, oa_ref, ob_ref, acc_ref, x_ref):
    i = pl.program_id(0)
    j = pl.program_id(1)

    @pl.when(j == 0)
    def _():
        x_ref[...] = _unpack_words(xa_ref[...], xb_ref[...]).astype(bf16)
        acc_ref[...] = jnp.zeros(acc_ref.shape, f32)

    @pl.when(i < nt_ref[0])
    def _():
        x = x_ref[...]
        h1 = jnp.dot(x, w1_ref[0].astype(bf16), preferred_element_type=f32)
        h3 = jnp.dot(x, w3_ref[0].astype(bf16), preferred_element_type=f32)
        acc_ref[...] += _bdot(h1 * _sigmoid(h1) * h3, w2_ref[0])

    @pl.when(j == pl.num_programs(1) - 1)
    def _():
        oa_ref[...], ob_ref[...] = _pack_words(acc_ref[...])


def _experts(xa, xb, tile_expert, n_tiles, w1, w3, w2, tf=512):
    P = xa.shape[0]
    dff = w1.shape[2]
    words = pl.BlockSpec((MOE_TILE, SC_WORDS), lambda i, j, te, nt: (i, 0))
    grid_spec = pltpu.PrefetchScalarGridSpec(
        num_scalar_prefetch=2, grid=(P // MOE_TILE, dff // tf),
        in_specs=[words, words,
                  pl.BlockSpec((1, D_MODEL, tf), lambda i, j, te, nt: (te[i], 0, j)),
                  pl.BlockSpec((1, D_MODEL, tf), lambda i, j, te, nt: (te[i], 0, j)),
                  pl.BlockSpec((1, tf, D_MODEL), lambda i, j, te, nt: (te[i], j, 0))],
        out_specs=[words, words],
        scratch_shapes=[pltpu.VMEM((MOE_TILE, D_MODEL), f32), pltpu.VMEM((MOE_TILE, D_MODEL), bf16)])
    return pl.pallas_call(
        _experts_body, grid_spec=grid_spec, out_shape=[jax.ShapeDtypeStruct((P, SC_WORDS), jnp.int32)] * 2,
        compiler_params=_cparams(("parallel", "arbitrary")), name="moe_experts")(tile_expert, n_tiles, xa, xb, w1, w3, w2)


def _combine_body(x_ref, y0a_ref, y0b_ref, y1a_ref, y1b_ref, meta_ref, ln_ref, o_ref):
    meta = meta_ref[...]
    ff = (meta[:, 4:5] * _unpack_words(y0a_ref[...], y0b_ref[...])
          + meta[:, 5:6] * _unpack_words(y1a_ref[...], y1b_ref[...]))
    o_ref[...] = _layer_norm(ALPHA * x_ref[...] + ff, ln_ref[0:1, :], ln_ref[1:2, :])


def _combine(xt, yga, ygb, meta, ln, tm=1024):
    T = xt.shape[0]
    tm = min(tm, T)
    nb = T // tm
    row = pl.BlockSpec((tm, D_MODEL), lambda i: (i, 0))
    first = pl.BlockSpec((tm, SC_WORDS), lambda i: (i, 0))
    second = pl.BlockSpec((tm, SC_WORDS), lambda i: (nb + i, 0))
    return pl.pallas_call(
        _combine_body, grid=(nb,),
        in_specs=[row, first, first, second, second, pl.BlockSpec((tm, LANES), lambda i: (i, 0)),
                  pl.BlockSpec((2, D_MODEL), lambda i: (0, 0))],
        out_specs=row, out_shape=jax.ShapeDtypeStruct((T, D_MODEL), f32),
        compiler_params=_cparams(("parallel",)), name="moe_combine")(xt, yga, ygb, yga, ygb, meta, ln)


def _moe(xt, router, w1, w3, w2, ln):
    T = xt.shape[0]
    xa, xb, meta, cnt = _router(xt, router)
    counts = cnt[0, :N_EXPERTS].astype(jnp.int32)
    tiles = (counts + MOE_TILE - 1) // MOE_TILE
    tile_end = jnp.cumsum(tiles)
    offset = (tile_end - tiles) * MOE_TILE
    expert = meta[:, 0:2].astype(jnp.int32)
    onehot = expert[:, :, None] == jnp.arange(N_EXPERTS, dtype=jnp.int32)[None, None, :]
    pos = jnp.sum(jnp.where(onehot, offset[None, None, :], 0), axis=2) + meta[:, 2:4].astype(jnp.int32)
    pos = pos.T.reshape(-1)
    P = 2 * T + N_EXPERTS * MOE_TILE
    tile_id = jnp.arange(P // MOE_TILE, dtype=jnp.int32)
    tile_expert = jnp.minimum(jnp.sum((tile_id[:, None] >= tile_end[None, :]).astype(jnp.int32), axis=1), N_EXPERTS - 1)
    ya, yb = _experts(_sc_scatter(xa, pos, P), _sc_scatter(xb, pos, P), tile_expert, tile_end[-1:], w1, w3, w2)
    return _combine(xt, _sc_gather(ya, pos), _sc_gather(yb, pos), meta, ln)


def kernel(x, w_in, b_in, att_gq, att_gk, rw_mix, rw_w0, rw_w2, rw_a0, rw_a2, rw_g2, rw_kk, rw_ka, rw_rk, rw_ln_g, rw_ln_b, s5_lam_re, s5_lam_im, s5_log_dt, s5_b_re, s5_b_im, s5_c_re, s5_c_im, s5_d, s5_glu_w, s5_glu_b, ml_conv_w, ml_conv_b, ml_ib, ml_fb, w_gate, b_gate, w_branch, w_out, ln1_g, ln1_b, ffn_w1, ffn_w3, ffn_w2, moe_router, moe_w1, moe_w3, moe_w2, ln2_g, ln2_b):
    B, S, D = x.shape
    xt = x.reshape(B * S, D)
    cos, sin = _rope_tables(S)
    for l in range(DEPTH):
        gain = jnp.concatenate([jnp.tile(att_gq[l], 8) * (HEAD_DIM ** -0.5), jnp.tile(att_gk[l], 2)])[None, :]
        (q, k, v, s5u, mq, mk, mv, mo, r, rk, rv, an, bn, lw, gate, bonus, gt) = _proj(
            xt, _proj_params(w_in[l], b_in[l], ml_ib[l], ml_fb[l]), (cos, sin, gain),
            (ml_conv_w[l], ml_conv_b[l][None, :]),
            _rw_params(rw_mix[l], rw_w0[l], rw_w2[l], rw_a0[l], rw_a2[l], rw_g2[l], rw_kk[l], rw_ka[l], rw_rk[l]), B, S)
        score_bound = 8.1 * jnp.max(jnp.abs(att_gq[l])) * jnp.max(jnp.abs(att_gk[l]))
        o_att = _flash(q, k, v, score_bound, B, S)
        yf, yb = _rwkv(r, rk, rv, an, bn, lw, B, S)
        y_s5 = _s5(s5u, _s5_params(s5_lam_re[l], s5_lam_im[l], s5_log_dt[l], s5_b_re[l], s5_b_im[l], s5_c_re[l],
                                   s5_c_im[l]), B, S)
        hf, hb = _mlstm(mq, mk, mv, gt, B, S)
        xt = _merge(xt, o_att, (yf, yb, gate, bonus), (y_s5, s5u), (hf, hb, mo), w_gate[l].astype(bf16), b_gate[l],
                    _att_branch_weight(w_branch[l, 0]).astype(bf16), w_branch[l, 1:].astype(bf16),
                    w_out[l].astype(bf16), jnp.stack([ln1_g[l], ln1_b[l]]), jnp.stack([rw_ln_g[l], rw_ln_b[l]]),
                    jnp.stack([s5_d[l], s5_glu_b[l]]), s5_glu_w[l].astype(bf16))
        ln2 = jnp.stack([ln2_g[l], ln2_b[l]])
        if l % 2 == 0:
            xt = _ffn(xt, ffn_w1[l // 2].astype(bf16), ffn_w3[l // 2].astype(bf16), ffn_w2[l // 2].astype(bf16), ln2)
        else:
            router = jnp.pad(moe_router[l // 2], ((0, 0), (0, LANES - N_EXPERTS)))
            xt = _moe(xt, router, moe_w1[l // 2], moe_w3[l // 2], moe_w2[l // 2], ln2)
    return xt.reshape(B, S, D)
```

```python
import functools
import math

import jax
import jax.numpy as jnp
import numpy as np
from jax import lax
from jax.experimental import pallas as pl
from jax.experimental.pallas import tpu as pltpu
from jax.experimental.pallas import tpu_sc as plsc

f32 = jnp.float32
bf16 = jnp.bfloat16

D_MODEL = 1024
DEPTH = 2
GRID_W = 64
BRANCH_W = 256
HEAD_DIM = 64
ATT_HEADS = 4
ATT_KV_HEADS = 2
ROPE_THETA = 10000.0
QK_EPS = 1e-6
RW_GN_EPS = 64e-5
RW_COLS = 1088
S5_GROUP = 16
S5_GROUPS = 16
S5_STATE = 64
ML_HEADS = 4
N_EXPERTS = 8
ALPHA = (2 * DEPTH) ** 0.25
LN_EPS = 1e-5

LANES = 128
CHUNK = 64
NEG = -1e30
VMEM_LIMIT = 56 * 1024 * 1024

PROJ_SPLITS = (768, 256, 512, 256, 256, RW_COLS)


def _cparams(sem):
    return pltpu.CompilerParams(dimension_semantics=sem, vmem_limit_bytes=VMEM_LIMIT)


def _sigmoid(x):
    return 1.0 / (1.0 + jnp.exp(-x))


def _softplus(x):
    return jnp.maximum(x, 0.0) + jnp.log(1.0 + jnp.exp(-jnp.abs(x)))


def _dims(a, lhs_c, rhs_c):
    lead = a.ndim - 2
    batch = tuple(range(lead))
    return (((lhs_c + lead,), (rhs_c + lead,)), (batch, batch))


def _bdot(a, b):
    return lax.dot_general(a.astype(bf16), b.astype(bf16), _dims(a, 1, 0), preferred_element_type=f32)


def _bdot_nt(a, b):
    return lax.dot_general(a.astype(bf16), b.astype(bf16), _dims(a, 1, 1), preferred_element_type=f32)


def _bdot_tn(a, b):
    return lax.dot_general(a.astype(bf16), b.astype(bf16), _dims(a, 0, 0), preferred_element_type=f32)


def _split(x):
    hi = x.astype(bf16)
    return hi, (x - hi.astype(f32)).astype(bf16)


def _sdot(a, b, exact):
    dims = _dims(a, 1, 0)
    if exact == "rhs":
        hi, lo = _split(a)
        bb = b.astype(bf16)
        return (lax.dot_general(hi, bb, dims, preferred_element_type=f32)
                + lax.dot_general(lo, bb, dims, preferred_element_type=f32))
    hi, lo = _split(b)
    ab = a.astype(bf16)
    return (lax.dot_general(ab, hi, dims, preferred_element_type=f32)
            + lax.dot_general(ab, lo, dims, preferred_element_type=f32))


def _sdot3(a, b):
    dims = _dims(a, 1, 0)
    ah, al = _split(a)
    bh, bl = _split(b)
    return (lax.dot_general(ah, bh, dims, preferred_element_type=f32)
            + lax.dot_general(ah, bl, dims, preferred_element_type=f32)
            + lax.dot_general(al, bh, dims, preferred_element_type=f32))


def _seg_matrix(n, seg=HEAD_DIM):
    r = lax.broadcasted_iota(jnp.int32, (n, n), 0) // seg
    c = lax.broadcasted_iota(jnp.int32, (n, n), 1) // seg
    return (r == c).astype(f32)


def _layer_norm(y, g, b):
    mu = jnp.mean(y, axis=-1, keepdims=True)
    d = y - mu
    var = jnp.mean(d * d, axis=-1, keepdims=True)
    return d * lax.rsqrt(var + LN_EPS) * g + b


def _row_to_col(row, eye):
    n = eye.shape[-1]
    return jnp.sum(jnp.where(eye, jnp.broadcast_to(row, (row.shape[0], n, n)), 0.0), axis=2, keepdims=True)


def _stack_heads(x):
    h0 = lax.broadcasted_iota(jnp.int32, x.shape, 2) < HEAD_DIM
    return jnp.concatenate([jnp.where(h0, x, 0.0), jnp.where(h0, 0.0, x)], axis=1)


def _proj_body(x_ref, xp_ref, xn_ref, cos_ref, sin_ref, w_ref, b_ref, wg_ref, bg_ref, gain_ref, cw_ref, cb_ref,
               mix_ref, w2_ref, a2_ref, g2_ref, vec_ref,
               q_ref, k_ref, v_ref, s5_ref, mq_ref, mk_ref, mv_ref, mo_ref, r_ref, rk_ref, rv_ref, an_ref, bn_ref,
               lw_ref, gate_ref, bonus_ref, g_ref, mbuf_ref, rbuf_ref):
    tm = x_ref.shape[0]
    xb, xp, xn = x_ref[...].astype(bf16), xp_ref[...].astype(bf16), xn_ref[...].astype(bf16)
    offs = np.cumsum((0,) + PROJ_SPLITS)

    def cols(rows, n):
        sl = slice(int(offs[n]), int(offs[n + 1]))
        return jnp.dot(rows, w_ref[:, sl], preferred_element_type=f32) + b_ref[:, sl]

    q_ref[...], k_ref[...], v_ref[...] = _att_prep(cols(xb, 0), cos_ref[...], sin_ref[...], gain_ref[...])
    s5_ref[...] = cols(xb, 1)
    _fill_halo(mbuf_ref, cols(xb, 2), cols(xp, 2), cols(xn, 2))
    mq_ref[...], mk_ref[...] = _ml_prep(mbuf_ref, tm, cw_ref[...], cb_ref[...])
    mv_ref[...] = cols(xb, 3)
    mo_ref[...] = cols(xb, 4)
    _fill_halo(rbuf_ref, cols(xb, 5), cols(xp, 5), cols(xn, 5))
    outs = _rw_prep(rbuf_ref, tm, mix_ref[...], w2_ref[...], a2_ref[...], g2_ref[...], vec_ref[...])
    for o_ref, val in zip((r_ref, rk_ref, rv_ref, an_ref, bn_ref), outs[0:5]):
        o_ref[...] = val
    lw_ref[0], lw_ref[1], gate_ref[...], bonus_ref[...] = outs[5:9]
    g_ref[...] = lax.dot_general(wg_ref[...], xb, (((1,), (1,)), ((), ())), preferred_element_type=f32) + bg_ref[...]


def _proj(xt, proj_prm, att_prm, ml_prm, rw_prm, B, S, tm=512):
    T = B * S
    nb = S // tm
    n_tot = sum(PROJ_SPLITS)
    row = lambda n: pl.BlockSpec((tm, n), lambda b, i: (b * nb + i, 0))
    const = lambda a: pl.BlockSpec(a.shape, lambda b, i: (0,) * a.ndim)
    tab = pl.BlockSpec((tm, LANES), lambda b, i: (i, 0))
    f32out = lambda n: jax.ShapeDtypeStruct((T, n), f32)
    consts = list(proj_prm) + [att_prm[2]] + list(ml_prm) + list(rw_prm)
    out_specs = ([row(512), row(LANES), row(2 * LANES)] + [row(BRANCH_W)] * 10
                 + [pl.BlockSpec((2, tm, BRANCH_W), lambda b, i: (0, b * nb + i, 0)), row(BRANCH_W), row(BRANCH_W),
                    pl.BlockSpec((16, tm), lambda b, i: (0, b * nb + i))])
    out_shape = ([jax.ShapeDtypeStruct((T, 512), bf16), jax.ShapeDtypeStruct((T, LANES), bf16),
                  jax.ShapeDtypeStruct((T, 2 * LANES), bf16)] + [f32out(BRANCH_W)] * 10
                 + [jax.ShapeDtypeStruct((2, T, BRANCH_W), f32), f32out(BRANCH_W), f32out(BRANCH_W),
                    jax.ShapeDtypeStruct((16, T), f32)])
    return pl.pallas_call(
        _proj_body, grid=(B, nb),
        in_specs=_halo_specs(D_MODEL, tm, B, S) + [tab, tab] + [const(a) for a in consts],
        out_specs=out_specs, out_shape=out_shape,
        scratch_shapes=[pltpu.VMEM((tm + 16, 512), f32), pltpu.VMEM((tm + 16, RW_COLS), f32)],
        compiler_params=_cparams(("parallel", "parallel")), name="proj")(
            xt, xt, xt, att_prm[0], att_prm[1], *consts)


def _proj_params(w_in, b_in, ml_ib, ml_fb):
    o = np.cumsum((0, 256, 128, 128, RW_COLS, 256, 512, 256, 8, 8, 256))
    sl = lambda i: (w_in[:, o[i]:o[i + 1]], b_in[o[i]:o[i + 1]])
    (wq, bq), (wk, bk), (wv, bv), (wrw, brw), (ws5, bs5), (wqk, bqk), (wmv, bmv), (wi, bi), (wf, bf), (wo, bo) = (
        sl(i) for i in range(10))
    zw, zb = jnp.zeros((D_MODEL, HEAD_DIM), f32), jnp.zeros((HEAD_DIM,), f32)
    wq_e, bq_e = [], []
    for h in range(ATT_HEADS):
        wh, bh = wq[:, 64 * h:64 * h + 64], bq[64 * h:64 * h + 64]
        wq_e += [wh, zw] if h // 2 == 0 else [zw, wh]
        bq_e += [bh, zb] if h // 2 == 0 else [zb, bh]
    w = jnp.concatenate(wq_e + [wk, wv, ws5, wqk, wmv, wo, wrw], axis=1)
    b = jnp.concatenate(bq_e + [bk, bv, bs5, bqk, bmv, bo, brw])
    wg = jnp.concatenate([wi, wf], axis=1).T
    bg = jnp.concatenate([bi + ml_ib.reshape(-1), bf + ml_fb.reshape(-1)])
    return w.astype(bf16), b[None, :], wg.astype(bf16), bg[:, None]


def _att_prep(att, cos, sin, gain):
    x = att[:, 0:640]
    ms = _sdot(x * x, _seg_matrix(640), "rhs") * (1.0 / HEAD_DIM)
    xn = x * lax.rsqrt(ms + QK_EPS) * gain
    lane = lax.broadcasted_iota(jnp.int32, xn.shape, 1)
    partner = jnp.where((lane % 32) < 16, pltpu.roll(xn, 640 - 16, 1), pltpu.roll(xn, 16, 1))
    rot = xn * jnp.concatenate([cos] * 5, axis=1) + partner * jnp.concatenate([sin] * 5, axis=1)
    v = jnp.concatenate([att[:, 640:768].astype(bf16), jnp.ones((x.shape[0], LANES), bf16)], axis=1)
    return rot[:, 0:512].astype(bf16), rot[:, 512:640].astype(bf16), v


def _rope_tables(S):
    t = np.arange(S)
    row = (t // GRID_W).astype(np.float32)
    col = (t % GRID_W).astype(np.float32)
    n = 16
    inv = np.power(np.float32(ROPE_THETA), -np.arange(n, dtype=np.float32) / n).astype(np.float32)
    ar = jnp.asarray(row)[:, None] * jnp.asarray(inv)
    ac = jnp.asarray(col)[:, None] * jnp.asarray(inv)
    cos = jnp.concatenate([jnp.cos(ar), jnp.cos(ar), jnp.cos(ac), jnp.cos(ac)], axis=1)
    sin = jnp.concatenate([-jnp.sin(ar), jnp.sin(ar), -jnp.sin(ac), jnp.sin(ac)], axis=1)
    return jnp.concatenate([cos, cos], axis=1), jnp.concatenate([sin, sin], axis=1)


def _flash_body(q_ref, k_ref, v_ref, o_ref, acc_ref, *m_scratch, tk, track_max):
    tq = q_ref.shape[0]
    nk = k_ref.shape[0] // tk
    q2 = jnp.concatenate([q_ref[:, 0:LANES], q_ref[:, LANES:2 * LANES]], axis=0)
    acc_ref[...] = jnp.zeros(acc_ref.shape, f32)
    if track_max:
        m_ref, = m_scratch
        m_ref[...] = jnp.full(m_ref.shape, NEG, f32)

    def step(j, carry):
        rows = pl.ds(pl.multiple_of(j * tk, tk), tk)
        s = lax.dot_general(q2, k_ref[rows, :], (((1,), (1,)), ((), ())), preferred_element_type=f32)
        if track_max:
            m_old = m_ref[...]
            m_new = jnp.maximum(m_old, jnp.max(s, axis=1, keepdims=True))
            p = jnp.exp(s - m_new).astype(bf16)
            acc_ref[...] = jnp.exp(m_old - m_new) * acc_ref[...] + jnp.dot(p, v_ref[rows, :], preferred_element_type=f32)
            m_ref[...] = m_new
        else:
            acc_ref[...] += jnp.dot(jnp.exp(s).astype(bf16), v_ref[rows, :], preferred_element_type=f32)
        return carry

    lax.fori_loop(0, nk, step, 0)
    o = acc_ref[:, 0:LANES] / acc_ref[:, LANES:2 * LANES]
    o_ref[...] = jnp.concatenate([o[0:tq], o[tq:2 * tq]], axis=1)


SCORE_BOUND_MAX = 60.0


def _flash(q, k, v, score_bound, B, S, tq=256, tk=8192):
    T = B * S
    nb = S // tq
    tk = min(tk, S)

    def call(track_max):
        scratch = [pltpu.VMEM((2 * tq, 2 * LANES), f32)] + ([pltpu.VMEM((2 * tq, 1), f32)] if track_max else [])
        return pl.pallas_call(
            functools.partial(_flash_body, tk=tk, track_max=track_max), grid=(B, ATT_KV_HEADS, nb),
            in_specs=[pl.BlockSpec((tq, 2 * LANES), lambda b, g, i: (b * nb + i, g)),
                      pl.BlockSpec((S, LANES), lambda b, g, i: (b, 0)),
                      pl.BlockSpec((S, 2 * LANES), lambda b, g, i: (b, 0))],
            out_specs=pl.BlockSpec((tq, 2 * LANES), lambda b, g, i: (b * nb + i, g)),
            out_shape=jax.ShapeDtypeStruct((T, 512), f32), scratch_shapes=scratch,
            compiler_params=_cparams(("parallel", "parallel", "parallel")),
            name="flash_safe" if track_max else "flash")(q, k, v)

    return lax.cond(score_bound <= SCORE_BOUND_MAX, lambda: call(False), lambda: call(True))


def _halo_specs(width, tm, B, S):
    nb = S // tm
    r8 = tm // 8
    last8 = B * S // 8 - 1

    def main(b, i):
        return (b * nb + i, 0)

    def prev(b, i):
        return (jnp.maximum(b * (S // 8) + i * r8 - 1, 0), 0)

    def nxt(b, i):
        return (jnp.minimum(b * (S // 8) + (i + 1) * r8, last8), 0)

    return [pl.BlockSpec((tm, width), main), pl.BlockSpec((8, width), prev), pl.BlockSpec((8, width), nxt)]


def _fill_halo(buf_ref, x, prev8, next8):
    tm = x.shape[0]
    i = pl.program_id(1)
    last = pl.num_programs(1) - 1
    buf_ref[pl.ds(8, tm), :] = x
    buf_ref[pl.ds(0, 8), :] = jnp.where(i > 0, prev8, 0.0)
    buf_ref[pl.ds(8 + tm, 8), :] = jnp.where(i < last, next8, 0.0)


def _rw_prep(buf_ref, tm, mix, w2, a2, g2, vec):
    x = buf_ref[pl.ds(8, tm), :]
    p = x + mix[0:1, :] * (buf_ref[pl.ds(7, tm), :] - x) + mix[1:2, :] * (buf_ref[pl.ds(9, tm), :] - x)
    r, k, v = p[:, 0:256], p[:, 256:512], p[:, 512:768]
    w0f, w0b, a0, k_k, k_a, r_k = (vec[j:j + 1, :] for j in range(6))
    dec = _bdot(jnp.tanh(p[:, 768:896]), w2)
    z = p[:, 896:1088]
    a = _sigmoid(a0 + _bdot(z, a2))
    gate = _bdot(_sigmoid(z), g2)
    seg = _seg_matrix(BRANCH_W)
    kk = k * k_k
    kk = kk / jnp.maximum(jnp.sqrt(_sdot(kk * kk, seg, "rhs")), 1e-12)
    k2 = k * (1.0 + (a - 1.0) * k_a)
    bonus = _sdot(r * k2 * r_k, seg, "rhs") * v
    lwf = -jnp.exp(-_softplus(-(w0f + dec[:, 0:256])) - 0.5)
    lwb = -jnp.exp(-_softplus(-(w0b + dec[:, 256:512])) - 0.5)
    return r, k2, v, -kk, kk * a, lwf, lwb, gate, bonus


def _by_direction(x, fwd, bwd, fn):
    h = x.shape[0] // 2
    return jnp.concatenate([fn(x[0:h], fwd), fn(x[h:2 * h], bwd)], axis=0)


def _keep(x, masks):
    return _by_direction(x, masks[0], masks[1], lambda t, m: jnp.where(m, t, 0.0))


def _both(masks, nchain):
    h = nchain // 2
    return jnp.concatenate([jnp.broadcast_to(m.astype(f32), (h,) + m.shape[1:]) for m in masks], axis=0)


def _pair_masks():
    n = 2 * CHUNK
    r = lax.broadcasted_iota(jnp.int32, (1, n, n), 1)
    c = lax.broadcasted_iota(jnp.int32, (1, n, n), 2)
    same = (r // CHUNK) == (c // CHUNK)
    return r, c, same, (same & (c < r), same & (c > r)), (same & (c <= r), same & (c >= r))


def _rw_chunk(st, r, k, v, an, bn, lw):
    L = CHUNK
    N = r.shape[0]
    ri = lax.broadcasted_iota(jnp.int32, (1, L, L), 1)
    ci = lax.broadcasted_iota(jnp.int32, (1, L, L), 2)
    cs = _sdot(_both((ci <= ri, ci >= ri), N), lw, "lhs")
    tot = jnp.sum(lw, axis=1, keepdims=True)
    e_neg = jnp.exp(-cs)
    a2 = _stack_heads(an * jnp.exp(cs - lw))
    r2 = _stack_heads(r * jnp.exp(cs))
    b2 = _stack_heads(bn * e_neg)
    k2 = _stack_heads(k * e_neg)
    v2 = _stack_heads(v)
    rr, cc, _, strict, incl = _pair_masks()
    n2 = 2 * L
    g = _bdot_nt(jnp.concatenate([a2, r2], axis=1), jnp.concatenate([b2, k2], axis=1))
    mab = _keep(g[:, 0:n2, 0:n2], strict)
    mak = _keep(g[:, 0:n2, n2:2 * n2], strict)
    pb = _keep(g[:, n2:2 * n2, 0:n2], incl)
    pk = _keep(g[:, n2:2 * n2, n2:2 * n2], incl)
    eye = rr == cc
    m8 = jnp.where((rr // 8) == (cc // 8), mab, 0.0)
    x = eye.astype(f32) + m8
    p = _bdot(m8, m8)
    x = x + _bdot(x, p)
    p = _bdot(p, p)
    x = x + _bdot(x, p)
    n = 8
    while n < L:
        e = jnp.where(((rr // (2 * n)) == (cc // (2 * n))) & ((rr // n) != (cc // n)), mab, 0.0)
        x = x + _bdot(_bdot(x, e), x)
        n *= 2
    wu = _bdot(x, jnp.concatenate([a2, _bdot(mak, v2)], axis=2))
    pwu = _bdot(pb, wu)
    rh = r2 + pwu[:, :, 0:LANES]
    y2 = pwu[:, :, LANES:2 * LANES] + _bdot(jnp.concatenate([pk, rh], axis=2), jnp.concatenate([v2, st], axis=1))
    y = y2[:, 0:L] + y2[:, L:2 * L]
    gam = _row_to_col(jnp.exp(tot), eye)
    bwu = _bdot_tn(b2, wu)
    st = gam * (st + _bdot(bwu[:, :, 0:LANES], st) + bwu[:, :, LANES:2 * LANES] + _bdot_tn(k2, v2))
    return y, st


def _load_pairs(ref, rows):
    return jnp.concatenate([ref[:, rows, 0:LANES], ref[:, rows, LANES:2 * LANES]], axis=0)


def _store_pairs(ref, rows, y):
    nb = ref.shape[0]
    ref[:, rows, 0:LANES] = y[0:nb]
    ref[:, rows, LANES:2 * LANES] = y[nb:2 * nb]


def _load_both(f_ref, b_ref, rows_f, rows_b):
    return jnp.concatenate([_load_pairs(f_ref, rows_f), _load_pairs(b_ref, rows_b)], axis=0)


def _chunk_rows(cc, nch):
    return (pl.ds(pl.multiple_of(cc * CHUNK, CHUNK), CHUNK),
            pl.ds(pl.multiple_of((nch - 1 - cc) * CHUNK, CHUNK), CHUNK))


def _rw_scan_body(rf_ref, rb_ref, kf_ref, kb_ref, vf_ref, vb_ref, anf_ref, anb_ref, bnf_ref, bnb_ref, lwf_ref, lwb_ref,
                  yf_ref, yb_ref, st_ref):
    nch = rf_ref.shape[1] // CHUNK
    half = st_ref.shape[0] // 2

    @pl.when(pl.program_id(0) == 0)
    def _():
        st_ref[...] = jnp.zeros(st_ref.shape, f32)

    def step(cc, carry):
        rows_f, rows_b = _chunk_rows(cc, nch)
        pairs = ((rf_ref, rb_ref), (kf_ref, kb_ref), (vf_ref, vb_ref), (anf_ref, anb_ref), (bnf_ref, bnb_ref),
                 (lwf_ref.at[0], lwb_ref.at[0]))
        y, st = _rw_chunk(st_ref[...], *(_load_both(f, b, rows_f, rows_b) for f, b in pairs))
        _store_pairs(yf_ref, rows_f, y[0:half])
        _store_pairs(yb_ref, rows_b, y[half:2 * half])
        st_ref[...] = st
        return carry

    lax.fori_loop(0, nch, step, 0)


def _rw_scan(r, k, v, an, bn, lw, B, S, ts=128):
    nb = S // ts
    fwd = pl.BlockSpec((B, ts, BRANCH_W), lambda i: (0, i, 0))
    bwd = pl.BlockSpec((B, ts, BRANCH_W), lambda i: (0, nb - 1 - i, 0))
    out = jax.ShapeDtypeStruct((B, S, BRANCH_W), f32)
    return pl.pallas_call(
        _rw_scan_body, grid=(nb,),
        in_specs=[fwd, bwd] * 5 + [pl.BlockSpec((1, B, ts, BRANCH_W), lambda i: (0, 0, i, 0)),
                                   pl.BlockSpec((1, B, ts, BRANCH_W), lambda i: (1, 0, nb - 1 - i, 0))],
        out_specs=[fwd, bwd], out_shape=[out, out],
        scratch_shapes=[pltpu.VMEM((2 * B * BRANCH_W // LANES, LANES, LANES), f32)],
        compiler_params=_cparams(("arbitrary",)), name="rw_scan")(r, r, k, k, v, v, an, an, bn, bn, lw, lw)


def _rw_finish(y, gate, bonus, gn):
    seg = _seg_matrix(BRANCH_W)
    mu = _sdot(y, seg, "rhs") * (1.0 / HEAD_DIM)
    d = y - mu
    var = _sdot(d * d, seg, "rhs") * (1.0 / HEAD_DIM)
    yn = d * lax.rsqrt(var + RW_GN_EPS) * gn[0:1, :] + gn[1:2, :]
    return (yn + bonus) * gate


def _rwkv(r, k, v, an, bn, lw, B, S):
    seq = [t.reshape(B, S, BRANCH_W) for t in (r, k, v, an, bn)] + [lw.reshape(2, B, S, BRANCH_W)]
    yf, yb = _rw_scan(*seq, B, S)
    return yf.reshape(B * S, BRANCH_W), yb.reshape(B * S, BRANCH_W)


def _rw_params(mix, w0, w2, a0, a2, g2, k_k, k_a, r_k):
    z = jnp.zeros((64, 256), f32)
    w2c = jnp.concatenate([jnp.concatenate([w2[0], z], axis=1), jnp.concatenate([z, w2[1]], axis=1)], axis=0)
    a2p = jnp.concatenate([a2, jnp.zeros((128, 256), f32)], axis=0)
    g2p = jnp.concatenate([jnp.zeros((64, 256), f32), g2], axis=0)
    vec = jnp.stack([w0[0], w0[1], a0, k_k, k_a, r_k.reshape(-1), jnp.zeros_like(a0), jnp.zeros_like(a0)])
    return mix, w2c.astype(bf16), a2p.astype(bf16), g2p.astype(bf16), vec


S5_HALF = S5_GROUPS * S5_STATE // 2


def _s5_scan_body(uf_ref, ub_ref, bh_ref, cre_ref, cim_ref, lre_ref, lim_ref, yf_ref, yb_ref,
                  sre_ref, sim_ref, bre_ref, bim_ref):
    ts = uf_ref.shape[0]
    n = S5_HALF

    @pl.when(pl.program_id(0) == 0)
    def _():
        sre_ref[...] = jnp.zeros(sre_ref.shape, f32)
        sim_ref[...] = jnp.zeros(sim_ref.shape, f32)

    row_half = lax.broadcasted_iota(jnp.int32, uf_ref.shape, 1) // 4
    lane_half = lax.broadcasted_iota(jnp.int32, uf_ref.shape, 2) // (BRANCH_W // 2)
    for d, u_ref in enumerate((uf_ref, ub_ref)):
        lhs = jnp.where(row_half == lane_half, u_ref[...], 0.0).reshape(ts * 8, BRANCH_W)
        bu = _bdot(lhs, bh_ref[d]).reshape(ts, 8, 2 * n)
        bre_ref[d] = bu[:, :, 0:n]
        bim_ref[d] = bu[:, :, n:2 * n]
    lre = lre_ref[...]
    lim = lim_ref[...]

    def step(t, carry):
        out = []
        for d, td in enumerate((t, ts - 1 - t)):
            sre, sim = carry[2 * d], carry[2 * d + 1]
            nre = lre[d] * sre - lim[d] * sim + bre_ref[d, td]
            nim = lre[d] * sim + lim[d] * sre + bim_ref[d, td]
            bre_ref[d, td] = nre
            bim_ref[d, td] = nim
            out += [nre, nim]
        return tuple(out)

    s = lax.fori_loop(0, ts, step, (sre_ref[0], sim_ref[0], sre_ref[1], sim_ref[1]), unroll=4)
    sre_ref[0], sim_ref[0], sre_ref[1], sim_ref[1] = s
    low = lane_half == 0
    for d, y_ref in enumerate((yf_ref, yb_ref)):
        yv = (_bdot(bre_ref[d].reshape(ts * 8, n), cre_ref[d])
              - _bdot(bim_ref[d].reshape(ts * 8, n), cim_ref[d])).reshape(ts, 8, BRANCH_W)
        y_ref[...] = jnp.where(low, yv, pltpu.roll(yv, 4, 1))


def _s5_scan(u8, bh, cre, cim, lre, lim, ts=128):
    S = u8.shape[0]
    nb = S // ts
    n = S5_HALF
    full = lambda shape: pl.BlockSpec(shape, lambda i: (0,) * len(shape))
    fwd = pl.BlockSpec((ts, 8, BRANCH_W), lambda i: (i, 0, 0))
    bwd = pl.BlockSpec((ts, 8, BRANCH_W), lambda i: (nb - 1 - i, 0, 0))
    out = jax.ShapeDtypeStruct((S, 8, BRANCH_W), f32)
    return pl.pallas_call(
        _s5_scan_body, grid=(nb,),
        in_specs=[fwd, bwd, full((2, BRANCH_W, 2 * n)), full((2, n, BRANCH_W)), full((2, n, BRANCH_W)),
                  full((2, 8, n)), full((2, 8, n))],
        out_specs=[fwd, bwd], out_shape=[out, out],
        scratch_shapes=[pltpu.VMEM((2, 8, n), f32), pltpu.VMEM((2, 8, n), f32),
                        pltpu.VMEM((2, ts, 8, n), f32), pltpu.VMEM((2, ts, 8, n), f32)],
        compiler_params=_cparams(("arbitrary",)), name="s5_scan")(u8, u8, bh, cre, cim, lre, lim)


def _s5_finish(y, u, vec, w):
    y = y + u * vec[0:1, :]
    y = 0.5 * y * (1.0 + jnp.tanh(math.sqrt(2.0 / math.pi) * (y + 0.044715 * (y * y * y))))
    return y * _sigmoid(_bdot(y, w) + vec[1:2, :])


def _s5_params(lam_re, lam_im, log_dt, b_re, b_im, c_re, c_im):
    G, P, C = S5_GROUPS, S5_STATE, S5_GROUP
    H = G // 2
    eye = jnp.eye(H, dtype=f32)
    b_c = lax.complex(b_re, b_im)

    def b_part(x):
        return jnp.einsum('ab,hapc->hacbp', eye, x.reshape(2, H, P, C)).reshape(G * C, H * P)

    def c_part(x):
        return jnp.einsum('ab,hacp->aphbc', eye, x.reshape(2, H, C, P)).reshape(H * P, G * C)

    def rows(x):
        return jnp.broadcast_to(x.reshape(2, 1, H * P), (2, 4, H * P)).reshape(8, H * P)

    bh, cre, cim, lre, lim = [], [], [], [], []
    for d in range(2):
        lam = lax.complex(jnp.minimum(lam_re[d], -1e-4), lam_im[d])
        lam_bar = jnp.exp(lam * jnp.exp(log_dt[d])[:, None])
        b_bar = ((lam_bar - 1.0) / lam)[..., None] * b_c
        bh.append(jnp.concatenate([b_part(jnp.real(b_bar)), b_part(jnp.imag(b_bar))], axis=1))
        cre.append(c_part(c_re[d]))
        cim.append(c_part(c_im[d]))
        lre.append(rows(jnp.real(lam_bar)))
        lim.append(rows(jnp.imag(lam_bar)))
    return (jnp.stack(bh).astype(bf16), jnp.stack(cre).astype(bf16), jnp.stack(cim).astype(bf16),
            jnp.stack(lre), jnp.stack(lim))


def _s5(u, prm, B, S):
    u3 = u.reshape(B, S, BRANCH_W).transpose(1, 0, 2)
    yf, yb = _s5_scan(jnp.concatenate([u3, u3], axis=1), *prm)
    return (yf[:, 0:B] + yb[:, 0:B]).transpose(1, 0, 2).reshape(B * S, BRANCH_W)


def _ml_prep(buf_ref, tm, w, b):
    y = b
    for j in range(5):
        y = y + w[j:j + 1, :] * buf_ref[pl.ds(6 + j, tm), :]
    y = y * _sigmoid(y)
    return y[:, 0:BRANCH_W], y[:, BRANCH_W:2 * BRANCH_W] * (HEAD_DIM ** -0.5)


def _ml_chunk(state, q, k, v, li, lfp):
    cn, m_row = state
    L = CHUNK
    N = q.shape[0]
    rr, cc, same, _, incl = _pair_masks()
    eye = rr == cc
    lane = lax.broadcasted_iota(jnp.int32, (1, 1, 2 * L), 2)
    lf = jnp.minimum(lfp, 0.0) - jnp.log(1.0 + jnp.exp(-jnp.abs(lfp)))
    lf8 = jnp.broadcast_to(lf, (N, 8, 2 * L))
    b_row = _sdot(lf8, _both((incl[1], incl[0]), N), "rhs")[:, 0:1]
    g_row = _sdot(lf8, _both((same, same), N), "rhs")[:, 0:1]
    w_end = g_row - b_row + li
    m0 = jnp.max(jnp.where(lane < L, w_end, NEG), axis=2, keepdims=True)
    m1 = jnp.max(jnp.where(lane < L, NEG, w_end), axis=2, keepdims=True)
    m_loc = jnp.where(lane < L, m0, m1)
    e_col = _row_to_col(jnp.exp(w_end - m_loc), eye)
    b_col = _row_to_col(b_row, eye)
    q2, k2, v2 = _stack_heads(q), _stack_heads(k), _stack_heads(v)
    v1 = jnp.concatenate([v2, _stack_heads(jnp.ones_like(v))], axis=2)
    log_inter = b_col + _row_to_col(m_row, eye)
    log_intra = _by_direction(b_col - b_row + li, incl[0], incl[1], lambda t, m: jnp.where(m, t, NEG))
    m_r = jnp.maximum(log_inter, jnp.max(log_intra, axis=2, keepdims=True))
    s = _bdot_nt(q2, k2) * jnp.exp(log_intra - m_r)
    inter = jnp.exp(log_inter - m_r)
    nd = _bdot(s, v1) + inter * _bdot(q2, cn)
    h2 = nd[:, :, 0:LANES] / jnp.maximum(jnp.abs(nd[:, :, LANES:2 * LANES]), jnp.exp(-m_r))
    h = h2[:, 0:L] + h2[:, L:2 * L]
    m_new = jnp.maximum(g_row + m_row, m_loc)
    a = jnp.exp(g_row + m_row - m_new)
    bb = jnp.exp(m_loc - m_new)
    cn = jnp.concatenate([a, a], axis=2) * cn + jnp.concatenate([bb, bb], axis=2) * _bdot_tn(e_col * k2, v1)
    return h, (cn, m_new)


GATE_ROWS = 8


def _ml_scan_body(qf_ref, qb_ref, kf_ref, kb_ref, vf_ref, vb_ref, gf_ref, gb_ref, hf_ref, hb_ref, c_ref, m_ref):
    nch = qf_ref.shape[1] // CHUNK
    half = c_ref.shape[0] // 2
    i = pl.program_id(0)
    nb = pl.num_programs(0)
    per = GATE_ROWS // nch
    base_f = (i % per) * nch
    base_b = ((nb - 1 - i) % per) * nch

    @pl.when(i == 0)
    def _():
        c_ref[...] = jnp.zeros(c_ref.shape, f32)
        m_ref[...] = jnp.zeros(m_ref.shape, f32)

    def step(cc, carry):
        rows_f, rows_b = _chunk_rows(cc, nch)
        gate = lambda t: jnp.concatenate(
            [gf_ref[t, 0, 0, :, pl.ds(base_f + cc, 1), :], gf_ref[t, 0, 1, :, pl.ds(base_f + cc, 1), :],
             gb_ref[t, 0, 0, :, pl.ds(base_b + nch - 1 - cc, 1), :], gb_ref[t, 0, 1, :, pl.ds(base_b + nch - 1 - cc, 1), :]],
            axis=0)
        h, (cn, m_row) = _ml_chunk((c_ref[...], m_ref[...]), _load_both(qf_ref, qb_ref, rows_f, rows_b),
                                   _load_both(kf_ref, kb_ref, rows_f, rows_b), _load_both(vf_ref, vb_ref, rows_f, rows_b),
                                   gate(0), gate(1))
        _store_pairs(hf_ref, rows_f, h[0:half])
        _store_pairs(hb_ref, rows_b, h[half:2 * half])
        c_ref[...] = cn
        m_ref[...] = m_row
        return carry

    lax.fori_loop(0, nch, step, 0)


def _ml_scan(q, k, v, g, B, S, ts=256):
    nb = S // ts
    per = GATE_ROWS * CHUNK // ts
    nchain = 2 * B * BRANCH_W // LANES
    fwd = pl.BlockSpec((B, ts, BRANCH_W), lambda i: (0, i, 0))
    bwd = pl.BlockSpec((B, ts, BRANCH_W), lambda i: (0, nb - 1 - i, 0))
    out = jax.ShapeDtypeStruct((B, S, BRANCH_W), f32)
    return pl.pallas_call(
        _ml_scan_body, grid=(nb,),
        in_specs=[fwd, bwd] * 3 + [pl.BlockSpec((2, 1, 2, B, GATE_ROWS, LANES), lambda i: (0, 0, 0, 0, i // per, 0)),
                                   pl.BlockSpec((2, 1, 2, B, GATE_ROWS, LANES),
                                                lambda i: (0, 1, 0, 0, (nb - 1 - i) // per, 0))],
        out_specs=[fwd, bwd], out_shape=[out, out],
        scratch_shapes=[pltpu.VMEM((nchain, LANES, 2 * LANES), f32), pltpu.VMEM((nchain, 1, LANES), f32)],
        compiler_params=_cparams(("arbitrary",)), name="ml_scan")(q, q, k, k, v, v, g, g)


def _mlstm(q, k, mv, gt, B, S):
    g = gt.reshape(2, 2, 2, 2, B, S // CHUNK, CHUNK).transpose(0, 1, 2, 4, 5, 3, 6).reshape(2, 2, 2, B, S // CHUNK, LANES)
    hf, hb = _ml_scan(*(t.reshape(B, S, BRANCH_W) for t in (q, k, mv)), g, B, S)
    return hf.reshape(B * S, BRANCH_W), hb.reshape(B * S, BRANCH_W)


def _merge_body(x_ref, att_ref, yf_ref, yb_ref, rg_ref, rb_ref, sy_ref, su_ref, hf_ref, hb_ref, mo_ref,
                wg_ref, bg_ref, wba_ref, wb_ref, wo_ref, ln_ref, gn_ref, sv_ref, sw_ref, o_ref):
    x = x_ref[...]
    xb = x.astype(bf16)
    rw = _rw_finish(yf_ref[...] + yb_ref[...], rg_ref[...], rb_ref[...], gn_ref[...])
    s5 = _s5_finish(sy_ref[...], su_ref[...], sv_ref[...], sw_ref[...])
    ml = _sigmoid(mo_ref[...]) * (hf_ref[...] + hb_ref[...])
    branches = (att_ref[...], rw, s5, ml)
    merged = None
    for n in range(4):
        gate = _sigmoid(jnp.dot(xb, wg_ref[n], preferred_element_type=f32) + bg_ref[n:n + 1, :])
        wide = _bdot(branches[n], wba_ref[...] if n == 0 else wb_ref[n - 1])
        merged = gate * wide if merged is None else merged + gate * wide
    y = ALPHA * x + _bdot(merged, wo_ref[...])
    o_ref[...] = _layer_norm(y, ln_ref[0:1, :], ln_ref[1:2, :])


def _merge(xt, att, rw_parts, s5_parts, ml_parts, wg, bg, wba, wb, wo, ln, gn, s5_vec, s5_w, tm=256):
    T = xt.shape[0]
    row = lambda n: pl.BlockSpec((tm, n), lambda i: (i, 0))
    const = lambda shape: pl.BlockSpec(shape, lambda i: (0,) * len(shape), pipeline_mode=pl.Buffered(1))
    return pl.pallas_call(
        _merge_body, grid=(T // tm,),
        in_specs=[row(D_MODEL), row(512)] + [row(BRANCH_W)] * 9
        + [const((4, D_MODEL, D_MODEL)), const((4, D_MODEL)), const((512, D_MODEL)), const((3, BRANCH_W, D_MODEL)),
           const((D_MODEL, D_MODEL)), const((2, D_MODEL)), const((2, BRANCH_W)), const((2, BRANCH_W)),
           const((BRANCH_W, BRANCH_W))],
        out_specs=row(D_MODEL), out_shape=jax.ShapeDtypeStruct((T, D_MODEL), f32),
        compiler_params=_cparams(("parallel",)), name="merge")(
            xt, att, *rw_parts, *s5_parts, *ml_parts, wg, bg, wba, wb, wo, ln, gn, s5_vec, s5_w)


def _att_branch_weight(wb):
    z = jnp.zeros((HEAD_DIM, D_MODEL), f32)
    parts = []
    for h in range(ATT_HEADS):
        wh = wb[64 * h:64 * h + 64]
        parts += [wh, z] if h // 2 == 0 else [z, wh]
    return jnp.concatenate(parts, axis=0)


def _ffn_body(x_ref, w1_ref, w3_ref, w2_ref, ln_ref, o_ref, acc_ref):
    j = pl.program_id(1)

    @pl.when(j == 0)
    def _():
        acc_ref[...] = jnp.zeros(acc_ref.shape, f32)

    xb = x_ref[...].astype(bf16)
    h1 = jnp.dot(xb, w1_ref[...], preferred_element_type=f32)
    h3 = jnp.dot(xb, w3_ref[...], preferred_element_type=f32)
    acc_ref[...] += _bdot(h1 * _sigmoid(h1) * h3, w2_ref[...])

    @pl.when(j == pl.num_programs(1) - 1)
    def _():
        o_ref[...] = _layer_norm(ALPHA * x_ref[...] + acc_ref[...], ln_ref[0:1, :], ln_ref[1:2, :])


def _ffn(xt, w1, w3, w2, ln, tm=512, tf=1408):
    T = xt.shape[0]
    dff = w1.shape[1]
    return pl.pallas_call(
        _ffn_body, grid=(T // tm, dff // tf),
        in_specs=[pl.BlockSpec((tm, D_MODEL), lambda i, j: (i, 0)), pl.BlockSpec((D_MODEL, tf), lambda i, j: (0, j)),
                  pl.BlockSpec((D_MODEL, tf), lambda i, j: (0, j)), pl.BlockSpec((tf, D_MODEL), lambda i, j: (j, 0)),
                  pl.BlockSpec((2, D_MODEL), lambda i, j: (0, 0))],
        out_specs=pl.BlockSpec((tm, D_MODEL), lambda i, j: (i, 0)),
        out_shape=jax.ShapeDtypeStruct((T, D_MODEL), f32), scratch_shapes=[pltpu.VMEM((tm, D_MODEL), f32)],
        compiler_params=_cparams(("parallel", "arbitrary")), name="ffn")(xt, w1, w3, w2, ln)


MOE_TILE = 1024
SC_WINDOW = 128
SC_WORDS = 256


def _pack_words(x):
    bits = lax.bitcast_convert_type(x.astype(bf16).astype(f32), jnp.int32)
    half = D_MODEL // 2
    w = lax.shift_right_logical(bits[:, 0:half], 16) | bits[:, half:D_MODEL]
    return w[:, 0:SC_WORDS], w[:, SC_WORDS:2 * SC_WORDS]


def _unpack_words(wa, wb):
    w = jnp.concatenate([wa, wb], axis=1)
    lo = lax.bitcast_convert_type(lax.shift_left(w, 16), f32)
    hi = lax.bitcast_convert_type(w & jnp.int32(-65536), f32)
    return jnp.concatenate([lo, hi], axis=1)


def _router_body(x_ref, rt_ref, xa_ref, xb_ref, meta_ref, cnt_ref, run_ref):
    tb = x_ref.shape[0]

    @pl.when(pl.program_id(0) == 0)
    def _():
        run_ref[...] = jnp.zeros(run_ref.shape, f32)

    x = x_ref[...]
    xa_ref[...], xb_ref[...] = _pack_words(x)
    logits = _sdot3(x, rt_ref[...])
    lane = lax.broadcasted_iota(jnp.int32, logits.shape, 1)
    lg = jnp.where(lane < N_EXPERTS, logits, NEG)
    v1 = jnp.max(lg, axis=1, keepdims=True)
    i1 = jnp.min(jnp.where(lg == v1, lane, LANES), axis=1, keepdims=True)
    lg2 = jnp.where(lane == i1, NEG, lg)
    v2 = jnp.max(lg2, axis=1, keepdims=True)
    i2 = jnp.min(jnp.where(lg2 == v2, lane, LANES), axis=1, keepdims=True)
    e2 = jnp.exp(v2 - v1)
    sel1, sel2 = lane == i1, lane == i2
    mask = (sel1 | sel2).astype(f32)
    r = lax.broadcasted_iota(jnp.int32, (tb, tb), 0)
    c = lax.broadcasted_iota(jnp.int32, (tb, tb), 1)
    rank = _bdot((c < r).astype(f32), mask) + run_ref[0:1, :]
    run_ref[...] = run_ref[...] + jnp.sum(mask, axis=0, keepdims=True)
    rank1 = jnp.sum(jnp.where(sel1, rank, 0.0), axis=1, keepdims=True)
    rank2 = jnp.sum(jnp.where(sel2, rank, 0.0), axis=1, keepdims=True)
    cols = (i1.astype(f32), i2.astype(f32), rank1, rank2, 1.0 / (1.0 + e2), e2 / (1.0 + e2))
    meta = jnp.zeros(logits.shape, f32)
    for n, col in enumerate(cols):
        meta = jnp.where(lane == n, col, meta)
    meta_ref[...] = meta
    cnt_ref[...] = run_ref[...]


def _router(xt, router, tb=1024):
    T = xt.shape[0]
    tb = min(tb, T)
    return pl.pallas_call(
        _router_body, grid=(T // tb,),
        in_specs=[pl.BlockSpec((tb, D_MODEL), lambda i: (i, 0)), pl.BlockSpec((D_MODEL, LANES), lambda i: (0, 0))],
        out_specs=[pl.BlockSpec((tb, SC_WORDS), lambda i: (i, 0)), pl.BlockSpec((tb, SC_WORDS), lambda i: (i, 0)),
                   pl.BlockSpec((tb, LANES), lambda i: (i, 0)), pl.BlockSpec((8, LANES), lambda i: (0, 0))],
        out_shape=[jax.ShapeDtypeStruct((T, SC_WORDS), jnp.int32), jax.ShapeDtypeStruct((T, SC_WORDS), jnp.int32),
                   jax.ShapeDtypeStruct((T, LANES), f32), jax.ShapeDtypeStruct((8, LANES), f32)],
        scratch_shapes=[pltpu.VMEM((8, LANES), f32)],
        compiler_params=_cparams(("arbitrary",)), name="moe_router")(xt, router)


def _sc_gather(table, idx):
    n = idx.shape[0]
    mesh = plsc.VectorSubcoreMesh(core_axis_name="c", subcore_axis_name="s")

    @functools.partial(pl.kernel, out_type=jax.ShapeDtypeStruct((n, SC_WORDS), table.dtype), mesh=mesh)
    def gather(x_hbm, i_hbm, o_hbm):
        def body(i_vmem, o_vmem):
            pltpu.sync_copy(x_hbm.at[i_vmem.at[0]], o_vmem)

        pltpu.emit_pipeline(
            body, grid=(n // SC_WINDOW,),
            in_specs=[pl.BlockSpec((1, SC_WINDOW), index_map=lambda i: (0, i))],
            out_specs=[pl.BlockSpec((SC_WINDOW, SC_WORDS), index_map=lambda i: (i, 0))],
            core_axis_name=("c", "s"), dimension_semantics=(pltpu.PARALLEL,))(i_hbm, o_hbm)

    return gather(table, idx.reshape(1, n))


def _sc_scatter(rows, idx, n_out):
    R = rows.shape[0]
    n = idx.shape[0]
    nblk = R // SC_WINDOW
    mesh = plsc.VectorSubcoreMesh(core_axis_name="c", subcore_axis_name="s")

    @functools.partial(pl.kernel, out_type=jax.ShapeDtypeStruct((n_out, SC_WORDS), rows.dtype), mesh=mesh,
                       scratch_types=[])
    def scatter(x_hbm, i_hbm, o_hbm):
        def body(x_vmem, i_vmem):
            pltpu.sync_copy(x_vmem, o_hbm.at[i_vmem.at[0]])

        pltpu.emit_pipeline(
            body, grid=(n // SC_WINDOW,),
            in_specs=[pl.BlockSpec((SC_WINDOW, SC_WORDS), index_map=lambda i: (i % nblk, 0)),
                      pl.BlockSpec((1, SC_WINDOW), index_map=lambda i: (0, i))],
            out_specs=[], core_axis_name=("c", "s"), dimension_semantics=(pltpu.PARALLEL,))(x_hbm, i_hbm)

    return scatter(rows, idx.reshape(1, n))


def _experts_body(te_ref, rows_ref, xa_ref, xb_ref, w1_ref, w3_ref, w2_ref, oa_ref, ob_ref, acc_ref, x_ref):
    i = pl.program_id(0)
    j = pl.program_id(1)

    @pl.when(j == 0)
    def _():
        valid = lax.broadcasted_iota(jnp.int32, (MOE_TILE, 1), 0) < rows_ref[i]
        x_ref[...] = jnp.where(valid, _unpack_words(xa_ref[...], xb_ref[...]), 0.0).astype(bf16)
        acc_ref[...] = jnp.zeros(acc_ref.shape, f32)

    @pl.when(rows_ref[i] > 0)
    def _():
        x = x_ref[...]
        h1 = jnp.dot(x, w1_ref[0].astype(bf16), preferred_element_type=f32)
        h3 = jnp.dot(x, w3_ref[0].astype(bf16), preferred_element_type=f32)
        acc_ref[...] += _bdot(h1 * _sigmoid(h1) * h3, w2_ref[0])

    @pl.when(j == pl.num_programs(1) - 1)
    def _():
        oa_ref[...], ob_ref[...] = _pack_words(acc_ref[...])


def _experts(xa, xb, tile_expert, tile_rows, w1, w3, w2, tf=512):
    P = xa.shape[0]
    dff = w1.shape[2]
    words = pl.BlockSpec((MOE_TILE, SC_WORDS), lambda i, j, te, nt: (i, 0))
    grid_spec = pltpu.PrefetchScalarGridSpec(
        num_scalar_prefetch=2, grid=(P // MOE_TILE, dff // tf),
        in_specs=[words, words,
                  pl.BlockSpec((1, D_MODEL, tf), lambda i, j, te, nt: (te[i], 0, j)),
                  pl.BlockSpec((1, D_MODEL, tf), lambda i, j, te, nt: (te[i], 0, j)),
                  pl.BlockSpec((1, tf, D_MODEL), lambda i, j, te, nt: (te[i], j, 0))],
        out_specs=[words, words],
        scratch_shapes=[pltpu.VMEM((MOE_TILE, D_MODEL), f32), pltpu.VMEM((MOE_TILE, D_MODEL), bf16)])
    return pl.pallas_call(
        _experts_body, grid_spec=grid_spec, out_shape=[jax.ShapeDtypeStruct((P, SC_WORDS), jnp.int32)] * 2,
        compiler_params=_cparams(("parallel", "arbitrary")), name="moe_experts")(tile_expert, tile_rows, xa, xb, w1, w3, w2)


def _combine_body(x_ref, y0a_ref, y0b_ref, y1a_ref, y1b_ref, meta_ref, ln_ref, o_ref):
    meta = meta_ref[...]
    ff = (meta[:, 4:5] * _unpack_words(y0a_ref[...], y0b_ref[...])
          + meta[:, 5:6] * _unpack_words(y1a_ref[...], y1b_ref[...]))
    o_ref[...] = _layer_norm(ALPHA * x_ref[...] + ff, ln_ref[0:1, :], ln_ref[1:2, :])


def _combine(xt, yga, ygb, meta, ln, tm=1024):
    T = xt.shape[0]
    tm = min(tm, T)
    nb = T // tm
    row = pl.BlockSpec((tm, D_MODEL), lambda i: (i, 0))
    first = pl.BlockSpec((tm, SC_WORDS), lambda i: (i, 0))
    second = pl.BlockSpec((tm, SC_WORDS), lambda i: (nb + i, 0))
    return pl.pallas_call(
        _combine_body, grid=(nb,),
        in_specs=[row, first, first, second, second, pl.BlockSpec((tm, LANES), lambda i: (i, 0)),
                  pl.BlockSpec((2, D_MODEL), lambda i: (0, 0))],
        out_specs=row, out_shape=jax.ShapeDtypeStruct((T, D_MODEL), f32),
        compiler_params=_cparams(("parallel",)), name="moe_combine")(xt, yga, ygb, yga, ygb, meta, ln)


def _moe(xt, router, w1, w3, w2, ln):
    T = xt.shape[0]
    xa, xb, meta, cnt = _router(xt, router)
    counts = cnt[0, :N_EXPERTS].astype(jnp.int32)
    tiles = (counts + MOE_TILE - 1) // MOE_TILE
    tile_end = jnp.cumsum(tiles)
    offset = (tile_end - tiles) * MOE_TILE
    expert = meta[:, 0:2].astype(jnp.int32)
    onehot = expert[:, :, None] == jnp.arange(N_EXPERTS, dtype=jnp.int32)[None, None, :]
    pos = jnp.sum(jnp.where(onehot, offset[None, None, :], 0), axis=2) + meta[:, 2:4].astype(jnp.int32)
    pos = pos.T.reshape(-1)
    P = 2 * T + N_EXPERTS * MOE_TILE
    tile_id = jnp.arange(P // MOE_TILE, dtype=jnp.int32)
    tile_expert = jnp.minimum(jnp.sum((tile_id[:, None] >= tile_end[None, :]).astype(jnp.int32), axis=1), N_EXPERTS - 1)
    first_tile = (tile_end - tiles)[tile_expert]
    tile_rows = jnp.clip(counts[tile_expert] - (tile_id - first_tile) * MOE_TILE, 0, MOE_TILE)
    ya, yb = _experts(_sc_scatter(xa, pos, P), _sc_scatter(xb, pos, P), tile_expert, tile_rows, w1, w3, w2)
    return _combine(xt, _sc_gather(ya, pos), _sc_gather(yb, pos), meta, ln)


def kernel(x, w_in, b_in, att_gq, att_gk, rw_mix, rw_w0, rw_w2, rw_a0, rw_a2, rw_g2, rw_kk, rw_ka, rw_rk, rw_ln_g, rw_ln_b, s5_lam_re, s5_lam_im, s5_log_dt, s5_b_re, s5_b_im, s5_c_re, s5_c_im, s5_d, s5_glu_w, s5_glu_b, ml_conv_w, ml_conv_b, ml_ib, ml_fb, w_gate, b_gate, w_branch, w_out, ln1_g, ln1_b, ffn_w1, ffn_w3, ffn_w2, moe_router, moe_w1, moe_w3, moe_w2, ln2_g, ln2_b):
    B, S, D = x.shape
    assert 2 * B == 8 and D == D_MODEL and S % 512 == 0 and w_in.shape[0] == DEPTH, (x.shape, w_in.shape)
    xt = x.reshape(B * S, D)
    cos, sin = _rope_tables(S)
    for l in range(DEPTH):
        gain = jnp.concatenate([jnp.tile(att_gq[l], 8) * (HEAD_DIM ** -0.5), jnp.tile(att_gk[l], 2)])[None, :]
        (q, k, v, s5u, mq, mk, mv, mo, r, rk, rv, an, bn, lw, gate, bonus, gt) = _proj(
            xt, _proj_params(w_in[l], b_in[l], ml_ib[l], ml_fb[l]), (cos, sin, gain),
            (ml_conv_w[l], ml_conv_b[l][None, :]),
            _rw_params(rw_mix[l], rw_w0[l], rw_w2[l], rw_a0[l], rw_a2[l], rw_g2[l], rw_kk[l], rw_ka[l], rw_rk[l]), B, S)
        score_bound = 8.1 * jnp.max(jnp.abs(att_gq[l])) * jnp.max(jnp.abs(att_gk[l]))
        o_att = _flash(q, k, v, score_bound, B, S)
        yf, yb = _rwkv(r, rk, rv, an, bn, lw, B, S)
        y_s5 = _s5(s5u, _s5_params(s5_lam_re[l], s5_lam_im[l], s5_log_dt[l], s5_b_re[l], s5_b_im[l], s5_c_re[l],
                                   s5_c_im[l]), B, S)
        hf, hb = _mlstm(mq, mk, mv, gt, B, S)
        xt = _merge(xt, o_att, (yf, yb, gate, bonus), (y_s5, s5u), (hf, hb, mo), w_gate[l].astype(bf16), b_gate[l],
                    _att_branch_weight(w_branch[l, 0]).astype(bf16), w_branch[l, 1:].astype(bf16),
                    w_out[l].astype(bf16), jnp.stack([ln1_g[l], ln1_b[l]]), jnp.stack([rw_ln_g[l], rw_ln_b[l]]),
                    jnp.stack([s5_d[l], s5_glu_b[l]]), s5_glu_w[l].astype(bf16))
        ln2 = jnp.stack([ln2_g[l], ln2_b[l]])
        if l % 2 == 0:
            xt = _ffn(xt, ffn_w1[l // 2].astype(bf16), ffn_w3[l // 2].astype(bf16), ffn_w2[l // 2].astype(bf16), ln2)
        else:
            router = jnp.pad(moe_router[l // 2], ((0, 0), (0, LANES - N_EXPERTS)))
            xt = _moe(xt, router, moe_w1[l // 2], moe_w3[l // 2], moe_w2[l // 2], ln2)
    return xt.reshape(B, S, D)
```

```python
import functools
import math

import jax
import jax.numpy as jnp
import numpy as np
from jax import lax
from jax.experimental import pallas as pl
from jax.experimental.pallas import tpu as pltpu
from jax.experimental.pallas import tpu_sc as plsc

f32 = jnp.float32
bf16 = jnp.bfloat16

D_MODEL = 1024
DEPTH = 2
GRID_W = 64
BRANCH_W = 256
HEAD_DIM = 64
ATT_HEADS = 4
ATT_KV_HEADS = 2
ROPE_THETA = 10000.0
QK_EPS = 1e-6
RW_GN_EPS = 64e-5
RW_COLS = 1088
S5_GROUP = 16
S5_GROUPS = 16
S5_STATE = 64
ML_HEADS = 4
N_EXPERTS = 8
ALPHA = (2 * DEPTH) ** 0.25
LN_EPS = 1e-5

LANES = 128
CHUNK = 64
NEG = -1e30
VMEM_LIMIT = 56 * 1024 * 1024

PROJ_SPLITS = (768, 256, 512, 256, 256, RW_COLS)


def _cparams(sem):
    return pltpu.CompilerParams(dimension_semantics=sem, vmem_limit_bytes=VMEM_LIMIT)


def _sigmoid(x):
    return 1.0 / (1.0 + jnp.exp(-x))


def _softplus(x):
    return jnp.maximum(x, 0.0) + jnp.log(1.0 + jnp.exp(-jnp.abs(x)))


def _dims(a, lhs_c, rhs_c):
    lead = a.ndim - 2
    batch = tuple(range(lead))
    return (((lhs_c + lead,), (rhs_c + lead,)), (batch, batch))


def _bdot(a, b):
    return lax.dot_general(a.astype(bf16), b.astype(bf16), _dims(a, 1, 0), preferred_element_type=f32)


def _bdot_nt(a, b):
    return lax.dot_general(a.astype(bf16), b.astype(bf16), _dims(a, 1, 1), preferred_element_type=f32)


def _bdot_tn(a, b):
    return lax.dot_general(a.astype(bf16), b.astype(bf16), _dims(a, 0, 0), preferred_element_type=f32)


def _split(x):
    hi = x.astype(bf16)
    return hi, (x - hi.astype(f32)).astype(bf16)


def _sdot(a, b, exact):
    dims = _dims(a, 1, 0)
    if exact == "rhs":
        hi, lo = _split(a)
        bb = b.astype(bf16)
        return (lax.dot_general(hi, bb, dims, preferred_element_type=f32)
                + lax.dot_general(lo, bb, dims, preferred_element_type=f32))
    hi, lo = _split(b)
    ab = a.astype(bf16)
    return (lax.dot_general(ab, hi, dims, preferred_element_type=f32)
            + lax.dot_general(ab, lo, dims, preferred_element_type=f32))


def _sdot3(a, b):
    dims = _dims(a, 1, 0)
    ah, al = _split(a)
    bh, bl = _split(b)
    return (lax.dot_general(ah, bh, dims, preferred_element_type=f32)
            + lax.dot_general(ah, bl, dims, preferred_element_type=f32)
            + lax.dot_general(al, bh, dims, preferred_element_type=f32))


def _seg_matrix(n, seg=HEAD_DIM):
    r = lax.broadcasted_iota(jnp.int32, (n, n), 0) // seg
    c = lax.broadcasted_iota(jnp.int32, (n, n), 1) // seg
    return (r == c).astype(f32)


def _layer_norm(y, g, b):
    mu = jnp.mean(y, axis=-1, keepdims=True)
    d = y - mu
    var = jnp.mean(d * d, axis=-1, keepdims=True)
    return d * lax.rsqrt(var + LN_EPS) * g + b


def _row_to_col(row, eye):
    n = eye.shape[-1]
    return jnp.sum(jnp.where(eye, jnp.broadcast_to(row, (row.shape[0], n, n)), 0.0), axis=2, keepdims=True)


def _stack_heads(x):
    h0 = lax.broadcasted_iota(jnp.int32, x.shape, 2) < HEAD_DIM
    return jnp.concatenate([jnp.where(h0, x, 0.0), jnp.where(h0, 0.0, x)], axis=1)


def _proj_body(x_ref, xp_ref, xn_ref, cos_ref, sin_ref, w_ref, b_ref, wg_ref, bg_ref, gain_ref, cw_ref, cb_ref,
               mix_ref, w2_ref, a2_ref, g2_ref, vec_ref,
               q_ref, k_ref, v_ref, s5_ref, mq_ref, mk_ref, mv_ref, mo_ref, r_ref, rk_ref, rv_ref, an_ref, bn_ref,
               lw_ref, gate_ref, bonus_ref, g_ref, mbuf_ref, rbuf_ref):
    tm = x_ref.shape[0]
    xb, xp, xn = x_ref[...].astype(bf16), xp_ref[...].astype(bf16), xn_ref[...].astype(bf16)
    offs = np.cumsum((0,) + PROJ_SPLITS)

    def cols(rows, n):
        sl = slice(int(offs[n]), int(offs[n + 1]))
        return jnp.dot(rows, w_ref[:, sl], preferred_element_type=f32) + b_ref[:, sl]

    q_ref[...], k_ref[...], v_ref[...] = _att_prep(cols(xb, 0), cos_ref[...], sin_ref[...], gain_ref[...])
    s5_ref[...] = cols(xb, 1)
    _fill_halo(mbuf_ref, cols(xb, 2), cols(xp, 2), cols(xn, 2))
    mq_ref[...], mk_ref[...] = _ml_prep(mbuf_ref, tm, cw_ref[...], cb_ref[...])
    mv_ref[...] = cols(xb, 3)
    mo_ref[...] = cols(xb, 4)
    _fill_halo(rbuf_ref, cols(xb, 5), cols(xp, 5), cols(xn, 5))
    outs = _rw_prep(rbuf_ref, tm, mix_ref[...], w2_ref[...], a2_ref[...], g2_ref[...], vec_ref[...])
    for o_ref, val in zip((r_ref, rk_ref, rv_ref, an_ref, bn_ref), outs[0:5]):
        o_ref[...] = val
    lw_ref[0], lw_ref[1], gate_ref[...], bonus_ref[...] = outs[5:9]
    g_ref[...] = lax.dot_general(wg_ref[...], xb, (((1,), (1,)), ((), ())), preferred_element_type=f32) + bg_ref[...]


def _proj(xt, proj_prm, att_prm, ml_prm, rw_prm, B, S, tm=512):
    T = B * S
    nb = S // tm
    n_tot = sum(PROJ_SPLITS)
    row = lambda n: pl.BlockSpec((tm, n), lambda b, i: (b * nb + i, 0))
    const = lambda a: pl.BlockSpec(a.shape, lambda b, i: (0,) * a.ndim)
    tab = pl.BlockSpec((tm, LANES), lambda b, i: (i, 0))
    f32out = lambda n: jax.ShapeDtypeStruct((T, n), f32)
    consts = list(proj_prm) + [att_prm[2]] + list(ml_prm) + list(rw_prm)
    out_specs = ([row(512), row(LANES), row(2 * LANES)] + [row(BRANCH_W)] * 10
                 + [pl.BlockSpec((2, tm, BRANCH_W), lambda b, i: (0, b * nb + i, 0)), row(BRANCH_W), row(BRANCH_W),
                    pl.BlockSpec((16, tm), lambda b, i: (0, b * nb + i))])
    out_shape = ([jax.ShapeDtypeStruct((T, 512), bf16), jax.ShapeDtypeStruct((T, LANES), bf16),
                  jax.ShapeDtypeStruct((T, 2 * LANES), bf16)] + [f32out(BRANCH_W)] * 10
                 + [jax.ShapeDtypeStruct((2, T, BRANCH_W), f32), f32out(BRANCH_W), f32out(BRANCH_W),
                    jax.ShapeDtypeStruct((16, T), f32)])
    return pl.pallas_call(
        _proj_body, grid=(B, nb),
        in_specs=_halo_specs(D_MODEL, tm, B, S) + [tab, tab] + [const(a) for a in consts],
        out_specs=out_specs, out_shape=out_shape,
        scratch_shapes=[pltpu.VMEM((tm + 16, 512), f32), pltpu.VMEM((tm + 16, RW_COLS), f32)],
        compiler_params=_cparams(("parallel", "parallel")), name="proj")(
            xt, xt, xt, att_prm[0], att_prm[1], *consts)


def _proj_params(w_in, b_in, ml_ib, ml_fb):
    o = np.cumsum((0, 256, 128, 128, RW_COLS, 256, 512, 256, 8, 8, 256))
    sl = lambda i: (w_in[:, o[i]:o[i + 1]], b_in[o[i]:o[i + 1]])
    (wq, bq), (wk, bk), (wv, bv), (wrw, brw), (ws5, bs5), (wqk, bqk), (wmv, bmv), (wi, bi), (wf, bf), (wo, bo) = (
        sl(i) for i in range(10))
    zw, zb = jnp.zeros((D_MODEL, HEAD_DIM), f32), jnp.zeros((HEAD_DIM,), f32)
    wq_e, bq_e = [], []
    for h in range(ATT_HEADS):
        wh, bh = wq[:, 64 * h:64 * h + 64], bq[64 * h:64 * h + 64]
        wq_e += [wh, zw] if h // 2 == 0 else [zw, wh]
        bq_e += [bh, zb] if h // 2 == 0 else [zb, bh]
    w = jnp.concatenate(wq_e + [wk, wv, ws5, wqk, wmv, wo, wrw], axis=1)
    b = jnp.concatenate(bq_e + [bk, bv, bs5, bqk, bmv, bo, brw])
    wg = jnp.concatenate([wi, wf], axis=1).T
    bg = jnp.concatenate([bi + ml_ib.reshape(-1), bf + ml_fb.reshape(-1)])
    return w.astype(bf16), b[None, :], wg.astype(bf16), bg[:, None]


def _att_prep(att, cos, sin, gain):
    x = att[:, 0:640]
    ms = _sdot(x * x, _seg_matrix(640), "rhs") * (1.0 / HEAD_DIM)
    xn = x * lax.rsqrt(ms + QK_EPS) * gain
    lane = lax.broadcasted_iota(jnp.int32, xn.shape, 1)
    partner = jnp.where((lane % 32) < 16, pltpu.roll(xn, 640 - 16, 1), pltpu.roll(xn, 16, 1))
    rot = xn * jnp.concatenate([cos] * 5, axis=1) + partner * jnp.concatenate([sin] * 5, axis=1)
    v = jnp.concatenate([att[:, 640:768].astype(bf16), jnp.ones((x.shape[0], LANES), bf16)], axis=1)
    return rot[:, 0:512].astype(bf16), rot[:, 512:640].astype(bf16), v


def _rope_tables(S):
    t = np.arange(S)
    row = (t // GRID_W).astype(np.float32)
    col = (t % GRID_W).astype(np.float32)
    n = 16
    inv = np.power(np.float32(ROPE_THETA), -np.arange(n, dtype=np.float32) / n).astype(np.float32)
    ar = jnp.asarray(row)[:, None] * jnp.asarray(inv)
    ac = jnp.asarray(col)[:, None] * jnp.asarray(inv)
    cos = jnp.concatenate([jnp.cos(ar), jnp.cos(ar), jnp.cos(ac), jnp.cos(ac)], axis=1)
    sin = jnp.concatenate([-jnp.sin(ar), jnp.sin(ar), -jnp.sin(ac), jnp.sin(ac)], axis=1)
    return jnp.concatenate([cos, cos], axis=1), jnp.concatenate([sin, sin], axis=1)


def _flash_body(q_ref, k_ref, v_ref, o_ref, acc_ref, *m_scratch, tk, track_max):
    tq = q_ref.shape[0]
    nk = k_ref.shape[0] // tk
    q2 = jnp.concatenate([q_ref[:, 0:LANES], q_ref[:, LANES:2 * LANES]], axis=0)
    acc_ref[...] = jnp.zeros(acc_ref.shape, f32)
    if track_max:
        m_ref, = m_scratch
        m_ref[...] = jnp.full(m_ref.shape, NEG, f32)

    def step(j, carry):
        rows = pl.ds(pl.multiple_of(j * tk, tk), tk)
        s = lax.dot_general(q2, k_ref[rows, :], (((1,), (1,)), ((), ())), preferred_element_type=f32)
        if track_max:
            m_old = m_ref[...]
            m_new = jnp.maximum(m_old, jnp.max(s, axis=1, keepdims=True))
            p = jnp.exp(s - m_new).astype(bf16)
            acc_ref[...] = jnp.exp(m_old - m_new) * acc_ref[...] + jnp.dot(p, v_ref[rows, :], preferred_element_type=f32)
            m_ref[...] = m_new
        else:
            acc_ref[...] += jnp.dot(jnp.exp(s).astype(bf16), v_ref[rows, :], preferred_element_type=f32)
        return carry

    lax.fori_loop(0, nk, step, 0)
    o = acc_ref[:, 0:LANES] / acc_ref[:, LANES:2 * LANES]
    o_ref[...] = jnp.concatenate([o[0:tq], o[tq:2 * tq]], axis=1)


SCORE_BOUND_MAX = 60.0


def _flash(q, k, v, score_bound, B, S, tq=256, tk=8192):
    T = B * S
    nb = S // tq
    tk = min(tk, S)

    def call(track_max):
        scratch = [pltpu.VMEM((2 * tq, 2 * LANES), f32)] + ([pltpu.VMEM((2 * tq, 1), f32)] if track_max else [])
        return pl.pallas_call(
            functools.partial(_flash_body, tk=tk, track_max=track_max), grid=(B, ATT_KV_HEADS, nb),
            in_specs=[pl.BlockSpec((tq, 2 * LANES), lambda b, g, i: (b * nb + i, g)),
                      pl.BlockSpec((S, LANES), lambda b, g, i: (b, 0)),
                      pl.BlockSpec((S, 2 * LANES), lambda b, g, i: (b, 0))],
            out_specs=pl.BlockSpec((tq, 2 * LANES), lambda b, g, i: (b * nb + i, g)),
            out_shape=jax.ShapeDtypeStruct((T, 512), f32), scratch_shapes=scratch,
            compiler_params=_cparams(("parallel", "parallel", "parallel")),
            name="flash_safe" if track_max else "flash")(q, k, v)

    return lax.cond(score_bound <= SCORE_BOUND_MAX, lambda: call(False), lambda: call(True))


def _halo_specs(width, tm, B, S):
    nb = S // tm
    r8 = tm // 8
    last8 = B * S // 8 - 1

    def main(b, i):
        return (b * nb + i, 0)

    def prev(b, i):
        return (jnp.maximum(b * (S // 8) + i * r8 - 1, 0), 0)

    def nxt(b, i):
        return (jnp.minimum(b * (S // 8) + (i + 1) * r8, last8), 0)

    return [pl.BlockSpec((tm, width), main), pl.BlockSpec((8, width), prev), pl.BlockSpec((8, width), nxt)]


def _fill_halo(buf_ref, x, prev8, next8):
    tm = x.shape[0]
    i = pl.program_id(1)
    last = pl.num_programs(1) - 1
    buf_ref[pl.ds(8, tm), :] = x
    buf_ref[pl.ds(0, 8), :] = jnp.where(i > 0, prev8, 0.0)
    buf_ref[pl.ds(8 + tm, 8), :] = jnp.where(i < last, next8, 0.0)


def _rw_prep(buf_ref, tm, mix, w2, a2, g2, vec):
    x = buf_ref[pl.ds(8, tm), :]
    p = x + mix[0:1, :] * (buf_ref[pl.ds(7, tm), :] - x) + mix[1:2, :] * (buf_ref[pl.ds(9, tm), :] - x)
    r, k, v = p[:, 0:256], p[:, 256:512], p[:, 512:768]
    w0f, w0b, a0, k_k, k_a, r_k = (vec[j:j + 1, :] for j in range(6))
    dec = _bdot(jnp.tanh(p[:, 768:896]), w2)
    z = p[:, 896:1088]
    a = _sigmoid(a0 + _bdot(z, a2))
    gate = _bdot(_sigmoid(z), g2)
    seg = _seg_matrix(BRANCH_W)
    kk = k * k_k
    kk = kk / jnp.maximum(jnp.sqrt(_sdot(kk * kk, seg, "rhs")), 1e-12)
    k2 = k * (1.0 + (a - 1.0) * k_a)
    bonus = _sdot(r * k2 * r_k, seg, "rhs") * v
    lwf = -jnp.exp(-_softplus(-(w0f + dec[:, 0:256])) - 0.5)
    lwb = -jnp.exp(-_softplus(-(w0b + dec[:, 256:512])) - 0.5)
    return r, k2, v, -kk, kk * a, lwf, lwb, gate, bonus


def _by_direction(x, fwd, bwd, fn):
    h = x.shape[0] // 2
    return jnp.concatenate([fn(x[0:h], fwd), fn(x[h:2 * h], bwd)], axis=0)


def _keep(x, masks):
    return _by_direction(x, masks[0], masks[1], lambda t, m: jnp.where(m, t, 0.0))


def _both(masks, nchain):
    h = nchain // 2
    return jnp.concatenate([jnp.broadcast_to(m.astype(f32), (h,) + m.shape[1:]) for m in masks], axis=0)


def _pair_masks():
    n = 2 * CHUNK
    r = lax.broadcasted_iota(jnp.int32, (1, n, n), 1)
    c = lax.broadcasted_iota(jnp.int32, (1, n, n), 2)
    same = (r // CHUNK) == (c // CHUNK)
    return r, c, same, (same & (c < r), same & (c > r)), (same & (c <= r), same & (c >= r))


def _rw_chunk(st, r, k, v, an, bn, lw):
    L = CHUNK
    N = r.shape[0]
    ri = lax.broadcasted_iota(jnp.int32, (1, L, L), 1)
    ci = lax.broadcasted_iota(jnp.int32, (1, L, L), 2)
    cs = _sdot(_both((ci <= ri, ci >= ri), N), lw, "lhs")
    tot = jnp.sum(lw, axis=1, keepdims=True)
    e_neg = jnp.exp(-cs)
    at = an * jnp.exp(cs - lw)
    rt = r * jnp.exp(cs)
    a2, b2, k2, v2 = _stack_heads(at), _stack_heads(bn * e_neg), _stack_heads(k * e_neg), _stack_heads(v)
    t = lax.broadcasted_iota(jnp.int32, (1, L, LANES), 1)
    tc = lax.broadcasted_iota(jnp.int32, (1, L, LANES), 2) % L
    strict, incl = (tc < t, tc > t), (tc <= t, tc >= t)
    g = _bdot_nt(jnp.concatenate([at, rt], axis=1), jnp.concatenate([b2, k2], axis=1))
    mab = _keep(g[:, 0:L, 0:LANES], strict)
    mak = _keep(g[:, 0:L, LANES:2 * LANES], strict)
    pb = _keep(g[:, L:2 * L, 0:LANES], incl)
    pk = _keep(g[:, L:2 * L, LANES:2 * LANES], incl)
    mul = lambda p, q: _bdot(p, _stack_heads(q))
    m8 = jnp.where((t // 8) == (tc // 8), mab, 0.0)
    x = (t == tc).astype(f32) + m8
    p = mul(m8, m8)
    x = x + mul(x, p)
    p = mul(p, p)
    x = x + mul(x, p)
    n = 8
    while n < L:
        e = jnp.where(((t // (2 * n)) == (tc // (2 * n))) & ((t // n) != (tc // n)), mab, 0.0)
        x = x + mul(mul(x, e), x)
        n *= 2
    wu = _bdot(x, jnp.concatenate([a2, _stack_heads(_bdot(mak, v2))], axis=2))
    wu2 = jnp.concatenate([_stack_heads(wu[:, :, 0:LANES]), _stack_heads(wu[:, :, LANES:2 * LANES])], axis=2)
    pwu = _bdot(pb, wu2)
    rh = rt + pwu[:, :, 0:LANES]
    y = pwu[:, :, LANES:2 * LANES] + _bdot(jnp.concatenate([pk, rh], axis=2), jnp.concatenate([v2, st], axis=1))
    r128 = lax.broadcasted_iota(jnp.int32, (1, LANES, LANES), 1)
    c128 = lax.broadcasted_iota(jnp.int32, (1, LANES, LANES), 2)
    gam = _row_to_col(jnp.exp(tot), r128 == c128)
    bwu = _bdot_tn(b2, wu2)
    st = gam * (st + _bdot(bwu[:, :, 0:LANES], st) + bwu[:, :, LANES:2 * LANES] + _bdot_tn(k2, v2))
    return y, st


def _load_pairs(ref, rows):
    return jnp.concatenate([ref[:, rows, 0:LANES], ref[:, rows, LANES:2 * LANES]], axis=0)


def _store_pairs(ref, rows, y):
    nb = ref.shape[0]
    ref[:, rows, 0:LANES] = y[0:nb]
    ref[:, rows, LANES:2 * LANES] = y[nb:2 * nb]


def _load_both(f_ref, b_ref, rows_f, rows_b):
    return jnp.concatenate([_load_pairs(f_ref, rows_f), _load_pairs(b_ref, rows_b)], axis=0)


def _chunk_rows(cc, nch):
    return (pl.ds(pl.multiple_of(cc * CHUNK, CHUNK), CHUNK),
            pl.ds(pl.multiple_of((nch - 1 - cc) * CHUNK, CHUNK), CHUNK))


def _rw_scan_body(rf_ref, rb_ref, kf_ref, kb_ref, vf_ref, vb_ref, anf_ref, anb_ref, bnf_ref, bnb_ref, lwf_ref, lwb_ref,
                  yf_ref, yb_ref, st_ref):
    nch = rf_ref.shape[1] // CHUNK
    half = st_ref.shape[0] // 2

    @pl.when(pl.program_id(0) == 0)
    def _():
        st_ref[...] = jnp.zeros(st_ref.shape, f32)

    def step(cc, carry):
        rows_f, rows_b = _chunk_rows(cc, nch)
        pairs = ((rf_ref, rb_ref), (kf_ref, kb_ref), (vf_ref, vb_ref), (anf_ref, anb_ref), (bnf_ref, bnb_ref),
                 (lwf_ref.at[0], lwb_ref.at[0]))
        y, st = _rw_chunk(st_ref[...], *(_load_both(f, b, rows_f, rows_b) for f, b in pairs))
        _store_pairs(yf_ref, rows_f, y[0:half])
        _store_pairs(yb_ref, rows_b, y[half:2 * half])
        st_ref[...] = st
        return carry

    lax.fori_loop(0, nch, step, 0)


def _rw_scan(r, k, v, an, bn, lw, B, S, ts=128):
    nb = S // ts
    fwd = pl.BlockSpec((B, ts, BRANCH_W), lambda i: (0, i, 0))
    bwd = pl.BlockSpec((B, ts, BRANCH_W), lambda i: (0, nb - 1 - i, 0))
    out = jax.ShapeDtypeStruct((B, S, BRANCH_W), f32)
    return pl.pallas_call(
        _rw_scan_body, grid=(nb,),
        in_specs=[fwd, bwd] * 5 + [pl.BlockSpec((1, B, ts, BRANCH_W), lambda i: (0, 0, i, 0)),
                                   pl.BlockSpec((1, B, ts, BRANCH_W), lambda i: (1, 0, nb - 1 - i, 0))],
        out_specs=[fwd, bwd], out_shape=[out, out],
        scratch_shapes=[pltpu.VMEM((2 * B * BRANCH_W // LANES, LANES, LANES), f32)],
        compiler_params=_cparams(("arbitrary",)), name="rw_scan")(r, r, k, k, v, v, an, an, bn, bn, lw, lw)


def _rw_finish(y, gate, bonus, gn):
    seg = _seg_matrix(BRANCH_W)
    mu = _sdot(y, seg, "rhs") * (1.0 / HEAD_DIM)
    d = y - mu
    var = _sdot(d * d, seg, "rhs") * (1.0 / HEAD_DIM)
    yn = d * lax.rsqrt(var + RW_GN_EPS) * gn[0:1, :] + gn[1:2, :]
    return (yn + bonus) * gate


def _rwkv(r, k, v, an, bn, lw, B, S):
    seq = [t.reshape(B, S, BRANCH_W) for t in (r, k, v, an, bn)] + [lw.reshape(2, B, S, BRANCH_W)]
    yf, yb = _rw_scan(*seq, B, S)
    return yf.reshape(B * S, BRANCH_W), yb.reshape(B * S, BRANCH_W)


def _rw_params(mix, w0, w2, a0, a2, g2, k_k, k_a, r_k):
    z = jnp.zeros((64, 256), f32)
    w2c = jnp.concatenate([jnp.concatenate([w2[0], z], axis=1), jnp.concatenate([z, w2[1]], axis=1)], axis=0)
    a2p = jnp.concatenate([a2, jnp.zeros((128, 256), f32)], axis=0)
    g2p = jnp.concatenate([jnp.zeros((64, 256), f32), g2], axis=0)
    vec = jnp.stack([w0[0], w0[1], a0, k_k, k_a, r_k.reshape(-1), jnp.zeros_like(a0), jnp.zeros_like(a0)])
    return mix, w2c.astype(bf16), a2p.astype(bf16), g2p.astype(bf16), vec


S5_HALF = S5_GROUPS * S5_STATE // 2


def _s5_scan_body(uf_ref, ub_ref, bh_ref, cre_ref, cim_ref, lre_ref, lim_ref, yf_ref, yb_ref,
                  sre_ref, sim_ref, bre_ref, bim_ref):
    ts = uf_ref.shape[0]
    n = S5_HALF

    @pl.when(pl.program_id(0) == 0)
    def _():
        sre_ref[...] = jnp.zeros(sre_ref.shape, f32)
        sim_ref[...] = jnp.zeros(sim_ref.shape, f32)

    row_half = lax.broadcasted_iota(jnp.int32, uf_ref.shape, 1) // 4
    lane_half = lax.broadcasted_iota(jnp.int32, uf_ref.shape, 2) // (BRANCH_W // 2)
    for d, u_ref in enumerate((uf_ref, ub_ref)):
        lhs = jnp.where(row_half == lane_half, u_ref[...], 0.0).reshape(ts * 8, BRANCH_W)
        bu = _bdot(lhs, bh_ref[d]).reshape(ts, 8, 2 * n)
        bre_ref[d] = bu[:, :, 0:n]
        bim_ref[d] = bu[:, :, n:2 * n]
    lre = lre_ref[...]
    lim = lim_ref[...]

    def step(t, carry):
        out = []
        for d, td in enumerate((t, ts - 1 - t)):
            sre, sim = carry[2 * d], carry[2 * d + 1]
            nre = lre[d] * sre - lim[d] * sim + bre_ref[d, td]
            nim = lre[d] * sim + lim[d] * sre + bim_ref[d, td]
            bre_ref[d, td] = nre
            bim_ref[d, td] = nim
            out += [nre, nim]
        return tuple(out)

    s = lax.fori_loop(0, ts, step, (sre_ref[0], sim_ref[0], sre_ref[1], sim_ref[1]), unroll=4)
    sre_ref[0], sim_ref[0], sre_ref[1], sim_ref[1] = s
    low = lane_half == 0
    for d, y_ref in enumerate((yf_ref, yb_ref)):
        yv = (_bdot(bre_ref[d].reshape(ts * 8, n), cre_ref[d])
              - _bdot(bim_ref[d].reshape(ts * 8, n), cim_ref[d])).reshape(ts, 8, BRANCH_W)
        y_ref[...] = jnp.where(low, yv, pltpu.roll(yv, 4, 1))


def _s5_scan(u8, bh, cre, cim, lre, lim, ts=128):
    S = u8.shape[0]
    nb = S // ts
    n = S5_HALF
    full = lambda shape: pl.BlockSpec(shape, lambda i: (0,) * len(shape))
    fwd = pl.BlockSpec((ts, 8, BRANCH_W), lambda i: (i, 0, 0))
    bwd = pl.BlockSpec((ts, 8, BRANCH_W), lambda i: (nb - 1 - i, 0, 0))
    out = jax.ShapeDtypeStruct((S, 8, BRANCH_W), f32)
    return pl.pallas_call(
        _s5_scan_body, grid=(nb,),
        in_specs=[fwd, bwd, full((2, BRANCH_W, 2 * n)), full((2, n, BRANCH_W)), full((2, n, BRANCH_W)),
                  full((2, 8, n)), full((2, 8, n))],
        out_specs=[fwd, bwd], out_shape=[out, out],
        scratch_shapes=[pltpu.VMEM((2, 8, n), f32), pltpu.VMEM((2, 8, n), f32),
                        pltpu.VMEM((2, ts, 8, n), f32), pltpu.VMEM((2, ts, 8, n), f32)],
        compiler_params=_cparams(("arbitrary",)), name="s5_scan")(u8, u8, bh, cre, cim, lre, lim)


def _s5_finish(y, u, vec, w):
    y = y + u * vec[0:1, :]
    y = 0.5 * y * (1.0 + jnp.tanh(math.sqrt(2.0 / math.pi) * (y + 0.044715 * (y * y * y))))
    return y * _sigmoid(_bdot(y, w) + vec[1:2, :])


def _s5_params(lam_re, lam_im, log_dt, b_re, b_im, c_re, c_im):
    G, P, C = S5_GROUPS, S5_STATE, S5_GROUP
    H = G // 2
    eye = jnp.eye(H, dtype=f32)
    b_c = lax.complex(b_re, b_im)

    def b_part(x):
        return jnp.einsum('ab,hapc->hacbp', eye, x.reshape(2, H, P, C)).reshape(G * C, H * P)

    def c_part(x):
        return jnp.einsum('ab,hacp->aphbc', eye, x.reshape(2, H, C, P)).reshape(H * P, G * C)

    def rows(x):
        return jnp.broadcast_to(x.reshape(2, 1, H * P), (2, 4, H * P)).reshape(8, H * P)

    bh, cre, cim, lre, lim = [], [], [], [], []
    for d in range(2):
        lam = lax.complex(jnp.minimum(lam_re[d], -1e-4), lam_im[d])
        lam_bar = jnp.exp(lam * jnp.exp(log_dt[d])[:, None])
        b_bar = ((lam_bar - 1.0) / lam)[..., None] * b_c
        bh.append(jnp.concatenate([b_part(jnp.real(b_bar)), b_part(jnp.imag(b_bar))], axis=1))
        cre.append(c_part(c_re[d]))
        cim.append(c_part(c_im[d]))
        lre.append(rows(jnp.real(lam_bar)))
        lim.append(rows(jnp.imag(lam_bar)))
    return (jnp.stack(bh).astype(bf16), jnp.stack(cre).astype(bf16), jnp.stack(cim).astype(bf16),
            jnp.stack(lre), jnp.stack(lim))


def _s5(u, prm, B, S):
    u3 = u.reshape(B, S, BRANCH_W).transpose(1, 0, 2)
    yf, yb = _s5_scan(jnp.concatenate([u3, u3], axis=1), *prm)
    return (yf[:, 0:B] + yb[:, 0:B]).transpose(1, 0, 2).reshape(B * S, BRANCH_W)


def _ml_prep(buf_ref, tm, w, b):
    y = b
    for j in range(5):
        y = y + w[j:j + 1, :] * buf_ref[pl.ds(6 + j, tm), :]
    y = y * _sigmoid(y)
    return y[:, 0:BRANCH_W], y[:, BRANCH_W:2 * BRANCH_W] * (HEAD_DIM ** -0.5)


def _ml_chunk(state, q, k, v, li, lfp):
    cn, m_row = state
    L = CHUNK
    N = q.shape[0]
    rr, cc, same, _, incl2 = _pair_masks()
    same_f = _both((same, same), N)
    lane = lax.broadcasted_iota(jnp.int32, (1, 1, 2 * L), 2)
    lf = jnp.minimum(lfp, 0.0) - jnp.log(1.0 + jnp.exp(-jnp.abs(lfp)))
    lf8 = jnp.broadcast_to(lf, (N, 8, 2 * L))
    b_row = _sdot(lf8, _both((incl2[1], incl2[0]), N), "rhs")[:, 0:1]
    g_row = _sdot(lf8, same_f, "rhs")[:, 0:1]
    w_end = g_row - b_row + li
    m0 = jnp.max(jnp.where(lane < L, w_end, NEG), axis=2, keepdims=True)
    m1 = jnp.max(jnp.where(lane < L, NEG, w_end), axis=2, keepdims=True)
    m_loc = jnp.where(lane < L, m0, m1)
    t = lax.broadcasted_iota(jnp.int32, (1, L, LANES), 1)
    tc = lax.broadcasted_iota(jnp.int32, (1, L, LANES), 2) % L
    diag = t == tc

    def cols(row):
        return _sdot(jnp.where(diag, jnp.broadcast_to(row, (N, L, LANES)), 0.0), same_f, "rhs")

    e_col, b_col = cols(jnp.exp(w_end - m_loc)), cols(b_row)
    k2, v2 = _stack_heads(k), _stack_heads(v)
    v1 = jnp.concatenate([v2, _stack_heads(jnp.ones_like(v))], axis=2)
    log_inter = b_col + m_row
    log_intra = _by_direction(b_col - b_row + li, tc <= t, tc >= t, lambda x, m: jnp.where(m, x, NEG))
    head0 = lane < L
    r0 = jnp.max(jnp.where(head0, log_intra, NEG), axis=2, keepdims=True)
    r1 = jnp.max(jnp.where(head0, NEG, log_intra), axis=2, keepdims=True)
    m_r = jnp.maximum(log_inter, jnp.where(head0, r0, r1))
    s = _bdot_nt(q, k2) * jnp.exp(log_intra - m_r)
    inter = jnp.exp(log_inter - m_r)
    nd = _bdot(s, v1) + jnp.concatenate([inter, inter], axis=2) * _bdot(q, cn)
    h = nd[:, :, 0:LANES] / jnp.maximum(jnp.abs(nd[:, :, LANES:2 * LANES]), jnp.exp(-m_r))
    m_new = jnp.maximum(g_row + m_row, m_loc)
    a = jnp.exp(g_row + m_row - m_new)
    bb = jnp.exp(m_loc - m_new)
    cn = (jnp.concatenate([a, a], axis=2) * cn
          + jnp.concatenate([bb, bb], axis=2) * _bdot_tn(_stack_heads(e_col * k), v1))
    return h, (cn, m_new)


GATE_ROWS = 8


def _ml_scan_body(qf_ref, qb_ref, kf_ref, kb_ref, vf_ref, vb_ref, gf_ref, gb_ref, hf_ref, hb_ref, c_ref, m_ref):
    nch = qf_ref.shape[1] // CHUNK
    half = c_ref.shape[0] // 2
    i = pl.program_id(0)
    nb = pl.num_programs(0)
    per = GATE_ROWS // nch
    base_f = (i % per) * nch
    base_b = ((nb - 1 - i) % per) * nch

    @pl.when(i == 0)
    def _():
        c_ref[...] = jnp.zeros(c_ref.shape, f32)
        m_ref[...] = jnp.zeros(m_ref.shape, f32)

    def step(cc, carry):
        rows_f, rows_b = _chunk_rows(cc, nch)
        gate = lambda t: jnp.concatenate(
            [gf_ref[t, 0, 0, :, pl.ds(base_f + cc, 1), :], gf_ref[t, 0, 1, :, pl.ds(base_f + cc, 1), :],
             gb_ref[t, 0, 0, :, pl.ds(base_b + nch - 1 - cc, 1), :], gb_ref[t, 0, 1, :, pl.ds(base_b + nch - 1 - cc, 1), :]],
            axis=0)
        h, (cn, m_row) = _ml_chunk((c_ref[...], m_ref[...]), _load_both(qf_ref, qb_ref, rows_f, rows_b),
                                   _load_both(kf_ref, kb_ref, rows_f, rows_b), _load_both(vf_ref, vb_ref, rows_f, rows_b),
                                   gate(0), gate(1))
        _store_pairs(hf_ref, rows_f, h[0:half])
        _store_pairs(hb_ref, rows_b, h[half:2 * half])
        c_ref[...] = cn
        m_ref[...] = m_row
        return carry

    lax.fori_loop(0, nch, step, 0)


def _ml_scan(q, k, v, g, B, S, ts=256):
    nb = S // ts
    per = GATE_ROWS * CHUNK // ts
    nchain = 2 * B * BRANCH_W // LANES
    fwd = pl.BlockSpec((B, ts, BRANCH_W), lambda i: (0, i, 0))
    bwd = pl.BlockSpec((B, ts, BRANCH_W), lambda i: (0, nb - 1 - i, 0))
    out = jax.ShapeDtypeStruct((B, S, BRANCH_W), f32)
    return pl.pallas_call(
        _ml_scan_body, grid=(nb,),
        in_specs=[fwd, bwd] * 3 + [pl.BlockSpec((2, 1, 2, B, GATE_ROWS, LANES), lambda i: (0, 0, 0, 0, i // per, 0)),
                                   pl.BlockSpec((2, 1, 2, B, GATE_ROWS, LANES),
                                                lambda i: (0, 1, 0, 0, (nb - 1 - i) // per, 0))],
        out_specs=[fwd, bwd], out_shape=[out, out],
        scratch_shapes=[pltpu.VMEM((nchain, LANES, 2 * LANES), f32), pltpu.VMEM((nchain, 1, LANES), f32)],
        compiler_params=_cparams(("arbitrary",)), name="ml_scan")(q, q, k, k, v, v, g, g)


def _mlstm(q, k, mv, gt, B, S):
    g = gt.reshape(2, 2, 2, 2, B, S // CHUNK, CHUNK).transpose(0, 1, 2, 4, 5, 3, 6).reshape(2, 2, 2, B, S // CHUNK, LANES)
    hf, hb = _ml_scan(*(t.reshape(B, S, BRANCH_W) for t in (q, k, mv)), g, B, S)
    return hf.reshape(B * S, BRANCH_W), hb.reshape(B * S, BRANCH_W)


def _merge_body(x_ref, att_ref, yf_ref, yb_ref, rg_ref, rb_ref, sy_ref, su_ref, hf_ref, hb_ref, mo_ref,
                wg_ref, bg_ref, wba_ref, wb_ref, wo_ref, ln_ref, gn_ref, sv_ref, sw_ref, o_ref):
    x = x_ref[...]
    xb = x.astype(bf16)
    rw = _rw_finish(yf_ref[...] + yb_ref[...], rg_ref[...], rb_ref[...], gn_ref[...])
    s5 = _s5_finish(sy_ref[...], su_ref[...], sv_ref[...], sw_ref[...])
    ml = _sigmoid(mo_ref[...]) * (hf_ref[...] + hb_ref[...])
    branches = (att_ref[...], rw, s5, ml)
    merged = None
    for n in range(4):
        gate = _sigmoid(jnp.dot(xb, wg_ref[n], preferred_element_type=f32) + bg_ref[n:n + 1, :])
        wide = _bdot(branches[n], wba_ref[...] if n == 0 else wb_ref[n - 1])
        merged = gate * wide if merged is None else merged + gate * wide
    y = ALPHA * x + _bdot(merged, wo_ref[...])
    o_ref[...] = _layer_norm(y, ln_ref[0:1, :], ln_ref[1:2, :])


def _merge(xt, att, rw_parts, s5_parts, ml_parts, wg, bg, wba, wb, wo, ln, gn, s5_vec, s5_w, tm=256):
    T = xt.shape[0]
    row = lambda n: pl.BlockSpec((tm, n), lambda i: (i, 0))
    const = lambda shape: pl.BlockSpec(shape, lambda i: (0,) * len(shape), pipeline_mode=pl.Buffered(1))
    return pl.pallas_call(
        _merge_body, grid=(T // tm,),
        in_specs=[row(D_MODEL), row(512)] + [row(BRANCH_W)] * 9
        + [const((4, D_MODEL, D_MODEL)), const((4, D_MODEL)), const((512, D_MODEL)), const((3, BRANCH_W, D_MODEL)),
           const((D_MODEL, D_MODEL)), const((2, D_MODEL)), const((2, BRANCH_W)), const((2, BRANCH_W)),
           const((BRANCH_W, BRANCH_W))],
        out_specs=row(D_MODEL), out_shape=jax.ShapeDtypeStruct((T, D_MODEL), f32),
        compiler_params=_cparams(("parallel",)), name="merge")(
            xt, att, *rw_parts, *s5_parts, *ml_parts, wg, bg, wba, wb, wo, ln, gn, s5_vec, s5_w)


def _att_branch_weight(wb):
    z = jnp.zeros((HEAD_DIM, D_MODEL), f32)
    parts = []
    for h in range(ATT_HEADS):
        wh = wb[64 * h:64 * h + 64]
        parts += [wh, z] if h // 2 == 0 else [z, wh]
    return jnp.concatenate(parts, axis=0)


def _ffn_body(x_ref, w1_ref, w3_ref, w2_ref, ln_ref, o_ref, acc_ref):
    j = pl.program_id(1)

    @pl.when(j == 0)
    def _():
        acc_ref[...] = jnp.zeros(acc_ref.shape, f32)

    xb = x_ref[...].astype(bf16)
    h1 = jnp.dot(xb, w1_ref[...], preferred_element_type=f32)
    h3 = jnp.dot(xb, w3_ref[...], preferred_element_type=f32)
    acc_ref[...] += _bdot(h1 * _sigmoid(h1) * h3, w2_ref[...])

    @pl.when(j == pl.num_programs(1) - 1)
    def _():
        o_ref[...] = _layer_norm(ALPHA * x_ref[...] + acc_ref[...], ln_ref[0:1, :], ln_ref[1:2, :])


def _ffn(xt, w1, w3, w2, ln, tm=512, tf=1408):
    T = xt.shape[0]
    dff = w1.shape[1]
    return pl.pallas_call(
        _ffn_body, grid=(T // tm, dff // tf),
        in_specs=[pl.BlockSpec((tm, D_MODEL), lambda i, j: (i, 0)), pl.BlockSpec((D_MODEL, tf), lambda i, j: (0, j)),
                  pl.BlockSpec((D_MODEL, tf), lambda i, j: (0, j)), pl.BlockSpec((tf, D_MODEL), lambda i, j: (j, 0)),
                  pl.BlockSpec((2, D_MODEL), lambda i, j: (0, 0))],
        out_specs=pl.BlockSpec((tm, D_MODEL), lambda i, j: (i, 0)),
        out_shape=jax.ShapeDtypeStruct((T, D_MODEL), f32), scratch_shapes=[pltpu.VMEM((tm, D_MODEL), f32)],
        compiler_params=_cparams(("parallel", "arbitrary")), name="ffn")(xt, w1, w3, w2, ln)


MOE_TILE = 1024
SC_WINDOW = 128
SC_WORDS = 256


def _pack_words(x):
    bits = lax.bitcast_convert_type(x.astype(bf16).astype(f32), jnp.int32)
    half = D_MODEL // 2
    w = lax.shift_right_logical(bits[:, 0:half], 16) | bits[:, half:D_MODEL]
    return w[:, 0:SC_WORDS], w[:, SC_WORDS:2 * SC_WORDS]


def _unpack_words(wa, wb):
    w = jnp.concatenate([wa, wb], axis=1)
    lo = lax.bitcast_convert_type(lax.shift_left(w, 16), f32)
    hi = lax.bitcast_convert_type(w & jnp.int32(-65536), f32)
    return jnp.concatenate([lo, hi], axis=1)


def _router_body(x_ref, rt_ref, xa_ref, xb_ref, meta_ref, cnt_ref, run_ref):
    tb = x_ref.shape[0]

    @pl.when(pl.program_id(0) == 0)
    def _():
        run_ref[...] = jnp.zeros(run_ref.shape, f32)

    x = x_ref[...]
    xa_ref[...], xb_ref[...] = _pack_words(x)
    logits = _sdot3(x, rt_ref[...])
    lane = lax.broadcasted_iota(jnp.int32, logits.shape, 1)
    lg = jnp.where(lane < N_EXPERTS, logits, NEG)
    v1 = jnp.max(lg, axis=1, keepdims=True)
    i1 = jnp.min(jnp.where(lg == v1, lane, LANES), axis=1, keepdims=True)
    lg2 = jnp.where(lane == i1, NEG, lg)
    v2 = jnp.max(lg2, axis=1, keepdims=True)
    i2 = jnp.min(jnp.where(lg2 == v2, lane, LANES), axis=1, keepdims=True)
    e2 = jnp.exp(v2 - v1)
    sel1, sel2 = lane == i1, lane == i2
    mask = (sel1 | sel2).astype(f32)
    r = lax.broadcasted_iota(jnp.int32, (tb, tb), 0)
    c = lax.broadcasted_iota(jnp.int32, (tb, tb), 1)
    rank = _bdot((c < r).astype(f32), mask) + run_ref[0:1, :]
    run_ref[...] = run_ref[...] + jnp.sum(mask, axis=0, keepdims=True)
    rank1 = jnp.sum(jnp.where(sel1, rank, 0.0), axis=1, keepdims=True)
    rank2 = jnp.sum(jnp.where(sel2, rank, 0.0), axis=1, keepdims=True)
    cols = (i1.astype(f32), i2.astype(f32), rank1, rank2, 1.0 / (1.0 + e2), e2 / (1.0 + e2))
    meta = jnp.zeros(logits.shape, f32)
    for n, col in enumerate(cols):
        meta = jnp.where(lane == n, col, meta)
    meta_ref[...] = meta
    cnt_ref[...] = run_ref[...]


def _router(xt, router, tb=1024):
    T = xt.shape[0]
    tb = min(tb, T)
    return pl.pallas_call(
        _router_body, grid=(T // tb,),
        in_specs=[pl.BlockSpec((tb, D_MODEL), lambda i: (i, 0)), pl.BlockSpec((D_MODEL, LANES), lambda i: (0, 0))],
        out_specs=[pl.BlockSpec((tb, SC_WORDS), lambda i: (i, 0)), pl.BlockSpec((tb, SC_WORDS), lambda i: (i, 0)),
                   pl.BlockSpec((tb, LANES), lambda i: (i, 0)), pl.BlockSpec((8, LANES), lambda i: (0, 0))],
        out_shape=[jax.ShapeDtypeStruct((T, SC_WORDS), jnp.int32), jax.ShapeDtypeStruct((T, SC_WORDS), jnp.int32),
                   jax.ShapeDtypeStruct((T, LANES), f32), jax.ShapeDtypeStruct((8, LANES), f32)],
        scratch_shapes=[pltpu.VMEM((8, LANES), f32)],
        compiler_params=_cparams(("arbitrary",)), name="moe_router")(xt, router)


def _sc_gather(table, idx):
    n = idx.shape[0]
    mesh = plsc.VectorSubcoreMesh(core_axis_name="c", subcore_axis_name="s")

    @functools.partial(pl.kernel, out_type=jax.ShapeDtypeStruct((n, SC_WORDS), table.dtype), mesh=mesh)
    def gather(x_hbm, i_hbm, o_hbm):
        def body(i_vmem, o_vmem):
            pltpu.sync_copy(x_hbm.at[i_vmem.at[0]], o_vmem)

        pltpu.emit_pipeline(
            body, grid=(n // SC_WINDOW,),
            in_specs=[pl.BlockSpec((1, SC_WINDOW), index_map=lambda i: (0, i))],
            out_specs=[pl.BlockSpec((SC_WINDOW, SC_WORDS), index_map=lambda i: (i, 0))],
            core_axis_name=("c", "s"), dimension_semantics=(pltpu.PARALLEL,))(i_hbm, o_hbm)

    return gather(table, idx.reshape(1, n))


def _sc_scatter(rows, idx, n_out):
    R = rows.shape[0]
    n = idx.shape[0]
    nblk = R // SC_WINDOW
    mesh = plsc.VectorSubcoreMesh(core_axis_name="c", subcore_axis_name="s")

    @functools.partial(pl.kernel, out_type=jax.ShapeDtypeStruct((n_out, SC_WORDS), rows.dtype), mesh=mesh,
                       scratch_types=[])
    def scatter(x_hbm, i_hbm, o_hbm):
        def body(x_vmem, i_vmem):
            pltpu.sync_copy(x_vmem, o_hbm.at[i_vmem.at[0]])

        pltpu.emit_pipeline(
            body, grid=(n // SC_WINDOW,),
            in_specs=[pl.BlockSpec((SC_WINDOW, SC_WORDS), index_map=lambda i: (i % nblk, 0)),
                      pl.BlockSpec((1, SC_WINDOW), index_map=lambda i: (0, i))],
            out_specs=[], core_axis_name=("c", "s"), dimension_semantics=(pltpu.PARALLEL,))(x_hbm, i_hbm)

    return scatter(rows, idx.reshape(1, n))


def _experts_body(te_ref, rows_ref, xa_ref, xb_ref, w1_ref, w3_ref, w2_ref, oa_ref, ob_ref, acc_ref, x_ref):
    i = pl.program_id(0)
    j = pl.program_id(1)

    @pl.when(j == 0)
    def _():
        valid = lax.broadcasted_iota(jnp.int32, (MOE_TILE, 1), 0) < rows_ref[i]
        x_ref[...] = jnp.where(valid, _unpack_words(xa_ref[...], xb_ref[...]), 0.0).astype(bf16)
        acc_ref[...] = jnp.zeros(acc_ref.shape, f32)

    @pl.when(rows_ref[i] > 0)
    def _():
        x = x_ref[...]
        h1 = jnp.dot(x, w1_ref[0].astype(bf16), preferred_element_type=f32)
        h3 = jnp.dot(x, w3_ref[0].astype(bf16), preferred_element_type=f32)
        acc_ref[...] += _bdot(h1 * _sigmoid(h1) * h3, w2_ref[0])

    @pl.when(j == pl.num_programs(1) - 1)
    def _():
        oa_ref[...], ob_ref[...] = _pack_words(acc_ref[...])


def _experts(xa, xb, tile_expert, tile_rows, w1, w3, w2, tf=512):
    P = xa.shape[0]
    dff = w1.shape[2]
    words = pl.BlockSpec((MOE_TILE, SC_WORDS), lambda i, j, te, nt: (i, 0))
    grid_spec = pltpu.PrefetchScalarGridSpec(
        num_scalar_prefetch=2, grid=(P // MOE_TILE, dff // tf),
        in_specs=[words, words,
                  pl.BlockSpec((1, D_MODEL, tf), lambda i, j, te, nt: (te[i], 0, j)),
                  pl.BlockSpec((1, D_MODEL, tf), lambda i, j, te, nt: (te[i], 0, j)),
                  pl.BlockSpec((1, tf, D_MODEL), lambda i, j, te, nt: (te[i], j, 0))],
        out_specs=[words, words],
        scratch_shapes=[pltpu.VMEM((MOE_TILE, D_MODEL), f32), pltpu.VMEM((MOE_TILE, D_MODEL), bf16)])
    return pl.pallas_call(
        _experts_body, grid_spec=grid_spec, out_shape=[jax.ShapeDtypeStruct((P, SC_WORDS), jnp.int32)] * 2,
        compiler_params=_cparams(("parallel", "arbitrary")), name="moe_experts")(tile_expert, tile_rows, xa, xb, w1, w3, w2)


def _combine_body(x_ref, y0a_ref, y0b_ref, y1a_ref, y1b_ref, meta_ref, ln_ref, o_ref):
    meta = meta_ref[...]
    ff = (meta[:, 4:5] * _unpack_words(y0a_ref[...], y0b_ref[...])
          + meta[:, 5:6] * _unpack_words(y1a_ref[...], y1b_ref[...]))
    o_ref[...] = _layer_norm(ALPHA * x_ref[...] + ff, ln_ref[0:1, :], ln_ref[1:2, :])


def _combine(xt, yga, ygb, meta, ln, tm=1024):
    T = xt.shape[0]
    tm = min(tm, T)
    nb = T // tm
    row = pl.BlockSpec((tm, D_MODEL), lambda i: (i, 0))
    first = pl.BlockSpec((tm, SC_WORDS), lambda i: (i, 0))
    second = pl.BlockSpec((tm, SC_WORDS), lambda i: (nb + i, 0))
    return pl.pallas_call(
        _combine_body, grid=(nb,),
        in_specs=[row, first, first, second, second, pl.BlockSpec((tm, LANES), lambda i: (i, 0)),
                  pl.BlockSpec((2, D_MODEL), lambda i: (0, 0))],
        out_specs=row, out_shape=jax.ShapeDtypeStruct((T, D_MODEL), f32),
        compiler_params=_cparams(("parallel",)), name="moe_combine")(xt, yga, ygb, yga, ygb, meta, ln)


def _moe(xt, router, w1, w3, w2, ln):
    T = xt.shape[0]
    xa, xb, meta, cnt = _router(xt, router)
    counts = cnt[0, :N_EXPERTS].astype(jnp.int32)
    tiles = (counts + MOE_TILE - 1) // MOE_TILE
    tile_end = jnp.cumsum(tiles)
    offset = (tile_end - tiles) * MOE_TILE
    expert = meta[:, 0:2].astype(jnp.int32)
    onehot = expert[:, :, None] == jnp.arange(N_EXPERTS, dtype=jnp.int32)[None, None, :]
    pos = jnp.sum(jnp.where(onehot, offset[None, None, :], 0), axis=2) + meta[:, 2:4].astype(jnp.int32)
    pos = pos.T.reshape(-1)
    P = 2 * T + N_EXPERTS * MOE_TILE
    tile_id = jnp.arange(P // MOE_TILE, dtype=jnp.int32)
    tile_expert = jnp.minimum(jnp.sum((tile_id[:, None] >= tile_end[None, :]).astype(jnp.int32), axis=1), N_EXPERTS - 1)
    first_tile = (tile_end - tiles)[tile_expert]
    tile_rows = jnp.clip(counts[tile_expert] - (tile_id - first_tile) * MOE_TILE, 0, MOE_TILE)
    ya, yb = _experts(_sc_scatter(xa, pos, P), _sc_scatter(xb, pos, P), tile_expert, tile_rows, w1, w3, w2)
    return _combine(xt, _sc_gather(ya, pos), _sc_gather(yb, pos), meta, ln)


def kernel(x, w_in, b_in, att_gq, att_gk, rw_mix, rw_w0, rw_w2, rw_a0, rw_a2, rw_g2, rw_kk, rw_ka, rw_rk, rw_ln_g, rw_ln_b, s5_lam_re, s5_lam_im, s5_log_dt, s5_b_re, s5_b_im, s5_c_re, s5_c_im, s5_d, s5_glu_w, s5_glu_b, ml_conv_w, ml_conv_b, ml_ib, ml_fb, w_gate, b_gate, w_branch, w_out, ln1_g, ln1_b, ffn_w1, ffn_w3, ffn_w2, moe_router, moe_w1, moe_w3, moe_w2, ln2_g, ln2_b):
    B, S, D = x.shape
    assert 2 * B == 8 and D == D_MODEL and S % 512 == 0 and w_in.shape[0] == DEPTH, (x.shape, w_in.shape)
    xt = x.reshape(B * S, D)
    cos, sin = _rope_tables(S)
    for l in range(DEPTH):
        gain = jnp.concatenate([jnp.tile(att_gq[l], 8) * (HEAD_DIM ** -0.5), jnp.tile(att_gk[l], 2)])[None, :]
        (q, k, v, s5u, mq, mk, mv, mo, r, rk, rv, an, bn, lw, gate, bonus, gt) = _proj(
            xt, _proj_params(w_in[l], b_in[l], ml_ib[l], ml_fb[l]), (cos, sin, gain),
            (ml_conv_w[l], ml_conv_b[l][None, :]),
            _rw_params(rw_mix[l], rw_w0[l], rw_w2[l], rw_a0[l], rw_a2[l], rw_g2[l], rw_kk[l], rw_ka[l], rw_rk[l]), B, S)
        score_bound = 8.1 * jnp.max(jnp.abs(att_gq[l])) * jnp.max(jnp.abs(att_gk[l]))
        o_att = _flash(q, k, v, score_bound, B, S)
        yf, yb = _rwkv(r, rk, rv, an, bn, lw, B, S)
        y_s5 = _s5(s5u, _s5_params(s5_lam_re[l], s5_lam_im[l], s5_log_dt[l], s5_b_re[l], s5_b_im[l], s5_c_re[l],
                                   s5_c_im[l]), B, S)
        hf, hb = _mlstm(mq, mk, mv, gt, B, S)
        xt = _merge(xt, o_att, (yf, yb, gate, bonus), (y_s5, s5u), (hf, hb, mo), w_gate[l].astype(bf16), b_gate[l],
                    _att_branch_weight(w_branch[l, 0]).astype(bf16), w_branch[l, 1:].astype(bf16),
                    w_out[l].astype(bf16), jnp.stack([ln1_g[l], ln1_b[l]]), jnp.stack([rw_ln_g[l], rw_ln_b[l]]),
                    jnp.stack([s5_d[l], s5_glu_b[l]]), s5_glu_w[l].astype(bf16))
        ln2 = jnp.stack([ln2_g[l], ln2_b[l]])
        if l % 2 == 0:
            xt = _ffn(xt, ffn_w1[l // 2].astype(bf16), ffn_w3[l // 2].astype(bf16), ffn_w2[l // 2].astype(bf16), ln2)
        else:
            router = jnp.pad(moe_router[l // 2], ((0, 0), (0, LANES - N_EXPERTS)))
            xt = _moe(xt, router, moe_w1[l // 2], moe_w3[l // 2], moe_w2[l // 2], ln2)
    return xt.reshape(B, S, D)
```

```python
import functools
import math

import jax
import jax.numpy as jnp
import numpy as np
from jax import lax
from jax.experimental import pallas as pl
from jax.experimental.pallas import tpu as pltpu
from jax.experimental.pallas import tpu_sc as plsc

f32 = jnp.float32
bf16 = jnp.bfloat16

D_MODEL = 1024
DEPTH = 2
GRID_W = 64
BRANCH_W = 256
HEAD_DIM = 64
ATT_HEADS = 4
ATT_KV_HEADS = 2
ROPE_THETA = 10000.0
QK_EPS = 1e-6
RW_GN_EPS = 64e-5
RW_COLS = 1088
S5_GROUP = 16
S5_GROUPS = 16
S5_STATE = 64
ML_HEADS = 4
N_EXPERTS = 8
ALPHA = (2 * DEPTH) ** 0.25
LN_EPS = 1e-5

LANES = 128
CHUNK = 64
NEG = -1e30
VMEM_LIMIT = 56 * 1024 * 1024

PROJ_SPLITS = (768, 256, 512, 256, 256, RW_COLS)


def _cparams(sem):
    return pltpu.CompilerParams(dimension_semantics=sem, vmem_limit_bytes=VMEM_LIMIT)


def _sigmoid(x):
    return 1.0 / (1.0 + jnp.exp(-x))


def _softplus(x):
    return jnp.maximum(x, 0.0) + jnp.log(1.0 + jnp.exp(-jnp.abs(x)))


def _dims(a, lhs_c, rhs_c):
    lead = a.ndim - 2
    batch = tuple(range(lead))
    return (((lhs_c + lead,), (rhs_c + lead,)), (batch, batch))


def _bdot(a, b):
    return lax.dot_general(a.astype(bf16), b.astype(bf16), _dims(a, 1, 0), preferred_element_type=f32)


def _bdot_nt(a, b):
    return lax.dot_general(a.astype(bf16), b.astype(bf16), _dims(a, 1, 1), preferred_element_type=f32)


def _bdot_tn(a, b):
    return lax.dot_general(a.astype(bf16), b.astype(bf16), _dims(a, 0, 0), preferred_element_type=f32)


def _split(x):
    hi = x.astype(bf16)
    return hi, (x - hi.astype(f32)).astype(bf16)


def _sdot(a, b, exact):
    dims = _dims(a, 1, 0)
    if exact == "rhs":
        hi, lo = _split(a)
        bb = b.astype(bf16)
        return (lax.dot_general(hi, bb, dims, preferred_element_type=f32)
                + lax.dot_general(lo, bb, dims, preferred_element_type=f32))
    hi, lo = _split(b)
    ab = a.astype(bf16)
    return (lax.dot_general(ab, hi, dims, preferred_element_type=f32)
            + lax.dot_general(ab, lo, dims, preferred_element_type=f32))


def _sdot3(a, b):
    dims = _dims(a, 1, 0)
    ah, al = _split(a)
    bh, bl = _split(b)
    return (lax.dot_general(ah, bh, dims, preferred_element_type=f32)
            + lax.dot_general(ah, bl, dims, preferred_element_type=f32)
            + lax.dot_general(al, bh, dims, preferred_element_type=f32))


def _seg_matrix(n, seg=HEAD_DIM):
    r = lax.broadcasted_iota(jnp.int32, (n, n), 0) // seg
    c = lax.broadcasted_iota(jnp.int32, (n, n), 1) // seg
    return (r == c).astype(f32)


def _layer_norm(y, g, b):
    mu = jnp.mean(y, axis=-1, keepdims=True)
    d = y - mu
    var = jnp.mean(d * d, axis=-1, keepdims=True)
    return d * lax.rsqrt(var + LN_EPS) * g + b


def _row_to_col(row, eye):
    n = eye.shape[-1]
    return jnp.sum(jnp.where(eye, jnp.broadcast_to(row, (row.shape[0], n, n)), 0.0), axis=2, keepdims=True)


def _stack_heads(x):
    h0 = lax.broadcasted_iota(jnp.int32, x.shape, 2) < HEAD_DIM
    return jnp.concatenate([jnp.where(h0, x, 0.0), jnp.where(h0, 0.0, x)], axis=1)


def _proj_body(x_ref, xp_ref, xn_ref, cos_ref, sin_ref, w_ref, b_ref, wg_ref, bg_ref, gain_ref, cw_ref, cb_ref,
               mix_ref, w2_ref, a2_ref, g2_ref, vec_ref,
               q_ref, k_ref, v_ref, s5_ref, mq_ref, mk_ref, mv_ref, mo_ref, r_ref, rk_ref, rv_ref, an_ref, bn_ref,
               lw_ref, gate_ref, bonus_ref, g_ref, mbuf_ref, rbuf_ref):
    tm = x_ref.shape[0]
    xb, xp, xn = x_ref[...].astype(bf16), xp_ref[...].astype(bf16), xn_ref[...].astype(bf16)
    offs = np.cumsum((0,) + PROJ_SPLITS)

    def cols(rows, n):
        sl = slice(int(offs[n]), int(offs[n + 1]))
        return jnp.dot(rows, w_ref[:, sl], preferred_element_type=f32) + b_ref[:, sl]

    q_ref[...], k_ref[...], v_ref[...] = _att_prep(cols(xb, 0), cos_ref[...], sin_ref[...], gain_ref[...])
    s5_ref[...] = cols(xb, 1)
    _fill_halo(mbuf_ref, cols(xb, 2), cols(xp, 2), cols(xn, 2))
    mq_ref[...], mk_ref[...] = _ml_prep(mbuf_ref, tm, cw_ref[...], cb_ref[...])
    mv_ref[...] = cols(xb, 3)
    mo_ref[...] = cols(xb, 4)
    _fill_halo(rbuf_ref, cols(xb, 5), cols(xp, 5), cols(xn, 5))
    outs = _rw_prep(rbuf_ref, tm, mix_ref[...], w2_ref[...], a2_ref[...], g2_ref[...], vec_ref[...])
    for o_ref, val in zip((r_ref, rk_ref, rv_ref, an_ref, bn_ref), outs[0:5]):
        o_ref[...] = val
    lw_ref[0], lw_ref[1], gate_ref[...], bonus_ref[...] = outs[5:9]
    g_ref[...] = lax.dot_general(wg_ref[...], xb, (((1,), (1,)), ((), ())), preferred_element_type=f32) + bg_ref[...]


def _proj(xt, proj_prm, att_prm, ml_prm, rw_prm, B, S, tm=512):
    T = B * S
    nb = S // tm
    n_tot = sum(PROJ_SPLITS)
    row = lambda n: pl.BlockSpec((tm, n), lambda b, i: (b * nb + i, 0))
    const = lambda a: pl.BlockSpec(a.shape, lambda b, i: (0,) * a.ndim)
    tab = pl.BlockSpec((tm, LANES), lambda b, i: (i, 0))
    f32out = lambda n: jax.ShapeDtypeStruct((T, n), f32)
    consts = list(proj_prm) + [att_prm[2]] + list(ml_prm) + list(rw_prm)
    out_specs = ([row(512), row(LANES), row(2 * LANES)] + [row(BRANCH_W)] * 10
                 + [pl.BlockSpec((2, tm, BRANCH_W), lambda b, i: (0, b * nb + i, 0)), row(BRANCH_W), row(BRANCH_W),
                    pl.BlockSpec((16, tm), lambda b, i: (0, b * nb + i))])
    out_shape = ([jax.ShapeDtypeStruct((T, 512), bf16), jax.ShapeDtypeStruct((T, LANES), bf16),
                  jax.ShapeDtypeStruct((T, 2 * LANES), bf16)] + [f32out(BRANCH_W)] * 10
                 + [jax.ShapeDtypeStruct((2, T, BRANCH_W), f32), f32out(BRANCH_W), f32out(BRANCH_W),
                    jax.ShapeDtypeStruct((16, T), f32)])
    return pl.pallas_call(
        _proj_body, grid=(B, nb),
        in_specs=_halo_specs(D_MODEL, tm, B, S) + [tab, tab] + [const(a) for a in consts],
        out_specs=out_specs, out_shape=out_shape,
        scratch_shapes=[pltpu.VMEM((tm + 16, 512), f32), pltpu.VMEM((tm + 16, RW_COLS), f32)],
        compiler_params=_cparams(("parallel", "parallel")), name="proj")(
            xt, xt, xt, att_prm[0], att_prm[1], *consts)


def _proj_params(w_in, b_in, ml_ib, ml_fb):
    o = np.cumsum((0, 256, 128, 128, RW_COLS, 256, 512, 256, 8, 8, 256))
    sl = lambda i: (w_in[:, o[i]:o[i + 1]], b_in[o[i]:o[i + 1]])
    (wq, bq), (wk, bk), (wv, bv), (wrw, brw), (ws5, bs5), (wqk, bqk), (wmv, bmv), (wi, bi), (wf, bf), (wo, bo) = (
        sl(i) for i in range(10))
    zw, zb = jnp.zeros((D_MODEL, HEAD_DIM), f32), jnp.zeros((HEAD_DIM,), f32)
    wq_e, bq_e = [], []
    for h in range(ATT_HEADS):
        wh, bh = wq[:, 64 * h:64 * h + 64], bq[64 * h:64 * h + 64]
        wq_e += [wh, zw] if h // 2 == 0 else [zw, wh]
        bq_e += [bh, zb] if h // 2 == 0 else [zb, bh]
    w = jnp.concatenate(wq_e + [wk, wv, ws5, wqk, wmv, wo, wrw], axis=1)
    b = jnp.concatenate(bq_e + [bk, bv, bs5, bqk, bmv, bo, brw])
    wg = jnp.concatenate([wi, wf], axis=1).T
    bg = jnp.concatenate([bi + ml_ib.reshape(-1), bf + ml_fb.reshape(-1)])
    return w.astype(bf16), b[None, :], wg.astype(bf16), bg[:, None]


def _att_prep(att, cos, sin, gain):
    x = att[:, 0:640]
    ms = _sdot(x * x, _seg_matrix(640), "rhs") * (1.0 / HEAD_DIM)
    xn = x * lax.rsqrt(ms + QK_EPS) * gain
    lane = lax.broadcasted_iota(jnp.int32, xn.shape, 1)
    partner = jnp.where((lane % 32) < 16, pltpu.roll(xn, 640 - 16, 1), pltpu.roll(xn, 16, 1))
    rot = xn * jnp.concatenate([cos] * 5, axis=1) + partner * jnp.concatenate([sin] * 5, axis=1)
    v = jnp.concatenate([att[:, 640:768].astype(bf16), jnp.ones((x.shape[0], LANES), bf16)], axis=1)
    return rot[:, 0:512].astype(bf16), rot[:, 512:640].astype(bf16), v


def _rope_tables(S):
    t = np.arange(S)
    row = (t // GRID_W).astype(np.float32)
    col = (t % GRID_W).astype(np.float32)
    n = 16
    inv = np.power(np.float32(ROPE_THETA), -np.arange(n, dtype=np.float32) / n).astype(np.float32)
    ar = jnp.asarray(row)[:, None] * jnp.asarray(inv)
    ac = jnp.asarray(col)[:, None] * jnp.asarray(inv)
    cos = jnp.concatenate([jnp.cos(ar), jnp.cos(ar), jnp.cos(ac), jnp.cos(ac)], axis=1)
    sin = jnp.concatenate([-jnp.sin(ar), jnp.sin(ar), -jnp.sin(ac), jnp.sin(ac)], axis=1)
    return jnp.concatenate([cos, cos], axis=1), jnp.concatenate([sin, sin], axis=1)


def _flash_body(q_ref, k_ref, v_ref, o_ref, acc_ref, *m_scratch, tk, track_max):
    tq = q_ref.shape[0]
    nk = k_ref.shape[0] // tk
    q2 = jnp.concatenate([q_ref[:, 0:LANES], q_ref[:, LANES:2 * LANES]], axis=0)
    acc_ref[...] = jnp.zeros(acc_ref.shape, f32)
    if track_max:
        m_ref, = m_scratch
        m_ref[...] = jnp.full(m_ref.shape, NEG, f32)

    def step(j, carry):
        rows = pl.ds(pl.multiple_of(j * tk, tk), tk)
        s = lax.dot_general(q2, k_ref[rows, :], (((1,), (1,)), ((), ())), preferred_element_type=f32)
        if track_max:
            m_old = m_ref[...]
            m_new = jnp.maximum(m_old, jnp.max(s, axis=1, keepdims=True))
            p = jnp.exp(s - m_new).astype(bf16)
            acc_ref[...] = jnp.exp(m_old - m_new) * acc_ref[...] + jnp.dot(p, v_ref[rows, :], preferred_element_type=f32)
            m_ref[...] = m_new
        else:
            acc_ref[...] += jnp.dot(jnp.exp(s).astype(bf16), v_ref[rows, :], preferred_element_type=f32)
        return carry

    lax.fori_loop(0, nk, step, 0)
    o = acc_ref[:, 0:LANES] / acc_ref[:, LANES:2 * LANES]
    o_ref[...] = jnp.concatenate([o[0:tq], o[tq:2 * tq]], axis=1)


SCORE_BOUND_MAX = 60.0


def _flash(q, k, v, score_bound, B, S, tq=256, tk=8192):
    T = B * S
    nb = S // tq
    tk = min(tk, S)

    def call(track_max):
        scratch = [pltpu.VMEM((2 * tq, 2 * LANES), f32)] + ([pltpu.VMEM((2 * tq, 1), f32)] if track_max else [])
        return pl.pallas_call(
            functools.partial(_flash_body, tk=tk, track_max=track_max), grid=(B, ATT_KV_HEADS, nb),
            in_specs=[pl.BlockSpec((tq, 2 * LANES), lambda b, g, i: (b * nb + i, g)),
                      pl.BlockSpec((S, LANES), lambda b, g, i: (b, 0)),
                      pl.BlockSpec((S, 2 * LANES), lambda b, g, i: (b, 0))],
            out_specs=pl.BlockSpec((tq, 2 * LANES), lambda b, g, i: (b * nb + i, g)),
            out_shape=jax.ShapeDtypeStruct((T, 512), f32), scratch_shapes=scratch,
            compiler_params=_cparams(("parallel", "parallel", "parallel")),
            name="flash_safe" if track_max else "flash")(q, k, v)

    return lax.cond(score_bound <= SCORE_BOUND_MAX, lambda: call(False), lambda: call(True))


def _halo_specs(width, tm, B, S):
    nb = S // tm
    r8 = tm // 8
    last8 = B * S // 8 - 1

    def main(b, i):
        return (b * nb + i, 0)

    def prev(b, i):
        return (jnp.maximum(b * (S // 8) + i * r8 - 1, 0), 0)

    def nxt(b, i):
        return (jnp.minimum(b * (S // 8) + (i + 1) * r8, last8), 0)

    return [pl.BlockSpec((tm, width), main), pl.BlockSpec((8, width), prev), pl.BlockSpec((8, width), nxt)]


def _fill_halo(buf_ref, x, prev8, next8):
    tm = x.shape[0]
    i = pl.program_id(1)
    last = pl.num_programs(1) - 1
    buf_ref[pl.ds(8, tm), :] = x
    buf_ref[pl.ds(0, 8), :] = jnp.where(i > 0, prev8, 0.0)
    buf_ref[pl.ds(8 + tm, 8), :] = jnp.where(i < last, next8, 0.0)


def _rw_prep(buf_ref, tm, mix, w2, a2, g2, vec):
    x = buf_ref[pl.ds(8, tm), :]
    p = x + mix[0:1, :] * (buf_ref[pl.ds(7, tm), :] - x) + mix[1:2, :] * (buf_ref[pl.ds(9, tm), :] - x)
    r, k, v = p[:, 0:256], p[:, 256:512], p[:, 512:768]
    w0f, w0b, a0, k_k, k_a, r_k = (vec[j:j + 1, :] for j in range(6))
    dec = _bdot(jnp.tanh(p[:, 768:896]), w2)
    z = p[:, 896:1088]
    a = _sigmoid(a0 + _bdot(z, a2))
    gate = _bdot(_sigmoid(z), g2)
    seg = _seg_matrix(BRANCH_W)
    kk = k * k_k
    kk = kk / jnp.maximum(jnp.sqrt(_sdot(kk * kk, seg, "rhs")), 1e-12)
    k2 = k * (1.0 + (a - 1.0) * k_a)
    bonus = _sdot(r * k2 * r_k, seg, "rhs") * v
    lwf = -jnp.exp(-_softplus(-(w0f + dec[:, 0:256])) - 0.5)
    lwb = -jnp.exp(-_softplus(-(w0b + dec[:, 256:512])) - 0.5)
    return r, k2, v, -kk, kk * a, lwf, lwb, gate, bonus


def _by_direction(x, fwd, bwd, fn):
    h = x.shape[0] // 2
    return jnp.concatenate([fn(x[0:h], fwd), fn(x[h:2 * h], bwd)], axis=0)


def _keep(x, masks):
    return _by_direction(x, masks[0], masks[1], lambda t, m: jnp.where(m, t, 0.0))


def _both(masks, nchain):
    h = nchain // 2
    return jnp.concatenate([jnp.broadcast_to(m.astype(f32), (h,) + m.shape[1:]) for m in masks], axis=0)


def _pair_masks():
    n = 2 * CHUNK
    r = lax.broadcasted_iota(jnp.int32, (1, n, n), 1)
    c = lax.broadcasted_iota(jnp.int32, (1, n, n), 2)
    same = (r // CHUNK) == (c // CHUNK)
    return r, c, same, (same & (c < r), same & (c > r)), (same & (c <= r), same & (c >= r))


def _rw_chunk(st, r, k, v, an, bn, lw):
    L = CHUNK
    N = r.shape[0]
    ri = lax.broadcasted_iota(jnp.int32, (1, L, L), 1)
    ci = lax.broadcasted_iota(jnp.int32, (1, L, L), 2)
    cs = _sdot(_both((ci <= ri, ci >= ri), N), lw, "lhs")
    tot = jnp.sum(lw, axis=1, keepdims=True)
    e_neg = jnp.exp(-cs)
    at = an * jnp.exp(cs - lw)
    rt = r * jnp.exp(cs)
    a2, b2, k2, v2 = _stack_heads(at), _stack_heads(bn * e_neg), _stack_heads(k * e_neg), _stack_heads(v)
    t = lax.broadcasted_iota(jnp.int32, (1, L, LANES), 1)
    tc = lax.broadcasted_iota(jnp.int32, (1, L, LANES), 2) % L
    strict, incl = (tc < t, tc > t), (tc <= t, tc >= t)
    g = _bdot_nt(jnp.concatenate([at, rt], axis=1), jnp.concatenate([b2, k2], axis=1))
    mab = _keep(g[:, 0:L, 0:LANES], strict)
    mak = _keep(g[:, 0:L, LANES:2 * LANES], strict)
    pb = _keep(g[:, L:2 * L, 0:LANES], incl)
    pk = _keep(g[:, L:2 * L, LANES:2 * LANES], incl)
    mul = lambda p, q: _bdot(p, _stack_heads(q))
    m8 = jnp.where((t // 8) == (tc // 8), mab, 0.0)
    x = (t == tc).astype(f32) + m8
    p = mul(m8, m8)
    x = x + mul(x, p)
    p = mul(p, p)
    x = x + mul(x, p)
    n = 8
    while n < L:
        e = jnp.where(((t // (2 * n)) == (tc // (2 * n))) & ((t // n) != (tc // n)), mab, 0.0)
        x = x + mul(mul(x, e), x)
        n *= 2
    wu = _bdot(x, jnp.concatenate([a2, _stack_heads(_bdot(mak, v2))], axis=2))
    wu2 = jnp.concatenate([_stack_heads(wu[:, :, 0:LANES]), _stack_heads(wu[:, :, LANES:2 * LANES])], axis=2)
    pwu = _bdot(pb, wu2)
    rh = rt + pwu[:, :, 0:LANES]
    y = pwu[:, :, LANES:2 * LANES] + _bdot(jnp.concatenate([pk, rh], axis=2), jnp.concatenate([v2, st], axis=1))
    r128 = lax.broadcasted_iota(jnp.int32, (1, LANES, LANES), 1)
    c128 = lax.broadcasted_iota(jnp.int32, (1, LANES, LANES), 2)
    gam = _row_to_col(jnp.exp(tot), r128 == c128)
    bwu = _bdot_tn(b2, wu2)
    st = gam * (st + _bdot(bwu[:, :, 0:LANES], st) + bwu[:, :, LANES:2 * LANES] + _bdot_tn(k2, v2))
    return y, st


def _load_pairs(ref, rows):
    return jnp.concatenate([ref[:, rows, 0:LANES], ref[:, rows, LANES:2 * LANES]], axis=0)


def _store_pairs(ref, rows, y):
    nb = ref.shape[0]
    ref[:, rows, 0:LANES] = y[0:nb]
    ref[:, rows, LANES:2 * LANES] = y[nb:2 * nb]


def _load_both(f_ref, b_ref, rows_f, rows_b):
    return jnp.concatenate([_load_pairs(f_ref, rows_f), _load_pairs(b_ref, rows_b)], axis=0)


def _chunk_rows(cc, nch):
    return (pl.ds(pl.multiple_of(cc * CHUNK, CHUNK), CHUNK),
            pl.ds(pl.multiple_of((nch - 1 - cc) * CHUNK, CHUNK), CHUNK))


def _rw_scan_body(rf_ref, rb_ref, kf_ref, kb_ref, vf_ref, vb_ref, anf_ref, anb_ref, bnf_ref, bnb_ref, lwf_ref, lwb_ref,
                  yf_ref, yb_ref, st_ref):
    nch = rf_ref.shape[1] // CHUNK
    half = st_ref.shape[0] // 2

    @pl.when(pl.program_id(0) == 0)
    def _():
        st_ref[...] = jnp.zeros(st_ref.shape, f32)

    def step(cc, carry):
        rows_f, rows_b = _chunk_rows(cc, nch)
        pairs = ((rf_ref, rb_ref), (kf_ref, kb_ref), (vf_ref, vb_ref), (anf_ref, anb_ref), (bnf_ref, bnb_ref),
                 (lwf_ref.at[0], lwb_ref.at[0]))
        y, st = _rw_chunk(st_ref[...], *(_load_both(f, b, rows_f, rows_b) for f, b in pairs))
        _store_pairs(yf_ref, rows_f, y[0:half])
        _store_pairs(yb_ref, rows_b, y[half:2 * half])
        st_ref[...] = st
        return carry

    lax.fori_loop(0, nch, step, 0)


def _rw_scan(r, k, v, an, bn, lw, B, S, ts=128):
    nb = S // ts
    fwd = pl.BlockSpec((B, ts, BRANCH_W), lambda i: (0, i, 0))
    bwd = pl.BlockSpec((B, ts, BRANCH_W), lambda i: (0, nb - 1 - i, 0))
    out = jax.ShapeDtypeStruct((B, S, BRANCH_W), f32)
    return pl.pallas_call(
        _rw_scan_body, grid=(nb,),
        in_specs=[fwd, bwd] * 5 + [pl.BlockSpec((1, B, ts, BRANCH_W), lambda i: (0, 0, i, 0)),
                                   pl.BlockSpec((1, B, ts, BRANCH_W), lambda i: (1, 0, nb - 1 - i, 0))],
        out_specs=[fwd, bwd], out_shape=[out, out],
        scratch_shapes=[pltpu.VMEM((2 * B * BRANCH_W // LANES, LANES, LANES), f32)],
        compiler_params=_cparams(("arbitrary",)), name="rw_scan")(r, r, k, k, v, v, an, an, bn, bn, lw, lw)


def _rw_finish(y, gate, bonus, gn):
    seg = _seg_matrix(BRANCH_W)
    mu = _sdot(y, seg, "rhs") * (1.0 / HEAD_DIM)
    d = y - mu
    var = _sdot(d * d, seg, "rhs") * (1.0 / HEAD_DIM)
    yn = d * lax.rsqrt(var + RW_GN_EPS) * gn[0:1, :] + gn[1:2, :]
    return (yn + bonus) * gate


def _rwkv(r, k, v, an, bn, lw, B, S):
    seq = [t.reshape(B, S, BRANCH_W) for t in (r, k, v, an, bn)] + [lw.reshape(2, B, S, BRANCH_W)]
    yf, yb = _rw_scan(*seq, B, S)
    return yf.reshape(B * S, BRANCH_W), yb.reshape(B * S, BRANCH_W)


def _rw_params(mix, w0, w2, a0, a2, g2, k_k, k_a, r_k):
    z = jnp.zeros((64, 256), f32)
    w2c = jnp.concatenate([jnp.concatenate([w2[0], z], axis=1), jnp.concatenate([z, w2[1]], axis=1)], axis=0)
    a2p = jnp.concatenate([a2, jnp.zeros((128, 256), f32)], axis=0)
    g2p = jnp.concatenate([jnp.zeros((64, 256), f32), g2], axis=0)
    vec = jnp.stack([w0[0], w0[1], a0, k_k, k_a, r_k.reshape(-1), jnp.zeros_like(a0), jnp.zeros_like(a0)])
    return mix, w2c.astype(bf16), a2p.astype(bf16), g2p.astype(bf16), vec


S5_HALF = S5_GROUPS * S5_STATE // 2


def _s5_scan_body(uf_ref, ub_ref, bh_ref, cre_ref, cim_ref, lre_ref, lim_ref, yf_ref, yb_ref,
                  sre_ref, sim_ref, bre_ref, bim_ref):
    ts = uf_ref.shape[0]
    n = S5_HALF

    @pl.when(pl.program_id(0) == 0)
    def _():
        sre_ref[...] = jnp.zeros(sre_ref.shape, f32)
        sim_ref[...] = jnp.zeros(sim_ref.shape, f32)

    row_half = lax.broadcasted_iota(jnp.int32, uf_ref.shape, 1) // 4
    lane_half = lax.broadcasted_iota(jnp.int32, uf_ref.shape, 2) // (BRANCH_W // 2)
    for d, u_ref in enumerate((uf_ref, ub_ref)):
        lhs = jnp.where(row_half == lane_half, u_ref[...], 0.0).reshape(ts * 8, BRANCH_W)
        bu = _bdot(lhs, bh_ref[d]).reshape(ts, 8, 2 * n)
        bre_ref[d] = bu[:, :, 0:n]
        bim_ref[d] = bu[:, :, n:2 * n]
    lre = lre_ref[...]
    lim = lim_ref[...]

    def step(t, carry):
        out = []
        for d, td in enumerate((t, ts - 1 - t)):
            sre, sim = carry[2 * d], carry[2 * d + 1]
            nre = lre[d] * sre - lim[d] * sim + bre_ref[d, td]
            nim = lre[d] * sim + lim[d] * sre + bim_ref[d, td]
            bre_ref[d, td] = nre
            bim_ref[d, td] = nim
            out += [nre, nim]
        return tuple(out)

    s = lax.fori_loop(0, ts, step, (sre_ref[0], sim_ref[0], sre_ref[1], sim_ref[1]), unroll=4)
    sre_ref[0], sim_ref[0], sre_ref[1], sim_ref[1] = s
    low = lane_half == 0
    for d, y_ref in enumerate((yf_ref, yb_ref)):
        yv = (_bdot(bre_ref[d].reshape(ts * 8, n), cre_ref[d])
              - _bdot(bim_ref[d].reshape(ts * 8, n), cim_ref[d])).reshape(ts, 8, BRANCH_W)
        y_ref[...] = jnp.where(low, yv, pltpu.roll(yv, 4, 1))


def _s5_scan(u8, bh, cre, cim, lre, lim, ts=128):
    S = u8.shape[0]
    nb = S // ts
    n = S5_HALF
    full = lambda shape: pl.BlockSpec(shape, lambda i: (0,) * len(shape))
    fwd = pl.BlockSpec((ts, 8, BRANCH_W), lambda i: (i, 0, 0))
    bwd = pl.BlockSpec((ts, 8, BRANCH_W), lambda i: (nb - 1 - i, 0, 0))
    out = jax.ShapeDtypeStruct((S, 8, BRANCH_W), f32)
    return pl.pallas_call(
        _s5_scan_body, grid=(nb,),
        in_specs=[fwd, bwd, full((2, BRANCH_W, 2 * n)), full((2, n, BRANCH_W)), full((2, n, BRANCH_W)),
                  full((2, 8, n)), full((2, 8, n))],
        out_specs=[fwd, bwd], out_shape=[out, out],
        scratch_shapes=[pltpu.VMEM((2, 8, n), f32), pltpu.VMEM((2, 8, n), f32),
                        pltpu.VMEM((2, ts, 8, n), f32), pltpu.VMEM((2, ts, 8, n), f32)],
        compiler_params=_cparams(("arbitrary",)), name="s5_scan")(u8, u8, bh, cre, cim, lre, lim)


def _s5_finish(y, u, vec, w):
    y = y + u * vec[0:1, :]
    y = 0.5 * y * (1.0 + jnp.tanh(math.sqrt(2.0 / math.pi) * (y + 0.044715 * (y * y * y))))
    return y * _sigmoid(_bdot(y, w) + vec[1:2, :])


def _s5_params(lam_re, lam_im, log_dt, b_re, b_im, c_re, c_im):
    G, P, C = S5_GROUPS, S5_STATE, S5_GROUP
    H = G // 2
    eye = jnp.eye(H, dtype=f32)
    b_c = lax.complex(b_re, b_im)

    def b_part(x):
        return jnp.einsum('ab,hapc->hacbp', eye, x.reshape(2, H, P, C)).reshape(G * C, H * P)

    def c_part(x):
        return jnp.einsum('ab,hacp->aphbc', eye, x.reshape(2, H, C, P)).reshape(H * P, G * C)

    def rows(x):
        return jnp.broadcast_to(x.reshape(2, 1, H * P), (2, 4, H * P)).reshape(8, H * P)

    bh, cre, cim, lre, lim = [], [], [], [], []
    for d in range(2):
        lam = lax.complex(jnp.minimum(lam_re[d], -1e-4), lam_im[d])
        lam_bar = jnp.exp(lam * jnp.exp(log_dt[d])[:, None])
        b_bar = ((lam_bar - 1.0) / lam)[..., None] * b_c
        bh.append(jnp.concatenate([b_part(jnp.real(b_bar)), b_part(jnp.imag(b_bar))], axis=1))
        cre.append(c_part(c_re[d]))
        cim.append(c_part(c_im[d]))
        lre.append(rows(jnp.real(lam_bar)))
        lim.append(rows(jnp.imag(lam_bar)))
    return (jnp.stack(bh).astype(bf16), jnp.stack(cre).astype(bf16), jnp.stack(cim).astype(bf16),
            jnp.stack(lre), jnp.stack(lim))


def _s5(u, prm, B, S):
    u3 = u.reshape(B, S, BRANCH_W).transpose(1, 0, 2)
    yf, yb = _s5_scan(jnp.concatenate([u3, u3], axis=1), *prm)
    return (yf[:, 0:B] + yb[:, 0:B]).transpose(1, 0, 2).reshape(B * S, BRANCH_W)


def _ml_prep(buf_ref, tm, w, b):
    y = b
    for j in range(5):
        y = y + w[j:j + 1, :] * buf_ref[pl.ds(6 + j, tm), :]
    y = y * _sigmoid(y)
    return y[:, 0:BRANCH_W], y[:, BRANCH_W:2 * BRANCH_W] * (HEAD_DIM ** -0.5)


def _ml_chunk(state, q, k, v, li, lfp):
    cn, m_row = state
    L = CHUNK
    N = q.shape[0]
    rr, cc, same, _, incl2 = _pair_masks()
    same_f = _both((same, same), N)
    lane = lax.broadcasted_iota(jnp.int32, (1, 1, 2 * L), 2)
    lf = jnp.minimum(lfp, 0.0) - jnp.log(1.0 + jnp.exp(-jnp.abs(lfp)))
    lf8 = jnp.broadcast_to(lf, (N, 8, 2 * L))
    b_row = _sdot(lf8, _both((incl2[1], incl2[0]), N), "rhs")[:, 0:1]
    g_row = _sdot(lf8, same_f, "rhs")[:, 0:1]
    w_end = g_row - b_row + li
    m0 = jnp.max(jnp.where(lane < L, w_end, NEG), axis=2, keepdims=True)
    m1 = jnp.max(jnp.where(lane < L, NEG, w_end), axis=2, keepdims=True)
    m_loc = jnp.where(lane < L, m0, m1)
    t = lax.broadcasted_iota(jnp.int32, (1, L, LANES), 1)
    tc = lax.broadcasted_iota(jnp.int32, (1, L, LANES), 2) % L
    diag = t == tc

    def cols(row):
        return _sdot(jnp.where(diag, jnp.broadcast_to(row, (N, L, LANES)), 0.0), same_f, "rhs")

    e_col, b_col = cols(jnp.exp(w_end - m_loc)), cols(b_row)
    k2, v2 = _stack_heads(k), _stack_heads(v)
    v1 = jnp.concatenate([v2, _stack_heads(jnp.ones_like(v))], axis=2)
    log_inter = b_col + m_row
    log_intra = _by_direction(b_col - b_row + li, tc <= t, tc >= t, lambda x, m: jnp.where(m, x, NEG))
    head0 = lane < L
    r0 = jnp.max(jnp.where(head0, log_intra, NEG), axis=2, keepdims=True)
    r1 = jnp.max(jnp.where(head0, NEG, log_intra), axis=2, keepdims=True)
    m_r = jnp.maximum(log_inter, jnp.where(head0, r0, r1))
    s = _bdot_nt(q, k2) * jnp.exp(log_intra - m_r)
    inter = jnp.exp(log_inter - m_r)
    nd = _bdot(s, v1) + jnp.concatenate([inter, inter], axis=2) * _bdot(q, cn)
    h = nd[:, :, 0:LANES] / jnp.maximum(jnp.abs(nd[:, :, LANES:2 * LANES]), jnp.exp(-m_r))
    m_new = jnp.maximum(g_row + m_row, m_loc)
    a = jnp.exp(g_row + m_row - m_new)
    bb = jnp.exp(m_loc - m_new)
    cn = (jnp.concatenate([a, a], axis=2) * cn
          + jnp.concatenate([bb, bb], axis=2) * _bdot_tn(_stack_heads(e_col * k), v1))
    return h, (cn, m_new)


GATE_ROWS = 8


def _ml_scan_body(qf_ref, qb_ref, kf_ref, kb_ref, vf_ref, vb_ref, gf_ref, gb_ref, hf_ref, hb_ref, c_ref, m_ref):
    nch = qf_ref.shape[1] // CHUNK
    half = c_ref.shape[0] // 2
    i = pl.program_id(0)
    nb = pl.num_programs(0)
    per = GATE_ROWS // nch
    base_f = (i % per) * nch
    base_b = ((nb - 1 - i) % per) * nch

    @pl.when(i == 0)
    def _():
        c_ref[...] = jnp.zeros(c_ref.shape, f32)
        m_ref[...] = jnp.zeros(m_ref.shape, f32)

    def step(cc, carry):
        rows_f, rows_b = _chunk_rows(cc, nch)
        gate = lambda t: jnp.concatenate(
            [gf_ref[t, 0, 0, :, pl.ds(base_f + cc, 1), :], gf_ref[t, 0, 1, :, pl.ds(base_f + cc, 1), :],
             gb_ref[t, 0, 0, :, pl.ds(base_b + nch - 1 - cc, 1), :], gb_ref[t, 0, 1, :, pl.ds(base_b + nch - 1 - cc, 1), :]],
            axis=0)
        h, (cn, m_row) = _ml_chunk((c_ref[...], m_ref[...]), _load_both(qf_ref, qb_ref, rows_f, rows_b),
                                   _load_both(kf_ref, kb_ref, rows_f, rows_b), _load_both(vf_ref, vb_ref, rows_f, rows_b),
                                   gate(0), gate(1))
        _store_pairs(hf_ref, rows_f, h[0:half])
        _store_pairs(hb_ref, rows_b, h[half:2 * half])
        c_ref[...] = cn
        m_ref[...] = m_row
        return carry

    lax.fori_loop(0, nch, step, 0)


def _ml_scan(q, k, v, g, B, S, ts=256):
    nb = S // ts
    per = GATE_ROWS * CHUNK // ts
    nchain = 2 * B * BRANCH_W // LANES
    fwd = pl.BlockSpec((B, ts, BRANCH_W), lambda i: (0, i, 0))
    bwd = pl.BlockSpec((B, ts, BRANCH_W), lambda i: (0, nb - 1 - i, 0))
    out = jax.ShapeDtypeStruct((B, S, BRANCH_W), f32)
    return pl.pallas_call(
        _ml_scan_body, grid=(nb,),
        in_specs=[fwd, bwd] * 3 + [pl.BlockSpec((2, 1, 2, B, GATE_ROWS, LANES), lambda i: (0, 0, 0, 0, i // per, 0)),
                                   pl.BlockSpec((2, 1, 2, B, GATE_ROWS, LANES),
                                                lambda i: (0, 1, 0, 0, (nb - 1 - i) // per, 0))],
        out_specs=[fwd, bwd], out_shape=[out, out],
        scratch_shapes=[pltpu.VMEM((nchain, LANES, 2 * LANES), f32), pltpu.VMEM((nchain, 1, LANES), f32)],
        compiler_params=_cparams(("arbitrary",)), name="ml_scan")(q, q, k, k, v, v, g, g)


def _mlstm(q, k, mv, gt, B, S):
    g = gt.reshape(2, 2, 2, 2, B, S // CHUNK, CHUNK).transpose(0, 1, 2, 4, 5, 3, 6).reshape(2, 2, 2, B, S // CHUNK, LANES)
    hf, hb = _ml_scan(*(t.reshape(B, S, BRANCH_W) for t in (q, k, mv)), g, B, S)
    return hf.reshape(B * S, BRANCH_W), hb.reshape(B * S, BRANCH_W)


def _merge_body(x_ref, att_ref, yf_ref, yb_ref, rg_ref, rb_ref, sy_ref, su_ref, hf_ref, hb_ref, mo_ref,
                wg_ref, bg_ref, wba_ref, wb_ref, wo_ref, ln_ref, gn_ref, sv_ref, sw_ref, o_ref):
    x = x_ref[...]
    xb = x.astype(bf16)
    rw = _rw_finish(yf_ref[...] + yb_ref[...], rg_ref[...], rb_ref[...], gn_ref[...])
    s5 = _s5_finish(sy_ref[...], su_ref[...], sv_ref[...], sw_ref[...])
    ml = _sigmoid(mo_ref[...]) * (hf_ref[...] + hb_ref[...])
    branches = (att_ref[...], rw, s5, ml)
    merged = None
    for n in range(4):
        gate = _sigmoid(jnp.dot(xb, wg_ref[n], preferred_element_type=f32) + bg_ref[n:n + 1, :])
        wide = _bdot(branches[n], wba_ref[...] if n == 0 else wb_ref[n - 1])
        merged = gate * wide if merged is None else merged + gate * wide
    y = ALPHA * x + _bdot(merged, wo_ref[...])
    o_ref[...] = _layer_norm(y, ln_ref[0:1, :], ln_ref[1:2, :])


def _merge(xt, att, rw_parts, s5_parts, ml_parts, wg, bg, wba, wb, wo, ln, gn, s5_vec, s5_w, tm=256):
    T = xt.shape[0]
    row = lambda n: pl.BlockSpec((tm, n), lambda i: (i, 0))
    const = lambda shape: pl.BlockSpec(shape, lambda i: (0,) * len(shape), pipeline_mode=pl.Buffered(1))
    return pl.pallas_call(
        _merge_body, grid=(T // tm,),
        in_specs=[row(D_MODEL), row(512)] + [row(BRANCH_W)] * 9
        + [const((4, D_MODEL, D_MODEL)), const((4, D_MODEL)), const((512, D_MODEL)), const((3, BRANCH_W, D_MODEL)),
           const((D_MODEL, D_MODEL)), const((2, D_MODEL)), const((2, BRANCH_W)), const((2, BRANCH_W)),
           const((BRANCH_W, BRANCH_W))],
        out_specs=row(D_MODEL), out_shape=jax.ShapeDtypeStruct((T, D_MODEL), f32),
        compiler_params=_cparams(("parallel",)), name="merge")(
            xt, att, *rw_parts, *s5_parts, *ml_parts, wg, bg, wba, wb, wo, ln, gn, s5_vec, s5_w)


def _att_branch_weight(wb):
    z = jnp.zeros((HEAD_DIM, D_MODEL), f32)
    parts = []
    for h in range(ATT_HEADS):
        wh = wb[64 * h:64 * h + 64]
        parts += [wh, z] if h // 2 == 0 else [z, wh]
    return jnp.concatenate(parts, axis=0)


def _ffn_body(x_ref, w1_ref, w3_ref, w2_ref, ln_ref, o_ref):
    x = x_ref[...]
    xb = x.astype(bf16)
    h1 = jnp.dot(xb, w1_ref[...], preferred_element_type=f32)
    h3 = jnp.dot(xb, w3_ref[...], preferred_element_type=f32)
    ff = _bdot(h1 * _sigmoid(h1) * h3, w2_ref[...])
    o_ref[...] = _layer_norm(ALPHA * x + ff, ln_ref[0:1, :], ln_ref[1:2, :])


def _ffn(xt, w1, w3, w2, ln, tm=512):
    T = xt.shape[0]
    row = pl.BlockSpec((tm, D_MODEL), lambda i: (i, 0))
    const = lambda a: pl.BlockSpec(a.shape, lambda i: (0,) * a.ndim, pipeline_mode=pl.Buffered(1))
    return pl.pallas_call(
        _ffn_body, grid=(T // tm,), in_specs=[row, const(w1), const(w3), const(w2), const(ln)],
        out_specs=row, out_shape=jax.ShapeDtypeStruct((T, D_MODEL), f32),
        compiler_params=_cparams(("parallel",)), name="ffn")(xt, w1, w3, w2, ln)


MOE_TILE = 1024
SC_WINDOW = 128
SC_WORDS = 256


def _pack_words(x):
    bits = lax.bitcast_convert_type(x.astype(bf16).astype(f32), jnp.int32)
    half = D_MODEL // 2
    w = lax.shift_right_logical(bits[:, 0:half], 16) | bits[:, half:D_MODEL]
    return w[:, 0:SC_WORDS], w[:, SC_WORDS:2 * SC_WORDS]


def _unpack_words(wa, wb):
    w = jnp.concatenate([wa, wb], axis=1)
    lo = lax.bitcast_convert_type(lax.shift_left(w, 16), f32)
    hi = lax.bitcast_convert_type(w & jnp.int32(-65536), f32)
    return jnp.concatenate([lo, hi], axis=1)


def _router_body(x_ref, rt_ref, xa_ref, xb_ref, meta_ref, cnt_ref, run_ref):
    tb = x_ref.shape[0]

    @pl.when(pl.program_id(0) == 0)
    def _():
        run_ref[...] = jnp.zeros(run_ref.shape, f32)

    x = x_ref[...]
    xa_ref[...], xb_ref[...] = _pack_words(x)
    logits = _sdot3(x, rt_ref[...])
    lane = lax.broadcasted_iota(jnp.int32, logits.shape, 1)
    lg = jnp.where(lane < N_EXPERTS, logits, NEG)
    v1 = jnp.max(lg, axis=1, keepdims=True)
    i1 = jnp.min(jnp.where(lg == v1, lane, LANES), axis=1, keepdims=True)
    lg2 = jnp.where(lane == i1, NEG, lg)
    v2 = jnp.max(lg2, axis=1, keepdims=True)
    i2 = jnp.min(jnp.where(lg2 == v2, lane, LANES), axis=1, keepdims=True)
    e2 = jnp.exp(v2 - v1)
    sel1, sel2 = lane == i1, lane == i2
    mask = (sel1 | sel2).astype(f32)
    r = lax.broadcasted_iota(jnp.int32, (tb, tb), 0)
    c = lax.broadcasted_iota(jnp.int32, (tb, tb), 1)
    rank = _bdot((c < r).astype(f32), mask) + run_ref[0:1, :]
    run_ref[...] = run_ref[...] + jnp.sum(mask, axis=0, keepdims=True)
    rank1 = jnp.sum(jnp.where(sel1, rank, 0.0), axis=1, keepdims=True)
    rank2 = jnp.sum(jnp.where(sel2, rank, 0.0), axis=1, keepdims=True)
    cols = (i1.astype(f32), i2.astype(f32), rank1, rank2, 1.0 / (1.0 + e2), e2 / (1.0 + e2))
    meta = jnp.zeros(logits.shape, f32)
    for n, col in enumerate(cols):
        meta = jnp.where(lane == n, col, meta)
    meta_ref[...] = meta
    cnt_ref[...] = run_ref[...]


def _router(xt, router, tb=1024):
    T = xt.shape[0]
    tb = min(tb, T)
    return pl.pallas_call(
        _router_body, grid=(T // tb,),
        in_specs=[pl.BlockSpec((tb, D_MODEL), lambda i: (i, 0)), pl.BlockSpec((D_MODEL, LANES), lambda i: (0, 0))],
        out_specs=[pl.BlockSpec((tb, SC_WORDS), lambda i: (i, 0)), pl.BlockSpec((tb, SC_WORDS), lambda i: (i, 0)),
                   pl.BlockSpec((tb, LANES), lambda i: (i, 0)), pl.BlockSpec((8, LANES), lambda i: (0, 0))],
        out_shape=[jax.ShapeDtypeStruct((T, SC_WORDS), jnp.int32), jax.ShapeDtypeStruct((T, SC_WORDS), jnp.int32),
                   jax.ShapeDtypeStruct((T, LANES), f32), jax.ShapeDtypeStruct((8, LANES), f32)],
        scratch_shapes=[pltpu.VMEM((8, LANES), f32)],
        compiler_params=_cparams(("arbitrary",)), name="moe_router")(xt, router)


def _sc_gather(table, idx):
    n = idx.shape[0]
    mesh = plsc.VectorSubcoreMesh(core_axis_name="c", subcore_axis_name="s")

    @functools.partial(pl.kernel, out_type=jax.ShapeDtypeStruct((n, SC_WORDS), table.dtype), mesh=mesh)
    def gather(x_hbm, i_hbm, o_hbm):
        def body(i_vmem, o_vmem):
            pltpu.sync_copy(x_hbm.at[i_vmem.at[0]], o_vmem)

        pltpu.emit_pipeline(
            body, grid=(n // SC_WINDOW,),
            in_specs=[pl.BlockSpec((1, SC_WINDOW), index_map=lambda i: (0, i))],
            out_specs=[pl.BlockSpec((SC_WINDOW, SC_WORDS), index_map=lambda i: (i, 0))],
            core_axis_name=("c", "s"), dimension_semantics=(pltpu.PARALLEL,))(i_hbm, o_hbm)

    return gather(table, idx.reshape(1, n))


def _sc_scatter(rows, idx, n_out):
    R = rows.shape[0]
    n = idx.shape[0]
    nblk = R // SC_WINDOW
    mesh = plsc.VectorSubcoreMesh(core_axis_name="c", subcore_axis_name="s")

    @functools.partial(pl.kernel, out_type=jax.ShapeDtypeStruct((n_out, SC_WORDS), rows.dtype), mesh=mesh,
                       scratch_types=[])
    def scatter(x_hbm, i_hbm, o_hbm):
        def body(x_vmem, i_vmem):
            pltpu.sync_copy(x_vmem, o_hbm.at[i_vmem.at[0]])

        pltpu.emit_pipeline(
            body, grid=(n // SC_WINDOW,),
            in_specs=[pl.BlockSpec((SC_WINDOW, SC_WORDS), index_map=lambda i: (i % nblk, 0)),
                      pl.BlockSpec((1, SC_WINDOW), index_map=lambda i: (0, i))],
            out_specs=[], core_axis_name=("c", "s"), dimension_semantics=(pltpu.PARALLEL,))(x_hbm, i_hbm)

    return scatter(rows, idx.reshape(1, n))


def _experts_body(te_ref, rows_ref, xa_ref, xb_ref, w1_ref, w3_ref, w2_ref, oa_ref, ob_ref, acc_ref, x_ref):
    i = pl.program_id(0)
    j = pl.program_id(1)

    @pl.when(j == 0)
    def _():
        valid = lax.broadcasted_iota(jnp.int32, (MOE_TILE, 1), 0) < rows_ref[i]
        x_ref[...] = jnp.where(valid, _unpack_words(xa_ref[...], xb_ref[...]), 0.0).astype(bf16)
        acc_ref[...] = jnp.zeros(acc_ref.shape, f32)

    @pl.when(rows_ref[i] > 0)
    def _():
        x = x_ref[...]
        h1 = jnp.dot(x, w1_ref[0].astype(bf16), preferred_element_type=f32)
        h3 = jnp.dot(x, w3_ref[0].astype(bf16), preferred_element_type=f32)
        acc_ref[...] += _bdot(h1 * _sigmoid(h1) * h3, w2_ref[0])

    @pl.when(j == pl.num_programs(1) - 1)
    def _():
        oa_ref[...], ob_ref[...] = _pack_words(acc_ref[...])


def _experts(xa, xb, tile_expert, tile_rows, w1, w3, w2, tf=512):
    P = xa.shape[0]
    dff = w1.shape[2]
    words = pl.BlockSpec((MOE_TILE, SC_WORDS), lambda i, j, te, nt: (i, 0))
    grid_spec = pltpu.PrefetchScalarGridSpec(
        num_scalar_prefetch=2, grid=(P // MOE_TILE, dff // tf),
        in_specs=[words, words,
                  pl.BlockSpec((1, D_MODEL, tf), lambda i, j, te, nt: (te[i], 0, j)),
                  pl.BlockSpec((1, D_MODEL, tf), lambda i, j, te, nt: (te[i], 0, j)),
                  pl.BlockSpec((1, tf, D_MODEL), lambda i, j, te, nt: (te[i], j, 0))],
        out_specs=[words, words],
        scratch_shapes=[pltpu.VMEM((MOE_TILE, D_MODEL), f32), pltpu.VMEM((MOE_TILE, D_MODEL), bf16)])
    return pl.pallas_call(
        _experts_body, grid_spec=grid_spec, out_shape=[jax.ShapeDtypeStruct((P, SC_WORDS), jnp.int32)] * 2,
        compiler_params=_cparams(("parallel", "arbitrary")), name="moe_experts")(tile_expert, tile_rows, xa, xb, w1, w3, w2)


def _combine_body(x_ref, y0a_ref, y0b_ref, y1a_ref, y1b_ref, meta_ref, ln_ref, o_ref):
    meta = meta_ref[...]
    ff = (meta[:, 4:5] * _unpack_words(y0a_ref[...], y0b_ref[...])
          + meta[:, 5:6] * _unpack_words(y1a_ref[...], y1b_ref[...]))
    o_ref[...] = _layer_norm(ALPHA * x_ref[...] + ff, ln_ref[0:1, :], ln_ref[1:2, :])


def _combine(xt, yga, ygb, meta, ln, tm=1024):
    T = xt.shape[0]
    tm = min(tm, T)
    nb = T // tm
    row = pl.BlockSpec((tm, D_MODEL), lambda i: (i, 0))
    first = pl.BlockSpec((tm, SC_WORDS), lambda i: (i, 0))
    second = pl.BlockSpec((tm, SC_WORDS), lambda i: (nb + i, 0))
    return pl.pallas_call(
        _combine_body, grid=(nb,),
        in_specs=[row, first, first, second, second, pl.BlockSpec((tm, LANES), lambda i: (i, 0)),
                  pl.BlockSpec((2, D_MODEL), lambda i: (0, 0))],
        out_specs=row, out_shape=jax.ShapeDtypeStruct((T, D_MODEL), f32),
        compiler_params=_cparams(("parallel",)), name="moe_combine")(xt, yga, ygb, yga, ygb, meta, ln)


def _moe(xt, router, w1, w3, w2, ln):
    T = xt.shape[0]
    xa, xb, meta, cnt = _router(xt, router)
    counts = cnt[0, :N_EXPERTS].astype(jnp.int32)
    tiles = (counts + MOE_TILE - 1) // MOE_TILE
    tile_end = jnp.cumsum(tiles)
    offset = (tile_end - tiles) * MOE_TILE
    expert = meta[:, 0:2].astype(jnp.int32)
    onehot = expert[:, :, None] == jnp.arange(N_EXPERTS, dtype=jnp.int32)[None, None, :]
    pos = jnp.sum(jnp.where(onehot, offset[None, None, :], 0), axis=2) + meta[:, 2:4].astype(jnp.int32)
    pos = pos.T.reshape(-1)
    P = 2 * T + N_EXPERTS * MOE_TILE
    tile_id = jnp.arange(P // MOE_TILE, dtype=jnp.int32)
    tile_expert = jnp.minimum(jnp.sum((tile_id[:, None] >= tile_end[None, :]).astype(jnp.int32), axis=1), N_EXPERTS - 1)
    first_tile = (tile_end - tiles)[tile_expert]
    tile_rows = jnp.clip(counts[tile_expert] - (tile_id - first_tile) * MOE_TILE, 0, MOE_TILE)
    ya, yb = _experts(_sc_scatter(xa, pos, P), _sc_scatter(xb, pos, P), tile_expert, tile_rows, w1, w3, w2)
    return _combine(xt, _sc_gather(ya, pos), _sc_gather(yb, pos), meta, ln)


def kernel(x, w_in, b_in, att_gq, att_gk, rw_mix, rw_w0, rw_w2, rw_a0, rw_a2, rw_g2, rw_kk, rw_ka, rw_rk, rw_ln_g, rw_ln_b, s5_lam_re, s5_lam_im, s5_log_dt, s5_b_re, s5_b_im, s5_c_re, s5_c_im, s5_d, s5_glu_w, s5_glu_b, ml_conv_w, ml_conv_b, ml_ib, ml_fb, w_gate, b_gate, w_branch, w_out, ln1_g, ln1_b, ffn_w1, ffn_w3, ffn_w2, moe_router, moe_w1, moe_w3, moe_w2, ln2_g, ln2_b):
    B, S, D = x.shape
    assert 2 * B == 8 and D == D_MODEL and S % 512 == 0 and w_in.shape[0] == DEPTH, (x.shape, w_in.shape)
    xt = x.reshape(B * S, D)
    cos, sin = _rope_tables(S)
    for l in range(DEPTH):
        gain = jnp.concatenate([jnp.tile(att_gq[l], 8) * (HEAD_DIM ** -0.5), jnp.tile(att_gk[l], 2)])[None, :]
        (q, k, v, s5u, mq, mk, mv, mo, r, rk, rv, an, bn, lw, gate, bonus, gt) = _proj(
            xt, _proj_params(w_in[l], b_in[l], ml_ib[l], ml_fb[l]), (cos, sin, gain),
            (ml_conv_w[l], ml_conv_b[l][None, :]),
            _rw_params(rw_mix[l], rw_w0[l], rw_w2[l], rw_a0[l], rw_a2[l], rw_g2[l], rw_kk[l], rw_ka[l], rw_rk[l]), B, S)
        score_bound = 8.1 * jnp.max(jnp.abs(att_gq[l])) * jnp.max(jnp.abs(att_gk[l]))
        o_att = _flash(q, k, v, score_bound, B, S)
        yf, yb = _rwkv(r, rk, rv, an, bn, lw, B, S)
        y_s5 = _s5(s5u, _s5_params(s5_lam_re[l], s5_lam_im[l], s5_log_dt[l], s5_b_re[l], s5_b_im[l], s5_c_re[l],
                                   s5_c_im[l]), B, S)
        hf, hb = _mlstm(mq, mk, mv, gt, B, S)
        xt = _merge(xt, o_att, (yf, yb, gate, bonus), (y_s5, s5u), (hf, hb, mo), w_gate[l].astype(bf16), b_gate[l],
                    _att_branch_weight(w_branch[l, 0]).astype(bf16), w_branch[l, 1:].astype(bf16),
                    w_out[l].astype(bf16), jnp.stack([ln1_g[l], ln1_b[l]]), jnp.stack([rw_ln_g[l], rw_ln_b[l]]),
                    jnp.stack([s5_d[l], s5_glu_b[l]]), s5_glu_w[l].astype(bf16))
        ln2 = jnp.stack([ln2_g[l], ln2_b[l]])
        if l % 2 == 0:
            xt = _ffn(xt, ffn_w1[l // 2].astype(bf16), ffn_w3[l // 2].astype(bf16), ffn_w2[l // 2].astype(bf16), ln2)
        else:
            router = jnp.pad(moe_router[l // 2], ((0, 0), (0, LANES - N_EXPERTS)))
            xt = _moe(xt, router, moe_w1[l // 2], moe_w3[l // 2], moe_w2[l // 2], ln2)
    return xt.reshape(B, S, D)
```

```python
import functools
import math

import jax
import jax.numpy as jnp
import numpy as np
from jax import lax
from jax.experimental import pallas as pl
from jax.experimental.pallas import tpu as pltpu
from jax.experimental.pallas import tpu_sc as plsc

f32 = jnp.float32
bf16 = jnp.bfloat16

D_MODEL = 1024
DEPTH = 2
GRID_W = 64
BRANCH_W = 256
HEAD_DIM = 64
ATT_HEADS = 4
ATT_KV_HEADS = 2
ROPE_THETA = 10000.0
QK_EPS = 1e-6
RW_GN_EPS = 64e-5
RW_COLS = 1088
S5_GROUP = 16
S5_GROUPS = 16
S5_STATE = 64
ML_HEADS = 4
N_EXPERTS = 8
ALPHA = (2 * DEPTH) ** 0.25
LN_EPS = 1e-5

LANES = 128
CHUNK = 64
NEG = -1e30
VMEM_LIMIT = 56 * 1024 * 1024

PROJ_SPLITS = (768, 256, 512, 256, 256, RW_COLS)


def _cparams(sem):
    return pltpu.CompilerParams(dimension_semantics=sem, vmem_limit_bytes=VMEM_LIMIT)


def _sigmoid(x):
    return 1.0 / (1.0 + jnp.exp(-x))


def _softplus(x):
    return jnp.maximum(x, 0.0) + jnp.log(1.0 + jnp.exp(-jnp.abs(x)))


def _dims(a, lhs_c, rhs_c):
    lead = a.ndim - 2
    batch = tuple(range(lead))
    return (((lhs_c + lead,), (rhs_c + lead,)), (batch, batch))


def _bdot(a, b):
    return lax.dot_general(a.astype(bf16), b.astype(bf16), _dims(a, 1, 0), preferred_element_type=f32)


def _bdot_nt(a, b):
    return lax.dot_general(a.astype(bf16), b.astype(bf16), _dims(a, 1, 1), preferred_element_type=f32)


def _bdot_tn(a, b):
    return lax.dot_general(a.astype(bf16), b.astype(bf16), _dims(a, 0, 0), preferred_element_type=f32)


def _split(x):
    hi = x.astype(bf16)
    return hi, (x - hi.astype(f32)).astype(bf16)


def _sdot(a, b, exact):
    dims = _dims(a, 1, 0)
    if exact == "rhs":
        hi, lo = _split(a)
        bb = b.astype(bf16)
        return (lax.dot_general(hi, bb, dims, preferred_element_type=f32)
                + lax.dot_general(lo, bb, dims, preferred_element_type=f32))
    hi, lo = _split(b)
    ab = a.astype(bf16)
    return (lax.dot_general(ab, hi, dims, preferred_element_type=f32)
            + lax.dot_general(ab, lo, dims, preferred_element_type=f32))


def _sdot3(a, b):
    dims = _dims(a, 1, 0)
    ah, al = _split(a)
    bh, bl = _split(b)
    return (lax.dot_general(ah, bh, dims, preferred_element_type=f32)
            + lax.dot_general(ah, bl, dims, preferred_element_type=f32)
            + lax.dot_general(al, bh, dims, preferred_element_type=f32))


def _seg_matrix(n, seg=HEAD_DIM):
    r = lax.broadcasted_iota(jnp.int32, (n, n), 0) // seg
    c = lax.broadcasted_iota(jnp.int32, (n, n), 1) // seg
    return (r == c).astype(f32)


def _layer_norm(y, g, b):
    mu = jnp.mean(y, axis=-1, keepdims=True)
    d = y - mu
    var = jnp.mean(d * d, axis=-1, keepdims=True)
    return d * lax.rsqrt(var + LN_EPS) * g + b


def _row_to_col(row, eye):
    n = eye.shape[-1]
    return jnp.sum(jnp.where(eye, jnp.broadcast_to(row, (row.shape[0], n, n)), 0.0), axis=2, keepdims=True)


def _stack_heads(x):
    h0 = lax.broadcasted_iota(jnp.int32, x.shape, 2) < HEAD_DIM
    return jnp.concatenate([jnp.where(h0, x, 0.0), jnp.where(h0, 0.0, x)], axis=1)


def _proj_body(x_ref, xp_ref, xn_ref, cos_ref, sin_ref, w_ref, b_ref, wg_ref, bg_ref, gain_ref, cw_ref, cb_ref,
               mix_ref, w2_ref, a2_ref, g2_ref, vec_ref,
               q_ref, k_ref, v_ref, s5_ref, mq_ref, mk_ref, mv_ref, mo_ref, r_ref, rk_ref, rv_ref, an_ref, bn_ref,
               lw_ref, gate_ref, bonus_ref, g_ref, mbuf_ref, rbuf_ref):
    tm = x_ref.shape[0]
    xb = x_ref[...].astype(bf16)
    xh = jnp.concatenate([xp_ref[...], x_ref[...], xn_ref[...]], axis=0).astype(bf16)
    offs = np.cumsum((0,) + PROJ_SPLITS)

    def cols(rows, n):
        sl = slice(int(offs[n]), int(offs[n + 1]))
        return jnp.dot(rows, w_ref[:, sl], preferred_element_type=f32) + b_ref[:, sl]

    q_ref[...], k_ref[...], v_ref[...] = _att_prep(cols(xb, 0), cos_ref[...], sin_ref[...], gain_ref[...])
    s5_ref[...] = cols(xb, 1)
    _fill_halo(mbuf_ref, cols(xh, 2))
    mq_ref[...], mk_ref[...] = _ml_prep(mbuf_ref, tm, cw_ref[...], cb_ref[...])
    mv_ref[...] = cols(xb, 3)
    mo_ref[...] = cols(xb, 4)
    _fill_halo(rbuf_ref, cols(xh, 5))
    outs = _rw_prep(rbuf_ref, tm, mix_ref[...], w2_ref[...], a2_ref[...], g2_ref[...], vec_ref[...])
    for o_ref, val in zip((r_ref, rk_ref, rv_ref, an_ref, bn_ref), outs[0:5]):
        o_ref[...] = val
    lw_ref[0], lw_ref[1], gate_ref[...], bonus_ref[...] = outs[5:9]
    g_ref[...] = lax.dot_general(wg_ref[...], xb, (((1,), (1,)), ((), ())), preferred_element_type=f32) + bg_ref[...]


def _proj(xt, proj_prm, att_prm, ml_prm, rw_prm, B, S, tm=512):
    T = B * S
    nb = S // tm
    n_tot = sum(PROJ_SPLITS)
    row = lambda n: pl.BlockSpec((tm, n), lambda b, i: (b * nb + i, 0))
    const = lambda a: pl.BlockSpec(a.shape, lambda b, i: (0,) * a.ndim)
    tab = pl.BlockSpec((tm, LANES), lambda b, i: (i, 0))
    f32out = lambda n: jax.ShapeDtypeStruct((T, n), f32)
    consts = list(proj_prm) + [att_prm[2]] + list(ml_prm) + list(rw_prm)
    out_specs = ([row(512), row(LANES), row(2 * LANES)] + [row(BRANCH_W)] * 10
                 + [pl.BlockSpec((2, tm, BRANCH_W), lambda b, i: (0, b * nb + i, 0)), row(BRANCH_W), row(BRANCH_W),
                    pl.BlockSpec((16, tm), lambda b, i: (0, b * nb + i))])
    out_shape = ([jax.ShapeDtypeStruct((T, 512), bf16), jax.ShapeDtypeStruct((T, LANES), bf16),
                  jax.ShapeDtypeStruct((T, 2 * LANES), bf16)] + [f32out(BRANCH_W)] * 10
                 + [jax.ShapeDtypeStruct((2, T, BRANCH_W), f32), f32out(BRANCH_W), f32out(BRANCH_W),
                    jax.ShapeDtypeStruct((16, T), f32)])
    return pl.pallas_call(
        _proj_body, grid=(B, nb),
        in_specs=_halo_specs(D_MODEL, tm, B, S) + [tab, tab] + [const(a) for a in consts],
        out_specs=out_specs, out_shape=out_shape,
        scratch_shapes=[pltpu.VMEM((tm + 16, 512), f32), pltpu.VMEM((tm + 16, RW_COLS), f32)],
        compiler_params=_cparams(("parallel", "parallel")), name="proj")(
            xt, xt, xt, att_prm[0], att_prm[1], *consts)


def _proj_params(w_in, b_in, ml_ib, ml_fb):
    o = np.cumsum((0, 256, 128, 128, RW_COLS, 256, 512, 256, 8, 8, 256))
    sl = lambda i: (w_in[:, o[i]:o[i + 1]], b_in[o[i]:o[i + 1]])
    (wq, bq), (wk, bk), (wv, bv), (wrw, brw), (ws5, bs5), (wqk, bqk), (wmv, bmv), (wi, bi), (wf, bf), (wo, bo) = (
        sl(i) for i in range(10))
    zw, zb = jnp.zeros((D_MODEL, HEAD_DIM), f32), jnp.zeros((HEAD_DIM,), f32)
    wq_e, bq_e = [], []
    for h in range(ATT_HEADS):
        wh, bh = wq[:, 64 * h:64 * h + 64], bq[64 * h:64 * h + 64]
        wq_e += [wh, zw] if h // 2 == 0 else [zw, wh]
        bq_e += [bh, zb] if h // 2 == 0 else [zb, bh]
    w = jnp.concatenate(wq_e + [wk, wv, ws5, wqk, wmv, wo, wrw], axis=1)
    b = jnp.concatenate(bq_e + [bk, bv, bs5, bqk, bmv, bo, brw])
    wg = jnp.concatenate([wi, wf], axis=1).T
    bg = jnp.concatenate([bi + ml_ib.reshape(-1), bf + ml_fb.reshape(-1)])
    return w.astype(bf16), b[None, :], wg.astype(bf16), bg[:, None]


def _att_prep(att, cos, sin, gain):
    x = att[:, 0:640]
    ms = _sdot(x * x, _seg_matrix(640), "rhs") * (1.0 / HEAD_DIM)
    xn = x * lax.rsqrt(ms + QK_EPS) * gain
    lane = lax.broadcasted_iota(jnp.int32, xn.shape, 1)
    partner = jnp.where((lane % 32) < 16, pltpu.roll(xn, 640 - 16, 1), pltpu.roll(xn, 16, 1))
    rot = xn * jnp.concatenate([cos] * 5, axis=1) + partner * jnp.concatenate([sin] * 5, axis=1)
    v = jnp.concatenate([att[:, 640:768].astype(bf16), jnp.ones((x.shape[0], LANES), bf16)], axis=1)
    return rot[:, 0:512].astype(bf16), rot[:, 512:640].astype(bf16), v


def _rope_tables(S):
    t = np.arange(S)
    row = (t // GRID_W).astype(np.float32)
    col = (t % GRID_W).astype(np.float32)
    n = 16
    inv = np.power(np.float32(ROPE_THETA), -np.arange(n, dtype=np.float32) / n).astype(np.float32)
    ar = jnp.asarray(row)[:, None] * jnp.asarray(inv)
    ac = jnp.asarray(col)[:, None] * jnp.asarray(inv)
    cos = jnp.concatenate([jnp.cos(ar), jnp.cos(ar), jnp.cos(ac), jnp.cos(ac)], axis=1)
    sin = jnp.concatenate([-jnp.sin(ar), jnp.sin(ar), -jnp.sin(ac), jnp.sin(ac)], axis=1)
    return jnp.concatenate([cos, cos], axis=1), jnp.concatenate([sin, sin], axis=1)


def _flash_body(q_ref, k_ref, v_ref, o_ref, acc_ref, *m_scratch, tk, track_max):
    tq = q_ref.shape[0]
    nk = k_ref.shape[0] // tk
    q2 = jnp.concatenate([q_ref[:, 0:LANES], q_ref[:, LANES:2 * LANES]], axis=0)
    acc_ref[...] = jnp.zeros(acc_ref.shape, f32)
    if track_max:
        m_ref, = m_scratch
        m_ref[...] = jnp.full(m_ref.shape, NEG, f32)

    def step(j, carry):
        rows = pl.ds(pl.multiple_of(j * tk, tk), tk)
        s = lax.dot_general(q2, k_ref[rows, :], (((1,), (1,)), ((), ())), preferred_element_type=f32)
        if track_max:
            m_old = m_ref[...]
            m_new = jnp.maximum(m_old, jnp.max(s, axis=1, keepdims=True))
            p = jnp.exp(s - m_new).astype(bf16)
            acc_ref[...] = jnp.exp(m_old - m_new) * acc_ref[...] + jnp.dot(p, v_ref[rows, :], preferred_element_type=f32)
            m_ref[...] = m_new
        else:
            acc_ref[...] += jnp.dot(jnp.exp(s).astype(bf16), v_ref[rows, :], preferred_element_type=f32)
        return carry

    lax.fori_loop(0, nk, step, 0)
    o = acc_ref[:, 0:LANES] / acc_ref[:, LANES:2 * LANES]
    o_ref[...] = jnp.concatenate([o[0:tq], o[tq:2 * tq]], axis=1)


SCORE_BOUND_MAX = 60.0


def _flash(q, k, v, score_bound, B, S, tq=256, tk=8192):
    T = B * S
    nb = S // tq
    tk = min(tk, S)

    def call(track_max):
        scratch = [pltpu.VMEM((2 * tq, 2 * LANES), f32)] + ([pltpu.VMEM((2 * tq, 1), f32)] if track_max else [])
        return pl.pallas_call(
            functools.partial(_flash_body, tk=tk, track_max=track_max), grid=(B, ATT_KV_HEADS, nb),
            in_specs=[pl.BlockSpec((tq, 2 * LANES), lambda b, g, i: (b * nb + i, g)),
                      pl.BlockSpec((S, LANES), lambda b, g, i: (b, 0)),
                      pl.BlockSpec((S, 2 * LANES), lambda b, g, i: (b, 0))],
            out_specs=pl.BlockSpec((tq, 2 * LANES), lambda b, g, i: (b * nb + i, g)),
            out_shape=jax.ShapeDtypeStruct((T, 512), f32), scratch_shapes=scratch,
            compiler_params=_cparams(("parallel", "parallel", "parallel")),
            name="flash_safe" if track_max else "flash")(q, k, v)

    return lax.cond(score_bound <= SCORE_BOUND_MAX, lambda: call(False), lambda: call(True))


def _halo_specs(width, tm, B, S):
    nb = S // tm
    r8 = tm // 8
    last8 = B * S // 8 - 1

    def main(b, i):
        return (b * nb + i, 0)

    def prev(b, i):
        return (jnp.maximum(b * (S // 8) + i * r8 - 1, 0), 0)

    def nxt(b, i):
        return (jnp.minimum(b * (S // 8) + (i + 1) * r8, last8), 0)

    return [pl.BlockSpec((tm, width), main), pl.BlockSpec((8, width), prev), pl.BlockSpec((8, width), nxt)]


def _fill_halo(buf_ref, xh):
    tm = xh.shape[0] - 16
    i = pl.program_id(1)
    last = pl.num_programs(1) - 1
    buf_ref[...] = xh
    buf_ref[pl.ds(0, 8), :] = jnp.where(i > 0, xh[0:8], 0.0)
    buf_ref[pl.ds(8 + tm, 8), :] = jnp.where(i < last, xh[8 + tm:16 + tm], 0.0)


def _rw_prep(buf_ref, tm, mix, w2, a2, g2, vec):
    x = buf_ref[pl.ds(8, tm), :]
    p = x + mix[0:1, :] * (buf_ref[pl.ds(7, tm), :] - x) + mix[1:2, :] * (buf_ref[pl.ds(9, tm), :] - x)
    r, k, v = p[:, 0:256], p[:, 256:512], p[:, 512:768]
    w0f, w0b, a0, k_k, k_a, r_k = (vec[j:j + 1, :] for j in range(6))
    dec = _bdot(jnp.tanh(p[:, 768:896]), w2)
    z = p[:, 896:1088]
    a = _sigmoid(a0 + _bdot(z, a2))
    gate = _bdot(_sigmoid(z), g2)
    seg = _seg_matrix(BRANCH_W)
    kk = k * k_k
    kk = kk / jnp.maximum(jnp.sqrt(_sdot(kk * kk, seg, "rhs")), 1e-12)
    k2 = k * (1.0 + (a - 1.0) * k_a)
    bonus = _sdot(r * k2 * r_k, seg, "rhs") * v
    lwf = -jnp.exp(-_softplus(-(w0f + dec[:, 0:256])) - 0.5)
    lwb = -jnp.exp(-_softplus(-(w0b + dec[:, 256:512])) - 0.5)
    return r, k2, v, -kk, kk * a, lwf, lwb, gate, bonus


def _by_direction(x, fwd, bwd, fn):
    h = x.shape[0] // 2
    return jnp.concatenate([fn(x[0:h], fwd), fn(x[h:2 * h], bwd)], axis=0)


def _keep(x, masks):
    return _by_direction(x, masks[0], masks[1], lambda t, m: jnp.where(m, t, 0.0))


def _both(masks, nchain):
    h = nchain // 2
    return jnp.concatenate([jnp.broadcast_to(m.astype(f32), (h,) + m.shape[1:]) for m in masks], axis=0)


def _pair_masks():
    n = 2 * CHUNK
    r = lax.broadcasted_iota(jnp.int32, (1, n, n), 1)
    c = lax.broadcasted_iota(jnp.int32, (1, n, n), 2)
    same = (r // CHUNK) == (c // CHUNK)
    return r, c, same, (same & (c < r), same & (c > r)), (same & (c <= r), same & (c >= r))


def _rw_chunk(st, r, k, v, an, bn, lw):
    L = CHUNK
    N = r.shape[0]
    ri = lax.broadcasted_iota(jnp.int32, (1, L, L), 1)
    ci = lax.broadcasted_iota(jnp.int32, (1, L, L), 2)
    cs = _sdot(_both((ci <= ri, ci >= ri), N), lw, "lhs")
    tot = jnp.sum(lw, axis=1, keepdims=True)
    e_neg = jnp.exp(-cs)
    at = an * jnp.exp(cs - lw)
    rt = r * jnp.exp(cs)
    a2, b2, k2, v2 = _stack_heads(at), _stack_heads(bn * e_neg), _stack_heads(k * e_neg), _stack_heads(v)
    t = lax.broadcasted_iota(jnp.int32, (1, L, LANES), 1)
    tc = lax.broadcasted_iota(jnp.int32, (1, L, LANES), 2) % L
    strict, incl = (tc < t, tc > t), (tc <= t, tc >= t)
    g = _bdot_nt(jnp.concatenate([at, rt], axis=1), jnp.concatenate([b2, k2], axis=1))
    mab = _keep(g[:, 0:L, 0:LANES], strict)
    mak = _keep(g[:, 0:L, LANES:2 * LANES], strict)
    pb = _keep(g[:, L:2 * L, 0:LANES], incl)
    pk = _keep(g[:, L:2 * L, LANES:2 * LANES], incl)
    mul = lambda p, q: _bdot(p, _stack_heads(q))
    m8 = jnp.where((t // 8) == (tc // 8), mab, 0.0)
    x = (t == tc).astype(f32) + m8
    p = mul(m8, m8)
    x = x + mul(x, p)
    p = mul(p, p)
    x = x + mul(x, p)
    n = 8
    while n < L:
        e = jnp.where(((t // (2 * n)) == (tc // (2 * n))) & ((t // n) != (tc // n)), mab, 0.0)
        x = x + mul(mul(x, e), x)
        n *= 2
    wu = _bdot(x, jnp.concatenate([a2, _stack_heads(_bdot(mak, v2))], axis=2))
    wu2 = jnp.concatenate([_stack_heads(wu[:, :, 0:LANES]), _stack_heads(wu[:, :, LANES:2 * LANES])], axis=2)
    pwu = _bdot(pb, wu2)
    rh = rt + pwu[:, :, 0:LANES]
    y = pwu[:, :, LANES:2 * LANES] + _bdot(jnp.concatenate([pk, rh], axis=2), jnp.concatenate([v2, st], axis=1))
    r128 = lax.broadcasted_iota(jnp.int32, (1, LANES, LANES), 1)
    c128 = lax.broadcasted_iota(jnp.int32, (1, LANES, LANES), 2)
    gam = _row_to_col(jnp.exp(tot), r128 == c128)
    bwu = _bdot_tn(b2, wu2)
    st = gam * (st + _bdot(bwu[:, :, 0:LANES], st) + bwu[:, :, LANES:2 * LANES] + _bdot_tn(k2, v2))
    return y, st


def _load_pairs(ref, rows):
    return jnp.concatenate([ref[:, rows, 0:LANES], ref[:, rows, LANES:2 * LANES]], axis=0)


def _store_pairs(ref, rows, y):
    nb = ref.shape[0]
    ref[:, rows, 0:LANES] = y[0:nb]
    ref[:, rows, LANES:2 * LANES] = y[nb:2 * nb]


def _load_both(f_ref, b_ref, rows_f, rows_b):
    return jnp.concatenate([_load_pairs(f_ref, rows_f), _load_pairs(b_ref, rows_b)], axis=0)


def _chunk_rows(cc, nch):
    return (pl.ds(pl.multiple_of(cc * CHUNK, CHUNK), CHUNK),
            pl.ds(pl.multiple_of((nch - 1 - cc) * CHUNK, CHUNK), CHUNK))


def _rw_scan_body(rf_ref, rb_ref, kf_ref, kb_ref, vf_ref, vb_ref, anf_ref, anb_ref, bnf_ref, bnb_ref, lwf_ref, lwb_ref,
                  yf_ref, yb_ref, st_ref):
    nch = rf_ref.shape[1] // CHUNK
    half = st_ref.shape[0] // 2

    @pl.when(pl.program_id(0) == 0)
    def _():
        st_ref[...] = jnp.zeros(st_ref.shape, f32)

    def step(cc, carry):
        rows_f, rows_b = _chunk_rows(cc, nch)
        pairs = ((rf_ref, rb_ref), (kf_ref, kb_ref), (vf_ref, vb_ref), (anf_ref, anb_ref), (bnf_ref, bnb_ref),
                 (lwf_ref.at[0], lwb_ref.at[0]))
        y, st = _rw_chunk(st_ref[...], *(_load_both(f, b, rows_f, rows_b) for f, b in pairs))
        _store_pairs(yf_ref, rows_f, y[0:half])
        _store_pairs(yb_ref, rows_b, y[half:2 * half])
        st_ref[...] = st
        return carry

    lax.fori_loop(0, nch, step, 0)


def _rw_scan(r, k, v, an, bn, lw, B, S, ts=128):
    nb = S // ts
    fwd = pl.BlockSpec((B, ts, BRANCH_W), lambda i: (0, i, 0))
    bwd = pl.BlockSpec((B, ts, BRANCH_W), lambda i: (0, nb - 1 - i, 0))
    out = jax.ShapeDtypeStruct((B, S, BRANCH_W), f32)
    return pl.pallas_call(
        _rw_scan_body, grid=(nb,),
        in_specs=[fwd, bwd] * 5 + [pl.BlockSpec((1, B, ts, BRANCH_W), lambda i: (0, 0, i, 0)),
                                   pl.BlockSpec((1, B, ts, BRANCH_W), lambda i: (1, 0, nb - 1 - i, 0))],
        out_specs=[fwd, bwd], out_shape=[out, out],
        scratch_shapes=[pltpu.VMEM((2 * B * BRANCH_W // LANES, LANES, LANES), f32)],
        compiler_params=_cparams(("arbitrary",)), name="rw_scan")(r, r, k, k, v, v, an, an, bn, bn, lw, lw)


def _rw_finish(y, gate, bonus, gn):
    seg = _seg_matrix(BRANCH_W)
    mu = _sdot(y, seg, "rhs") * (1.0 / HEAD_DIM)
    d = y - mu
    var = _sdot(d * d, seg, "rhs") * (1.0 / HEAD_DIM)
    yn = d * lax.rsqrt(var + RW_GN_EPS) * gn[0:1, :] + gn[1:2, :]
    return (yn + bonus) * gate


def _rwkv(r, k, v, an, bn, lw, B, S):
    seq = [t.reshape(B, S, BRANCH_W) for t in (r, k, v, an, bn)] + [lw.reshape(2, B, S, BRANCH_W)]
    yf, yb = _rw_scan(*seq, B, S)
    return yf.reshape(B * S, BRANCH_W), yb.reshape(B * S, BRANCH_W)


def _rw_params(mix, w0, w2, a0, a2, g2, k_k, k_a, r_k):
    z = jnp.zeros((64, 256), f32)
    w2c = jnp.concatenate([jnp.concatenate([w2[0], z], axis=1), jnp.concatenate([z, w2[1]], axis=1)], axis=0)
    a2p = jnp.concatenate([a2, jnp.zeros((128, 256), f32)], axis=0)
    g2p = jnp.concatenate([jnp.zeros((64, 256), f32), g2], axis=0)
    vec = jnp.stack([w0[0], w0[1], a0, k_k, k_a, r_k.reshape(-1), jnp.zeros_like(a0), jnp.zeros_like(a0)])
    return mix, w2c.astype(bf16), a2p.astype(bf16), g2p.astype(bf16), vec


S5_HALF = S5_GROUPS * S5_STATE // 2


def _s5_scan_body(uf_ref, ub_ref, bh_ref, cre_ref, cim_ref, lre_ref, lim_ref, yf_ref, yb_ref,
                  sre_ref, sim_ref, bre_ref, bim_ref):
    ts = uf_ref.shape[0]
    n = S5_HALF

    @pl.when(pl.program_id(0) == 0)
    def _():
        sre_ref[...] = jnp.zeros(sre_ref.shape, f32)
        sim_ref[...] = jnp.zeros(sim_ref.shape, f32)

    row_half = lax.broadcasted_iota(jnp.int32, uf_ref.shape, 1) // 4
    lane_half = lax.broadcasted_iota(jnp.int32, uf_ref.shape, 2) // (BRANCH_W // 2)
    for d, u_ref in enumerate((uf_ref, ub_ref)):
        lhs = jnp.where(row_half == lane_half, u_ref[...], 0.0).reshape(ts * 8, BRANCH_W)
        bu = _bdot(lhs, bh_ref[d]).reshape(ts, 8, 2 * n)
        bre_ref[d] = bu[:, :, 0:n]
        bim_ref[d] = bu[:, :, n:2 * n]
    lre = lre_ref[...]
    lim = lim_ref[...]

    def step(t, carry):
        out = []
        for d, td in enumerate((t, ts - 1 - t)):
            sre, sim = carry[2 * d], carry[2 * d + 1]
            nre = lre[d] * sre - lim[d] * sim + bre_ref[d, td]
            nim = lre[d] * sim + lim[d] * sre + bim_ref[d, td]
            bre_ref[d, td] = nre
            bim_ref[d, td] = nim
            out += [nre, nim]
        return tuple(out)

    s = lax.fori_loop(0, ts, step, (sre_ref[0], sim_ref[0], sre_ref[1], sim_ref[1]), unroll=4)
    sre_ref[0], sim_ref[0], sre_ref[1], sim_ref[1] = s
    low = lane_half == 0
    for d, y_ref in enumerate((yf_ref, yb_ref)):
        yv = (_bdot(bre_ref[d].reshape(ts * 8, n), cre_ref[d])
              - _bdot(bim_ref[d].reshape(ts * 8, n), cim_ref[d])).reshape(ts, 8, BRANCH_W)
        y_ref[...] = jnp.where(low, yv, pltpu.roll(yv, 4, 1))


def _s5_scan(u8, bh, cre, cim, lre, lim, ts=128):
    S = u8.shape[0]
    nb = S // ts
    n = S5_HALF
    full = lambda shape: pl.BlockSpec(shape, lambda i: (0,) * len(shape))
    fwd = pl.BlockSpec((ts, 8, BRANCH_W), lambda i: (i, 0, 0))
    bwd = pl.BlockSpec((ts, 8, BRANCH_W), lambda i: (nb - 1 - i, 0, 0))
    out = jax.ShapeDtypeStruct((S, 8, BRANCH_W), f32)
    return pl.pallas_call(
        _s5_scan_body, grid=(nb,),
        in_specs=[fwd, bwd, full((2, BRANCH_W, 2 * n)), full((2, n, BRANCH_W)), full((2, n, BRANCH_W)),
                  full((2, 8, n)), full((2, 8, n))],
        out_specs=[fwd, bwd], out_shape=[out, out],
        scratch_shapes=[pltpu.VMEM((2, 8, n), f32), pltpu.VMEM((2, 8, n), f32),
                        pltpu.VMEM((2, ts, 8, n), f32), pltpu.VMEM((2, ts, 8, n), f32)],
        compiler_params=_cparams(("arbitrary",)), name="s5_scan")(u8, u8, bh, cre, cim, lre, lim)


def _s5_finish(y, u, vec, w):
    y = y + u * vec[0:1, :]
    y = 0.5 * y * (1.0 + jnp.tanh(math.sqrt(2.0 / math.pi) * (y + 0.044715 * (y * y * y))))
    return y * _sigmoid(_bdot(y, w) + vec[1:2, :])


def _s5_params(lam_re, lam_im, log_dt, b_re, b_im, c_re, c_im):
    G, P, C = S5_GROUPS, S5_STATE, S5_GROUP
    H = G // 2
    eye = jnp.eye(H, dtype=f32)
    b_c = lax.complex(b_re, b_im)

    def b_part(x):
        return jnp.einsum('ab,hapc->hacbp', eye, x.reshape(2, H, P, C)).reshape(G * C, H * P)

    def c_part(x):
        return jnp.einsum('ab,hacp->aphbc', eye, x.reshape(2, H, C, P)).reshape(H * P, G * C)

    def rows(x):
        return jnp.broadcast_to(x.reshape(2, 1, H * P), (2, 4, H * P)).reshape(8, H * P)

    bh, cre, cim, lre, lim = [], [], [], [], []
    for d in range(2):
        lam = lax.complex(jnp.minimum(lam_re[d], -1e-4), lam_im[d])
        lam_bar = jnp.exp(lam * jnp.exp(log_dt[d])[:, None])
        b_bar = ((lam_bar - 1.0) / lam)[..., None] * b_c
        bh.append(jnp.concatenate([b_part(jnp.real(b_bar)), b_part(jnp.imag(b_bar))], axis=1))
        cre.append(c_part(c_re[d]))
        cim.append(c_part(c_im[d]))
        lre.append(rows(jnp.real(lam_bar)))
        lim.append(rows(jnp.imag(lam_bar)))
    return (jnp.stack(bh).astype(bf16), jnp.stack(cre).astype(bf16), jnp.stack(cim).astype(bf16),
            jnp.stack(lre), jnp.stack(lim))


def _s5(u, prm, B, S):
    u3 = u.reshape(B, S, BRANCH_W).transpose(1, 0, 2)
    yf, yb = _s5_scan(jnp.concatenate([u3, u3], axis=1), *prm)
    return (yf[:, 0:B] + yb[:, 0:B]).transpose(1, 0, 2).reshape(B * S, BRANCH_W)


def _ml_prep(buf_ref, tm, w, b):
    y = b
    for j in range(5):
        y = y + w[j:j + 1, :] * buf_ref[pl.ds(6 + j, tm), :]
    y = y * _sigmoid(y)
    return y[:, 0:BRANCH_W], y[:, BRANCH_W:2 * BRANCH_W] * (HEAD_DIM ** -0.5)


def _ml_chunk(state, q, k, v, li, lfp):
    cn, m_row = state
    L = CHUNK
    N = q.shape[0]
    rr, cc, same, _, incl2 = _pair_masks()
    same_f = _both((same, same), N)
    lane = lax.broadcasted_iota(jnp.int32, (1, 1, 2 * L), 2)
    lf = jnp.minimum(lfp, 0.0) - jnp.log(1.0 + jnp.exp(-jnp.abs(lfp)))
    lf8 = jnp.broadcast_to(lf, (N, 8, 2 * L))
    b_row = _sdot(lf8, _both((incl2[1], incl2[0]), N), "rhs")[:, 0:1]
    g_row = _sdot(lf8, same_f, "rhs")[:, 0:1]
    w_end = g_row - b_row + li
    m0 = jnp.max(jnp.where(lane < L, w_end, NEG), axis=2, keepdims=True)
    m1 = jnp.max(jnp.where(lane < L, NEG, w_end), axis=2, keepdims=True)
    m_loc = jnp.where(lane < L, m0, m1)
    t = lax.broadcasted_iota(jnp.int32, (1, L, LANES), 1)
    tc = lax.broadcasted_iota(jnp.int32, (1, L, LANES), 2) % L
    diag = t == tc

    def cols(row):
        return _sdot(jnp.where(diag, jnp.broadcast_to(row, (N, L, LANES)), 0.0), same_f, "rhs")

    e_col, b_col = cols(jnp.exp(w_end - m_loc)), cols(b_row)
    k2, v2 = _stack_heads(k), _stack_heads(v)
    v1 = jnp.concatenate([v2, _stack_heads(jnp.ones_like(v))], axis=2)
    log_inter = b_col + m_row
    log_intra = _by_direction(b_col - b_row + li, tc <= t, tc >= t, lambda x, m: jnp.where(m, x, NEG))
    head0 = lane < L
    r0 = jnp.max(jnp.where(head0, log_intra, NEG), axis=2, keepdims=True)
    r1 = jnp.max(jnp.where(head0, NEG, log_intra), axis=2, keepdims=True)
    m_r = jnp.maximum(log_inter, jnp.where(head0, r0, r1))
    s = _bdot_nt(q, k2) * jnp.exp(log_intra - m_r)
    inter = jnp.exp(log_inter - m_r)
    nd = _bdot(s, v1) + jnp.concatenate([inter, inter], axis=2) * _bdot(q, cn)
    h = nd[:, :, 0:LANES] / jnp.maximum(jnp.abs(nd[:, :, LANES:2 * LANES]), jnp.exp(-m_r))
    m_new = jnp.maximum(g_row + m_row, m_loc)
    a = jnp.exp(g_row + m_row - m_new)
    bb = jnp.exp(m_loc - m_new)
    cn = (jnp.concatenate([a, a], axis=2) * cn
          + jnp.concatenate([bb, bb], axis=2) * _bdot_tn(_stack_heads(e_col * k), v1))
    return h, (cn, m_new)


GATE_ROWS = 8


def _ml_scan_body(qf_ref, qb_ref, kf_ref, kb_ref, vf_ref, vb_ref, gf_ref, gb_ref, hf_ref, hb_ref, c_ref, m_ref):
    nch = qf_ref.shape[1] // CHUNK
    half = c_ref.shape[0] // 2
    i = pl.program_id(0)
    nb = pl.num_programs(0)
    per = GATE_ROWS // nch
    base_f = (i % per) * nch
    base_b = ((nb - 1 - i) % per) * nch

    @pl.when(i == 0)
    def _():
        c_ref[...] = jnp.zeros(c_ref.shape, f32)
        m_ref[...] = jnp.zeros(m_ref.shape, f32)

    def step(cc, carry):
        rows_f, rows_b = _chunk_rows(cc, nch)
        gate = lambda t: jnp.concatenate(
            [gf_ref[t, 0, 0, :, pl.ds(base_f + cc, 1), :], gf_ref[t, 0, 1, :, pl.ds(base_f + cc, 1), :],
             gb_ref[t, 0, 0, :, pl.ds(base_b + nch - 1 - cc, 1), :], gb_ref[t, 0, 1, :, pl.ds(base_b + nch - 1 - cc, 1), :]],
            axis=0)
        h, (cn, m_row) = _ml_chunk((c_ref[...], m_ref[...]), _load_both(qf_ref, qb_ref, rows_f, rows_b),
                                   _load_both(kf_ref, kb_ref, rows_f, rows_b), _load_both(vf_ref, vb_ref, rows_f, rows_b),
                                   gate(0), gate(1))
        _store_pairs(hf_ref, rows_f, h[0:half])
        _store_pairs(hb_ref, rows_b, h[half:2 * half])
        c_ref[...] = cn
        m_ref[...] = m_row
        return carry

    lax.fori_loop(0, nch, step, 0)


def _ml_scan(q, k, v, g, B, S, ts=256):
    nb = S // ts
    per = GATE_ROWS * CHUNK // ts
    nchain = 2 * B * BRANCH_W // LANES
    fwd = pl.BlockSpec((B, ts, BRANCH_W), lambda i: (0, i, 0))
    bwd = pl.BlockSpec((B, ts, BRANCH_W), lambda i: (0, nb - 1 - i, 0))
    out = jax.ShapeDtypeStruct((B, S, BRANCH_W), f32)
    return pl.pallas_call(
        _ml_scan_body, grid=(nb,),
        in_specs=[fwd, bwd] * 3 + [pl.BlockSpec((2, 1, 2, B, GATE_ROWS, LANES), lambda i: (0, 0, 0, 0, i // per, 0)),
                                   pl.BlockSpec((2, 1, 2, B, GATE_ROWS, LANES),
                                                lambda i: (0, 1, 0, 0, (nb - 1 - i) // per, 0))],
        out_specs=[fwd, bwd], out_shape=[out, out],
        scratch_shapes=[pltpu.VMEM((nchain, LANES, 2 * LANES), f32), pltpu.VMEM((nchain, 1, LANES), f32)],
        compiler_params=_cparams(("arbitrary",)), name="ml_scan")(q, q, k, k, v, v, g, g)


def _mlstm(q, k, mv, gt, B, S):
    g = gt.reshape(2, 2, 2, 2, B, S // CHUNK, CHUNK).transpose(0, 1, 2, 4, 5, 3, 6).reshape(2, 2, 2, B, S // CHUNK, LANES)
    hf, hb = _ml_scan(*(t.reshape(B, S, BRANCH_W) for t in (q, k, mv)), g, B, S)
    return hf.reshape(B * S, BRANCH_W), hb.reshape(B * S, BRANCH_W)


def _merge_body(x_ref, att_ref, yf_ref, yb_ref, rg_ref, rb_ref, sy_ref, su_ref, hf_ref, hb_ref, mo_ref,
                wg_ref, bg_ref, wba_ref, wb_ref, wo_ref, ln_ref, gn_ref, sv_ref, sw_ref, o_ref):
    x = x_ref[...]
    xb = x.astype(bf16)
    rw = _rw_finish(yf_ref[...] + yb_ref[...], rg_ref[...], rb_ref[...], gn_ref[...])
    s5 = _s5_finish(sy_ref[...], su_ref[...], sv_ref[...], sw_ref[...])
    ml = _sigmoid(mo_ref[...]) * (hf_ref[...] + hb_ref[...])
    branches = (att_ref[...], rw, s5, ml)
    merged = None
    for n in range(4):
        gate = _sigmoid(jnp.dot(xb, wg_ref[n], preferred_element_type=f32) + bg_ref[n:n + 1, :])
        wide = _bdot(branches[n], wba_ref[...] if n == 0 else wb_ref[n - 1])
        merged = gate * wide if merged is None else merged + gate * wide
    y = ALPHA * x + _bdot(merged, wo_ref[...])
    o_ref[...] = _layer_norm(y, ln_ref[0:1, :], ln_ref[1:2, :])


def _merge(xt, att, rw_parts, s5_parts, ml_parts, wg, bg, wba, wb, wo, ln, gn, s5_vec, s5_w, tm=256):
    T = xt.shape[0]
    row = lambda n: pl.BlockSpec((tm, n), lambda i: (i, 0))
    const = lambda shape: pl.BlockSpec(shape, lambda i: (0,) * len(shape), pipeline_mode=pl.Buffered(1))
    return pl.pallas_call(
        _merge_body, grid=(T // tm,),
        in_specs=[row(D_MODEL), row(512)] + [row(BRANCH_W)] * 9
        + [const((4, D_MODEL, D_MODEL)), const((4, D_MODEL)), const((512, D_MODEL)), const((3, BRANCH_W, D_MODEL)),
           const((D_MODEL, D_MODEL)), const((2, D_MODEL)), const((2, BRANCH_W)), const((2, BRANCH_W)),
           const((BRANCH_W, BRANCH_W))],
        out_specs=row(D_MODEL), out_shape=jax.ShapeDtypeStruct((T, D_MODEL), f32),
        compiler_params=_cparams(("parallel",)), name="merge")(
            xt, att, *rw_parts, *s5_parts, *ml_parts, wg, bg, wba, wb, wo, ln, gn, s5_vec, s5_w)


def _att_branch_weight(wb):
    z = jnp.zeros((HEAD_DIM, D_MODEL), f32)
    parts = []
    for h in range(ATT_HEADS):
        wh = wb[64 * h:64 * h + 64]
        parts += [wh, z] if h // 2 == 0 else [z, wh]
    return jnp.concatenate(parts, axis=0)


def _ffn_body(x_ref, w1_ref, w3_ref, w2_ref, ln_ref, o_ref):
    x = x_ref[...]
    xb = x.astype(bf16)
    h1 = jnp.dot(xb, w1_ref[...], preferred_element_type=f32)
    h3 = jnp.dot(xb, w3_ref[...], preferred_element_type=f32)
    ff = _bdot(h1 * _sigmoid(h1) * h3, w2_ref[...])
    o_ref[...] = _layer_norm(ALPHA * x + ff, ln_ref[0:1, :], ln_ref[1:2, :])


def _ffn(xt, w1, w3, w2, ln, tm=512):
    T = xt.shape[0]
    row = pl.BlockSpec((tm, D_MODEL), lambda i: (i, 0))
    const = lambda a: pl.BlockSpec(a.shape, lambda i: (0,) * a.ndim, pipeline_mode=pl.Buffered(1))
    return pl.pallas_call(
        _ffn_body, grid=(T // tm,), in_specs=[row, const(w1), const(w3), const(w2), const(ln)],
        out_specs=row, out_shape=jax.ShapeDtypeStruct((T, D_MODEL), f32),
        compiler_params=_cparams(("parallel",)), name="ffn")(xt, w1, w3, w2, ln)


MOE_TILE = 1024
SC_WINDOW = 128
SC_WORDS = 256


def _pack_words(x):
    bits = lax.bitcast_convert_type(x.astype(bf16).astype(f32), jnp.int32)
    half = D_MODEL // 2
    w = lax.shift_right_logical(bits[:, 0:half], 16) | bits[:, half:D_MODEL]
    return w[:, 0:SC_WORDS], w[:, SC_WORDS:2 * SC_WORDS]


def _unpack_words(wa, wb):
    w = jnp.concatenate([wa, wb], axis=1)
    lo = lax.bitcast_convert_type(lax.shift_left(w, 16), f32)
    hi = lax.bitcast_convert_type(w & jnp.int32(-65536), f32)
    return jnp.concatenate([lo, hi], axis=1)


def _router_body(x_ref, rt_ref, xa_ref, xb_ref, meta_ref, cnt_ref, run_ref):
    tb = x_ref.shape[0]

    @pl.when(pl.program_id(0) == 0)
    def _():
        run_ref[...] = jnp.zeros(run_ref.shape, f32)

    x = x_ref[...]
    xa_ref[...], xb_ref[...] = _pack_words(x)
    logits = _sdot3(x, rt_ref[...])
    lane = lax.broadcasted_iota(jnp.int32, logits.shape, 1)
    lg = jnp.where(lane < N_EXPERTS, logits, NEG)
    v1 = jnp.max(lg, axis=1, keepdims=True)
    i1 = jnp.min(jnp.where(lg == v1, lane, LANES), axis=1, keepdims=True)
    lg2 = jnp.where(lane == i1, NEG, lg)
    v2 = jnp.max(lg2, axis=1, keepdims=True)
    i2 = jnp.min(jnp.where(lg2 == v2, lane, LANES), axis=1, keepdims=True)
    e2 = jnp.exp(v2 - v1)
    sel1, sel2 = lane == i1, lane == i2
    mask = (sel1 | sel2).astype(f32)
    r = lax.broadcasted_iota(jnp.int32, (tb, tb), 0)
    c = lax.broadcasted_iota(jnp.int32, (tb, tb), 1)
    rank = _bdot((c < r).astype(f32), mask) + run_ref[0:1, :]
    run_ref[...] = run_ref[...] + jnp.sum(mask, axis=0, keepdims=True)
    rank1 = jnp.sum(jnp.where(sel1, rank, 0.0), axis=1, keepdims=True)
    rank2 = jnp.sum(jnp.where(sel2, rank, 0.0), axis=1, keepdims=True)
    cols = (i1.astype(f32), i2.astype(f32), rank1, rank2, 1.0 / (1.0 + e2), e2 / (1.0 + e2))
    meta = jnp.zeros(logits.shape, f32)
    for n, col in enumerate(cols):
        meta = jnp.where(lane == n, col, meta)
    meta_ref[...] = meta
    cnt_ref[...] = run_ref[...]


def _router(xt, router, tb=1024):
    T = xt.shape[0]
    tb = min(tb, T)
    return pl.pallas_call(
        _router_body, grid=(T // tb,),
        in_specs=[pl.BlockSpec((tb, D_MODEL), lambda i: (i, 0)), pl.BlockSpec((D_MODEL, LANES), lambda i: (0, 0))],
        out_specs=[pl.BlockSpec((tb, SC_WORDS), lambda i: (i, 0)), pl.BlockSpec((tb, SC_WORDS), lambda i: (i, 0)),
                   pl.BlockSpec((tb, LANES), lambda i: (i, 0)), pl.BlockSpec((8, LANES), lambda i: (0, 0))],
        out_shape=[jax.ShapeDtypeStruct((T, SC_WORDS), jnp.int32), jax.ShapeDtypeStruct((T, SC_WORDS), jnp.int32),
                   jax.ShapeDtypeStruct((T, LANES), f32), jax.ShapeDtypeStruct((8, LANES), f32)],
        scratch_shapes=[pltpu.VMEM((8, LANES), f32)],
        compiler_params=_cparams(("arbitrary",)), name="moe_router")(xt, router)


def _sc_gather(table, idx):
    n = idx.shape[0]
    mesh = plsc.VectorSubcoreMesh(core_axis_name="c", subcore_axis_name="s")

    @functools.partial(pl.kernel, out_type=jax.ShapeDtypeStruct((n, SC_WORDS), table.dtype), mesh=mesh)
    def gather(x_hbm, i_hbm, o_hbm):
        def body(i_vmem, o_vmem):
            pltpu.sync_copy(x_hbm.at[i_vmem.at[0]], o_vmem)

        pltpu.emit_pipeline(
            body, grid=(n // SC_WINDOW,),
            in_specs=[pl.BlockSpec((1, SC_WINDOW), index_map=lambda i: (0, i))],
            out_specs=[pl.BlockSpec((SC_WINDOW, SC_WORDS), index_map=lambda i: (i, 0))],
            core_axis_name=("c", "s"), dimension_semantics=(pltpu.PARALLEL,))(i_hbm, o_hbm)

    return gather(table, idx.reshape(1, n))


def _sc_scatter(rows, idx, n_out):
    R = rows.shape[0]
    n = idx.shape[0]
    nblk = R // SC_WINDOW
    mesh = plsc.VectorSubcoreMesh(core_axis_name="c", subcore_axis_name="s")

    @functools.partial(pl.kernel, out_type=jax.ShapeDtypeStruct((n_out, SC_WORDS), rows.dtype), mesh=mesh,
                       scratch_types=[])
    def scatter(x_hbm, i_hbm, o_hbm):
        def body(x_vmem, i_vmem):
            pltpu.sync_copy(x_vmem, o_hbm.at[i_vmem.at[0]])

        pltpu.emit_pipeline(
            body, grid=(n // SC_WINDOW,),
            in_specs=[pl.BlockSpec((SC_WINDOW, SC_WORDS), index_map=lambda i: (i % nblk, 0)),
                      pl.BlockSpec((1, SC_WINDOW), index_map=lambda i: (0, i))],
            out_specs=[], core_axis_name=("c", "s"), dimension_semantics=(pltpu.PARALLEL,))(x_hbm, i_hbm)

    return scatter(rows, idx.reshape(1, n))


def _experts_body(te_ref, rows_ref, xa_ref, xb_ref, w1_ref, w3_ref, w2_ref, oa_ref, ob_ref, acc_ref, x_ref):
    i = pl.program_id(0)
    j = pl.program_id(1)

    @pl.when(j == 0)
    def _():
        valid = lax.broadcasted_iota(jnp.int32, (MOE_TILE, 1), 0) < rows_ref[i]
        x_ref[...] = jnp.where(valid, _unpack_words(xa_ref[...], xb_ref[...]), 0.0).astype(bf16)
        acc_ref[...] = jnp.zeros(acc_ref.shape, f32)

    @pl.when(rows_ref[i] > 0)
    def _():
        x = x_ref[...]
        h1 = jnp.dot(x, w1_ref[0].astype(bf16), preferred_element_type=f32)
        h3 = jnp.dot(x, w3_ref[0].astype(bf16), preferred_element_type=f32)
        acc_ref[...] += _bdot(h1 * _sigmoid(h1) * h3, w2_ref[0])

    @pl.when(j == pl.num_programs(1) - 1)
    def _():
        oa_ref[...], ob_ref[...] = _pack_words(acc_ref[...])


def _experts(xa, xb, tile_expert, tile_rows, w1, w3, w2, tf=512):
    P = xa.shape[0]
    dff = w1.shape[2]
    words = pl.BlockSpec((MOE_TILE, SC_WORDS), lambda i, j, te, nt: (i, 0))
    grid_spec = pltpu.PrefetchScalarGridSpec(
        num_scalar_prefetch=2, grid=(P // MOE_TILE, dff // tf),
        in_specs=[words, words,
                  pl.BlockSpec((1, D_MODEL, tf), lambda i, j, te, nt: (te[i], 0, j)),
                  pl.BlockSpec((1, D_MODEL, tf), lambda i, j, te, nt: (te[i], 0, j)),
                  pl.BlockSpec((1, tf, D_MODEL), lambda i, j, te, nt: (te[i], j, 0))],
        out_specs=[words, words],
        scratch_shapes=[pltpu.VMEM((MOE_TILE, D_MODEL), f32), pltpu.VMEM((MOE_TILE, D_MODEL), bf16)])
    return pl.pallas_call(
        _experts_body, grid_spec=grid_spec, out_shape=[jax.ShapeDtypeStruct((P, SC_WORDS), jnp.int32)] * 2,
        compiler_params=_cparams(("parallel", "arbitrary")), name="moe_experts")(tile_expert, tile_rows, xa, xb, w1, w3, w2)


def _combine_body(x_ref, y0a_ref, y0b_ref, y1a_ref, y1b_ref, meta_ref, ln_ref, o_ref):
    meta = meta_ref[...]
    ff = (meta[:, 4:5] * _unpack_words(y0a_ref[...], y0b_ref[...])
          + meta[:, 5:6] * _unpack_words(y1a_ref[...], y1b_ref[...]))
    o_ref[...] = _layer_norm(ALPHA * x_ref[...] + ff, ln_ref[0:1, :], ln_ref[1:2, :])


def _combine(xt, yga, ygb, meta, ln, tm=1024):
    T = xt.shape[0]
    tm = min(tm, T)
    nb = T // tm
    row = pl.BlockSpec((tm, D_MODEL), lambda i: (i, 0))
    first = pl.BlockSpec((tm, SC_WORDS), lambda i: (i, 0))
    second = pl.BlockSpec((tm, SC_WORDS), lambda i: (nb + i, 0))
    return pl.pallas_call(
        _combine_body, grid=(nb,),
        in_specs=[row, first, first, second, second, pl.BlockSpec((tm, LANES), lambda i: (i, 0)),
                  pl.BlockSpec((2, D_MODEL), lambda i: (0, 0))],
        out_specs=row, out_shape=jax.ShapeDtypeStruct((T, D_MODEL), f32),
        compiler_params=_cparams(("parallel",)), name="moe_combine")(xt, yga, ygb, yga, ygb, meta, ln)


def _moe(xt, router, w1, w3, w2, ln):
    T = xt.shape[0]
    xa, xb, meta, cnt = _router(xt, router)
    counts = cnt[0, :N_EXPERTS].astype(jnp.int32)
    tiles = (counts + MOE_TILE - 1) // MOE_TILE
    tile_end = jnp.cumsum(tiles)
    offset = (tile_end - tiles) * MOE_TILE
    expert = meta[:, 0:2].astype(jnp.int32)
    onehot = expert[:, :, None] == jnp.arange(N_EXPERTS, dtype=jnp.int32)[None, None, :]
    pos = jnp.sum(jnp.where(onehot, offset[None, None, :], 0), axis=2) + meta[:, 2:4].astype(jnp.int32)
    pos = pos.T.reshape(-1)
    P = 2 * T + N_EXPERTS * MOE_TILE
    tile_id = jnp.arange(P // MOE_TILE, dtype=jnp.int32)
    tile_expert = jnp.minimum(jnp.sum((tile_id[:, None] >= tile_end[None, :]).astype(jnp.int32), axis=1), N_EXPERTS - 1)
    first_tile = (tile_end - tiles)[tile_expert]
    tile_rows = jnp.clip(counts[tile_expert] - (tile_id - first_tile) * MOE_TILE, 0, MOE_TILE)
    ya, yb = _experts(_sc_scatter(xa, pos, P), _sc_scatter(xb, pos, P), tile_expert, tile_rows, w1, w3, w2)
    return _combine(xt, _sc_gather(ya, pos), _sc_gather(yb, pos), meta, ln)


def kernel(x, w_in, b_in, att_gq, att_gk, rw_mix, rw_w0, rw_w2, rw_a0, rw_a2, rw_g2, rw_kk, rw_ka, rw_rk, rw_ln_g, rw_ln_b, s5_lam_re, s5_lam_im, s5_log_dt, s5_b_re, s5_b_im, s5_c_re, s5_c_im, s5_d, s5_glu_w, s5_glu_b, ml_conv_w, ml_conv_b, ml_ib, ml_fb, w_gate, b_gate, w_branch, w_out, ln1_g, ln1_b, ffn_w1, ffn_w3, ffn_w2, moe_router, moe_w1, moe_w3, moe_w2, ln2_g, ln2_b):
    B, S, D = x.shape
    assert 2 * B == 8 and D == D_MODEL and S % 512 == 0 and w_in.shape[0] == DEPTH, (x.shape, w_in.shape)
    xt = x.reshape(B * S, D)
    cos, sin = _rope_tables(S)
    for l in range(DEPTH):
        gain = jnp.concatenate([jnp.tile(att_gq[l], 8) * (HEAD_DIM ** -0.5), jnp.tile(att_gk[l], 2)])[None, :]
        (q, k, v, s5u, mq, mk, mv, mo, r, rk, rv, an, bn, lw, gate, bonus, gt) = _proj(
            xt, _proj_params(w_in[l], b_in[l], ml_ib[l], ml_fb[l]), (cos, sin, gain),
            (ml_conv_w[l], ml_conv_b[l][None, :]),
            _rw_params(rw_mix[l], rw_w0[l], rw_w2[l], rw_a0[l], rw_a2[l], rw_g2[l], rw_kk[l], rw_ka[l], rw_rk[l]), B, S)
        score_bound = 8.1 * jnp.max(jnp.abs(att_gq[l])) * jnp.max(jnp.abs(att_gk[l]))
        o_att = _flash(q, k, v, score_bound, B, S)
        yf, yb = _rwkv(r, rk, rv, an, bn, lw, B, S)
        y_s5 = _s5(s5u, _s5_params(s5_lam_re[l], s5_lam_im[l], s5_log_dt[l], s5_b_re[l], s5_b_im[l], s5_c_re[l],
                                   s5_c_im[l]), B, S)
        hf, hb = _mlstm(mq, mk, mv, gt, B, S)
        xt = _merge(xt, o_att, (yf, yb, gate, bonus), (y_s5, s5u), (hf, hb, mo), w_gate[l].astype(bf16), b_gate[l],
                    _att_branch_weight(w_branch[l, 0]).astype(bf16), w_branch[l, 1:].astype(bf16),
                    w_out[l].astype(bf16), jnp.stack([ln1_g[l], ln1_b[l]]), jnp.stack([rw_ln_g[l], rw_ln_b[l]]),
                    jnp.stack([s5_d[l], s5_glu_b[l]]), s5_glu_w[l].astype(bf16))
        ln2 = jnp.stack([ln2_g[l], ln2_b[l]])
        if l % 2 == 0:
            xt = _ffn(xt, ffn_w1[l // 2].astype(bf16), ffn_w3[l // 2].astype(bf16), ffn_w2[l // 2].astype(bf16), ln2)
        else:
            router = jnp.pad(moe_router[l // 2], ((0, 0), (0, LANES - N_EXPERTS)))
            xt = _moe(xt, router, moe_w1[l // 2], moe_w3[l // 2], moe_w2[l // 2], ln2)
    return xt.reshape(B, S, D)
```

```python
import functools
import math

import jax
import jax.numpy as jnp
import numpy as np
from jax import lax
from jax.experimental import pallas as pl
from jax.experimental.pallas import tpu as pltpu
from jax.experimental.pallas import tpu_sc as plsc

f32 = jnp.float32
bf16 = jnp.bfloat16

D_MODEL = 1024
DEPTH = 2
GRID_W = 64
BRANCH_W = 256
HEAD_DIM = 64
ATT_HEADS = 4
ATT_KV_HEADS = 2
ROPE_THETA = 10000.0
QK_EPS = 1e-6
RW_GN_EPS = 64e-5
RW_COLS = 1088
S5_GROUP = 16
S5_GROUPS = 16
S5_STATE = 64
ML_HEADS = 4
N_EXPERTS = 8
ALPHA = (2 * DEPTH) ** 0.25
LN_EPS = 1e-5

LANES = 128
CHUNK = 64
NEG = -1e30
VMEM_LIMIT = 56 * 1024 * 1024

PROJ_SPLITS = (768, 256, 512, 256, 256, RW_COLS)


def _cparams(sem):
    return pltpu.CompilerParams(dimension_semantics=sem, vmem_limit_bytes=VMEM_LIMIT)


def _sigmoid(x):
    return 0.5 * jnp.tanh(0.5 * x) + 0.5


def _dims(a, lhs_c, rhs_c):
    lead = a.ndim - 2
    batch = tuple(range(lead))
    return (((lhs_c + lead,), (rhs_c + lead,)), (batch, batch))


def _bdot(a, b):
    return lax.dot_general(a.astype(bf16), b.astype(bf16), _dims(a, 1, 0), preferred_element_type=f32)


def _bdot_nt(a, b):
    return lax.dot_general(a.astype(bf16), b.astype(bf16), _dims(a, 1, 1), preferred_element_type=f32)


def _bdot_tn(a, b):
    return lax.dot_general(a.astype(bf16), b.astype(bf16), _dims(a, 0, 0), preferred_element_type=f32)


def _split(x):
    hi = x.astype(bf16)
    return hi, (x - hi.astype(f32)).astype(bf16)


def _sdot(a, b, exact):
    dims = _dims(a, 1, 0)
    if exact == "rhs":
        hi, lo = _split(a)
        bb = b.astype(bf16)
        return (lax.dot_general(hi, bb, dims, preferred_element_type=f32)
                + lax.dot_general(lo, bb, dims, preferred_element_type=f32))
    hi, lo = _split(b)
    ab = a.astype(bf16)
    return (lax.dot_general(ab, hi, dims, preferred_element_type=f32)
            + lax.dot_general(ab, lo, dims, preferred_element_type=f32))


def _sdot3(a, b):
    dims = _dims(a, 1, 0)
    ah, al = _split(a)
    bh, bl = _split(b)
    return (lax.dot_general(ah, bh, dims, preferred_element_type=f32)
            + lax.dot_general(ah, bl, dims, preferred_element_type=f32)
            + lax.dot_general(al, bh, dims, preferred_element_type=f32))


def _seg_matrix(n, seg=HEAD_DIM):
    r = lax.broadcasted_iota(jnp.int32, (n, n), 0) // seg
    c = lax.broadcasted_iota(jnp.int32, (n, n), 1) // seg
    return (r == c).astype(f32)


def _layer_norm(y, g, b):
    mu = jnp.mean(y, axis=-1, keepdims=True)
    d = y - mu
    var = jnp.mean(d * d, axis=-1, keepdims=True)
    return d * lax.rsqrt(var + LN_EPS) * g + b


def _row_to_col(row, eye):
    n = eye.shape[-1]
    return jnp.sum(jnp.where(eye, jnp.broadcast_to(row, (row.shape[0], n, n)), 0.0), axis=2, keepdims=True)


def _stack_heads(x):
    h0 = lax.broadcasted_iota(jnp.int32, x.shape, 2) < HEAD_DIM
    return jnp.concatenate([jnp.where(h0, x, 0.0), jnp.where(h0, 0.0, x)], axis=1)


def _proj_body(x_ref, xp_ref, xn_ref, cos_ref, sin_ref, w_ref, b_ref, wg_ref, bg_ref, gain_ref, cw_ref, cb_ref,
               mix_ref, w2_ref, a2_ref, g2_ref, vec_ref,
               q_ref, k_ref, v_ref, s5_ref, mq_ref, mk_ref, mv_ref, mo_ref, r_ref, rk_ref, rv_ref, an_ref, bn_ref,
               lw_ref, gate_ref, bonus_ref, g_ref, mbuf_ref, rbuf_ref):
    tm = x_ref.shape[0]
    xb = x_ref[...].astype(bf16)
    xh = jnp.concatenate([xp_ref[...], x_ref[...], xn_ref[...]], axis=0).astype(bf16)
    offs = np.cumsum((0,) + PROJ_SPLITS)

    def cols(rows, n):
        sl = slice(int(offs[n]), int(offs[n + 1]))
        return jnp.dot(rows, w_ref[:, sl], preferred_element_type=f32) + b_ref[:, sl]

    q_ref[...], k_ref[...], v_ref[...] = _att_prep(cols(xb, 0), cos_ref[...], sin_ref[...], gain_ref[...])
    s5_ref[...] = cols(xb, 1)
    _fill_halo(mbuf_ref, cols(xh, 2))
    mq_ref[...], mk_ref[...] = _ml_prep(mbuf_ref, tm, cw_ref[...], cb_ref[...])
    mv_ref[...] = cols(xb, 3)
    mo_ref[...] = cols(xb, 4)
    _fill_halo(rbuf_ref, cols(xh, 5))
    outs = _rw_prep(rbuf_ref, tm, mix_ref[...], w2_ref[...], a2_ref[...], g2_ref[...], vec_ref[...])
    for o_ref, val in zip((r_ref, rk_ref, rv_ref, an_ref, bn_ref), outs[0:5]):
        o_ref[...] = val
    lw_ref[0], lw_ref[1], gate_ref[...], bonus_ref[...] = outs[5:9]
    g_ref[...] = lax.dot_general(wg_ref[...], xb, (((1,), (1,)), ((), ())), preferred_element_type=f32) + bg_ref[...]


def _proj(xt, proj_prm, att_prm, ml_prm, rw_prm, B, S, tm=512):
    T = B * S
    nb = S // tm
    n_tot = sum(PROJ_SPLITS)
    row = lambda n: pl.BlockSpec((tm, n), lambda b, i: (b * nb + i, 0))
    const = lambda a: pl.BlockSpec(a.shape, lambda b, i: (0,) * a.ndim)
    tab = pl.BlockSpec((tm, LANES), lambda b, i: (i, 0))
    f32out = lambda n: jax.ShapeDtypeStruct((T, n), f32)
    consts = list(proj_prm) + [att_prm[2]] + list(ml_prm) + list(rw_prm)
    out_specs = ([row(512), row(LANES), row(2 * LANES)] + [row(BRANCH_W)] * 10
                 + [pl.BlockSpec((2, tm, BRANCH_W), lambda b, i: (0, b * nb + i, 0)), row(BRANCH_W), row(BRANCH_W),
                    pl.BlockSpec((16, tm), lambda b, i: (0, b * nb + i))])
    out_shape = ([jax.ShapeDtypeStruct((T, 512), bf16), jax.ShapeDtypeStruct((T, LANES), bf16),
                  jax.ShapeDtypeStruct((T, 2 * LANES), bf16)] + [f32out(BRANCH_W)] * 10
                 + [jax.ShapeDtypeStruct((2, T, BRANCH_W), f32), f32out(BRANCH_W), f32out(BRANCH_W),
                    jax.ShapeDtypeStruct((16, T), f32)])
    return pl.pallas_call(
        _proj_body, grid=(B, nb),
        in_specs=_halo_specs(D_MODEL, tm, B, S) + [tab, tab] + [const(a) for a in consts],
        out_specs=out_specs, out_shape=out_shape,
        scratch_shapes=[pltpu.VMEM((tm + 16, 512), f32), pltpu.VMEM((tm + 16, RW_COLS), f32)],
        compiler_params=_cparams(("parallel", "parallel")), name="proj")(
            xt, xt, xt, att_prm[0], att_prm[1], *consts)


def _proj_params(w_in, b_in, ml_ib, ml_fb):
    o = np.cumsum((0, 256, 128, 128, RW_COLS, 256, 512, 256, 8, 8, 256))
    sl = lambda i: (w_in[:, o[i]:o[i + 1]], b_in[o[i]:o[i + 1]])
    (wq, bq), (wk, bk), (wv, bv), (wrw, brw), (ws5, bs5), (wqk, bqk), (wmv, bmv), (wi, bi), (wf, bf), (wo, bo) = (
        sl(i) for i in range(10))
    zw, zb = jnp.zeros((D_MODEL, HEAD_DIM), f32), jnp.zeros((HEAD_DIM,), f32)
    wq_e, bq_e = [], []
    for h in range(ATT_HEADS):
        wh, bh = wq[:, 64 * h:64 * h + 64], bq[64 * h:64 * h + 64]
        wq_e += [wh, zw] if h // 2 == 0 else [zw, wh]
        bq_e += [bh, zb] if h // 2 == 0 else [zb, bh]
    w = jnp.concatenate(wq_e + [wk, wv, ws5, wqk, wmv, wo, wrw], axis=1)
    b = jnp.concatenate(bq_e + [bk, bv, bs5, bqk, bmv, bo, brw])
    wg = jnp.concatenate([wi, wf], axis=1).T
    bg = jnp.concatenate([bi + ml_ib.reshape(-1), bf + ml_fb.reshape(-1)])
    return w.astype(bf16), b[None, :], wg.astype(bf16), bg[:, None]


def _att_prep(att, cos, sin, gain):
    x = att[:, 0:640]
    ms = _sdot(x * x, _seg_matrix(640), "rhs") * (1.0 / HEAD_DIM)
    xn = x * lax.rsqrt(ms + QK_EPS) * gain
    lane = lax.broadcasted_iota(jnp.int32, xn.shape, 1)
    partner = jnp.where((lane % 32) < 16, pltpu.roll(xn, 640 - 16, 1), pltpu.roll(xn, 16, 1))
    rot = xn * jnp.concatenate([cos] * 5, axis=1) + partner * jnp.concatenate([sin] * 5, axis=1)
    v = jnp.concatenate([att[:, 640:768].astype(bf16), jnp.ones((x.shape[0], LANES), bf16)], axis=1)
    return rot[:, 0:512].astype(bf16), rot[:, 512:640].astype(bf16), v


def _rope_tables(S):
    t = np.arange(S)
    row = (t // GRID_W).astype(np.float32)
    col = (t % GRID_W).astype(np.float32)
    n = 16
    inv = np.power(np.float32(ROPE_THETA), -np.arange(n, dtype=np.float32) / n).astype(np.float32)
    ar = jnp.asarray(row)[:, None] * jnp.asarray(inv)
    ac = jnp.asarray(col)[:, None] * jnp.asarray(inv)
    cos = jnp.concatenate([jnp.cos(ar), jnp.cos(ar), jnp.cos(ac), jnp.cos(ac)], axis=1)
    sin = jnp.concatenate([-jnp.sin(ar), jnp.sin(ar), -jnp.sin(ac), jnp.sin(ac)], axis=1)
    return jnp.concatenate([cos, cos], axis=1), jnp.concatenate([sin, sin], axis=1)


def _flash_body(q_ref, k_ref, v_ref, o_ref, acc_ref, *m_scratch, tk, track_max):
    tq = q_ref.shape[0]
    nk = k_ref.shape[0] // tk
    q2 = jnp.concatenate([q_ref[:, 0:LANES], q_ref[:, LANES:2 * LANES]], axis=0)
    acc_ref[...] = jnp.zeros(acc_ref.shape, f32)
    if track_max:
        m_ref, = m_scratch
        m_ref[...] = jnp.full(m_ref.shape, NEG, f32)

    def step(j, carry):
        rows = pl.ds(pl.multiple_of(j * tk, tk), tk)
        s = lax.dot_general(q2, k_ref[rows, :], (((1,), (1,)), ((), ())), preferred_element_type=f32)
        if track_max:
            m_old = m_ref[...]
            m_new = jnp.maximum(m_old, jnp.max(s, axis=1, keepdims=True))
            p = jnp.exp(s - m_new).astype(bf16)
            acc_ref[...] = jnp.exp(m_old - m_new) * acc_ref[...] + jnp.dot(p, v_ref[rows, :], preferred_element_type=f32)
            m_ref[...] = m_new
        else:
            acc_ref[...] += jnp.dot(jnp.exp(s).astype(bf16), v_ref[rows, :], preferred_element_type=f32)
        return carry

    lax.fori_loop(0, nk, step, 0)
    o = acc_ref[:, 0:LANES] / acc_ref[:, LANES:2 * LANES]
    o_ref[...] = jnp.concatenate([o[0:tq], o[tq:2 * tq]], axis=1)


SCORE_BOUND_MAX = 60.0


def _flash(q, k, v, score_bound, B, S, tq=256, tk=8192):
    T = B * S
    nb = S // tq
    tk = min(tk, S)

    def call(track_max):
        scratch = [pltpu.VMEM((2 * tq, 2 * LANES), f32)] + ([pltpu.VMEM((2 * tq, 1), f32)] if track_max else [])
        return pl.pallas_call(
            functools.partial(_flash_body, tk=tk, track_max=track_max), grid=(B, ATT_KV_HEADS, nb),
            in_specs=[pl.BlockSpec((tq, 2 * LANES), lambda b, g, i: (b * nb + i, g)),
                      pl.BlockSpec((S, LANES), lambda b, g, i: (b, 0)),
                      pl.BlockSpec((S, 2 * LANES), lambda b, g, i: (b, 0))],
            out_specs=pl.BlockSpec((tq, 2 * LANES), lambda b, g, i: (b * nb + i, g)),
            out_shape=jax.ShapeDtypeStruct((T, 512), f32), scratch_shapes=scratch,
            compiler_params=_cparams(("parallel", "parallel", "parallel")),
            name="flash_safe" if track_max else "flash")(q, k, v)

    return lax.cond(score_bound <= SCORE_BOUND_MAX, lambda: call(False), lambda: call(True))


def _halo_specs(width, tm, B, S):
    nb = S // tm
    r8 = tm // 8
    last8 = B * S // 8 - 1

    def main(b, i):
        return (b * nb + i, 0)

    def prev(b, i):
        return (jnp.maximum(b * (S // 8) + i * r8 - 1, 0), 0)

    def nxt(b, i):
        return (jnp.minimum(b * (S // 8) + (i + 1) * r8, last8), 0)

    return [pl.BlockSpec((tm, width), main), pl.BlockSpec((8, width), prev), pl.BlockSpec((8, width), nxt)]


def _fill_halo(buf_ref, xh):
    tm = xh.shape[0] - 16
    i = pl.program_id(1)
    last = pl.num_programs(1) - 1
    buf_ref[...] = xh
    buf_ref[pl.ds(0, 8), :] = jnp.where(i > 0, xh[0:8], 0.0)
    buf_ref[pl.ds(8 + tm, 8), :] = jnp.where(i < last, xh[8 + tm:16 + tm], 0.0)


def _rw_prep(buf_ref, tm, mix, w2, a2, g2, vec):
    x = buf_ref[pl.ds(8, tm), :]
    p = x + mix[0:1, :] * (buf_ref[pl.ds(7, tm), :] - x) + mix[1:2, :] * (buf_ref[pl.ds(9, tm), :] - x)
    r, k, v = p[:, 0:256], p[:, 256:512], p[:, 512:768]
    w0f, w0b, a0, k_k, k_a, r_k = (vec[j:j + 1, :] for j in range(6))
    dec = _bdot(jnp.tanh(p[:, 768:896]), w2)
    z = p[:, 896:1088]
    a = _sigmoid(a0 + _bdot(z, a2))
    gate = _bdot(_sigmoid(z), g2)
    seg = _seg_matrix(BRANCH_W)
    kk = k * k_k
    kk = kk * jnp.minimum(lax.rsqrt(_sdot(kk * kk, seg, "rhs")), 1e12)
    k2 = k * (1.0 + (a - 1.0) * k_a)
    bonus = _sdot(r * k2 * r_k, seg, "rhs") * v
    lwf = -math.exp(-0.5) * _sigmoid(w0f + dec[:, 0:256])
    lwb = -math.exp(-0.5) * _sigmoid(w0b + dec[:, 256:512])
    return r, k2, v, -kk, kk * a, lwf, lwb, gate, bonus


def _by_direction(x, fwd, bwd, fn):
    h = x.shape[0] // 2
    return jnp.concatenate([fn(x[0:h], fwd), fn(x[h:2 * h], bwd)], axis=0)


def _keep(x, masks):
    return _by_direction(x, masks[0], masks[1], lambda t, m: jnp.where(m, t, 0.0))


def _both(masks, nchain):
    h = nchain // 2
    return jnp.concatenate([jnp.broadcast_to(m.astype(f32), (h,) + m.shape[1:]) for m in masks], axis=0)


def _pair_masks():
    n = 2 * CHUNK
    r = lax.broadcasted_iota(jnp.int32, (1, n, n), 1)
    c = lax.broadcasted_iota(jnp.int32, (1, n, n), 2)
    same = (r // CHUNK) == (c // CHUNK)
    return r, c, same, (same & (c < r), same & (c > r)), (same & (c <= r), same & (c >= r))


def _rw_chunk(st, r, k, v, an, bn, lw):
    L = CHUNK
    N = r.shape[0]
    ri = lax.broadcasted_iota(jnp.int32, (1, L, L), 1)
    ci = lax.broadcasted_iota(jnp.int32, (1, L, L), 2)
    cs = _sdot(_both((ci <= ri, ci >= ri), N), lw, "lhs")
    tot = jnp.sum(lw, axis=1, keepdims=True)
    e_neg = jnp.exp(-cs)
    at = an * jnp.exp(cs - lw)
    rt = r * jnp.exp(cs)
    a2, b2, k2, v2 = _stack_heads(at), _stack_heads(bn * e_neg), _stack_heads(k * e_neg), _stack_heads(v)
    t = lax.broadcasted_iota(jnp.int32, (1, L, LANES), 1)
    tc = lax.broadcasted_iota(jnp.int32, (1, L, LANES), 2) % L
    strict, incl = (tc < t, tc > t), (tc <= t, tc >= t)
    g = _bdot_nt(jnp.concatenate([at, rt], axis=1), jnp.concatenate([b2, k2], axis=1))
    mab = _keep(g[:, 0:L, 0:LANES], strict)
    mak = _keep(g[:, 0:L, LANES:2 * LANES], strict)
    pb = _keep(g[:, L:2 * L, 0:LANES], incl)
    pk = _keep(g[:, L:2 * L, LANES:2 * LANES], incl)
    mul = lambda p, q: _bdot(p, _stack_heads(q))
    m8 = jnp.where((t // 8) == (tc // 8), mab, 0.0)
    x = (t == tc).astype(f32) + m8
    p = mul(m8, m8)
    x = x + mul(x, p)
    p = mul(p, p)
    x = x + mul(x, p)
    n = 8
    while n < L:
        e = jnp.where(((t // (2 * n)) == (tc // (2 * n))) & ((t // n) != (tc // n)), mab, 0.0)
        x = x + mul(mul(x, e), x)
        n *= 2
    wu = _bdot(x, jnp.concatenate([a2, _stack_heads(_bdot(mak, v2))], axis=2))
    wu2 = jnp.concatenate([_stack_heads(wu[:, :, 0:LANES]), _stack_heads(wu[:, :, LANES:2 * LANES])], axis=2)
    pwu = _bdot(pb, wu2)
    rh = rt + pwu[:, :, 0:LANES]
    y = pwu[:, :, LANES:2 * LANES] + _bdot(jnp.concatenate([pk, rh], axis=2), jnp.concatenate([v2, st], axis=1))
    r128 = lax.broadcasted_iota(jnp.int32, (1, LANES, LANES), 1)
    c128 = lax.broadcasted_iota(jnp.int32, (1, LANES, LANES), 2)
    gam = _row_to_col(jnp.exp(tot), r128 == c128)
    bwu = _bdot_tn(b2, wu2)
    st = gam * (st + _bdot(bwu[:, :, 0:LANES], st) + bwu[:, :, LANES:2 * LANES] + _bdot_tn(k2, v2))
    return y, st


def _load_pairs(ref, rows):
    return jnp.concatenate([ref[:, rows, 0:LANES], ref[:, rows, LANES:2 * LANES]], axis=0)


def _store_pairs(ref, rows, y):
    nb = ref.shape[0]
    ref[:, rows, 0:LANES] = y[0:nb]
    ref[:, rows, LANES:2 * LANES] = y[nb:2 * nb]


def _load_both(f_ref, b_ref, rows_f, rows_b):
    return jnp.concatenate([_load_pairs(f_ref, rows_f), _load_pairs(b_ref, rows_b)], axis=0)


def _chunk_rows(cc, nch):
    return (pl.ds(pl.multiple_of(cc * CHUNK, CHUNK), CHUNK),
            pl.ds(pl.multiple_of((nch - 1 - cc) * CHUNK, CHUNK), CHUNK))


def _rw_scan_body(rf_ref, rb_ref, kf_ref, kb_ref, vf_ref, vb_ref, anf_ref, anb_ref, bnf_ref, bnb_ref, lwf_ref, lwb_ref,
                  yf_ref, yb_ref, st_ref):
    nch = rf_ref.shape[1] // CHUNK
    half = st_ref.shape[0] // 2

    @pl.when(pl.program_id(0) == 0)
    def _():
        st_ref[...] = jnp.zeros(st_ref.shape, f32)

    def step(cc, carry):
        rows_f, rows_b = _chunk_rows(cc, nch)
        pairs = ((rf_ref, rb_ref), (kf_ref, kb_ref), (vf_ref, vb_ref), (anf_ref, anb_ref), (bnf_ref, bnb_ref),
                 (lwf_ref.at[0], lwb_ref.at[0]))
        y, st = _rw_chunk(st_ref[...], *(_load_both(f, b, rows_f, rows_b) for f, b in pairs))
        _store_pairs(yf_ref, rows_f, y[0:half])
        _store_pairs(yb_ref, rows_b, y[half:2 * half])
        st_ref[...] = st
        return carry

    lax.fori_loop(0, nch, step, 0)


def _rw_scan(r, k, v, an, bn, lw, B, S, ts=128):
    nb = S // ts
    fwd = pl.BlockSpec((B, ts, BRANCH_W), lambda i: (0, i, 0))
    bwd = pl.BlockSpec((B, ts, BRANCH_W), lambda i: (0, nb - 1 - i, 0))
    out = jax.ShapeDtypeStruct((B, S, BRANCH_W), f32)
    return pl.pallas_call(
        _rw_scan_body, grid=(nb,),
        in_specs=[fwd, bwd] * 5 + [pl.BlockSpec((1, B, ts, BRANCH_W), lambda i: (0, 0, i, 0)),
                                   pl.BlockSpec((1, B, ts, BRANCH_W), lambda i: (1, 0, nb - 1 - i, 0))],
        out_specs=[fwd, bwd], out_shape=[out, out],
        scratch_shapes=[pltpu.VMEM((2 * B * BRANCH_W // LANES, LANES, LANES), f32)],
        compiler_params=_cparams(("arbitrary",)), name="rw_scan")(r, r, k, k, v, v, an, an, bn, bn, lw, lw)


def _rw_finish(y, gate, bonus, gn):
    seg = _seg_matrix(BRANCH_W)
    mu = _sdot(y, seg, "rhs") * (1.0 / HEAD_DIM)
    d = y - mu
    var = _sdot(d * d, seg, "rhs") * (1.0 / HEAD_DIM)
    yn = d * lax.rsqrt(var + RW_GN_EPS) * gn[0:1, :] + gn[1:2, :]
    return (yn + bonus) * gate


def _rwkv(r, k, v, an, bn, lw, B, S):
    seq = [t.reshape(B, S, BRANCH_W) for t in (r, k, v, an, bn)] + [lw.reshape(2, B, S, BRANCH_W)]
    yf, yb = _rw_scan(*seq, B, S)
    return yf.reshape(B * S, BRANCH_W), yb.reshape(B * S, BRANCH_W)


def _rw_params(mix, w0, w2, a0, a2, g2, k_k, k_a, r_k):
    z = jnp.zeros((64, 256), f32)
    w2c = jnp.concatenate([jnp.concatenate([w2[0], z], axis=1), jnp.concatenate([z, w2[1]], axis=1)], axis=0)
    a2p = jnp.concatenate([a2, jnp.zeros((128, 256), f32)], axis=0)
    g2p = jnp.concatenate([jnp.zeros((64, 256), f32), g2], axis=0)
    vec = jnp.stack([w0[0], w0[1], a0, k_k, k_a, r_k.reshape(-1), jnp.zeros_like(a0), jnp.zeros_like(a0)])
    return mix, w2c.astype(bf16), a2p.astype(bf16), g2p.astype(bf16), vec


S5_HALF = S5_GROUPS * S5_STATE // 2


def _s5_scan_body(uf_ref, ub_ref, bh_ref, cre_ref, cim_ref, lre_ref, lim_ref, yf_ref, yb_ref,
                  sre_ref, sim_ref, bre_ref, bim_ref):
    ts = uf_ref.shape[0]
    n = S5_HALF

    @pl.when(pl.program_id(0) == 0)
    def _():
        sre_ref[...] = jnp.zeros(sre_ref.shape, f32)
        sim_ref[...] = jnp.zeros(sim_ref.shape, f32)

    row_half = lax.broadcasted_iota(jnp.int32, uf_ref.shape, 1) // 4
    lane_half = lax.broadcasted_iota(jnp.int32, uf_ref.shape, 2) // (BRANCH_W // 2)
    for d, u_ref in enumerate((uf_ref, ub_ref)):
        lhs = jnp.where(row_half == lane_half, u_ref[...], 0.0).reshape(ts * 8, BRANCH_W)
        bu = _bdot(lhs, bh_ref[d]).reshape(ts, 8, 2 * n)
        bre_ref[d] = bu[:, :, 0:n]
        bim_ref[d] = bu[:, :, n:2 * n]
    lre = lre_ref[...]
    lim = lim_ref[...]

    def step(t, carry):
        out = []
        for d, td in enumerate((t, ts - 1 - t)):
            sre, sim = carry[2 * d], carry[2 * d + 1]
            nre = lre[d] * sre - lim[d] * sim + bre_ref[d, td]
            nim = lre[d] * sim + lim[d] * sre + bim_ref[d, td]
            bre_ref[d, td] = nre
            bim_ref[d, td] = nim
            out += [nre, nim]
        return tuple(out)

    s = lax.fori_loop(0, ts, step, (sre_ref[0], sim_ref[0], sre_ref[1], sim_ref[1]), unroll=4)
    sre_ref[0], sim_ref[0], sre_ref[1], sim_ref[1] = s
    low = lane_half == 0
    for d, y_ref in enumerate((yf_ref, yb_ref)):
        yv = (_bdot(bre_ref[d].reshape(ts * 8, n), cre_ref[d])
              - _bdot(bim_ref[d].reshape(ts * 8, n), cim_ref[d])).reshape(ts, 8, BRANCH_W)
        y_ref[...] = jnp.where(low, yv, pltpu.roll(yv, 4, 1))


def _s5_scan(u8, bh, cre, cim, lre, lim, ts=128):
    S = u8.shape[0]
    nb = S // ts
    n = S5_HALF
    full = lambda shape: pl.BlockSpec(shape, lambda i: (0,) * len(shape))
    fwd = pl.BlockSpec((ts, 8, BRANCH_W), lambda i: (i, 0, 0))
    bwd = pl.BlockSpec((ts, 8, BRANCH_W), lambda i: (nb - 1 - i, 0, 0))
    out = jax.ShapeDtypeStruct((S, 8, BRANCH_W), f32)
    return pl.pallas_call(
        _s5_scan_body, grid=(nb,),
        in_specs=[fwd, bwd, full((2, BRANCH_W, 2 * n)), full((2, n, BRANCH_W)), full((2, n, BRANCH_W)),
                  full((2, 8, n)), full((2, 8, n))],
        out_specs=[fwd, bwd], out_shape=[out, out],
        scratch_shapes=[pltpu.VMEM((2, 8, n), f32), pltpu.VMEM((2, 8, n), f32),
                        pltpu.VMEM((2, ts, 8, n), f32), pltpu.VMEM((2, ts, 8, n), f32)],
        compiler_params=_cparams(("arbitrary",)), name="s5_scan")(u8, u8, bh, cre, cim, lre, lim)


def _s5_finish(y, u, vec, w):
    y = y + u * vec[0:1, :]
    y = 0.5 * y * (1.0 + jnp.tanh(math.sqrt(2.0 / math.pi) * (y + 0.044715 * (y * y * y))))
    return y * _sigmoid(_bdot(y, w) + vec[1:2, :])


def _s5_params(lam_re, lam_im, log_dt, b_re, b_im, c_re, c_im):
    G, P, C = S5_GROUPS, S5_STATE, S5_GROUP
    H = G // 2
    eye = jnp.eye(H, dtype=f32)
    b_c = lax.complex(b_re, b_im)

    def b_part(x):
        return jnp.einsum('ab,hapc->hacbp', eye, x.reshape(2, H, P, C)).reshape(G * C, H * P)

    def c_part(x):
        return jnp.einsum('ab,hacp->aphbc', eye, x.reshape(2, H, C, P)).reshape(H * P, G * C)

    def rows(x):
        return jnp.broadcast_to(x.reshape(2, 1, H * P), (2, 4, H * P)).reshape(8, H * P)

    bh, cre, cim, lre, lim = [], [], [], [], []
    for d in range(2):
        lam = lax.complex(jnp.minimum(lam_re[d], -1e-4), lam_im[d])
        lam_bar = jnp.exp(lam * jnp.exp(log_dt[d])[:, None])
        b_bar = ((lam_bar - 1.0) / lam)[..., None] * b_c
        bh.append(jnp.concatenate([b_part(jnp.real(b_bar)), b_part(jnp.imag(b_bar))], axis=1))
        cre.append(c_part(c_re[d]))
        cim.append(c_part(c_im[d]))
        lre.append(rows(jnp.real(lam_bar)))
        lim.append(rows(jnp.imag(lam_bar)))
    return (jnp.stack(bh).astype(bf16), jnp.stack(cre).astype(bf16), jnp.stack(cim).astype(bf16),
            jnp.stack(lre), jnp.stack(lim))


def _s5(u, prm, B, S):
    u3 = u.reshape(B, S, BRANCH_W).transpose(1, 0, 2)
    yf, yb = _s5_scan(jnp.concatenate([u3, u3], axis=1), *prm)
    return (yf[:, 0:B] + yb[:, 0:B]).transpose(1, 0, 2).reshape(B * S, BRANCH_W)


def _ml_prep(buf_ref, tm, w, b):
    y = b
    for j in range(5):
        y = y + w[j:j + 1, :] * buf_ref[pl.ds(6 + j, tm), :]
    y = y * _sigmoid(y)
    return y[:, 0:BRANCH_W], y[:, BRANCH_W:2 * BRANCH_W] * (HEAD_DIM ** -0.5)


def _ml_chunk(state, q, k, v, li, lfp):
    cn, m_row = state
    L = CHUNK
    N = q.shape[0]
    rr, cc, same, _, incl2 = _pair_masks()
    same_f = _both((same, same), N)
    lane = lax.broadcasted_iota(jnp.int32, (1, 1, 2 * L), 2)
    lf = jnp.minimum(lfp, 0.0) - jnp.log(1.0 + jnp.exp(-jnp.abs(lfp)))
    lf8 = jnp.broadcast_to(lf, (N, 8, 2 * L))
    b_row = _sdot(lf8, _both((incl2[1], incl2[0]), N), "rhs")[:, 0:1]
    g_row = _sdot(lf8, same_f, "rhs")[:, 0:1]
    w_end = g_row - b_row + li
    m0 = jnp.max(jnp.where(lane < L, w_end, NEG), axis=2, keepdims=True)
    m1 = jnp.max(jnp.where(lane < L, NEG, w_end), axis=2, keepdims=True)
    m_loc = jnp.where(lane < L, m0, m1)
    t = lax.broadcasted_iota(jnp.int32, (1, L, LANES), 1)
    tc = lax.broadcasted_iota(jnp.int32, (1, L, LANES), 2) % L
    diag = t == tc

    def cols(row):
        return _sdot(jnp.where(diag, jnp.broadcast_to(row, (N, L, LANES)), 0.0), same_f, "rhs")

    e_col, b_col = cols(jnp.exp(w_end - m_loc)), cols(b_row)
    k2, v2 = _stack_heads(k), _stack_heads(v)
    v1 = jnp.concatenate([v2, _stack_heads(jnp.ones_like(v))], axis=2)
    log_inter = b_col + m_row
    log_intra = _by_direction(b_col - b_row + li, tc <= t, tc >= t, lambda x, m: jnp.where(m, x, NEG))
    head0 = lane < L
    r0 = jnp.max(jnp.where(head0, log_intra, NEG), axis=2, keepdims=True)
    r1 = jnp.max(jnp.where(head0, NEG, log_intra), axis=2, keepdims=True)
    m_r = jnp.maximum(log_inter, jnp.where(head0, r0, r1))
    s = _bdot_nt(q, k2) * jnp.exp(log_intra - m_r)
    inter = jnp.exp(log_inter - m_r)
    nd = _bdot(s, v1) + jnp.concatenate([inter, inter], axis=2) * _bdot(q, cn)
    h = nd[:, :, 0:LANES] / jnp.maximum(jnp.abs(nd[:, :, LANES:2 * LANES]), jnp.exp(-m_r))
    m_new = jnp.maximum(g_row + m_row, m_loc)
    a = jnp.exp(g_row + m_row - m_new)
    bb = jnp.exp(m_loc - m_new)
    cn = (jnp.concatenate([a, a], axis=2) * cn
          + jnp.concatenate([bb, bb], axis=2) * _bdot_tn(_stack_heads(e_col * k), v1))
    return h, (cn, m_new)


GATE_ROWS = 8


def _ml_scan_body(qf_ref, qb_ref, kf_ref, kb_ref, vf_ref, vb_ref, gf_ref, gb_ref, hf_ref, hb_ref, c_ref, m_ref):
    nch = qf_ref.shape[1] // CHUNK
    half = c_ref.shape[0] // 2
    i = pl.program_id(0)
    nb = pl.num_programs(0)
    per = GATE_ROWS // nch
    base_f = (i % per) * nch
    base_b = ((nb - 1 - i) % per) * nch

    @pl.when(i == 0)
    def _():
        c_ref[...] = jnp.zeros(c_ref.shape, f32)
        m_ref[...] = jnp.zeros(m_ref.shape, f32)

    def step(cc, carry):
        rows_f, rows_b = _chunk_rows(cc, nch)
        gate = lambda t: jnp.concatenate(
            [gf_ref[t, 0, 0, :, pl.ds(base_f + cc, 1), :], gf_ref[t, 0, 1, :, pl.ds(base_f + cc, 1), :],
             gb_ref[t, 0, 0, :, pl.ds(base_b + nch - 1 - cc, 1), :], gb_ref[t, 0, 1, :, pl.ds(base_b + nch - 1 - cc, 1), :]],
            axis=0)
        h, (cn, m_row) = _ml_chunk((c_ref[...], m_ref[...]), _load_both(qf_ref, qb_ref, rows_f, rows_b),
                                   _load_both(kf_ref, kb_ref, rows_f, rows_b), _load_both(vf_ref, vb_ref, rows_f, rows_b),
                                   gate(0), gate(1))
        _store_pairs(hf_ref, rows_f, h[0:half])
        _store_pairs(hb_ref, rows_b, h[half:2 * half])
        c_ref[...] = cn
        m_ref[...] = m_row
        return carry

    lax.fori_loop(0, nch, step, 0)


def _ml_scan(q, k, v, g, B, S, ts=256):
    nb = S // ts
    per = GATE_ROWS * CHUNK // ts
    nchain = 2 * B * BRANCH_W // LANES
    fwd = pl.BlockSpec((B, ts, BRANCH_W), lambda i: (0, i, 0))
    bwd = pl.BlockSpec((B, ts, BRANCH_W), lambda i: (0, nb - 1 - i, 0))
    out = jax.ShapeDtypeStruct((B, S, BRANCH_W), f32)
    return pl.pallas_call(
        _ml_scan_body, grid=(nb,),
        in_specs=[fwd, bwd] * 3 + [pl.BlockSpec((2, 1, 2, B, GATE_ROWS, LANES), lambda i: (0, 0, 0, 0, i // per, 0)),
                                   pl.BlockSpec((2, 1, 2, B, GATE_ROWS, LANES),
                                                lambda i: (0, 1, 0, 0, (nb - 1 - i) // per, 0))],
        out_specs=[fwd, bwd], out_shape=[out, out],
        scratch_shapes=[pltpu.VMEM((nchain, LANES, 2 * LANES), f32), pltpu.VMEM((nchain, 1, LANES), f32)],
        compiler_params=_cparams(("arbitrary",)), name="ml_scan")(q, q, k, k, v, v, g, g)


def _mlstm(q, k, mv, gt, B, S):
    g = gt.reshape(2, 2, 2, 2, B, S // CHUNK, CHUNK).transpose(0, 1, 2, 4, 5, 3, 6).reshape(2, 2, 2, B, S // CHUNK, LANES)
    hf, hb = _ml_scan(*(t.reshape(B, S, BRANCH_W) for t in (q, k, mv)), g, B, S)
    return hf.reshape(B * S, BRANCH_W), hb.reshape(B * S, BRANCH_W)


def _merge_body(x_ref, att_ref, yf_ref, yb_ref, rg_ref, rb_ref, sy_ref, su_ref, hf_ref, hb_ref, mo_ref,
                wg_ref, bg_ref, wba_ref, wb_ref, wo_ref, ln_ref, gn_ref, sv_ref, sw_ref, o_ref):
    x = x_ref[...]
    xb = x.astype(bf16)
    rw = _rw_finish(yf_ref[...] + yb_ref[...], rg_ref[...], rb_ref[...], gn_ref[...])
    s5 = _s5_finish(sy_ref[...], su_ref[...], sv_ref[...], sw_ref[...])
    ml = _sigmoid(mo_ref[...]) * (hf_ref[...] + hb_ref[...])
    branches = (att_ref[...], rw, s5, ml)
    merged = None
    for n in range(4):
        gate = _sigmoid(jnp.dot(xb, wg_ref[n], preferred_element_type=f32) + bg_ref[n:n + 1, :])
        wide = _bdot(branches[n], wba_ref[...] if n == 0 else wb_ref[n - 1])
        merged = gate * wide if merged is None else merged + gate * wide
    y = ALPHA * x + _bdot(merged, wo_ref[...])
    o_ref[...] = _layer_norm(y, ln_ref[0:1, :], ln_ref[1:2, :])


def _merge(xt, att, rw_parts, s5_parts, ml_parts, wg, bg, wba, wb, wo, ln, gn, s5_vec, s5_w, tm=256):
    T = xt.shape[0]
    row = lambda n: pl.BlockSpec((tm, n), lambda i: (i, 0))
    const = lambda shape: pl.BlockSpec(shape, lambda i: (0,) * len(shape), pipeline_mode=pl.Buffered(1))
    return pl.pallas_call(
        _merge_body, grid=(T // tm,),
        in_specs=[row(D_MODEL), row(512)] + [row(BRANCH_W)] * 9
        + [const((4, D_MODEL, D_MODEL)), const((4, D_MODEL)), const((512, D_MODEL)), const((3, BRANCH_W, D_MODEL)),
           const((D_MODEL, D_MODEL)), const((2, D_MODEL)), const((2, BRANCH_W)), const((2, BRANCH_W)),
           const((BRANCH_W, BRANCH_W))],
        out_specs=row(D_MODEL), out_shape=jax.ShapeDtypeStruct((T, D_MODEL), f32),
        compiler_params=_cparams(("parallel",)), name="merge")(
            xt, att, *rw_parts, *s5_parts, *ml_parts, wg, bg, wba, wb, wo, ln, gn, s5_vec, s5_w)


def _att_branch_weight(wb):
    z = jnp.zeros((HEAD_DIM, D_MODEL), f32)
    parts = []
    for h in range(ATT_HEADS):
        wh = wb[64 * h:64 * h + 64]
        parts += [wh, z] if h // 2 == 0 else [z, wh]
    return jnp.concatenate(parts, axis=0)


def _ffn_body(x_ref, w1_ref, w3_ref, w2_ref, ln_ref, o_ref):
    x = x_ref[...]
    xb = x.astype(bf16)
    h1 = jnp.dot(xb, w1_ref[...], preferred_element_type=f32)
    h3 = jnp.dot(xb, w3_ref[...], preferred_element_type=f32)
    ff = _bdot(h1 * _sigmoid(h1) * h3, w2_ref[...])
    o_ref[...] = _layer_norm(ALPHA * x + ff, ln_ref[0:1, :], ln_ref[1:2, :])


def _ffn(xt, w1, w3, w2, ln, tm=512):
    T = xt.shape[0]
    row = pl.BlockSpec((tm, D_MODEL), lambda i: (i, 0))
    const = lambda a: pl.BlockSpec(a.shape, lambda i: (0,) * a.ndim, pipeline_mode=pl.Buffered(1))
    return pl.pallas_call(
        _ffn_body, grid=(T // tm,), in_specs=[row, const(w1), const(w3), const(w2), const(ln)],
        out_specs=row, out_shape=jax.ShapeDtypeStruct((T, D_MODEL), f32),
        compiler_params=_cparams(("parallel",)), name="ffn")(xt, w1, w3, w2, ln)


MOE_TILE = 1024
SC_WINDOW = 128
SC_WORDS = 256


def _pack_words(x):
    bits = lax.bitcast_convert_type(x.astype(bf16).astype(f32), jnp.int32)
    half = D_MODEL // 2
    w = lax.shift_right_logical(bits[:, 0:half], 16) | bits[:, half:D_MODEL]
    return w[:, 0:SC_WORDS], w[:, SC_WORDS:2 * SC_WORDS]


def _unpack_words(wa, wb):
    w = jnp.concatenate([wa, wb], axis=1)
    lo = lax.bitcast_convert_type(lax.shift_left(w, 16), f32)
    hi = lax.bitcast_convert_type(w & jnp.int32(-65536), f32)
    return jnp.concatenate([lo, hi], axis=1)


def _router_body(x_ref, rt_ref, xa_ref, xb_ref, meta_ref, cnt_ref, run_ref):
    tb = x_ref.shape[0]

    @pl.when(pl.program_id(0) == 0)
    def _():
        run_ref[...] = jnp.zeros(run_ref.shape, f32)

    x = x_ref[...]
    xa_ref[...], xb_ref[...] = _pack_words(x)
    logits = _sdot3(x, rt_ref[...])
    lane = lax.broadcasted_iota(jnp.int32, logits.shape, 1)
    lg = jnp.where(lane < N_EXPERTS, logits, NEG)
    v1 = jnp.max(lg, axis=1, keepdims=True)
    i1 = jnp.min(jnp.where(lg == v1, lane, LANES), axis=1, keepdims=True)
    lg2 = jnp.where(lane == i1, NEG, lg)
    v2 = jnp.max(lg2, axis=1, keepdims=True)
    i2 = jnp.min(jnp.where(lg2 == v2, lane, LANES), axis=1, keepdims=True)
    e2 = jnp.exp(v2 - v1)
    sel1, sel2 = lane == i1, lane == i2
    mask = (sel1 | sel2).astype(f32)
    r = lax.broadcasted_iota(jnp.int32, (tb, tb), 0)
    c = lax.broadcasted_iota(jnp.int32, (tb, tb), 1)
    rank = _bdot((c < r).astype(f32), mask) + run_ref[0:1, :]
    run_ref[...] = run_ref[...] + jnp.sum(mask, axis=0, keepdims=True)
    rank1 = jnp.sum(jnp.where(sel1, rank, 0.0), axis=1, keepdims=True)
    rank2 = jnp.sum(jnp.where(sel2, rank, 0.0), axis=1, keepdims=True)
    cols = (i1.astype(f32), i2.astype(f32), rank1, rank2, 1.0 / (1.0 + e2), e2 / (1.0 + e2))
    meta = jnp.zeros(logits.shape, f32)
    for n, col in enumerate(cols):
        meta = jnp.where(lane == n, col, meta)
    meta_ref[...] = meta
    cnt_ref[...] = run_ref[...]


def _router(xt, router, tb=1024):
    T = xt.shape[0]
    tb = min(tb, T)
    return pl.pallas_call(
        _router_body, grid=(T // tb,),
        in_specs=[pl.BlockSpec((tb, D_MODEL), lambda i: (i, 0)), pl.BlockSpec((D_MODEL, LANES), lambda i: (0, 0))],
        out_specs=[pl.BlockSpec((tb, SC_WORDS), lambda i: (i, 0)), pl.BlockSpec((tb, SC_WORDS), lambda i: (i, 0)),
                   pl.BlockSpec((tb, LANES), lambda i: (i, 0)), pl.BlockSpec((8, LANES), lambda i: (0, 0))],
        out_shape=[jax.ShapeDtypeStruct((T, SC_WORDS), jnp.int32), jax.ShapeDtypeStruct((T, SC_WORDS), jnp.int32),
                   jax.ShapeDtypeStruct((T, LANES), f32), jax.ShapeDtypeStruct((8, LANES), f32)],
        scratch_shapes=[pltpu.VMEM((8, LANES), f32)],
        compiler_params=_cparams(("arbitrary",)), name="moe_router")(xt, router)


def _sc_gather(table, idx):
    n = idx.shape[0]
    mesh = plsc.VectorSubcoreMesh(core_axis_name="c", subcore_axis_name="s")

    @functools.partial(pl.kernel, out_type=jax.ShapeDtypeStruct((n, SC_WORDS), table.dtype), mesh=mesh)
    def gather(x_hbm, i_hbm, o_hbm):
        def body(i_vmem, o_vmem):
            pltpu.sync_copy(x_hbm.at[i_vmem.at[0]], o_vmem)

        pltpu.emit_pipeline(
            body, grid=(n // SC_WINDOW,),
            in_specs=[pl.BlockSpec((1, SC_WINDOW), index_map=lambda i: (0, i))],
            out_specs=[pl.BlockSpec((SC_WINDOW, SC_WORDS), index_map=lambda i: (i, 0))],
            core_axis_name=("c", "s"), dimension_semantics=(pltpu.PARALLEL,))(i_hbm, o_hbm)

    return gather(table, idx.reshape(1, n))


def _sc_scatter(rows, idx, n_out):
    R = rows.shape[0]
    n = idx.shape[0]
    nblk = R // SC_WINDOW
    mesh = plsc.VectorSubcoreMesh(core_axis_name="c", subcore_axis_name="s")

    @functools.partial(pl.kernel, out_type=jax.ShapeDtypeStruct((n_out, SC_WORDS), rows.dtype), mesh=mesh,
                       scratch_types=[])
    def scatter(x_hbm, i_hbm, o_hbm):
        def body(x_vmem, i_vmem):
            pltpu.sync_copy(x_vmem, o_hbm.at[i_vmem.at[0]])

        pltpu.emit_pipeline(
            body, grid=(n // SC_WINDOW,),
            in_specs=[pl.BlockSpec((SC_WINDOW, SC_WORDS), index_map=lambda i: (i % nblk, 0)),
                      pl.BlockSpec((1, SC_WINDOW), index_map=lambda i: (0, i))],
            out_specs=[], core_axis_name=("c", "s"), dimension_semantics=(pltpu.PARALLEL,))(x_hbm, i_hbm)

    return scatter(rows, idx.reshape(1, n))


def _experts_body(te_ref, rows_ref, xa_ref, xb_ref, w1_ref, w3_ref, w2_ref, oa_ref, ob_ref, acc_ref, x_ref):
    i = pl.program_id(0)
    j = pl.program_id(1)

    @pl.when(j == 0)
    def _():
        valid = lax.broadcasted_iota(jnp.int32, (MOE_TILE, 1), 0) < rows_ref[i]
        x_ref[...] = jnp.where(valid, _unpack_words(xa_ref[...], xb_ref[...]), 0.0).astype(bf16)
        acc_ref[...] = jnp.zeros(acc_ref.shape, f32)

    @pl.when(rows_ref[i] > 0)
    def _():
        x = x_ref[...]
        h1 = jnp.dot(x, w1_ref[0].astype(bf16), preferred_element_type=f32)
        h3 = jnp.dot(x, w3_ref[0].astype(bf16), preferred_element_type=f32)
        acc_ref[...] += _bdot(h1 * _sigmoid(h1) * h3, w2_ref[0])

    @pl.when(j == pl.num_programs(1) - 1)
    def _():
        oa_ref[...], ob_ref[...] = _pack_words(acc_ref[...])


def _experts(xa, xb, tile_expert, tile_rows, w1, w3, w2, tf=512):
    P = xa.shape[0]
    dff = w1.shape[2]
    words = pl.BlockSpec((MOE_TILE, SC_WORDS), lambda i, j, te, nt: (i, 0))
    grid_spec = pltpu.PrefetchScalarGridSpec(
        num_scalar_prefetch=2, grid=(P // MOE_TILE, dff // tf),
        in_specs=[words, words,
                  pl.BlockSpec((1, D_MODEL, tf), lambda i, j, te, nt: (te[i], 0, j)),
                  pl.BlockSpec((1, D_MODEL, tf), lambda i, j, te, nt: (te[i], 0, j)),
                  pl.BlockSpec((1, tf, D_MODEL), lambda i, j, te, nt: (te[i], j, 0))],
        out_specs=[words, words],
        scratch_shapes=[pltpu.VMEM((MOE_TILE, D_MODEL), f32), pltpu.VMEM((MOE_TILE, D_MODEL), bf16)])
    return pl.pallas_call(
        _experts_body, grid_spec=grid_spec, out_shape=[jax.ShapeDtypeStruct((P, SC_WORDS), jnp.int32)] * 2,
        compiler_params=_cparams(("parallel", "arbitrary")), name="moe_experts")(tile_expert, tile_rows, xa, xb, w1, w3, w2)


def _combine_body(x_ref, y0a_ref, y0b_ref, y1a_ref, y1b_ref, meta_ref, ln_ref, o_ref):
    meta = meta_ref[...]
    ff = (meta[:, 4:5] * _unpack_words(y0a_ref[...], y0b_ref[...])
          + meta[:, 5:6] * _unpack_words(y1a_ref[...], y1b_ref[...]))
    o_ref[...] = _layer_norm(ALPHA * x_ref[...] + ff, ln_ref[0:1, :], ln_ref[1:2, :])


def _combine(xt, yga, ygb, meta, ln, tm=1024):
    T = xt.shape[0]
    tm = min(tm, T)
    nb = T // tm
    row = pl.BlockSpec((tm, D_MODEL), lambda i: (i, 0))
    first = pl.BlockSpec((tm, SC_WORDS), lambda i: (i, 0))
    second = pl.BlockSpec((tm, SC_WORDS), lambda i: (nb + i, 0))
    return pl.pallas_call(
        _combine_body, grid=(nb,),
        in_specs=[row, first, first, second, second, pl.BlockSpec((tm, LANES), lambda i: (i, 0)),
                  pl.BlockSpec((2, D_MODEL), lambda i: (0, 0))],
        out_specs=row, out_shape=jax.ShapeDtypeStruct((T, D_MODEL), f32),
        compiler_params=_cparams(("parallel",)), name="moe_combine")(xt, yga, ygb, yga, ygb, meta, ln)


def _moe(xt, router, w1, w3, w2, ln):
    T = xt.shape[0]
    xa, xb, meta, cnt = _router(xt, router)
    counts = cnt[0, :N_EXPERTS].astype(jnp.int32)
    tiles = (counts + MOE_TILE - 1) // MOE_TILE
    tile_end = jnp.cumsum(tiles)
    offset = (tile_end - tiles) * MOE_TILE
    expert = meta[:, 0:2].astype(jnp.int32)
    onehot = expert[:, :, None] == jnp.arange(N_EXPERTS, dtype=jnp.int32)[None, None, :]
    pos = jnp.sum(jnp.where(onehot, offset[None, None, :], 0), axis=2) + meta[:, 2:4].astype(jnp.int32)
    pos = pos.T.reshape(-1)
    P = 2 * T + N_EXPERTS * MOE_TILE
    tile_id = jnp.arange(P // MOE_TILE, dtype=jnp.int32)
    tile_expert = jnp.minimum(jnp.sum((tile_id[:, None] >= tile_end[None, :]).astype(jnp.int32), axis=1), N_EXPERTS - 1)
    first_tile = (tile_end - tiles)[tile_expert]
    tile_rows = jnp.clip(counts[tile_expert] - (tile_id - first_tile) * MOE_TILE, 0, MOE_TILE)
    ya, yb = _experts(_sc_scatter(xa, pos, P), _sc_scatter(xb, pos, P), tile_expert, tile_rows, w1, w3, w2)
    return _combine(xt, _sc_gather(ya, pos), _sc_gather(yb, pos), meta, ln)


def kernel(x, w_in, b_in, att_gq, att_gk, rw_mix, rw_w0, rw_w2, rw_a0, rw_a2, rw_g2, rw_kk, rw_ka, rw_rk, rw_ln_g, rw_ln_b, s5_lam_re, s5_lam_im, s5_log_dt, s5_b_re, s5_b_im, s5_c_re, s5_c_im, s5_d, s5_glu_w, s5_glu_b, ml_conv_w, ml_conv_b, ml_ib, ml_fb, w_gate, b_gate, w_branch, w_out, ln1_g, ln1_b, ffn_w1, ffn_w3, ffn_w2, moe_router, moe_w1, moe_w3, moe_w2, ln2_g, ln2_b):
    B, S, D = x.shape
    assert 2 * B == 8 and D == D_MODEL and S % 512 == 0 and w_in.shape[0] == DEPTH, (x.shape, w_in.shape)
    xt = x.reshape(B * S, D)
    cos, sin = _rope_tables(S)
    for l in range(DEPTH):
        gain = jnp.concatenate([jnp.tile(att_gq[l], 8) * (HEAD_DIM ** -0.5), jnp.tile(att_gk[l], 2)])[None, :]
        (q, k, v, s5u, mq, mk, mv, mo, r, rk, rv, an, bn, lw, gate, bonus, gt) = _proj(
            xt, _proj_params(w_in[l], b_in[l], ml_ib[l], ml_fb[l]), (cos, sin, gain),
            (ml_conv_w[l], ml_conv_b[l][None, :]),
            _rw_params(rw_mix[l], rw_w0[l], rw_w2[l], rw_a0[l], rw_a2[l], rw_g2[l], rw_kk[l], rw_ka[l], rw_rk[l]), B, S)
        score_bound = 8.1 * jnp.max(jnp.abs(att_gq[l])) * jnp.max(jnp.abs(att_gk[l]))
        o_att = _flash(q, k, v, score_bound, B, S)
        yf, yb = _rwkv(r, rk, rv, an, bn, lw, B, S)
        y_s5 = _s5(s5u, _s5_params(s5_lam_re[l], s5_lam_im[l], s5_log_dt[l], s5_b_re[l], s5_b_im[l], s5_c_re[l],
                                   s5_c_im[l]), B, S)
        hf, hb = _mlstm(mq, mk, mv, gt, B, S)
        xt = _merge(xt, o_att, (yf, yb, gate, bonus), (y_s5, s5u), (hf, hb, mo), w_gate[l].astype(bf16), b_gate[l],
                    _att_branch_weight(w_branch[l, 0]).astype(bf16), w_branch[l, 1:].astype(bf16),
                    w_out[l].astype(bf16), jnp.stack([ln1_g[l], ln1_b[l]]), jnp.stack([rw_ln_g[l], rw_ln_b[l]]),
                    jnp.stack([s5_d[l], s5_glu_b[l]]), s5_glu_w[l].astype(bf16))
        ln2 = jnp.stack([ln2_g[l], ln2_b[l]])
        if l % 2 == 0:
            xt = _ffn(xt, ffn_w1[l // 2].astype(bf16), ffn_w3[l // 2].astype(bf16), ffn_w2[l // 2].astype(bf16), ln2)
        else:
            router = jnp.pad(moe_router[l // 2], ((0, 0), (0, LANES - N_EXPERTS)))
            xt = _moe(xt, router, moe_w1[l // 2], moe_w3[l // 2], moe_w2[l // 2], ln2)
    return xt.reshape(B, S, D)
```

```python
import functools
import math

import jax
import jax.numpy as jnp
import numpy as np
from jax import lax
from jax.experimental import pallas as pl
from jax.experimental.pallas import tpu as pltpu
from jax.experimental.pallas import tpu_sc as plsc

f32 = jnp.float32
bf16 = jnp.bfloat16

D_MODEL = 1024
DEPTH = 2
GRID_W = 64
BRANCH_W = 256
HEAD_DIM = 64
ATT_HEADS = 4
ATT_KV_HEADS = 2
ROPE_THETA = 10000.0
QK_EPS = 1e-6
RW_GN_EPS = 64e-5
RW_COLS = 1088
S5_GROUP = 16
S5_GROUPS = 16
S5_STATE = 64
ML_HEADS = 4
N_EXPERTS = 8
ALPHA = (2 * DEPTH) ** 0.25
LN_EPS = 1e-5

LANES = 128
CHUNK = 64
NEG = -1e30
VMEM_LIMIT = 56 * 1024 * 1024

PROJ_SPLITS = (768, 256, 512, 256, 256, RW_COLS)


def _cparams(sem):
    return pltpu.CompilerParams(dimension_semantics=sem, vmem_limit_bytes=VMEM_LIMIT)


def _sigmoid(x):
    return 0.5 * jnp.tanh(0.5 * x) + 0.5


def _dims(a, lhs_c, rhs_c):
    lead = a.ndim - 2
    batch = tuple(range(lead))
    return (((lhs_c + lead,), (rhs_c + lead,)), (batch, batch))


def _bdot(a, b):
    return lax.dot_general(a.astype(bf16), b.astype(bf16), _dims(a, 1, 0), preferred_element_type=f32)


def _bdot_nt(a, b):
    return lax.dot_general(a.astype(bf16), b.astype(bf16), _dims(a, 1, 1), preferred_element_type=f32)


def _bdot_tn(a, b):
    return lax.dot_general(a.astype(bf16), b.astype(bf16), _dims(a, 0, 0), preferred_element_type=f32)


def _split(x):
    hi = x.astype(bf16)
    return hi, (x - hi.astype(f32)).astype(bf16)


def _sdot(a, b, exact):
    dims = _dims(a, 1, 0)
    if exact == "rhs":
        hi, lo = _split(a)
        bb = b.astype(bf16)
        return (lax.dot_general(hi, bb, dims, preferred_element_type=f32)
                + lax.dot_general(lo, bb, dims, preferred_element_type=f32))
    hi, lo = _split(b)
    ab = a.astype(bf16)
    return (lax.dot_general(ab, hi, dims, preferred_element_type=f32)
            + lax.dot_general(ab, lo, dims, preferred_element_type=f32))


def _sdot3(a, b):
    dims = _dims(a, 1, 0)
    ah, al = _split(a)
    bh, bl = _split(b)
    return (lax.dot_general(ah, bh, dims, preferred_element_type=f32)
            + lax.dot_general(ah, bl, dims, preferred_element_type=f32)
            + lax.dot_general(al, bh, dims, preferred_element_type=f32))


def _seg_matrix(n, seg=HEAD_DIM):
    r = lax.broadcasted_iota(jnp.int32, (n, n), 0) // seg
    c = lax.broadcasted_iota(jnp.int32, (n, n), 1) // seg
    return (r == c).astype(f32)


def _layer_norm(y, g, b):
    mu = jnp.mean(y, axis=-1, keepdims=True)
    d = y - mu
    var = jnp.mean(d * d, axis=-1, keepdims=True)
    return d * lax.rsqrt(var + LN_EPS) * g + b


def _row_to_col(row, eye):
    n = eye.shape[-1]
    return jnp.sum(jnp.where(eye, jnp.broadcast_to(row, (row.shape[0], n, n)), 0.0), axis=2, keepdims=True)


def _stack_heads(x):
    h0 = lax.broadcasted_iota(jnp.int32, x.shape, 2) < HEAD_DIM
    return jnp.concatenate([jnp.where(h0, x, 0.0), jnp.where(h0, 0.0, x)], axis=1)


def _proj_body(x_ref, xp_ref, xn_ref, cos_ref, sin_ref, w_ref, b_ref, wg_ref, bg_ref, gain_ref, cw_ref, cb_ref,
               mix_ref, w2_ref, a2_ref, g2_ref, vec_ref,
               q_ref, k_ref, v_ref, s5_ref, mq_ref, mk_ref, mv_ref, mo_ref, r_ref, rk_ref, rv_ref, an_ref, bn_ref,
               lw_ref, gate_ref, bonus_ref, g_ref, mbuf_ref, rbuf_ref):
    tm = x_ref.shape[0]
    xb = x_ref[...].astype(bf16)
    xh = jnp.concatenate([xp_ref[...], x_ref[...], xn_ref[...]], axis=0).astype(bf16)
    offs = np.cumsum((0,) + PROJ_SPLITS)

    def cols(rows, n):
        sl = slice(int(offs[n]), int(offs[n + 1]))
        return jnp.dot(rows, w_ref[:, sl], preferred_element_type=f32) + b_ref[:, sl]

    q_ref[...], k_ref[...], v_ref[...] = _att_prep(cols(xb, 0), cos_ref[...], sin_ref[...], gain_ref[...])
    s5_ref[...] = cols(xb, 1)
    _fill_halo(mbuf_ref, cols(xh, 2))
    mq_ref[...], mk_ref[...] = _ml_prep(mbuf_ref, tm, cw_ref[...], cb_ref[...])
    mv_ref[...] = cols(xb, 3)
    mo_ref[...] = cols(xb, 4)
    _fill_halo(rbuf_ref, cols(xh, 5))
    outs = _rw_prep(rbuf_ref, tm, mix_ref[...], w2_ref[...], a2_ref[...], g2_ref[...], vec_ref[...])
    for o_ref, val in zip((r_ref, rk_ref, rv_ref, an_ref, bn_ref), outs[0:5]):
        o_ref[...] = val
    lw_ref[0], lw_ref[1], gate_ref[...], bonus_ref[...] = outs[5:9]
    g_ref[...] = lax.dot_general(wg_ref[...], xb, (((1,), (1,)), ((), ())), preferred_element_type=f32) + bg_ref[...]


def _proj(xt, proj_prm, att_prm, ml_prm, rw_prm, B, S, tm=512):
    T = B * S
    nb = S // tm
    n_tot = sum(PROJ_SPLITS)
    row = lambda n: pl.BlockSpec((tm, n), lambda b, i: (b * nb + i, 0))
    const = lambda a: pl.BlockSpec(a.shape, lambda b, i: (0,) * a.ndim)
    tab = pl.BlockSpec((tm, LANES), lambda b, i: (i, 0))
    f32out = lambda n: jax.ShapeDtypeStruct((T, n), f32)
    consts = list(proj_prm) + [att_prm[2]] + list(ml_prm) + list(rw_prm)
    out_specs = ([row(512), row(LANES), row(2 * LANES)] + [row(BRANCH_W)] * 10
                 + [pl.BlockSpec((2, tm, BRANCH_W), lambda b, i: (0, b * nb + i, 0)), row(BRANCH_W), row(BRANCH_W),
                    pl.BlockSpec((16, tm), lambda b, i: (0, b * nb + i))])
    out_shape = ([jax.ShapeDtypeStruct((T, 512), bf16), jax.ShapeDtypeStruct((T, LANES), bf16),
                  jax.ShapeDtypeStruct((T, 2 * LANES), bf16)] + [f32out(BRANCH_W)] * 10
                 + [jax.ShapeDtypeStruct((2, T, BRANCH_W), f32), f32out(BRANCH_W), f32out(BRANCH_W),
                    jax.ShapeDtypeStruct((16, T), f32)])
    return pl.pallas_call(
        _proj_body, grid=(B, nb),
        in_specs=_halo_specs(D_MODEL, tm, B, S) + [tab, tab] + [const(a) for a in consts],
        out_specs=out_specs, out_shape=out_shape,
        scratch_shapes=[pltpu.VMEM((tm + 16, 512), f32), pltpu.VMEM((tm + 16, RW_COLS), f32)],
        compiler_params=_cparams(("parallel", "parallel")), name="proj")(
            xt, xt, xt, att_prm[0], att_prm[1], *consts)


def _proj_params(w_in, b_in, ml_ib, ml_fb):
    o = np.cumsum((0, 256, 128, 128, RW_COLS, 256, 512, 256, 8, 8, 256))
    sl = lambda i: (w_in[:, o[i]:o[i + 1]], b_in[o[i]:o[i + 1]])
    (wq, bq), (wk, bk), (wv, bv), (wrw, brw), (ws5, bs5), (wqk, bqk), (wmv, bmv), (wi, bi), (wf, bf), (wo, bo) = (
        sl(i) for i in range(10))
    zw, zb = jnp.zeros((D_MODEL, HEAD_DIM), f32), jnp.zeros((HEAD_DIM,), f32)
    wq_e, bq_e = [], []
    for h in range(ATT_HEADS):
        wh, bh = wq[:, 64 * h:64 * h + 64], bq[64 * h:64 * h + 64]
        wq_e += [wh, zw] if h // 2 == 0 else [zw, wh]
        bq_e += [bh, zb] if h // 2 == 0 else [zb, bh]
    w = jnp.concatenate(wq_e + [wk, wv, ws5, wqk, wmv, wo, wrw], axis=1)
    b = jnp.concatenate(bq_e + [bk, bv, bs5, bqk, bmv, bo, brw])
    wg = jnp.concatenate([wi, wf], axis=1).T
    bg = jnp.concatenate([bi + ml_ib.reshape(-1), bf + ml_fb.reshape(-1)])
    return w.astype(bf16), b[None, :], wg.astype(bf16), bg[:, None]


def _att_prep(att, cos, sin, gain):
    x = att[:, 0:640]
    ms = _sdot(x * x, _seg_matrix(640), "rhs") * (1.0 / HEAD_DIM)
    xn = x * lax.rsqrt(ms + QK_EPS) * gain
    lane = lax.broadcasted_iota(jnp.int32, xn.shape, 1)
    partner = jnp.where((lane % 32) < 16, pltpu.roll(xn, 640 - 16, 1), pltpu.roll(xn, 16, 1))
    rot = xn * jnp.concatenate([cos] * 5, axis=1) + partner * jnp.concatenate([sin] * 5, axis=1)
    v = jnp.concatenate([att[:, 640:768].astype(bf16), jnp.ones((x.shape[0], LANES), bf16)], axis=1)
    return rot[:, 0:512].astype(bf16), rot[:, 512:640].astype(bf16), v


def _rope_tables(S):
    t = np.arange(S)
    row = (t // GRID_W).astype(np.float32)
    col = (t % GRID_W).astype(np.float32)
    n = 16
    inv = np.power(np.float32(ROPE_THETA), -np.arange(n, dtype=np.float32) / n).astype(np.float32)
    ar = jnp.asarray(row)[:, None] * jnp.asarray(inv)
    ac = jnp.asarray(col)[:, None] * jnp.asarray(inv)
    cos = jnp.concatenate([jnp.cos(ar), jnp.cos(ar), jnp.cos(ac), jnp.cos(ac)], axis=1)
    sin = jnp.concatenate([-jnp.sin(ar), jnp.sin(ar), -jnp.sin(ac), jnp.sin(ac)], axis=1)
    return jnp.concatenate([cos, cos], axis=1), jnp.concatenate([sin, sin], axis=1)


def _flash_body(q_ref, k_ref, v_ref, o_ref, acc_ref, *m_scratch, tk, track_max):
    tq = q_ref.shape[0]
    nk = k_ref.shape[0] // tk
    q2 = jnp.concatenate([q_ref[:, 0:LANES], q_ref[:, LANES:2 * LANES]], axis=0)
    acc_ref[...] = jnp.zeros(acc_ref.shape, f32)
    if track_max:
        m_ref, = m_scratch
        m_ref[...] = jnp.full(m_ref.shape, NEG, f32)

    def step(j, carry):
        rows = pl.ds(pl.multiple_of(j * tk, tk), tk)
        s = lax.dot_general(q2, k_ref[rows, :], (((1,), (1,)), ((), ())), preferred_element_type=f32)
        if track_max:
            m_old = m_ref[...]
            m_new = jnp.maximum(m_old, jnp.max(s, axis=1, keepdims=True))
            p = jnp.exp(s - m_new).astype(bf16)
            acc_ref[...] = jnp.exp(m_old - m_new) * acc_ref[...] + jnp.dot(p, v_ref[rows, :], preferred_element_type=f32)
            m_ref[...] = m_new
        else:
            acc_ref[...] += jnp.dot(jnp.exp(s).astype(bf16), v_ref[rows, :], preferred_element_type=f32)
        return carry

    lax.fori_loop(0, nk, step, 0)
    o = acc_ref[:, 0:LANES] / acc_ref[:, LANES:2 * LANES]
    o_ref[...] = jnp.concatenate([o[0:tq], o[tq:2 * tq]], axis=1)


SCORE_BOUND_MAX = 60.0


def _flash(q, k, v, score_bound, B, S, tq=256, tk=8192):
    T = B * S
    nb = S // tq
    tk = min(tk, S)

    def call(track_max):
        scratch = [pltpu.VMEM((2 * tq, 2 * LANES), f32)] + ([pltpu.VMEM((2 * tq, 1), f32)] if track_max else [])
        return pl.pallas_call(
            functools.partial(_flash_body, tk=tk, track_max=track_max), grid=(B, ATT_KV_HEADS, nb),
            in_specs=[pl.BlockSpec((tq, 2 * LANES), lambda b, g, i: (b * nb + i, g)),
                      pl.BlockSpec((S, LANES), lambda b, g, i: (b, 0)),
                      pl.BlockSpec((S, 2 * LANES), lambda b, g, i: (b, 0))],
            out_specs=pl.BlockSpec((tq, 2 * LANES), lambda b, g, i: (b * nb + i, g)),
            out_shape=jax.ShapeDtypeStruct((T, 512), f32), scratch_shapes=scratch,
            compiler_params=_cparams(("parallel", "parallel", "parallel")),
            name="flash_safe" if track_max else "flash")(q, k, v)

    return lax.cond(score_bound <= SCORE_BOUND_MAX, lambda: call(False), lambda: call(True))


def _halo_specs(width, tm, B, S):
    nb = S // tm
    r8 = tm // 8
    last8 = B * S // 8 - 1

    def main(b, i):
        return (b * nb + i, 0)

    def prev(b, i):
        return (jnp.maximum(b * (S // 8) + i * r8 - 1, 0), 0)

    def nxt(b, i):
        return (jnp.minimum(b * (S // 8) + (i + 1) * r8, last8), 0)

    return [pl.BlockSpec((tm, width), main), pl.BlockSpec((8, width), prev), pl.BlockSpec((8, width), nxt)]


def _fill_halo(buf_ref, xh):
    tm = xh.shape[0] - 16
    i = pl.program_id(1)
    last = pl.num_programs(1) - 1
    buf_ref[...] = xh
    buf_ref[pl.ds(0, 8), :] = jnp.where(i > 0, xh[0:8], 0.0)
    buf_ref[pl.ds(8 + tm, 8), :] = jnp.where(i < last, xh[8 + tm:16 + tm], 0.0)


def _rw_prep(buf_ref, tm, mix, w2, a2, g2, vec):
    x = buf_ref[pl.ds(8, tm), :]
    p = x + mix[0:1, :] * (buf_ref[pl.ds(7, tm), :] - x) + mix[1:2, :] * (buf_ref[pl.ds(9, tm), :] - x)
    r, k, v = p[:, 0:256], p[:, 256:512], p[:, 512:768]
    w0f, w0b, a0, k_k, k_a, r_k = (vec[j:j + 1, :] for j in range(6))
    dec = _bdot(jnp.tanh(p[:, 768:896]), w2)
    z = p[:, 896:1088]
    a = _sigmoid(a0 + _bdot(z, a2))
    gate = _bdot(_sigmoid(z), g2)
    seg = _seg_matrix(BRANCH_W)
    kk = k * k_k
    kk = kk * jnp.minimum(lax.rsqrt(_sdot(kk * kk, seg, "rhs")), 1e12)
    k2 = k * (1.0 + (a - 1.0) * k_a)
    bonus = _sdot(r * k2 * r_k, seg, "rhs") * v
    lwf = -math.exp(-0.5) * _sigmoid(w0f + dec[:, 0:256])
    lwb = -math.exp(-0.5) * _sigmoid(w0b + dec[:, 256:512])
    return r, k2, v, -kk, kk * a, lwf, lwb, gate, bonus


def _by_direction(x, fwd, bwd, fn):
    h = x.shape[0] // 2
    return jnp.concatenate([fn(x[0:h], fwd), fn(x[h:2 * h], bwd)], axis=0)


def _keep(x, masks):
    return _by_direction(x, masks[0], masks[1], lambda t, m: jnp.where(m, t, 0.0))


def _both(masks, nchain):
    h = nchain // 2
    return jnp.concatenate([jnp.broadcast_to(m.astype(f32), (h,) + m.shape[1:]) for m in masks], axis=0)


def _pair_masks():
    n = 2 * CHUNK
    r = lax.broadcasted_iota(jnp.int32, (1, n, n), 1)
    c = lax.broadcasted_iota(jnp.int32, (1, n, n), 2)
    same = (r // CHUNK) == (c // CHUNK)
    return r, c, same, (same & (c < r), same & (c > r)), (same & (c <= r), same & (c >= r))


def _rw_chunk(st, r, k, v, an, bn, lw):
    L = CHUNK
    N = r.shape[0]
    ri = lax.broadcasted_iota(jnp.int32, (1, L, L), 1)
    ci = lax.broadcasted_iota(jnp.int32, (1, L, L), 2)
    cs = _sdot(_both((ci <= ri, ci >= ri), N), lw, "lhs")
    tot = jnp.sum(lw, axis=1, keepdims=True)
    e_neg = jnp.exp(-cs)
    at = an * jnp.exp(cs - lw)
    rt = r * jnp.exp(cs)
    a2, b2, k2, v2 = _stack_heads(at), _stack_heads(bn * e_neg), _stack_heads(k * e_neg), _stack_heads(v)
    t = lax.broadcasted_iota(jnp.int32, (1, L, LANES), 1)
    tc = lax.broadcasted_iota(jnp.int32, (1, L, LANES), 2) % L
    strict, incl = (tc < t, tc > t), (tc <= t, tc >= t)
    g = _bdot_nt(jnp.concatenate([at, rt], axis=1), jnp.concatenate([b2, k2], axis=1))
    mab = _keep(g[:, 0:L, 0:LANES], strict)
    mak = _keep(g[:, 0:L, LANES:2 * LANES], strict)
    pb = _keep(g[:, L:2 * L, 0:LANES], incl)
    pk = _keep(g[:, L:2 * L, LANES:2 * LANES], incl)
    mul = lambda p, q: _bdot(p, _stack_heads(q))
    m8 = jnp.where((t // 8) == (tc // 8), mab, 0.0)
    x = (t == tc).astype(f32) + m8
    p = mul(m8, m8)
    x = x + mul(x, p)
    p = mul(p, p)
    x = x + mul(x, p)
    n = 8
    while n < L:
        e = jnp.where(((t // (2 * n)) == (tc // (2 * n))) & ((t // n) != (tc // n)), mab, 0.0)
        x = x + mul(mul(x, e), x)
        n *= 2
    wu = _bdot(x, jnp.concatenate([a2, _stack_heads(_bdot(mak, v2))], axis=2))
    wu2 = jnp.concatenate([_stack_heads(wu[:, :, 0:LANES]), _stack_heads(wu[:, :, LANES:2 * LANES])], axis=2)
    pwu = _bdot(pb, wu2)
    rh = rt + pwu[:, :, 0:LANES]
    y = pwu[:, :, LANES:2 * LANES] + _bdot(jnp.concatenate([pk, rh], axis=2), jnp.concatenate([v2, st], axis=1))
    r128 = lax.broadcasted_iota(jnp.int32, (1, LANES, LANES), 1)
    c128 = lax.broadcasted_iota(jnp.int32, (1, LANES, LANES), 2)
    gam = _row_to_col(jnp.exp(tot), r128 == c128)
    bwu = _bdot_tn(b2, wu2)
    st = gam * (st + _bdot(bwu[:, :, 0:LANES], st) + bwu[:, :, LANES:2 * LANES] + _bdot_tn(k2, v2))
    return y, st


def _load_pairs(ref, rows):
    return jnp.concatenate([ref[:, rows, 0:LANES], ref[:, rows, LANES:2 * LANES]], axis=0)


def _store_pairs(ref, rows, y):
    nb = ref.shape[0]
    ref[:, rows, 0:LANES] = y[0:nb]
    ref[:, rows, LANES:2 * LANES] = y[nb:2 * nb]


def _load_both(f_ref, b_ref, rows_f, rows_b):
    return jnp.concatenate([_load_pairs(f_ref, rows_f), _load_pairs(b_ref, rows_b)], axis=0)


def _chunk_rows(cc, nch):
    return (pl.ds(pl.multiple_of(cc * CHUNK, CHUNK), CHUNK),
            pl.ds(pl.multiple_of((nch - 1 - cc) * CHUNK, CHUNK), CHUNK))


def _rw_scan_body(rf_ref, rb_ref, kf_ref, kb_ref, vf_ref, vb_ref, anf_ref, anb_ref, bnf_ref, bnb_ref, lwf_ref, lwb_ref,
                  yf_ref, yb_ref, st_ref):
    nch = rf_ref.shape[1] // CHUNK
    half = st_ref.shape[0] // 2

    @pl.when(pl.program_id(0) == 0)
    def _():
        st_ref[...] = jnp.zeros(st_ref.shape, f32)

    def step(cc, carry):
        rows_f, rows_b = _chunk_rows(cc, nch)
        pairs = ((rf_ref, rb_ref), (kf_ref, kb_ref), (vf_ref, vb_ref), (anf_ref, anb_ref), (bnf_ref, bnb_ref),
                 (lwf_ref.at[0], lwb_ref.at[0]))
        y, st = _rw_chunk(st_ref[...], *(_load_both(f, b, rows_f, rows_b) for f, b in pairs))
        _store_pairs(yf_ref, rows_f, y[0:half])
        _store_pairs(yb_ref, rows_b, y[half:2 * half])
        st_ref[...] = st
        return carry

    lax.fori_loop(0, nch, step, 0)


def _rw_scan(r, k, v, an, bn, lw, B, S, ts=256):
    nb = S // ts
    fwd = pl.BlockSpec((B, ts, BRANCH_W), lambda i: (0, i, 0))
    bwd = pl.BlockSpec((B, ts, BRANCH_W), lambda i: (0, nb - 1 - i, 0))
    out = jax.ShapeDtypeStruct((B, S, BRANCH_W), f32)
    return pl.pallas_call(
        _rw_scan_body, grid=(nb,),
        in_specs=[fwd, bwd] * 5 + [pl.BlockSpec((1, B, ts, BRANCH_W), lambda i: (0, 0, i, 0)),
                                   pl.BlockSpec((1, B, ts, BRANCH_W), lambda i: (1, 0, nb - 1 - i, 0))],
        out_specs=[fwd, bwd], out_shape=[out, out],
        scratch_shapes=[pltpu.VMEM((2 * B * BRANCH_W // LANES, LANES, LANES), f32)],
        compiler_params=_cparams(("arbitrary",)), name="rw_scan")(r, r, k, k, v, v, an, an, bn, bn, lw, lw)


def _rw_finish(y, gate, bonus, gn):
    seg = _seg_matrix(BRANCH_W)
    mu = _sdot(y, seg, "rhs") * (1.0 / HEAD_DIM)
    d = y - mu
    var = _sdot(d * d, seg, "rhs") * (1.0 / HEAD_DIM)
    yn = d * lax.rsqrt(var + RW_GN_EPS) * gn[0:1, :] + gn[1:2, :]
    return (yn + bonus) * gate


def _rwkv(r, k, v, an, bn, lw, B, S):
    seq = [t.reshape(B, S, BRANCH_W) for t in (r, k, v, an, bn)] + [lw.reshape(2, B, S, BRANCH_W)]
    yf, yb = _rw_scan(*seq, B, S)
    return yf.reshape(B * S, BRANCH_W), yb.reshape(B * S, BRANCH_W)


def _rw_params(mix, w0, w2, a0, a2, g2, k_k, k_a, r_k):
    z = jnp.zeros((64, 256), f32)
    w2c = jnp.concatenate([jnp.concatenate([w2[0], z], axis=1), jnp.concatenate([z, w2[1]], axis=1)], axis=0)
    a2p = jnp.concatenate([a2, jnp.zeros((128, 256), f32)], axis=0)
    g2p = jnp.concatenate([jnp.zeros((64, 256), f32), g2], axis=0)
    vec = jnp.stack([w0[0], w0[1], a0, k_k, k_a, r_k.reshape(-1), jnp.zeros_like(a0), jnp.zeros_like(a0)])
    return mix, w2c.astype(bf16), a2p.astype(bf16), g2p.astype(bf16), vec


S5_HALF = S5_GROUPS * S5_STATE // 2


def _s5_scan_body(uf_ref, ub_ref, bh_ref, cre_ref, cim_ref, lre_ref, lim_ref, yf_ref, yb_ref,
                  sre_ref, sim_ref, bre_ref, bim_ref):
    ts = uf_ref.shape[0]
    n = S5_HALF

    @pl.when(pl.program_id(0) == 0)
    def _():
        sre_ref[...] = jnp.zeros(sre_ref.shape, f32)
        sim_ref[...] = jnp.zeros(sim_ref.shape, f32)

    row_half = lax.broadcasted_iota(jnp.int32, uf_ref.shape, 1) // 4
    lane_half = lax.broadcasted_iota(jnp.int32, uf_ref.shape, 2) // (BRANCH_W // 2)
    for d, u_ref in enumerate((uf_ref, ub_ref)):
        lhs = jnp.where(row_half == lane_half, u_ref[...], 0.0).reshape(ts * 8, BRANCH_W)
        bu = _bdot(lhs, bh_ref[d]).reshape(ts, 8, 2 * n)
        bre_ref[d] = bu[:, :, 0:n]
        bim_ref[d] = bu[:, :, n:2 * n]
    lre = lre_ref[...]
    lim = lim_ref[...]

    def step(t, carry):
        out = []
        for d, td in enumerate((t, ts - 1 - t)):
            sre, sim = carry[2 * d], carry[2 * d + 1]
            nre = lre[d] * sre - lim[d] * sim + bre_ref[d, td]
            nim = lre[d] * sim + lim[d] * sre + bim_ref[d, td]
            bre_ref[d, td] = nre
            bim_ref[d, td] = nim
            out += [nre, nim]
        return tuple(out)

    s = lax.fori_loop(0, ts, step, (sre_ref[0], sim_ref[0], sre_ref[1], sim_ref[1]), unroll=4)
    sre_ref[0], sim_ref[0], sre_ref[1], sim_ref[1] = s
    low = lane_half == 0
    for d, y_ref in enumerate((yf_ref, yb_ref)):
        yv = (_bdot(bre_ref[d].reshape(ts * 8, n), cre_ref[d])
              - _bdot(bim_ref[d].reshape(ts * 8, n), cim_ref[d])).reshape(ts, 8, BRANCH_W)
        y_ref[...] = jnp.where(low, yv, pltpu.roll(yv, 4, 1))


def _s5_scan(u8, bh, cre, cim, lre, lim, ts=128):
    S = u8.shape[0]
    nb = S // ts
    n = S5_HALF
    full = lambda shape: pl.BlockSpec(shape, lambda i: (0,) * len(shape))
    fwd = pl.BlockSpec((ts, 8, BRANCH_W), lambda i: (i, 0, 0))
    bwd = pl.BlockSpec((ts, 8, BRANCH_W), lambda i: (nb - 1 - i, 0, 0))
    out = jax.ShapeDtypeStruct((S, 8, BRANCH_W), f32)
    return pl.pallas_call(
        _s5_scan_body, grid=(nb,),
        in_specs=[fwd, bwd, full((2, BRANCH_W, 2 * n)), full((2, n, BRANCH_W)), full((2, n, BRANCH_W)),
                  full((2, 8, n)), full((2, 8, n))],
        out_specs=[fwd, bwd], out_shape=[out, out],
        scratch_shapes=[pltpu.VMEM((2, 8, n), f32), pltpu.VMEM((2, 8, n), f32),
                        pltpu.VMEM((2, ts, 8, n), f32), pltpu.VMEM((2, ts, 8, n), f32)],
        compiler_params=_cparams(("arbitrary",)), name="s5_scan")(u8, u8, bh, cre, cim, lre, lim)


def _s5_finish(y, u, vec, w):
    y = y + u * vec[0:1, :]
    y = 0.5 * y * (1.0 + jnp.tanh(math.sqrt(2.0 / math.pi) * (y + 0.044715 * (y * y * y))))
    return y * _sigmoid(_bdot(y, w) + vec[1:2, :])


def _s5_params(lam_re, lam_im, log_dt, b_re, b_im, c_re, c_im):
    G, P, C = S5_GROUPS, S5_STATE, S5_GROUP
    H = G // 2
    eye = jnp.eye(H, dtype=f32)
    b_c = lax.complex(b_re, b_im)

    def b_part(x):
        return jnp.einsum('ab,hapc->hacbp', eye, x.reshape(2, H, P, C)).reshape(G * C, H * P)

    def c_part(x):
        return jnp.einsum('ab,hacp->aphbc', eye, x.reshape(2, H, C, P)).reshape(H * P, G * C)

    def rows(x):
        return jnp.broadcast_to(x.reshape(2, 1, H * P), (2, 4, H * P)).reshape(8, H * P)

    bh, cre, cim, lre, lim = [], [], [], [], []
    for d in range(2):
        lam = lax.complex(jnp.minimum(lam_re[d], -1e-4), lam_im[d])
        lam_bar = jnp.exp(lam * jnp.exp(log_dt[d])[:, None])
        b_bar = ((lam_bar - 1.0) / lam)[..., None] * b_c
        bh.append(jnp.concatenate([b_part(jnp.real(b_bar)), b_part(jnp.imag(b_bar))], axis=1))
        cre.append(c_part(c_re[d]))
        cim.append(c_part(c_im[d]))
        lre.append(rows(jnp.real(lam_bar)))
        lim.append(rows(jnp.imag(lam_bar)))
    return (jnp.stack(bh).astype(bf16), jnp.stack(cre).astype(bf16), jnp.stack(cim).astype(bf16),
            jnp.stack(lre), jnp.stack(lim))


def _s5(u, prm, B, S):
    u3 = u.reshape(B, S, BRANCH_W).transpose(1, 0, 2)
    yf, yb = _s5_scan(jnp.concatenate([u3, u3], axis=1), *prm)
    return (yf[:, 0:B] + yb[:, 0:B]).transpose(1, 0, 2).reshape(B * S, BRANCH_W)


def _ml_prep(buf_ref, tm, w, b):
    y = b
    for j in range(5):
        y = y + w[j:j + 1, :] * buf_ref[pl.ds(6 + j, tm), :]
    y = y * _sigmoid(y)
    return y[:, 0:BRANCH_W], y[:, BRANCH_W:2 * BRANCH_W] * (HEAD_DIM ** -0.5)


def _ml_chunk(state, q, k, v, li, lfp):
    cn, m_row = state
    L = CHUNK
    N = q.shape[0]
    rr, cc, same, _, incl2 = _pair_masks()
    same_f = _both((same, same), N)
    lane = lax.broadcasted_iota(jnp.int32, (1, 1, 2 * L), 2)
    lf = jnp.minimum(lfp, 0.0) - jnp.log(1.0 + jnp.exp(-jnp.abs(lfp)))
    lf8 = jnp.broadcast_to(lf, (N, 8, 2 * L))
    b_row = _sdot(lf8, _both((incl2[1], incl2[0]), N), "rhs")[:, 0:1]
    g_row = _sdot(lf8, same_f, "rhs")[:, 0:1]
    w_end = g_row - b_row + li
    m0 = jnp.max(jnp.where(lane < L, w_end, NEG), axis=2, keepdims=True)
    m1 = jnp.max(jnp.where(lane < L, NEG, w_end), axis=2, keepdims=True)
    m_loc = jnp.where(lane < L, m0, m1)
    t = lax.broadcasted_iota(jnp.int32, (1, L, LANES), 1)
    tc = lax.broadcasted_iota(jnp.int32, (1, L, LANES), 2) % L
    diag = t == tc

    def cols(row):
        return _sdot(jnp.where(diag, jnp.broadcast_to(row, (N, L, LANES)), 0.0), same_f, "rhs")

    e_col, b_col = cols(jnp.exp(w_end - m_loc)), cols(b_row)
    k2, v2 = _stack_heads(k), _stack_heads(v)
    v1 = jnp.concatenate([v2, _stack_heads(jnp.ones_like(v))], axis=2)
    log_inter = b_col + m_row
    log_intra = _by_direction(b_col - b_row + li, tc <= t, tc >= t, lambda x, m: jnp.where(m, x, NEG))
    head0 = lane < L
    r0 = jnp.max(jnp.where(head0, log_intra, NEG), axis=2, keepdims=True)
    r1 = jnp.max(jnp.where(head0, NEG, log_intra), axis=2, keepdims=True)
    m_r = jnp.maximum(log_inter, jnp.where(head0, r0, r1))
    s = _bdot_nt(q, k2) * jnp.exp(log_intra - m_r)
    inter = jnp.exp(log_inter - m_r)
    nd = _bdot(s, v1) + jnp.concatenate([inter, inter], axis=2) * _bdot(q, cn)
    h = nd[:, :, 0:LANES] / jnp.maximum(jnp.abs(nd[:, :, LANES:2 * LANES]), jnp.exp(-m_r))
    m_new = jnp.maximum(g_row + m_row, m_loc)
    a = jnp.exp(g_row + m_row - m_new)
    bb = jnp.exp(m_loc - m_new)
    cn = (jnp.concatenate([a, a], axis=2) * cn
          + jnp.concatenate([bb, bb], axis=2) * _bdot_tn(_stack_heads(e_col * k), v1))
    return h, (cn, m_new)


GATE_ROWS = 8


def _ml_scan_body(qf_ref, qb_ref, kf_ref, kb_ref, vf_ref, vb_ref, gf_ref, gb_ref, hf_ref, hb_ref, c_ref, m_ref):
    nch = qf_ref.shape[1] // CHUNK
    half = c_ref.shape[0] // 2
    i = pl.program_id(0)
    nb = pl.num_programs(0)
    per = GATE_ROWS // nch
    base_f = (i % per) * nch
    base_b = ((nb - 1 - i) % per) * nch

    @pl.when(i == 0)
    def _():
        c_ref[...] = jnp.zeros(c_ref.shape, f32)
        m_ref[...] = jnp.zeros(m_ref.shape, f32)

    def step(cc, carry):
        rows_f, rows_b = _chunk_rows(cc, nch)
        gate = lambda t: jnp.concatenate(
            [gf_ref[t, 0, 0, :, pl.ds(base_f + cc, 1), :], gf_ref[t, 0, 1, :, pl.ds(base_f + cc, 1), :],
             gb_ref[t, 0, 0, :, pl.ds(base_b + nch - 1 - cc, 1), :], gb_ref[t, 0, 1, :, pl.ds(base_b + nch - 1 - cc, 1), :]],
            axis=0)
        h, (cn, m_row) = _ml_chunk((c_ref[...], m_ref[...]), _load_both(qf_ref, qb_ref, rows_f, rows_b),
                                   _load_both(kf_ref, kb_ref, rows_f, rows_b), _load_both(vf_ref, vb_ref, rows_f, rows_b),
                                   gate(0), gate(1))
        _store_pairs(hf_ref, rows_f, h[0:half])
        _store_pairs(hb_ref, rows_b, h[half:2 * half])
        c_ref[...] = cn
        m_ref[...] = m_row
        return carry

    lax.fori_loop(0, nch, step, 0)


def _ml_scan(q, k, v, g, B, S, ts=512):
    nb = S // ts
    per = GATE_ROWS * CHUNK // ts
    nchain = 2 * B * BRANCH_W // LANES
    fwd = pl.BlockSpec((B, ts, BRANCH_W), lambda i: (0, i, 0))
    bwd = pl.BlockSpec((B, ts, BRANCH_W), lambda i: (0, nb - 1 - i, 0))
    out = jax.ShapeDtypeStruct((B, S, BRANCH_W), f32)
    return pl.pallas_call(
        _ml_scan_body, grid=(nb,),
        in_specs=[fwd, bwd] * 3 + [pl.BlockSpec((2, 1, 2, B, GATE_ROWS, LANES), lambda i: (0, 0, 0, 0, i // per, 0)),
                                   pl.BlockSpec((2, 1, 2, B, GATE_ROWS, LANES),
                                                lambda i: (0, 1, 0, 0, (nb - 1 - i) // per, 0))],
        out_specs=[fwd, bwd], out_shape=[out, out],
        scratch_shapes=[pltpu.VMEM((nchain, LANES, 2 * LANES), f32), pltpu.VMEM((nchain, 1, LANES), f32)],
        compiler_params=_cparams(("arbitrary",)), name="ml_scan")(q, q, k, k, v, v, g, g)


def _mlstm(q, k, mv, gt, B, S):
    g = gt.reshape(2, 2, 2, 2, B, S // CHUNK, CHUNK).transpose(0, 1, 2, 4, 5, 3, 6).reshape(2, 2, 2, B, S // CHUNK, LANES)
    hf, hb = _ml_scan(*(t.reshape(B, S, BRANCH_W) for t in (q, k, mv)), g, B, S)
    return hf.reshape(B * S, BRANCH_W), hb.reshape(B * S, BRANCH_W)


def _merge_body(x_ref, att_ref, yf_ref, yb_ref, rg_ref, rb_ref, sy_ref, su_ref, hf_ref, hb_ref, mo_ref,
                wg_ref, bg_ref, wba_ref, wb_ref, wo_ref, ln_ref, gn_ref, sv_ref, sw_ref, o_ref):
    x = x_ref[...]
    xb = x.astype(bf16)
    rw = _rw_finish(yf_ref[...] + yb_ref[...], rg_ref[...], rb_ref[...], gn_ref[...])
    s5 = _s5_finish(sy_ref[...], su_ref[...], sv_ref[...], sw_ref[...])
    ml = _sigmoid(mo_ref[...]) * (hf_ref[...] + hb_ref[...])
    branches = (att_ref[...], rw, s5, ml)
    merged = None
    for n in range(4):
        gate = _sigmoid(jnp.dot(xb, wg_ref[n], preferred_element_type=f32) + bg_ref[n:n + 1, :])
        wide = _bdot(branches[n], wba_ref[...] if n == 0 else wb_ref[n - 1])
        merged = gate * wide if merged is None else merged + gate * wide
    y = ALPHA * x + _bdot(merged, wo_ref[...])
    o_ref[...] = _layer_norm(y, ln_ref[0:1, :], ln_ref[1:2, :])


def _merge(xt, att, rw_parts, s5_parts, ml_parts, wg, bg, wba, wb, wo, ln, gn, s5_vec, s5_w, tm=256):
    T = xt.shape[0]
    row = lambda n: pl.BlockSpec((tm, n), lambda i: (i, 0))
    const = lambda shape: pl.BlockSpec(shape, lambda i: (0,) * len(shape), pipeline_mode=pl.Buffered(1))
    return pl.pallas_call(
        _merge_body, grid=(T // tm,),
        in_specs=[row(D_MODEL), row(512)] + [row(BRANCH_W)] * 9
        + [const((4, D_MODEL, D_MODEL)), const((4, D_MODEL)), const((512, D_MODEL)), const((3, BRANCH_W, D_MODEL)),
           const((D_MODEL, D_MODEL)), const((2, D_MODEL)), const((2, BRANCH_W)), const((2, BRANCH_W)),
           const((BRANCH_W, BRANCH_W))],
        out_specs=row(D_MODEL), out_shape=jax.ShapeDtypeStruct((T, D_MODEL), f32),
        compiler_params=_cparams(("parallel",)), name="merge")(
            xt, att, *rw_parts, *s5_parts, *ml_parts, wg, bg, wba, wb, wo, ln, gn, s5_vec, s5_w)


def _att_branch_weight(wb):
    z = jnp.zeros((HEAD_DIM, D_MODEL), f32)
    parts = []
    for h in range(ATT_HEADS):
        wh = wb[64 * h:64 * h + 64]
        parts += [wh, z] if h // 2 == 0 else [z, wh]
    return jnp.concatenate(parts, axis=0)


def _ffn_body(x_ref, w1_ref, w3_ref, w2_ref, ln_ref, o_ref):
    x = x_ref[...]
    xb = x.astype(bf16)
    h1 = jnp.dot(xb, w1_ref[...], preferred_element_type=f32)
    h3 = jnp.dot(xb, w3_ref[...], preferred_element_type=f32)
    ff = _bdot(h1 * _sigmoid(h1) * h3, w2_ref[...])
    o_ref[...] = _layer_norm(ALPHA * x + ff, ln_ref[0:1, :], ln_ref[1:2, :])


def _ffn(xt, w1, w3, w2, ln, tm=512):
    T = xt.shape[0]
    row = pl.BlockSpec((tm, D_MODEL), lambda i: (i, 0))
    const = lambda a: pl.BlockSpec(a.shape, lambda i: (0,) * a.ndim, pipeline_mode=pl.Buffered(1))
    return pl.pallas_call(
        _ffn_body, grid=(T // tm,), in_specs=[row, const(w1), const(w3), const(w2), const(ln)],
        out_specs=row, out_shape=jax.ShapeDtypeStruct((T, D_MODEL), f32),
        compiler_params=_cparams(("parallel",)), name="ffn")(xt, w1, w3, w2, ln)


MOE_TILE = 1024
SC_WINDOW = 128
SC_WORDS = 256


def _pack_words(x):
    bits = lax.bitcast_convert_type(x.astype(bf16).astype(f32), jnp.int32)
    half = D_MODEL // 2
    w = lax.shift_right_logical(bits[:, 0:half], 16) | bits[:, half:D_MODEL]
    return w[:, 0:SC_WORDS], w[:, SC_WORDS:2 * SC_WORDS]


def _unpack_words(wa, wb):
    w = jnp.concatenate([wa, wb], axis=1)
    lo = lax.bitcast_convert_type(lax.shift_left(w, 16), f32)
    hi = lax.bitcast_convert_type(w & jnp.int32(-65536), f32)
    return jnp.concatenate([lo, hi], axis=1)


def _router_body(x_ref, rt_ref, xa_ref, xb_ref, meta_ref, cnt_ref, run_ref):
    tb = x_ref.shape[0]

    @pl.when(pl.program_id(0) == 0)
    def _():
        run_ref[...] = jnp.zeros(run_ref.shape, f32)

    x = x_ref[...]
    xa_ref[...], xb_ref[...] = _pack_words(x)
    logits = _sdot3(x, rt_ref[...])
    lane = lax.broadcasted_iota(jnp.int32, logits.shape, 1)
    lg = jnp.where(lane < N_EXPERTS, logits, NEG)
    v1 = jnp.max(lg, axis=1, keepdims=True)
    i1 = jnp.min(jnp.where(lg == v1, lane, LANES), axis=1, keepdims=True)
    lg2 = jnp.where(lane == i1, NEG, lg)
    v2 = jnp.max(lg2, axis=1, keepdims=True)
    i2 = jnp.min(jnp.where(lg2 == v2, lane, LANES), axis=1, keepdims=True)
    e2 = jnp.exp(v2 - v1)
    sel1, sel2 = lane == i1, lane == i2
    mask = (sel1 | sel2).astype(f32)
    r = lax.broadcasted_iota(jnp.int32, (tb, tb), 0)
    c = lax.broadcasted_iota(jnp.int32, (tb, tb), 1)
    rank = _bdot((c < r).astype(f32), mask) + run_ref[0:1, :]
    run_ref[...] = run_ref[...] + jnp.sum(mask, axis=0, keepdims=True)
    rank1 = jnp.sum(jnp.where(sel1, rank, 0.0), axis=1, keepdims=True)
    rank2 = jnp.sum(jnp.where(sel2, rank, 0.0), axis=1, keepdims=True)
    cols = (i1.astype(f32), i2.astype(f32), rank1, rank2, 1.0 / (1.0 + e2), e2 / (1.0 + e2))
    meta = jnp.zeros(logits.shape, f32)
    for n, col in enumerate(cols):
        meta = jnp.where(lane == n, col, meta)
    meta_ref[...] = meta
    cnt_ref[...] = run_ref[...]


def _router(xt, router, tb=1024):
    T = xt.shape[0]
    tb = min(tb, T)
    return pl.pallas_call(
        _router_body, grid=(T // tb,),
        in_specs=[pl.BlockSpec((tb, D_MODEL), lambda i: (i, 0)), pl.BlockSpec((D_MODEL, LANES), lambda i: (0, 0))],
        out_specs=[pl.BlockSpec((tb, SC_WORDS), lambda i: (i, 0)), pl.BlockSpec((tb, SC_WORDS), lambda i: (i, 0)),
                   pl.BlockSpec((tb, LANES), lambda i: (i, 0)), pl.BlockSpec((8, LANES), lambda i: (0, 0))],
        out_shape=[jax.ShapeDtypeStruct((T, SC_WORDS), jnp.int32), jax.ShapeDtypeStruct((T, SC_WORDS), jnp.int32),
                   jax.ShapeDtypeStruct((T, LANES), f32), jax.ShapeDtypeStruct((8, LANES), f32)],
        scratch_shapes=[pltpu.VMEM((8, LANES), f32)],
        compiler_params=_cparams(("arbitrary",)), name="moe_router")(xt, router)


def _sc_gather(table, idx):
    n = idx.shape[0]
    mesh = plsc.VectorSubcoreMesh(core_axis_name="c", subcore_axis_name="s")

    @functools.partial(pl.kernel, out_type=jax.ShapeDtypeStruct((n, SC_WORDS), table.dtype), mesh=mesh)
    def gather(x_hbm, i_hbm, o_hbm):
        def body(i_vmem, o_vmem):
            pltpu.sync_copy(x_hbm.at[i_vmem.at[0]], o_vmem)

        pltpu.emit_pipeline(
            body, grid=(n // SC_WINDOW,),
            in_specs=[pl.BlockSpec((1, SC_WINDOW), index_map=lambda i: (0, i))],
            out_specs=[pl.BlockSpec((SC_WINDOW, SC_WORDS), index_map=lambda i: (i, 0))],
            core_axis_name=("c", "s"), dimension_semantics=(pltpu.PARALLEL,))(i_hbm, o_hbm)

    return gather(table, idx.reshape(1, n))


def _sc_scatter(rows, idx, n_out):
    R = rows.shape[0]
    n = idx.shape[0]
    nblk = R // SC_WINDOW
    mesh = plsc.VectorSubcoreMesh(core_axis_name="c", subcore_axis_name="s")

    @functools.partial(pl.kernel, out_type=jax.ShapeDtypeStruct((n_out, SC_WORDS), rows.dtype), mesh=mesh,
                       scratch_types=[])
    def scatter(x_hbm, i_hbm, o_hbm):
        def body(x_vmem, i_vmem):
            pltpu.sync_copy(x_vmem, o_hbm.at[i_vmem.at[0]])

        pltpu.emit_pipeline(
            body, grid=(n // SC_WINDOW,),
            in_specs=[pl.BlockSpec((SC_WINDOW, SC_WORDS), index_map=lambda i: (i % nblk, 0)),
                      pl.BlockSpec((1, SC_WINDOW), index_map=lambda i: (0, i))],
            out_specs=[], core_axis_name=("c", "s"), dimension_semantics=(pltpu.PARALLEL,))(x_hbm, i_hbm)

    return scatter(rows, idx.reshape(1, n))


def _experts_body(te_ref, rows_ref, xa_ref, xb_ref, w1_ref, w3_ref, w2_ref, oa_ref, ob_ref, acc_ref, x_ref):
    i = pl.program_id(0)
    j = pl.program_id(1)

    @pl.when(j == 0)
    def _():
        valid = lax.broadcasted_iota(jnp.int32, (MOE_TILE, 1), 0) < rows_ref[i]
        x_ref[...] = jnp.where(valid, _unpack_words(xa_ref[...], xb_ref[...]), 0.0).astype(bf16)
        acc_ref[...] = jnp.zeros(acc_ref.shape, f32)

    @pl.when(rows_ref[i] > 0)
    def _():
        x = x_ref[...]
        h1 = jnp.dot(x, w1_ref[0].astype(bf16), preferred_element_type=f32)
        h3 = jnp.dot(x, w3_ref[0].astype(bf16), preferred_element_type=f32)
        acc_ref[...] += _bdot(h1 * _sigmoid(h1) * h3, w2_ref[0])

    @pl.when(j == pl.num_programs(1) - 1)
    def _():
        oa_ref[...], ob_ref[...] = _pack_words(acc_ref[...])


def _experts(xa, xb, tile_expert, tile_rows, w1, w3, w2, tf=512):
    P = xa.shape[0]
    dff = w1.shape[2]
    words = pl.BlockSpec((MOE_TILE, SC_WORDS), lambda i, j, te, nt: (i, 0))
    grid_spec = pltpu.PrefetchScalarGridSpec(
        num_scalar_prefetch=2, grid=(P // MOE_TILE, dff // tf),
        in_specs=[words, words,
                  pl.BlockSpec((1, D_MODEL, tf), lambda i, j, te, nt: (te[i], 0, j)),
                  pl.BlockSpec((1, D_MODEL, tf), lambda i, j, te, nt: (te[i], 0, j)),
                  pl.BlockSpec((1, tf, D_MODEL), lambda i, j, te, nt: (te[i], j, 0))],
        out_specs=[words, words],
        scratch_shapes=[pltpu.VMEM((MOE_TILE, D_MODEL), f32), pltpu.VMEM((MOE_TILE, D_MODEL), bf16)])
    return pl.pallas_call(
        _experts_body, grid_spec=grid_spec, out_shape=[jax.ShapeDtypeStruct((P, SC_WORDS), jnp.int32)] * 2,
        compiler_params=_cparams(("parallel", "arbitrary")), name="moe_experts")(tile_expert, tile_rows, xa, xb, w1, w3, w2)


def _combine_body(x_ref, y0a_ref, y0b_ref, y1a_ref, y1b_ref, meta_ref, ln_ref, o_ref):
    meta = meta_ref[...]
    ff = (meta[:, 4:5] * _unpack_words(y0a_ref[...], y0b_ref[...])
          + meta[:, 5:6] * _unpack_words(y1a_ref[...], y1b_ref[...]))
    o_ref[...] = _layer_norm(ALPHA * x_ref[...] + ff, ln_ref[0:1, :], ln_ref[1:2, :])


def _combine(xt, yga, ygb, meta, ln, tm=1024):
    T = xt.shape[0]
    tm = min(tm, T)
    nb = T // tm
    row = pl.BlockSpec((tm, D_MODEL), lambda i: (i, 0))
    first = pl.BlockSpec((tm, SC_WORDS), lambda i: (i, 0))
    second = pl.BlockSpec((tm, SC_WORDS), lambda i: (nb + i, 0))
    return pl.pallas_call(
        _combine_body, grid=(nb,),
        in_specs=[row, first, first, second, second, pl.BlockSpec((tm, LANES), lambda i: (i, 0)),
                  pl.BlockSpec((2, D_MODEL), lambda i: (0, 0))],
        out_specs=row, out_shape=jax.ShapeDtypeStruct((T, D_MODEL), f32),
        compiler_params=_cparams(("parallel",)), name="moe_combine")(xt, yga, ygb, yga, ygb, meta, ln)


def _moe(xt, router, w1, w3, w2, ln):
    T = xt.shape[0]
    xa, xb, meta, cnt = _router(xt, router)
    counts = cnt[0, :N_EXPERTS].astype(jnp.int32)
    tiles = (counts + MOE_TILE - 1) // MOE_TILE
    tile_end = jnp.cumsum(tiles)
    offset = (tile_end - tiles) * MOE_TILE
    expert = meta[:, 0:2].astype(jnp.int32)
    onehot = expert[:, :, None] == jnp.arange(N_EXPERTS, dtype=jnp.int32)[None, None, :]
    pos = jnp.sum(jnp.where(onehot, offset[None, None, :], 0), axis=2) + meta[:, 2:4].astype(jnp.int32)
    pos = pos.T.reshape(-1)
    P = 2 * T + N_EXPERTS * MOE_TILE
    tile_id = jnp.arange(P // MOE_TILE, dtype=jnp.int32)
    tile_expert = jnp.minimum(jnp.sum((tile_id[:, None] >= tile_end[None, :]).astype(jnp.int32), axis=1), N_EXPERTS - 1)
    first_tile = (tile_end - tiles)[tile_expert]
    tile_rows = jnp.clip(counts[tile_expert] - (tile_id - first_tile) * MOE_TILE, 0, MOE_TILE)
    ya, yb = _experts(_sc_scatter(xa, pos, P), _sc_scatter(xb, pos, P), tile_expert, tile_rows, w1, w3, w2)
    return _combine(xt, _sc_gather(ya, pos), _sc_gather(yb, pos), meta, ln)


def kernel(x, w_in, b_in, att_gq, att_gk, rw_mix, rw_w0, rw_w2, rw_a0, rw_a2, rw_g2, rw_kk, rw_ka, rw_rk, rw_ln_g, rw_ln_b, s5_lam_re, s5_lam_im, s5_log_dt, s5_b_re, s5_b_im, s5_c_re, s5_c_im, s5_d, s5_glu_w, s5_glu_b, ml_conv_w, ml_conv_b, ml_ib, ml_fb, w_gate, b_gate, w_branch, w_out, ln1_g, ln1_b, ffn_w1, ffn_w3, ffn_w2, moe_router, moe_w1, moe_w3, moe_w2, ln2_g, ln2_b):
    B, S, D = x.shape
    assert 2 * B == 8 and D == D_MODEL and S % 512 == 0 and w_in.shape[0] == DEPTH, (x.shape, w_in.shape)
    xt = x.reshape(B * S, D)
    cos, sin = _rope_tables(S)
    for l in range(DEPTH):
        gain = jnp.concatenate([jnp.tile(att_gq[l], 8) * (HEAD_DIM ** -0.5), jnp.tile(att_gk[l], 2)])[None, :]
        (q, k, v, s5u, mq, mk, mv, mo, r, rk, rv, an, bn, lw, gate, bonus, gt) = _proj(
            xt, _proj_params(w_in[l], b_in[l], ml_ib[l], ml_fb[l]), (cos, sin, gain),
            (ml_conv_w[l], ml_conv_b[l][None, :]),
            _rw_params(rw_mix[l], rw_w0[l], rw_w2[l], rw_a0[l], rw_a2[l], rw_g2[l], rw_kk[l], rw_ka[l], rw_rk[l]), B, S)
        score_bound = 8.1 * jnp.max(jnp.abs(att_gq[l])) * jnp.max(jnp.abs(att_gk[l]))
        o_att = _flash(q, k, v, score_bound, B, S)
        yf, yb = _rwkv(r, rk, rv, an, bn, lw, B, S)
        y_s5 = _s5(s5u, _s5_params(s5_lam_re[l], s5_lam_im[l], s5_log_dt[l], s5_b_re[l], s5_b_im[l], s5_c_re[l],
                                   s5_c_im[l]), B, S)
        hf, hb = _mlstm(mq, mk, mv, gt, B, S)
        xt = _merge(xt, o_att, (yf, yb, gate, bonus), (y_s5, s5u), (hf, hb, mo), w_gate[l].astype(bf16), b_gate[l],
                    _att_branch_weight(w_branch[l, 0]).astype(bf16), w_branch[l, 1:].astype(bf16),
                    w_out[l].astype(bf16), jnp.stack([ln1_g[l], ln1_b[l]]), jnp.stack([rw_ln_g[l], rw_ln_b[l]]),
                    jnp.stack([s5_d[l], s5_glu_b[l]]), s5_glu_w[l].astype(bf16))
        ln2 = jnp.stack([ln2_g[l], ln2_b[l]])
        if l % 2 == 0:
            xt = _ffn(xt, ffn_w1[l // 2].astype(bf16), ffn_w3[l // 2].astype(bf16), ffn_w2[l // 2].astype(bf16), ln2)
        else:
            router = jnp.pad(moe_router[l // 2], ((0, 0), (0, LANES - N_EXPERTS)))
            xt = _moe(xt, router, moe_w1[l // 2], moe_w3[l // 2], moe_w2[l // 2], ln2)
    return xt.reshape(B, S, D)
```

```python
import functools
import math

import jax
import jax.numpy as jnp
import numpy as np
from jax import lax
from jax.experimental import pallas as pl
from jax.experimental.pallas import tpu as pltpu
from jax.experimental.pallas import tpu_sc as plsc

f32 = jnp.float32
bf16 = jnp.bfloat16

D_MODEL = 1024
DEPTH = 2
GRID_W = 64
BRANCH_W = 256
HEAD_DIM = 64
ATT_HEADS = 4
ATT_KV_HEADS = 2
ROPE_THETA = 10000.0
QK_EPS = 1e-6
RW_GN_EPS = 64e-5
RW_COLS = 1088
S5_GROUP = 16
S5_GROUPS = 16
S5_STATE = 64
N_EXPERTS = 8
ALPHA = (2 * DEPTH) ** 0.25
LN_EPS = 1e-5

LANES = 128
CHUNK = 64
NEG = -1e30
VMEM_LIMIT = 56 * 1024 * 1024

PROJ_SPLITS = (768, 256, 512, 256, 256, RW_COLS)


def _cparams(sem):
    return pltpu.CompilerParams(dimension_semantics=sem, vmem_limit_bytes=VMEM_LIMIT)


def _sigmoid(x):
    return 0.5 * jnp.tanh(0.5 * x) + 0.5


def _dims(a, lhs_c, rhs_c):
    lead = a.ndim - 2
    batch = tuple(range(lead))
    return (((lhs_c + lead,), (rhs_c + lead,)), (batch, batch))


def _bdot(a, b):
    return lax.dot_general(a.astype(bf16), b.astype(bf16), _dims(a, 1, 0), preferred_element_type=f32)


def _bdot_nt(a, b):
    return lax.dot_general(a.astype(bf16), b.astype(bf16), _dims(a, 1, 1), preferred_element_type=f32)


def _bdot_tn(a, b):
    return lax.dot_general(a.astype(bf16), b.astype(bf16), _dims(a, 0, 0), preferred_element_type=f32)


def _split(x):
    hi = x.astype(bf16)
    return hi, (x - hi.astype(f32)).astype(bf16)


def _sdot(a, b, exact):
    dims = _dims(a, 1, 0)
    if exact == "rhs":
        hi, lo = _split(a)
        bb = b.astype(bf16)
        return (lax.dot_general(hi, bb, dims, preferred_element_type=f32)
                + lax.dot_general(lo, bb, dims, preferred_element_type=f32))
    hi, lo = _split(b)
    ab = a.astype(bf16)
    return (lax.dot_general(ab, hi, dims, preferred_element_type=f32)
            + lax.dot_general(ab, lo, dims, preferred_element_type=f32))


def _sdot3(a, b):
    dims = _dims(a, 1, 0)
    ah, al = _split(a)
    bh, bl = _split(b)
    return (lax.dot_general(ah, bh, dims, preferred_element_type=f32)
            + lax.dot_general(ah, bl, dims, preferred_element_type=f32)
            + lax.dot_general(al, bh, dims, preferred_element_type=f32))


def _seg_matrix(n, seg=HEAD_DIM):
    r = lax.broadcasted_iota(jnp.int32, (n, n), 0) // seg
    c = lax.broadcasted_iota(jnp.int32, (n, n), 1) // seg
    return (r == c).astype(f32)


def _layer_norm(y, g, b):
    mu = jnp.mean(y, axis=-1, keepdims=True)
    d = y - mu
    var = jnp.mean(d * d, axis=-1, keepdims=True)
    return d * lax.rsqrt(var + LN_EPS) * g + b


def _row_to_col(row, eye):
    n = eye.shape[-1]
    return jnp.sum(jnp.where(eye, jnp.broadcast_to(row, (row.shape[0], n, n)), 0.0), axis=2, keepdims=True)


def _stack_heads(x):
    h0 = lax.broadcasted_iota(jnp.int32, x.shape, 2) < HEAD_DIM
    return jnp.concatenate([jnp.where(h0, x, 0.0), jnp.where(h0, 0.0, x)], axis=1)


def _proj_body(x_ref, xp_ref, xn_ref, cos_ref, sin_ref, w_ref, b_ref, wg_ref, bg_ref, gain_ref, cw_ref, cb_ref,
               mix_ref, w2_ref, a2_ref, g2_ref, vec_ref,
               q_ref, k_ref, v_ref, s5_ref, mq_ref, mk_ref, mv_ref, mo_ref, r_ref, rk_ref, rv_ref, an_ref, bn_ref,
               lw_ref, gate_ref, bonus_ref, g_ref, mbuf_ref, rbuf_ref):
    tm = x_ref.shape[0]
    xb = x_ref[...].astype(bf16)
    xh = jnp.concatenate([xp_ref[...], x_ref[...], xn_ref[...]], axis=0).astype(bf16)
    offs = np.cumsum((0,) + PROJ_SPLITS)

    def cols(rows, n):
        sl = slice(int(offs[n]), int(offs[n + 1]))
        return jnp.dot(rows, w_ref[:, sl], preferred_element_type=f32) + b_ref[:, sl]

    q_ref[...], k_ref[...], v_ref[...] = _att_prep(cols(xb, 0), cos_ref[...], sin_ref[...], gain_ref[...])
    s5_ref[...] = cols(xb, 1)
    _fill_halo(mbuf_ref, cols(xh, 2))
    mq_ref[...], mk_ref[...] = _ml_prep(mbuf_ref, tm, cw_ref[...], cb_ref[...])
    mv_ref[...] = cols(xb, 3)
    mo_ref[...] = cols(xb, 4)
    _fill_halo(rbuf_ref, cols(xh, 5))
    outs = _rw_prep(rbuf_ref, tm, mix_ref[...], w2_ref[...], a2_ref[...], g2_ref[...], vec_ref[...])
    for o_ref, val in zip((r_ref, rk_ref, rv_ref, an_ref, bn_ref), outs[0:5]):
        o_ref[...] = val
    lw_ref[0], lw_ref[1], gate_ref[...], bonus_ref[...] = outs[5:9]
    g_ref[...] = lax.dot_general(wg_ref[...], xb, (((1,), (1,)), ((), ())), preferred_element_type=f32) + bg_ref[...]


def _proj(xt, proj_prm, att_prm, ml_prm, rw_prm, B, S, tm=512):
    T = B * S
    nb = S // tm
    n_tot = sum(PROJ_SPLITS)
    row = lambda n: pl.BlockSpec((tm, n), lambda b, i: (b * nb + i, 0))
    const = lambda a: pl.BlockSpec(a.shape, lambda b, i: (0,) * a.ndim)
    tab = pl.BlockSpec((tm, LANES), lambda b, i: (i, 0))
    f32out = lambda n: jax.ShapeDtypeStruct((T, n), f32)
    consts = list(proj_prm) + [att_prm[2]] + list(ml_prm) + list(rw_prm)
    out_specs = ([row(512), row(LANES), row(2 * LANES)] + [row(BRANCH_W)] * 10
                 + [pl.BlockSpec((2, tm, BRANCH_W), lambda b, i: (0, b * nb + i, 0)), row(BRANCH_W), row(BRANCH_W),
                    pl.BlockSpec((16, tm), lambda b, i: (0, b * nb + i))])
    out_shape = ([jax.ShapeDtypeStruct((T, 512), bf16), jax.ShapeDtypeStruct((T, LANES), bf16),
                  jax.ShapeDtypeStruct((T, 2 * LANES), bf16)] + [f32out(BRANCH_W)] * 10
                 + [jax.ShapeDtypeStruct((2, T, BRANCH_W), f32), f32out(BRANCH_W), f32out(BRANCH_W),
                    jax.ShapeDtypeStruct((16, T), f32)])
    return pl.pallas_call(
        _proj_body, grid=(B, nb),
        in_specs=_halo_specs(D_MODEL, tm, B, S) + [tab, tab] + [const(a) for a in consts],
        out_specs=out_specs, out_shape=out_shape,
        scratch_shapes=[pltpu.VMEM((tm + 16, 512), f32), pltpu.VMEM((tm + 16, RW_COLS), f32)],
        compiler_params=_cparams(("parallel", "parallel")), name="proj")(
            xt, xt, xt, att_prm[0], att_prm[1], *consts)


def _proj_params(w_in, b_in, ml_ib, ml_fb):
    o = np.cumsum((0, 256, 128, 128, RW_COLS, 256, 512, 256, 8, 8, 256))
    sl = lambda i: (w_in[:, o[i]:o[i + 1]], b_in[o[i]:o[i + 1]])
    (wq, bq), (wk, bk), (wv, bv), (wrw, brw), (ws5, bs5), (wqk, bqk), (wmv, bmv), (wi, bi), (wf, bf), (wo, bo) = (
        sl(i) for i in range(10))
    zw, zb = jnp.zeros((D_MODEL, HEAD_DIM), f32), jnp.zeros((HEAD_DIM,), f32)
    wq_e, bq_e = [], []
    for h in range(ATT_HEADS):
        wh, bh = wq[:, 64 * h:64 * h + 64], bq[64 * h:64 * h + 64]
        wq_e += [wh, zw] if h // 2 == 0 else [zw, wh]
        bq_e += [bh, zb] if h // 2 == 0 else [zb, bh]
    w = jnp.concatenate(wq_e + [wk, wv, ws5, wqk, wmv, wo, wrw], axis=1)
    b = jnp.concatenate(bq_e + [bk, bv, bs5, bqk, bmv, bo, brw])
    wg = jnp.concatenate([wi, wf], axis=1).T
    bg = jnp.concatenate([bi + ml_ib.reshape(-1), bf + ml_fb.reshape(-1)])
    return w.astype(bf16), b[None, :], wg.astype(bf16), bg[:, None]


def _att_prep(att, cos, sin, gain):
    x = att[:, 0:640]
    ms = _sdot(x * x, _seg_matrix(640), "rhs") * (1.0 / HEAD_DIM)
    xn = x * lax.rsqrt(ms + QK_EPS) * gain
    lane = lax.broadcasted_iota(jnp.int32, xn.shape, 1)
    partner = jnp.where((lane % 32) < 16, pltpu.roll(xn, 640 - 16, 1), pltpu.roll(xn, 16, 1))
    rot = xn * jnp.concatenate([cos] * 5, axis=1) + partner * jnp.concatenate([sin] * 5, axis=1)
    v = jnp.concatenate([att[:, 640:768].astype(bf16), jnp.ones((x.shape[0], LANES), bf16)], axis=1)
    return rot[:, 0:512].astype(bf16), rot[:, 512:640].astype(bf16), v


def _rope_tables(S):
    t = np.arange(S)
    row = (t // GRID_W).astype(np.float32)
    col = (t % GRID_W).astype(np.float32)
    n = 16
    inv = np.power(np.float32(ROPE_THETA), -np.arange(n, dtype=np.float32) / n).astype(np.float32)
    ar = jnp.asarray(row)[:, None] * jnp.asarray(inv)
    ac = jnp.asarray(col)[:, None] * jnp.asarray(inv)
    cos = jnp.concatenate([jnp.cos(ar), jnp.cos(ar), jnp.cos(ac), jnp.cos(ac)], axis=1)
    sin = jnp.concatenate([-jnp.sin(ar), jnp.sin(ar), -jnp.sin(ac), jnp.sin(ac)], axis=1)
    return jnp.concatenate([cos, cos], axis=1), jnp.concatenate([sin, sin], axis=1)


def _flash_body(q_ref, k_ref, v_ref, o_ref, acc_ref, *m_scratch, tk, track_max):
    tq = q_ref.shape[0]
    nk = k_ref.shape[0] // tk
    q2 = jnp.concatenate([q_ref[:, 0:LANES], q_ref[:, LANES:2 * LANES]], axis=0)
    acc_ref[...] = jnp.zeros(acc_ref.shape, f32)
    if track_max:
        m_ref, = m_scratch
        m_ref[...] = jnp.full(m_ref.shape, NEG, f32)

    def step(j, carry):
        rows = pl.ds(pl.multiple_of(j * tk, tk), tk)
        s = lax.dot_general(q2, k_ref[rows, :], (((1,), (1,)), ((), ())), preferred_element_type=f32)
        if track_max:
            m_old = m_ref[...]
            m_new = jnp.maximum(m_old, jnp.max(s, axis=1, keepdims=True))
            p = jnp.exp(s - m_new).astype(bf16)
            acc_ref[...] = jnp.exp(m_old - m_new) * acc_ref[...] + jnp.dot(p, v_ref[rows, :], preferred_element_type=f32)
            m_ref[...] = m_new
        else:
            acc_ref[...] += jnp.dot(jnp.exp(s).astype(bf16), v_ref[rows, :], preferred_element_type=f32)
        return carry

    lax.fori_loop(0, nk, step, 0)
    o = acc_ref[:, 0:LANES] / acc_ref[:, LANES:2 * LANES]
    o_ref[...] = jnp.concatenate([o[0:tq], o[tq:2 * tq]], axis=1)


SCORE_BOUND_MAX = 60.0


def _flash(q, k, v, score_bound, B, S, tq=256, tk=8192):
    T = B * S
    nb = S // tq
    tk = min(tk, S)

    def call(track_max):
        scratch = [pltpu.VMEM((2 * tq, 2 * LANES), f32)] + ([pltpu.VMEM((2 * tq, 1), f32)] if track_max else [])
        return pl.pallas_call(
            functools.partial(_flash_body, tk=tk, track_max=track_max), grid=(B, ATT_KV_HEADS, nb),
            in_specs=[pl.BlockSpec((tq, 2 * LANES), lambda b, g, i: (b * nb + i, g)),
                      pl.BlockSpec((S, LANES), lambda b, g, i: (b, 0)),
                      pl.BlockSpec((S, 2 * LANES), lambda b, g, i: (b, 0))],
            out_specs=pl.BlockSpec((tq, 2 * LANES), lambda b, g, i: (b * nb + i, g)),
            out_shape=jax.ShapeDtypeStruct((T, 512), f32), scratch_shapes=scratch,
            compiler_params=_cparams(("parallel", "parallel", "parallel")),
            name="flash_safe" if track_max else "flash")(q, k, v)

    return lax.cond(score_bound <= SCORE_BOUND_MAX, lambda: call(False), lambda: call(True))


def _halo_specs(width, tm, B, S):
    nb = S // tm
    r8 = tm // 8
    last8 = B * S // 8 - 1

    def main(b, i):
        return (b * nb + i, 0)

    def prev(b, i):
        return (jnp.maximum(b * (S // 8) + i * r8 - 1, 0), 0)

    def nxt(b, i):
        return (jnp.minimum(b * (S // 8) + (i + 1) * r8, last8), 0)

    return [pl.BlockSpec((tm, width), main), pl.BlockSpec((8, width), prev), pl.BlockSpec((8, width), nxt)]


def _fill_halo(buf_ref, xh):
    tm = xh.shape[0] - 16
    i = pl.program_id(1)
    last = pl.num_programs(1) - 1
    buf_ref[...] = xh
    buf_ref[pl.ds(0, 8), :] = jnp.where(i > 0, xh[0:8], 0.0)
    buf_ref[pl.ds(8 + tm, 8), :] = jnp.where(i < last, xh[8 + tm:16 + tm], 0.0)


def _rw_prep(buf_ref, tm, mix, w2, a2, g2, vec):
    x = buf_ref[pl.ds(8, tm), :]
    p = ((1.0 - mix[0:1, :] - mix[1:2, :]) * x + mix[0:1, :] * buf_ref[pl.ds(7, tm), :]
         + mix[1:2, :] * buf_ref[pl.ds(9, tm), :])
    r, k, v = p[:, 0:256], p[:, 256:512], p[:, 512:768]
    w0f, w0b, a0, k_k, k_a, r_k = (vec[j:j + 1, :] for j in range(6))
    dec = _bdot(jnp.tanh(p[:, 768:896]), w2)
    z = p[:, 896:1088]
    a = _sigmoid(a0 + _bdot(z, a2))
    gate = _bdot(_sigmoid(z), g2)
    seg = _seg_matrix(BRANCH_W)
    kk = k * k_k
    kk = kk * jnp.minimum(lax.rsqrt(_sdot(kk * kk, seg, "rhs")), 1e12)
    k2 = k * (1.0 + (a - 1.0) * k_a)
    bonus = _sdot(r * k2 * r_k, seg, "rhs") * v
    lwf = -math.exp(-0.5) * _sigmoid(w0f + dec[:, 0:256])
    lwb = -math.exp(-0.5) * _sigmoid(w0b + dec[:, 256:512])
    return r, k2, v, -kk, kk * a, lwf, lwb, gate, bonus


def _by_direction(x, fwd, bwd, fn):
    h = x.shape[0] // 2
    return jnp.concatenate([fn(x[0:h], fwd), fn(x[h:2 * h], bwd)], axis=0)


def _keep(x, masks):
    return _by_direction(x, masks[0], masks[1], lambda t, m: jnp.where(m, t, 0.0))


def _both(masks, nchain):
    h = nchain // 2
    return jnp.concatenate([jnp.broadcast_to(m.astype(f32), (h,) + m.shape[1:]) for m in masks], axis=0)


def _pair_masks():
    n = 2 * CHUNK
    r = lax.broadcasted_iota(jnp.int32, (1, n, n), 1)
    c = lax.broadcasted_iota(jnp.int32, (1, n, n), 2)
    same = (r // CHUNK) == (c // CHUNK)
    return r, c, same, (same & (c < r), same & (c > r)), (same & (c <= r), same & (c >= r))


def _rw_chunk(st, r, k, v, an, bn, lw):
    L = CHUNK
    N = r.shape[0]
    ri = lax.broadcasted_iota(jnp.int32, (1, L, L), 1)
    ci = lax.broadcasted_iota(jnp.int32, (1, L, L), 2)
    cs = _sdot(_both((ci <= ri, ci >= ri), N), lw, "lhs")
    tot = jnp.sum(lw, axis=1, keepdims=True)
    e_neg = jnp.exp(-cs)
    at = an * jnp.exp(cs - lw)
    rt = r * jnp.exp(cs)
    a2, b2, k2, v2 = _stack_heads(at), _stack_heads(bn * e_neg), _stack_heads(k * e_neg), _stack_heads(v)
    t = lax.broadcasted_iota(jnp.int32, (1, L, LANES), 1)
    tc = lax.broadcasted_iota(jnp.int32, (1, L, LANES), 2) % L
    strict, incl = (tc < t, tc > t), (tc <= t, tc >= t)
    g = _bdot_nt(jnp.concatenate([at, rt], axis=1), jnp.concatenate([b2, k2], axis=1))
    mab = _keep(g[:, 0:L, 0:LANES], strict)
    mak = _keep(g[:, 0:L, LANES:2 * LANES], strict)
    pb = _keep(g[:, L:2 * L, 0:LANES], incl)
    pk = _keep(g[:, L:2 * L, LANES:2 * LANES], incl)
    mul = lambda p, q: _bdot(p, _stack_heads(q))
    m8 = jnp.where((t // 8) == (tc // 8), mab, 0.0)
    x = (t == tc).astype(f32) + m8
    p = mul(m8, m8)
    x = x + mul(x, p)
    p = mul(p, p)
    x = x + mul(x, p)
    n = 8
    while n < L:
        e = jnp.where(((t // (2 * n)) == (tc // (2 * n))) & ((t // n) != (tc // n)), mab, 0.0)
        x = x + mul(mul(x, e), x)
        n *= 2
    wu = _bdot(x, jnp.concatenate([a2, _stack_heads(_bdot(mak, v2))], axis=2))
    wu2 = jnp.concatenate([_stack_heads(wu[:, :, 0:LANES]), _stack_heads(wu[:, :, LANES:2 * LANES])], axis=2)
    pwu = _bdot(pb, wu2)
    rh = rt + pwu[:, :, 0:LANES]
    y = pwu[:, :, LANES:2 * LANES] + _bdot(jnp.concatenate([pk, rh], axis=2), jnp.concatenate([v2, st], axis=1))
    r128 = lax.broadcasted_iota(jnp.int32, (1, LANES, LANES), 1)
    c128 = lax.broadcasted_iota(jnp.int32, (1, LANES, LANES), 2)
    gam = _row_to_col(jnp.exp(tot), r128 == c128)
    bwu = _bdot_tn(b2, wu2)
    st = gam * (st + _bdot(bwu[:, :, 0:LANES], st) + bwu[:, :, LANES:2 * LANES] + _bdot_tn(k2, v2))
    return y, st


def _load_pairs(ref, rows):
    return jnp.concatenate([ref[:, rows, 0:LANES], ref[:, rows, LANES:2 * LANES]], axis=0)


def _store_pairs(ref, rows, y):
    nb = ref.shape[0]
    ref[:, rows, 0:LANES] = y[0:nb]
    ref[:, rows, LANES:2 * LANES] = y[nb:2 * nb]


def _load_both(f_ref, b_ref, rows_f, rows_b):
    return jnp.concatenate([_load_pairs(f_ref, rows_f), _load_pairs(b_ref, rows_b)], axis=0)


def _chunk_rows(cc, nch):
    return (pl.ds(pl.multiple_of(cc * CHUNK, CHUNK), CHUNK),
            pl.ds(pl.multiple_of((nch - 1 - cc) * CHUNK, CHUNK), CHUNK))


def _rw_scan_body(rf_ref, rb_ref, kf_ref, kb_ref, vf_ref, vb_ref, anf_ref, anb_ref, bnf_ref, bnb_ref, lwf_ref, lwb_ref,
                  yf_ref, yb_ref, st_ref):
    nch = rf_ref.shape[1] // CHUNK
    half = st_ref.shape[0] // 2

    @pl.when(pl.program_id(0) == 0)
    def _():
        st_ref[...] = jnp.zeros(st_ref.shape, f32)

    def step(cc, carry):
        rows_f, rows_b = _chunk_rows(cc, nch)
        pairs = ((rf_ref, rb_ref), (kf_ref, kb_ref), (vf_ref, vb_ref), (anf_ref, anb_ref), (bnf_ref, bnb_ref),
                 (lwf_ref.at[0], lwb_ref.at[0]))
        y, st = _rw_chunk(st_ref[...], *(_load_both(f, b, rows_f, rows_b) for f, b in pairs))
        _store_pairs(yf_ref, rows_f, y[0:half])
        _store_pairs(yb_ref, rows_b, y[half:2 * half])
        st_ref[...] = st
        return carry

    lax.fori_loop(0, nch, step, 0)


def _rw_scan(r, k, v, an, bn, lw, B, S, ts=256):
    nb = S // ts
    fwd = pl.BlockSpec((B, ts, BRANCH_W), lambda i: (0, i, 0))
    bwd = pl.BlockSpec((B, ts, BRANCH_W), lambda i: (0, nb - 1 - i, 0))
    out = jax.ShapeDtypeStruct((B, S, BRANCH_W), f32)
    return pl.pallas_call(
        _rw_scan_body, grid=(nb,),
        in_specs=[fwd, bwd] * 5 + [pl.BlockSpec((1, B, ts, BRANCH_W), lambda i: (0, 0, i, 0)),
                                   pl.BlockSpec((1, B, ts, BRANCH_W), lambda i: (1, 0, nb - 1 - i, 0))],
        out_specs=[fwd, bwd], out_shape=[out, out],
        scratch_shapes=[pltpu.VMEM((2 * B * BRANCH_W // LANES, LANES, LANES), f32)],
        compiler_params=_cparams(("arbitrary",)), name="rw_scan")(r, r, k, k, v, v, an, an, bn, bn, lw, lw)


def _rw_finish(y, gate, bonus, gn):
    seg = _seg_matrix(BRANCH_W)
    mu = _sdot(y, seg, "rhs") * (1.0 / HEAD_DIM)
    d = y - mu
    var = _sdot(d * d, seg, "rhs") * (1.0 / HEAD_DIM)
    yn = d * lax.rsqrt(var + RW_GN_EPS) * gn[0:1, :] + gn[1:2, :]
    return (yn + bonus) * gate


def _rwkv(r, k, v, an, bn, lw, B, S):
    seq = [t.reshape(B, S, BRANCH_W) for t in (r, k, v, an, bn)] + [lw.reshape(2, B, S, BRANCH_W)]
    yf, yb = _rw_scan(*seq, B, S)
    return yf.reshape(B * S, BRANCH_W), yb.reshape(B * S, BRANCH_W)


def _rw_params(mix, w0, w2, a0, a2, g2, k_k, k_a, r_k):
    z = jnp.zeros((64, 256), f32)
    w2c = jnp.concatenate([jnp.concatenate([w2[0], z], axis=1), jnp.concatenate([z, w2[1]], axis=1)], axis=0)
    a2p = jnp.concatenate([a2, jnp.zeros((128, 256), f32)], axis=0)
    g2p = jnp.concatenate([jnp.zeros((64, 256), f32), g2], axis=0)
    vec = jnp.stack([w0[0], w0[1], a0, k_k, k_a, r_k.reshape(-1), jnp.zeros_like(a0), jnp.zeros_like(a0)])
    return mix, w2c.astype(bf16), a2p.astype(bf16), g2p.astype(bf16), vec


S5_HALF = S5_GROUPS * S5_STATE // 2


def _s5_scan_body(uf_ref, ub_ref, bh_ref, cre_ref, cim_ref, lre_ref, lim_ref, yf_ref, yb_ref,
                  sre_ref, sim_ref, bre_ref, bim_ref):
    ts = uf_ref.shape[0]
    n = S5_HALF

    @pl.when(pl.program_id(0) == 0)
    def _():
        sre_ref[...] = jnp.zeros(sre_ref.shape, f32)
        sim_ref[...] = jnp.zeros(sim_ref.shape, f32)

    row_half = lax.broadcasted_iota(jnp.int32, uf_ref.shape, 1) // 4
    lane_half = lax.broadcasted_iota(jnp.int32, uf_ref.shape, 2) // (BRANCH_W // 2)
    for d, u_ref in enumerate((uf_ref, ub_ref)):
        lhs = jnp.where(row_half == lane_half, u_ref[...], 0.0).reshape(ts * 8, BRANCH_W)
        bu = _bdot(lhs, bh_ref[d]).reshape(ts, 8, 2 * n)
        bre_ref[d] = bu[:, :, 0:n]
        bim_ref[d] = bu[:, :, n:2 * n]
    lre = lre_ref[...]
    lim = lim_ref[...]

    def step(t, carry):
        out = []
        for d, td in enumerate((t, ts - 1 - t)):
            sre, sim = carry[2 * d], carry[2 * d + 1]
            nre = lre[d] * sre - lim[d] * sim + bre_ref[d, td]
            nim = lre[d] * sim + lim[d] * sre + bim_ref[d, td]
            bre_ref[d, td] = nre
            bim_ref[d, td] = nim
            out += [nre, nim]
        return tuple(out)

    s = lax.fori_loop(0, ts, step, (sre_ref[0], sim_ref[0], sre_ref[1], sim_ref[1]), unroll=4)
    sre_ref[0], sim_ref[0], sre_ref[1], sim_ref[1] = s
    low = lane_half == 0
    for d, y_ref in enumerate((yf_ref, yb_ref)):
        yv = (_bdot(bre_ref[d].reshape(ts * 8, n), cre_ref[d])
              - _bdot(bim_ref[d].reshape(ts * 8, n), cim_ref[d])).reshape(ts, 8, BRANCH_W)
        y_ref[...] = jnp.where(low, yv, pltpu.roll(yv, 4, 1))


def _s5_scan(u8, bh, cre, cim, lre, lim, ts=128):
    S = u8.shape[0]
    nb = S // ts
    n = S5_HALF
    full = lambda shape: pl.BlockSpec(shape, lambda i: (0,) * len(shape))
    fwd = pl.BlockSpec((ts, 8, BRANCH_W), lambda i: (i, 0, 0))
    bwd = pl.BlockSpec((ts, 8, BRANCH_W), lambda i: (nb - 1 - i, 0, 0))
    out = jax.ShapeDtypeStruct((S, 8, BRANCH_W), f32)
    return pl.pallas_call(
        _s5_scan_body, grid=(nb,),
        in_specs=[fwd, bwd, full((2, BRANCH_W, 2 * n)), full((2, n, BRANCH_W)), full((2, n, BRANCH_W)),
                  full((2, 8, n)), full((2, 8, n))],
        out_specs=[fwd, bwd], out_shape=[out, out],
        scratch_shapes=[pltpu.VMEM((2, 8, n), f32), pltpu.VMEM((2, 8, n), f32),
                        pltpu.VMEM((2, ts, 8, n), f32), pltpu.VMEM((2, ts, 8, n), f32)],
        compiler_params=_cparams(("arbitrary",)), name="s5_scan")(u8, u8, bh, cre, cim, lre, lim)


def _s5_finish(y, u, vec, w):
    y = y + u * vec[0:1, :]
    y = 0.5 * y * (1.0 + jnp.tanh(math.sqrt(2.0 / math.pi) * (y + 0.044715 * (y * y * y))))
    return y * _sigmoid(_bdot(y, w) + vec[1:2, :])


def _s5_params(lam_re, lam_im, log_dt, b_re, b_im, c_re, c_im):
    G, P, C = S5_GROUPS, S5_STATE, S5_GROUP
    H = G // 2
    eye = jnp.eye(H, dtype=f32)
    b_c = lax.complex(b_re, b_im)

    def b_part(x):
        return jnp.einsum('ab,hapc->hacbp', eye, x.reshape(2, H, P, C)).reshape(G * C, H * P)

    def c_part(x):
        return jnp.einsum('ab,hacp->aphbc', eye, x.reshape(2, H, C, P)).reshape(H * P, G * C)

    def rows(x):
        return jnp.broadcast_to(x.reshape(2, 1, H * P), (2, 4, H * P)).reshape(8, H * P)

    bh, cre, cim, lre, lim = [], [], [], [], []
    for d in range(2):
        lam = lax.complex(jnp.minimum(lam_re[d], -1e-4), lam_im[d])
        lam_bar = jnp.exp(lam * jnp.exp(log_dt[d])[:, None])
        b_bar = ((lam_bar - 1.0) / lam)[..., None] * b_c
        bh.append(jnp.concatenate([b_part(jnp.real(b_bar)), b_part(jnp.imag(b_bar))], axis=1))
        cre.append(c_part(c_re[d]))
        cim.append(c_part(c_im[d]))
        lre.append(rows(jnp.real(lam_bar)))
        lim.append(rows(jnp.imag(lam_bar)))
    return (jnp.stack(bh).astype(bf16), jnp.stack(cre).astype(bf16), jnp.stack(cim).astype(bf16),
            jnp.stack(lre), jnp.stack(lim))


def _s5(u, prm, B, S):
    u3 = u.reshape(B, S, BRANCH_W).transpose(1, 0, 2)
    yf, yb = _s5_scan(jnp.concatenate([u3, u3], axis=1), *prm)
    return (yf[:, 0:B] + yb[:, 0:B]).transpose(1, 0, 2).reshape(B * S, BRANCH_W)


def _ml_prep(buf_ref, tm, w, b):
    y = b
    for j in range(5):
        y = y + w[j:j + 1, :] * buf_ref[pl.ds(6 + j, tm), :]
    y = y * _sigmoid(y)
    return y[:, 0:BRANCH_W], y[:, BRANCH_W:2 * BRANCH_W] * (HEAD_DIM ** -0.5)


def _ml_chunk(state, q, k, v, li, lfp):
    cn, m_row = state
    L = CHUNK
    N = q.shape[0]
    rr, cc, same, _, incl2 = _pair_masks()
    same_f = _both((same, same), N)
    lane = lax.broadcasted_iota(jnp.int32, (1, 1, 2 * L), 2)
    lf = jnp.minimum(lfp, 0.0) - jnp.log(1.0 + jnp.exp(-jnp.abs(lfp)))
    lf8 = jnp.broadcast_to(lf, (N, 8, 2 * L))
    b_row = _sdot(lf8, _both((incl2[1], incl2[0]), N), "rhs")[:, 0:1]
    g_row = _sdot(lf8, same_f, "rhs")[:, 0:1]
    w_end = g_row - b_row + li
    m0 = jnp.max(jnp.where(lane < L, w_end, NEG), axis=2, keepdims=True)
    m1 = jnp.max(jnp.where(lane < L, NEG, w_end), axis=2, keepdims=True)
    m_loc = jnp.where(lane < L, m0, m1)
    t = lax.broadcasted_iota(jnp.int32, (1, L, LANES), 1)
    tc = lax.broadcasted_iota(jnp.int32, (1, L, LANES), 2) % L
    diag = t == tc

    def cols(row):
        return _sdot(jnp.where(diag, jnp.broadcast_to(row, (N, L, LANES)), 0.0), same_f, "rhs")

    e_col, b_col = cols(jnp.exp(w_end - m_loc)), cols(b_row)
    k2, v2 = _stack_heads(k), _stack_heads(v)
    v1 = jnp.concatenate([v2, _stack_heads(jnp.ones_like(v))], axis=2)
    log_inter = b_col + m_row
    log_intra = _by_direction(b_col - b_row + li, tc <= t, tc >= t, lambda x, m: jnp.where(m, x, NEG))
    head0 = lane < L
    r0 = jnp.max(jnp.where(head0, log_intra, NEG), axis=2, keepdims=True)
    r1 = jnp.max(jnp.where(head0, NEG, log_intra), axis=2, keepdims=True)
    m_r = jnp.maximum(log_inter, jnp.where(head0, r0, r1))
    s = _bdot_nt(q, k2) * jnp.exp(log_intra - m_r)
    inter = jnp.exp(log_inter - m_r)
    nd = _bdot(s, v1) + jnp.concatenate([inter, inter], axis=2) * _bdot(q, cn)
    h = nd[:, :, 0:LANES] / jnp.maximum(jnp.abs(nd[:, :, LANES:2 * LANES]), jnp.exp(-m_r))
    m_new = jnp.maximum(g_row + m_row, m_loc)
    a = jnp.exp(g_row + m_row - m_new)
    bb = jnp.exp(m_loc - m_new)
    cn = (jnp.concatenate([a, a], axis=2) * cn
          + jnp.concatenate([bb, bb], axis=2) * _bdot_tn(_stack_heads(e_col * k), v1))
    return h, (cn, m_new)


GATE_ROWS = 8


def _ml_scan_body(qf_ref, qb_ref, kf_ref, kb_ref, vf_ref, vb_ref, gf_ref, gb_ref, hf_ref, hb_ref, c_ref, m_ref):
    nch = qf_ref.shape[1] // CHUNK
    half = c_ref.shape[0] // 2
    i = pl.program_id(0)
    nb = pl.num_programs(0)
    per = GATE_ROWS // nch
    base_f = (i % per) * nch
    base_b = ((nb - 1 - i) % per) * nch

    @pl.when(i == 0)
    def _():
        c_ref[...] = jnp.zeros(c_ref.shape, f32)
        m_ref[...] = jnp.zeros(m_ref.shape, f32)

    def step(cc, carry):
        rows_f, rows_b = _chunk_rows(cc, nch)
        gate = lambda t: jnp.concatenate(
            [gf_ref[t, 0, 0, :, pl.ds(base_f + cc, 1), :], gf_ref[t, 0, 1, :, pl.ds(base_f + cc, 1), :],
             gb_ref[t, 0, 0, :, pl.ds(base_b + nch - 1 - cc, 1), :], gb_ref[t, 0, 1, :, pl.ds(base_b + nch - 1 - cc, 1), :]],
            axis=0)
        h, (cn, m_row) = _ml_chunk((c_ref[...], m_ref[...]), _load_both(qf_ref, qb_ref, rows_f, rows_b),
                                   _load_both(kf_ref, kb_ref, rows_f, rows_b), _load_both(vf_ref, vb_ref, rows_f, rows_b),
                                   gate(0), gate(1))
        _store_pairs(hf_ref, rows_f, h[0:half])
        _store_pairs(hb_ref, rows_b, h[half:2 * half])
        c_ref[...] = cn
        m_ref[...] = m_row
        return carry

    lax.fori_loop(0, nch, step, 0)


def _ml_scan(q, k, v, g, B, S, ts=512):
    nb = S // ts
    per = GATE_ROWS * CHUNK // ts
    nchain = 2 * B * BRANCH_W // LANES
    fwd = pl.BlockSpec((B, ts, BRANCH_W), lambda i: (0, i, 0))
    bwd = pl.BlockSpec((B, ts, BRANCH_W), lambda i: (0, nb - 1 - i, 0))
    out = jax.ShapeDtypeStruct((B, S, BRANCH_W), f32)
    return pl.pallas_call(
        _ml_scan_body, grid=(nb,),
        in_specs=[fwd, bwd] * 3 + [pl.BlockSpec((2, 1, 2, B, GATE_ROWS, LANES), lambda i: (0, 0, 0, 0, i // per, 0)),
                                   pl.BlockSpec((2, 1, 2, B, GATE_ROWS, LANES),
                                                lambda i: (0, 1, 0, 0, (nb - 1 - i) // per, 0))],
        out_specs=[fwd, bwd], out_shape=[out, out],
        scratch_shapes=[pltpu.VMEM((nchain, LANES, 2 * LANES), f32), pltpu.VMEM((nchain, 1, LANES), f32)],
        compiler_params=_cparams(("arbitrary",)), name="ml_scan")(q, q, k, k, v, v, g, g)


def _mlstm(q, k, mv, gt, B, S):
    g = gt.reshape(2, 2, 2, 2, B, S // CHUNK, CHUNK).transpose(0, 1, 2, 4, 5, 3, 6).reshape(2, 2, 2, B, S // CHUNK, LANES)
    hf, hb = _ml_scan(*(t.reshape(B, S, BRANCH_W) for t in (q, k, mv)), g, B, S)
    return hf.reshape(B * S, BRANCH_W), hb.reshape(B * S, BRANCH_W)


def _merge_body(x_ref, att_ref, yf_ref, yb_ref, rg_ref, rb_ref, sy_ref, su_ref, hf_ref, hb_ref, mo_ref,
                wg_ref, bg_ref, wba_ref, wb_ref, wo_ref, ln_ref, gn_ref, sv_ref, sw_ref, o_ref):
    x = x_ref[...]
    xb = x.astype(bf16)
    rw = _rw_finish(yf_ref[...] + yb_ref[...], rg_ref[...], rb_ref[...], gn_ref[...])
    s5 = _s5_finish(sy_ref[...], su_ref[...], sv_ref[...], sw_ref[...])
    ml = _sigmoid(mo_ref[...]) * (hf_ref[...] + hb_ref[...])
    branches = (att_ref[...], rw, s5, ml)
    merged = None
    for n in range(4):
        gate = _sigmoid(jnp.dot(xb, wg_ref[n], preferred_element_type=f32) + bg_ref[n:n + 1, :])
        wide = _bdot(branches[n], wba_ref[...] if n == 0 else wb_ref[n - 1])
        merged = gate * wide if merged is None else merged + gate * wide
    y = ALPHA * x + _bdot(merged, wo_ref[...])
    o_ref[...] = _layer_norm(y, ln_ref[0:1, :], ln_ref[1:2, :])


def _merge(xt, att, rw_parts, s5_parts, ml_parts, wg, bg, wba, wb, wo, ln, gn, s5_vec, s5_w, tm=256):
    T = xt.shape[0]
    row = lambda n: pl.BlockSpec((tm, n), lambda i: (i, 0))
    const = lambda shape: pl.BlockSpec(shape, lambda i: (0,) * len(shape), pipeline_mode=pl.Buffered(1))
    return pl.pallas_call(
        _merge_body, grid=(T // tm,),
        in_specs=[row(D_MODEL), row(512)] + [row(BRANCH_W)] * 9
        + [const((4, D_MODEL, D_MODEL)), const((4, D_MODEL)), const((512, D_MODEL)), const((3, BRANCH_W, D_MODEL)),
           const((D_MODEL, D_MODEL)), const((2, D_MODEL)), const((2, BRANCH_W)), const((2, BRANCH_W)),
           const((BRANCH_W, BRANCH_W))],
        out_specs=row(D_MODEL), out_shape=jax.ShapeDtypeStruct((T, D_MODEL), f32),
        compiler_params=_cparams(("parallel",)), name="merge")(
            xt, att, *rw_parts, *s5_parts, *ml_parts, wg, bg, wba, wb, wo, ln, gn, s5_vec, s5_w)


def _att_branch_weight(wb):
    z = jnp.zeros((HEAD_DIM, D_MODEL), f32)
    parts = []
    for h in range(ATT_HEADS):
        wh = wb[64 * h:64 * h + 64]
        parts += [wh, z] if h // 2 == 0 else [z, wh]
    return jnp.concatenate(parts, axis=0)


def _ffn_body(x_ref, w1_ref, w3_ref, w2_ref, ln_ref, o_ref):
    x = x_ref[...]
    xb = x.astype(bf16)
    h1 = jnp.dot(xb, w1_ref[...], preferred_element_type=f32)
    h3 = jnp.dot(xb, w3_ref[...], preferred_element_type=f32)
    ff = _bdot(h1 * _sigmoid(h1) * h3, w2_ref[...])
    o_ref[...] = _layer_norm(ALPHA * x + ff, ln_ref[0:1, :], ln_ref[1:2, :])


def _ffn(xt, w1, w3, w2, ln, tm=512):
    T = xt.shape[0]
    row = pl.BlockSpec((tm, D_MODEL), lambda i: (i, 0))
    const = lambda a: pl.BlockSpec(a.shape, lambda i: (0,) * a.ndim, pipeline_mode=pl.Buffered(1))
    return pl.pallas_call(
        _ffn_body, grid=(T // tm,), in_specs=[row, const(w1), const(w3), const(w2), const(ln)],
        out_specs=row, out_shape=jax.ShapeDtypeStruct((T, D_MODEL), f32),
        compiler_params=_cparams(("parallel",)), name="ffn")(xt, w1, w3, w2, ln)


MOE_TILE = 1024
SC_WINDOW = 128
SC_WORDS = 256


def _pack_words(x):
    bits = lax.bitcast_convert_type(x.astype(bf16).astype(f32), jnp.int32)
    half = D_MODEL // 2
    w = lax.shift_right_logical(bits[:, 0:half], 16) | bits[:, half:D_MODEL]
    return w[:, 0:SC_WORDS], w[:, SC_WORDS:2 * SC_WORDS]


def _unpack_words(wa, wb):
    w = jnp.concatenate([wa, wb], axis=1)
    lo = lax.bitcast_convert_type(lax.shift_left(w, 16), f32)
    hi = lax.bitcast_convert_type(w & jnp.int32(-65536), f32)
    return jnp.concatenate([lo, hi], axis=1)


def _router_body(x_ref, rt_ref, xa_ref, xb_ref, meta_ref, cnt_ref, run_ref):
    tb = x_ref.shape[0]

    @pl.when(pl.program_id(0) == 0)
    def _():
        run_ref[...] = jnp.zeros(run_ref.shape, f32)

    x = x_ref[...]
    xa_ref[...], xb_ref[...] = _pack_words(x)
    logits = _sdot3(x, rt_ref[...])
    lane = lax.broadcasted_iota(jnp.int32, logits.shape, 1)
    lg = jnp.where(lane < N_EXPERTS, logits, NEG)
    v1 = jnp.max(lg, axis=1, keepdims=True)
    i1 = jnp.min(jnp.where(lg == v1, lane, LANES), axis=1, keepdims=True)
    lg2 = jnp.where(lane == i1, NEG, lg)
    v2 = jnp.max(lg2, axis=1, keepdims=True)
    i2 = jnp.min(jnp.where(lg2 == v2, lane, LANES), axis=1, keepdims=True)
    e2 = jnp.exp(v2 - v1)
    sel1, sel2 = lane == i1, lane == i2
    mask = (sel1 | sel2).astype(f32)
    r = lax.broadcasted_iota(jnp.int32, (tb, tb), 0)
    c = lax.broadcasted_iota(jnp.int32, (tb, tb), 1)
    rank = _bdot((c < r).astype(f32), mask) + run_ref[0:1, :]
    run_ref[...] = run_ref[...] + jnp.sum(mask, axis=0, keepdims=True)
    rank1 = jnp.sum(jnp.where(sel1, rank, 0.0), axis=1, keepdims=True)
    rank2 = jnp.sum(jnp.where(sel2, rank, 0.0), axis=1, keepdims=True)
    cols = (i1.astype(f32), i2.astype(f32), rank1, rank2, 1.0 / (1.0 + e2), e2 / (1.0 + e2))
    meta = jnp.zeros(logits.shape, f32)
    for n, col in enumerate(cols):
        meta = jnp.where(lane == n, col, meta)
    meta_ref[...] = meta
    cnt_ref[...] = run_ref[...]


def _router(xt, router, tb=512):
    T = xt.shape[0]
    tb = min(tb, T)
    return pl.pallas_call(
        _router_body, grid=(T // tb,),
        in_specs=[pl.BlockSpec((tb, D_MODEL), lambda i: (i, 0)), pl.BlockSpec((D_MODEL, LANES), lambda i: (0, 0))],
        out_specs=[pl.BlockSpec((tb, SC_WORDS), lambda i: (i, 0)), pl.BlockSpec((tb, SC_WORDS), lambda i: (i, 0)),
                   pl.BlockSpec((tb, LANES), lambda i: (i, 0)), pl.BlockSpec((8, LANES), lambda i: (0, 0))],
        out_shape=[jax.ShapeDtypeStruct((T, SC_WORDS), jnp.int32), jax.ShapeDtypeStruct((T, SC_WORDS), jnp.int32),
                   jax.ShapeDtypeStruct((T, LANES), f32), jax.ShapeDtypeStruct((8, LANES), f32)],
        scratch_shapes=[pltpu.VMEM((8, LANES), f32)],
        compiler_params=_cparams(("arbitrary",)), name="moe_router")(xt, router)


def _sc_gather(table, idx):
    n = idx.shape[0]
    mesh = plsc.VectorSubcoreMesh(core_axis_name="c", subcore_axis_name="s")

    @functools.partial(pl.kernel, out_type=jax.ShapeDtypeStruct((n, SC_WORDS), table.dtype), mesh=mesh)
    def gather(x_hbm, i_hbm, o_hbm):
        def body(i_vmem, o_vmem):
            pltpu.sync_copy(x_hbm.at[i_vmem.at[0]], o_vmem)

        pltpu.emit_pipeline(
            body, grid=(n // SC_WINDOW,),
            in_specs=[pl.BlockSpec((1, SC_WINDOW), index_map=lambda i: (0, i))],
            out_specs=[pl.BlockSpec((SC_WINDOW, SC_WORDS), index_map=lambda i: (i, 0))],
            core_axis_name=("c", "s"), dimension_semantics=(pltpu.PARALLEL,))(i_hbm, o_hbm)

    return gather(table, idx.reshape(1, n))


def _sc_scatter(rows, idx, n_out):
    R = rows.shape[0]
    n = idx.shape[0]
    nblk = R // SC_WINDOW
    mesh = plsc.VectorSubcoreMesh(core_axis_name="c", subcore_axis_name="s")

    @functools.partial(pl.kernel, out_type=jax.ShapeDtypeStruct((n_out, SC_WORDS), rows.dtype), mesh=mesh,
                       scratch_types=[])
    def scatter(x_hbm, i_hbm, o_hbm):
        def body(x_vmem, i_vmem):
            pltpu.sync_copy(x_vmem, o_hbm.at[i_vmem.at[0]])

        pltpu.emit_pipeline(
            body, grid=(n // SC_WINDOW,),
            in_specs=[pl.BlockSpec((SC_WINDOW, SC_WORDS), index_map=lambda i: (i % nblk, 0)),
                      pl.BlockSpec((1, SC_WINDOW), index_map=lambda i: (0, i))],
            out_specs=[], core_axis_name=("c", "s"), dimension_semantics=(pltpu.PARALLEL,))(x_hbm, i_hbm)

    return scatter(rows, idx.reshape(1, n))


def _experts_body(te_ref, rows_ref, xa_ref, xb_ref, w1_ref, w3_ref, w2_ref, oa_ref, ob_ref, acc_ref, x_ref):
    i = pl.program_id(0)
    j = pl.program_id(1)

    @pl.when(j == 0)
    def _():
        valid = lax.broadcasted_iota(jnp.int32, (MOE_TILE, 1), 0) < rows_ref[i]
        x_ref[...] = jnp.where(valid, _unpack_words(xa_ref[...], xb_ref[...]), 0.0).astype(bf16)
        acc_ref[...] = jnp.zeros(acc_ref.shape, f32)

    @pl.when(rows_ref[i] > 0)
    def _():
        x = x_ref[...]
        h1 = jnp.dot(x, w1_ref[0].astype(bf16), preferred_element_type=f32)
        h3 = jnp.dot(x, w3_ref[0].astype(bf16), preferred_element_type=f32)
        acc_ref[...] += _bdot(h1 * _sigmoid(h1) * h3, w2_ref[0])

    @pl.when(j == pl.num_programs(1) - 1)
    def _():
        oa_ref[...], ob_ref[...] = _pack_words(acc_ref[...])


def _experts(xa, xb, tile_expert, tile_rows, w1, w3, w2, tf=512):
    P = xa.shape[0]
    dff = w1.shape[2]
    words = pl.BlockSpec((MOE_TILE, SC_WORDS), lambda i, j, te, nt: (i, 0))
    grid_spec = pltpu.PrefetchScalarGridSpec(
        num_scalar_prefetch=2, grid=(P // MOE_TILE, dff // tf),
        in_specs=[words, words,
                  pl.BlockSpec((1, D_MODEL, tf), lambda i, j, te, nt: (te[i], 0, j)),
                  pl.BlockSpec((1, D_MODEL, tf), lambda i, j, te, nt: (te[i], 0, j)),
                  pl.BlockSpec((1, tf, D_MODEL), lambda i, j, te, nt: (te[i], j, 0))],
        out_specs=[words, words],
        scratch_shapes=[pltpu.VMEM((MOE_TILE, D_MODEL), f32), pltpu.VMEM((MOE_TILE, D_MODEL), bf16)])
    return pl.pallas_call(
        _experts_body, grid_spec=grid_spec, out_shape=[jax.ShapeDtypeStruct((P, SC_WORDS), jnp.int32)] * 2,
        compiler_params=_cparams(("parallel", "arbitrary")), name="moe_experts")(tile_expert, tile_rows, xa, xb, w1, w3, w2)


def _combine_body(x_ref, y0a_ref, y0b_ref, y1a_ref, y1b_ref, meta_ref, ln_ref, o_ref):
    meta = meta_ref[...]
    ff = (meta[:, 4:5] * _unpack_words(y0a_ref[...], y0b_ref[...])
          + meta[:, 5:6] * _unpack_words(y1a_ref[...], y1b_ref[...]))
    o_ref[...] = _layer_norm(ALPHA * x_ref[...] + ff, ln_ref[0:1, :], ln_ref[1:2, :])


def _combine(xt, yga, ygb, meta, ln, tm=1024):
    T = xt.shape[0]
    tm = min(tm, T)
    nb = T // tm
    row = pl.BlockSpec((tm, D_MODEL), lambda i: (i, 0))
    first = pl.BlockSpec((tm, SC_WORDS), lambda i: (i, 0))
    second = pl.BlockSpec((tm, SC_WORDS), lambda i: (nb + i, 0))
    return pl.pallas_call(
        _combine_body, grid=(nb,),
        in_specs=[row, first, first, second, second, pl.BlockSpec((tm, LANES), lambda i: (i, 0)),
                  pl.BlockSpec((2, D_MODEL), lambda i: (0, 0))],
        out_specs=row, out_shape=jax.ShapeDtypeStruct((T, D_MODEL), f32),
        compiler_params=_cparams(("parallel",)), name="moe_combine")(xt, yga, ygb, yga, ygb, meta, ln)


def _moe(xt, router, w1, w3, w2, ln):
    T = xt.shape[0]
    xa, xb, meta, cnt = _router(xt, router)
    counts = cnt[0, :N_EXPERTS].astype(jnp.int32)
    tiles = (counts + MOE_TILE - 1) // MOE_TILE
    tile_end = jnp.cumsum(tiles)
    offset = (tile_end - tiles) * MOE_TILE
    expert = meta[:, 0:2].astype(jnp.int32)
    onehot = expert[:, :, None] == jnp.arange(N_EXPERTS, dtype=jnp.int32)[None, None, :]
    pos = jnp.sum(jnp.where(onehot, offset[None, None, :], 0), axis=2) + meta[:, 2:4].astype(jnp.int32)
    pos = pos.T.reshape(-1)
    P = 2 * T + N_EXPERTS * MOE_TILE
    tile_id = jnp.arange(P // MOE_TILE, dtype=jnp.int32)
    tile_expert = jnp.minimum(jnp.sum((tile_id[:, None] >= tile_end[None, :]).astype(jnp.int32), axis=1), N_EXPERTS - 1)
    first_tile = (tile_end - tiles)[tile_expert]
    tile_rows = jnp.clip(counts[tile_expert] - (tile_id - first_tile) * MOE_TILE, 0, MOE_TILE)
    ya, yb = _experts(_sc_scatter(xa, pos, P), _sc_scatter(xb, pos, P), tile_expert, tile_rows, w1, w3, w2)
    return _combine(xt, _sc_gather(ya, pos), _sc_gather(yb, pos), meta, ln)


def kernel(x, w_in, b_in, att_gq, att_gk, rw_mix, rw_w0, rw_w2, rw_a0, rw_a2, rw_g2, rw_kk, rw_ka, rw_rk, rw_ln_g, rw_ln_b, s5_lam_re, s5_lam_im, s5_log_dt, s5_b_re, s5_b_im, s5_c_re, s5_c_im, s5_d, s5_glu_w, s5_glu_b, ml_conv_w, ml_conv_b, ml_ib, ml_fb, w_gate, b_gate, w_branch, w_out, ln1_g, ln1_b, ffn_w1, ffn_w3, ffn_w2, moe_router, moe_w1, moe_w3, moe_w2, ln2_g, ln2_b):
    B, S, D = x.shape
    assert 2 * B == 8 and D == D_MODEL and S % 512 == 0 and w_in.shape[0] == DEPTH, (x.shape, w_in.shape)
    xt = x.reshape(B * S, D)
    cos, sin = _rope_tables(S)
    for l in range(DEPTH):
        gain = jnp.concatenate([jnp.tile(att_gq[l], 8) * (HEAD_DIM ** -0.5), jnp.tile(att_gk[l], 2)])[None, :]
        (q, k, v, s5u, mq, mk, mv, mo, r, rk, rv, an, bn, lw, gate, bonus, gt) = _proj(
            xt, _proj_params(w_in[l], b_in[l], ml_ib[l], ml_fb[l]), (cos, sin, gain),
            (ml_conv_w[l], ml_conv_b[l][None, :]),
            _rw_params(rw_mix[l], rw_w0[l], rw_w2[l], rw_a0[l], rw_a2[l], rw_g2[l], rw_kk[l], rw_ka[l], rw_rk[l]), B, S)
        score_bound = 8.1 * jnp.max(jnp.abs(att_gq[l])) * jnp.max(jnp.abs(att_gk[l]))
        o_att = _flash(q, k, v, score_bound, B, S)
        yf, yb = _rwkv(r, rk, rv, an, bn, lw, B, S)
        y_s5 = _s5(s5u, _s5_params(s5_lam_re[l], s5_lam_im[l], s5_log_dt[l], s5_b_re[l], s5_b_im[l], s5_c_re[l],
                                   s5_c_im[l]), B, S)
        hf, hb = _mlstm(mq, mk, mv, gt, B, S)
        xt = _merge(xt, o_att, (yf, yb, gate, bonus), (y_s5, s5u), (hf, hb, mo), w_gate[l].astype(bf16), b_gate[l],
                    _att_branch_weight(w_branch[l, 0]).astype(bf16), w_branch[l, 1:].astype(bf16),
                    w_out[l].astype(bf16), jnp.stack([ln1_g[l], ln1_b[l]]), jnp.stack([rw_ln_g[l], rw_ln_b[l]]),
                    jnp.stack([s5_d[l], s5_glu_b[l]]), s5_glu_w[l].astype(bf16))
        ln2 = jnp.stack([ln2_g[l], ln2_b[l]])
        if l % 2 == 0:
            xt = _ffn(xt, ffn_w1[l // 2].astype(bf16), ffn_w3[l // 2].astype(bf16), ffn_w2[l // 2].astype(bf16), ln2)
        else:
            router = jnp.pad(moe_router[l // 2], ((0, 0), (0, LANES - N_EXPERTS)))
            xt = _moe(xt, router, moe_w1[l // 2], moe_w3[l // 2], moe_w2[l // 2], ln2)
    return xt.reshape(B, S, D)
```
